```python
import jax, jax.numpy as jnp
from jax import lax
import numpy as np

D_MODEL = 1024
BATCH = 8
SEQ = 4096
DEPTH = 1

HEAD_DIM = 64
N_FOX_HEADS = 8
N_MOBA_HEADS = 8
FOX_WIDTH = N_FOX_HEADS * HEAD_DIM
MOBA_WIDTH = N_MOBA_HEADS * HEAD_DIM
Q_BLOCK = 128
MOBA_BLOCK = 256
MOBA_TOPK = 3
MOBA_Q_CHUNK = 32
ROPE_THETA = 10000.0
PLE_DIM = 256
N_GROUPS = 4
EXPERTS_PER_GROUP = 8
N_EXPERTS = N_GROUPS * EXPERTS_PER_GROUP
TOP_K_FINE = 2
D_EXPERT = 512
MOE_ROW_BLOCK = 128
RMS_EPS = 1e-6
IN_COLS = 3 * FOX_WIDTH + 3 * MOBA_WIDTH + N_FOX_HEADS + 2 * D_MODEL

kernel_name = 'fox_moba_hmoe_hybrid_block'


def rmsnorm(x, g):
    xf = x.astype(jnp.float32)
    y = xf * lax.rsqrt(jnp.mean(xf * xf, axis=-1, keepdims=True) + RMS_EPS)
    return (y * g.astype(jnp.float32)).astype(x.dtype)


def rope(x):
    s, d = x.shape[2], x.shape[3]
    half = d // 2
    inv = 1.0 / (ROPE_THETA ** (jnp.arange(0, d, 2, dtype=jnp.float32) / d))
    ang = jnp.arange(s, dtype=jnp.float32)[:, None] * inv[None, :]
    cos, sin = jnp.cos(ang), jnp.sin(ang)
    xf = x.astype(jnp.float32)
    x1, x2 = xf[..., :half], xf[..., half:]
    out = jnp.concatenate([x1 * cos - x2 * sin, x2 * cos + x1 * sin], axis=-1)
    return out.astype(x.dtype)


def forgetting_attention(q, k, v, log_f):
    b, h, s, d = q.shape
    c = jnp.cumsum(log_f, axis=-1)
    nqb = s // Q_BLOCK
    q_blocks = jnp.moveaxis(q.reshape(b, h, nqb, Q_BLOCK, d), 2, 0)
    c_blocks = jnp.moveaxis(c.reshape(b, h, nqb, Q_BLOCK), 2, 0)
    starts = jnp.arange(nqb, dtype=jnp.int32) * Q_BLOCK
    k_pos = jnp.arange(s, dtype=jnp.int32)
    scale = HEAD_DIM ** -0.5

    def block(args):
        qb, cb, s0 = args
        logits = jnp.einsum('bhqd,bhkd->bhqk', qb, k, preferred_element_type=jnp.float32) * scale
        logits = logits + cb[..., None] - c[:, :, None, :]
        q_pos = s0 + jnp.arange(Q_BLOCK, dtype=jnp.int32)
        causal = k_pos[None, :] <= q_pos[:, None]
        logits = jnp.where(causal, logits, -jnp.inf)
        probs = jax.nn.softmax(logits, axis=-1)
        return jnp.einsum('bhqk,bhkd->bhqd', probs.astype(v.dtype), v)

    out = lax.map(block, (q_blocks, c_blocks, starts))
    return jnp.moveaxis(out, 0, 2).reshape(b, h, s, d)


def moba_attention(q, k, v):
    b, h, s, d = q.shape
    nb = -(-s // MOBA_BLOCK)
    sp = nb * MOBA_BLOCK
    pad = sp - s
    q, k, v = [jnp.pad(t, ((0, 0), (0, 0), (0, pad), (0, 0))) for t in (q, k, v)]
    k_blocks = k.reshape(b, h, nb, MOBA_BLOCK, d)
    v_blocks = v.reshape(b, h, nb, MOBA_BLOCK, d)
    k_mean = jnp.mean(k_blocks.astype(jnp.float32), axis=3)
    gate = jnp.einsum('bhsd,bhnd->bhsn', q.astype(jnp.float32), k_mean)
    q_blk = jnp.arange(sp, dtype=jnp.int32) // MOBA_BLOCK
    past = jnp.arange(nb, dtype=jnp.int32)[None, :] < q_blk[:, None]
    gate = jnp.where(past, gate, -jnp.inf)
    n_sel = min(MOBA_TOPK, nb)
    _, sel = lax.top_k(gate, n_sel)
    sel_valid = sel < q_blk[:, None]

    nc = sp // MOBA_Q_CHUNK
    q_c = jnp.moveaxis(q.reshape(b, h, nc, MOBA_Q_CHUNK, d), 2, 0)
    sel_c = jnp.moveaxis(sel.reshape(b, h, nc, MOBA_Q_CHUNK, n_sel), 2, 0)
    val_c = jnp.moveaxis(sel_valid.reshape(b, h, nc, MOBA_Q_CHUNK, n_sel), 2, 0)
    starts = jnp.arange(nc, dtype=jnp.int32) * MOBA_Q_CHUNK
    gather = jax.vmap(jax.vmap(lambda blocks, ix: blocks[ix]))
    own_off = jnp.arange(MOBA_BLOCK, dtype=jnp.int32)
    scale = HEAD_DIM ** -0.5
    n_keys_sel = n_sel * MOBA_BLOCK

    def chunk(args):
        qc, sc, vc, s0 = args
        b0 = s0 // MOBA_BLOCK
        ks = gather(k_blocks, sc)
        vs = gather(v_blocks, sc)
        l_sel = jnp.einsum('bhqd,bhqnkd->bhqnk', qc, ks, preferred_element_type=jnp.float32) * scale
        l_sel = jnp.where(vc[..., None], l_sel, -jnp.inf).reshape(b, h, MOBA_Q_CHUNK, n_keys_sel)
        k_own = lax.dynamic_index_in_dim(k_blocks, b0, axis=2, keepdims=False)
        v_own = lax.dynamic_index_in_dim(v_blocks, b0, axis=2, keepdims=False)
        l_own = jnp.einsum('bhqd,bhkd->bhqk', qc, k_own, preferred_element_type=jnp.float32) * scale
        q_pos = s0 + jnp.arange(MOBA_Q_CHUNK, dtype=jnp.int32)
        k_pos = b0 * MOBA_BLOCK + own_off
        l_own = jnp.where(k_pos[None, :] <= q_pos[:, None], l_own, -jnp.inf)
        probs = jax.nn.softmax(jnp.concatenate([l_sel, l_own], axis=-1), axis=-1)
        p_sel = probs[..., :n_keys_sel].reshape(b, h, MOBA_Q_CHUNK, n_sel, MOBA_BLOCK)
        p_own = probs[..., n_keys_sel:]
        return (jnp.einsum('bhqnk,bhqnkd->bhqd', p_sel.astype(v.dtype), vs)
                + jnp.einsum('bhqk,bhkd->bhqd', p_own.astype(v.dtype), v_own))

    out = lax.map(chunk, (q_c, sel_c, val_c, starts))
    return jnp.moveaxis(out, 0, 2).reshape(b, h, sp, d)[:, :, :s]


def hierarchical_moe(h, w_group, b_group, w_fine, b_fine, w_gate, w_up, w_down):
    b, s, dm = h.shape
    t = b * s
    xt = h.reshape(t, dm)
    xf = xt.astype(jnp.float32)
    g_prob = jax.nn.softmax(xf @ w_group.astype(jnp.float32) + b_group.astype(jnp.float32), axis=-1)
    g_top, g_idx = lax.top_k(g_prob, 1)
    f_logits = (xf @ w_fine.astype(jnp.float32) + b_fine.astype(jnp.float32)).reshape(t, N_GROUPS, EXPERTS_PER_GROUP)
    f_logits = jnp.take_along_axis(f_logits, g_idx[:, :, None], axis=1)[:, 0]
    f_top, f_idx = lax.top_k(f_logits, TOP_K_FINE)
    weights = (g_top * jax.nn.softmax(f_top, axis=-1)).astype(h.dtype)
    expert_idx = g_idx * EXPERTS_PER_GROUP + f_idx

    n_assign = t * TOP_K_FINE
    flat_e = expert_idx.reshape(-1).astype(jnp.int32)
    flat_w = weights.reshape(-1)
    flat_t = jnp.repeat(jnp.arange(t, dtype=jnp.int32), TOP_K_FINE)
    order = jnp.argsort(flat_e)
    se, st, sw = flat_e[order], flat_t[order], flat_w[order]
    counts = jnp.bincount(flat_e, length=N_EXPERTS)
    starts = jnp.cumsum(counts) - counts
    padded = ((counts + MOE_ROW_BLOCK - 1) // MOE_ROW_BLOCK) * MOE_ROW_BLOCK
    pends = jnp.cumsum(padded)
    pstarts = pends - padded
    dest = pstarts[se] + (jnp.arange(n_assign, dtype=jnp.int32) - starts[se])
    n_rows = n_assign + N_EXPERTS * MOE_ROW_BLOCK
    n_blk = n_rows // MOE_ROW_BLOCK
    row_token = jnp.full((n_rows,), t, dtype=jnp.int32).at[dest].set(st)
    row_weight = jnp.zeros((n_rows,), dtype=h.dtype).at[dest].set(sw)
    block_start = jnp.arange(n_blk, dtype=jnp.int32) * MOE_ROW_BLOCK
    block_expert = jnp.minimum(jnp.searchsorted(pends, block_start, side='right'), N_EXPERTS - 1)
    x_pad = jnp.concatenate([xt, jnp.zeros((1, dm), xt.dtype)], axis=0)
    rows = x_pad[row_token].reshape(n_blk, MOE_ROW_BLOCK, dm)

    def expert_block(args):
        r, e = args
        return (jax.nn.silu(r @ w_gate[e]) * (r @ w_up[e])) @ w_down[e]

    out_rows = lax.map(expert_block, (rows, block_expert)).reshape(n_rows, dm)
    out = jax.ops.segment_sum(out_rows * row_weight[:, None], row_token, num_segments=t + 1)[:t]
    return out.reshape(b, s, dm)


def setup_inputs(seed: int = 0) -> dict:
    key = jax.random.key(seed)
    ks = jax.random.split(key, 24)
    f32 = jnp.float32

    def nrm(k, shape, fan_in):
        return jax.random.normal(k, shape, f32) * (fan_in ** -0.5)

    def gain(k, shape):
        return 1.0 + 0.01 * jax.random.normal(k, shape, f32)

    return {
        'x': jax.random.normal(ks[0], (BATCH, SEQ, D_MODEL), f32),
        'p': jax.random.normal(ks[1], (DEPTH, BATCH, SEQ, PLE_DIM), f32),
        'attn_norm': gain(ks[2], (DEPTH, D_MODEL)),
        'w_in': nrm(ks[3], (DEPTH, D_MODEL, IN_COLS), D_MODEL),
        'b_forget': 0.1 * jax.random.normal(ks[4], (DEPTH, N_FOX_HEADS), f32),
        'w_fox_branch': nrm(ks[5], (DEPTH, FOX_WIDTH, D_MODEL), FOX_WIDTH),
        'w_moba_branch': nrm(ks[6], (DEPTH, MOBA_WIDTH, D_MODEL), MOBA_WIDTH),
        'w_out': nrm(ks[7], (DEPTH, D_MODEL, D_MODEL), D_MODEL),
        'moe_norm': gain(ks[8], (DEPTH, D_MODEL)),
        'w_group': nrm(ks[9], (DEPTH, D_MODEL, N_GROUPS), D_MODEL),
        'b_group': 0.01 * jax.random.normal(ks[10], (DEPTH, N_GROUPS), f32),
        'w_fine': nrm(ks[11], (DEPTH, D_MODEL, N_EXPERTS), D_MODEL),
        'b_fine': 0.01 * jax.random.normal(ks[12], (DEPTH, N_EXPERTS), f32),
        'w_gate': nrm(ks[13], (DEPTH, N_EXPERTS, D_MODEL, D_EXPERT), D_MODEL),
        'w_up': nrm(ks[14], (DEPTH, N_EXPERTS, D_MODEL, D_EXPERT), D_MODEL),
        'w_down': nrm(ks[15], (DEPTH, N_EXPERTS, D_EXPERT, D_MODEL), D_EXPERT),
        'ple_norm': gain(ks[16], (DEPTH, D_MODEL)),
        'w_ple_gate': nrm(ks[17], (DEPTH, D_MODEL, D_MODEL), D_MODEL),
        'w_ple_proj': nrm(ks[18], (DEPTH, PLE_DIM, D_MODEL), PLE_DIM),
        'final_norm': gain(ks[19], (D_MODEL,)),
    }


def reference(x, p, attn_norm, w_in, b_forget, w_fox_branch, w_moba_branch, w_out,
              moe_norm, w_group, b_group, w_fine, b_fine, w_gate, w_up, w_down,
              ple_norm, w_ple_gate, w_ple_proj, final_norm):
    b, s, _ = x.shape
    split_at = [int(v) for v in np.cumsum([FOX_WIDTH] * 3 + [MOBA_WIDTH] * 3 + [N_FOX_HEADS, D_MODEL])]

    def heads(t, n):
        return t.reshape(b, s, n, HEAD_DIM).transpose(0, 2, 1, 3)

    def merge(t):
        return t.transpose(0, 2, 1, 3).reshape(b, s, -1)

    for i in range(DEPTH):
        h = rmsnorm(x, attn_norm[i])
        proj = jnp.einsum('bsd,dc->bsc', h, w_in[i])
        qf, kf, vf, qm, km, vm, f_logit, ga, gb = jnp.split(proj, split_at, axis=-1)
        log_f = jax.nn.log_sigmoid(f_logit.astype(jnp.float32) + b_forget[i].astype(jnp.float32))
        y_fox = forgetting_attention(heads(qf, N_FOX_HEADS), heads(kf, N_FOX_HEADS),
                                     heads(vf, N_FOX_HEADS), log_f.transpose(0, 2, 1))
        y_moba = moba_attention(rope(heads(qm, N_MOBA_HEADS)), rope(heads(km, N_MOBA_HEADS)),
                                heads(vm, N_MOBA_HEADS))
        ya = merge(y_fox) @ w_fox_branch[i]
        yb = merge(y_moba) @ w_moba_branch[i]
        mixed = jax.nn.sigmoid(ga) * ya + jax.nn.sigmoid(gb) * yb
        x = x + mixed @ w_out[i]
        h2 = rmsnorm(x, moe_norm[i])
        x = x + hierarchical_moe(h2, w_group[i], b_group[i], w_fine[i], b_fine[i],
                                 w_gate[i], w_up[i], w_down[i])
        g = jax.nn.sigmoid(rmsnorm(x, ple_norm[i]) @ w_ple_gate[i])
        x = x + g * (p[i] @ w_ple_proj[i])
    return rmsnorm(x, final_norm)
```

```python
import functools

import jax
import jax.numpy as jnp
from jax import lax
from jax.experimental import pallas as pl
from jax.experimental.pallas import tpu as pltpu

HEAD_DIM = 64
N_FOX_HEADS = 8
N_MOBA_HEADS = 8
MOBA_BLOCK = 256
MOBA_TOPK = 3
ROPE_THETA = 10000.0
N_GROUPS = 4
EXPERTS_PER_GROUP = 8
N_EXPERTS = N_GROUPS * EXPERTS_PER_GROUP
RMS_EPS = 1e-6

LANES = 128
HEADS_PER_STEP = LANES // HEAD_DIM
ATTN_BLOCK = 256
ROW_TILE = 512
FINAL_TILE = 256
EXPERT_TILE = 256
NEG_BIG = -1e30
VMEM_LIMIT = 48 * 1024 * 1024

_BF16 = jnp.bfloat16
_F32 = jnp.float32


def _params(*sem):
    return pltpu.CompilerParams(dimension_semantics=sem, vmem_limit_bytes=VMEM_LIMIT)


def _rms(x, g):
    return x * lax.rsqrt(jnp.mean(x * x, axis=-1, keepdims=True) + RMS_EPS) * g


def _const_spec(shape):
    return pl.BlockSpec(shape, lambda *_: (0,) * len(shape))


def _inproj_body(x_ref, g_ref, w_ref, wft_ref, bf_ref, cos_ref, sin_ref,
                 qf_ref, kf_ref, vf_ref, qm_ref, km_ref, vm_ref, gates_ref, c_ref, kmean_ref,
                 carry_ref, *, tm, width, gate_cols):
    j = pl.program_id(1)
    hb = _rms(x_ref[...], g_ref[...]).astype(_BF16)

    cos = jnp.concatenate([cos_ref[...]] * (width // LANES), axis=1)
    sin = jnp.concatenate([sin_ref[...]] * (width // LANES), axis=1)
    first_half = (lax.broadcasted_iota(jnp.int32, (1, width), 1) % HEAD_DIM) < (HEAD_DIM // 2)

    def rope(t):
        partner = jnp.where(first_half, pltpu.roll(t, width - HEAD_DIM // 2, 1),
                            pltpu.roll(t, HEAD_DIM // 2, 1))
        return t * cos + partner * sin

    def proj(c):
        return jnp.dot(hb, w_ref[:, c * width:(c + 1) * width], preferred_element_type=_F32)

    scale = HEAD_DIM ** -0.5
    qf_ref[...] = (proj(0) * scale).astype(_BF16)
    kf_ref[...] = proj(1).astype(_BF16)
    vf_ref[...] = proj(2).astype(_BF16)
    qm_ref[...] = (rope(proj(3)) * scale).astype(_BF16)
    km = rope(proj(4))
    km_ref[...] = km.astype(_BF16)
    vm_ref[...] = proj(5).astype(_BF16)

    @pl.when(j == 0)
    def _():
        kmean_ref[...] = jnp.zeros_like(kmean_ref)
        carry_ref[...] = jnp.zeros_like(carry_ref)

    blocks = tm // MOBA_BLOCK
    for b in range(blocks):
        kmean_ref[pl.ds(j * blocks + b, 1), :] = jnp.mean(
            km[b * MOBA_BLOCK:(b + 1) * MOBA_BLOCK], axis=0, keepdims=True)

    for c in range(gate_cols // width):
        g = jnp.dot(hb, w_ref[:, 6 * width + c * width:6 * width + (c + 1) * width],
                    preferred_element_type=_F32)
        gates_ref[:, c * width:(c + 1) * width] = jax.nn.sigmoid(g).astype(_BF16)

    z = lax.dot_general(wft_ref[...], hb, (((1,), (1,)), ((), ())),
                        preferred_element_type=_F32) + bf_ref[:, 0:1]
    log_f = jnp.minimum(z, 0.0) - jnp.log1p(jnp.exp(-jnp.abs(z)))
    r = lax.broadcasted_iota(jnp.int32, (tm, tm), 0)
    c = lax.broadcasted_iota(jnp.int32, (tm, tm), 1)
    tri = (r <= c).astype(_F32)
    cum = jnp.dot(log_f, tri, preferred_element_type=_F32,
                  precision=lax.Precision.HIGHEST) + carry_ref[:, 0:1]
    c_ref[...] = cum
    carry_ref[...] = jnp.broadcast_to(cum[:, tm - 1:tm], carry_ref.shape)


def _inproj(x, g, w, wft, bfp, cos, sin, *, tm):
    b, s, d = x.shape
    width = N_FOX_HEADS * HEAD_DIM
    gate_cols = w.shape[1] - 6 * width
    nb_pad = LANES
    hrows = wft.shape[0]
    act = jax.ShapeDtypeStruct((b, s, width), _BF16)
    act_spec = pl.BlockSpec((None, tm, width), lambda bi, j: (bi, j, 0))
    return pl.pallas_call(
        functools.partial(_inproj_body, tm=tm, width=width, gate_cols=gate_cols),
        grid=(b, s // tm),
        in_specs=[
            pl.BlockSpec((None, tm, d), lambda bi, j: (bi, j, 0)),
            _const_spec((1, d)),
            _const_spec(w.shape),
            _const_spec(wft.shape),
            _const_spec(bfp.shape),
            pl.BlockSpec((tm, LANES), lambda bi, j: (j, 0)),
            pl.BlockSpec((tm, LANES), lambda bi, j: (j, 0)),
        ],
        out_specs=[act_spec] * 6 + [
            pl.BlockSpec((None, tm, gate_cols), lambda bi, j: (bi, j, 0)),
            pl.BlockSpec((None, hrows, tm), lambda bi, j: (bi, 0, j)),
            pl.BlockSpec((None, nb_pad, width), lambda bi, j: (bi, 0, 0)),
        ],
        out_shape=[act] * 6 + [
            jax.ShapeDtypeStruct((b, s, gate_cols), _BF16),
            jax.ShapeDtypeStruct((b, hrows, s), _F32),
            jax.ShapeDtypeStruct((b, nb_pad, width), _F32),
        ],
        scratch_shapes=[pltpu.VMEM((hrows, LANES), _F32)],
        compiler_params=_params("arbitrary", "arbitrary"),
        name="inproj",
    )(x, g, w, wft, bfp, cos, sin)


def _attn_body(q_ref, k_ref, v_ref, aux_ref, o_ref, m_ref, l_ref, acc_ref, sel_ref, *, mode, t):
    hp = pl.program_id(1)
    i = pl.program_id(2)
    lane = lax.broadcasted_iota(jnp.int32, (1, LANES), 1)
    low = lane < HEAD_DIM
    q2 = q_ref[...]
    zero = jnp.zeros_like(q2)
    qh = [jnp.where(low, q2, zero), jnp.where(low, zero, q2)]

    m_ref[...] = jnp.full_like(m_ref, NEG_BIG)
    l_ref[...] = jnp.zeros_like(l_ref)
    acc_ref[...] = jnp.zeros_like(acc_ref)

    if mode == "moba":
        col = lax.broadcasted_iota(jnp.int32, (t, LANES), 1)
        kmean = aux_ref[...].astype(_BF16)
        for h in range(HEADS_PER_STEP):
            gate = lax.dot_general(qh[h], kmean, (((1,), (1,)), ((), ())),
                                   preferred_element_type=_F32)
            gate = jnp.where(col < i, gate, -jnp.inf)
            sel = jnp.zeros((t, LANES), _F32)
            for _ in range(MOBA_TOPK):
                mx = jnp.max(gate, axis=-1, keepdims=True)
                cand = jnp.where((gate == mx) & (mx > -jnp.inf), col, LANES)
                pick = col == jnp.min(cand, axis=-1, keepdims=True)
                sel = jnp.where(pick, 1.0, sel)
                gate = jnp.where(pick, -jnp.inf, gate)
            sel_ref[h] = sel

    def block(n, diag):
        start = pl.multiple_of(n * t, t)
        k2 = k_ref[pl.ds(start, t), :]
        v2 = v_ref[pl.ds(start, t), :]
        alphas, pvs = [], []
        for h in range(HEADS_PER_STEP):
            s = lax.dot_general(qh[h], k2, (((1,), (1,)), ((), ())), preferred_element_type=_F32)
            if mode == "fox":
                row = aux_ref[h:h + 1, pl.ds(start, t)]
                ref0 = aux_ref[h:h + 1, pl.ds(pl.multiple_of(i * t, t), LANES)][:, 0:1]
                s = s - (row - ref0)
            elif not diag:
                colsel = lax.broadcasted_iota(jnp.int32, (t, LANES), 1) == n
                picked = jnp.max(jnp.where(colsel, sel_ref[h], 0.0), axis=-1, keepdims=True)
                s = jnp.where(picked > 0.0, s, NEG_BIG)
            if diag:
                rr = lax.broadcasted_iota(jnp.int32, (t, t), 0)
                cc = lax.broadcasted_iota(jnp.int32, (t, t), 1)
                s = jnp.where(cc <= rr, s, NEG_BIG)
            m_prev = m_ref[h]
            m_new = jnp.maximum(m_prev, jnp.max(s, axis=-1, keepdims=True))
            alpha = jnp.exp(m_prev - m_new)
            p = jnp.exp(s - m_new[:, 0:1])
            l_ref[h] = alpha * l_ref[h] + jnp.sum(p, axis=-1, keepdims=True)
            m_ref[h] = m_new
            alphas.append(alpha)
            pvs.append(jnp.dot(p.astype(_BF16), v2, preferred_element_type=_F32))
        acc_ref[...] = (acc_ref[...] * jnp.where(low, alphas[0], alphas[1])
                        + jnp.where(low, pvs[0], pvs[1]))

    def body(n, carry):
        block(n, False)
        return carry

    lax.fori_loop(0, i, body, 0)
    block(i, True)
    o_ref[...] = (acc_ref[...] / jnp.where(low, l_ref[0], l_ref[1])).astype(o_ref.dtype)


def _attention(q, k, v, aux, *, mode):
    b, s, width = q.shape
    t = ATTN_BLOCK
    aux_spec = (pl.BlockSpec((None, None) + aux.shape[2:], lambda bi, hp, i: (bi, hp, 0, 0)) if mode == "fox"
                else pl.BlockSpec((None, aux.shape[1], LANES), lambda bi, hp, i: (bi, 0, hp)))
    return pl.pallas_call(
        functools.partial(_attn_body, mode=mode, t=t),
        grid=(b, width // LANES, s // t),
        in_specs=[
            pl.BlockSpec((None, t, LANES), lambda bi, hp, i: (bi, i, hp)),
            pl.BlockSpec((None, s, LANES), lambda bi, hp, i: (bi, 0, hp)),
            pl.BlockSpec((None, s, LANES), lambda bi, hp, i: (bi, 0, hp)),
            aux_spec,
        ],
        out_specs=pl.BlockSpec((None, t, LANES), lambda bi, hp, i: (bi, i, hp)),
        out_shape=jax.ShapeDtypeStruct((b, s, width), _BF16),
        scratch_shapes=[
            pltpu.VMEM((HEADS_PER_STEP, t, LANES), _F32),
            pltpu.VMEM((HEADS_PER_STEP, t, LANES), _F32),
            pltpu.VMEM((t, LANES), _F32),
            pltpu.VMEM((HEADS_PER_STEP, t, LANES), _F32),
        ],
        compiler_params=_params("arbitrary", "arbitrary", "arbitrary"),
        name=mode,
    )(q, k, v, aux)


def _pack_bf16_pair(a, b):
    lo = pltpu.bitcast(a.astype(_BF16).astype(_F32), jnp.uint32) >> 16
    hi = pltpu.bitcast(b.astype(_BF16).astype(_F32), jnp.uint32) & jnp.uint32(0xFFFF0000)
    return lo | hi


def _unpack_bf16_pair(u):
    lo = pltpu.bitcast(u << 16, _F32).astype(_BF16)
    hi = pltpu.bitcast(u & jnp.uint32(0xFFFF0000), _F32).astype(_BF16)
    return jnp.concatenate([lo, hi], axis=1)


def _postattn_body(yf_ref, ym_ref, gates_ref, x_ref, wfb_ref, wmb_ref, wout_ref, g_ref, wr_ref, br_ref,
                   x1_ref, h2_ref, route_ref, counts_ref, carry_ref, *, tm, d):
    step = pl.program_id(0)
    ya = jnp.dot(yf_ref[...], wfb_ref[...], preferred_element_type=_F32)
    yb = jnp.dot(ym_ref[...], wmb_ref[...], preferred_element_type=_F32)
    mixed = gates_ref[:, 0:d].astype(_F32) * ya + gates_ref[:, d:2 * d].astype(_F32) * yb
    x1 = x_ref[...] + jnp.dot(mixed.astype(_BF16), wout_ref[...], preferred_element_type=_F32)
    x1_ref[...] = x1
    h2 = _rms(x1, g_ref[...])
    h2_ref[...] = _pack_bf16_pair(h2[:, 0:d // 2], h2[:, d // 2:d])

    logits = jnp.dot(h2.astype(_BF16), wr_ref[...], preferred_element_type=_F32) + br_ref[...]
    lane = lax.broadcasted_iota(jnp.int32, (tm, LANES), 1)
    is_group = (lane >= N_EXPERTS) & (lane < N_EXPERTS + N_GROUPS)
    gl = jnp.where(is_group, logits, -jnp.inf)
    ge = jnp.exp(gl - jnp.max(gl, axis=-1, keepdims=True))
    gp = ge / jnp.sum(ge, axis=-1, keepdims=True)
    g_top = jnp.max(gp, axis=-1, keepdims=True)
    g_idx = jnp.min(jnp.where(is_group & (gp == g_top), lane, 2 * LANES), axis=-1,
                    keepdims=True) - N_EXPERTS
    in_group = (lane < N_EXPERTS) & ((lane // EXPERTS_PER_GROUP) == g_idx)
    fl = jnp.where(in_group, logits, -jnp.inf)
    f1 = jnp.max(fl, axis=-1, keepdims=True)
    e1 = jnp.min(jnp.where(fl == f1, lane, 2 * LANES), axis=-1, keepdims=True)
    fl2 = jnp.where(lane == e1, -jnp.inf, fl)
    f2 = jnp.max(fl2, axis=-1, keepdims=True)
    e2 = jnp.min(jnp.where(fl2 == f2, lane, 2 * LANES), axis=-1, keepdims=True)
    t2 = jnp.exp(f2 - f1)
    w1 = g_top * (1.0 / (1.0 + t2))
    w2 = g_top * (t2 / (1.0 + t2))

    @pl.when(step == 0)
    def _():
        carry_ref[...] = jnp.zeros_like(carry_ref)

    onehot = ((lane == e1) | (lane == e2)).astype(_F32)
    rr = lax.broadcasted_iota(jnp.int32, (tm, tm), 0)
    cc = lax.broadcasted_iota(jnp.int32, (tm, tm), 1)
    before = jnp.dot((cc < rr).astype(_BF16), onehot.astype(_BF16), preferred_element_type=_F32)
    before = before + carry_ref[0:1, :]
    r1 = jnp.sum(jnp.where(lane == e1, before, 0.0), axis=-1, keepdims=True)
    r2 = jnp.sum(jnp.where(lane == e2, before, 0.0), axis=-1, keepdims=True)
    total = carry_ref[0:1, :] + jnp.sum(onehot, axis=0, keepdims=True)
    carry_ref[...] = jnp.broadcast_to(total, carry_ref.shape)
    counts_ref[...] = jnp.broadcast_to(total, counts_ref.shape)

    slab = jnp.where(lane == 0, w1, 0.0)
    slab = jnp.where(lane == 1, w2, slab)
    slab = jnp.where(lane == 2, e1.astype(_F32), slab)
    slab = jnp.where(lane == 3, e2.astype(_F32), slab)
    slab = jnp.where(lane == 4, r1, slab)
    slab = jnp.where(lane == 5, r2, slab)
    route_ref[...] = slab


def _postattn(yf, ym, gates, x, wfb, wmb, wout, g, wr, br, *, tm):
    t, d = x.shape
    width = yf.shape[1]
    row = lambda cols: pl.BlockSpec((tm, cols), lambda i: (i, 0))
    return pl.pallas_call(
        functools.partial(_postattn_body, tm=tm, d=d),
        grid=(t // tm,),
        in_specs=[row(width), row(width), row(2 * d), row(d),
                  _const_spec(wfb.shape), _const_spec(wmb.shape), _const_spec(wout.shape),
                  _const_spec((1, d)), _const_spec(wr.shape), _const_spec(br.shape)],
        out_specs=[row(d), row(d // 2), row(LANES), _const_spec((8, LANES))],
        out_shape=[jax.ShapeDtypeStruct((t, d), _F32),
                   jax.ShapeDtypeStruct((t, d // 2), jnp.uint32),
                   jax.ShapeDtypeStruct((t, LANES), _F32),
                   jax.ShapeDtypeStruct((8, LANES), _F32)],
        scratch_shapes=[pltpu.VMEM((8, LANES), _F32)],
        compiler_params=_params("arbitrary"),
        name="postattn",
    )(yf, ym, gates, x, wfb, wmb, wout, g, wr, br)


def _dispatch_body(dest_ref, h2_ref, xs_in_ref, xs_ref, sem, *, tm):
    del xs_in_ref

    def row_copy(r, k):
        return pltpu.make_async_copy(h2_ref.at[pl.ds(r, 1)],
                                     xs_ref.at[pl.ds(dest_ref[2 * r + k], 1)], sem)

    def start(r, carry):
        row_copy(r, 0).start()
        row_copy(r, 1).start()
        return carry

    def wait(r, carry):
        row_copy(r, 0).wait()
        row_copy(r, 1).wait()
        return carry

    lax.fori_loop(0, tm, start, 0)
    lax.fori_loop(0, tm, wait, 0)


def _dispatch(dest_flat, h2p, xs0, *, tm):
    t, cols = h2p.shape
    return pl.pallas_call(
        functools.partial(_dispatch_body, tm=tm),
        grid=(t // tm,),
        in_specs=[pl.BlockSpec((2 * tm,), lambda i: (i,), memory_space=pltpu.SMEM),
                  pl.BlockSpec((tm, cols), lambda i: (i, 0)),
                  pl.BlockSpec(memory_space=pl.ANY)],
        out_specs=pl.BlockSpec(memory_space=pl.ANY),
        out_shape=jax.ShapeDtypeStruct(xs0.shape, xs0.dtype),
        scratch_shapes=[pltpu.SemaphoreType.DMA(())],
        input_output_aliases={2: 0},
        compiler_params=_params("arbitrary"),
        name="dispatch",
    )(dest_flat, h2p, xs0)


def _experts_body(be_ref, nused_ref, xs_ref, wg_ref, wu_ref, wd_ref, out_ref):
    used = pl.program_id(0) < nused_ref[0]

    @pl.when(jnp.logical_not(used))
    def _():
        out_ref[...] = jnp.zeros_like(out_ref)

    @pl.when(used)
    def _():
        rows = _unpack_bf16_pair(xs_ref[...])
        gate = jnp.dot(rows, wg_ref[...], preferred_element_type=_F32)
        up = jnp.dot(rows, wu_ref[...], preferred_element_type=_F32)
        act = (gate * jax.nn.sigmoid(gate)) * up
        out_ref[...] = jnp.dot(act.astype(_BF16), wd_ref[...], preferred_element_type=_F32)


def _experts(block_expert, n_used, xs, wg, wu, wd, *, te):
    n_rows, half = xs.shape
    _, d, de = wg.shape

    def blk(j, be, nu):
        return jnp.minimum(j, nu[0] - 1)

    return pl.pallas_call(
        _experts_body,
        grid_spec=pltpu.PrefetchScalarGridSpec(
            num_scalar_prefetch=2,
            grid=(n_rows // te,),
            in_specs=[pl.BlockSpec((te, half), lambda j, be, nu: (blk(j, be, nu), 0)),
                      pl.BlockSpec((None, d, de), lambda j, be, nu: (be[blk(j, be, nu)], 0, 0)),
                      pl.BlockSpec((None, d, de), lambda j, be, nu: (be[blk(j, be, nu)], 0, 0)),
                      pl.BlockSpec((None, de, d), lambda j, be, nu: (be[blk(j, be, nu)], 0, 0))],
            out_specs=pl.BlockSpec((te, d), lambda j, be, nu: (j, 0)),
        ),
        out_shape=jax.ShapeDtypeStruct((n_rows, d), _F32),
        compiler_params=_params("arbitrary"),
        name="experts",
    )(block_expert, n_used, xs, wg, wu, wd)


def _final_body(dest_ref, x1_ref, route_ref, p_ref, rows_ref, gp_ref, wpg_ref, wpp_ref, gf_ref,
                o_ref, buf_ref, sem, *, tm):
    def row_copy(r, k):
        return pltpu.make_async_copy(rows_ref.at[pl.ds(dest_ref[2 * r + k], 1)],
                                     buf_ref.at[k, pl.ds(r, 1)], sem)

    def start(r, carry):
        row_copy(r, 0).start()
        row_copy(r, 1).start()
        return carry

    def wait(r, carry):
        row_copy(r, 0).wait()
        row_copy(r, 1).wait()
        return carry

    lax.fori_loop(0, tm, start, 0)
    emb = jnp.dot(p_ref[...].astype(_BF16), wpp_ref[...], preferred_element_type=_F32)
    lax.fori_loop(0, tm, wait, 0)

    route = route_ref[...]
    x2 = x1_ref[...] + (buf_ref[0] * route[:, 0:1] + buf_ref[1] * route[:, 1:2])
    gate = jax.nn.sigmoid(jnp.dot(_rms(x2, gp_ref[...]).astype(_BF16), wpg_ref[...],
                                  preferred_element_type=_F32))
    o_ref[...] = _rms(x2 + gate * emb, gf_ref[...])


def _final(dest_flat, x1, route, p, rows, gp, wpg, wpp, gf, *, tm):
    t, d = x1.shape
    ple = p.shape[1]
    row = lambda cols: pl.BlockSpec((tm, cols), lambda i: (i, 0))
    return pl.pallas_call(
        functools.partial(_final_body, tm=tm),
        grid=(t // tm,),
        in_specs=[pl.BlockSpec((2 * tm,), lambda i: (i,), memory_space=pltpu.SMEM),
                  row(d), row(LANES), row(ple),
                  pl.BlockSpec(memory_space=pl.ANY),
                  _const_spec((1, d)), _const_spec(wpg.shape), _const_spec(wpp.shape),
                  _const_spec((1, d))],
        out_specs=row(d),
        out_shape=jax.ShapeDtypeStruct((t, d), _F32),
        scratch_shapes=[pltpu.VMEM((2, tm, d), _F32), pltpu.SemaphoreType.DMA(())],
        compiler_params=_params("arbitrary"),
        name="final",
    )(dest_flat, x1, route, p, rows, gp, wpg, wpp, gf)


def _rope_tables(s):
    half = HEAD_DIM // 2
    inv = 1.0 / (ROPE_THETA ** (jnp.arange(0, HEAD_DIM, 2, dtype=_F32) / HEAD_DIM))
    ang = jnp.arange(s, dtype=_F32)[:, None] * inv[None, :]
    cos, sin = jnp.cos(ang), jnp.sin(ang)
    reps = LANES // half
    cos_t = jnp.tile(cos, (1, reps))
    sin_t = jnp.tile(jnp.concatenate([-sin, sin], axis=1), (1, reps // 2))
    return cos_t, sin_t


def _layer(x, p, attn_norm, w_in, b_forget, w_fox_branch, w_moba_branch, w_out, moe_norm,
           w_group, b_group, w_fine, b_fine, w_gate, w_up, w_down, ple_norm, w_ple_gate, w_ple_proj,
           final_gain):
    b, s, d = x.shape
    t = b * s
    width = N_FOX_HEADS * HEAD_DIM
    assert N_MOBA_HEADS * HEAD_DIM == width and ATTN_BLOCK == MOBA_BLOCK
    assert s % ROW_TILE == 0 and ROW_TILE % MOBA_BLOCK == 0 and s // MOBA_BLOCK <= LANES
    assert t % FINAL_TILE == 0 and N_EXPERTS + N_GROUPS <= LANES

    qkv_cols = 6 * width
    f_cols = N_FOX_HEADS
    w_main = jnp.concatenate([w_in[:, :qkv_cols], w_in[:, qkv_cols + f_cols:]], axis=1).astype(_BF16)
    hrows = 16
    wft = jnp.zeros((hrows, d), _BF16).at[:f_cols].set(w_in[:, qkv_cols:qkv_cols + f_cols].T.astype(_BF16))
    bfp = jnp.zeros((hrows, LANES), _F32).at[:f_cols].set(
        jnp.broadcast_to(b_forget.astype(_F32)[:, None], (f_cols, LANES)))
    cos_t, sin_t = _rope_tables(s)

    qf, kf, vf, qm, km, vm, gates, cums, kmean = _inproj(
        x, attn_norm.reshape(1, d), w_main, wft, bfp, cos_t, sin_t, tm=ROW_TILE)
    cums = cums[:, :N_FOX_HEADS].reshape(b, N_FOX_HEADS // HEADS_PER_STEP, HEADS_PER_STEP, s)
    cums = jnp.pad(cums, ((0, 0), (0, 0), (0, 8 - HEADS_PER_STEP), (0, 0)))
    y_fox = _attention(qf, kf, vf, cums, mode="fox")
    y_moba = _attention(qm, km, vm, kmean, mode="moba")

    wr = jnp.zeros((d, LANES), _BF16)
    wr = wr.at[:, :N_EXPERTS].set(w_fine.astype(_BF16))
    wr = wr.at[:, N_EXPERTS:N_EXPERTS + N_GROUPS].set(w_group.astype(_BF16))
    br = jnp.zeros((1, LANES), _F32)
    br = br.at[0, :N_EXPERTS].set(b_fine.astype(_F32))
    br = br.at[0, N_EXPERTS:N_EXPERTS + N_GROUPS].set(b_group.astype(_F32))
    x1, h2p, route, counts = _postattn(
        y_fox.reshape(t, width), y_moba.reshape(t, width), gates.reshape(t, 2 * d), x.reshape(t, d),
        w_fox_branch.astype(_BF16), w_moba_branch.astype(_BF16), w_out.astype(_BF16),
        moe_norm.reshape(1, d), wr, br, tm=ROW_TILE)

    te = EXPERT_TILE
    n_blk = (2 * t) // te + N_EXPERTS
    counts = counts[0, :N_EXPERTS].astype(jnp.int32)
    blocks_per = (counts + te - 1) // te
    block_end = jnp.cumsum(blocks_per)
    row_start = (block_end - blocks_per) * te
    experts = route[:, 2:4].astype(jnp.int32)
    dest = (row_start[experts] + route[:, 4:6].astype(jnp.int32)).reshape(-1)
    n_used = block_end[-1:].astype(jnp.int32)
    block_expert = jnp.minimum(
        jnp.searchsorted(block_end, jnp.arange(n_blk, dtype=jnp.int32), side="right"),
        N_EXPERTS - 1).astype(jnp.int32)

    xs = _dispatch(dest, h2p, jnp.zeros((n_blk * te, d // 2), jnp.uint32), tm=ROW_TILE)
    rows = _experts(block_expert, n_used, xs, w_gate.astype(_BF16), w_up.astype(_BF16),
                    w_down.astype(_BF16), te=te)
    out = _final(dest, x1, route, p.reshape(t, -1), rows, ple_norm.reshape(1, d),
                 w_ple_gate.astype(_BF16), w_ple_proj.astype(_BF16), final_gain.reshape(1, d),
                 tm=FINAL_TILE)
    return out.reshape(b, s, d)


def kernel(x, p, attn_norm, w_in, b_forget, w_fox_branch, w_moba_branch, w_out, moe_norm, w_group,
           b_group, w_fine, b_fine, w_gate, w_up, w_down, ple_norm, w_ple_gate, w_ple_proj, final_norm):
    depth = p.shape[0]
    assert depth == 1, "the final norm is fused into the last layer's kernel"
    i = 0
    return _layer(x, p[i], attn_norm[i], w_in[i], b_forget[i], w_fox_branch[i], w_moba_branch[i],
                  w_out[i], moe_norm[i], w_group[i], b_group[i], w_fine[i], b_fine[i], w_gate[i],
                  w_up[i], w_down[i], ple_norm[i], w_ple_gate[i], w_ple_proj[i], final_norm)
```

```python
import functools

import jax
import jax.numpy as jnp
from jax import lax
from jax.experimental import pallas as pl
from jax.experimental.pallas import tpu as pltpu

HEAD_DIM = 64
N_FOX_HEADS = 8
N_MOBA_HEADS = 8
MOBA_BLOCK = 256
MOBA_TOPK = 3
ROPE_THETA = 10000.0
N_GROUPS = 4
EXPERTS_PER_GROUP = 8
N_EXPERTS = N_GROUPS * EXPERTS_PER_GROUP
RMS_EPS = 1e-6

LANES = 128
HEADS_PER_STEP = LANES // HEAD_DIM
ATTN_Q = 256
ATTN_K = 512
BIAS_PIECES = 3
ROW_TILE = 512
FINAL_TILE = 256
EXPERT_TILE = 256
NEG_BIG = -1e30
VMEM_LIMIT = 48 * 1024 * 1024

_BF16 = jnp.bfloat16
_F32 = jnp.float32


def _params(*sem):
    return pltpu.CompilerParams(dimension_semantics=sem, vmem_limit_bytes=VMEM_LIMIT)


def _rms(x, g):
    return x * lax.rsqrt(jnp.mean(x * x, axis=-1, keepdims=True) + RMS_EPS) * g


def _const_spec(shape):
    return pl.BlockSpec(shape, lambda *_: (0,) * len(shape))


def _inproj_body(x_ref, g_ref, w_ref, wt_ref, wf_ref, bf_ref, cos_ref, sin_ref, cost_ref, sint_ref,
                 qft_ref, kf_ref, vft_ref, qmt_ref, km_ref, vmt_ref, gates_ref, cb_ref, kmean_ref,
                 carry_ref, *, tm, width, gate_cols):
    j = pl.program_id(1)
    hb = _rms(x_ref[...], g_ref[...]).astype(_BF16)
    reps = width // LANES
    half = HEAD_DIM // 2

    def proj(c):
        return jnp.dot(hb, w_ref[:, c * width:(c + 1) * width], preferred_element_type=_F32)

    def proj_t(c):
        return lax.dot_general(wt_ref[c * width:(c + 1) * width, :], hb, (((1,), (1,)), ((), ())),
                               preferred_element_type=_F32)

    def rope(t):
        cos = jnp.concatenate([cos_ref[...]] * reps, axis=1)
        sin = jnp.concatenate([sin_ref[...]] * reps, axis=1)
        first = (lax.broadcasted_iota(jnp.int32, (1, width), 1) % HEAD_DIM) < half
        partner = jnp.where(first, pltpu.roll(t, width - half, 1), pltpu.roll(t, half, 1))
        return t * cos + partner * sin

    def rope_t(t):
        cos = jnp.concatenate([cost_ref[...]] * reps, axis=0)
        sin = jnp.concatenate([sint_ref[...]] * reps, axis=0)
        first = (lax.broadcasted_iota(jnp.int32, (width, 1), 0) % HEAD_DIM) < half
        partner = jnp.where(first, pltpu.roll(t, width - half, 0), pltpu.roll(t, half, 0))
        return t * cos + partner * sin

    scale = HEAD_DIM ** -0.5
    qft_ref[...] = (proj_t(0) * scale).astype(_BF16)
    vft_ref[...] = proj_t(1).astype(_BF16)
    qmt_ref[...] = (rope_t(proj_t(2)) * scale).astype(_BF16)
    vmt_ref[...] = proj_t(3).astype(_BF16)
    kf_ref[...] = proj(0).astype(_BF16)
    km = rope(proj(1))
    km_ref[...] = km.astype(_BF16)

    @pl.when(j == 0)
    def _():
        kmean_ref[...] = jnp.zeros_like(kmean_ref)
        carry_ref[...] = jnp.zeros_like(carry_ref)

    blocks = tm // MOBA_BLOCK
    for b in range(blocks):
        kmean_ref[pl.ds(j * blocks + b, 1), :] = jnp.mean(
            km[b * MOBA_BLOCK:(b + 1) * MOBA_BLOCK], axis=0, keepdims=True)

    for c in range(gate_cols // width):
        g = jnp.dot(hb, w_ref[:, (2 + c) * width:(3 + c) * width],
                    preferred_element_type=_F32)
        gates_ref[:, c * width:(c + 1) * width] = jax.nn.sigmoid(g).astype(_BF16)

    z = jnp.dot(hb, wf_ref[...], preferred_element_type=_F32) + bf_ref[...]
    log_f = jnp.minimum(z, 0.0) - jnp.log1p(jnp.exp(-jnp.abs(z)))
    r = lax.broadcasted_iota(jnp.int32, (tm, tm), 0)
    c = lax.broadcasted_iota(jnp.int32, (tm, tm), 1)
    cum = jnp.dot((c <= r).astype(_F32), log_f, preferred_element_type=_F32,
                  precision=lax.Precision.HIGHEST) + carry_ref[0:1, :]
    carry_ref[...] = jnp.broadcast_to(cum[tm - 1:tm, :], carry_ref.shape)

    head_lane = lax.broadcasted_iota(jnp.int32, (1, LANES), 1) < N_FOX_HEADS
    rest = jnp.where(head_lane, -cum, 0.0)
    placed = jnp.zeros((tm, LANES), _F32)
    for piece in range(BIAS_PIECES):
        part = rest.astype(_BF16).astype(_F32)
        rest = rest - part
        placed = placed + (pltpu.roll(part, N_FOX_HEADS * piece, 1) if piece else part)
    cb_ref[...] = placed.astype(_BF16)


def _inproj(x, g, w, wt, wf, bfp, cos, sin, cos_t, sin_t, *, tm):
    b, s, d = x.shape
    width = N_FOX_HEADS * HEAD_DIM
    gate_cols = w.shape[1] - 2 * width
    act = jax.ShapeDtypeStruct((b, s, width), _BF16)
    act_t = jax.ShapeDtypeStruct((b, width, s), _BF16)
    act_spec = pl.BlockSpec((None, tm, width), lambda bi, j: (bi, j, 0))
    act_t_spec = pl.BlockSpec((None, width, tm), lambda bi, j: (bi, 0, j))
    return pl.pallas_call(
        functools.partial(_inproj_body, tm=tm, width=width, gate_cols=gate_cols),
        grid=(b, s // tm),
        in_specs=[
            pl.BlockSpec((None, tm, d), lambda bi, j: (bi, j, 0)),
            _const_spec((1, d)),
            _const_spec(w.shape),
            _const_spec(wt.shape),
            _const_spec(wf.shape),
            _const_spec(bfp.shape),
            pl.BlockSpec((tm, LANES), lambda bi, j: (j, 0)),
            pl.BlockSpec((tm, LANES), lambda bi, j: (j, 0)),
            pl.BlockSpec((LANES, tm), lambda bi, j: (0, j)),
            pl.BlockSpec((LANES, tm), lambda bi, j: (0, j)),
        ],
        out_specs=[act_t_spec, act_spec, act_t_spec, act_t_spec, act_spec, act_t_spec,
                   pl.BlockSpec((None, tm, gate_cols), lambda bi, j: (bi, j, 0)),
                   pl.BlockSpec((None, tm, LANES), lambda bi, j: (bi, j, 0)),
                   pl.BlockSpec((None, LANES, width), lambda bi, j: (bi, 0, 0))],
        out_shape=[act_t, act, act_t, act_t, act, act_t,
                   jax.ShapeDtypeStruct((b, s, gate_cols), _BF16),
                   jax.ShapeDtypeStruct((b, s, LANES), _BF16),
                   jax.ShapeDtypeStruct((b, LANES, width), _F32)],
        scratch_shapes=[pltpu.VMEM((8, LANES), _F32)],
        compiler_params=_params("arbitrary", "arbitrary"),
        name="inproj",
    )(x, g, w, wt, wf, bfp, cos, sin, cos_t, sin_t)


def _attn_body(qt_ref, k_ref, kx_ref, vt_ref, aux_ref, o_ref, m_ref, l_ref, acc_ref, st_ref, p_ref,
               alpha_ref, *, mode, tq, tk, nb):
    hp = pl.program_id(1)
    i = pl.program_id(2)
    wide = HEADS_PER_STEP * tq
    qt = qt_ref[...]
    feat = lax.broadcasted_iota(jnp.int32, (LANES, 1), 0)
    zero = jnp.zeros_like(qt)
    wq = jnp.concatenate([jnp.where(feat < HEAD_DIM, qt, zero),
                          jnp.where(feat < HEAD_DIM, zero, qt)], axis=1)

    xr = lax.broadcasted_iota(jnp.int32, (LANES, wide), 0)
    xc = lax.broadcasted_iota(jnp.int32, (LANES, wide), 1)
    if mode == "fox":
        head = hp * HEADS_PER_STEP + xc // tq
        lower = ((xr % N_FOX_HEADS == head) & (xr < BIAS_PIECES * N_FOX_HEADS)).astype(_BF16)
    else:
        own = (i * tq) // MOBA_BLOCK
        gate = jnp.dot(aux_ref[...].astype(_BF16), wq, preferred_element_type=_F32)[0:nb]
        blk = lax.broadcasted_iota(jnp.int32, (nb, wide), 0)
        gate = jnp.where(blk < own, gate, -jnp.inf)
        keep = blk == own
        for _ in range(MOBA_TOPK):
            mx = jnp.max(gate, axis=0, keepdims=True)
            cand = jnp.where((gate == mx) & (mx > -jnp.inf), blk, nb)
            pick = blk == jnp.min(cand, axis=0, keepdims=True)
            keep = keep | pick
            gate = jnp.where(pick, -jnp.inf, gate)
        lower = jnp.where(keep, 0.0, NEG_BIG)
        if nb < LANES:
            lower = jnp.concatenate([lower, jnp.zeros((LANES - nb, wide), _F32)], axis=0)
        lower = lower.astype(_BF16)
    w_aug = jnp.concatenate([wq, lower], axis=0)

    m_ref[...] = jnp.full_like(m_ref, NEG_BIG)
    l_ref[...] = jnp.zeros_like(l_ref)
    acc_ref[...] = jnp.zeros_like(acc_ref)
    st_ref[1] = jnp.full((tk, wide), NEG_BIG, _F32)
    p_ref[...] = jnp.zeros_like(p_ref)
    alpha_ref[...] = jnp.ones_like(alpha_ref)

    def scores(t, causal):
        start = pl.multiple_of(t * tk, tk)
        ka = jnp.concatenate([k_ref[pl.ds(start, tk), :], kx_ref[pl.ds(start, tk), :]], axis=1)
        st = jnp.dot(ka, w_aug, preferred_element_type=_F32)
        if causal:
            key = start + lax.broadcasted_iota(jnp.int32, (tk, wide), 0)
            qry = i * tq + lax.broadcasted_iota(jnp.int32, (tk, wide), 1) % tq
            st = jnp.where(key <= qry, st, NEG_BIG)
        st_ref[t % 2] = st

    def softmax(t):
        slot = t % 2
        st = st_ref[slot]
        m_prev = m_ref[0:1, :]
        m_new = jnp.maximum(m_prev, jnp.max(st, axis=0, keepdims=True))
        alpha = jnp.exp(m_prev - m_new)
        p = jnp.exp(st - m_new)
        l_ref[...] = jnp.broadcast_to(alpha * l_ref[0:1, :] + jnp.sum(p, axis=0, keepdims=True),
                                      l_ref.shape)
        m_ref[...] = jnp.broadcast_to(m_new, m_ref.shape)
        p_ref[slot] = p.astype(_BF16)
        alpha_ref[slot] = jnp.broadcast_to(alpha, alpha_ref.shape[1:])

    def values(t):
        slot = t % 2
        start = pl.multiple_of(jnp.maximum(t, 0) * tk, tk)
        pv = jnp.dot(vt_ref[:, pl.ds(start, tk)], p_ref[slot], preferred_element_type=_F32)
        acc_ref[...] = acc_ref[...] * alpha_ref[slot][0:1, :] + pv

    full = (i * tq) // tk

    @pl.when(full > 0)
    def _():
        scores(0, False)

    def body(t, carry):
        values(t - 2)
        softmax(t - 1)
        scores(t, False)
        return carry

    lax.fori_loop(1, full, body, 0)
    values(full - 2)
    softmax(full - 1)
    scores(full, True)
    values(full - 1)
    softmax(full)
    values(full)

    out = acc_ref[...] / l_ref[0:1, :]
    out_t = jnp.concatenate([out[0:HEAD_DIM, 0:tq], out[HEAD_DIM:LANES, tq:wide]], axis=0)
    o_ref[...] = out_t.T.astype(o_ref.dtype)


def _attention(qt, k, kx, vt, aux, *, mode):
    b, width, s = qt.shape
    tq, tk = ATTN_Q, ATTN_K
    nb = s // MOBA_BLOCK
    if mode == "fox":
        aux_spec = _const_spec(aux.shape)
        kx_spec = pl.BlockSpec((None, s, LANES), lambda bi, hp, i: (bi, 0, 0))
    else:
        aux_spec = pl.BlockSpec((None, LANES, LANES), lambda bi, hp, i: (bi, 0, hp))
        kx_spec = _const_spec(kx.shape)
    return pl.pallas_call(
        functools.partial(_attn_body, mode=mode, tq=tq, tk=tk, nb=nb),
        grid=(b, width // LANES, s // tq),
        in_specs=[
            pl.BlockSpec((None, LANES, tq), lambda bi, hp, i: (bi, hp, i)),
            pl.BlockSpec((None, s, LANES), lambda bi, hp, i: (bi, 0, hp)),
            kx_spec,
            pl.BlockSpec((None, LANES, s), lambda bi, hp, i: (bi, hp, 0)),
            aux_spec,
        ],
        out_specs=pl.BlockSpec((None, tq, LANES), lambda bi, hp, i: (bi, i, hp)),
        out_shape=jax.ShapeDtypeStruct((b, s, width), _BF16),
        scratch_shapes=[
            pltpu.VMEM((8, HEADS_PER_STEP * tq), _F32),
            pltpu.VMEM((8, HEADS_PER_STEP * tq), _F32),
            pltpu.VMEM((LANES, HEADS_PER_STEP * tq), _F32),
            pltpu.VMEM((2, tk, HEADS_PER_STEP * tq), _F32),
            pltpu.VMEM((2, tk, HEADS_PER_STEP * tq), _BF16),
            pltpu.VMEM((2, 8, HEADS_PER_STEP * tq), _F32),
        ],
        compiler_params=_params("arbitrary", "arbitrary", "arbitrary"),
        name=mode,
    )(qt, k, kx, vt, aux)


def _pack_bf16_pair(a, b):
    lo = pltpu.bitcast(a.astype(_BF16).astype(_F32), jnp.uint32) >> 16
    hi = pltpu.bitcast(b.astype(_BF16).astype(_F32), jnp.uint32) & jnp.uint32(0xFFFF0000)
    return lo | hi


def _unpack_bf16_pair(u):
    lo = pltpu.bitcast(u << 16, _F32).astype(_BF16)
    hi = pltpu.bitcast(u & jnp.uint32(0xFFFF0000), _F32).astype(_BF16)
    return jnp.concatenate([lo, hi], axis=1)


def _postattn_body(yf_ref, ym_ref, gates_ref, x_ref, wfb_ref, wmb_ref, wout_ref, g_ref, wr_ref, br_ref,
                   x1_ref, h2_ref, route_ref, counts_ref, carry_ref, *, tm, d):
    step = pl.program_id(0)
    ya = jnp.dot(yf_ref[...], wfb_ref[...], preferred_element_type=_F32)
    yb = jnp.dot(ym_ref[...], wmb_ref[...], preferred_element_type=_F32)
    mixed = gates_ref[:, 0:d].astype(_F32) * ya + gates_ref[:, d:2 * d].astype(_F32) * yb
    x1 = x_ref[...] + jnp.dot(mixed.astype(_BF16), wout_ref[...], preferred_element_type=_F32)
    x1_ref[...] = x1
    h2 = _rms(x1, g_ref[...])
    h2_ref[...] = _pack_bf16_pair(h2[:, 0:d // 2], h2[:, d // 2:d])

    logits = jnp.dot(h2.astype(_BF16), wr_ref[...], preferred_element_type=_F32) + br_ref[...]
    lane = lax.broadcasted_iota(jnp.int32, (tm, LANES), 1)
    is_group = (lane >= N_EXPERTS) & (lane < N_EXPERTS + N_GROUPS)
    gl = jnp.where(is_group, logits, -jnp.inf)
    ge = jnp.exp(gl - jnp.max(gl, axis=-1, keepdims=True))
    gp = ge / jnp.sum(ge, axis=-1, keepdims=True)
    g_top = jnp.max(gp, axis=-1, keepdims=True)
    g_idx = jnp.min(jnp.where(is_group & (gp == g_top), lane, 2 * LANES), axis=-1,
                    keepdims=True) - N_EXPERTS
    in_group = (lane < N_EXPERTS) & ((lane // EXPERTS_PER_GROUP) == g_idx)
    fl = jnp.where(in_group, logits, -jnp.inf)
    f1 = jnp.max(fl, axis=-1, keepdims=True)
    e1 = jnp.min(jnp.where(fl == f1, lane, 2 * LANES), axis=-1, keepdims=True)
    fl2 = jnp.where(lane == e1, -jnp.inf, fl)
    f2 = jnp.max(fl2, axis=-1, keepdims=True)
    e2 = jnp.min(jnp.where(fl2 == f2, lane, 2 * LANES), axis=-1, keepdims=True)
    t2 = jnp.exp(f2 - f1)
    w1 = g_top * (1.0 / (1.0 + t2))
    w2 = g_top * (t2 / (1.0 + t2))

    @pl.when(step == 0)
    def _():
        carry_ref[...] = jnp.zeros_like(carry_ref)

    onehot = ((lane == e1) | (lane == e2)).astype(_F32)
    rr = lax.broadcasted_iota(jnp.int32, (tm, tm), 0)
    cc = lax.broadcasted_iota(jnp.int32, (tm, tm), 1)
    before = jnp.dot((cc < rr).astype(_BF16), onehot.astype(_BF16), preferred_element_type=_F32)
    before = before + carry_ref[0:1, :]
    r1 = jnp.sum(jnp.where(lane == e1, before, 0.0), axis=-1, keepdims=True)
    r2 = jnp.sum(jnp.where(lane == e2, before, 0.0), axis=-1, keepdims=True)
    total = carry_ref[0:1, :] + jnp.sum(onehot, axis=0, keepdims=True)
    carry_ref[...] = jnp.broadcast_to(total, carry_ref.shape)
    counts_ref[...] = jnp.broadcast_to(total, counts_ref.shape)

    slab = jnp.where(lane == 0, w1, 0.0)
    slab = jnp.where(lane == 1, w2, slab)
    slab = jnp.where(lane == 2, e1.astype(_F32), slab)
    slab = jnp.where(lane == 3, e2.astype(_F32), slab)
    slab = jnp.where(lane == 4, r1, slab)
    slab = jnp.where(lane == 5, r2, slab)
    route_ref[...] = slab


def _postattn(yf, ym, gates, x, wfb, wmb, wout, g, wr, br, *, tm):
    t, d = x.shape
    width = yf.shape[1]
    row = lambda cols: pl.BlockSpec((tm, cols), lambda i: (i, 0))
    return pl.pallas_call(
        functools.partial(_postattn_body, tm=tm, d=d),
        grid=(t // tm,),
        in_specs=[row(width), row(width), row(2 * d), row(d),
                  _const_spec(wfb.shape), _const_spec(wmb.shape), _const_spec(wout.shape),
                  _const_spec((1, d)), _const_spec(wr.shape), _const_spec(br.shape)],
        out_specs=[row(d), row(d // 2), row(LANES), _const_spec((8, LANES))],
        out_shape=[jax.ShapeDtypeStruct((t, d), _F32),
                   jax.ShapeDtypeStruct((t, d // 2), jnp.uint32),
                   jax.ShapeDtypeStruct((t, LANES), _F32),
                   jax.ShapeDtypeStruct((8, LANES), _F32)],
        scratch_shapes=[pltpu.VMEM((8, LANES), _F32)],
        compiler_params=_params("arbitrary"),
        name="postattn",
    )(yf, ym, gates, x, wfb, wmb, wout, g, wr, br)


def _dispatch_body(dest_ref, h2_ref, xs_in_ref, xs_ref, sem, *, tm):
    del xs_in_ref

    def row_copy(r, k):
        return pltpu.make_async_copy(h2_ref.at[pl.ds(r, 1)],
                                     xs_ref.at[pl.ds(dest_ref[2 * r + k], 1)], sem)

    def start(r, carry):
        row_copy(r, 0).start()
        row_copy(r, 1).start()
        return carry

    def wait(r, carry):
        row_copy(r, 0).wait()
        row_copy(r, 1).wait()
        return carry

    lax.fori_loop(0, tm, start, 0)
    lax.fori_loop(0, tm, wait, 0)


def _dispatch(dest_flat, h2p, xs0, *, tm):
    t, cols = h2p.shape
    return pl.pallas_call(
        functools.partial(_dispatch_body, tm=tm),
        grid=(t // tm,),
        in_specs=[pl.BlockSpec((2 * tm,), lambda i: (i,), memory_space=pltpu.SMEM),
                  pl.BlockSpec((tm, cols), lambda i: (i, 0)),
                  pl.BlockSpec(memory_space=pl.ANY)],
        out_specs=pl.BlockSpec(memory_space=pl.ANY),
        out_shape=jax.ShapeDtypeStruct(xs0.shape, xs0.dtype),
        scratch_shapes=[pltpu.SemaphoreType.DMA(())],
        input_output_aliases={2: 0},
        compiler_params=_params("arbitrary"),
        name="dispatch",
    )(dest_flat, h2p, xs0)


def _experts_body(be_ref, nused_ref, xs_ref, wg_ref, wu_ref, wd_ref, out_ref):
    used = pl.program_id(0) < nused_ref[0]

    @pl.when(jnp.logical_not(used))
    def _():
        out_ref[...] = jnp.zeros_like(out_ref)

    @pl.when(used)
    def _():
        rows = _unpack_bf16_pair(xs_ref[...])
        gate = jnp.dot(rows, wg_ref[...], preferred_element_type=_F32)
        up = jnp.dot(rows, wu_ref[...], preferred_element_type=_F32)
        act = (gate * jax.nn.sigmoid(gate)) * up
        out_ref[...] = jnp.dot(act.astype(_BF16), wd_ref[...], preferred_element_type=_F32)


def _experts(block_expert, n_used, xs, wg, wu, wd, *, te):
    n_rows, half = xs.shape
    _, d, de = wg.shape

    def blk(j, be, nu):
        return jnp.minimum(j, nu[0] - 1)

    return pl.pallas_call(
        _experts_body,
        grid_spec=pltpu.PrefetchScalarGridSpec(
            num_scalar_prefetch=2,
            grid=(n_rows // te,),
            in_specs=[pl.BlockSpec((te, half), lambda j, be, nu: (blk(j, be, nu), 0)),
                      pl.BlockSpec((None, d, de), lambda j, be, nu: (be[blk(j, be, nu)], 0, 0)),
                      pl.BlockSpec((None, d, de), lambda j, be, nu: (be[blk(j, be, nu)], 0, 0)),
                      pl.BlockSpec((None, de, d), lambda j, be, nu: (be[blk(j, be, nu)], 0, 0))],
            out_specs=pl.BlockSpec((te, d), lambda j, be, nu: (j, 0)),
        ),
        out_shape=jax.ShapeDtypeStruct((n_rows, d), _F32),
        compiler_params=_params("arbitrary"),
        name="experts",
    )(block_expert, n_used, xs, wg, wu, wd)


def _final_body(dest_ref, x1_ref, route_ref, p_ref, rows_ref, gp_ref, wpg_ref, wpp_ref, gf_ref,
                o_ref, buf_ref, sem, *, tm):
    def row_copy(r, k):
        return pltpu.make_async_copy(rows_ref.at[pl.ds(dest_ref[2 * r + k], 1)],
                                     buf_ref.at[k, pl.ds(r, 1)], sem)

    def start(r, carry):
        row_copy(r, 0).start()
        row_copy(r, 1).start()
        return carry

    def wait(r, carry):
        row_copy(r, 0).wait()
        row_copy(r, 1).wait()
        return carry

    lax.fori_loop(0, tm, start, 0)
    emb = jnp.dot(p_ref[...].astype(_BF16), wpp_ref[...], preferred_element_type=_F32)
    lax.fori_loop(0, tm, wait, 0)

    route = route_ref[...]
    x2 = x1_ref[...] + (buf_ref[0] * route[:, 0:1] + buf_ref[1] * route[:, 1:2])
    gate = jax.nn.sigmoid(jnp.dot(_rms(x2, gp_ref[...]).astype(_BF16), wpg_ref[...],
                                  preferred_element_type=_F32))
    o_ref[...] = _rms(x2 + gate * emb, gf_ref[...])


def _final(dest_flat, x1, route, p, rows, gp, wpg, wpp, gf, *, tm):
    t, d = x1.shape
    ple = p.shape[1]
    row = lambda cols: pl.BlockSpec((tm, cols), lambda i: (i, 0))
    return pl.pallas_call(
        functools.partial(_final_body, tm=tm),
        grid=(t // tm,),
        in_specs=[pl.BlockSpec((2 * tm,), lambda i: (i,), memory_space=pltpu.SMEM),
                  row(d), row(LANES), row(ple),
                  pl.BlockSpec(memory_space=pl.ANY),
                  _const_spec((1, d)), _const_spec(wpg.shape), _const_spec(wpp.shape),
                  _const_spec((1, d))],
        out_specs=row(d),
        out_shape=jax.ShapeDtypeStruct((t, d), _F32),
        scratch_shapes=[pltpu.VMEM((2, tm, d), _F32), pltpu.SemaphoreType.DMA(())],
        compiler_params=_params("arbitrary"),
        name="final",
    )(dest_flat, x1, route, p, rows, gp, wpg, wpp, gf)


def _rope_tables(s):
    half = HEAD_DIM // 2
    inv = 1.0 / (ROPE_THETA ** (jnp.arange(0, HEAD_DIM, 2, dtype=_F32) / HEAD_DIM))
    ang = jnp.arange(s, dtype=_F32)[:, None] * inv[None, :]
    cos, sin = jnp.cos(ang), jnp.sin(ang)
    reps = LANES // half
    cos_t = jnp.tile(cos, (1, reps))
    sin_t = jnp.tile(jnp.concatenate([-sin, sin], axis=1), (1, reps // 2))
    return cos_t, sin_t


def _layer(x, p, attn_norm, w_in, b_forget, w_fox_branch, w_moba_branch, w_out, moe_norm,
           w_group, b_group, w_fine, b_fine, w_gate, w_up, w_down, ple_norm, w_ple_gate, w_ple_proj,
           final_gain):
    b, s, d = x.shape
    t = b * s
    width = N_FOX_HEADS * HEAD_DIM
    assert N_MOBA_HEADS * HEAD_DIM == width and MOBA_BLOCK % ATTN_Q == 0 and ATTN_K % MOBA_BLOCK == 0
    assert BIAS_PIECES * N_FOX_HEADS <= LANES and s % ATTN_K == 0
    assert s % ROW_TILE == 0 and ROW_TILE % MOBA_BLOCK == 0 and s // MOBA_BLOCK <= LANES
    assert t % FINAL_TILE == 0 and N_EXPERTS + N_GROUPS <= LANES

    qkv_cols = 6 * width
    f_cols = N_FOX_HEADS
    chunk = lambda c: w_in[:, c * width:(c + 1) * width]
    w_rows = jnp.concatenate([chunk(1), chunk(4), w_in[:, qkv_cols + f_cols:]], axis=1).astype(_BF16)
    w_cols = jnp.concatenate([chunk(0), chunk(2), chunk(3), chunk(5)], axis=1).T.astype(_BF16)
    wf = jnp.zeros((d, LANES), _BF16).at[:, :f_cols].set(w_in[:, qkv_cols:qkv_cols + f_cols].astype(_BF16))
    bfp = jnp.zeros((1, LANES), _F32).at[0, :f_cols].set(b_forget.astype(_F32))
    cos, sin = _rope_tables(s)

    qft, kf, vft, qmt, km, vmt, gates, cb, kmean = _inproj(
        x, attn_norm.reshape(1, d), w_rows, w_cols, wf, bfp, cos, sin, cos.T, sin.T, tm=ROW_TILE)
    y_fox = _attention(qft, kf, cb, vft, jnp.zeros((8, LANES), _F32), mode="fox")
    block_of_key = jnp.arange(s, dtype=jnp.int32)[:, None] // MOBA_BLOCK
    block_onehot = (block_of_key == jnp.arange(LANES, dtype=jnp.int32)[None, :]).astype(_BF16)
    y_moba = _attention(qmt, km, block_onehot, vmt, kmean, mode="moba")

    wr = jnp.zeros((d, LANES), _BF16)
    wr = wr.at[:, :N_EXPERTS].set(w_fine.astype(_BF16))
    wr = wr.at[:, N_EXPERTS:N_EXPERTS + N_GROUPS].set(w_group.astype(_BF16))
    br = jnp.zeros((1, LANES), _F32)
    br = br.at[0, :N_EXPERTS].set(b_fine.astype(_F32))
    br = br.at[0, N_EXPERTS:N_EXPERTS + N_GROUPS].set(b_group.astype(_F32))
    x1, h2p, route, counts = _postattn(
        y_fox.reshape(t, width), y_moba.reshape(t, width), gates.reshape(t, 2 * d), x.reshape(t, d),
        w_fox_branch.astype(_BF16), w_moba_branch.astype(_BF16), w_out.astype(_BF16),
        moe_norm.reshape(1, d), wr, br, tm=ROW_TILE)

    te = EXPERT_TILE
    n_blk = (2 * t) // te + N_EXPERTS
    counts = counts[0, :N_EXPERTS].astype(jnp.int32)
    blocks_per = (counts + te - 1) // te
    block_end = jnp.cumsum(blocks_per)
    row_start = (block_end - blocks_per) * te
    experts = route[:, 2:4].astype(jnp.int32)
    dest = (row_start[experts] + route[:, 4:6].astype(jnp.int32)).reshape(-1)
    n_used = block_end[-1:].astype(jnp.int32)
    block_expert = jnp.minimum(
        jnp.searchsorted(block_end, jnp.arange(n_blk, dtype=jnp.int32), side="right"),
        N_EXPERTS - 1).astype(jnp.int32)

    xs = _dispatch(dest, h2p, jnp.zeros((n_blk * te, d // 2), jnp.uint32), tm=ROW_TILE)
    rows = _experts(block_expert, n_used, xs, w_gate.astype(_BF16), w_up.astype(_BF16),
                    w_down.astype(_BF16), te=te)
    out = _final(dest, x1, route, p.reshape(t, -1), rows, ple_norm.reshape(1, d),
                 w_ple_gate.astype(_BF16), w_ple_proj.astype(_BF16), final_gain.reshape(1, d),
                 tm=FINAL_TILE)
    return out.reshape(b, s, d)


def kernel(x, p, attn_norm, w_in, b_forget, w_fox_branch, w_moba_branch, w_out, moe_norm, w_group,
           b_group, w_fine, b_fine, w_gate, w_up, w_down, ple_norm, w_ple_gate, w_ple_proj, final_norm):
    depth = p.shape[0]
    assert depth == 1, "the final norm is fused into the last layer's kernel"
    i = 0
    return _layer(x, p[i], attn_norm[i], w_in[i], b_forget[i], w_fox_branch[i], w_moba_branch[i],
                  w_out[i], moe_norm[i], w_group[i], b_group[i], w_fine[i], b_fine[i], w_gate[i],
                  w_up[i], w_down[i], ple_norm[i], w_ple_gate[i], w_ple_proj[i], final_norm)
```

```python
import functools

import jax
import jax.numpy as jnp
from jax import lax
from jax.experimental import pallas as pl
from jax.experimental.pallas import tpu as pltpu

HEAD_DIM = 64
N_FOX_HEADS = 8
N_MOBA_HEADS = 8
MOBA_BLOCK = 256
MOBA_TOPK = 3
ROPE_THETA = 10000.0
N_GROUPS = 4
EXPERTS_PER_GROUP = 8
N_EXPERTS = N_GROUPS * EXPERTS_PER_GROUP
RMS_EPS = 1e-6

LANES = 128
HEADS_PER_STEP = LANES // HEAD_DIM
ATTN_Q = 256
ATTN_K = 256
PIPE_UNITS = 8
SOFTMAX_LAG = 2
ONES_ROWS = 16
SOFTMAX_ROWS = 64
BIAS_PIECES = 3
ROW_TILE = 512
FINAL_TILE = 256
EXPERT_TILE = 256
LOG2_E = 1.4426950408889634
NEG_BIG = -1e30
VMEM_LIMIT = 48 * 1024 * 1024

_BF16 = jnp.bfloat16
_F32 = jnp.float32


def _params(*sem):
    return pltpu.CompilerParams(dimension_semantics=sem, vmem_limit_bytes=VMEM_LIMIT)


def _rms(x, g):
    return x * lax.rsqrt(jnp.mean(x * x, axis=-1, keepdims=True) + RMS_EPS) * g


def _const_spec(shape):
    return pl.BlockSpec(shape, lambda *_: (0,) * len(shape))


def _inproj_body(x_ref, g_ref, w_ref, wt_ref, wf_ref, bf_ref, cos_ref, sin_ref, cost_ref, sint_ref,
                 qft_ref, kf_ref, vft_ref, qmt_ref, km_ref, vmt_ref, gates_ref, cb_ref, kmean_ref,
                 carry_ref, *, tm, width, gate_cols):
    j = pl.program_id(1)
    hb = _rms(x_ref[...], g_ref[...]).astype(_BF16)
    reps = width // LANES
    half = HEAD_DIM // 2

    def proj(c):
        return jnp.dot(hb, w_ref[:, c * width:(c + 1) * width], preferred_element_type=_F32)

    def proj_t(c):
        return lax.dot_general(wt_ref[c * width:(c + 1) * width, :], hb, (((1,), (1,)), ((), ())),
                               preferred_element_type=_F32)

    def rope(t):
        cos = jnp.concatenate([cos_ref[...]] * reps, axis=1)
        sin = jnp.concatenate([sin_ref[...]] * reps, axis=1)
        first = (lax.broadcasted_iota(jnp.int32, (1, width), 1) % HEAD_DIM) < half
        partner = jnp.where(first, pltpu.roll(t, width - half, 1), pltpu.roll(t, half, 1))
        return t * cos + partner * sin

    def rope_t(t):
        cos = jnp.concatenate([cost_ref[...]] * reps, axis=0)
        sin = jnp.concatenate([sint_ref[...]] * reps, axis=0)
        first = (lax.broadcasted_iota(jnp.int32, (width, 1), 0) % HEAD_DIM) < half
        partner = jnp.where(first, pltpu.roll(t, width - half, 0), pltpu.roll(t, half, 0))
        return t * cos + partner * sin

    scale = HEAD_DIM ** -0.5 * LOG2_E
    qft_ref[...] = (proj_t(0) * scale).astype(_BF16)
    vft_ref[...] = proj_t(1).astype(_BF16)
    qmt_ref[...] = (rope_t(proj_t(2)) * scale).astype(_BF16)
    vmt_ref[...] = proj_t(3).astype(_BF16)
    kf_ref[...] = proj(0).astype(_BF16)
    km = rope(proj(1))
    km_ref[...] = km.astype(_BF16)

    @pl.when(j == 0)
    def _():
        kmean_ref[...] = jnp.zeros_like(kmean_ref)
        carry_ref[...] = jnp.zeros_like(carry_ref)

    blocks = tm // MOBA_BLOCK
    for b in range(blocks):
        kmean_ref[pl.ds(j * blocks + b, 1), :] = jnp.mean(
            km[b * MOBA_BLOCK:(b + 1) * MOBA_BLOCK], axis=0, keepdims=True)

    for c in range(gate_cols // width):
        g = jnp.dot(hb, w_ref[:, (2 + c) * width:(3 + c) * width],
                    preferred_element_type=_F32)
        gates_ref[:, c * width:(c + 1) * width] = jax.nn.sigmoid(g).astype(_BF16)

    z = jnp.dot(hb, wf_ref[...], preferred_element_type=_F32) + bf_ref[...]
    log_f = jnp.minimum(z, 0.0) - jnp.log1p(jnp.exp(-jnp.abs(z)))
    r = lax.broadcasted_iota(jnp.int32, (tm, tm), 0)
    c = lax.broadcasted_iota(jnp.int32, (tm, tm), 1)
    cum = jnp.dot((c <= r).astype(_F32), log_f, preferred_element_type=_F32,
                  precision=lax.Precision.HIGHEST) + carry_ref[0:1, :]
    carry_ref[...] = jnp.broadcast_to(cum[tm - 1:tm, :], carry_ref.shape)

    head_lane = lax.broadcasted_iota(jnp.int32, (1, LANES), 1) < N_FOX_HEADS
    rest = jnp.where(head_lane, -cum * LOG2_E, 0.0)
    placed = jnp.zeros((tm, LANES), _F32)
    for piece in range(BIAS_PIECES):
        part = rest.astype(_BF16).astype(_F32)
        rest = rest - part
        placed = placed + (pltpu.roll(part, N_FOX_HEADS * piece, 1) if piece else part)
    cb_ref[...] = placed.astype(_BF16)


def _inproj(x, g, w, wt, wf, bfp, cos, sin, cos_t, sin_t, *, tm):
    b, s, d = x.shape
    width = N_FOX_HEADS * HEAD_DIM
    gate_cols = w.shape[1] - 2 * width
    act = jax.ShapeDtypeStruct((b, s, width), _BF16)
    act_t = jax.ShapeDtypeStruct((b, width, s), _BF16)
    act_spec = pl.BlockSpec((None, tm, width), lambda bi, j: (bi, j, 0))
    act_t_spec = pl.BlockSpec((None, width, tm), lambda bi, j: (bi, 0, j))
    return pl.pallas_call(
        functools.partial(_inproj_body, tm=tm, width=width, gate_cols=gate_cols),
        grid=(b, s // tm),
        in_specs=[
            pl.BlockSpec((None, tm, d), lambda bi, j: (bi, j, 0)),
            _const_spec((1, d)),
            _const_spec(w.shape),
            _const_spec(wt.shape),
            _const_spec(wf.shape),
            _const_spec(bfp.shape),
            pl.BlockSpec((tm, LANES), lambda bi, j: (j, 0)),
            pl.BlockSpec((tm, LANES), lambda bi, j: (j, 0)),
            pl.BlockSpec((LANES, tm), lambda bi, j: (0, j)),
            pl.BlockSpec((LANES, tm), lambda bi, j: (0, j)),
        ],
        out_specs=[act_t_spec, act_spec, act_t_spec, act_t_spec, act_spec, act_t_spec,
                   pl.BlockSpec((None, tm, gate_cols), lambda bi, j: (bi, j, 0)),
                   pl.BlockSpec((None, tm, LANES), lambda bi, j: (bi, j, 0)),
                   pl.BlockSpec((None, LANES, width), lambda bi, j: (bi, 0, 0))],
        out_shape=[act_t, act, act_t, act_t, act, act_t,
                   jax.ShapeDtypeStruct((b, s, gate_cols), _BF16),
                   jax.ShapeDtypeStruct((b, s, LANES), _BF16),
                   jax.ShapeDtypeStruct((b, LANES, width), _F32)],
        scratch_shapes=[pltpu.VMEM((8, LANES), _F32)],
        compiler_params=_params("arbitrary", "arbitrary"),
        name="inproj",
    )(x, g, w, wt, wf, bfp, cos, sin, cos_t, sin_t)


def _attn_body(uq_ref, uk_ref, qt_ref, k_ref, kx_ref, vt_ref, aux_ref, o_ref,
               w_ref, m_ref, l_ref, acc_ref, st_ref, p_ref, alpha_ref, *, mode, tq, tk, nb, n_full):
    hp = pl.program_id(1)
    wide = HEADS_PER_STEP * tq
    nq = w_ref.shape[0]
    feat = lax.broadcasted_iota(jnp.int32, (LANES, 1), 0)
    xr = lax.broadcasted_iota(jnp.int32, (LANES, wide), 0)
    xc = lax.broadcasted_iota(jnp.int32, (LANES, wide), 1)

    def setup(i, carry):
        qt = qt_ref[:, pl.ds(pl.multiple_of(i * tq, tq), tq)]
        zero = jnp.zeros_like(qt)
        wq = jnp.concatenate([jnp.where(feat < HEAD_DIM, qt, zero),
                              jnp.where(feat < HEAD_DIM, zero, qt)], axis=1)
        if mode == "fox":
            head = hp * HEADS_PER_STEP + xc // tq
            lower = ((xr % N_FOX_HEADS == head) & (xr < BIAS_PIECES * N_FOX_HEADS)).astype(_BF16)
        else:
            own = (i * tq) // MOBA_BLOCK
            gate = jnp.dot(aux_ref[...].astype(_BF16), wq, preferred_element_type=_F32)[0:nb]
            blk = lax.broadcasted_iota(jnp.int32, (nb, wide), 0)
            gate = jnp.where(blk < own, gate, -jnp.inf)
            keep = blk == own
            for _ in range(MOBA_TOPK):
                mx = jnp.max(gate, axis=0, keepdims=True)
                cand = jnp.where((gate == mx) & (mx > -jnp.inf), blk, nb)
                pick = blk == jnp.min(cand, axis=0, keepdims=True)
                keep = keep | pick
                gate = jnp.where(pick, -jnp.inf, gate)
            lower = jnp.where(keep, 0.0, NEG_BIG)
            if nb < LANES:
                lower = jnp.concatenate([lower, jnp.zeros((LANES - nb, wide), _F32)], axis=0)
            lower = lower.astype(_BF16)
        w_ref[i] = jnp.concatenate([wq, lower], axis=0)
        m_ref[i] = jnp.full(m_ref.shape[1:], NEG_BIG, _F32)
        l_ref[i] = jnp.zeros(l_ref.shape[1:], _F32)
        acc_ref[i] = jnp.zeros(acc_ref.shape[1:], _F32)
        return carry

    lax.fori_loop(0, nq, setup, 0)

    def sweep(n_units, unit, causal):
        def scores(t, slot):
            q, kb = unit(t)
            start = pl.multiple_of(kb * tk, tk)
            ka = jnp.concatenate([k_ref[pl.ds(start, tk), :], kx_ref[pl.ds(start, tk), :]], axis=1)
            st = jnp.dot(ka, w_ref[q], preferred_element_type=_F32)
            if causal:
                key = start + lax.broadcasted_iota(jnp.int32, (tk, wide), 0)
                qry = q * tq + lax.broadcasted_iota(jnp.int32, (tk, wide), 1) % tq
                st = jnp.where(key <= qry, st, NEG_BIG)
            st_ref[slot] = st

        def softmax(t, slot):
            q, _ = unit(t)
            rows = SOFTMAX_ROWS
            m_prev = m_ref[q][0:1, :]
            part = None
            for r in range(0, tk, rows):
                blk = st_ref[slot, r:r + rows, :].reshape(rows // 8, 8, wide)
                cur = jnp.max(blk, axis=0)
                part = cur if part is None else jnp.maximum(part, cur)
            m_new = jnp.maximum(m_prev, jnp.max(part, axis=0, keepdims=True))
            for r in range(0, tk, rows):
                p_ref[slot, r:r + rows, :] = jnp.exp2(st_ref[slot, r:r + rows, :] - m_new).astype(_BF16)
            m_ref[q] = jnp.broadcast_to(m_new, m_ref.shape[1:])
            alpha_ref[slot] = jnp.broadcast_to(jnp.exp2(m_prev - m_new), alpha_ref.shape[1:])

        def values(t, slot):
            q, kb = unit(t)
            start = pl.multiple_of(kb * tk, tk)
            ones = jnp.ones((ONES_ROWS, tk), _BF16)
            for h in range(HEADS_PER_STEP):
                cols = slice(h * tq, (h + 1) * tq)
                feats = slice(h * HEAD_DIM, (h + 1) * HEAD_DIM)
                lhs = jnp.concatenate([vt_ref[feats, pl.ds(start, tk)], ones], axis=0)
                pv = jnp.dot(lhs, p_ref[slot, :, cols], preferred_element_type=_F32)
                alpha = alpha_ref[slot, 0:1, cols]
                acc_ref[q, feats, :] = acc_ref[q, feats, :] * alpha + pv[0:HEAD_DIM]
                l_ref[q, :, cols] = jnp.broadcast_to(
                    alpha * l_ref[q, 0:1, cols] + pv[HEAD_DIM:HEAD_DIM + 1], (8, tq))

        def step(base, c, do_scores, do_softmax, do_values):
            u = base + c
            if do_scores:
                scores(u, c % PIPE_UNITS)
            if do_values:
                values(u - 2 * SOFTMAX_LAG, (c - 2 * SOFTMAX_LAG) % PIPE_UNITS)
            if do_softmax:
                softmax(u - SOFTMAX_LAG, (c - SOFTMAX_LAG) % PIPE_UNITS)

        for c in range(PIPE_UNITS):
            step(0, c, True, c >= SOFTMAX_LAG, c >= 2 * SOFTMAX_LAG)

        def group(j, carry):
            for c in range(PIPE_UNITS):
                step(j * PIPE_UNITS, c, True, True, True)
            return carry

        lax.fori_loop(1, n_units // PIPE_UNITS, group, 0)
        for c in range(2 * SOFTMAX_LAG):
            step(n_units, c, False, c < SOFTMAX_LAG, True)

    sweep(nq, lambda t: (t, (t * tq) // tk), True)
    if n_full:
        sweep(n_full, lambda t: (uq_ref[t], uk_ref[t]), False)

    def finish(i, carry):
        out_t = jnp.concatenate(
            [acc_ref[i, h * HEAD_DIM:(h + 1) * HEAD_DIM, :] / l_ref[i, 0:1, h * tq:(h + 1) * tq]
             for h in range(HEADS_PER_STEP)], axis=0)
        o_ref[pl.ds(pl.multiple_of(i * tq, tq), tq), :] = out_t.T.astype(o_ref.dtype)
        return carry

    lax.fori_loop(0, nq, finish, 0)


def _attention(qt, k, kx, vt, aux, *, mode):
    b, width, s = qt.shape
    tq, tk = ATTN_Q, ATTN_K
    nq = s // tq
    nb = s // MOBA_BLOCK
    wide = HEADS_PER_STEP * tq
    full_units = [(i, n) for i in range(nq) for n in range((i * tq) // tk)]
    assert nq % PIPE_UNITS == 0 and len(full_units) % PIPE_UNITS == 0 and PIPE_UNITS >= 2 * SOFTMAX_LAG
    unit_q = jnp.asarray([u[0] for u in full_units] or [0], jnp.int32)
    unit_k = jnp.asarray([u[1] for u in full_units] or [0], jnp.int32)
    if mode == "fox":
        aux_spec = pl.BlockSpec(aux.shape, lambda bi, hp, uq, uk: (0, 0))
        kx_spec = pl.BlockSpec((None, s, LANES), lambda bi, hp, uq, uk: (bi, 0, 0))
    else:
        aux_spec = pl.BlockSpec((None, LANES, LANES), lambda bi, hp, uq, uk: (bi, 0, hp))
        kx_spec = pl.BlockSpec(kx.shape, lambda bi, hp, uq, uk: (0, 0))
    return pl.pallas_call(
        functools.partial(_attn_body, mode=mode, tq=tq, tk=tk, nb=nb, n_full=len(full_units)),
        grid_spec=pltpu.PrefetchScalarGridSpec(
            num_scalar_prefetch=2,
            grid=(b, width // LANES),
            in_specs=[
                pl.BlockSpec((None, LANES, s), lambda bi, hp, uq, uk: (bi, hp, 0)),
                pl.BlockSpec((None, s, LANES), lambda bi, hp, uq, uk: (bi, 0, hp)),
                kx_spec,
                pl.BlockSpec((None, LANES, s), lambda bi, hp, uq, uk: (bi, hp, 0)),
                aux_spec,
            ],
            out_specs=pl.BlockSpec((None, s, LANES), lambda bi, hp, uq, uk: (bi, 0, hp)),
            scratch_shapes=[
                pltpu.VMEM((nq, 2 * LANES, wide), _BF16),
                pltpu.VMEM((nq, 8, wide), _F32),
                pltpu.VMEM((nq, 8, wide), _F32),
                pltpu.VMEM((nq, LANES, tq), _F32),
                pltpu.VMEM((PIPE_UNITS, tk, wide), _F32),
                pltpu.VMEM((PIPE_UNITS, tk, wide), _BF16),
                pltpu.VMEM((PIPE_UNITS, 8, wide), _F32),
            ],
        ),
        out_shape=jax.ShapeDtypeStruct((b, s, width), _BF16),
        compiler_params=_params("arbitrary", "arbitrary"),
        name=mode,
    )(unit_q, unit_k, qt, k, kx, vt, aux)


def _pack_bf16_pair(a, b):
    lo = pltpu.bitcast(a.astype(_BF16).astype(_F32), jnp.uint32) >> 16
    hi = pltpu.bitcast(b.astype(_BF16).astype(_F32), jnp.uint32) & jnp.uint32(0xFFFF0000)
    return lo | hi


def _unpack_bf16_pair(u):
    lo = pltpu.bitcast(u << 16, _F32).astype(_BF16)
    hi = pltpu.bitcast(u & jnp.uint32(0xFFFF0000), _F32).astype(_BF16)
    return jnp.concatenate([lo, hi], axis=1)


def _postattn_body(yf_ref, ym_ref, gates_ref, x_ref, wfb_ref, wmb_ref, wout_ref, g_ref, wr_ref, br_ref,
                   x1_ref, h2_ref, route_ref, counts_ref, carry_ref, *, tm, d):
    step = pl.program_id(0)
    ya = jnp.dot(yf_ref[...], wfb_ref[...], preferred_element_type=_F32)
    yb = jnp.dot(ym_ref[...], wmb_ref[...], preferred_element_type=_F32)
    mixed = gates_ref[:, 0:d].astype(_F32) * ya + gates_ref[:, d:2 * d].astype(_F32) * yb
    x1 = x_ref[...] + jnp.dot(mixed.astype(_BF16), wout_ref[...], preferred_element_type=_F32)
    x1_ref[...] = x1
    h2 = _rms(x1, g_ref[...])
    h2_ref[...] = _pack_bf16_pair(h2[:, 0:d // 2], h2[:, d // 2:d])

    logits = jnp.dot(h2.astype(_BF16), wr_ref[...], preferred_element_type=_F32) + br_ref[...]
    lane = lax.broadcasted_iota(jnp.int32, (tm, LANES), 1)
    is_group = (lane >= N_EXPERTS) & (lane < N_EXPERTS + N_GROUPS)
    gl = jnp.where(is_group, logits, -jnp.inf)
    ge = jnp.exp(gl - jnp.max(gl, axis=-1, keepdims=True))
    gp = ge / jnp.sum(ge, axis=-1, keepdims=True)
    g_top = jnp.max(gp, axis=-1, keepdims=True)
    g_idx = jnp.min(jnp.where(is_group & (gp == g_top), lane, 2 * LANES), axis=-1,
                    keepdims=True) - N_EXPERTS
    in_group = (lane < N_EXPERTS) & ((lane // EXPERTS_PER_GROUP) == g_idx)
    fl = jnp.where(in_group, logits, -jnp.inf)
    f1 = jnp.max(fl, axis=-1, keepdims=True)
    e1 = jnp.min(jnp.where(fl == f1, lane, 2 * LANES), axis=-1, keepdims=True)
    fl2 = jnp.where(lane == e1, -jnp.inf, fl)
    f2 = jnp.max(fl2, axis=-1, keepdims=True)
    e2 = jnp.min(jnp.where(fl2 == f2, lane, 2 * LANES), axis=-1, keepdims=True)
    t2 = jnp.exp(f2 - f1)
    w1 = g_top * (1.0 / (1.0 + t2))
    w2 = g_top * (t2 / (1.0 + t2))

    @pl.when(step == 0)
    def _():
        carry_ref[...] = jnp.zeros_like(carry_ref)

    onehot = ((lane == e1) | (lane == e2)).astype(_F32)
    rr = lax.broadcasted_iota(jnp.int32, (tm, tm), 0)
    cc = lax.broadcasted_iota(jnp.int32, (tm, tm), 1)
    before = jnp.dot((cc < rr).astype(_BF16), onehot.astype(_BF16), preferred_element_type=_F32)
    before = before + carry_ref[0:1, :]
    r1 = jnp.sum(jnp.where(lane == e1, before, 0.0), axis=-1, keepdims=True)
    r2 = jnp.sum(jnp.where(lane == e2, before, 0.0), axis=-1, keepdims=True)
    total = carry_ref[0:1, :] + jnp.sum(onehot, axis=0, keepdims=True)
    carry_ref[...] = jnp.broadcast_to(total, carry_ref.shape)
    counts_ref[...] = jnp.broadcast_to(total, counts_ref.shape)

    slab = jnp.where(lane == 0, w1, 0.0)
    slab = jnp.where(lane == 1, w2, slab)
    slab = jnp.where(lane == 2, e1.astype(_F32), slab)
    slab = jnp.where(lane == 3, e2.astype(_F32), slab)
    slab = jnp.where(lane == 4, r1, slab)
    slab = jnp.where(lane == 5, r2, slab)
    route_ref[...] = slab


def _postattn(yf, ym, gates, x, wfb, wmb, wout, g, wr, br, *, tm):
    t, d = x.shape
    width = yf.shape[1]
    row = lambda cols: pl.BlockSpec((tm, cols), lambda i: (i, 0))
    return pl.pallas_call(
        functools.partial(_postattn_body, tm=tm, d=d),
        grid=(t // tm,),
        in_specs=[row(width), row(width), row(2 * d), row(d),
                  _const_spec(wfb.shape), _const_spec(wmb.shape), _const_spec(wout.shape),
                  _const_spec((1, d)), _const_spec(wr.shape), _const_spec(br.shape)],
        out_specs=[row(d), row(d // 2), row(LANES), _const_spec((8, LANES))],
        out_shape=[jax.ShapeDtypeStruct((t, d), _F32),
                   jax.ShapeDtypeStruct((t, d // 2), jnp.uint32),
                   jax.ShapeDtypeStruct((t, LANES), _F32),
                   jax.ShapeDtypeStruct((8, LANES), _F32)],
        scratch_shapes=[pltpu.VMEM((8, LANES), _F32)],
        compiler_params=_params("arbitrary"),
        name="postattn",
    )(yf, ym, gates, x, wfb, wmb, wout, g, wr, br)


def _dispatch_body(dest_ref, h2_ref, xs_in_ref, xs_ref, sem, *, tm):
    del xs_in_ref

    def row_copy(r, k):
        return pltpu.make_async_copy(h2_ref.at[pl.ds(r, 1)],
                                     xs_ref.at[pl.ds(dest_ref[2 * r + k], 1)], sem)

    def start(r, carry):
        row_copy(r, 0).start()
        row_copy(r, 1).start()
        return carry

    lax.fori_loop(0, tm, start, 0)
    for k in range(2):
        pltpu.make_async_copy(h2_ref, xs_ref.at[pl.ds(0, tm)], sem).wait()


def _dispatch(dest_flat, h2p, xs0, *, tm):
    t, cols = h2p.shape
    return pl.pallas_call(
        functools.partial(_dispatch_body, tm=tm),
        grid=(t // tm,),
        in_specs=[pl.BlockSpec((2 * tm,), lambda i: (i,), memory_space=pltpu.SMEM),
                  pl.BlockSpec((tm, cols), lambda i: (i, 0)),
                  pl.BlockSpec(memory_space=pl.ANY)],
        out_specs=pl.BlockSpec(memory_space=pl.ANY),
        out_shape=jax.ShapeDtypeStruct(xs0.shape, xs0.dtype),
        scratch_shapes=[pltpu.SemaphoreType.DMA(())],
        input_output_aliases={2: 0},
        compiler_params=_params("arbitrary"),
        name="dispatch",
    )(dest_flat, h2p, xs0)


def _experts_body(be_ref, nused_ref, xs_ref, wg_ref, wu_ref, wd_ref, out_ref):
    used = pl.program_id(0) < nused_ref[0]

    @pl.when(jnp.logical_not(used))
    def _():
        out_ref[...] = jnp.zeros_like(out_ref)

    @pl.when(used)
    def _():
        rows = _unpack_bf16_pair(xs_ref[...])
        gate = jnp.dot(rows, wg_ref[...], preferred_element_type=_F32)
        up = jnp.dot(rows, wu_ref[...], preferred_element_type=_F32)
        act = (gate * jax.nn.sigmoid(gate)) * up
        out_ref[...] = jnp.dot(act.astype(_BF16), wd_ref[...], preferred_element_type=_F32)


def _experts(block_expert, n_used, xs, wg, wu, wd, *, te):
    n_rows, half = xs.shape
    _, d, de = wg.shape

    def blk(j, be, nu):
        return jnp.minimum(j, nu[0] - 1)

    return pl.pallas_call(
        _experts_body,
        grid_spec=pltpu.PrefetchScalarGridSpec(
            num_scalar_prefetch=2,
            grid=(n_rows // te,),
            in_specs=[pl.BlockSpec((te, half), lambda j, be, nu: (blk(j, be, nu), 0)),
                      pl.BlockSpec((None, d, de), lambda j, be, nu: (be[blk(j, be, nu)], 0, 0)),
                      pl.BlockSpec((None, d, de), lambda j, be, nu: (be[blk(j, be, nu)], 0, 0)),
                      pl.BlockSpec((None, de, d), lambda j, be, nu: (be[blk(j, be, nu)], 0, 0))],
            out_specs=pl.BlockSpec((te, d), lambda j, be, nu: (j, 0)),
        ),
        out_shape=jax.ShapeDtypeStruct((n_rows, d), _F32),
        compiler_params=_params("arbitrary"),
        name="experts",
    )(block_expert, n_used, xs, wg, wu, wd)


def _final_body(dest_ref, x1_ref, route_ref, p_ref, rows_ref, gp_ref, wpg_ref, wpp_ref, gf_ref,
                o_ref, buf_ref, sem, *, tm):
    def row_copy(r, k):
        return pltpu.make_async_copy(rows_ref.at[pl.ds(dest_ref[2 * r + k], 1)],
                                     buf_ref.at[k, pl.ds(r, 1)], sem)

    def start(r, carry):
        row_copy(r, 0).start()
        row_copy(r, 1).start()
        return carry

    lax.fori_loop(0, tm, start, 0)
    emb = jnp.dot(p_ref[...].astype(_BF16), wpp_ref[...], preferred_element_type=_F32)
    for k in range(2):
        pltpu.make_async_copy(rows_ref.at[pl.ds(0, tm)], buf_ref.at[k], sem).wait()

    route = route_ref[...]
    x2 = x1_ref[...] + (buf_ref[0] * route[:, 0:1] + buf_ref[1] * route[:, 1:2])
    gate = jax.nn.sigmoid(jnp.dot(_rms(x2, gp_ref[...]).astype(_BF16), wpg_ref[...],
                                  preferred_element_type=_F32))
    o_ref[...] = _rms(x2 + gate * emb, gf_ref[...])


def _final(dest_flat, x1, route, p, rows, gp, wpg, wpp, gf, *, tm):
    t, d = x1.shape
    ple = p.shape[1]
    row = lambda cols: pl.BlockSpec((tm, cols), lambda i: (i, 0))
    return pl.pallas_call(
        functools.partial(_final_body, tm=tm),
        grid=(t // tm,),
        in_specs=[pl.BlockSpec((2 * tm,), lambda i: (i,), memory_space=pltpu.SMEM),
                  row(d), row(LANES), row(ple),
                  pl.BlockSpec(memory_space=pl.ANY),
                  _const_spec((1, d)), _const_spec(wpg.shape), _const_spec(wpp.shape),
                  _const_spec((1, d))],
        out_specs=row(d),
        out_shape=jax.ShapeDtypeStruct((t, d), _F32),
        scratch_shapes=[pltpu.VMEM((2, tm, d), _F32), pltpu.SemaphoreType.DMA(())],
        compiler_params=_params("arbitrary"),
        name="final",
    )(dest_flat, x1, route, p, rows, gp, wpg, wpp, gf)


def _rope_tables(s):
    half = HEAD_DIM // 2
    inv = 1.0 / (ROPE_THETA ** (jnp.arange(0, HEAD_DIM, 2, dtype=_F32) / HEAD_DIM))
    ang = jnp.arange(s, dtype=_F32)[:, None] * inv[None, :]
    cos, sin = jnp.cos(ang), jnp.sin(ang)
    reps = LANES // half
    cos_t = jnp.tile(cos, (1, reps))
    sin_t = jnp.tile(jnp.concatenate([-sin, sin], axis=1), (1, reps // 2))
    return cos_t, sin_t


def _layer(x, p, attn_norm, w_in, b_forget, w_fox_branch, w_moba_branch, w_out, moe_norm,
           w_group, b_group, w_fine, b_fine, w_gate, w_up, w_down, ple_norm, w_ple_gate, w_ple_proj,
           final_gain):
    b, s, d = x.shape
    t = b * s
    width = N_FOX_HEADS * HEAD_DIM
    assert N_MOBA_HEADS * HEAD_DIM == width and MOBA_BLOCK % ATTN_Q == 0 and ATTN_K % MOBA_BLOCK == 0
    assert BIAS_PIECES * N_FOX_HEADS <= LANES and s % ATTN_K == 0
    assert s % ROW_TILE == 0 and ROW_TILE % MOBA_BLOCK == 0 and s // MOBA_BLOCK <= LANES
    assert t % FINAL_TILE == 0 and N_EXPERTS + N_GROUPS <= LANES

    qkv_cols = 6 * width
    f_cols = N_FOX_HEADS
    chunk = lambda c: w_in[:, c * width:(c + 1) * width]
    w_rows = jnp.concatenate([chunk(1), chunk(4), w_in[:, qkv_cols + f_cols:]], axis=1).astype(_BF16)
    w_cols = jnp.concatenate([chunk(0), chunk(2), chunk(3), chunk(5)], axis=1).T.astype(_BF16)
    wf = jnp.zeros((d, LANES), _BF16).at[:, :f_cols].set(w_in[:, qkv_cols:qkv_cols + f_cols].astype(_BF16))
    bfp = jnp.zeros((1, LANES), _F32).at[0, :f_cols].set(b_forget.astype(_F32))
    cos, sin = _rope_tables(s)

    qft, kf, vft, qmt, km, vmt, gates, cb, kmean = _inproj(
        x, attn_norm.reshape(1, d), w_rows, w_cols, wf, bfp, cos, sin, cos.T, sin.T, tm=ROW_TILE)
    y_fox = _attention(qft, kf, cb, vft, jnp.zeros((8, LANES), _F32), mode="fox")
    block_of_key = jnp.arange(s, dtype=jnp.int32)[:, None] // MOBA_BLOCK
    block_onehot = (block_of_key == jnp.arange(LANES, dtype=jnp.int32)[None, :]).astype(_BF16)
    y_moba = _attention(qmt, km, block_onehot, vmt, kmean, mode="moba")

    wr = jnp.zeros((d, LANES), _BF16)
    wr = wr.at[:, :N_EXPERTS].set(w_fine.astype(_BF16))
    wr = wr.at[:, N_EXPERTS:N_EXPERTS + N_GROUPS].set(w_group.astype(_BF16))
    br = jnp.zeros((1, LANES), _F32)
    br = br.at[0, :N_EXPERTS].set(b_fine.astype(_F32))
    br = br.at[0, N_EXPERTS:N_EXPERTS + N_GROUPS].set(b_group.astype(_F32))
    x1, h2p, route, counts = _postattn(
        y_fox.reshape(t, width), y_moba.reshape(t, width), gates.reshape(t, 2 * d), x.reshape(t, d),
        w_fox_branch.astype(_BF16), w_moba_branch.astype(_BF16), w_out.astype(_BF16),
        moe_norm.reshape(1, d), wr, br, tm=ROW_TILE)

    te = EXPERT_TILE
    n_blk = (2 * t) // te + N_EXPERTS
    counts = counts[0, :N_EXPERTS].astype(jnp.int32)
    blocks_per = (counts + te - 1) // te
    block_end = jnp.cumsum(blocks_per)
    row_start = (block_end - blocks_per) * te
    experts = route[:, 2:4].astype(jnp.int32)
    dest = (row_start[experts] + route[:, 4:6].astype(jnp.int32)).reshape(-1)
    n_used = block_end[-1:].astype(jnp.int32)
    block_ids = jnp.arange(n_blk, dtype=jnp.int32)
    block_expert = jnp.minimum(
        jnp.sum((block_ids[:, None] >= block_end[None, :]).astype(jnp.int32), axis=1), N_EXPERTS - 1)

    xs = _dispatch(dest, h2p, jnp.zeros((n_blk * te, d // 2), jnp.uint32), tm=ROW_TILE)
    rows = _experts(block_expert, n_used, xs, w_gate.astype(_BF16), w_up.astype(_BF16),
                    w_down.astype(_BF16), te=te)
    out = _final(dest, x1, route, p.reshape(t, -1), rows, ple_norm.reshape(1, d),
                 w_ple_gate.astype(_BF16), w_ple_proj.astype(_BF16), final_gain.reshape(1, d),
                 tm=FINAL_TILE)
    return out.reshape(b, s, d)


def kernel(x, p, attn_norm, w_in, b_forget, w_fox_branch, w_moba_branch, w_out, moe_norm, w_group,
           b_group, w_fine, b_fine, w_gate, w_up, w_down, ple_norm, w_ple_gate, w_ple_proj, final_norm):
    depth = p.shape[0]
    assert depth == 1, "the final norm is fused into the last layer's kernel"
    i = 0
    return _layer(x, p[i], attn_norm[i], w_in[i], b_forget[i], w_fox_branch[i], w_moba_branch[i],
                  w_out[i], moe_norm[i], w_group[i], b_group[i], w_fine[i], b_fine[i], w_gate[i],
                  w_up[i], w_down[i], ple_norm[i], w_ple_gate[i], w_ple_proj[i], final_norm)
```

```python
import functools

import jax
import jax.numpy as jnp
from jax import lax
from jax.experimental import pallas as pl
from jax.experimental.pallas import tpu as pltpu

HEAD_DIM = 64
N_FOX_HEADS = 8
N_MOBA_HEADS = 8
MOBA_BLOCK = 256
MOBA_TOPK = 3
ROPE_THETA = 10000.0
N_GROUPS = 4
EXPERTS_PER_GROUP = 8
N_EXPERTS = N_GROUPS * EXPERTS_PER_GROUP
RMS_EPS = 1e-6

LANES = 128
HEADS_PER_STEP = LANES // HEAD_DIM
ATTN_Q = 256
ATTN_K = 256
PIPE_UNITS = 8
SOFTMAX_LAG = 4
ONES_ROWS = 16
SOFTMAX_ROWS = 64
BIAS_PIECES = 3
ROW_TILE = 512
DMA_UNROLL = 8
FINAL_TILE = 256
EXPERT_TILE = 512
LOG2_E = 1.4426950408889634
NEG_BIG = -1e30
VMEM_LIMIT = 48 * 1024 * 1024

_BF16 = jnp.bfloat16
_F32 = jnp.float32


def _params(*sem):
    return pltpu.CompilerParams(dimension_semantics=sem, vmem_limit_bytes=VMEM_LIMIT)


def _rms(x, g):
    return x * lax.rsqrt(jnp.mean(x * x, axis=-1, keepdims=True) + RMS_EPS) * g


def _const_spec(shape):
    return pl.BlockSpec(shape, lambda *_: (0,) * len(shape))


def _inproj_body(x_ref, g_ref, w_ref, wt_ref, wf_ref, bf_ref, cos_ref, sin_ref, cost_ref, sint_ref,
                 qft_ref, kf_ref, vft_ref, qmt_ref, km_ref, vmt_ref, gates_ref, cb_ref, kmean_ref,
                 carry_ref, *, tm, width, gate_cols):
    j = pl.program_id(1)
    hb = _rms(x_ref[...], g_ref[...]).astype(_BF16)
    reps = width // LANES
    half = HEAD_DIM // 2

    def proj(c):
        return jnp.dot(hb, w_ref[:, c * width:(c + 1) * width], preferred_element_type=_F32)

    def proj_t(c):
        return lax.dot_general(wt_ref[c * width:(c + 1) * width, :], hb, (((1,), (1,)), ((), ())),
                               preferred_element_type=_F32)

    def rope(t):
        cos = jnp.concatenate([cos_ref[...]] * reps, axis=1)
        sin = jnp.concatenate([sin_ref[...]] * reps, axis=1)
        first = (lax.broadcasted_iota(jnp.int32, (1, width), 1) % HEAD_DIM) < half
        partner = jnp.where(first, pltpu.roll(t, width - half, 1), pltpu.roll(t, half, 1))
        return t * cos + partner * sin

    def rope_t(t):
        cos = jnp.concatenate([cost_ref[...]] * reps, axis=0)
        sin = jnp.concatenate([sint_ref[...]] * reps, axis=0)
        first = (lax.broadcasted_iota(jnp.int32, (width, 1), 0) % HEAD_DIM) < half
        partner = jnp.where(first, pltpu.roll(t, width - half, 0), pltpu.roll(t, half, 0))
        return t * cos + partner * sin

    scale = HEAD_DIM ** -0.5 * LOG2_E
    qft_ref[...] = (proj_t(0) * scale).astype(_BF16)
    vft_ref[...] = proj_t(1).astype(_BF16)
    qmt_ref[...] = (rope_t(proj_t(2)) * scale).astype(_BF16)
    vmt_ref[...] = proj_t(3).astype(_BF16)
    kf_ref[...] = proj(0).astype(_BF16)
    km = rope(proj(1))
    km_ref[...] = km.astype(_BF16)

    @pl.when(j == 0)
    def _():
        kmean_ref[...] = jnp.zeros_like(kmean_ref)
        carry_ref[...] = jnp.zeros_like(carry_ref)

    blocks = tm // MOBA_BLOCK
    for b in range(blocks):
        kmean_ref[pl.ds(j * blocks + b, 1), :] = jnp.mean(
            km[b * MOBA_BLOCK:(b + 1) * MOBA_BLOCK], axis=0, keepdims=True)

    for c in range(gate_cols // width):
        g = jnp.dot(hb, w_ref[:, (2 + c) * width:(3 + c) * width],
                    preferred_element_type=_F32)
        gates_ref[:, c * width:(c + 1) * width] = jax.nn.sigmoid(g).astype(_BF16)

    z = jnp.dot(hb, wf_ref[...], preferred_element_type=_F32) + bf_ref[...]
    log_f = jnp.minimum(z, 0.0) - jnp.log1p(jnp.exp(-jnp.abs(z)))
    r = lax.broadcasted_iota(jnp.int32, (tm, tm), 0)
    c = lax.broadcasted_iota(jnp.int32, (tm, tm), 1)
    cum = jnp.dot((c <= r).astype(_F32), log_f, preferred_element_type=_F32,
                  precision=lax.Precision.HIGHEST) + carry_ref[0:1, :]
    carry_ref[...] = jnp.broadcast_to(cum[tm - 1:tm, :], carry_ref.shape)

    head_lane = lax.broadcasted_iota(jnp.int32, (1, LANES), 1) < N_FOX_HEADS
    rest = jnp.where(head_lane, -cum * LOG2_E, 0.0)
    placed = jnp.zeros((tm, LANES), _F32)
    for piece in range(BIAS_PIECES):
        part = rest.astype(_BF16).astype(_F32)
        rest = rest - part
        placed = placed + (pltpu.roll(part, N_FOX_HEADS * piece, 1) if piece else part)
    cb_ref[...] = placed.astype(_BF16)


def _inproj(x, g, w, wt, wf, bfp, cos, sin, cos_t, sin_t, *, tm):
    b, s, d = x.shape
    width = N_FOX_HEADS * HEAD_DIM
    gate_cols = w.shape[1] - 2 * width
    act = jax.ShapeDtypeStruct((b, s, width), _BF16)
    act_t = jax.ShapeDtypeStruct((b, width, s), _BF16)
    act_spec = pl.BlockSpec((None, tm, width), lambda bi, j: (bi, j, 0))
    act_t_spec = pl.BlockSpec((None, width, tm), lambda bi, j: (bi, 0, j))
    return pl.pallas_call(
        functools.partial(_inproj_body, tm=tm, width=width, gate_cols=gate_cols),
        grid=(b, s // tm),
        in_specs=[
            pl.BlockSpec((None, tm, d), lambda bi, j: (bi, j, 0)),
            _const_spec((1, d)),
            _const_spec(w.shape),
            _const_spec(wt.shape),
            _const_spec(wf.shape),
            _const_spec(bfp.shape),
            pl.BlockSpec((tm, LANES), lambda bi, j: (j, 0)),
            pl.BlockSpec((tm, LANES), lambda bi, j: (j, 0)),
            pl.BlockSpec((LANES, tm), lambda bi, j: (0, j)),
            pl.BlockSpec((LANES, tm), lambda bi, j: (0, j)),
        ],
        out_specs=[act_t_spec, act_spec, act_t_spec, act_t_spec, act_spec, act_t_spec,
                   pl.BlockSpec((None, tm, gate_cols), lambda bi, j: (bi, j, 0)),
                   pl.BlockSpec((None, tm, LANES), lambda bi, j: (bi, j, 0)),
                   pl.BlockSpec((None, LANES, width), lambda bi, j: (bi, 0, 0))],
        out_shape=[act_t, act, act_t, act_t, act, act_t,
                   jax.ShapeDtypeStruct((b, s, gate_cols), _BF16),
                   jax.ShapeDtypeStruct((b, s, LANES), _BF16),
                   jax.ShapeDtypeStruct((b, LANES, width), _F32)],
        scratch_shapes=[pltpu.VMEM((8, LANES), _F32)],
        compiler_params=_params("arbitrary", "arbitrary"),
        name="inproj",
    )(x, g, w, wt, wf, bfp, cos, sin, cos_t, sin_t)


def _attn_body(uq_ref, uk_ref, qt_ref, k_ref, kx_ref, vt_ref, aux_ref, o_ref,
               w_ref, m_ref, l_ref, acc_ref, st_ref, p_ref, alpha_ref, *, mode, tq, tk, nb, n_full):
    hp = pl.program_id(1)
    wide = HEADS_PER_STEP * tq
    nq = w_ref.shape[0]
    feat = lax.broadcasted_iota(jnp.int32, (LANES, 1), 0)
    xr = lax.broadcasted_iota(jnp.int32, (LANES, wide), 0)
    xc = lax.broadcasted_iota(jnp.int32, (LANES, wide), 1)

    def setup(i, carry):
        qt = qt_ref[:, pl.ds(pl.multiple_of(i * tq, tq), tq)]
        zero = jnp.zeros_like(qt)
        wq = jnp.concatenate([jnp.where(feat < HEAD_DIM, qt, zero),
                              jnp.where(feat < HEAD_DIM, zero, qt)], axis=1)
        if mode == "fox":
            head = hp * HEADS_PER_STEP + xc // tq
            lower = ((xr % N_FOX_HEADS == head) & (xr < BIAS_PIECES * N_FOX_HEADS)).astype(_BF16)
        else:
            own = (i * tq) // MOBA_BLOCK
            gate = jnp.dot(aux_ref[...].astype(_BF16), wq, preferred_element_type=_F32)[0:nb]
            blk = lax.broadcasted_iota(jnp.int32, (nb, wide), 0)
            gate = jnp.where(blk < own, gate, -jnp.inf)
            keep = blk == own
            for _ in range(MOBA_TOPK):
                mx = jnp.max(gate, axis=0, keepdims=True)
                cand = jnp.where((gate == mx) & (mx > -jnp.inf), blk, nb)
                pick = blk == jnp.min(cand, axis=0, keepdims=True)
                keep = keep | pick
                gate = jnp.where(pick, -jnp.inf, gate)
            lower = jnp.where(keep, 0.0, NEG_BIG)
            if nb < LANES:
                lower = jnp.concatenate([lower, jnp.zeros((LANES - nb, wide), _F32)], axis=0)
            lower = lower.astype(_BF16)
        w_ref[i] = jnp.concatenate([wq, lower], axis=0)
        m_ref[i] = jnp.full(m_ref.shape[1:], NEG_BIG, _F32)
        l_ref[i] = jnp.zeros(l_ref.shape[1:], _F32)
        acc_ref[i] = jnp.zeros(acc_ref.shape[1:], _F32)
        return carry

    lax.fori_loop(0, nq, setup, 0)

    def sweep(n_units, unit, causal):
        def scores(t, slot):
            q, kb = unit(t)
            start = pl.multiple_of(kb * tk, tk)
            ka = jnp.concatenate([k_ref[pl.ds(start, tk), :], kx_ref[pl.ds(start, tk), :]], axis=1)
            st = jnp.dot(ka, w_ref[q], preferred_element_type=_F32)
            if causal:
                key = start + lax.broadcasted_iota(jnp.int32, (tk, wide), 0)
                qry = q * tq + lax.broadcasted_iota(jnp.int32, (tk, wide), 1) % tq
                st = jnp.where(key <= qry, st, NEG_BIG)
            st_ref[slot] = st

        def softmax(t, slot):
            q, _ = unit(t)
            rows = SOFTMAX_ROWS
            m_prev = m_ref[q][0:1, :]
            part = None
            for r in range(0, tk, rows):
                blk = st_ref[slot, r:r + rows, :].reshape(rows // 8, 8, wide)
                cur = jnp.max(blk, axis=0)
                part = cur if part is None else jnp.maximum(part, cur)
            m_new = jnp.maximum(m_prev, jnp.max(part, axis=0, keepdims=True))
            for r in range(0, tk, rows):
                p_ref[slot, r:r + rows, :] = jnp.exp2(st_ref[slot, r:r + rows, :] - m_new).astype(_BF16)
            m_ref[q] = jnp.broadcast_to(m_new, m_ref.shape[1:])
            alpha_ref[slot] = jnp.broadcast_to(jnp.exp2(m_prev - m_new), alpha_ref.shape[1:])

        def values(t, slot):
            q, kb = unit(t)
            start = pl.multiple_of(kb * tk, tk)
            ones = jnp.ones((ONES_ROWS, tk), _BF16)
            for h in range(HEADS_PER_STEP):
                cols = slice(h * tq, (h + 1) * tq)
                feats = slice(h * HEAD_DIM, (h + 1) * HEAD_DIM)
                lhs = jnp.concatenate([vt_ref[feats, pl.ds(start, tk)], ones], axis=0)
                pv = jnp.dot(lhs, p_ref[slot, :, cols], preferred_element_type=_F32)
                alpha = alpha_ref[slot, 0:1, cols]
                acc_ref[q, feats, :] = acc_ref[q, feats, :] * alpha + pv[0:HEAD_DIM]
                l_ref[q, :, cols] = jnp.broadcast_to(
                    alpha * l_ref[q, 0:1, cols] + pv[HEAD_DIM:HEAD_DIM + 1], (8, tq))

        def step(base, c, do_scores, do_softmax, do_values):
            u = base + c
            if do_scores:
                scores(u, c % PIPE_UNITS)
            if do_values:
                values(u - 2 * SOFTMAX_LAG, (c - 2 * SOFTMAX_LAG) % PIPE_UNITS)
            if do_softmax:
                softmax(u - SOFTMAX_LAG, (c - SOFTMAX_LAG) % PIPE_UNITS)

        for c in range(PIPE_UNITS):
            step(0, c, True, c >= SOFTMAX_LAG, c >= 2 * SOFTMAX_LAG)

        def group(j, carry):
            for c in range(PIPE_UNITS):
                step(j * PIPE_UNITS, c, True, True, True)
            return carry

        lax.fori_loop(1, n_units // PIPE_UNITS, group, 0)
        for c in range(2 * SOFTMAX_LAG):
            step(n_units, c, False, c < SOFTMAX_LAG, True)

    sweep(nq, lambda t: (t, (t * tq) // tk), True)
    if n_full:
        sweep(n_full, lambda t: (uq_ref[t], uk_ref[t]), False)

    def finish(i, carry):
        out_t = jnp.concatenate(
            [acc_ref[i, h * HEAD_DIM:(h + 1) * HEAD_DIM, :] / l_ref[i, 0:1, h * tq:(h + 1) * tq]
             for h in range(HEADS_PER_STEP)], axis=0)
        o_ref[pl.ds(pl.multiple_of(i * tq, tq), tq), :] = out_t.T.astype(o_ref.dtype)
        return carry

    lax.fori_loop(0, nq, finish, 0, unroll=2)


def _attention(qt, k, kx, vt, aux, *, mode):
    b, width, s = qt.shape
    tq, tk = ATTN_Q, ATTN_K
    nq = s // tq
    nb = s // MOBA_BLOCK
    wide = HEADS_PER_STEP * tq
    full_units = [(i, n) for i in range(nq) for n in range((i * tq) // tk)]
    assert nq % PIPE_UNITS == 0 and len(full_units) % PIPE_UNITS == 0 and PIPE_UNITS >= 2 * SOFTMAX_LAG
    unit_q = jnp.asarray([u[0] for u in full_units] or [0], jnp.int32)
    unit_k = jnp.asarray([u[1] for u in full_units] or [0], jnp.int32)
    if mode == "fox":
        aux_spec = pl.BlockSpec(aux.shape, lambda bi, hp, uq, uk: (0, 0))
        kx_spec = pl.BlockSpec((None, s, LANES), lambda bi, hp, uq, uk: (bi, 0, 0))
    else:
        aux_spec = pl.BlockSpec((None, LANES, LANES), lambda bi, hp, uq, uk: (bi, 0, hp))
        kx_spec = pl.BlockSpec(kx.shape, lambda bi, hp, uq, uk: (0, 0))
    return pl.pallas_call(
        functools.partial(_attn_body, mode=mode, tq=tq, tk=tk, nb=nb, n_full=len(full_units)),
        grid_spec=pltpu.PrefetchScalarGridSpec(
            num_scalar_prefetch=2,
            grid=(b, width // LANES),
            in_specs=[
                pl.BlockSpec((None, LANES, s), lambda bi, hp, uq, uk: (bi, hp, 0)),
                pl.BlockSpec((None, s, LANES), lambda bi, hp, uq, uk: (bi, 0, hp)),
                kx_spec,
                pl.BlockSpec((None, LANES, s), lambda bi, hp, uq, uk: (bi, hp, 0)),
                aux_spec,
            ],
            out_specs=pl.BlockSpec((None, s, LANES), lambda bi, hp, uq, uk: (bi, 0, hp)),
            scratch_shapes=[
                pltpu.VMEM((nq, 2 * LANES, wide), _BF16),
                pltpu.VMEM((nq, 8, wide), _F32),
                pltpu.VMEM((nq, 8, wide), _F32),
                pltpu.VMEM((nq, LANES, tq), _F32),
                pltpu.VMEM((PIPE_UNITS, tk, wide), _F32),
                pltpu.VMEM((PIPE_UNITS, tk, wide), _BF16),
                pltpu.VMEM((PIPE_UNITS, 8, wide), _F32),
            ],
        ),
        out_shape=jax.ShapeDtypeStruct((b, s, width), _BF16),
        compiler_params=_params("arbitrary", "arbitrary"),
        name=mode,
    )(unit_q, unit_k, qt, k, kx, vt, aux)


def _pack_bf16_pair(a, b):
    lo = pltpu.bitcast(a.astype(_BF16).astype(_F32), jnp.uint32) >> 16
    hi = pltpu.bitcast(b.astype(_BF16).astype(_F32), jnp.uint32) & jnp.uint32(0xFFFF0000)
    return lo | hi


def _unpack_bf16_pair(u):
    lo = pltpu.bitcast(u << 16, _F32).astype(_BF16)
    hi = pltpu.bitcast(u & jnp.uint32(0xFFFF0000), _F32).astype(_BF16)
    return jnp.concatenate([lo, hi], axis=1)


def _postattn_body(yf_ref, ym_ref, gates_ref, x_ref, wfb_ref, wmb_ref, wout_ref, g_ref, wr_ref, br_ref,
                   x1_ref, h2_ref, route_ref, counts_ref, carry_ref, *, tm, d):
    step = pl.program_id(0)
    ya = jnp.dot(yf_ref[...], wfb_ref[...], preferred_element_type=_F32)
    yb = jnp.dot(ym_ref[...], wmb_ref[...], preferred_element_type=_F32)
    mixed = gates_ref[:, 0:d].astype(_F32) * ya + gates_ref[:, d:2 * d].astype(_F32) * yb
    x1 = x_ref[...] + jnp.dot(mixed.astype(_BF16), wout_ref[...], preferred_element_type=_F32)
    x1_ref[...] = x1
    h2 = _rms(x1, g_ref[...])
    h2_ref[...] = _pack_bf16_pair(h2[:, 0:d // 2], h2[:, d // 2:d])

    logits = jnp.dot(h2.astype(_BF16), wr_ref[...], preferred_element_type=_F32) + br_ref[...]
    lane = lax.broadcasted_iota(jnp.int32, (tm, LANES), 1)
    is_group = (lane >= N_EXPERTS) & (lane < N_EXPERTS + N_GROUPS)
    gl = jnp.where(is_group, logits, -jnp.inf)
    ge = jnp.exp(gl - jnp.max(gl, axis=-1, keepdims=True))
    gp = ge / jnp.sum(ge, axis=-1, keepdims=True)
    g_top = jnp.max(gp, axis=-1, keepdims=True)
    g_idx = jnp.min(jnp.where(is_group & (gp == g_top), lane, 2 * LANES), axis=-1,
                    keepdims=True) - N_EXPERTS
    in_group = (lane < N_EXPERTS) & ((lane // EXPERTS_PER_GROUP) == g_idx)
    fl = jnp.where(in_group, logits, -jnp.inf)
    f1 = jnp.max(fl, axis=-1, keepdims=True)
    e1 = jnp.min(jnp.where(fl == f1, lane, 2 * LANES), axis=-1, keepdims=True)
    fl2 = jnp.where(lane == e1, -jnp.inf, fl)
    f2 = jnp.max(fl2, axis=-1, keepdims=True)
    e2 = jnp.min(jnp.where(fl2 == f2, lane, 2 * LANES), axis=-1, keepdims=True)
    t2 = jnp.exp(f2 - f1)
    w1 = g_top * (1.0 / (1.0 + t2))
    w2 = g_top * (t2 / (1.0 + t2))

    @pl.when(step == 0)
    def _():
        carry_ref[...] = jnp.zeros_like(carry_ref)

    onehot = ((lane == e1) | (lane == e2)).astype(_F32)
    rr = lax.broadcasted_iota(jnp.int32, (tm, tm), 0)
    cc = lax.broadcasted_iota(jnp.int32, (tm, tm), 1)
    before = jnp.dot((cc < rr).astype(_BF16), onehot.astype(_BF16), preferred_element_type=_F32)
    before = before + carry_ref[0:1, :]
    r1 = jnp.sum(jnp.where(lane == e1, before, 0.0), axis=-1, keepdims=True)
    r2 = jnp.sum(jnp.where(lane == e2, before, 0.0), axis=-1, keepdims=True)
    total = carry_ref[0:1, :] + jnp.sum(onehot, axis=0, keepdims=True)
    carry_ref[...] = jnp.broadcast_to(total, carry_ref.shape)
    counts_ref[...] = jnp.broadcast_to(total, counts_ref.shape)

    slab = jnp.where(lane == 0, w1, 0.0)
    slab = jnp.where(lane == 1, w2, slab)
    slab = jnp.where(lane == 2, e1.astype(_F32), slab)
    slab = jnp.where(lane == 3, e2.astype(_F32), slab)
    slab = jnp.where(lane == 4, r1, slab)
    slab = jnp.where(lane == 5, r2, slab)
    route_ref[...] = slab


def _postattn(yf, ym, gates, x, wfb, wmb, wout, g, wr, br, *, tm):
    t, d = x.shape
    width = yf.shape[1]
    row = lambda cols: pl.BlockSpec((tm, cols), lambda i: (i, 0))
    return pl.pallas_call(
        functools.partial(_postattn_body, tm=tm, d=d),
        grid=(t // tm,),
        in_specs=[row(width), row(width), row(2 * d), row(d),
                  _const_spec(wfb.shape), _const_spec(wmb.shape), _const_spec(wout.shape),
                  _const_spec((1, d)), _const_spec(wr.shape), _const_spec(br.shape)],
        out_specs=[row(d), row(d // 2), row(LANES), _const_spec((8, LANES))],
        out_shape=[jax.ShapeDtypeStruct((t, d), _F32),
                   jax.ShapeDtypeStruct((t, d // 2), jnp.uint32),
                   jax.ShapeDtypeStruct((t, LANES), _F32),
                   jax.ShapeDtypeStruct((8, LANES), _F32)],
        scratch_shapes=[pltpu.VMEM((8, LANES), _F32)],
        compiler_params=_params("arbitrary"),
        name="postattn",
    )(yf, ym, gates, x, wfb, wmb, wout, g, wr, br)


def _dispatch_body(dest_ref, h2_ref, xs_in_ref, xs_ref, sem, *, tm):
    del xs_in_ref

    def row_copy(r, k):
        return pltpu.make_async_copy(h2_ref.at[pl.ds(r, 1)],
                                     xs_ref.at[pl.ds(dest_ref[2 * r + k], 1)], sem)

    def start(r, carry):
        row_copy(r, 0).start(priority=0)
        row_copy(r, 1).start(priority=1)
        return carry

    lax.fori_loop(0, tm, start, 0, unroll=DMA_UNROLL)
    for k in range(2):
        pltpu.make_async_copy(h2_ref, xs_ref.at[pl.ds(0, tm)], sem).wait()


def _dispatch(dest_flat, h2p, xs0, *, tm):
    t, cols = h2p.shape
    return pl.pallas_call(
        functools.partial(_dispatch_body, tm=tm),
        grid=(t // tm,),
        in_specs=[pl.BlockSpec((2 * tm,), lambda i: (i,), memory_space=pltpu.SMEM),
                  pl.BlockSpec((tm, cols), lambda i: (i, 0)),
                  pl.BlockSpec(memory_space=pl.ANY)],
        out_specs=pl.BlockSpec(memory_space=pl.ANY),
        out_shape=jax.ShapeDtypeStruct(xs0.shape, xs0.dtype),
        scratch_shapes=[pltpu.SemaphoreType.DMA(())],
        input_output_aliases={2: 0},
        compiler_params=_params("arbitrary"),
        name="dispatch",
    )(dest_flat, h2p, xs0)


def _experts_body(be_ref, nused_ref, xs_ref, wg_ref, wu_ref, wd_ref, out_ref):
    used = pl.program_id(0) < nused_ref[0]

    @pl.when(jnp.logical_not(used))
    def _():
        out_ref[...] = jnp.zeros_like(out_ref)

    @pl.when(used)
    def _():
        rows = _unpack_bf16_pair(xs_ref[...])
        gate = jnp.dot(rows, wg_ref[...], preferred_element_type=_F32)
        up = jnp.dot(rows, wu_ref[...], preferred_element_type=_F32)
        act = (gate * jax.nn.sigmoid(gate)) * up
        out_ref[...] = jnp.dot(act.astype(_BF16), wd_ref[...], preferred_element_type=_F32)


def _experts(block_expert, n_used, xs, wg, wu, wd, *, te):
    n_rows, half = xs.shape
    _, d, de = wg.shape

    def blk(j, be, nu):
        return jnp.minimum(j, nu[0] - 1)

    return pl.pallas_call(
        _experts_body,
        grid_spec=pltpu.PrefetchScalarGridSpec(
            num_scalar_prefetch=2,
            grid=(n_rows // te,),
            in_specs=[pl.BlockSpec((te, half), lambda j, be, nu: (blk(j, be, nu), 0)),
                      pl.BlockSpec((None, d, de), lambda j, be, nu: (be[blk(j, be, nu)], 0, 0)),
                      pl.BlockSpec((None, d, de), lambda j, be, nu: (be[blk(j, be, nu)], 0, 0)),
                      pl.BlockSpec((None, de, d), lambda j, be, nu: (be[blk(j, be, nu)], 0, 0))],
            out_specs=pl.BlockSpec((te, d), lambda j, be, nu: (j, 0)),
        ),
        out_shape=jax.ShapeDtypeStruct((n_rows, d), _F32),
        compiler_params=_params("arbitrary"),
        name="experts",
    )(block_expert, n_used, xs, wg, wu, wd)


def _final_body(dest_ref, dest_next_ref, x1_ref, route_ref, p_ref, rows_ref, gp_ref, wpg_ref, wpp_ref,
                gf_ref, o_ref, buf_ref, sems, *, tm):
    i = pl.program_id(0)
    slot = i % 2

    def gather(idx_ref, to_slot):
        def start(r, carry):
            for k in range(2):
                pltpu.make_async_copy(rows_ref.at[pl.ds(idx_ref[2 * r + k], 1)],
                                      buf_ref.at[to_slot, k, pl.ds(r, 1)],
                                      sems.at[to_slot]).start(priority=k)
            return carry

        lax.fori_loop(0, tm, start, 0, unroll=DMA_UNROLL)

    @pl.when(i == 0)
    def _():
        gather(dest_ref, 0)

    @pl.when(i + 1 < pl.num_programs(0))
    def _():
        gather(dest_next_ref, 1 - slot)

    emb = jnp.dot(p_ref[...].astype(_BF16), wpp_ref[...], preferred_element_type=_F32)
    for k in range(2):
        pltpu.make_async_copy(rows_ref.at[pl.ds(0, tm)], buf_ref.at[slot, k], sems.at[slot]).wait()

    route = route_ref[...]
    x2 = x1_ref[...] + (buf_ref[slot, 0] * route[:, 0:1] + buf_ref[slot, 1] * route[:, 1:2])
    gate = jax.nn.sigmoid(jnp.dot(_rms(x2, gp_ref[...]).astype(_BF16), wpg_ref[...],
                                  preferred_element_type=_F32))
    o_ref[...] = _rms(x2 + gate * emb, gf_ref[...])


def _final(dest_flat, x1, route, p, rows, gp, wpg, wpp, gf, *, tm):
    t, d = x1.shape
    ple = p.shape[1]
    last = t // tm - 1
    row = lambda cols: pl.BlockSpec((tm, cols), lambda i: (i, 0))
    return pl.pallas_call(
        functools.partial(_final_body, tm=tm),
        grid=(t // tm,),
        in_specs=[pl.BlockSpec((2 * tm,), lambda i: (i,), memory_space=pltpu.SMEM),
                  pl.BlockSpec((2 * tm,), lambda i: (jnp.minimum(i + 1, last),), memory_space=pltpu.SMEM),
                  row(d), row(LANES), row(ple),
                  pl.BlockSpec(memory_space=pl.ANY),
                  _const_spec((1, d)), _const_spec(wpg.shape), _const_spec(wpp.shape),
                  _const_spec((1, d))],
        out_specs=row(d),
        out_shape=jax.ShapeDtypeStruct((t, d), _F32),
        scratch_shapes=[pltpu.VMEM((2, 2, tm, d), _F32), pltpu.SemaphoreType.DMA((2,))],
        compiler_params=_params("arbitrary"),
        name="final",
    )(dest_flat, dest_flat, x1, route, p, rows, gp, wpg, wpp, gf)


def _rope_tables(s):
    half = HEAD_DIM // 2
    inv = 1.0 / (ROPE_THETA ** (jnp.arange(0, HEAD_DIM, 2, dtype=_F32) / HEAD_DIM))
    ang = jnp.arange(s, dtype=_F32)[:, None] * inv[None, :]
    cos, sin = jnp.cos(ang), jnp.sin(ang)
    reps = LANES // half
    cos_t = jnp.tile(cos, (1, reps))
    sin_t = jnp.tile(jnp.concatenate([-sin, sin], axis=1), (1, reps // 2))
    return cos_t, sin_t


def _layer(x, p, attn_norm, w_in, b_forget, w_fox_branch, w_moba_branch, w_out, moe_norm,
           w_group, b_group, w_fine, b_fine, w_gate, w_up, w_down, ple_norm, w_ple_gate, w_ple_proj,
           final_gain):
    b, s, d = x.shape
    t = b * s
    width = N_FOX_HEADS * HEAD_DIM
    assert N_MOBA_HEADS * HEAD_DIM == width and MOBA_BLOCK % ATTN_Q == 0 and ATTN_K % MOBA_BLOCK == 0
    assert BIAS_PIECES * N_FOX_HEADS <= LANES and s % ATTN_K == 0
    assert s % ROW_TILE == 0 and ROW_TILE % MOBA_BLOCK == 0 and s // MOBA_BLOCK <= LANES
    assert t % FINAL_TILE == 0 and N_EXPERTS + N_GROUPS <= LANES

    qkv_cols = 6 * width
    f_cols = N_FOX_HEADS
    chunk = lambda c: w_in[:, c * width:(c + 1) * width]
    w_rows = jnp.concatenate([chunk(1), chunk(4), w_in[:, qkv_cols + f_cols:]], axis=1).astype(_BF16)
    w_cols = jnp.concatenate([chunk(0), chunk(2), chunk(3), chunk(5)], axis=1).T.astype(_BF16)
    wf = jnp.zeros((d, LANES), _BF16).at[:, :f_cols].set(w_in[:, qkv_cols:qkv_cols + f_cols].astype(_BF16))
    bfp = jnp.zeros((1, LANES), _F32).at[0, :f_cols].set(b_forget.astype(_F32))
    cos, sin = _rope_tables(s)

    qft, kf, vft, qmt, km, vmt, gates, cb, kmean = _inproj(
        x, attn_norm.reshape(1, d), w_rows, w_cols, wf, bfp, cos, sin, cos.T, sin.T, tm=ROW_TILE)
    y_fox = _attention(qft, kf, cb, vft, jnp.zeros((8, LANES), _F32), mode="fox")
    block_of_key = jnp.arange(s, dtype=jnp.int32)[:, None] // MOBA_BLOCK
    block_onehot = (block_of_key == jnp.arange(LANES, dtype=jnp.int32)[None, :]).astype(_BF16)
    y_moba = _attention(qmt, km, block_onehot, vmt, kmean, mode="moba")

    wr = jnp.zeros((d, LANES), _BF16)
    wr = wr.at[:, :N_EXPERTS].set(w_fine.astype(_BF16))
    wr = wr.at[:, N_EXPERTS:N_EXPERTS + N_GROUPS].set(w_group.astype(_BF16))
    br = jnp.zeros((1, LANES), _F32)
    br = br.at[0, :N_EXPERTS].set(b_fine.astype(_F32))
    br = br.at[0, N_EXPERTS:N_EXPERTS + N_GROUPS].set(b_group.astype(_F32))
    x1, h2p, route, counts = _postattn(
        y_fox.reshape(t, width), y_moba.reshape(t, width), gates.reshape(t, 2 * d), x.reshape(t, d),
        w_fox_branch.astype(_BF16), w_moba_branch.astype(_BF16), w_out.astype(_BF16),
        moe_norm.reshape(1, d), wr, br, tm=ROW_TILE)

    te = EXPERT_TILE
    n_blk = (2 * t) // te + N_EXPERTS
    counts = counts[0, :N_EXPERTS].astype(jnp.int32)
    blocks_per = (counts + te - 1) // te
    block_end = jnp.cumsum(blocks_per)
    row_start = (block_end - blocks_per) * te
    experts = route[:, 2:4].astype(jnp.int32)
    dest = (row_start[experts] + route[:, 4:6].astype(jnp.int32)).reshape(-1)
    n_used = block_end[-1:].astype(jnp.int32)
    block_ids = jnp.arange(n_blk, dtype=jnp.int32)
    block_expert = jnp.minimum(
        jnp.sum((block_ids[:, None] >= block_end[None, :]).astype(jnp.int32), axis=1), N_EXPERTS - 1)

    xs = _dispatch(dest, h2p, jnp.zeros((n_blk * te, d // 2), jnp.uint32), tm=ROW_TILE)
    rows = _experts(block_expert, n_used, xs, w_gate.astype(_BF16), w_up.astype(_BF16),
                    w_down.astype(_BF16), te=te)
    out = _final(dest, x1, route, p.reshape(t, -1), rows, ple_norm.reshape(1, d),
                 w_ple_gate.astype(_BF16), w_ple_proj.astype(_BF16), final_gain.reshape(1, d),
                 tm=FINAL_TILE)
    return out.reshape(b, s, d)


def kernel(x, p, attn_norm, w_in, b_forget, w_fox_branch, w_moba_branch, w_out, moe_norm, w_group,
           b_group, w_fine, b_fine, w_gate, w_up, w_down, ple_norm, w_ple_gate, w_ple_proj, final_norm):
    depth = p.shape[0]
    assert depth == 1, "the final norm is fused into the last layer's kernel"
    i = 0
    return _layer(x, p[i], attn_norm[i], w_in[i], b_forget[i], w_fox_branch[i], w_moba_branch[i],
                  w_out[i], moe_norm[i], w_group[i], b_group[i], w_fine[i], b_fine[i], w_gate[i],
                  w_up[i], w_down[i], ple_norm[i], w_ple_gate[i], w_ple_proj[i], final_norm)
```

```python
import functools

import jax
import jax.numpy as jnp
from jax import lax
from jax.experimental import pallas as pl
from jax.experimental.pallas import tpu as pltpu

HEAD_DIM = 64
N_FOX_HEADS = 8
N_MOBA_HEADS = 8
MOBA_BLOCK = 256
MOBA_TOPK = 3
ROPE_THETA = 10000.0
N_GROUPS = 4
EXPERTS_PER_GROUP = 8
N_EXPERTS = N_GROUPS * EXPERTS_PER_GROUP
RMS_EPS = 1e-6

LANES = 128
HEADS_PER_STEP = LANES // HEAD_DIM
ATTN_Q = 256
ATTN_K = 256
PIPE_UNITS = 8
SOFTMAX_LAG = 4
ONES_ROWS = 16
SOFTMAX_ROWS = 64
BIAS_PIECES = 3
ROW_TILE = 512
DMA_UNROLL = 8
FINAL_TILE = 256
EXPERT_TILE = 512
LOG2_E = 1.4426950408889634
NEG_BIG = -1e30
VMEM_LIMIT = 48 * 1024 * 1024

_BF16 = jnp.bfloat16
_F32 = jnp.float32


def _params(*sem):
    return pltpu.CompilerParams(dimension_semantics=sem, vmem_limit_bytes=VMEM_LIMIT)


def _rms(x, g):
    return x * lax.rsqrt(jnp.mean(x * x, axis=-1, keepdims=True) + RMS_EPS) * g


def _const_spec(shape):
    return pl.BlockSpec(shape, lambda *_: (0,) * len(shape))


def _inproj_body(x_ref, g_ref, w_ref, wt_ref, wf_ref, bf_ref, cos_ref, sin_ref, cost_ref, sint_ref,
                 qft_ref, kf_ref, vft_ref, qmt_ref, km_ref, vmt_ref, gates_ref, cb_ref, kmean_ref,
                 carry_ref, *, tm, width, gate_cols):
    j = pl.program_id(1)
    h = _rms(x_ref[...], g_ref[...])
    hb = h.astype(_BF16)
    hbt = h.T.astype(_BF16)
    reps = width // LANES
    half = HEAD_DIM // 2

    def proj(c):
        return jnp.dot(hb, w_ref[:, c * width:(c + 1) * width], preferred_element_type=_F32)

    def proj_t(c):
        return jnp.dot(wt_ref[c * width:(c + 1) * width, :], hbt, preferred_element_type=_F32)

    def rope(t):
        cos = jnp.concatenate([cos_ref[...]] * reps, axis=1)
        sin = jnp.concatenate([sin_ref[...]] * reps, axis=1)
        first = (lax.broadcasted_iota(jnp.int32, (1, width), 1) % HEAD_DIM) < half
        partner = jnp.where(first, pltpu.roll(t, width - half, 1), pltpu.roll(t, half, 1))
        return t * cos + partner * sin

    def rope_t(t):
        cos = jnp.concatenate([cost_ref[...]] * reps, axis=0)
        sin = jnp.concatenate([sint_ref[...]] * reps, axis=0)
        first = (lax.broadcasted_iota(jnp.int32, (width, 1), 0) % HEAD_DIM) < half
        partner = jnp.where(first, pltpu.roll(t, width - half, 0), pltpu.roll(t, half, 0))
        return t * cos + partner * sin

    scale = HEAD_DIM ** -0.5 * LOG2_E
    qft_ref[...] = (proj_t(0) * scale).astype(_BF16)
    vft_ref[...] = proj_t(1).astype(_BF16)
    qmt_ref[...] = (rope_t(proj_t(2)) * scale).astype(_BF16)
    vmt_ref[...] = proj_t(3).astype(_BF16)
    kf_ref[...] = proj(0).astype(_BF16)
    km = rope(proj(1))
    km_ref[...] = km.astype(_BF16)

    @pl.when(j == 0)
    def _():
        kmean_ref[...] = jnp.zeros_like(kmean_ref)
        carry_ref[...] = jnp.zeros_like(carry_ref)

    blocks = tm // MOBA_BLOCK
    for b in range(blocks):
        kmean_ref[pl.ds(j * blocks + b, 1), :] = jnp.mean(
            km[b * MOBA_BLOCK:(b + 1) * MOBA_BLOCK], axis=0, keepdims=True)

    for c in range(gate_cols // width):
        g = jnp.dot(hb, w_ref[:, (2 + c) * width:(3 + c) * width],
                    preferred_element_type=_F32)
        gates_ref[:, c * width:(c + 1) * width] = jax.nn.sigmoid(g).astype(_BF16)

    z = jnp.dot(hb, wf_ref[...], preferred_element_type=_F32) + bf_ref[...]
    log_f = jnp.minimum(z, 0.0) - jnp.log1p(jnp.exp(-jnp.abs(z)))
    r = lax.broadcasted_iota(jnp.int32, (LANES, LANES), 0)
    c = lax.broadcasted_iota(jnp.int32, (LANES, LANES), 1)
    tri = (c <= r).astype(_BF16)
    offset = carry_ref[0:1, :]
    blocks_cum = []
    for blk in range(tm // LANES):
        rest = log_f[blk * LANES:(blk + 1) * LANES]
        within = jnp.zeros((LANES, LANES), _F32)
        for _ in range(BIAS_PIECES):
            part = rest.astype(_BF16)
            rest = rest - part.astype(_F32)
            within = within + jnp.dot(tri, part, preferred_element_type=_F32)
        blocks_cum.append(within + offset)
        offset = offset + within[LANES - 1:LANES, :]
    cum = jnp.concatenate(blocks_cum, axis=0)
    carry_ref[...] = jnp.broadcast_to(offset, carry_ref.shape)

    head_lane = lax.broadcasted_iota(jnp.int32, (1, LANES), 1) < N_FOX_HEADS
    rest = jnp.where(head_lane, -cum * LOG2_E, 0.0)
    placed = jnp.zeros((tm, LANES), _F32)
    for piece in range(BIAS_PIECES):
        part = rest.astype(_BF16).astype(_F32)
        rest = rest - part
        placed = placed + (pltpu.roll(part, N_FOX_HEADS * piece, 1) if piece else part)
    cb_ref[...] = placed.astype(_BF16)


def _inproj(x, g, w, wt, wf, bfp, cos, sin, cos_t, sin_t, *, tm):
    b, s, d = x.shape
    width = N_FOX_HEADS * HEAD_DIM
    gate_cols = w.shape[1] - 2 * width
    act = jax.ShapeDtypeStruct((b, s, width), _BF16)
    act_t = jax.ShapeDtypeStruct((b, width, s), _BF16)
    act_spec = pl.BlockSpec((None, tm, width), lambda bi, j: (bi, j, 0))
    act_t_spec = pl.BlockSpec((None, width, tm), lambda bi, j: (bi, 0, j))
    return pl.pallas_call(
        functools.partial(_inproj_body, tm=tm, width=width, gate_cols=gate_cols),
        grid=(b, s // tm),
        in_specs=[
            pl.BlockSpec((None, tm, d), lambda bi, j: (bi, j, 0)),
            _const_spec((1, d)),
            _const_spec(w.shape),
            _const_spec(wt.shape),
            _const_spec(wf.shape),
            _const_spec(bfp.shape),
            pl.BlockSpec((tm, LANES), lambda bi, j: (j, 0)),
            pl.BlockSpec((tm, LANES), lambda bi, j: (j, 0)),
            pl.BlockSpec((LANES, tm), lambda bi, j: (0, j)),
            pl.BlockSpec((LANES, tm), lambda bi, j: (0, j)),
        ],
        out_specs=[act_t_spec, act_spec, act_t_spec, act_t_spec, act_spec, act_t_spec,
                   pl.BlockSpec((None, tm, gate_cols), lambda bi, j: (bi, j, 0)),
                   pl.BlockSpec((None, tm, LANES), lambda bi, j: (bi, j, 0)),
                   pl.BlockSpec((None, LANES, width), lambda bi, j: (bi, 0, 0))],
        out_shape=[act_t, act, act_t, act_t, act, act_t,
                   jax.ShapeDtypeStruct((b, s, gate_cols), _BF16),
                   jax.ShapeDtypeStruct((b, s, LANES), _BF16),
                   jax.ShapeDtypeStruct((b, LANES, width), _F32)],
        scratch_shapes=[pltpu.VMEM((8, LANES), _F32)],
        compiler_params=_params("arbitrary", "arbitrary"),
        name="inproj",
    )(x, g, w, wt, wf, bfp, cos, sin, cos_t, sin_t)


def _attn_body(uq_ref, uk_ref, qt_ref, k_ref, kx_ref, vt_ref, aux_ref, o_ref,
               w_ref, m_ref, l_ref, acc_ref, st_ref, p_ref, alpha_ref, *, mode, tq, tk, nb, n_full):
    hp = pl.program_id(1)
    wide = HEADS_PER_STEP * tq
    nq = w_ref.shape[0]
    feat = lax.broadcasted_iota(jnp.int32, (LANES, 1), 0)
    xr = lax.broadcasted_iota(jnp.int32, (LANES, wide), 0)
    xc = lax.broadcasted_iota(jnp.int32, (LANES, wide), 1)

    def setup(i, carry):
        qt = qt_ref[:, pl.ds(pl.multiple_of(i * tq, tq), tq)]
        zero = jnp.zeros_like(qt)
        wq = jnp.concatenate([jnp.where(feat < HEAD_DIM, qt, zero),
                              jnp.where(feat < HEAD_DIM, zero, qt)], axis=1)
        if mode == "fox":
            head = hp * HEADS_PER_STEP + xc // tq
            lower = ((xr % N_FOX_HEADS == head) & (xr < BIAS_PIECES * N_FOX_HEADS)).astype(_BF16)
        else:
            own = (i * tq) // MOBA_BLOCK
            gate = jnp.dot(aux_ref[...].astype(_BF16), wq, preferred_element_type=_F32)[0:nb]
            blk = lax.broadcasted_iota(jnp.int32, (nb, wide), 0)
            gate = jnp.where(blk < own, gate, -jnp.inf)
            keep = blk == own
            for _ in range(MOBA_TOPK):
                mx = jnp.max(gate, axis=0, keepdims=True)
                cand = jnp.where((gate == mx) & (mx > -jnp.inf), blk, nb)
                pick = blk == jnp.min(cand, axis=0, keepdims=True)
                keep = keep | pick
                gate = jnp.where(pick, -jnp.inf, gate)
            lower = jnp.where(keep, 0.0, NEG_BIG)
            if nb < LANES:
                lower = jnp.concatenate([lower, jnp.zeros((LANES - nb, wide), _F32)], axis=0)
            lower = lower.astype(_BF16)
        w_ref[i] = jnp.concatenate([wq, lower], axis=0)
        m_ref[i] = jnp.full(m_ref.shape[1:], NEG_BIG, _F32)
        l_ref[i] = jnp.zeros(l_ref.shape[1:], _F32)
        acc_ref[i] = jnp.zeros(acc_ref.shape[1:], _F32)
        return carry

    lax.fori_loop(0, nq, setup, 0)

    def sweep(n_units, unit, causal):
        def scores(t, slot):
            q, kb = unit(t)
            start = pl.multiple_of(kb * tk, tk)
            ka = jnp.concatenate([k_ref[pl.ds(start, tk), :], kx_ref[pl.ds(start, tk), :]], axis=1)
            st = jnp.dot(ka, w_ref[q], preferred_element_type=_F32)
            if causal:
                key = start + lax.broadcasted_iota(jnp.int32, (tk, wide), 0)
                qry = q * tq + lax.broadcasted_iota(jnp.int32, (tk, wide), 1) % tq
                st = jnp.where(key <= qry, st, NEG_BIG)
            st_ref[slot] = st

        def softmax(t, slot):
            q, _ = unit(t)
            rows = SOFTMAX_ROWS
            m_prev = m_ref[q][0:1, :]
            part = None
            for r in range(0, tk, rows):
                blk = st_ref[slot, r:r + rows, :].reshape(rows // 8, 8, wide)
                cur = jnp.max(blk, axis=0)
                part = cur if part is None else jnp.maximum(part, cur)
            m_new = jnp.maximum(m_prev, jnp.max(part, axis=0, keepdims=True))
            for r in range(0, tk, rows):
                p_ref[slot, r:r + rows, :] = jnp.exp2(st_ref[slot, r:r + rows, :] - m_new).astype(_BF16)
            m_ref[q] = jnp.broadcast_to(m_new, m_ref.shape[1:])
            alpha_ref[slot] = jnp.broadcast_to(jnp.exp2(m_prev - m_new), alpha_ref.shape[1:])

        def values(t, slot):
            q, kb = unit(t)
            start = pl.multiple_of(kb * tk, tk)
            ones = jnp.ones((ONES_ROWS, tk), _BF16)
            for h in range(HEADS_PER_STEP):
                cols = slice(h * tq, (h + 1) * tq)
                feats = slice(h * HEAD_DIM, (h + 1) * HEAD_DIM)
                lhs = jnp.concatenate([vt_ref[feats, pl.ds(start, tk)], ones], axis=0)
                pv = jnp.dot(lhs, p_ref[slot, :, cols], preferred_element_type=_F32)
                alpha = alpha_ref[slot, 0:1, cols]
                acc_ref[q, feats, :] = acc_ref[q, feats, :] * alpha + pv[0:HEAD_DIM]
                l_ref[q, :, cols] = jnp.broadcast_to(
                    alpha * l_ref[q, 0:1, cols] + pv[HEAD_DIM:HEAD_DIM + 1], (8, tq))

        def step(base, c, do_scores, do_softmax, do_values):
            u = base + c
            if do_scores:
                scores(u, c % PIPE_UNITS)
            if do_values:
                values(u - 2 * SOFTMAX_LAG, (c - 2 * SOFTMAX_LAG) % PIPE_UNITS)
            if do_softmax:
                softmax(u - SOFTMAX_LAG, (c - SOFTMAX_LAG) % PIPE_UNITS)

        for c in range(PIPE_UNITS):
            step(0, c, True, c >= SOFTMAX_LAG, c >= 2 * SOFTMAX_LAG)

        def group(j, carry):
            for c in range(PIPE_UNITS):
                step(j * PIPE_UNITS, c, True, True, True)
            return carry

        lax.fori_loop(1, n_units // PIPE_UNITS, group, 0)
        for c in range(2 * SOFTMAX_LAG):
            step(n_units, c, False, c < SOFTMAX_LAG, True)

    sweep(nq, lambda t: (t, (t * tq) // tk), True)
    if n_full:
        sweep(n_full, lambda t: (uq_ref[t], uk_ref[t]), False)

    def finish(i, carry):
        out_t = jnp.concatenate(
            [acc_ref[i, h * HEAD_DIM:(h + 1) * HEAD_DIM, :] / l_ref[i, 0:1, h * tq:(h + 1) * tq]
             for h in range(HEADS_PER_STEP)], axis=0)
        o_ref[pl.ds(pl.multiple_of(i * tq, tq), tq), :] = out_t.T.astype(o_ref.dtype)
        return carry

    lax.fori_loop(0, nq, finish, 0, unroll=2)


def _attention(qt, k, kx, vt, aux, *, mode):
    b, width, s = qt.shape
    tq, tk = ATTN_Q, ATTN_K
    nq = s // tq
    nb = s // MOBA_BLOCK
    wide = HEADS_PER_STEP * tq
    full_units = [(i, n) for i in range(nq) for n in range((i * tq) // tk)]
    assert nq % PIPE_UNITS == 0 and len(full_units) % PIPE_UNITS == 0 and PIPE_UNITS >= 2 * SOFTMAX_LAG
    unit_q = jnp.asarray([u[0] for u in full_units] or [0], jnp.int32)
    unit_k = jnp.asarray([u[1] for u in full_units] or [0], jnp.int32)
    if mode == "fox":
        aux_spec = pl.BlockSpec(aux.shape, lambda bi, hp, uq, uk: (0, 0))
        kx_spec = pl.BlockSpec((None, s, LANES), lambda bi, hp, uq, uk: (bi, 0, 0))
    else:
        aux_spec = pl.BlockSpec((None, LANES, LANES), lambda bi, hp, uq, uk: (bi, 0, hp))
        kx_spec = pl.BlockSpec(kx.shape, lambda bi, hp, uq, uk: (0, 0))
    return pl.pallas_call(
        functools.partial(_attn_body, mode=mode, tq=tq, tk=tk, nb=nb, n_full=len(full_units)),
        grid_spec=pltpu.PrefetchScalarGridSpec(
            num_scalar_prefetch=2,
            grid=(b, width // LANES),
            in_specs=[
                pl.BlockSpec((None, LANES, s), lambda bi, hp, uq, uk: (bi, hp, 0)),
                pl.BlockSpec((None, s, LANES), lambda bi, hp, uq, uk: (bi, 0, hp)),
                kx_spec,
                pl.BlockSpec((None, LANES, s), lambda bi, hp, uq, uk: (bi, hp, 0)),
                aux_spec,
            ],
            out_specs=pl.BlockSpec((None, s, LANES), lambda bi, hp, uq, uk: (bi, 0, hp)),
            scratch_shapes=[
                pltpu.VMEM((nq, 2 * LANES, wide), _BF16),
                pltpu.VMEM((nq, 8, wide), _F32),
                pltpu.VMEM((nq, 8, wide), _F32),
                pltpu.VMEM((nq, LANES, tq), _F32),
                pltpu.VMEM((PIPE_UNITS, tk, wide), _F32),
                pltpu.VMEM((PIPE_UNITS, tk, wide), _BF16),
                pltpu.VMEM((PIPE_UNITS, 8, wide), _F32),
            ],
        ),
        out_shape=jax.ShapeDtypeStruct((b, s, width), _BF16),
        compiler_params=_params("arbitrary", "arbitrary"),
        name=mode,
    )(unit_q, unit_k, qt, k, kx, vt, aux)


def _pack_bf16_pair(a, b):
    lo = pltpu.bitcast(a.astype(_BF16).astype(_F32), jnp.uint32) >> 16
    hi = pltpu.bitcast(b.astype(_BF16).astype(_F32), jnp.uint32) & jnp.uint32(0xFFFF0000)
    return lo | hi


def _unpack_bf16_pair(u):
    lo = pltpu.bitcast(u << 16, _F32).astype(_BF16)
    hi = pltpu.bitcast(u & jnp.uint32(0xFFFF0000), _F32).astype(_BF16)
    return jnp.concatenate([lo, hi], axis=1)


def _postattn_body(yf_ref, ym_ref, gates_ref, x_ref, wfb_ref, wmb_ref, wout_ref, g_ref, wr_ref, br_ref,
                   x1_ref, h2_ref, route_ref, counts_ref, carry_ref, *, tm, d):
    step = pl.program_id(0)
    ya = jnp.dot(yf_ref[...], wfb_ref[...], preferred_element_type=_F32)
    yb = jnp.dot(ym_ref[...], wmb_ref[...], preferred_element_type=_F32)
    mixed = gates_ref[:, 0:d].astype(_F32) * ya + gates_ref[:, d:2 * d].astype(_F32) * yb
    x1 = x_ref[...] + jnp.dot(mixed.astype(_BF16), wout_ref[...], preferred_element_type=_F32)
    x1_ref[...] = x1
    h2 = _rms(x1, g_ref[...])
    h2_ref[...] = _pack_bf16_pair(h2[:, 0:d // 2], h2[:, d // 2:d])

    logits = jnp.dot(h2.astype(_BF16), wr_ref[...], preferred_element_type=_F32) + br_ref[...]
    lane = lax.broadcasted_iota(jnp.int32, (tm, LANES), 1)
    is_group = (lane >= N_EXPERTS) & (lane < N_EXPERTS + N_GROUPS)
    gl = jnp.where(is_group, logits, -jnp.inf)
    ge = jnp.exp(gl - jnp.max(gl, axis=-1, keepdims=True))
    gp = ge / jnp.sum(ge, axis=-1, keepdims=True)
    g_top = jnp.max(gp, axis=-1, keepdims=True)
    g_idx = jnp.min(jnp.where(is_group & (gp == g_top), lane, 2 * LANES), axis=-1,
                    keepdims=True) - N_EXPERTS
    in_group = (lane < N_EXPERTS) & ((lane // EXPERTS_PER_GROUP) == g_idx)
    fl = jnp.where(in_group, logits, -jnp.inf)
    f1 = jnp.max(fl, axis=-1, keepdims=True)
    e1 = jnp.min(jnp.where(fl == f1, lane, 2 * LANES), axis=-1, keepdims=True)
    fl2 = jnp.where(lane == e1, -jnp.inf, fl)
    f2 = jnp.max(fl2, axis=-1, keepdims=True)
    e2 = jnp.min(jnp.where(fl2 == f2, lane, 2 * LANES), axis=-1, keepdims=True)
    t2 = jnp.exp(f2 - f1)
    w1 = g_top * (1.0 / (1.0 + t2))
    w2 = g_top * (t2 / (1.0 + t2))

    @pl.when(step == 0)
    def _():
        carry_ref[...] = jnp.zeros_like(carry_ref)

    onehot = ((lane == e1) | (lane == e2)).astype(_F32)
    rr = lax.broadcasted_iota(jnp.int32, (tm, tm), 0)
    cc = lax.broadcasted_iota(jnp.int32, (tm, tm), 1)
    before = jnp.dot((cc < rr).astype(_BF16), onehot.astype(_BF16), preferred_element_type=_F32)
    before = before + carry_ref[0:1, :]
    r1 = jnp.sum(jnp.where(lane == e1, before, 0.0), axis=-1, keepdims=True)
    r2 = jnp.sum(jnp.where(lane == e2, before, 0.0), axis=-1, keepdims=True)
    total = carry_ref[0:1, :] + jnp.sum(onehot, axis=0, keepdims=True)
    carry_ref[...] = jnp.broadcast_to(total, carry_ref.shape)
    counts_ref[...] = jnp.broadcast_to(total, counts_ref.shape)

    slab = jnp.where(lane == 0, w1, 0.0)
    slab = jnp.where(lane == 1, w2, slab)
    slab = jnp.where(lane == 2, e1.astype(_F32), slab)
    slab = jnp.where(lane == 3, e2.astype(_F32), slab)
    slab = jnp.where(lane == 4, r1, slab)
    slab = jnp.where(lane == 5, r2, slab)
    route_ref[...] = slab


def _postattn(yf, ym, gates, x, wfb, wmb, wout, g, wr, br, *, tm):
    t, d = x.shape
    width = yf.shape[1]
    row = lambda cols: pl.BlockSpec((tm, cols), lambda i: (i, 0))
    return pl.pallas_call(
        functools.partial(_postattn_body, tm=tm, d=d),
        grid=(t // tm,),
        in_specs=[row(width), row(width), row(2 * d), row(d),
                  _const_spec(wfb.shape), _const_spec(wmb.shape), _const_spec(wout.shape),
                  _const_spec((1, d)), _const_spec(wr.shape), _const_spec(br.shape)],
        out_specs=[row(d), row(d // 2), row(LANES), _const_spec((8, LANES))],
        out_shape=[jax.ShapeDtypeStruct((t, d), _F32),
                   jax.ShapeDtypeStruct((t, d // 2), jnp.uint32),
                   jax.ShapeDtypeStruct((t, LANES), _F32),
                   jax.ShapeDtypeStruct((8, LANES), _F32)],
        scratch_shapes=[pltpu.VMEM((8, LANES), _F32)],
        compiler_params=_params("arbitrary"),
        name="postattn",
    )(yf, ym, gates, x, wfb, wmb, wout, g, wr, br)


def _dispatch_body(dest_ref, h2_ref, xs_in_ref, xs_ref, sem, *, tm):
    del xs_in_ref

    def row_copy(r, k):
        return pltpu.make_async_copy(h2_ref.at[pl.ds(r, 1)],
                                     xs_ref.at[pl.ds(dest_ref[2 * r + k], 1)], sem)

    def start(r, carry):
        row_copy(r, 0).start(priority=0)
        row_copy(r, 1).start(priority=1)
        return carry

    lax.fori_loop(0, tm, start, 0, unroll=DMA_UNROLL)
    for k in range(2):
        pltpu.make_async_copy(h2_ref, xs_ref.at[pl.ds(0, tm)], sem).wait()


def _dispatch(dest_flat, h2p, xs0, *, tm):
    t, cols = h2p.shape
    return pl.pallas_call(
        functools.partial(_dispatch_body, tm=tm),
        grid=(t // tm,),
        in_specs=[pl.BlockSpec((2 * tm,), lambda i: (i,), memory_space=pltpu.SMEM),
                  pl.BlockSpec((tm, cols), lambda i: (i, 0)),
                  pl.BlockSpec(memory_space=pl.ANY)],
        out_specs=pl.BlockSpec(memory_space=pl.ANY),
        out_shape=jax.ShapeDtypeStruct(xs0.shape, xs0.dtype),
        scratch_shapes=[pltpu.SemaphoreType.DMA(())],
        input_output_aliases={2: 0},
        compiler_params=_params("arbitrary"),
        name="dispatch",
    )(dest_flat, h2p, xs0)


def _experts_body(be_ref, nused_ref, xs_ref, wg_ref, wu_ref, wd_ref, out_ref, wgb_ref, wub_ref, wdb_ref):
    j = pl.program_id(0)
    used = j < nused_ref[0]

    @pl.when(jnp.logical_not(used))
    def _():
        out_ref[...] = jnp.zeros_like(out_ref)

    @pl.when(used & ((j == 0) | (be_ref[j] != be_ref[jnp.maximum(j - 1, 0)])))
    def _():
        wgb_ref[...] = wg_ref[...].astype(_BF16)
        wub_ref[...] = wu_ref[...].astype(_BF16)
        wdb_ref[...] = wd_ref[...].astype(_BF16)

    @pl.when(used)
    def _():
        rows = _unpack_bf16_pair(xs_ref[...])
        gate = jnp.dot(rows, wgb_ref[...], preferred_element_type=_F32)
        up = jnp.dot(rows, wub_ref[...], preferred_element_type=_F32)
        act = (gate * jax.nn.sigmoid(gate)) * up
        out_ref[...] = jnp.dot(act.astype(_BF16), wdb_ref[...], preferred_element_type=_F32)


def _experts(block_expert, n_used, xs, wg, wu, wd, *, te):
    n_rows, half = xs.shape
    _, d, de = wg.shape

    def blk(j, be, nu):
        return jnp.minimum(j, nu[0] - 1)

    return pl.pallas_call(
        _experts_body,
        grid_spec=pltpu.PrefetchScalarGridSpec(
            num_scalar_prefetch=2,
            grid=(n_rows // te,),
            in_specs=[pl.BlockSpec((te, half), lambda j, be, nu: (blk(j, be, nu), 0)),
                      pl.BlockSpec((None, d, de), lambda j, be, nu: (be[blk(j, be, nu)], 0, 0)),
                      pl.BlockSpec((None, d, de), lambda j, be, nu: (be[blk(j, be, nu)], 0, 0)),
                      pl.BlockSpec((None, de, d), lambda j, be, nu: (be[blk(j, be, nu)], 0, 0))],
            out_specs=pl.BlockSpec((te, d), lambda j, be, nu: (j, 0)),
            scratch_shapes=[pltpu.VMEM((d, de), _BF16), pltpu.VMEM((d, de), _BF16),
                            pltpu.VMEM((de, d), _BF16)],
        ),
        out_shape=jax.ShapeDtypeStruct((n_rows, d), _F32),
        compiler_params=_params("arbitrary"),
        name="experts",
    )(block_expert, n_used, xs, wg, wu, wd)


def _final_body(dest_ref, dest_next_ref, x1_ref, route_ref, p_ref, rows_ref, gp_ref, wpg_ref, wpp_ref,
                gf_ref, o_ref, buf_ref, sems, *, tm):
    i = pl.program_id(0)
    slot = i % 2

    def gather(idx_ref, to_slot):
        def start(r, carry):
            for k in range(2):
                pltpu.make_async_copy(rows_ref.at[pl.ds(idx_ref[2 * r + k], 1)],
                                      buf_ref.at[to_slot, k, pl.ds(r, 1)],
                                      sems.at[to_slot]).start(priority=k)
            return carry

        lax.fori_loop(0, tm, start, 0, unroll=DMA_UNROLL)

    @pl.when(i == 0)
    def _():
        gather(dest_ref, 0)

    @pl.when(i + 1 < pl.num_programs(0))
    def _():
        gather(dest_next_ref, 1 - slot)

    emb = jnp.dot(p_ref[...].astype(_BF16), wpp_ref[...], preferred_element_type=_F32)
    for k in range(2):
        pltpu.make_async_copy(rows_ref.at[pl.ds(0, tm)], buf_ref.at[slot, k], sems.at[slot]).wait()

    route = route_ref[...]
    x2 = x1_ref[...] + (buf_ref[slot, 0] * route[:, 0:1] + buf_ref[slot, 1] * route[:, 1:2])
    gate = jax.nn.sigmoid(jnp.dot(_rms(x2, gp_ref[...]).astype(_BF16), wpg_ref[...],
                                  preferred_element_type=_F32))
    o_ref[...] = _rms(x2 + gate * emb, gf_ref[...])


def _final(dest_flat, x1, route, p, rows, gp, wpg, wpp, gf, *, tm):
    t, d = x1.shape
    ple = p.shape[1]
    last = t // tm - 1
    row = lambda cols: pl.BlockSpec((tm, cols), lambda i: (i, 0))
    return pl.pallas_call(
        functools.partial(_final_body, tm=tm),
        grid=(t // tm,),
        in_specs=[pl.BlockSpec((2 * tm,), lambda i: (i,), memory_space=pltpu.SMEM),
                  pl.BlockSpec((2 * tm,), lambda i: (jnp.minimum(i + 1, last),), memory_space=pltpu.SMEM),
                  row(d), row(LANES), row(ple),
                  pl.BlockSpec(memory_space=pl.ANY),
                  _const_spec((1, d)), _const_spec(wpg.shape), _const_spec(wpp.shape),
                  _const_spec((1, d))],
        out_specs=row(d),
        out_shape=jax.ShapeDtypeStruct((t, d), _F32),
        scratch_shapes=[pltpu.VMEM((2, 2, tm, d), _F32), pltpu.SemaphoreType.DMA((2,))],
        compiler_params=_params("arbitrary"),
        name="final",
    )(dest_flat, dest_flat, x1, route, p, rows, gp, wpg, wpp, gf)


def _rope_tables(s):
    half = HEAD_DIM // 2
    inv = 1.0 / (ROPE_THETA ** (jnp.arange(0, HEAD_DIM, 2, dtype=_F32) / HEAD_DIM))
    ang = jnp.arange(s, dtype=_F32)[:, None] * inv[None, :]
    cos, sin = jnp.cos(ang), jnp.sin(ang)
    reps = LANES // half
    cos_t = jnp.tile(cos, (1, reps))
    sin_t = jnp.tile(jnp.concatenate([-sin, sin], axis=1), (1, reps // 2))
    return cos_t, sin_t


def _layer(x, p, attn_norm, w_in, b_forget, w_fox_branch, w_moba_branch, w_out, moe_norm,
           w_group, b_group, w_fine, b_fine, w_gate, w_up, w_down, ple_norm, w_ple_gate, w_ple_proj,
           final_gain):
    b, s, d = x.shape
    t = b * s
    width = N_FOX_HEADS * HEAD_DIM
    assert N_MOBA_HEADS * HEAD_DIM == width and MOBA_BLOCK % ATTN_Q == 0 and ATTN_K % MOBA_BLOCK == 0
    assert BIAS_PIECES * N_FOX_HEADS <= LANES and s % ATTN_K == 0
    assert s % ROW_TILE == 0 and ROW_TILE % MOBA_BLOCK == 0 and s // MOBA_BLOCK <= LANES
    assert t % FINAL_TILE == 0 and N_EXPERTS + N_GROUPS <= LANES

    qkv_cols = 6 * width
    f_cols = N_FOX_HEADS
    chunk = lambda c: w_in[:, c * width:(c + 1) * width]
    w_rows = jnp.concatenate([chunk(1), chunk(4), w_in[:, qkv_cols + f_cols:]], axis=1).astype(_BF16)
    w_cols = jnp.concatenate([chunk(0), chunk(2), chunk(3), chunk(5)], axis=1).T.astype(_BF16)
    wf = jnp.zeros((d, LANES), _BF16).at[:, :f_cols].set(w_in[:, qkv_cols:qkv_cols + f_cols].astype(_BF16))
    bfp = jnp.zeros((1, LANES), _F32).at[0, :f_cols].set(b_forget.astype(_F32))
    cos, sin = _rope_tables(s)

    qft, kf, vft, qmt, km, vmt, gates, cb, kmean = _inproj(
        x, attn_norm.reshape(1, d), w_rows, w_cols, wf, bfp, cos, sin, cos.T, sin.T, tm=ROW_TILE)
    y_fox = _attention(qft, kf, cb, vft, jnp.zeros((8, LANES), _F32), mode="fox")
    block_of_key = jnp.arange(s, dtype=jnp.int32)[:, None] // MOBA_BLOCK
    block_onehot = (block_of_key == jnp.arange(LANES, dtype=jnp.int32)[None, :]).astype(_BF16)
    y_moba = _attention(qmt, km, block_onehot, vmt, kmean, mode="moba")

    wr = jnp.zeros((d, LANES), _BF16)
    wr = wr.at[:, :N_EXPERTS].set(w_fine.astype(_BF16))
    wr = wr.at[:, N_EXPERTS:N_EXPERTS + N_GROUPS].set(w_group.astype(_BF16))
    br = jnp.zeros((1, LANES), _F32)
    br = br.at[0, :N_EXPERTS].set(b_fine.astype(_F32))
    br = br.at[0, N_EXPERTS:N_EXPERTS + N_GROUPS].set(b_group.astype(_F32))
    x1, h2p, route, counts = _postattn(
        y_fox.reshape(t, width), y_moba.reshape(t, width), gates.reshape(t, 2 * d), x.reshape(t, d),
        w_fox_branch.astype(_BF16), w_moba_branch.astype(_BF16), w_out.astype(_BF16),
        moe_norm.reshape(1, d), wr, br, tm=ROW_TILE)

    te = EXPERT_TILE
    n_blk = (2 * t) // te + N_EXPERTS
    counts = counts[0, :N_EXPERTS].astype(jnp.int32)
    blocks_per = (counts + te - 1) // te
    block_end = jnp.cumsum(blocks_per)
    row_start = (block_end - blocks_per) * te
    experts = route[:, 2:4].astype(jnp.int32)
    dest = (row_start[experts] + route[:, 4:6].astype(jnp.int32)).reshape(-1)
    n_used = block_end[-1:].astype(jnp.int32)
    block_ids = jnp.arange(n_blk, dtype=jnp.int32)
    block_expert = jnp.minimum(
        jnp.sum((block_ids[:, None] >= block_end[None, :]).astype(jnp.int32), axis=1), N_EXPERTS - 1)

    xs = _dispatch(dest, h2p, jnp.zeros((n_blk * te, d // 2), jnp.uint32), tm=ROW_TILE)
    rows = _experts(block_expert, n_used, xs, w_gate, w_up, w_down, te=te)
    out = _final(dest, x1, route, p.reshape(t, -1), rows, ple_norm.reshape(1, d),
                 w_ple_gate.astype(_BF16), w_ple_proj.astype(_BF16), final_gain.reshape(1, d),
                 tm=FINAL_TILE)
    return out.reshape(b, s, d)


def kernel(x, p, attn_norm, w_in, b_forget, w_fox_branch, w_moba_branch, w_out, moe_norm, w_group,
           b_group, w_fine, b_fine, w_gate, w_up, w_down, ple_norm, w_ple_gate, w_ple_proj, final_norm):
    depth = p.shape[0]
    assert depth == 1, "the final norm is fused into the last layer's kernel"
    i = 0
    return _layer(x, p[i], attn_norm[i], w_in[i], b_forget[i], w_fox_branch[i], w_moba_branch[i],
                  w_out[i], moe_norm[i], w_group[i], b_group[i], w_fine[i], b_fine[i], w_gate[i],
                  w_up[i], w_down[i], ple_norm[i], w_ple_gate[i], w_ple_proj[i], final_norm)
```

```python
import functools

import jax
import jax.numpy as jnp
from jax import lax
from jax.experimental import pallas as pl
from jax.experimental.pallas import tpu as pltpu
from jax.experimental.pallas import tpu_sc as plsc

HEAD_DIM = 64
N_FOX_HEADS = 8
N_MOBA_HEADS = 8
MOBA_BLOCK = 256
MOBA_TOPK = 3
ROPE_THETA = 10000.0
N_GROUPS = 4
EXPERTS_PER_GROUP = 8
N_EXPERTS = N_GROUPS * EXPERTS_PER_GROUP
RMS_EPS = 1e-6

LANES = 128
HEADS_PER_STEP = LANES // HEAD_DIM
ATTN_Q = 256
ATTN_K = 256
PIPE_UNITS = 8
SOFTMAX_LAG = 4
ONES_ROWS = 16
SOFTMAX_ROWS = 64
BIAS_PIECES = 3
ROW_TILE = 512
SC_CORES = 2
SC_SUBCORES = 16
SC_CHUNK = 32
DMA_UNROLL = 8
FINAL_TILE = 256
EXPERT_TILE = 512
LOG2_E = 1.4426950408889634
NEG_BIG = -1e30
VMEM_LIMIT = 48 * 1024 * 1024

_BF16 = jnp.bfloat16
_F32 = jnp.float32


def _params(*sem):
    return pltpu.CompilerParams(dimension_semantics=sem, vmem_limit_bytes=VMEM_LIMIT)


def _rms(x, g):
    return x * lax.rsqrt(jnp.mean(x * x, axis=-1, keepdims=True) + RMS_EPS) * g


def _const_spec(shape):
    return pl.BlockSpec(shape, lambda *_: (0,) * len(shape))


def _inproj_body(x_ref, g_ref, w_ref, wt_ref, wf_ref, bf_ref, cos_ref, sin_ref, cost_ref, sint_ref,
                 qft_ref, kf_ref, vft_ref, qmt_ref, km_ref, vmt_ref, gates_ref, cb_ref, kmean_ref,
                 carry_ref, *, tm, width, gate_cols):
    j = pl.program_id(1)
    h = _rms(x_ref[...], g_ref[...])
    hb = h.astype(_BF16)
    hbt = h.T.astype(_BF16)
    reps = width // LANES
    half = HEAD_DIM // 2

    def proj(c):
        return jnp.dot(hb, w_ref[:, c * width:(c + 1) * width], preferred_element_type=_F32)

    def proj_t(c):
        return jnp.dot(wt_ref[c * width:(c + 1) * width, :], hbt, preferred_element_type=_F32)

    def rope(t):
        cos = jnp.concatenate([cos_ref[...]] * reps, axis=1)
        sin = jnp.concatenate([sin_ref[...]] * reps, axis=1)
        first = (lax.broadcasted_iota(jnp.int32, (1, width), 1) % HEAD_DIM) < half
        partner = jnp.where(first, pltpu.roll(t, width - half, 1), pltpu.roll(t, half, 1))
        return t * cos + partner * sin

    def rope_t(t):
        cos = jnp.concatenate([cost_ref[...]] * reps, axis=0)
        sin = jnp.concatenate([sint_ref[...]] * reps, axis=0)
        first = (lax.broadcasted_iota(jnp.int32, (width, 1), 0) % HEAD_DIM) < half
        partner = jnp.where(first, pltpu.roll(t, width - half, 0), pltpu.roll(t, half, 0))
        return t * cos + partner * sin

    scale = HEAD_DIM ** -0.5 * LOG2_E
    qft_ref[...] = (proj_t(0) * scale).astype(_BF16)
    vft_ref[...] = proj_t(1).astype(_BF16)
    qmt_ref[...] = (rope_t(proj_t(2)) * scale).astype(_BF16)
    vmt_ref[...] = proj_t(3).astype(_BF16)
    kf_ref[...] = proj(0).astype(_BF16)
    km = rope(proj(1))
    km_ref[...] = km.astype(_BF16)

    @pl.when(j == 0)
    def _():
        kmean_ref[...] = jnp.zeros_like(kmean_ref)
        carry_ref[...] = jnp.zeros_like(carry_ref)

    blocks = tm // MOBA_BLOCK
    for b in range(blocks):
        kmean_ref[pl.ds(j * blocks + b, 1), :] = jnp.mean(
            km[b * MOBA_BLOCK:(b + 1) * MOBA_BLOCK], axis=0, keepdims=True)

    for c in range(gate_cols // width):
        g = jnp.dot(hb, w_ref[:, (2 + c) * width:(3 + c) * width],
                    preferred_element_type=_F32)
        gates_ref[:, c * width:(c + 1) * width] = jax.nn.sigmoid(g).astype(_BF16)

    z = jnp.dot(hb, wf_ref[...], preferred_element_type=_F32) + bf_ref[...]
    log_f = jnp.minimum(z, 0.0) - jnp.log1p(jnp.exp(-jnp.abs(z)))
    r = lax.broadcasted_iota(jnp.int32, (LANES, LANES), 0)
    c = lax.broadcasted_iota(jnp.int32, (LANES, LANES), 1)
    tri = (c <= r).astype(_BF16)
    offset = carry_ref[0:1, :]
    blocks_cum = []
    for blk in range(tm // LANES):
        rest = log_f[blk * LANES:(blk + 1) * LANES]
        within = jnp.zeros((LANES, LANES), _F32)
        for _ in range(BIAS_PIECES):
            part = rest.astype(_BF16)
            rest = rest - part.astype(_F32)
            within = within + jnp.dot(tri, part, preferred_element_type=_F32)
        blocks_cum.append(within + offset)
        offset = offset + within[LANES - 1:LANES, :]
    cum = jnp.concatenate(blocks_cum, axis=0)
    carry_ref[...] = jnp.broadcast_to(offset, carry_ref.shape)

    head_lane = lax.broadcasted_iota(jnp.int32, (1, LANES), 1) < N_FOX_HEADS
    rest = jnp.where(head_lane, -cum * LOG2_E, 0.0)
    placed = jnp.zeros((tm, LANES), _F32)
    for piece in range(BIAS_PIECES):
        part = rest.astype(_BF16).astype(_F32)
        rest = rest - part
        placed = placed + (pltpu.roll(part, N_FOX_HEADS * piece, 1) if piece else part)
    cb_ref[...] = placed.astype(_BF16)


def _inproj(x, g, w, wt, wf, bfp, cos, sin, cos_t, sin_t, *, tm):
    b, s, d = x.shape
    width = N_FOX_HEADS * HEAD_DIM
    gate_cols = w.shape[1] - 2 * width
    act = jax.ShapeDtypeStruct((b, s, width), _BF16)
    act_t = jax.ShapeDtypeStruct((b, width, s), _BF16)
    act_spec = pl.BlockSpec((None, tm, width), lambda bi, j: (bi, j, 0))
    act_t_spec = pl.BlockSpec((None, width, tm), lambda bi, j: (bi, 0, j))
    return pl.pallas_call(
        functools.partial(_inproj_body, tm=tm, width=width, gate_cols=gate_cols),
        grid=(b, s // tm),
        in_specs=[
            pl.BlockSpec((None, tm, d), lambda bi, j: (bi, j, 0)),
            _const_spec((1, d)),
            _const_spec(w.shape),
            _const_spec(wt.shape),
            _const_spec(wf.shape),
            _const_spec(bfp.shape),
            pl.BlockSpec((tm, LANES), lambda bi, j: (j, 0)),
            pl.BlockSpec((tm, LANES), lambda bi, j: (j, 0)),
            pl.BlockSpec((LANES, tm), lambda bi, j: (0, j)),
            pl.BlockSpec((LANES, tm), lambda bi, j: (0, j)),
        ],
        out_specs=[act_t_spec, act_spec, act_t_spec, act_t_spec, act_spec, act_t_spec,
                   pl.BlockSpec((None, tm, gate_cols), lambda bi, j: (bi, j, 0)),
                   pl.BlockSpec((None, tm, LANES), lambda bi, j: (bi, j, 0)),
                   pl.BlockSpec((None, LANES, width), lambda bi, j: (bi, 0, 0))],
        out_shape=[act_t, act, act_t, act_t, act, act_t,
                   jax.ShapeDtypeStruct((b, s, gate_cols), _BF16),
                   jax.ShapeDtypeStruct((b, s, LANES), _BF16),
                   jax.ShapeDtypeStruct((b, LANES, width), _F32)],
        scratch_shapes=[pltpu.VMEM((8, LANES), _F32)],
        compiler_params=_params("arbitrary", "arbitrary"),
        name="inproj",
    )(x, g, w, wt, wf, bfp, cos, sin, cos_t, sin_t)


def _attn_body(uq_ref, uk_ref, qt_ref, k_ref, kx_ref, vt_ref, aux_ref, o_ref,
               w_ref, m_ref, l_ref, acc_ref, st_ref, p_ref, alpha_ref, *, mode, tq, tk, nb, n_full):
    hp = pl.program_id(1)
    wide = HEADS_PER_STEP * tq
    nq = w_ref.shape[0]
    feat = lax.broadcasted_iota(jnp.int32, (LANES, 1), 0)
    xr = lax.broadcasted_iota(jnp.int32, (LANES, wide), 0)
    xc = lax.broadcasted_iota(jnp.int32, (LANES, wide), 1)

    def setup(i, carry):
        qt = qt_ref[:, pl.ds(pl.multiple_of(i * tq, tq), tq)]
        zero = jnp.zeros_like(qt)
        wq = jnp.concatenate([jnp.where(feat < HEAD_DIM, qt, zero),
                              jnp.where(feat < HEAD_DIM, zero, qt)], axis=1)
        if mode == "fox":
            head = hp * HEADS_PER_STEP + xc // tq
            lower = ((xr % N_FOX_HEADS == head) & (xr < BIAS_PIECES * N_FOX_HEADS)).astype(_BF16)
        else:
            own = (i * tq) // MOBA_BLOCK
            gate = jnp.dot(aux_ref[...].astype(_BF16), wq, preferred_element_type=_F32)[0:nb]
            blk = lax.broadcasted_iota(jnp.int32, (nb, wide), 0)
            gate = jnp.where(blk < own, gate, -jnp.inf)
            keep = blk == own
            for _ in range(MOBA_TOPK):
                mx = jnp.max(gate, axis=0, keepdims=True)
                cand = jnp.where((gate == mx) & (mx > -jnp.inf), blk, nb)
                pick = blk == jnp.min(cand, axis=0, keepdims=True)
                keep = keep | pick
                gate = jnp.where(pick, -jnp.inf, gate)
            lower = jnp.where(keep, 0.0, NEG_BIG)
            if nb < LANES:
                lower = jnp.concatenate([lower, jnp.zeros((LANES - nb, wide), _F32)], axis=0)
            lower = lower.astype(_BF16)
        w_ref[i] = jnp.concatenate([wq, lower], axis=0)
        m_ref[i] = jnp.full(m_ref.shape[1:], NEG_BIG, _F32)
        l_ref[i] = jnp.zeros(l_ref.shape[1:], _F32)
        acc_ref[i] = jnp.zeros(acc_ref.shape[1:], _F32)
        return carry

    lax.fori_loop(0, nq, setup, 0)

    def sweep(n_units, unit, causal):
        def scores(t, slot):
            q, kb = unit(t)
            start = pl.multiple_of(kb * tk, tk)
            ka = jnp.concatenate([k_ref[pl.ds(start, tk), :], kx_ref[pl.ds(start, tk), :]], axis=1)
            st = jnp.dot(ka, w_ref[q], preferred_element_type=_F32)
            if causal:
                key = start + lax.broadcasted_iota(jnp.int32, (tk, wide), 0)
                qry = q * tq + lax.broadcasted_iota(jnp.int32, (tk, wide), 1) % tq
                st = jnp.where(key <= qry, st, NEG_BIG)
            st_ref[slot] = st

        def softmax(t, slot):
            q, _ = unit(t)
            rows = SOFTMAX_ROWS
            m_prev = m_ref[q][0:1, :]
            part = None
            for r in range(0, tk, rows):
                blk = st_ref[slot, r:r + rows, :].reshape(rows // 8, 8, wide)
                cur = jnp.max(blk, axis=0)
                part = cur if part is None else jnp.maximum(part, cur)
            m_new = jnp.maximum(m_prev, jnp.max(part, axis=0, keepdims=True))
            for r in range(0, tk, rows):
                p_ref[slot, r:r + rows, :] = jnp.exp2(st_ref[slot, r:r + rows, :] - m_new).astype(_BF16)
            m_ref[q] = jnp.broadcast_to(m_new, m_ref.shape[1:])
            alpha_ref[slot] = jnp.broadcast_to(jnp.exp2(m_prev - m_new), alpha_ref.shape[1:])

        def values(t, slot):
            q, kb = unit(t)
            start = pl.multiple_of(kb * tk, tk)
            ones = jnp.ones((ONES_ROWS, tk), _BF16)
            for h in range(HEADS_PER_STEP):
                cols = slice(h * tq, (h + 1) * tq)
                feats = slice(h * HEAD_DIM, (h + 1) * HEAD_DIM)
                lhs = jnp.concatenate([vt_ref[feats, pl.ds(start, tk)], ones], axis=0)
                pv = jnp.dot(lhs, p_ref[slot, :, cols], preferred_element_type=_F32)
                alpha = alpha_ref[slot, 0:1, cols]
                acc_ref[q, feats, :] = acc_ref[q, feats, :] * alpha + pv[0:HEAD_DIM]
                l_ref[q, :, cols] = jnp.broadcast_to(
                    alpha * l_ref[q, 0:1, cols] + pv[HEAD_DIM:HEAD_DIM + 1], (8, tq))

        def step(base, c, do_scores, do_softmax, do_values):
            u = base + c
            if do_scores:
                scores(u, c % PIPE_UNITS)
            if do_values:
                values(u - 2 * SOFTMAX_LAG, (c - 2 * SOFTMAX_LAG) % PIPE_UNITS)
            if do_softmax:
                softmax(u - SOFTMAX_LAG, (c - SOFTMAX_LAG) % PIPE_UNITS)

        for c in range(PIPE_UNITS):
            step(0, c, True, c >= SOFTMAX_LAG, c >= 2 * SOFTMAX_LAG)

        def group(j, carry):
            for c in range(PIPE_UNITS):
                step(j * PIPE_UNITS, c, True, True, True)
            return carry

        lax.fori_loop(1, n_units // PIPE_UNITS, group, 0)
        for c in range(2 * SOFTMAX_LAG):
            step(n_units, c, False, c < SOFTMAX_LAG, True)

    sweep(nq, lambda t: (t, (t * tq) // tk), True)
    if n_full:
        sweep(n_full, lambda t: (uq_ref[t], uk_ref[t]), False)

    def finish(i, carry):
        out_t = jnp.concatenate(
            [acc_ref[i, h * HEAD_DIM:(h + 1) * HEAD_DIM, :] / l_ref[i, 0:1, h * tq:(h + 1) * tq]
             for h in range(HEADS_PER_STEP)], axis=0)
        o_ref[pl.ds(pl.multiple_of(i * tq, tq), tq), :] = out_t.T.astype(o_ref.dtype)
        return carry

    lax.fori_loop(0, nq, finish, 0, unroll=2)


def _attention(qt, k, kx, vt, aux, *, mode):
    b, width, s = qt.shape
    tq, tk = ATTN_Q, ATTN_K
    nq = s // tq
    nb = s // MOBA_BLOCK
    wide = HEADS_PER_STEP * tq
    full_units = [(i, n) for i in range(nq) for n in range((i * tq) // tk)]
    assert nq % PIPE_UNITS == 0 and len(full_units) % PIPE_UNITS == 0 and PIPE_UNITS >= 2 * SOFTMAX_LAG
    unit_q = jnp.asarray([u[0] for u in full_units] or [0], jnp.int32)
    unit_k = jnp.asarray([u[1] for u in full_units] or [0], jnp.int32)
    if mode == "fox":
        aux_spec = pl.BlockSpec(aux.shape, lambda bi, hp, uq, uk: (0, 0))
        kx_spec = pl.BlockSpec((None, s, LANES), lambda bi, hp, uq, uk: (bi, 0, 0))
    else:
        aux_spec = pl.BlockSpec((None, LANES, LANES), lambda bi, hp, uq, uk: (bi, 0, hp))
        kx_spec = pl.BlockSpec(kx.shape, lambda bi, hp, uq, uk: (0, 0))
    return pl.pallas_call(
        functools.partial(_attn_body, mode=mode, tq=tq, tk=tk, nb=nb, n_full=len(full_units)),
        grid_spec=pltpu.PrefetchScalarGridSpec(
            num_scalar_prefetch=2,
            grid=(b, width // LANES),
            in_specs=[
                pl.BlockSpec((None, LANES, s), lambda bi, hp, uq, uk: (bi, hp, 0)),
                pl.BlockSpec((None, s, LANES), lambda bi, hp, uq, uk: (bi, 0, hp)),
                kx_spec,
                pl.BlockSpec((None, LANES, s), lambda bi, hp, uq, uk: (bi, hp, 0)),
                aux_spec,
            ],
            out_specs=pl.BlockSpec((None, s, LANES), lambda bi, hp, uq, uk: (bi, 0, hp)),
            scratch_shapes=[
                pltpu.VMEM((nq, 2 * LANES, wide), _BF16),
                pltpu.VMEM((nq, 8, wide), _F32),
                pltpu.VMEM((nq, 8, wide), _F32),
                pltpu.VMEM((nq, LANES, tq), _F32),
                pltpu.VMEM((PIPE_UNITS, tk, wide), _F32),
                pltpu.VMEM((PIPE_UNITS, tk, wide), _BF16),
                pltpu.VMEM((PIPE_UNITS, 8, wide), _F32),
            ],
        ),
        out_shape=jax.ShapeDtypeStruct((b, s, width), _BF16),
        compiler_params=_params("arbitrary", "arbitrary"),
        name=mode,
    )(unit_q, unit_k, qt, k, kx, vt, aux)


def _pack_bf16_pair(a, b):
    lo = pltpu.bitcast(a.astype(_BF16).astype(_F32), jnp.uint32) >> 16
    hi = pltpu.bitcast(b.astype(_BF16).astype(_F32), jnp.uint32) & jnp.uint32(0xFFFF0000)
    return lo | hi


def _unpack_bf16_pair(u):
    lo = pltpu.bitcast(u << 16, _F32).astype(_BF16)
    hi = pltpu.bitcast(u & jnp.uint32(0xFFFF0000), _F32).astype(_BF16)
    return jnp.concatenate([lo, hi], axis=1)


def _postattn_body(yf_ref, ym_ref, gates_ref, x_ref, wfb_ref, wmb_ref, wout_ref, g_ref, wr_ref, br_ref,
                   x1_ref, h2_ref, route_ref, counts_ref, carry_ref, *, tm, d):
    step = pl.program_id(0)
    ya = jnp.dot(yf_ref[...], wfb_ref[...], preferred_element_type=_F32)
    yb = jnp.dot(ym_ref[...], wmb_ref[...], preferred_element_type=_F32)
    mixed = gates_ref[:, 0:d].astype(_F32) * ya + gates_ref[:, d:2 * d].astype(_F32) * yb
    x1 = x_ref[...] + jnp.dot(mixed.astype(_BF16), wout_ref[...], preferred_element_type=_F32)
    x1_ref[...] = x1
    h2 = _rms(x1, g_ref[...])
    h2_ref[...] = _pack_bf16_pair(h2[:, 0:d // 2], h2[:, d // 2:d])

    logits = jnp.dot(h2.astype(_BF16), wr_ref[...], preferred_element_type=_F32) + br_ref[...]
    lane = lax.broadcasted_iota(jnp.int32, (tm, LANES), 1)
    is_group = (lane >= N_EXPERTS) & (lane < N_EXPERTS + N_GROUPS)
    gl = jnp.where(is_group, logits, -jnp.inf)
    ge = jnp.exp(gl - jnp.max(gl, axis=-1, keepdims=True))
    gp = ge / jnp.sum(ge, axis=-1, keepdims=True)
    g_top = jnp.max(gp, axis=-1, keepdims=True)
    g_idx = jnp.min(jnp.where(is_group & (gp == g_top), lane, 2 * LANES), axis=-1,
                    keepdims=True) - N_EXPERTS
    in_group = (lane < N_EXPERTS) & ((lane // EXPERTS_PER_GROUP) == g_idx)
    fl = jnp.where(in_group, logits, -jnp.inf)
    f1 = jnp.max(fl, axis=-1, keepdims=True)
    e1 = jnp.min(jnp.where(fl == f1, lane, 2 * LANES), axis=-1, keepdims=True)
    fl2 = jnp.where(lane == e1, -jnp.inf, fl)
    f2 = jnp.max(fl2, axis=-1, keepdims=True)
    e2 = jnp.min(jnp.where(fl2 == f2, lane, 2 * LANES), axis=-1, keepdims=True)
    t2 = jnp.exp(f2 - f1)
    w1 = g_top * (1.0 / (1.0 + t2))
    w2 = g_top * (t2 / (1.0 + t2))

    @pl.when(step == 0)
    def _():
        carry_ref[...] = jnp.zeros_like(carry_ref)

    onehot = ((lane == e1) | (lane == e2)).astype(_F32)
    rr = lax.broadcasted_iota(jnp.int32, (tm, tm), 0)
    cc = lax.broadcasted_iota(jnp.int32, (tm, tm), 1)
    before = jnp.dot((cc < rr).astype(_BF16), onehot.astype(_BF16), preferred_element_type=_F32)
    before = before + carry_ref[0:1, :]
    r1 = jnp.sum(jnp.where(lane == e1, before, 0.0), axis=-1, keepdims=True)
    r2 = jnp.sum(jnp.where(lane == e2, before, 0.0), axis=-1, keepdims=True)
    total = carry_ref[0:1, :] + jnp.sum(onehot, axis=0, keepdims=True)
    carry_ref[...] = jnp.broadcast_to(total, carry_ref.shape)
    counts_ref[...] = jnp.broadcast_to(total, counts_ref.shape)

    slab = jnp.where(lane == 0, w1, 0.0)
    slab = jnp.where(lane == 1, w2, slab)
    slab = jnp.where(lane == 2, e1.astype(_F32), slab)
    slab = jnp.where(lane == 3, e2.astype(_F32), slab)
    slab = jnp.where(lane == 4, r1, slab)
    slab = jnp.where(lane == 5, r2, slab)
    route_ref[...] = slab


def _postattn(yf, ym, gates, x, wfb, wmb, wout, g, wr, br, *, tm):
    t, d = x.shape
    width = yf.shape[1]
    row = lambda cols: pl.BlockSpec((tm, cols), lambda i: (i, 0))
    return pl.pallas_call(
        functools.partial(_postattn_body, tm=tm, d=d),
        grid=(t // tm,),
        in_specs=[row(width), row(width), row(2 * d), row(d),
                  _const_spec(wfb.shape), _const_spec(wmb.shape), _const_spec(wout.shape),
                  _const_spec((1, d)), _const_spec(wr.shape), _const_spec(br.shape)],
        out_specs=[row(d), row(d // 2), row(LANES), _const_spec((8, LANES))],
        out_shape=[jax.ShapeDtypeStruct((t, d), _F32),
                   jax.ShapeDtypeStruct((t, d // 2), jnp.uint32),
                   jax.ShapeDtypeStruct((t, LANES), _F32),
                   jax.ShapeDtypeStruct((8, LANES), _F32)],
        scratch_shapes=[pltpu.VMEM((8, LANES), _F32)],
        compiler_params=_params("arbitrary"),
        name="postattn",
    )(yf, ym, gates, x, wfb, wmb, wout, g, wr, br)


def _dispatch_body(dest_ref, h2_ref, xs_in_ref, xs_ref, sem, *, tm):
    del xs_in_ref

    def row_copy(r, k):
        return pltpu.make_async_copy(h2_ref.at[pl.ds(r, 1)],
                                     xs_ref.at[pl.ds(dest_ref[2 * r + k], 1)], sem)

    def start(r, carry):
        row_copy(r, 0).start(priority=0)
        row_copy(r, 1).start(priority=1)
        return carry

    lax.fori_loop(0, tm, start, 0, unroll=DMA_UNROLL)
    for k in range(2):
        pltpu.make_async_copy(h2_ref, xs_ref.at[pl.ds(0, tm)], sem).wait()


def _dispatch(dest_flat, h2p, xs0, *, tm):
    t, cols = h2p.shape
    return pl.pallas_call(
        functools.partial(_dispatch_body, tm=tm),
        grid=(t // tm,),
        in_specs=[pl.BlockSpec((2 * tm,), lambda i: (i,), memory_space=pltpu.SMEM),
                  pl.BlockSpec((tm, cols), lambda i: (i, 0)),
                  pl.BlockSpec(memory_space=pl.ANY)],
        out_specs=pl.BlockSpec(memory_space=pl.ANY),
        out_shape=jax.ShapeDtypeStruct(xs0.shape, xs0.dtype),
        scratch_shapes=[pltpu.SemaphoreType.DMA(())],
        input_output_aliases={2: 0},
        compiler_params=_params("arbitrary"),
        name="dispatch",
    )(dest_flat, h2p, xs0)


def _experts_body(be_ref, nused_ref, xs_ref, wg_ref, wu_ref, wd_ref, out_ref, wgb_ref, wub_ref, wdb_ref):
    j = pl.program_id(0)
    used = j < nused_ref[0]

    @pl.when(jnp.logical_not(used))
    def _():
        out_ref[...] = jnp.zeros_like(out_ref)

    @pl.when(used & ((j == 0) | (be_ref[j] != be_ref[jnp.maximum(j - 1, 0)])))
    def _():
        wgb_ref[...] = wg_ref[...].astype(_BF16)
        wub_ref[...] = wu_ref[...].astype(_BF16)
        wdb_ref[...] = wd_ref[...].astype(_BF16)

    @pl.when(used)
    def _():
        rows = _unpack_bf16_pair(xs_ref[...])
        gate = jnp.dot(rows, wgb_ref[...], preferred_element_type=_F32)
        up = jnp.dot(rows, wub_ref[...], preferred_element_type=_F32)
        act = (gate * jax.nn.sigmoid(gate)) * up
        out_ref[...] = jnp.dot(act.astype(_BF16), wdb_ref[...], preferred_element_type=_F32)


def _experts(block_expert, n_used, xs, wg, wu, wd, *, te):
    n_rows, half = xs.shape
    _, d, de = wg.shape

    def blk(j, be, nu):
        return jnp.minimum(j, nu[0] - 1)

    return pl.pallas_call(
        _experts_body,
        grid_spec=pltpu.PrefetchScalarGridSpec(
            num_scalar_prefetch=2,
            grid=(n_rows // te,),
            in_specs=[pl.BlockSpec((te, half), lambda j, be, nu: (blk(j, be, nu), 0)),
                      pl.BlockSpec((None, d, de), lambda j, be, nu: (be[blk(j, be, nu)], 0, 0)),
                      pl.BlockSpec((None, d, de), lambda j, be, nu: (be[blk(j, be, nu)], 0, 0)),
                      pl.BlockSpec((None, de, d), lambda j, be, nu: (be[blk(j, be, nu)], 0, 0))],
            out_specs=pl.BlockSpec((te, d), lambda j, be, nu: (j, 0)),
            scratch_shapes=[pltpu.VMEM((d, de), _BF16), pltpu.VMEM((d, de), _BF16),
                            pltpu.VMEM((de, d), _BF16)],
        ),
        out_shape=jax.ShapeDtypeStruct((n_rows, d), _F32),
        compiler_params=_params("arbitrary"),
        name="experts",
    )(block_expert, n_used, xs, wg, wu, wd)


def _sc_gather(table, idx):
    n, = idx.shape
    width = table.shape[1]
    workers = SC_CORES * SC_SUBCORES
    per_worker = n // workers
    chunks = per_worker // SC_CHUNK
    assert n % (workers * SC_CHUNK) == 0
    mesh = plsc.VectorSubcoreMesh(core_axis_name="c", subcore_axis_name="s",
                                  num_cores=SC_CORES, num_subcores=SC_SUBCORES)

    @functools.partial(
        pl.kernel, out_type=jax.ShapeDtypeStruct((n, width), table.dtype), mesh=mesh,
        scratch_types=[pltpu.VMEM((per_worker,), jnp.int32),
                       pltpu.VMEM((SC_CHUNK, width), table.dtype),
                       pltpu.SemaphoreType.DMA],
        name="sc_gather")
    def gather(table_hbm, idx_hbm, out_hbm, idx_v, rows_v, sem):
        worker = lax.axis_index("s") * SC_CORES + lax.axis_index("c")
        base = worker * per_worker
        pltpu.sync_copy(idx_hbm.at[pl.ds(base, per_worker)], idx_v)

        def chunk(j, carry):
            off = pl.multiple_of(j * SC_CHUNK, SC_CHUNK)
            pltpu.async_copy(table_hbm.at[idx_v.at[pl.ds(off, SC_CHUNK)]], rows_v, sem).wait()
            pltpu.sync_copy(rows_v, out_hbm.at[pl.ds(base + off, SC_CHUNK)])
            return carry

        lax.fori_loop(0, chunks, chunk, 0)

    return gather(table, idx)


def _final_body(x1_ref, route_ref, p_ref, rows_ref, gp_ref, wpg_ref, wpp_ref, gf_ref, o_ref, *, d):
    emb = jnp.dot(p_ref[...].astype(_BF16), wpp_ref[...], preferred_element_type=_F32)
    route = route_ref[...]
    x2 = x1_ref[...] + (rows_ref[:, 0:d] * route[:, 0:1] + rows_ref[:, d:2 * d] * route[:, 1:2])
    gate = jax.nn.sigmoid(jnp.dot(_rms(x2, gp_ref[...]).astype(_BF16), wpg_ref[...],
                                  preferred_element_type=_F32))
    o_ref[...] = _rms(x2 + gate * emb, gf_ref[...])


def _final(x1, route, p, pair_rows, gp, wpg, wpp, gf, *, tm):
    t, d = x1.shape
    ple = p.shape[1]
    row = lambda cols: pl.BlockSpec((tm, cols), lambda i: (i, 0))
    return pl.pallas_call(
        functools.partial(_final_body, d=d),
        grid=(t // tm,),
        in_specs=[row(d), row(LANES), row(ple), row(2 * d),
                  _const_spec((1, d)), _const_spec(wpg.shape), _const_spec(wpp.shape),
                  _const_spec((1, d))],
        out_specs=row(d),
        out_shape=jax.ShapeDtypeStruct((t, d), _F32),
        compiler_params=_params("arbitrary"),
        name="final",
    )(x1, route, p, pair_rows, gp, wpg, wpp, gf)


def _rope_tables(s):
    half = HEAD_DIM // 2
    inv = 1.0 / (ROPE_THETA ** (jnp.arange(0, HEAD_DIM, 2, dtype=_F32) / HEAD_DIM))
    ang = jnp.arange(s, dtype=_F32)[:, None] * inv[None, :]
    cos, sin = jnp.cos(ang), jnp.sin(ang)
    reps = LANES // half
    cos_t = jnp.tile(cos, (1, reps))
    sin_t = jnp.tile(jnp.concatenate([-sin, sin], axis=1), (1, reps // 2))
    return cos_t, sin_t


def _layer(x, p, attn_norm, w_in, b_forget, w_fox_branch, w_moba_branch, w_out, moe_norm,
           w_group, b_group, w_fine, b_fine, w_gate, w_up, w_down, ple_norm, w_ple_gate, w_ple_proj,
           final_gain):
    b, s, d = x.shape
    t = b * s
    width = N_FOX_HEADS * HEAD_DIM
    assert N_MOBA_HEADS * HEAD_DIM == width and MOBA_BLOCK % ATTN_Q == 0 and ATTN_K % MOBA_BLOCK == 0
    assert BIAS_PIECES * N_FOX_HEADS <= LANES and s % ATTN_K == 0
    assert s % ROW_TILE == 0 and ROW_TILE % MOBA_BLOCK == 0 and s // MOBA_BLOCK <= LANES
    assert t % FINAL_TILE == 0 and N_EXPERTS + N_GROUPS <= LANES

    qkv_cols = 6 * width
    f_cols = N_FOX_HEADS
    chunk = lambda c: w_in[:, c * width:(c + 1) * width]
    w_rows = jnp.concatenate([chunk(1), chunk(4), w_in[:, qkv_cols + f_cols:]], axis=1).astype(_BF16)
    w_cols = jnp.concatenate([chunk(0), chunk(2), chunk(3), chunk(5)], axis=1).T.astype(_BF16)
    wf = jnp.zeros((d, LANES), _BF16).at[:, :f_cols].set(w_in[:, qkv_cols:qkv_cols + f_cols].astype(_BF16))
    bfp = jnp.zeros((1, LANES), _F32).at[0, :f_cols].set(b_forget.astype(_F32))
    cos, sin = _rope_tables(s)

    qft, kf, vft, qmt, km, vmt, gates, cb, kmean = _inproj(
        x, attn_norm.reshape(1, d), w_rows, w_cols, wf, bfp, cos, sin, cos.T, sin.T, tm=ROW_TILE)
    y_fox = _attention(qft, kf, cb, vft, jnp.zeros((8, LANES), _F32), mode="fox")
    block_of_key = jnp.arange(s, dtype=jnp.int32)[:, None] // MOBA_BLOCK
    block_onehot = (block_of_key == jnp.arange(LANES, dtype=jnp.int32)[None, :]).astype(_BF16)
    y_moba = _attention(qmt, km, block_onehot, vmt, kmean, mode="moba")

    wr = jnp.zeros((d, LANES), _BF16)
    wr = wr.at[:, :N_EXPERTS].set(w_fine.astype(_BF16))
    wr = wr.at[:, N_EXPERTS:N_EXPERTS + N_GROUPS].set(w_group.astype(_BF16))
    br = jnp.zeros((1, LANES), _F32)
    br = br.at[0, :N_EXPERTS].set(b_fine.astype(_F32))
    br = br.at[0, N_EXPERTS:N_EXPERTS + N_GROUPS].set(b_group.astype(_F32))
    x1, h2p, route, counts = _postattn(
        y_fox.reshape(t, width), y_moba.reshape(t, width), gates.reshape(t, 2 * d), x.reshape(t, d),
        w_fox_branch.astype(_BF16), w_moba_branch.astype(_BF16), w_out.astype(_BF16),
        moe_norm.reshape(1, d), wr, br, tm=ROW_TILE)

    te = EXPERT_TILE
    n_blk = (2 * t) // te + N_EXPERTS
    counts = counts[0, :N_EXPERTS].astype(jnp.int32)
    blocks_per = (counts + te - 1) // te
    block_end = jnp.cumsum(blocks_per)
    row_start = (block_end - blocks_per) * te
    experts = route[:, 2:4].astype(jnp.int32)
    dest = (row_start[experts] + route[:, 4:6].astype(jnp.int32)).reshape(-1)
    n_used = block_end[-1:].astype(jnp.int32)
    block_ids = jnp.arange(n_blk, dtype=jnp.int32)
    block_expert = jnp.minimum(
        jnp.sum((block_ids[:, None] >= block_end[None, :]).astype(jnp.int32), axis=1), N_EXPERTS - 1)

    xs = _dispatch(dest, h2p, jnp.zeros((n_blk * te, d // 2), jnp.uint32), tm=ROW_TILE)
    rows = _experts(block_expert, n_used, xs, w_gate, w_up, w_down, te=te)
    pair_rows = _sc_gather(rows, dest).reshape(t, 2 * d)
    out = _final(x1, route, p.reshape(t, -1), pair_rows, ple_norm.reshape(1, d),
                 w_ple_gate.astype(_BF16), w_ple_proj.astype(_BF16), final_gain.reshape(1, d),
                 tm=FINAL_TILE)
    return out.reshape(b, s, d)


def kernel(x, p, attn_norm, w_in, b_forget, w_fox_branch, w_moba_branch, w_out, moe_norm, w_group,
           b_group, w_fine, b_fine, w_gate, w_up, w_down, ple_norm, w_ple_gate, w_ple_proj, final_norm):
    depth = p.shape[0]
    assert depth == 1, "the final norm is fused into the last layer's kernel"
    i = 0
    return _layer(x, p[i], attn_norm[i], w_in[i], b_forget[i], w_fox_branch[i], w_moba_branch[i],
                  w_out[i], moe_norm[i], w_group[i], b_group[i], w_fine[i], b_fine[i], w_gate[i],
                  w_up[i], w_down[i], ple_norm[i], w_ple_gate[i], w_ple_proj[i], final_norm)
```

```python
import functools

import jax
import jax.numpy as jnp
from jax import lax
from jax.experimental import pallas as pl
from jax.experimental.pallas import tpu as pltpu
from jax.experimental.pallas import tpu_sc as plsc

HEAD_DIM = 64
N_FOX_HEADS = 8
N_MOBA_HEADS = 8
MOBA_BLOCK = 256
MOBA_TOPK = 3
ROPE_THETA = 10000.0
N_GROUPS = 4
EXPERTS_PER_GROUP = 8
N_EXPERTS = N_GROUPS * EXPERTS_PER_GROUP
RMS_EPS = 1e-6

LANES = 128
HEADS_PER_STEP = LANES // HEAD_DIM
ATTN_Q = 256
ATTN_K = 256
PIPE_UNITS = 8
SOFTMAX_LAG = 4
ONES_ROWS = 16
SOFTMAX_ROWS = 64
BIAS_PIECES = 3
ROW_TILE = 512
SC_CORES = 2
SC_SUBCORES = 16
SC_CHUNK = 64
SC_SCATTER_CHUNK = 128
FINAL_TILE = 256
EXPERT_TILE = 512
LOG2_E = 1.4426950408889634
NEG_BIG = -1e30
VMEM_LIMIT = 48 * 1024 * 1024

_BF16 = jnp.bfloat16
_F32 = jnp.float32


def _params(*sem):
    return pltpu.CompilerParams(dimension_semantics=sem, vmem_limit_bytes=VMEM_LIMIT)


def _rms(x, g):
    return x * lax.rsqrt(jnp.mean(x * x, axis=-1, keepdims=True) + RMS_EPS) * g


def _const_spec(shape):
    return pl.BlockSpec(shape, lambda *_: (0,) * len(shape))


def _inproj_body(x_ref, g_ref, w_ref, wt_ref, wf_ref, bf_ref, cos_ref, sin_ref, cost_ref, sint_ref,
                 qft_ref, kf_ref, vft_ref, qmt_ref, km_ref, vmt_ref, gates_ref, cb_ref, kmean_ref,
                 carry_ref, *, tm, width, gate_cols):
    j = pl.program_id(1)
    h = _rms(x_ref[...], g_ref[...])
    hb = h.astype(_BF16)
    hbt = h.T.astype(_BF16)
    reps = width // LANES
    half = HEAD_DIM // 2

    def proj(c):
        return jnp.dot(hb, w_ref[:, c * width:(c + 1) * width], preferred_element_type=_F32)

    def proj_t(c):
        return jnp.dot(wt_ref[c * width:(c + 1) * width, :], hbt, preferred_element_type=_F32)

    def rope(t):
        cos = jnp.concatenate([cos_ref[...]] * reps, axis=1)
        sin = jnp.concatenate([sin_ref[...]] * reps, axis=1)
        first = (lax.broadcasted_iota(jnp.int32, (1, width), 1) % HEAD_DIM) < half
        partner = jnp.where(first, pltpu.roll(t, width - half, 1), pltpu.roll(t, half, 1))
        return t * cos + partner * sin

    def rope_t(t):
        cos = jnp.concatenate([cost_ref[...]] * reps, axis=0)
        sin = jnp.concatenate([sint_ref[...]] * reps, axis=0)
        first = (lax.broadcasted_iota(jnp.int32, (width, 1), 0) % HEAD_DIM) < half
        partner = jnp.where(first, pltpu.roll(t, width - half, 0), pltpu.roll(t, half, 0))
        return t * cos + partner * sin

    scale = HEAD_DIM ** -0.5 * LOG2_E
    qft_ref[...] = (proj_t(0) * scale).astype(_BF16)
    vft_ref[...] = proj_t(1).astype(_BF16)
    qmt_ref[...] = (rope_t(proj_t(2)) * scale).astype(_BF16)
    vmt_ref[...] = proj_t(3).astype(_BF16)
    kf_ref[...] = proj(0).astype(_BF16)
    km = rope(proj(1))
    km_ref[...] = km.astype(_BF16)

    @pl.when(j == 0)
    def _():
        kmean_ref[...] = jnp.zeros_like(kmean_ref)
        carry_ref[...] = jnp.zeros_like(carry_ref)

    blocks = tm // MOBA_BLOCK
    for b in range(blocks):
        kmean_ref[pl.ds(j * blocks + b, 1), :] = jnp.mean(
            km[b * MOBA_BLOCK:(b + 1) * MOBA_BLOCK], axis=0, keepdims=True)

    for c in range(gate_cols // width):
        g = jnp.dot(hb, w_ref[:, (2 + c) * width:(3 + c) * width],
                    preferred_element_type=_F32)
        gates_ref[:, c * width:(c + 1) * width] = jax.nn.sigmoid(g).astype(_BF16)

    z = jnp.dot(hb, wf_ref[...], preferred_element_type=_F32) + bf_ref[...]
    log_f = jnp.minimum(z, 0.0) - jnp.log1p(jnp.exp(-jnp.abs(z)))
    r = lax.broadcasted_iota(jnp.int32, (LANES, LANES), 0)
    c = lax.broadcasted_iota(jnp.int32, (LANES, LANES), 1)
    tri = (c <= r).astype(_BF16)
    offset = carry_ref[0:1, :]
    blocks_cum = []
    for blk in range(tm // LANES):
        rest = log_f[blk * LANES:(blk + 1) * LANES]
        within = jnp.zeros((LANES, LANES), _F32)
        for _ in range(BIAS_PIECES):
            part = rest.astype(_BF16)
            rest = rest - part.astype(_F32)
            within = within + jnp.dot(tri, part, preferred_element_type=_F32)
        blocks_cum.append(within + offset)
        offset = offset + within[LANES - 1:LANES, :]
    cum = jnp.concatenate(blocks_cum, axis=0)
    carry_ref[...] = jnp.broadcast_to(offset, carry_ref.shape)

    head_lane = lax.broadcasted_iota(jnp.int32, (1, LANES), 1) < N_FOX_HEADS
    rest = jnp.where(head_lane, -cum * LOG2_E, 0.0)
    placed = jnp.zeros((tm, LANES), _F32)
    for piece in range(BIAS_PIECES):
        part = rest.astype(_BF16).astype(_F32)
        rest = rest - part
        placed = placed + (pltpu.roll(part, N_FOX_HEADS * piece, 1) if piece else part)
    cb_ref[...] = placed.astype(_BF16)


def _inproj(x, g, w, wt, wf, bfp, cos, sin, cos_t, sin_t, *, tm):
    b, s, d = x.shape
    width = N_FOX_HEADS * HEAD_DIM
    gate_cols = w.shape[1] - 2 * width
    act = jax.ShapeDtypeStruct((b, s, width), _BF16)
    act_t = jax.ShapeDtypeStruct((b, width, s), _BF16)
    act_spec = pl.BlockSpec((None, tm, width), lambda bi, j: (bi, j, 0))
    act_t_spec = pl.BlockSpec((None, width, tm), lambda bi, j: (bi, 0, j))
    return pl.pallas_call(
        functools.partial(_inproj_body, tm=tm, width=width, gate_cols=gate_cols),
        grid=(b, s // tm),
        in_specs=[
            pl.BlockSpec((None, tm, d), lambda bi, j: (bi, j, 0)),
            _const_spec((1, d)),
            _const_spec(w.shape),
            _const_spec(wt.shape),
            _const_spec(wf.shape),
            _const_spec(bfp.shape),
            pl.BlockSpec((tm, LANES), lambda bi, j: (j, 0)),
            pl.BlockSpec((tm, LANES), lambda bi, j: (j, 0)),
            pl.BlockSpec((LANES, tm), lambda bi, j: (0, j)),
            pl.BlockSpec((LANES, tm), lambda bi, j: (0, j)),
        ],
        out_specs=[act_t_spec, act_spec, act_t_spec, act_t_spec, act_spec, act_t_spec,
                   pl.BlockSpec((None, tm, gate_cols), lambda bi, j: (bi, j, 0)),
                   pl.BlockSpec((None, tm, LANES), lambda bi, j: (bi, j, 0)),
                   pl.BlockSpec((None, LANES, width), lambda bi, j: (bi, 0, 0))],
        out_shape=[act_t, act, act_t, act_t, act, act_t,
                   jax.ShapeDtypeStruct((b, s, gate_cols), _BF16),
                   jax.ShapeDtypeStruct((b, s, LANES), _BF16),
                   jax.ShapeDtypeStruct((b, LANES, width), _F32)],
        scratch_shapes=[pltpu.VMEM((8, LANES), _F32)],
        compiler_params=_params("arbitrary", "arbitrary"),
        name="inproj",
    )(x, g, w, wt, wf, bfp, cos, sin, cos_t, sin_t)


def _attn_body(uq_ref, uk_ref, qt_ref, k_ref, kx_ref, vt_ref, aux_ref, o_ref,
               w_ref, m_ref, l_ref, acc_ref, st_ref, p_ref, alpha_ref, *, mode, tq, tk, nb, n_full):
    hp = pl.program_id(1)
    wide = HEADS_PER_STEP * tq
    nq = w_ref.shape[0]
    feat = lax.broadcasted_iota(jnp.int32, (LANES, 1), 0)
    xr = lax.broadcasted_iota(jnp.int32, (LANES, wide), 0)
    xc = lax.broadcasted_iota(jnp.int32, (LANES, wide), 1)

    def setup(i, carry):
        qt = qt_ref[:, pl.ds(pl.multiple_of(i * tq, tq), tq)]
        zero = jnp.zeros_like(qt)
        wq = jnp.concatenate([jnp.where(feat < HEAD_DIM, qt, zero),
                              jnp.where(feat < HEAD_DIM, zero, qt)], axis=1)
        if mode == "fox":
            head = hp * HEADS_PER_STEP + xc // tq
            lower = ((xr % N_FOX_HEADS == head) & (xr < BIAS_PIECES * N_FOX_HEADS)).astype(_BF16)
        else:
            own = (i * tq) // MOBA_BLOCK
            gate = jnp.dot(aux_ref[...].astype(_BF16), wq, preferred_element_type=_F32)[0:nb]
            blk = lax.broadcasted_iota(jnp.int32, (nb, wide), 0)
            gate = jnp.where(blk < own, gate, -jnp.inf)
            keep = blk == own
            for _ in range(MOBA_TOPK):
                mx = jnp.max(gate, axis=0, keepdims=True)
                cand = jnp.where((gate == mx) & (mx > -jnp.inf), blk, nb)
                pick = blk == jnp.min(cand, axis=0, keepdims=True)
                keep = keep | pick
                gate = jnp.where(pick, -jnp.inf, gate)
            lower = jnp.where(keep, 0.0, NEG_BIG)
            if nb < LANES:
                lower = jnp.concatenate([lower, jnp.zeros((LANES - nb, wide), _F32)], axis=0)
            lower = lower.astype(_BF16)
        w_ref[i] = jnp.concatenate([wq, lower], axis=0)
        m_ref[i] = jnp.full(m_ref.shape[1:], NEG_BIG, _F32)
        l_ref[i] = jnp.zeros(l_ref.shape[1:], _F32)
        acc_ref[i] = jnp.zeros(acc_ref.shape[1:], _F32)
        return carry

    lax.fori_loop(0, nq, setup, 0)

    def sweep(n_units, unit, causal):
        def scores(t, slot):
            q, kb = unit(t)
            start = pl.multiple_of(kb * tk, tk)
            ka = jnp.concatenate([k_ref[pl.ds(start, tk), :], kx_ref[pl.ds(start, tk), :]], axis=1)
            st = jnp.dot(ka, w_ref[q], preferred_element_type=_F32)
            if causal:
                key = start + lax.broadcasted_iota(jnp.int32, (tk, wide), 0)
                qry = q * tq + lax.broadcasted_iota(jnp.int32, (tk, wide), 1) % tq
                st = jnp.where(key <= qry, st, NEG_BIG)
            st_ref[slot] = st

        def softmax(t, slot):
            q, _ = unit(t)
            rows = SOFTMAX_ROWS
            m_prev = m_ref[q][0:1, :]
            part = None
            for r in range(0, tk, rows):
                blk = st_ref[slot, r:r + rows, :].reshape(rows // 8, 8, wide)
                cur = jnp.max(blk, axis=0)
                part = cur if part is None else jnp.maximum(part, cur)
            m_new = jnp.maximum(m_prev, jnp.max(part, axis=0, keepdims=True))
            for r in range(0, tk, rows):
                p_ref[slot, r:r + rows, :] = jnp.exp2(st_ref[slot, r:r + rows, :] - m_new).astype(_BF16)
            m_ref[q] = jnp.broadcast_to(m_new, m_ref.shape[1:])
            alpha_ref[slot] = jnp.broadcast_to(jnp.exp2(m_prev - m_new), alpha_ref.shape[1:])

        def values(t, slot):
            q, kb = unit(t)
            start = pl.multiple_of(kb * tk, tk)
            ones = jnp.ones((ONES_ROWS, tk), _BF16)
            for h in range(HEADS_PER_STEP):
                cols = slice(h * tq, (h + 1) * tq)
                feats = slice(h * HEAD_DIM, (h + 1) * HEAD_DIM)
                lhs = jnp.concatenate([vt_ref[feats, pl.ds(start, tk)], ones], axis=0)
                pv = jnp.dot(lhs, p_ref[slot, :, cols], preferred_element_type=_F32)
                alpha = alpha_ref[slot, 0:1, cols]
                acc_ref[q, feats, :] = acc_ref[q, feats, :] * alpha + pv[0:HEAD_DIM]
                l_ref[q, :, cols] = jnp.broadcast_to(
                    alpha * l_ref[q, 0:1, cols] + pv[HEAD_DIM:HEAD_DIM + 1], (8, tq))

        def step(base, c, do_scores, do_softmax, do_values):
            u = base + c
            if do_scores:
                scores(u, c % PIPE_UNITS)
            if do_values:
                values(u - 2 * SOFTMAX_LAG, (c - 2 * SOFTMAX_LAG) % PIPE_UNITS)
            if do_softmax:
                softmax(u - SOFTMAX_LAG, (c - SOFTMAX_LAG) % PIPE_UNITS)

        for c in range(PIPE_UNITS):
            step(0, c, True, c >= SOFTMAX_LAG, c >= 2 * SOFTMAX_LAG)

        def group(j, carry):
            for c in range(PIPE_UNITS):
                step(j * PIPE_UNITS, c, True, True, True)
            return carry

        lax.fori_loop(1, n_units // PIPE_UNITS, group, 0)
        for c in range(2 * SOFTMAX_LAG):
            step(n_units, c, False, c < SOFTMAX_LAG, True)

    sweep(nq, lambda t: (t, (t * tq) // tk), True)
    if n_full:
        sweep(n_full, lambda t: (uq_ref[t], uk_ref[t]), False)

    def finish(i, carry):
        out_t = jnp.concatenate(
            [acc_ref[i, h * HEAD_DIM:(h + 1) * HEAD_DIM, :] / l_ref[i, 0:1, h * tq:(h + 1) * tq]
             for h in range(HEADS_PER_STEP)], axis=0)
        o_ref[pl.ds(pl.multiple_of(i * tq, tq), tq), :] = out_t.T.astype(o_ref.dtype)
        return carry

    lax.fori_loop(0, nq, finish, 0, unroll=2)


def _attention(qt, k, kx, vt, aux, *, mode):
    b, width, s = qt.shape
    tq, tk = ATTN_Q, ATTN_K
    nq = s // tq
    nb = s // MOBA_BLOCK
    wide = HEADS_PER_STEP * tq
    full_units = [(i, n) for i in range(nq) for n in range((i * tq) // tk)]
    assert nq % PIPE_UNITS == 0 and len(full_units) % PIPE_UNITS == 0 and PIPE_UNITS >= 2 * SOFTMAX_LAG
    unit_q = jnp.asarray([u[0] for u in full_units] or [0], jnp.int32)
    unit_k = jnp.asarray([u[1] for u in full_units] or [0], jnp.int32)
    if mode == "fox":
        aux_spec = pl.BlockSpec(aux.shape, lambda bi, hp, uq, uk: (0, 0))
        kx_spec = pl.BlockSpec((None, s, LANES), lambda bi, hp, uq, uk: (bi, 0, 0))
    else:
        aux_spec = pl.BlockSpec((None, LANES, LANES), lambda bi, hp, uq, uk: (bi, 0, hp))
        kx_spec = pl.BlockSpec(kx.shape, lambda bi, hp, uq, uk: (0, 0))
    return pl.pallas_call(
        functools.partial(_attn_body, mode=mode, tq=tq, tk=tk, nb=nb, n_full=len(full_units)),
        grid_spec=pltpu.PrefetchScalarGridSpec(
            num_scalar_prefetch=2,
            grid=(b, width // LANES),
            in_specs=[
                pl.BlockSpec((None, LANES, s), lambda bi, hp, uq, uk: (bi, hp, 0)),
                pl.BlockSpec((None, s, LANES), lambda bi, hp, uq, uk: (bi, 0, hp)),
                kx_spec,
                pl.BlockSpec((None, LANES, s), lambda bi, hp, uq, uk: (bi, hp, 0)),
                aux_spec,
            ],
            out_specs=pl.BlockSpec((None, s, LANES), lambda bi, hp, uq, uk: (bi, 0, hp)),
            scratch_shapes=[
                pltpu.VMEM((nq, 2 * LANES, wide), _BF16),
                pltpu.VMEM((nq, 8, wide), _F32),
                pltpu.VMEM((nq, 8, wide), _F32),
                pltpu.VMEM((nq, LANES, tq), _F32),
                pltpu.VMEM((PIPE_UNITS, tk, wide), _F32),
                pltpu.VMEM((PIPE_UNITS, tk, wide), _BF16),
                pltpu.VMEM((PIPE_UNITS, 8, wide), _F32),
            ],
        ),
        out_shape=jax.ShapeDtypeStruct((b, s, width), _BF16),
        compiler_params=_params("arbitrary", "arbitrary"),
        name=mode,
    )(unit_q, unit_k, qt, k, kx, vt, aux)


def _pack_bf16_pair(a, b):
    lo = pltpu.bitcast(a.astype(_BF16).astype(_F32), jnp.uint32) >> 16
    hi = pltpu.bitcast(b.astype(_BF16).astype(_F32), jnp.uint32) & jnp.uint32(0xFFFF0000)
    return lo | hi


def _unpack_bf16_pair(u):
    lo = pltpu.bitcast(u << 16, _F32)
    hi = pltpu.bitcast(u & jnp.uint32(0xFFFF0000), _F32)
    return jnp.concatenate([lo, hi], axis=1)


def _postattn_body(yf_ref, ym_ref, gates_ref, x_ref, wfb_ref, wmb_ref, wout_ref, g_ref, wr_ref, br_ref,
                   x1_ref, h2_ref, route_ref, counts_ref, carry_ref, *, tm, d):
    step = pl.program_id(0)
    ya = jnp.dot(yf_ref[...], wfb_ref[...], preferred_element_type=_F32)
    yb = jnp.dot(ym_ref[...], wmb_ref[...], preferred_element_type=_F32)
    mixed = gates_ref[:, 0:d].astype(_F32) * ya + gates_ref[:, d:2 * d].astype(_F32) * yb
    x1 = x_ref[...] + jnp.dot(mixed.astype(_BF16), wout_ref[...], preferred_element_type=_F32)
    x1_ref[...] = x1
    h2 = _rms(x1, g_ref[...])
    h2_ref[...] = _pack_bf16_pair(h2[:, 0:d // 2], h2[:, d // 2:d])

    logits = jnp.dot(h2.astype(_BF16), wr_ref[...], preferred_element_type=_F32) + br_ref[...]
    lane = lax.broadcasted_iota(jnp.int32, (tm, LANES), 1)
    is_group = (lane >= N_EXPERTS) & (lane < N_EXPERTS + N_GROUPS)
    gl = jnp.where(is_group, logits, -jnp.inf)
    ge = jnp.exp(gl - jnp.max(gl, axis=-1, keepdims=True))
    gp = ge / jnp.sum(ge, axis=-1, keepdims=True)
    g_top = jnp.max(gp, axis=-1, keepdims=True)
    g_idx = jnp.min(jnp.where(is_group & (gp == g_top), lane, 2 * LANES), axis=-1,
                    keepdims=True) - N_EXPERTS
    in_group = (lane < N_EXPERTS) & ((lane // EXPERTS_PER_GROUP) == g_idx)
    fl = jnp.where(in_group, logits, -jnp.inf)
    f1 = jnp.max(fl, axis=-1, keepdims=True)
    e1 = jnp.min(jnp.where(fl == f1, lane, 2 * LANES), axis=-1, keepdims=True)
    fl2 = jnp.where(lane == e1, -jnp.inf, fl)
    f2 = jnp.max(fl2, axis=-1, keepdims=True)
    e2 = jnp.min(jnp.where(fl2 == f2, lane, 2 * LANES), axis=-1, keepdims=True)
    t2 = jnp.exp(f2 - f1)
    w1 = g_top * (1.0 / (1.0 + t2))
    w2 = g_top * (t2 / (1.0 + t2))

    @pl.when(step == 0)
    def _():
        carry_ref[...] = jnp.zeros_like(carry_ref)

    onehot = ((lane == e1) | (lane == e2)).astype(_F32)
    rr = lax.broadcasted_iota(jnp.int32, (tm, tm), 0)
    cc = lax.broadcasted_iota(jnp.int32, (tm, tm), 1)
    before = jnp.dot((cc < rr).astype(_BF16), onehot.astype(_BF16), preferred_element_type=_F32)
    before = before + carry_ref[0:1, :]
    r1 = jnp.sum(jnp.where(lane == e1, before, 0.0), axis=-1, keepdims=True)
    r2 = jnp.sum(jnp.where(lane == e2, before, 0.0), axis=-1, keepdims=True)
    total = carry_ref[0:1, :] + jnp.sum(onehot, axis=0, keepdims=True)
    carry_ref[...] = jnp.broadcast_to(total, carry_ref.shape)
    counts_ref[...] = jnp.broadcast_to(total, counts_ref.shape)

    slab = jnp.where(lane == 0, w1, 0.0)
    slab = jnp.where(lane == 1, w2, slab)
    slab = jnp.where(lane == 2, e1.astype(_F32), slab)
    slab = jnp.where(lane == 3, e2.astype(_F32), slab)
    slab = jnp.where(lane == 4, r1, slab)
    slab = jnp.where(lane == 5, r2, slab)
    route_ref[...] = slab


def _postattn(yf, ym, gates, x, wfb, wmb, wout, g, wr, br, *, tm):
    t, d = x.shape
    width = yf.shape[1]
    row = lambda cols: pl.BlockSpec((tm, cols), lambda i: (i, 0))
    return pl.pallas_call(
        functools.partial(_postattn_body, tm=tm, d=d),
        grid=(t // tm,),
        in_specs=[row(width), row(width), row(2 * d), row(d),
                  _const_spec(wfb.shape), _const_spec(wmb.shape), _const_spec(wout.shape),
                  _const_spec((1, d)), _const_spec(wr.shape), _const_spec(br.shape)],
        out_specs=[row(d), row(d // 2), row(LANES), _const_spec((8, LANES))],
        out_shape=[jax.ShapeDtypeStruct((t, d), _F32),
                   jax.ShapeDtypeStruct((t, d // 2), jnp.uint32),
                   jax.ShapeDtypeStruct((t, LANES), _F32),
                   jax.ShapeDtypeStruct((8, LANES), _F32)],
        scratch_shapes=[pltpu.VMEM((8, LANES), _F32)],
        compiler_params=_params("arbitrary"),
        name="postattn",
    )(yf, ym, gates, x, wfb, wmb, wout, g, wr, br)


def _sc_scatter(rows, idx, n_out):
    n, width = rows.shape
    workers = SC_CORES * SC_SUBCORES
    per_worker = n // workers
    chunks = per_worker // SC_SCATTER_CHUNK
    assert n % (workers * SC_SCATTER_CHUNK) == 0 and idx.shape == (2, n // SC_SCATTER_CHUNK, SC_SCATTER_CHUNK)
    mesh = plsc.VectorSubcoreMesh(core_axis_name="c", subcore_axis_name="s",
                                  num_cores=SC_CORES, num_subcores=SC_SUBCORES)

    @functools.partial(
        pl.kernel, out_type=jax.ShapeDtypeStruct((n_out, width), rows.dtype), mesh=mesh,
        scratch_types=[pltpu.VMEM((2, chunks, SC_SCATTER_CHUNK), jnp.int32),
                       pltpu.VMEM((SC_SCATTER_CHUNK, width), rows.dtype),
                       pltpu.SemaphoreType.DMA],
        name="sc_scatter")
    def scatter(rows_hbm, idx_hbm, out_hbm, idx_v, rows_v, sem):
        worker = lax.axis_index("s") * SC_CORES + lax.axis_index("c")
        for k in range(2):
            pltpu.sync_copy(idx_hbm.at[k, pl.ds(worker * chunks, chunks)], idx_v.at[k])

        def chunk(j, carry):
            start = worker * per_worker + j * SC_SCATTER_CHUNK
            pltpu.sync_copy(rows_hbm.at[pl.ds(start, SC_SCATTER_CHUNK)], rows_v)
            for k in range(2):
                pltpu.async_copy(rows_v, out_hbm.at[idx_v.at[k, j]], sem).wait()
            return carry

        lax.fori_loop(0, chunks, chunk, 0)

    return scatter(rows, idx)


def _experts_body(be_ref, valid_ref, nused_ref, xs_ref, wg_ref, wu_ref, wd_ref, out_ref,
                  wgb_ref, wub_ref, wdb_ref):
    j = pl.program_id(0)
    used = j < nused_ref[0]

    @pl.when(jnp.logical_not(used))
    def _():
        out_ref[...] = jnp.zeros_like(out_ref)

    @pl.when(used & ((j == 0) | (be_ref[j] != be_ref[jnp.maximum(j - 1, 0)])))
    def _():
        wgb_ref[...] = wg_ref[...].astype(_BF16)
        wub_ref[...] = wu_ref[...].astype(_BF16)
        wdb_ref[...] = wd_ref[...].astype(_BF16)

    @pl.when(used)
    def _():
        row = lax.broadcasted_iota(jnp.int32, xs_ref.shape, 0)
        packed = jnp.where(row < valid_ref[j], xs_ref[...], jnp.uint32(0))
        rows = _unpack_bf16_pair(packed).astype(_BF16)
        gate = jnp.dot(rows, wgb_ref[...], preferred_element_type=_F32)
        up = jnp.dot(rows, wub_ref[...], preferred_element_type=_F32)
        act = (gate * jax.nn.sigmoid(gate)) * up
        out = jnp.dot(act.astype(_BF16), wdb_ref[...], preferred_element_type=_F32)
        half = out.shape[1] // 2
        out_ref[...] = _pack_bf16_pair(out[:, 0:half], out[:, half:])


def _experts(block_expert, block_valid, n_used, xs, wg, wu, wd, *, te):
    n_rows, half = xs.shape
    _, d, de = wg.shape

    def blk(j, be, bv, nu):
        return jnp.minimum(j, nu[0] - 1)

    def weight(j, be, bv, nu):
        return (be[blk(j, be, bv, nu)], 0, 0)

    return pl.pallas_call(
        _experts_body,
        grid_spec=pltpu.PrefetchScalarGridSpec(
            num_scalar_prefetch=3,
            grid=(n_rows // te,),
            in_specs=[pl.BlockSpec((te, half), lambda j, be, bv, nu: (blk(j, be, bv, nu), 0)),
                      pl.BlockSpec((None, d, de), weight),
                      pl.BlockSpec((None, d, de), weight),
                      pl.BlockSpec((None, de, d), weight)],
            out_specs=pl.BlockSpec((te, d // 2), lambda j, be, bv, nu: (j, 0)),
            scratch_shapes=[pltpu.VMEM((d, de), _BF16), pltpu.VMEM((d, de), _BF16),
                            pltpu.VMEM((de, d), _BF16)],
        ),
        out_shape=jax.ShapeDtypeStruct((n_rows, d // 2), jnp.uint32),
        compiler_params=_params("arbitrary"),
        name="experts",
    )(block_expert, block_valid, n_used, xs, wg, wu, wd)


def _sc_gather(table, idx):
    n, = idx.shape
    width = table.shape[1]
    workers = SC_CORES * SC_SUBCORES
    per_worker = n // workers
    chunks = per_worker // SC_CHUNK
    assert n % (workers * SC_CHUNK) == 0
    mesh = plsc.VectorSubcoreMesh(core_axis_name="c", subcore_axis_name="s",
                                  num_cores=SC_CORES, num_subcores=SC_SUBCORES)

    @functools.partial(
        pl.kernel, out_type=jax.ShapeDtypeStruct((n, width), table.dtype), mesh=mesh,
        scratch_types=[pltpu.VMEM((per_worker,), jnp.int32),
                       pltpu.VMEM((SC_CHUNK, width), table.dtype),
                       pltpu.SemaphoreType.DMA],
        name="sc_gather")
    def gather(table_hbm, idx_hbm, out_hbm, idx_v, rows_v, sem):
        worker = lax.axis_index("s") * SC_CORES + lax.axis_index("c")
        base = worker * per_worker
        pltpu.sync_copy(idx_hbm.at[pl.ds(base, per_worker)], idx_v)

        def chunk(j, carry):
            off = pl.multiple_of(j * SC_CHUNK, SC_CHUNK)
            pltpu.async_copy(table_hbm.at[idx_v.at[pl.ds(off, SC_CHUNK)]], rows_v, sem).wait()
            pltpu.sync_copy(rows_v, out_hbm.at[pl.ds(base + off, SC_CHUNK)])
            return carry

        lax.fori_loop(0, chunks, chunk, 0)

    return gather(table, idx)


def _final_body(x1_ref, route_ref, p_ref, r1_ref, r2_ref, gp_ref, wpg_ref, wpp_ref, gf_ref, o_ref):
    emb = jnp.dot(p_ref[...].astype(_BF16), wpp_ref[...], preferred_element_type=_F32)
    route = route_ref[...]
    x2 = x1_ref[...] + (_unpack_bf16_pair(r1_ref[...]) * route[:, 0:1]
                        + _unpack_bf16_pair(r2_ref[...]) * route[:, 1:2])
    gate = jax.nn.sigmoid(jnp.dot(_rms(x2, gp_ref[...]).astype(_BF16), wpg_ref[...],
                                  preferred_element_type=_F32))
    o_ref[...] = _rms(x2 + gate * emb, gf_ref[...])


def _final(x1, route, p, slot_rows, gp, wpg, wpp, gf, *, tm):
    t, d = x1.shape
    ple = p.shape[1]
    steps = t // tm
    row = lambda cols: pl.BlockSpec((tm, cols), lambda i: (i, 0))
    return pl.pallas_call(
        _final_body,
        grid=(steps,),
        in_specs=[row(d), row(LANES), row(ple), row(d // 2),
                  pl.BlockSpec((tm, d // 2), lambda i: (steps + i, 0)),
                  _const_spec((1, d)), _const_spec(wpg.shape), _const_spec(wpp.shape),
                  _const_spec((1, d))],
        out_specs=row(d),
        out_shape=jax.ShapeDtypeStruct((t, d), _F32),
        compiler_params=_params("arbitrary"),
        name="final",
    )(x1, route, p, slot_rows, slot_rows, gp, wpg, wpp, gf)


def _rope_tables(s):
    half = HEAD_DIM // 2
    inv = 1.0 / (ROPE_THETA ** (jnp.arange(0, HEAD_DIM, 2, dtype=_F32) / HEAD_DIM))
    ang = jnp.arange(s, dtype=_F32)[:, None] * inv[None, :]
    cos, sin = jnp.cos(ang), jnp.sin(ang)
    reps = LANES // half
    cos_t = jnp.tile(cos, (1, reps))
    sin_t = jnp.tile(jnp.concatenate([-sin, sin], axis=1), (1, reps // 2))
    return cos_t, sin_t


def _layer(x, p, attn_norm, w_in, b_forget, w_fox_branch, w_moba_branch, w_out, moe_norm,
           w_group, b_group, w_fine, b_fine, w_gate, w_up, w_down, ple_norm, w_ple_gate, w_ple_proj,
           final_gain):
    b, s, d = x.shape
    t = b * s
    width = N_FOX_HEADS * HEAD_DIM
    assert N_MOBA_HEADS * HEAD_DIM == width and MOBA_BLOCK % ATTN_Q == 0 and ATTN_K % MOBA_BLOCK == 0
    assert BIAS_PIECES * N_FOX_HEADS <= LANES and s % ATTN_K == 0
    assert s % ROW_TILE == 0 and ROW_TILE % MOBA_BLOCK == 0 and s // MOBA_BLOCK <= LANES
    assert t % FINAL_TILE == 0 and N_EXPERTS + N_GROUPS <= LANES

    qkv_cols = 6 * width
    f_cols = N_FOX_HEADS
    chunk = lambda c: w_in[:, c * width:(c + 1) * width]
    w_rows = jnp.concatenate([chunk(1), chunk(4), w_in[:, qkv_cols + f_cols:]], axis=1).astype(_BF16)
    w_cols = jnp.concatenate([chunk(0), chunk(2), chunk(3), chunk(5)], axis=1).T.astype(_BF16)
    wf = jnp.zeros((d, LANES), _BF16).at[:, :f_cols].set(w_in[:, qkv_cols:qkv_cols + f_cols].astype(_BF16))
    bfp = jnp.zeros((1, LANES), _F32).at[0, :f_cols].set(b_forget.astype(_F32))
    cos, sin = _rope_tables(s)

    qft, kf, vft, qmt, km, vmt, gates, cb, kmean = _inproj(
        x, attn_norm.reshape(1, d), w_rows, w_cols, wf, bfp, cos, sin, cos.T, sin.T, tm=ROW_TILE)
    y_fox = _attention(qft, kf, cb, vft, jnp.zeros((8, LANES), _F32), mode="fox")
    block_of_key = jnp.arange(s, dtype=jnp.int32)[:, None] // MOBA_BLOCK
    block_onehot = (block_of_key == jnp.arange(LANES, dtype=jnp.int32)[None, :]).astype(_BF16)
    y_moba = _attention(qmt, km, block_onehot, vmt, kmean, mode="moba")

    wr = jnp.zeros((d, LANES), _BF16)
    wr = wr.at[:, :N_EXPERTS].set(w_fine.astype(_BF16))
    wr = wr.at[:, N_EXPERTS:N_EXPERTS + N_GROUPS].set(w_group.astype(_BF16))
    br = jnp.zeros((1, LANES), _F32)
    br = br.at[0, :N_EXPERTS].set(b_fine.astype(_F32))
    br = br.at[0, N_EXPERTS:N_EXPERTS + N_GROUPS].set(b_group.astype(_F32))
    x1, h2p, route, counts = _postattn(
        y_fox.reshape(t, width), y_moba.reshape(t, width), gates.reshape(t, 2 * d), x.reshape(t, d),
        w_fox_branch.astype(_BF16), w_moba_branch.astype(_BF16), w_out.astype(_BF16),
        moe_norm.reshape(1, d), wr, br, tm=ROW_TILE)

    te = EXPERT_TILE
    n_blk = (2 * t) // te + N_EXPERTS
    counts = counts[0, :N_EXPERTS].astype(jnp.int32)
    blocks_per = (counts + te - 1) // te
    block_end = jnp.cumsum(blocks_per)
    row_start = (block_end - blocks_per) * te
    experts = route[:, 2:4].astype(jnp.int32)
    dest = (row_start[experts] + route[:, 4:6].astype(jnp.int32)).reshape(-1)
    n_used = block_end[-1:].astype(jnp.int32)
    block_ids = jnp.arange(n_blk, dtype=jnp.int32)
    block_expert = jnp.minimum(
        jnp.sum((block_ids[:, None] >= block_end[None, :]).astype(jnp.int32), axis=1), N_EXPERTS - 1)

    first_block = block_end - blocks_per
    block_valid = jnp.clip(counts[block_expert] - (block_ids - first_block[block_expert]) * te, 0, te)
    slot_major = dest.reshape(t, 2).T
    xs = _sc_scatter(h2p, slot_major.reshape(2, t // SC_SCATTER_CHUNK, SC_SCATTER_CHUNK), n_blk * te)
    rows = _experts(block_expert, block_valid, n_used, xs, w_gate, w_up, w_down, te=te)
    slot_rows = _sc_gather(rows, slot_major.reshape(-1))
    out = _final(x1, route, p.reshape(t, -1), slot_rows, ple_norm.reshape(1, d),
                 w_ple_gate.astype(_BF16), w_ple_proj.astype(_BF16), final_gain.reshape(1, d),
                 tm=FINAL_TILE)
    return out.reshape(b, s, d)


def kernel(x, p, attn_norm, w_in, b_forget, w_fox_branch, w_moba_branch, w_out, moe_norm, w_group,
           b_group, w_fine, b_fine, w_gate, w_up, w_down, ple_norm, w_ple_gate, w_ple_proj, final_norm):
    depth = p.shape[0]
    assert depth == 1, "the final norm is fused into the last layer's kernel"
    i = 0
    return _layer(x, p[i], attn_norm[i], w_in[i], b_forget[i], w_fox_branch[i], w_moba_branch[i],
                  w_out[i], moe_norm[i], w_group[i], b_group[i], w_fine[i], b_fine[i], w_gate[i],
                  w_up[i], w_down[i], ple_norm[i], w_ple_gate[i], w_ple_proj[i], final_norm)
```

```python
import functools

import jax
import jax.numpy as jnp
from jax import lax
from jax.experimental import pallas as pl
from jax.experimental.pallas import tpu as pltpu
from jax.experimental.pallas import tpu_sc as plsc

HEAD_DIM = 64
N_FOX_HEADS = 8
N_MOBA_HEADS = 8
MOBA_BLOCK = 256
MOBA_TOPK = 3
ROPE_THETA = 10000.0
N_GROUPS = 4
EXPERTS_PER_GROUP = 8
N_EXPERTS = N_GROUPS * EXPERTS_PER_GROUP
RMS_EPS = 1e-6

LANES = 128
HEADS_PER_STEP = LANES // HEAD_DIM
ATTN_Q = 256
ATTN_K = 256
PIPE_UNITS = 8
SOFTMAX_LAG = 4
ONES_ROWS = 16
SOFTMAX_ROWS = 64
BIAS_PIECES = 3
ROW_TILE = 512
SC_CORES = 2
SC_SUBCORES = 16
SC_CHUNK = 64
SC_SCATTER_CHUNK = 128
FINAL_TILE = 256
EXPERT_TILE = 512
LOG2_E = 1.4426950408889634
NEG_BIG = -1e30
VMEM_LIMIT = 48 * 1024 * 1024

_BF16 = jnp.bfloat16
_F32 = jnp.float32


def _params(*sem):
    return pltpu.CompilerParams(dimension_semantics=sem, vmem_limit_bytes=VMEM_LIMIT)


def _rms(x, g):
    return x * lax.rsqrt(jnp.mean(x * x, axis=-1, keepdims=True) + RMS_EPS) * g


def _const_spec(shape):
    return pl.BlockSpec(shape, lambda *_: (0,) * len(shape))


def _inproj_body(x_ref, g_ref, w_ref, wt_ref, wf_ref, bf_ref, cos_ref, sin_ref, cost_ref, sint_ref,
                 qft_ref, kf_ref, vft_ref, qmt_ref, km_ref, vmt_ref, gates_ref, cb_ref, kmean_ref,
                 carry_ref, *, tm, width, gate_cols):
    j = pl.program_id(1)
    h = _rms(x_ref[...], g_ref[...])
    hb = h.astype(_BF16)
    hbt = h.T.astype(_BF16)
    reps = width // LANES
    half = HEAD_DIM // 2

    def proj(c):
        return jnp.dot(hb, w_ref[:, c * width:(c + 1) * width], preferred_element_type=_F32)

    def proj_t(c):
        return jnp.dot(wt_ref[c * width:(c + 1) * width, :], hbt, preferred_element_type=_F32)

    def rope(t):
        cos = jnp.concatenate([cos_ref[...]] * reps, axis=1)
        sin = jnp.concatenate([sin_ref[...]] * reps, axis=1)
        first = (lax.broadcasted_iota(jnp.int32, (1, width), 1) % HEAD_DIM) < half
        partner = jnp.where(first, pltpu.roll(t, width - half, 1), pltpu.roll(t, half, 1))
        return t * cos + partner * sin

    def rope_t(t):
        cos = jnp.concatenate([cost_ref[...]] * reps, axis=0)
        sin = jnp.concatenate([sint_ref[...]] * reps, axis=0)
        first = (lax.broadcasted_iota(jnp.int32, (width, 1), 0) % HEAD_DIM) < half
        partner = jnp.where(first, pltpu.roll(t, width - half, 0), pltpu.roll(t, half, 0))
        return t * cos + partner * sin

    scale = HEAD_DIM ** -0.5 * LOG2_E
    qft_ref[...] = (proj_t(0) * scale).astype(_BF16)
    vft_ref[...] = proj_t(1).astype(_BF16)
    qmt_ref[...] = (rope_t(proj_t(2)) * scale).astype(_BF16)
    vmt_ref[...] = proj_t(3).astype(_BF16)
    kf_ref[...] = proj(0).astype(_BF16)
    km = rope(proj(1))
    km_ref[...] = km.astype(_BF16)

    @pl.when(j == 0)
    def _():
        kmean_ref[...] = jnp.zeros_like(kmean_ref)
        carry_ref[...] = jnp.zeros_like(carry_ref)

    blocks = tm // MOBA_BLOCK
    for b in range(blocks):
        kmean_ref[pl.ds(j * blocks + b, 1), :] = jnp.mean(
            km[b * MOBA_BLOCK:(b + 1) * MOBA_BLOCK], axis=0, keepdims=True)

    for c in range(gate_cols // width):
        g = jnp.dot(hb, w_ref[:, (2 + c) * width:(3 + c) * width],
                    preferred_element_type=_F32)
        gates_ref[:, c * width:(c + 1) * width] = jax.nn.sigmoid(g).astype(_BF16)

    z = jnp.dot(hb, wf_ref[...], preferred_element_type=_F32) + bf_ref[...]
    log_f = jnp.minimum(z, 0.0) - jnp.log1p(jnp.exp(-jnp.abs(z)))
    r = lax.broadcasted_iota(jnp.int32, (LANES, LANES), 0)
    c = lax.broadcasted_iota(jnp.int32, (LANES, LANES), 1)
    tri = (c <= r).astype(_BF16)
    offset = carry_ref[0:1, :]
    blocks_cum = []
    for blk in range(tm // LANES):
        rest = log_f[blk * LANES:(blk + 1) * LANES]
        within = jnp.zeros((LANES, LANES), _F32)
        for _ in range(BIAS_PIECES):
            part = rest.astype(_BF16)
            rest = rest - part.astype(_F32)
            within = within + jnp.dot(tri, part, preferred_element_type=_F32)
        blocks_cum.append(within + offset)
        offset = offset + within[LANES - 1:LANES, :]
    cum = jnp.concatenate(blocks_cum, axis=0)
    carry_ref[...] = jnp.broadcast_to(offset, carry_ref.shape)

    head_lane = lax.broadcasted_iota(jnp.int32, (1, LANES), 1) < N_FOX_HEADS
    rest = jnp.where(head_lane, -cum * LOG2_E, 0.0)
    placed = jnp.zeros((tm, LANES), _F32)
    for piece in range(BIAS_PIECES):
        part = rest.astype(_BF16).astype(_F32)
        rest = rest - part
        placed = placed + (pltpu.roll(part, N_FOX_HEADS * piece, 1) if piece else part)
    cb_ref[...] = placed.astype(_BF16)


def _inproj(x, g, w, wt, wf, bfp, cos, sin, cos_t, sin_t, *, tm):
    b, s, d = x.shape
    width = N_FOX_HEADS * HEAD_DIM
    gate_cols = w.shape[1] - 2 * width
    act = jax.ShapeDtypeStruct((b, s, width), _BF16)
    act_t = jax.ShapeDtypeStruct((b, width, s), _BF16)
    act_spec = pl.BlockSpec((None, tm, width), lambda bi, j: (bi, j, 0))
    act_t_spec = pl.BlockSpec((None, width, tm), lambda bi, j: (bi, 0, j))
    return pl.pallas_call(
        functools.partial(_inproj_body, tm=tm, width=width, gate_cols=gate_cols),
        grid=(b, s // tm),
        in_specs=[
            pl.BlockSpec((None, tm, d), lambda bi, j: (bi, j, 0)),
            _const_spec((1, d)),
            _const_spec(w.shape),
            _const_spec(wt.shape),
            _const_spec(wf.shape),
            _const_spec(bfp.shape),
            pl.BlockSpec((tm, LANES), lambda bi, j: (j, 0)),
            pl.BlockSpec((tm, LANES), lambda bi, j: (j, 0)),
            pl.BlockSpec((LANES, tm), lambda bi, j: (0, j)),
            pl.BlockSpec((LANES, tm), lambda bi, j: (0, j)),
        ],
        out_specs=[act_t_spec, act_spec, act_t_spec, act_t_spec, act_spec, act_t_spec,
                   pl.BlockSpec((None, tm, gate_cols), lambda bi, j: (bi, j, 0)),
                   pl.BlockSpec((None, tm, LANES), lambda bi, j: (bi, j, 0)),
                   pl.BlockSpec((None, LANES, width), lambda bi, j: (bi, 0, 0))],
        out_shape=[act_t, act, act_t, act_t, act, act_t,
                   jax.ShapeDtypeStruct((b, s, gate_cols), _BF16),
                   jax.ShapeDtypeStruct((b, s, LANES), _BF16),
                   jax.ShapeDtypeStruct((b, LANES, width), _F32)],
        scratch_shapes=[pltpu.VMEM((8, LANES), _F32)],
        compiler_params=_params("arbitrary", "arbitrary"),
        name="inproj",
    )(x, g, w, wt, wf, bfp, cos, sin, cos_t, sin_t)


def _attn_body(uq_ref, uk_ref, qt_ref, k_ref, kx_ref, vt_ref, aux_ref, o_ref,
               w_ref, m_ref, l_ref, acc_ref, st_ref, p_ref, alpha_ref, pmax_ref, *, mode, tq, tk, nb, n_diag, n_units):
    hp = pl.program_id(1)
    wide = HEADS_PER_STEP * tq
    nq = w_ref.shape[0]
    feat = lax.broadcasted_iota(jnp.int32, (LANES, 1), 0)
    xr = lax.broadcasted_iota(jnp.int32, (LANES, wide), 0)
    xc = lax.broadcasted_iota(jnp.int32, (LANES, wide), 1)

    def setup(i, carry):
        qt = qt_ref[:, pl.ds(pl.multiple_of(i * tq, tq), tq)]
        zero = jnp.zeros_like(qt)
        wq = jnp.concatenate([jnp.where(feat < HEAD_DIM, qt, zero),
                              jnp.where(feat < HEAD_DIM, zero, qt)], axis=1)
        if mode == "fox":
            head = hp * HEADS_PER_STEP + xc // tq
            lower = ((xr % N_FOX_HEADS == head) & (xr < BIAS_PIECES * N_FOX_HEADS)).astype(_BF16)
        else:
            own = (i * tq) // MOBA_BLOCK
            gate = jnp.dot(aux_ref[...].astype(_BF16), wq, preferred_element_type=_F32)[0:nb]
            blk = lax.broadcasted_iota(jnp.int32, (nb, wide), 0)
            gate = jnp.where(blk < own, gate, -jnp.inf)
            keep = blk == own
            for _ in range(MOBA_TOPK):
                mx = jnp.max(gate, axis=0, keepdims=True)
                cand = jnp.where((gate == mx) & (mx > -jnp.inf), blk, nb)
                pick = blk == jnp.min(cand, axis=0, keepdims=True)
                keep = keep | pick
                gate = jnp.where(pick, -jnp.inf, gate)
            lower = jnp.where(keep, 0.0, NEG_BIG)
            if nb < LANES:
                lower = jnp.concatenate([lower, jnp.zeros((LANES - nb, wide), _F32)], axis=0)
            lower = lower.astype(_BF16)
        w_ref[i] = jnp.concatenate([wq, lower], axis=0)
        m_ref[i] = jnp.full(m_ref.shape[1:], NEG_BIG, _F32)
        l_ref[i] = jnp.zeros(l_ref.shape[1:], _F32)
        acc_ref[i] = jnp.zeros(acc_ref.shape[1:], _F32)
        return carry

    lax.fori_loop(0, nq, setup, 0)

    def unit(t):
        return uq_ref[t], uk_ref[t]

    def scores(t, slot, causal):
        q, kb = unit(t)
        start = pl.multiple_of(kb * tk, tk)
        ka = jnp.concatenate([k_ref[pl.ds(start, tk), :], kx_ref[pl.ds(start, tk), :]], axis=1)
        st = jnp.dot(ka, w_ref[q], preferred_element_type=_F32)
        if causal:
            key = start + lax.broadcasted_iota(jnp.int32, (tk, wide), 0)
            qry = q * tq + lax.broadcasted_iota(jnp.int32, (tk, wide), 1) % tq
            st = jnp.where(key <= qry, st, NEG_BIG)
        st_ref[slot] = st
        pmax_ref[slot] = jnp.max(st.reshape(tk // 8, 8, wide), axis=0)

    def softmax(t, slot):
        q, _ = unit(t)
        rows = SOFTMAX_ROWS
        m_prev = m_ref[q][0:1, :]
        m_new = jnp.maximum(m_prev, jnp.max(pmax_ref[slot], axis=0, keepdims=True))
        for r in range(0, tk, rows):
            p_ref[slot, r:r + rows, :] = jnp.exp2(st_ref[slot, r:r + rows, :] - m_new).astype(_BF16)
        m_ref[q] = jnp.broadcast_to(m_new, m_ref.shape[1:])
        alpha_ref[slot] = jnp.broadcast_to(jnp.exp2(m_prev - m_new), alpha_ref.shape[1:])

    def values(t, slot):
        q, kb = unit(t)
        start = pl.multiple_of(kb * tk, tk)
        ones = jnp.ones((ONES_ROWS, tk), _BF16)
        for h in range(HEADS_PER_STEP):
            cols = slice(h * tq, (h + 1) * tq)
            feats = slice(h * HEAD_DIM, (h + 1) * HEAD_DIM)
            lhs = jnp.concatenate([vt_ref[feats, pl.ds(start, tk)], ones], axis=0)
            pv = jnp.dot(lhs, p_ref[slot, :, cols], preferred_element_type=_F32)
            alpha = alpha_ref[slot, 0:1, cols]
            acc_ref[q, feats, :] = acc_ref[q, feats, :] * alpha + pv[0:HEAD_DIM]
            l_ref[q, :, cols] = jnp.broadcast_to(
                alpha * l_ref[q, 0:1, cols] + pv[HEAD_DIM:HEAD_DIM + 1], (8, tq))

    def step(base, c, causal, do_scores=True, do_softmax=True, do_values=True):
        u = base + c
        if do_scores:
            scores(u, c % PIPE_UNITS, causal)
        if do_values:
            values(u - 2 * SOFTMAX_LAG, (c - 2 * SOFTMAX_LAG) % PIPE_UNITS)
        if do_softmax:
            softmax(u - SOFTMAX_LAG, (c - SOFTMAX_LAG) % PIPE_UNITS)

    def group(causal):
        def body(j, carry):
            for c in range(PIPE_UNITS):
                step(j * PIPE_UNITS, c, causal)
            return carry
        return body

    for c in range(PIPE_UNITS):
        step(0, c, True, do_softmax=c >= SOFTMAX_LAG, do_values=c >= 2 * SOFTMAX_LAG)
    lax.fori_loop(1, n_diag // PIPE_UNITS, group(True), 0)
    lax.fori_loop(n_diag // PIPE_UNITS, n_units // PIPE_UNITS, group(False), 0)
    for c in range(2 * SOFTMAX_LAG):
        step(n_units, c, False, do_scores=False, do_softmax=c < SOFTMAX_LAG)

    def finish(i, carry):
        out_t = jnp.concatenate(
            [acc_ref[i, h * HEAD_DIM:(h + 1) * HEAD_DIM, :] / l_ref[i, 0:1, h * tq:(h + 1) * tq]
             for h in range(HEADS_PER_STEP)], axis=0)
        o_ref[pl.ds(pl.multiple_of(i * tq, tq), tq), :] = out_t.T.astype(o_ref.dtype)
        return carry

    lax.fori_loop(0, nq, finish, 0, unroll=2)


def _attention(qt, k, kx, vt, aux, *, mode):
    b, width, s = qt.shape
    tq, tk = ATTN_Q, ATTN_K
    nq = s // tq
    nb = s // MOBA_BLOCK
    wide = HEADS_PER_STEP * tq
    units = [(i, (i * tq) // tk) for i in range(nq)]
    units += [(i, n) for i in range(nq) for n in range((i * tq) // tk)]
    assert nq % PIPE_UNITS == 0 and len(units) % PIPE_UNITS == 0 and PIPE_UNITS >= 2 * SOFTMAX_LAG
    unit_q = jnp.asarray([u[0] for u in units], jnp.int32)
    unit_k = jnp.asarray([u[1] for u in units], jnp.int32)
    if mode == "fox":
        aux_spec = pl.BlockSpec(aux.shape, lambda bi, hp, uq, uk: (0, 0))
        kx_spec = pl.BlockSpec((None, s, LANES), lambda bi, hp, uq, uk: (bi, 0, 0))
    else:
        aux_spec = pl.BlockSpec((None, LANES, LANES), lambda bi, hp, uq, uk: (bi, 0, hp))
        kx_spec = pl.BlockSpec(kx.shape, lambda bi, hp, uq, uk: (0, 0))
    return pl.pallas_call(
        functools.partial(_attn_body, mode=mode, tq=tq, tk=tk, nb=nb, n_diag=nq, n_units=len(units)),
        grid_spec=pltpu.PrefetchScalarGridSpec(
            num_scalar_prefetch=2,
            grid=(b, width // LANES),
            in_specs=[
                pl.BlockSpec((None, LANES, s), lambda bi, hp, uq, uk: (bi, hp, 0)),
                pl.BlockSpec((None, s, LANES), lambda bi, hp, uq, uk: (bi, 0, hp)),
                kx_spec,
                pl.BlockSpec((None, LANES, s), lambda bi, hp, uq, uk: (bi, hp, 0)),
                aux_spec,
            ],
            out_specs=pl.BlockSpec((None, s, LANES), lambda bi, hp, uq, uk: (bi, 0, hp)),
            scratch_shapes=[
                pltpu.VMEM((nq, 2 * LANES, wide), _BF16),
                pltpu.VMEM((nq, 8, wide), _F32),
                pltpu.VMEM((nq, 8, wide), _F32),
                pltpu.VMEM((nq, LANES, tq), _F32),
                pltpu.VMEM((PIPE_UNITS, tk, wide), _F32),
                pltpu.VMEM((PIPE_UNITS, tk, wide), _BF16),
                pltpu.VMEM((PIPE_UNITS, 8, wide), _F32),
                pltpu.VMEM((PIPE_UNITS, 8, wide), _F32),
            ],
        ),
        out_shape=jax.ShapeDtypeStruct((b, s, width), _BF16),
        compiler_params=_params("arbitrary", "arbitrary"),
        name=mode,
    )(unit_q, unit_k, qt, k, kx, vt, aux)


def _pack_bf16_pair(a, b):
    lo = pltpu.bitcast(a.astype(_BF16).astype(_F32), jnp.uint32) >> 16
    hi = pltpu.bitcast(b.astype(_BF16).astype(_F32), jnp.uint32) & jnp.uint32(0xFFFF0000)
    return lo | hi


def _unpack_bf16_pair(u):
    lo = pltpu.bitcast(u << 16, _F32)
    hi = pltpu.bitcast(u & jnp.uint32(0xFFFF0000), _F32)
    return jnp.concatenate([lo, hi], axis=1)


def _postattn_body(yf_ref, ym_ref, gates_ref, x_ref, wfb_ref, wmb_ref, wout_ref, g_ref, wr_ref, br_ref,
                   x1_ref, h2_ref, route_ref, meta_ref, counts_ref, carry_ref, *, tm, d):
    step = pl.program_id(0)
    ya = jnp.dot(yf_ref[...], wfb_ref[...], preferred_element_type=_F32)
    yb = jnp.dot(ym_ref[...], wmb_ref[...], preferred_element_type=_F32)
    mixed = gates_ref[:, 0:d].astype(_F32) * ya + gates_ref[:, d:2 * d].astype(_F32) * yb
    x1 = x_ref[...] + jnp.dot(mixed.astype(_BF16), wout_ref[...], preferred_element_type=_F32)
    x1_ref[...] = x1
    h2 = _rms(x1, g_ref[...])
    h2_ref[...] = _pack_bf16_pair(h2[:, 0:d // 2], h2[:, d // 2:d])

    logits = jnp.dot(h2.astype(_BF16), wr_ref[...], preferred_element_type=_F32) + br_ref[...]
    lane = lax.broadcasted_iota(jnp.int32, (tm, LANES), 1)
    is_group = (lane >= N_EXPERTS) & (lane < N_EXPERTS + N_GROUPS)
    gl = jnp.where(is_group, logits, -jnp.inf)
    ge = jnp.exp(gl - jnp.max(gl, axis=-1, keepdims=True))
    gp = ge / jnp.sum(ge, axis=-1, keepdims=True)
    g_top = jnp.max(gp, axis=-1, keepdims=True)
    g_idx = jnp.min(jnp.where(is_group & (gp == g_top), lane, 2 * LANES), axis=-1,
                    keepdims=True) - N_EXPERTS
    in_group = (lane < N_EXPERTS) & ((lane // EXPERTS_PER_GROUP) == g_idx)
    fl = jnp.where(in_group, logits, -jnp.inf)
    f1 = jnp.max(fl, axis=-1, keepdims=True)
    e1 = jnp.min(jnp.where(fl == f1, lane, 2 * LANES), axis=-1, keepdims=True)
    fl2 = jnp.where(lane == e1, -jnp.inf, fl)
    f2 = jnp.max(fl2, axis=-1, keepdims=True)
    e2 = jnp.min(jnp.where(fl2 == f2, lane, 2 * LANES), axis=-1, keepdims=True)
    t2 = jnp.exp(f2 - f1)
    w1 = g_top * (1.0 / (1.0 + t2))
    w2 = g_top * (t2 / (1.0 + t2))

    @pl.when(step == 0)
    def _():
        carry_ref[...] = jnp.zeros_like(carry_ref)

    onehot = ((lane == e1) | (lane == e2)).astype(_F32)
    rr = lax.broadcasted_iota(jnp.int32, (tm, tm), 0)
    cc = lax.broadcasted_iota(jnp.int32, (tm, tm), 1)
    before = jnp.dot((cc < rr).astype(_BF16), onehot.astype(_BF16), preferred_element_type=_F32)
    before = before + carry_ref[0:1, :]
    r1 = jnp.sum(jnp.where(lane == e1, before, 0.0), axis=-1, keepdims=True)
    r2 = jnp.sum(jnp.where(lane == e2, before, 0.0), axis=-1, keepdims=True)
    total = carry_ref[0:1, :] + jnp.sum(onehot, axis=0, keepdims=True)
    carry_ref[...] = jnp.broadcast_to(total, carry_ref.shape)
    counts_ref[...] = jnp.broadcast_to(total, counts_ref.shape)

    slab = jnp.where(lane == 0, w1, 0.0)
    slab = jnp.where(lane == 1, w2, slab)
    slab = jnp.where(lane == 2, e1.astype(_F32), slab)
    slab = jnp.where(lane == 3, e2.astype(_F32), slab)
    slab = jnp.where(lane == 4, r1, slab)
    slab = jnp.where(lane == 5, r2, slab)
    route_ref[...] = slab
    meta_ref[...] = slab.T[0:8].astype(jnp.int32)


def _postattn(yf, ym, gates, x, wfb, wmb, wout, g, wr, br, *, tm):
    t, d = x.shape
    width = yf.shape[1]
    row = lambda cols: pl.BlockSpec((tm, cols), lambda i: (i, 0))
    return pl.pallas_call(
        functools.partial(_postattn_body, tm=tm, d=d),
        grid=(t // tm,),
        in_specs=[row(width), row(width), row(2 * d), row(d),
                  _const_spec(wfb.shape), _const_spec(wmb.shape), _const_spec(wout.shape),
                  _const_spec((1, d)), _const_spec(wr.shape), _const_spec(br.shape)],
        out_specs=[row(d), row(d // 2), row(LANES), pl.BlockSpec((8, tm), lambda i: (0, i)),
                   _const_spec((8, LANES))],
        out_shape=[jax.ShapeDtypeStruct((t, d), _F32),
                   jax.ShapeDtypeStruct((t, d // 2), jnp.uint32),
                   jax.ShapeDtypeStruct((t, LANES), _F32),
                   jax.ShapeDtypeStruct((8, t), jnp.int32),
                   jax.ShapeDtypeStruct((8, LANES), _F32)],
        scratch_shapes=[pltpu.VMEM((8, LANES), _F32)],
        compiler_params=_params("arbitrary"),
        name="postattn",
    )(yf, ym, gates, x, wfb, wmb, wout, g, wr, br)


def _sc_scatter(rows, idx, n_out):
    n, width = rows.shape
    workers = SC_CORES * SC_SUBCORES
    per_worker = n // workers
    chunks = per_worker // SC_SCATTER_CHUNK
    assert n % (workers * SC_SCATTER_CHUNK) == 0 and idx.shape == (2, n // SC_SCATTER_CHUNK, SC_SCATTER_CHUNK)
    mesh = plsc.VectorSubcoreMesh(core_axis_name="c", subcore_axis_name="s",
                                  num_cores=SC_CORES, num_subcores=SC_SUBCORES)

    @functools.partial(
        pl.kernel, out_type=jax.ShapeDtypeStruct((n_out, width), rows.dtype), mesh=mesh,
        scratch_types=[pltpu.VMEM((2, chunks, SC_SCATTER_CHUNK), jnp.int32),
                       pltpu.VMEM((SC_SCATTER_CHUNK, width), rows.dtype),
                       pltpu.SemaphoreType.DMA],
        name="sc_scatter")
    def scatter(rows_hbm, idx_hbm, out_hbm, idx_v, rows_v, sem):
        worker = lax.axis_index("s") * SC_CORES + lax.axis_index("c")
        for k in range(2):
            pltpu.sync_copy(idx_hbm.at[k, pl.ds(worker * chunks, chunks)], idx_v.at[k])

        def chunk(j, carry):
            start = worker * per_worker + j * SC_SCATTER_CHUNK
            pltpu.sync_copy(rows_hbm.at[pl.ds(start, SC_SCATTER_CHUNK)], rows_v)
            for k in range(2):
                pltpu.async_copy(rows_v, out_hbm.at[idx_v.at[k, j]], sem).wait()
            return carry

        lax.fori_loop(0, chunks, chunk, 0)

    return scatter(rows, idx)


def _experts_body(be_ref, valid_ref, nused_ref, xs_ref, wg_ref, wu_ref, wd_ref, out_ref,
                  wgb_ref, wub_ref, wdb_ref):
    j = pl.program_id(0)
    used = j < nused_ref[0]

    @pl.when(jnp.logical_not(used))
    def _():
        out_ref[...] = jnp.zeros_like(out_ref)

    @pl.when(used & ((j == 0) | (be_ref[j] != be_ref[jnp.maximum(j - 1, 0)])))
    def _():
        wgb_ref[...] = wg_ref[...].astype(_BF16)
        wub_ref[...] = wu_ref[...].astype(_BF16)
        wdb_ref[...] = wd_ref[...].astype(_BF16)

    @pl.when(used)
    def _():
        row = lax.broadcasted_iota(jnp.int32, xs_ref.shape, 0)
        packed = jnp.where(row < valid_ref[j], xs_ref[...], jnp.uint32(0))
        rows = _unpack_bf16_pair(packed).astype(_BF16)
        gate = jnp.dot(rows, wgb_ref[...], preferred_element_type=_F32)
        up = jnp.dot(rows, wub_ref[...], preferred_element_type=_F32)
        act = (gate * jax.nn.sigmoid(gate)) * up
        out = jnp.dot(act.astype(_BF16), wdb_ref[...], preferred_element_type=_F32)
        half = out.shape[1] // 2
        out_ref[...] = _pack_bf16_pair(out[:, 0:half], out[:, half:])


def _experts(block_expert, block_valid, n_used, xs, wg, wu, wd, *, te):
    n_rows, half = xs.shape
    _, d, de = wg.shape

    def blk(j, be, bv, nu):
        return jnp.minimum(j, nu[0] - 1)

    def weight(j, be, bv, nu):
        return (be[blk(j, be, bv, nu)], 0, 0)

    return pl.pallas_call(
        _experts_body,
        grid_spec=pltpu.PrefetchScalarGridSpec(
            num_scalar_prefetch=3,
            grid=(n_rows // te,),
            in_specs=[pl.BlockSpec((te, half), lambda j, be, bv, nu: (blk(j, be, bv, nu), 0)),
                      pl.BlockSpec((None, d, de), weight),
                      pl.BlockSpec((None, d, de), weight),
                      pl.BlockSpec((None, de, d), weight)],
            out_specs=pl.BlockSpec((te, d // 2), lambda j, be, bv, nu: (j, 0)),
            scratch_shapes=[pltpu.VMEM((d, de), _BF16), pltpu.VMEM((d, de), _BF16),
                            pltpu.VMEM((de, d), _BF16)],
        ),
        out_shape=jax.ShapeDtypeStruct((n_rows, d // 2), jnp.uint32),
        compiler_params=_params("arbitrary"),
        name="experts",
    )(block_expert, block_valid, n_used, xs, wg, wu, wd)


def _sc_gather(table, idx):
    n, = idx.shape
    width = table.shape[1]
    workers = SC_CORES * SC_SUBCORES
    per_worker = n // workers
    chunks = per_worker // SC_CHUNK
    assert n % (workers * SC_CHUNK) == 0
    mesh = plsc.VectorSubcoreMesh(core_axis_name="c", subcore_axis_name="s",
                                  num_cores=SC_CORES, num_subcores=SC_SUBCORES)

    @functools.partial(
        pl.kernel, out_type=jax.ShapeDtypeStruct((n, width), table.dtype), mesh=mesh,
        scratch_types=[pltpu.VMEM((per_worker,), jnp.int32),
                       pltpu.VMEM((SC_CHUNK, width), table.dtype),
                       pltpu.SemaphoreType.DMA],
        name="sc_gather")
    def gather(table_hbm, idx_hbm, out_hbm, idx_v, rows_v, sem):
        worker = lax.axis_index("s") * SC_CORES + lax.axis_index("c")
        base = worker * per_worker
        pltpu.sync_copy(idx_hbm.at[pl.ds(base, per_worker)], idx_v)

        def chunk(j, carry):
            off = pl.multiple_of(j * SC_CHUNK, SC_CHUNK)
            pltpu.async_copy(table_hbm.at[idx_v.at[pl.ds(off, SC_CHUNK)]], rows_v, sem).wait()
            pltpu.sync_copy(rows_v, out_hbm.at[pl.ds(base + off, SC_CHUNK)])
            return carry

        lax.fori_loop(0, chunks, chunk, 0)

    return gather(table, idx)


def _final_body(x1_ref, route_ref, p_ref, r1_ref, r2_ref, gp_ref, wpg_ref, wpp_ref, gf_ref, o_ref):
    emb = jnp.dot(p_ref[...].astype(_BF16), wpp_ref[...], preferred_element_type=_F32)
    route = route_ref[...]
    x2 = x1_ref[...] + (_unpack_bf16_pair(r1_ref[...]) * route[:, 0:1]
                        + _unpack_bf16_pair(r2_ref[...]) * route[:, 1:2])
    gate = jax.nn.sigmoid(jnp.dot(_rms(x2, gp_ref[...]).astype(_BF16), wpg_ref[...],
                                  preferred_element_type=_F32))
    o_ref[...] = _rms(x2 + gate * emb, gf_ref[...])


def _final(x1, route, p, slot_rows, gp, wpg, wpp, gf, *, tm):
    t, d = x1.shape
    ple = p.shape[1]
    steps = t // tm
    row = lambda cols: pl.BlockSpec((tm, cols), lambda i: (i, 0))
    return pl.pallas_call(
        _final_body,
        grid=(steps,),
        in_specs=[row(d), row(LANES), row(ple), row(d // 2),
                  pl.BlockSpec((tm, d // 2), lambda i: (steps + i, 0)),
                  _const_spec((1, d)), _const_spec(wpg.shape), _const_spec(wpp.shape),
                  _const_spec((1, d))],
        out_specs=row(d),
        out_shape=jax.ShapeDtypeStruct((t, d), _F32),
        compiler_params=_params("arbitrary"),
        name="final",
    )(x1, route, p, slot_rows, slot_rows, gp, wpg, wpp, gf)


def _rope_tables(s):
    half = HEAD_DIM // 2
    inv = 1.0 / (ROPE_THETA ** (jnp.arange(0, HEAD_DIM, 2, dtype=_F32) / HEAD_DIM))
    ang = jnp.arange(s, dtype=_F32)[:, None] * inv[None, :]
    cos, sin = jnp.cos(ang), jnp.sin(ang)
    reps = LANES // half
    cos_t = jnp.tile(cos, (1, reps))
    sin_t = jnp.tile(jnp.concatenate([-sin, sin], axis=1), (1, reps // 2))
    return cos_t, sin_t


def _layer(x, p, attn_norm, w_in, b_forget, w_fox_branch, w_moba_branch, w_out, moe_norm,
           w_group, b_group, w_fine, b_fine, w_gate, w_up, w_down, ple_norm, w_ple_gate, w_ple_proj,
           final_gain):
    b, s, d = x.shape
    t = b * s
    width = N_FOX_HEADS * HEAD_DIM
    assert N_MOBA_HEADS * HEAD_DIM == width and MOBA_BLOCK % ATTN_Q == 0 and ATTN_K % MOBA_BLOCK == 0
    assert BIAS_PIECES * N_FOX_HEADS <= LANES and s % ATTN_K == 0
    assert s % ROW_TILE == 0 and ROW_TILE % MOBA_BLOCK == 0 and s // MOBA_BLOCK <= LANES
    assert t % FINAL_TILE == 0 and N_EXPERTS + N_GROUPS <= LANES

    qkv_cols = 6 * width
    f_cols = N_FOX_HEADS
    chunk = lambda c: w_in[:, c * width:(c + 1) * width]
    w_rows = jnp.concatenate([chunk(1), chunk(4), w_in[:, qkv_cols + f_cols:]], axis=1).astype(_BF16)
    w_cols = jnp.concatenate([chunk(0), chunk(2), chunk(3), chunk(5)], axis=1).T.astype(_BF16)
    wf = jnp.zeros((d, LANES), _BF16).at[:, :f_cols].set(w_in[:, qkv_cols:qkv_cols + f_cols].astype(_BF16))
    bfp = jnp.zeros((1, LANES), _F32).at[0, :f_cols].set(b_forget.astype(_F32))
    cos, sin = _rope_tables(s)

    qft, kf, vft, qmt, km, vmt, gates, cb, kmean = _inproj(
        x, attn_norm.reshape(1, d), w_rows, w_cols, wf, bfp, cos, sin, cos.T, sin.T, tm=ROW_TILE)
    y_fox = _attention(qft, kf, cb, vft, jnp.zeros((8, LANES), _F32), mode="fox")
    block_of_key = jnp.arange(s, dtype=jnp.int32)[:, None] // MOBA_BLOCK
    block_onehot = (block_of_key == jnp.arange(LANES, dtype=jnp.int32)[None, :]).astype(_BF16)
    y_moba = _attention(qmt, km, block_onehot, vmt, kmean, mode="moba")

    wr = jnp.zeros((d, LANES), _BF16)
    wr = wr.at[:, :N_EXPERTS].set(w_fine.astype(_BF16))
    wr = wr.at[:, N_EXPERTS:N_EXPERTS + N_GROUPS].set(w_group.astype(_BF16))
    br = jnp.zeros((1, LANES), _F32)
    br = br.at[0, :N_EXPERTS].set(b_fine.astype(_F32))
    br = br.at[0, N_EXPERTS:N_EXPERTS + N_GROUPS].set(b_group.astype(_F32))
    x1, h2p, route, meta, counts = _postattn(
        y_fox.reshape(t, width), y_moba.reshape(t, width), gates.reshape(t, 2 * d), x.reshape(t, d),
        w_fox_branch.astype(_BF16), w_moba_branch.astype(_BF16), w_out.astype(_BF16),
        moe_norm.reshape(1, d), wr, br, tm=ROW_TILE)

    te = EXPERT_TILE
    n_blk = (2 * t) // te + N_EXPERTS
    counts = counts[0, :N_EXPERTS].astype(jnp.int32)
    blocks_per = (counts + te - 1) // te
    block_end = jnp.cumsum(blocks_per)
    row_start = (block_end - blocks_per) * te
    slot_major = row_start[meta[2:4]] + meta[4:6]
    n_used = block_end[-1:].astype(jnp.int32)
    block_ids = jnp.arange(n_blk, dtype=jnp.int32)
    block_expert = jnp.minimum(
        jnp.sum((block_ids[:, None] >= block_end[None, :]).astype(jnp.int32), axis=1), N_EXPERTS - 1)

    first_block = block_end - blocks_per
    block_valid = jnp.clip(counts[block_expert] - (block_ids - first_block[block_expert]) * te, 0, te)
    xs = _sc_scatter(h2p, slot_major.reshape(2, t // SC_SCATTER_CHUNK, SC_SCATTER_CHUNK), n_blk * te)
    rows = _experts(block_expert, block_valid, n_used, xs, w_gate, w_up, w_down, te=te)
    slot_rows = _sc_gather(rows, slot_major.reshape(-1))
    out = _final(x1, route, p.reshape(t, -1), slot_rows, ple_norm.reshape(1, d),
                 w_ple_gate.astype(_BF16), w_ple_proj.astype(_BF16), final_gain.reshape(1, d),
                 tm=FINAL_TILE)
    return out.reshape(b, s, d)


def kernel(x, p, attn_norm, w_in, b_forget, w_fox_branch, w_moba_branch, w_out, moe_norm, w_group,
           b_group, w_fine, b_fine, w_gate, w_up, w_down, ple_norm, w_ple_gate, w_ple_proj, final_norm):
    depth = p.shape[0]
    assert depth == 1, "the final norm is fused into the last layer's kernel"
    i = 0
    return _layer(x, p[i], attn_norm[i], w_in[i], b_forget[i], w_fox_branch[i], w_moba_branch[i],
                  w_out[i], moe_norm[i], w_group[i], b_group[i], w_fine[i], b_fine[i], w_gate[i],
                  w_up[i], w_down[i], ple_norm[i], w_ple_gate[i], w_ple_proj[i], final_norm)
```

```python
import functools

import jax
import jax.numpy as jnp
from jax import lax
from jax.experimental import pallas as pl
from jax.experimental.pallas import tpu as pltpu
from jax.experimental.pallas import tpu_sc as plsc

HEAD_DIM = 64
N_FOX_HEADS = 8
N_MOBA_HEADS = 8
MOBA_BLOCK = 256
MOBA_TOPK = 3
ROPE_THETA = 10000.0
N_GROUPS = 4
EXPERTS_PER_GROUP = 8
N_EXPERTS = N_GROUPS * EXPERTS_PER_GROUP
RMS_EPS = 1e-6

LANES = 128
HEADS_PER_STEP = LANES // HEAD_DIM
ATTN_Q = 256
ATTN_K = 256
PIPE_UNITS = 8
SOFTMAX_LAG = 4
ONES_ROWS = 16
SOFTMAX_ROWS = 64
BIAS_PIECES = 3
ROW_TILE = 512
SC_CORES = 2
SC_SUBCORES = 16
SC_CHUNK = 64
SC_SCATTER_CHUNK = 128
FINAL_TILE = 256
EXPERT_TILE = 512
LOG2_E = 1.4426950408889634
NEG_BIG = -1e30
VMEM_LIMIT = 48 * 1024 * 1024

_BF16 = jnp.bfloat16
_F32 = jnp.float32


def _params(*sem):
    return pltpu.CompilerParams(dimension_semantics=sem, vmem_limit_bytes=VMEM_LIMIT)


def _rms(x, g):
    return x * lax.rsqrt(jnp.mean(x * x, axis=-1, keepdims=True) + RMS_EPS) * g


def _const_spec(shape):
    return pl.BlockSpec(shape, lambda *_: (0,) * len(shape))


def _inproj_body(x_ref, g_ref, w_ref, wt_ref, wf_ref, bf_ref, cos_ref, sin_ref, cost_ref, sint_ref,
                 qft_ref, kf_ref, vft_ref, qmt_ref, km_ref, vmt_ref, gates_ref, cb_ref, kmean_ref,
                 carry_ref, *, tm, width, gate_cols):
    j = pl.program_id(1)
    h = _rms(x_ref[...], g_ref[...])
    hb = h.astype(_BF16)
    hbt = h.T.astype(_BF16)
    reps = width // LANES
    half = HEAD_DIM // 2

    def proj(c):
        return jnp.dot(hb, w_ref[:, c * width:(c + 1) * width], preferred_element_type=_F32)

    def proj_t(c):
        return jnp.dot(wt_ref[c * width:(c + 1) * width, :], hbt, preferred_element_type=_F32)

    def rope(t):
        cos = jnp.concatenate([cos_ref[...]] * reps, axis=1)
        sin = jnp.concatenate([sin_ref[...]] * reps, axis=1)
        first = (lax.broadcasted_iota(jnp.int32, (1, width), 1) % HEAD_DIM) < half
        partner = jnp.where(first, pltpu.roll(t, width - half, 1), pltpu.roll(t, half, 1))
        return t * cos + partner * sin

    def rope_t(t):
        cos = jnp.concatenate([cost_ref[...]] * reps, axis=0)
        sin = jnp.concatenate([sint_ref[...]] * reps, axis=0)
        first = (lax.broadcasted_iota(jnp.int32, (width, 1), 0) % HEAD_DIM) < half
        partner = jnp.where(first, pltpu.roll(t, width - half, 0), pltpu.roll(t, half, 0))
        return t * cos + partner * sin

    scale = HEAD_DIM ** -0.5 * LOG2_E
    qft_ref[...] = (proj_t(0) * scale).astype(_BF16)
    vft_ref[...] = proj_t(1).astype(_BF16)
    qmt_ref[...] = (rope_t(proj_t(2)) * scale).astype(_BF16)
    vmt_ref[...] = proj_t(3).astype(_BF16)
    kf_ref[...] = proj(0).astype(_BF16)
    km = rope(proj(1))
    km_ref[...] = km.astype(_BF16)

    @pl.when(j == 0)
    def _():
        kmean_ref[...] = jnp.zeros_like(kmean_ref)
        carry_ref[...] = jnp.zeros_like(carry_ref)

    blocks = tm // MOBA_BLOCK
    for b in range(blocks):
        kmean_ref[pl.ds(j * blocks + b, 1), :] = jnp.mean(
            km[b * MOBA_BLOCK:(b + 1) * MOBA_BLOCK], axis=0, keepdims=True)

    for c in range(gate_cols // width):
        g = jnp.dot(hb, w_ref[:, (2 + c) * width:(3 + c) * width],
                    preferred_element_type=_F32)
        gates_ref[:, c * width:(c + 1) * width] = jax.nn.sigmoid(g).astype(_BF16)

    z = jnp.dot(hb, wf_ref[...], preferred_element_type=_F32) + bf_ref[...]
    log_f = jnp.minimum(z, 0.0) - jnp.log1p(jnp.exp(-jnp.abs(z)))
    r = lax.broadcasted_iota(jnp.int32, (LANES, LANES), 0)
    c = lax.broadcasted_iota(jnp.int32, (LANES, LANES), 1)
    tri = (c <= r).astype(_BF16)
    offset = carry_ref[0:1, :]
    blocks_cum = []
    for blk in range(tm // LANES):
        rest = log_f[blk * LANES:(blk + 1) * LANES]
        within = jnp.zeros((LANES, LANES), _F32)
        for _ in range(BIAS_PIECES):
            part = rest.astype(_BF16)
            rest = rest - part.astype(_F32)
            within = within + jnp.dot(tri, part, preferred_element_type=_F32)
        blocks_cum.append(within + offset)
        offset = offset + within[LANES - 1:LANES, :]
    cum = jnp.concatenate(blocks_cum, axis=0)
    carry_ref[...] = jnp.broadcast_to(offset, carry_ref.shape)

    head_lane = lax.broadcasted_iota(jnp.int32, (1, LANES), 1) < N_FOX_HEADS
    rest = jnp.where(head_lane, -cum * LOG2_E, 0.0)
    placed = jnp.zeros((tm, LANES), _F32)
    for piece in range(BIAS_PIECES):
        part = rest.astype(_BF16).astype(_F32)
        rest = rest - part
        placed = placed + (pltpu.roll(part, N_FOX_HEADS * piece, 1) if piece else part)
    cb_ref[...] = placed.astype(_BF16)


def _inproj(x, g, w, wt, wf, bfp, cos, sin, cos_t, sin_t, *, tm):
    b, s, d = x.shape
    width = N_FOX_HEADS * HEAD_DIM
    gate_cols = w.shape[1] - 2 * width
    act = jax.ShapeDtypeStruct((b, s, width), _BF16)
    act_t = jax.ShapeDtypeStruct((b, width, s), _BF16)
    act_spec = pl.BlockSpec((None, tm, width), lambda bi, j: (bi, j, 0))
    act_t_spec = pl.BlockSpec((None, width, tm), lambda bi, j: (bi, 0, j))
    return pl.pallas_call(
        functools.partial(_inproj_body, tm=tm, width=width, gate_cols=gate_cols),
        grid=(b, s // tm),
        in_specs=[
            pl.BlockSpec((None, tm, d), lambda bi, j: (bi, j, 0)),
            _const_spec((1, d)),
            _const_spec(w.shape),
            _const_spec(wt.shape),
            _const_spec(wf.shape),
            _const_spec(bfp.shape),
            pl.BlockSpec((tm, LANES), lambda bi, j: (j, 0)),
            pl.BlockSpec((tm, LANES), lambda bi, j: (j, 0)),
            pl.BlockSpec((LANES, tm), lambda bi, j: (0, j)),
            pl.BlockSpec((LANES, tm), lambda bi, j: (0, j)),
        ],
        out_specs=[act_t_spec, act_spec, act_t_spec, act_t_spec, act_spec, act_t_spec,
                   pl.BlockSpec((None, tm, gate_cols), lambda bi, j: (bi, j, 0)),
                   pl.BlockSpec((None, tm, LANES), lambda bi, j: (bi, j, 0)),
                   pl.BlockSpec((None, LANES, width), lambda bi, j: (bi, 0, 0))],
        out_shape=[act_t, act, act_t, act_t, act, act_t,
                   jax.ShapeDtypeStruct((b, s, gate_cols), _BF16),
                   jax.ShapeDtypeStruct((b, s, LANES), _BF16),
                   jax.ShapeDtypeStruct((b, LANES, width), _F32)],
        scratch_shapes=[pltpu.VMEM((8, LANES), _F32)],
        compiler_params=_params("arbitrary", "arbitrary"),
        name="inproj",
    )(x, g, w, wt, wf, bfp, cos, sin, cos_t, sin_t)


def _attn_body(uq_ref, uk_ref, qt_ref, k_ref, kx_ref, vt_ref, aux_ref, o_ref,
               w_ref, m_ref, l_ref, acc_ref, st_ref, p_ref, alpha_ref, pmax_ref, *, mode, tq, tk, nb, n_diag, n_units):
    hp = pl.program_id(1)
    wide = HEADS_PER_STEP * tq
    nq = w_ref.shape[0]
    feat = lax.broadcasted_iota(jnp.int32, (LANES, 1), 0)
    xr = lax.broadcasted_iota(jnp.int32, (LANES, wide), 0)
    xc = lax.broadcasted_iota(jnp.int32, (LANES, wide), 1)

    def setup(i, carry):
        qt = qt_ref[:, pl.ds(pl.multiple_of(i * tq, tq), tq)]
        zero = jnp.zeros_like(qt)
        wq = jnp.concatenate([jnp.where(feat < HEAD_DIM, qt, zero),
                              jnp.where(feat < HEAD_DIM, zero, qt)], axis=1)
        if mode == "fox":
            head = hp * HEADS_PER_STEP + xc // tq
            lower = ((xr % N_FOX_HEADS == head) & (xr < BIAS_PIECES * N_FOX_HEADS)).astype(_BF16)
        else:
            own = (i * tq) // MOBA_BLOCK
            gate = jnp.dot(aux_ref[...].astype(_BF16), wq, preferred_element_type=_F32)[0:nb]
            blk = lax.broadcasted_iota(jnp.int32, (nb, wide), 0)
            gate = jnp.where(blk < own, gate, -jnp.inf)
            keep = blk == own
            for _ in range(MOBA_TOPK):
                mx = jnp.max(gate, axis=0, keepdims=True)
                cand = jnp.where((gate == mx) & (mx > -jnp.inf), blk, nb)
                pick = blk == jnp.min(cand, axis=0, keepdims=True)
                keep = keep | pick
                gate = jnp.where(pick, -jnp.inf, gate)
            lower = jnp.where(keep, 0.0, NEG_BIG)
            if nb < LANES:
                lower = jnp.concatenate([lower, jnp.zeros((LANES - nb, wide), _F32)], axis=0)
            lower = lower.astype(_BF16)
        w_ref[i] = jnp.concatenate([wq, lower], axis=0)
        m_ref[i] = jnp.full(m_ref.shape[1:], NEG_BIG, _F32)
        l_ref[i] = jnp.zeros(l_ref.shape[1:], _F32)
        acc_ref[i] = jnp.zeros(acc_ref.shape[1:], _F32)
        return carry

    lax.fori_loop(0, nq, setup, 0)

    def unit(t):
        return uq_ref[t], uk_ref[t]

    def scores(t, slot, causal):
        q, kb = unit(t)
        start = pl.multiple_of(kb * tk, tk)
        ka = jnp.concatenate([k_ref[pl.ds(start, tk), :], kx_ref[pl.ds(start, tk), :]], axis=1)
        st = jnp.dot(ka, w_ref[q], preferred_element_type=_F32)
        if causal:
            key = start + lax.broadcasted_iota(jnp.int32, (tk, wide), 0)
            qry = q * tq + lax.broadcasted_iota(jnp.int32, (tk, wide), 1) % tq
            st = jnp.where(key <= qry, st, NEG_BIG)
        st_ref[slot] = st
        pmax_ref[slot] = jnp.max(st.reshape(tk // 8, 8, wide), axis=0)

    def softmax(t, slot):
        q, _ = unit(t)
        rows = SOFTMAX_ROWS
        m_prev = m_ref[q][0:1, :]
        m_new = jnp.maximum(m_prev, jnp.max(pmax_ref[slot], axis=0, keepdims=True))
        for r in range(0, tk, rows):
            p_ref[slot, r:r + rows, :] = jnp.exp2(st_ref[slot, r:r + rows, :] - m_new).astype(_BF16)
        m_ref[q] = jnp.broadcast_to(m_new, m_ref.shape[1:])
        alpha_ref[slot] = jnp.broadcast_to(jnp.exp2(m_prev - m_new), alpha_ref.shape[1:])

    def values(t, slot):
        q, kb = unit(t)
        start = pl.multiple_of(kb * tk, tk)
        ones = jnp.ones((ONES_ROWS, tk), _BF16)
        for h in range(HEADS_PER_STEP):
            cols = slice(h * tq, (h + 1) * tq)
            feats = slice(h * HEAD_DIM, (h + 1) * HEAD_DIM)
            lhs = jnp.concatenate([vt_ref[feats, pl.ds(start, tk)], ones], axis=0)
            pv = jnp.dot(lhs, p_ref[slot, :, cols], preferred_element_type=_F32)
            alpha = alpha_ref[slot, 0:1, cols]
            acc_ref[q, feats, :] = acc_ref[q, feats, :] * alpha + pv[0:HEAD_DIM]
            l_ref[q, :, cols] = jnp.broadcast_to(
                alpha * l_ref[q, 0:1, cols] + pv[HEAD_DIM:HEAD_DIM + 1], (8, tq))

    def step(base, c, causal, do_scores=True, do_softmax=True, do_values=True):
        u = base + c
        if do_scores:
            scores(u, c % PIPE_UNITS, causal)
        if do_values:
            values(u - 2 * SOFTMAX_LAG, (c - 2 * SOFTMAX_LAG) % PIPE_UNITS)
        if do_softmax:
            softmax(u - SOFTMAX_LAG, (c - SOFTMAX_LAG) % PIPE_UNITS)

    def group(causal):
        def body(j, carry):
            for c in range(PIPE_UNITS):
                step(j * PIPE_UNITS, c, causal)
            return carry
        return body

    for c in range(PIPE_UNITS):
        step(0, c, True, do_softmax=c >= SOFTMAX_LAG, do_values=c >= 2 * SOFTMAX_LAG)
    lax.fori_loop(1, n_diag // PIPE_UNITS, group(True), 0)
    lax.fori_loop(n_diag // PIPE_UNITS, n_units // PIPE_UNITS, group(False), 0)
    for c in range(2 * SOFTMAX_LAG):
        step(n_units, c, False, do_scores=False, do_softmax=c < SOFTMAX_LAG)

    def finish(i, carry):
        out_t = jnp.concatenate(
            [acc_ref[i, h * HEAD_DIM:(h + 1) * HEAD_DIM, :] / l_ref[i, 0:1, h * tq:(h + 1) * tq]
             for h in range(HEADS_PER_STEP)], axis=0)
        o_ref[pl.ds(pl.multiple_of(i * tq, tq), tq), :] = out_t.T.astype(o_ref.dtype)
        return carry

    lax.fori_loop(0, nq, finish, 0, unroll=2)


def _attention(qt, k, kx, vt, aux, *, mode):
    b, width, s = qt.shape
    tq, tk = ATTN_Q, ATTN_K
    nq = s // tq
    nb = s // MOBA_BLOCK
    wide = HEADS_PER_STEP * tq
    units = [(i, (i * tq) // tk) for i in range(nq)]
    units += [(i, n) for i in range(nq) for n in range((i * tq) // tk)]
    assert nq % PIPE_UNITS == 0 and len(units) % PIPE_UNITS == 0 and PIPE_UNITS >= 2 * SOFTMAX_LAG
    unit_q = jnp.asarray([u[0] for u in units], jnp.int32)
    unit_k = jnp.asarray([u[1] for u in units], jnp.int32)
    if mode == "fox":
        aux_spec = pl.BlockSpec(aux.shape, lambda bi, hp, uq, uk: (0, 0))
        kx_spec = pl.BlockSpec((None, s, LANES), lambda bi, hp, uq, uk: (bi, 0, 0))
    else:
        aux_spec = pl.BlockSpec((None, LANES, LANES), lambda bi, hp, uq, uk: (bi, 0, hp))
        kx_spec = pl.BlockSpec(kx.shape, lambda bi, hp, uq, uk: (0, 0))
    return pl.pallas_call(
        functools.partial(_attn_body, mode=mode, tq=tq, tk=tk, nb=nb, n_diag=nq, n_units=len(units)),
        grid_spec=pltpu.PrefetchScalarGridSpec(
            num_scalar_prefetch=2,
            grid=(b, width // LANES),
            in_specs=[
                pl.BlockSpec((None, LANES, s), lambda bi, hp, uq, uk: (bi, hp, 0)),
                pl.BlockSpec((None, s, LANES), lambda bi, hp, uq, uk: (bi, 0, hp)),
                kx_spec,
                pl.BlockSpec((None, LANES, s), lambda bi, hp, uq, uk: (bi, hp, 0)),
                aux_spec,
            ],
            out_specs=pl.BlockSpec((None, s, LANES), lambda bi, hp, uq, uk: (bi, 0, hp)),
            scratch_shapes=[
                pltpu.VMEM((nq, 2 * LANES, wide), _BF16),
                pltpu.VMEM((nq, 8, wide), _F32),
                pltpu.VMEM((nq, 8, wide), _F32),
                pltpu.VMEM((nq, LANES, tq), _F32),
                pltpu.VMEM((PIPE_UNITS, tk, wide), _F32),
                pltpu.VMEM((PIPE_UNITS, tk, wide), _BF16),
                pltpu.VMEM((PIPE_UNITS, 8, wide), _F32),
                pltpu.VMEM((PIPE_UNITS, 8, wide), _F32),
            ],
        ),
        out_shape=jax.ShapeDtypeStruct((b, s, width), _BF16),
        compiler_params=_params("arbitrary", "arbitrary"),
        name=mode,
    )(unit_q, unit_k, qt, k, kx, vt, aux)


def _pack_bf16_pair(a, b):
    lo = pltpu.bitcast(a.astype(_BF16).astype(_F32), jnp.uint32) >> 16
    hi = pltpu.bitcast(b.astype(_BF16).astype(_F32), jnp.uint32) & jnp.uint32(0xFFFF0000)
    return lo | hi


def _unpack_bf16_pair(u):
    lo = pltpu.bitcast(u << 16, _F32)
    hi = pltpu.bitcast(u & jnp.uint32(0xFFFF0000), _F32)
    return jnp.concatenate([lo, hi], axis=1)


def _postattn_body(yf_ref, ym_ref, gates_ref, x_ref, wfb_ref, wmb_ref, wout_ref, g_ref, wr_ref, br_ref,
                   x1_ref, h2_ref, route_ref, meta_ref, counts_ref, carry_ref, *, tm, d):
    step = pl.program_id(0)
    ya = jnp.dot(yf_ref[...], wfb_ref[...], preferred_element_type=_F32)
    yb = jnp.dot(ym_ref[...], wmb_ref[...], preferred_element_type=_F32)
    mixed = gates_ref[:, 0:d].astype(_F32) * ya + gates_ref[:, d:2 * d].astype(_F32) * yb
    x1 = x_ref[...] + jnp.dot(mixed.astype(_BF16), wout_ref[...], preferred_element_type=_F32)
    x1_ref[...] = x1
    h2 = _rms(x1, g_ref[...])
    h2_ref[...] = _pack_bf16_pair(h2[:, 0:d // 2], h2[:, d // 2:d])

    logits = jnp.dot(h2.astype(_BF16), wr_ref[...], preferred_element_type=_F32) + br_ref[...]
    lane = lax.broadcasted_iota(jnp.int32, (tm, LANES), 1)
    is_group = (lane >= N_EXPERTS) & (lane < N_EXPERTS + N_GROUPS)
    gl = jnp.where(is_group, logits, -jnp.inf)
    ge = jnp.exp(gl - jnp.max(gl, axis=-1, keepdims=True))
    gp = ge / jnp.sum(ge, axis=-1, keepdims=True)
    g_top = jnp.max(gp, axis=-1, keepdims=True)
    g_idx = jnp.min(jnp.where(is_group & (gp == g_top), lane, 2 * LANES), axis=-1,
                    keepdims=True) - N_EXPERTS
    in_group = (lane < N_EXPERTS) & ((lane // EXPERTS_PER_GROUP) == g_idx)
    fl = jnp.where(in_group, logits, -jnp.inf)
    f1 = jnp.max(fl, axis=-1, keepdims=True)
    e1 = jnp.min(jnp.where(fl == f1, lane, 2 * LANES), axis=-1, keepdims=True)
    fl2 = jnp.where(lane == e1, -jnp.inf, fl)
    f2 = jnp.max(fl2, axis=-1, keepdims=True)
    e2 = jnp.min(jnp.where(fl2 == f2, lane, 2 * LANES), axis=-1, keepdims=True)
    t2 = jnp.exp(f2 - f1)
    w1 = g_top * (1.0 / (1.0 + t2))
    w2 = g_top * (t2 / (1.0 + t2))

    @pl.when(step == 0)
    def _():
        carry_ref[...] = jnp.zeros_like(carry_ref)

    onehot = ((lane == e1) | (lane == e2)).astype(_F32)
    rr = lax.broadcasted_iota(jnp.int32, (tm, tm), 0)
    cc = lax.broadcasted_iota(jnp.int32, (tm, tm), 1)
    before = jnp.dot((cc < rr).astype(_BF16), onehot.astype(_BF16), preferred_element_type=_F32)
    before = before + carry_ref[0:1, :]
    r1 = jnp.sum(jnp.where(lane == e1, before, 0.0), axis=-1, keepdims=True)
    r2 = jnp.sum(jnp.where(lane == e2, before, 0.0), axis=-1, keepdims=True)
    total = carry_ref[0:1, :] + jnp.sum(onehot, axis=0, keepdims=True)
    carry_ref[...] = jnp.broadcast_to(total, carry_ref.shape)
    counts_ref[...] = jnp.broadcast_to(total, counts_ref.shape)

    slab = jnp.where(lane == 0, w1, 0.0)
    slab = jnp.where(lane == 1, w2, slab)
    slab = jnp.where(lane == 2, e1.astype(_F32), slab)
    slab = jnp.where(lane == 3, e2.astype(_F32), slab)
    slab = jnp.where(lane == 4, r1, slab)
    slab = jnp.where(lane == 5, r2, slab)
    route_ref[...] = slab
    meta_ref[...] = slab.T[0:8].astype(jnp.int32)


def _postattn(yf, ym, gates, x, wfb, wmb, wout, g, wr, br, *, tm):
    t, d = x.shape
    width = yf.shape[1]
    row = lambda cols: pl.BlockSpec((tm, cols), lambda i: (i, 0))
    return pl.pallas_call(
        functools.partial(_postattn_body, tm=tm, d=d),
        grid=(t // tm,),
        in_specs=[row(width), row(width), row(2 * d), row(d),
                  _const_spec(wfb.shape), _const_spec(wmb.shape), _const_spec(wout.shape),
                  _const_spec((1, d)), _const_spec(wr.shape), _const_spec(br.shape)],
        out_specs=[row(d), row(d // 2), row(LANES), pl.BlockSpec((8, tm), lambda i: (0, i)),
                   _const_spec((8, LANES))],
        out_shape=[jax.ShapeDtypeStruct((t, d), _F32),
                   jax.ShapeDtypeStruct((t, d // 2), jnp.uint32),
                   jax.ShapeDtypeStruct((t, LANES), _F32),
                   jax.ShapeDtypeStruct((8, t), jnp.int32),
                   jax.ShapeDtypeStruct((8, LANES), _F32)],
        scratch_shapes=[pltpu.VMEM((8, LANES), _F32)],
        compiler_params=_params("arbitrary"),
        name="postattn",
    )(yf, ym, gates, x, wfb, wmb, wout, g, wr, br)


def _sc_scatter(rows, idx, n_out):
    n, width = rows.shape
    workers = SC_CORES * SC_SUBCORES
    per_worker = n // workers
    chunks = per_worker // SC_SCATTER_CHUNK
    assert n % (workers * SC_SCATTER_CHUNK) == 0 and idx.shape == (2, n // SC_SCATTER_CHUNK, SC_SCATTER_CHUNK)
    mesh = plsc.VectorSubcoreMesh(core_axis_name="c", subcore_axis_name="s",
                                  num_cores=SC_CORES, num_subcores=SC_SUBCORES)

    @functools.partial(
        pl.kernel, out_type=jax.ShapeDtypeStruct((n_out, width), rows.dtype), mesh=mesh,
        scratch_types=[pltpu.VMEM((2, chunks, SC_SCATTER_CHUNK), jnp.int32),
                       pltpu.VMEM((SC_SCATTER_CHUNK, width), rows.dtype),
                       pltpu.SemaphoreType.DMA],
        name="sc_scatter")
    def scatter(rows_hbm, idx_hbm, out_hbm, idx_v, rows_v, sem):
        worker = lax.axis_index("s") * SC_CORES + lax.axis_index("c")
        for k in range(2):
            pltpu.sync_copy(idx_hbm.at[k, pl.ds(worker * chunks, chunks)], idx_v.at[k])

        def chunk(j, carry):
            start = worker * per_worker + j * SC_SCATTER_CHUNK
            pltpu.sync_copy(rows_hbm.at[pl.ds(start, SC_SCATTER_CHUNK)], rows_v)
            for k in range(2):
                pltpu.async_copy(rows_v, out_hbm.at[idx_v.at[k, j]], sem).wait()
            return carry

        lax.fori_loop(0, chunks, chunk, 0)

    return scatter(rows, idx)


def _experts_body(be_ref, valid_ref, nused_ref, xs_ref, wg_ref, wu_ref, wd_ref, out_ref,
                  wgb_ref, wub_ref, wdb_ref):
    j = pl.program_id(0)
    used = j < nused_ref[0]

    @pl.when(jnp.logical_not(used))
    def _():
        out_ref[...] = jnp.zeros_like(out_ref)

    @pl.when(used & ((j == 0) | (be_ref[j] != be_ref[jnp.maximum(j - 1, 0)])))
    def _():
        wgb_ref[...] = wg_ref[...].astype(_BF16)
        wub_ref[...] = wu_ref[...].astype(_BF16)
        wdb_ref[...] = wd_ref[...].astype(_BF16)

    @pl.when(used)
    def _():
        row = lax.broadcasted_iota(jnp.int32, xs_ref.shape, 0)
        packed = jnp.where(row < valid_ref[j], xs_ref[...], jnp.uint32(0))
        rows = _unpack_bf16_pair(packed).astype(_BF16)
        gate = jnp.dot(rows, wgb_ref[...], preferred_element_type=_F32)
        up = jnp.dot(rows, wub_ref[...], preferred_element_type=_F32)
        act = (gate * jax.nn.sigmoid(gate)) * up
        out = jnp.dot(act.astype(_BF16), wdb_ref[...], preferred_element_type=_F32)
        half = out.shape[1] // 2
        out_ref[...] = _pack_bf16_pair(out[:, 0:half], out[:, half:])


def _experts(block_expert, block_valid, n_used, xs, wg, wu, wd, *, te):
    n_rows, half = xs.shape
    _, d, de = wg.shape

    def blk(j, be, bv, nu):
        return jnp.minimum(j, nu[0] - 1)

    def weight(j, be, bv, nu):
        return (be[blk(j, be, bv, nu)], 0, 0)

    return pl.pallas_call(
        _experts_body,
        grid_spec=pltpu.PrefetchScalarGridSpec(
            num_scalar_prefetch=3,
            grid=(n_rows // te,),
            in_specs=[pl.BlockSpec((te, half), lambda j, be, bv, nu: (blk(j, be, bv, nu), 0)),
                      pl.BlockSpec((None, d, de), weight),
                      pl.BlockSpec((None, d, de), weight),
                      pl.BlockSpec((None, de, d), weight)],
            out_specs=pl.BlockSpec((te, d // 2), lambda j, be, bv, nu: (j, 0)),
            scratch_shapes=[pltpu.VMEM((d, de), _BF16), pltpu.VMEM((d, de), _BF16),
                            pltpu.VMEM((de, d), _BF16)],
        ),
        out_shape=jax.ShapeDtypeStruct((n_rows, d // 2), jnp.uint32),
        compiler_params=_params("arbitrary"),
        name="experts",
    )(block_expert, block_valid, n_used, xs, wg, wu, wd)


def _sc_gather(table, idx):
    n, = idx.shape
    width = table.shape[1]
    workers = SC_CORES * SC_SUBCORES
    per_worker = n // workers
    chunks = per_worker // SC_CHUNK
    assert n % (workers * SC_CHUNK) == 0
    mesh = plsc.VectorSubcoreMesh(core_axis_name="c", subcore_axis_name="s",
                                  num_cores=SC_CORES, num_subcores=SC_SUBCORES)

    @functools.partial(
        pl.kernel, out_type=jax.ShapeDtypeStruct((n, width), table.dtype), mesh=mesh,
        scratch_types=[pltpu.VMEM((per_worker,), jnp.int32),
                       pltpu.VMEM((SC_CHUNK, width), table.dtype),
                       pltpu.SemaphoreType.DMA],
        name="sc_gather")
    def gather(table_hbm, idx_hbm, out_hbm, idx_v, rows_v, sem):
        worker = lax.axis_index("s") * SC_CORES + lax.axis_index("c")
        base = worker * per_worker
        pltpu.sync_copy(idx_hbm.at[pl.ds(base, per_worker)], idx_v)

        def chunk(j, carry):
            off = pl.multiple_of(j * SC_CHUNK, SC_CHUNK)
            pltpu.async_copy(table_hbm.at[idx_v.at[pl.ds(off, SC_CHUNK)]], rows_v, sem).wait()
            pltpu.sync_copy(rows_v, out_hbm.at[pl.ds(base + off, SC_CHUNK)])
            return carry

        lax.fori_loop(0, chunks, chunk, 0)

    return gather(table, idx)


def _final_body(x1_ref, route_ref, p_ref, r1_ref, r2_ref, gp_ref, wpg_ref, wpp_ref, gf_ref, o_ref):
    emb = jnp.dot(p_ref[...].astype(_BF16), wpp_ref[...], preferred_element_type=_F32)
    route = route_ref[...]
    x2 = x1_ref[...] + (_unpack_bf16_pair(r1_ref[...]) * route[:, 0:1]
                        + _unpack_bf16_pair(r2_ref[...]) * route[:, 1:2])
    gate = jax.nn.sigmoid(jnp.dot(_rms(x2, gp_ref[...]).astype(_BF16), wpg_ref[...],
                                  preferred_element_type=_F32))
    o_ref[...] = _rms(x2 + gate * emb, gf_ref[...])


def _final(x1, route, p, slot_rows, gp, wpg, wpp, gf, *, tm):
    t, d = x1.shape
    ple = p.shape[1]
    steps = t // tm
    row = lambda cols: pl.BlockSpec((tm, cols), lambda i: (i, 0))
    return pl.pallas_call(
        _final_body,
        grid=(steps,),
        in_specs=[row(d), row(LANES), row(ple), row(d // 2),
                  pl.BlockSpec((tm, d // 2), lambda i: (steps + i, 0)),
                  _const_spec((1, d)), _const_spec(wpg.shape), _const_spec(wpp.shape),
                  _const_spec((1, d))],
        out_specs=row(d),
        out_shape=jax.ShapeDtypeStruct((t, d), _F32),
        compiler_params=_params("arbitrary"),
        name="final",
    )(x1, route, p, slot_rows, slot_rows, gp, wpg, wpp, gf)


def _rope_tables(s):
    half = HEAD_DIM // 2
    inv = 1.0 / (ROPE_THETA ** (jnp.arange(0, HEAD_DIM, 2, dtype=_F32) / HEAD_DIM))
    ang = jnp.arange(s, dtype=_F32)[:, None] * inv[None, :]
    cos, sin = jnp.cos(ang), jnp.sin(ang)
    reps = LANES // half
    cos_t = jnp.tile(cos, (1, reps))
    sin_t = jnp.tile(jnp.concatenate([-sin, sin], axis=1), (1, reps // 2))
    return cos_t, sin_t


def _layer(x, p, attn_norm, w_in, b_forget, w_fox_branch, w_moba_branch, w_out, moe_norm,
           w_group, b_group, w_fine, b_fine, w_gate, w_up, w_down, ple_norm, w_ple_gate, w_ple_proj,
           final_gain):
    b, s, d = x.shape
    t = b * s
    width = N_FOX_HEADS * HEAD_DIM
    assert N_MOBA_HEADS * HEAD_DIM == width and MOBA_BLOCK % ATTN_Q == 0 and ATTN_K % MOBA_BLOCK == 0
    assert BIAS_PIECES * N_FOX_HEADS <= LANES and s % ATTN_K == 0
    assert s % ROW_TILE == 0 and ROW_TILE % MOBA_BLOCK == 0 and s // MOBA_BLOCK <= LANES
    assert t % FINAL_TILE == 0 and N_EXPERTS + N_GROUPS <= LANES

    qkv_cols = 6 * width
    f_cols = N_FOX_HEADS
    chunk = lambda c: w_in[:, c * width:(c + 1) * width]
    w_rows = jnp.concatenate([chunk(1), chunk(4), w_in[:, qkv_cols + f_cols:]], axis=1).astype(_BF16)
    w_cols = jnp.concatenate([chunk(0), chunk(2), chunk(3), chunk(5)], axis=1).T.astype(_BF16)
    wf = jnp.zeros((d, LANES), _BF16).at[:, :f_cols].set(w_in[:, qkv_cols:qkv_cols + f_cols].astype(_BF16))
    bfp = jnp.zeros((1, LANES), _F32).at[0, :f_cols].set(b_forget.astype(_F32))
    cos, sin = _rope_tables(s)

    qft, kf, vft, qmt, km, vmt, gates, cb, kmean = _inproj(
        x, attn_norm.reshape(1, d), w_rows, w_cols, wf, bfp, cos, sin, cos.T, sin.T, tm=ROW_TILE)
    y_fox = _attention(qft, kf, cb, vft, jnp.zeros((8, LANES), _F32), mode="fox")
    block_of_key = jnp.arange(s, dtype=jnp.int32)[:, None] // MOBA_BLOCK
    block_onehot = (block_of_key == jnp.arange(LANES, dtype=jnp.int32)[None, :]).astype(_BF16)
    y_moba = _attention(qmt, km, block_onehot, vmt, kmean, mode="moba")

    wr = jnp.zeros((d, LANES), _BF16)
    wr = wr.at[:, :N_EXPERTS].set(w_fine.astype(_BF16))
    wr = wr.at[:, N_EXPERTS:N_EXPERTS + N_GROUPS].set(w_group.astype(_BF16))
    br = jnp.zeros((1, LANES), _F32)
    br = br.at[0, :N_EXPERTS].set(b_fine.astype(_F32))
    br = br.at[0, N_EXPERTS:N_EXPERTS + N_GROUPS].set(b_group.astype(_F32))
    x1, h2p, route, meta, counts = _postattn(
        y_fox.reshape(t, width), y_moba.reshape(t, width), gates.reshape(t, 2 * d), x.reshape(t, d),
        w_fox_branch.astype(_BF16), w_moba_branch.astype(_BF16), w_out.astype(_BF16),
        moe_norm.reshape(1, d), wr, br, tm=ROW_TILE)

    te = EXPERT_TILE
    n_blk = (2 * t) // te + N_EXPERTS
    counts = counts[0, :N_EXPERTS].astype(jnp.int32)
    blocks_per = (counts + te - 1) // te
    block_end = jnp.cumsum(blocks_per)
    row_start = (block_end - blocks_per) * te
    expert_ids = jnp.arange(N_EXPERTS, dtype=jnp.int32)[:, None, None]
    slot_major = meta[4:6] + jnp.sum(
        jnp.where(meta[2:4][None] == expert_ids, row_start[:, None, None], 0), axis=0)
    n_used = block_end[-1:].astype(jnp.int32)
    block_ids = jnp.arange(n_blk, dtype=jnp.int32)
    block_expert = jnp.minimum(
        jnp.sum((block_ids[:, None] >= block_end[None, :]).astype(jnp.int32), axis=1), N_EXPERTS - 1)

    first_block = block_end - blocks_per
    block_valid = jnp.clip(counts[block_expert] - (block_ids - first_block[block_expert]) * te, 0, te)
    xs = _sc_scatter(h2p, slot_major.reshape(2, t // SC_SCATTER_CHUNK, SC_SCATTER_CHUNK), n_blk * te)
    rows = _experts(block_expert, block_valid, n_used, xs, w_gate, w_up, w_down, te=te)
    slot_rows = _sc_gather(rows, slot_major.reshape(-1))
    out = _final(x1, route, p.reshape(t, -1), slot_rows, ple_norm.reshape(1, d),
                 w_ple_gate.astype(_BF16), w_ple_proj.astype(_BF16), final_gain.reshape(1, d),
                 tm=FINAL_TILE)
    return out.reshape(b, s, d)


def kernel(x, p, attn_norm, w_in, b_forget, w_fox_branch, w_moba_branch, w_out, moe_norm, w_group,
           b_group, w_fine, b_fine, w_gate, w_up, w_down, ple_norm, w_ple_gate, w_ple_proj, final_norm):
    depth = p.shape[0]
    assert depth == 1, "the final norm is fused into the last layer's kernel"
    i = 0
    return _layer(x, p[i], attn_norm[i], w_in[i], b_forget[i], w_fox_branch[i], w_moba_branch[i],
                  w_out[i], moe_norm[i], w_group[i], b_group[i], w_fine[i], b_fine[i], w_gate[i],
                  w_up[i], w_down[i], ple_norm[i], w_ple_gate[i], w_ple_proj[i], final_norm)
```

```python
import functools

import jax
import jax.numpy as jnp
from jax import lax
from jax.experimental import pallas as pl
from jax.experimental.pallas import tpu as pltpu
from jax.experimental.pallas import tpu_sc as plsc

HEAD_DIM = 64
N_FOX_HEADS = 8
N_MOBA_HEADS = 8
MOBA_BLOCK = 256
MOBA_TOPK = 3
ROPE_THETA = 10000.0
N_GROUPS = 4
EXPERTS_PER_GROUP = 8
N_EXPERTS = N_GROUPS * EXPERTS_PER_GROUP
RMS_EPS = 1e-6

LANES = 128
HEADS_PER_STEP = LANES // HEAD_DIM
ATTN_Q = 256
ATTN_K = 256
PIPE_UNITS = 8
SOFTMAX_LAG = 4
ONES_ROWS = 16
SOFTMAX_ROWS = 64
BIAS_PIECES = 3
ROW_TILE = 512
SC_CORES = 2
SC_SUBCORES = 16
SC_CHUNK = 64
SC_SCATTER_CHUNK = 128
FINAL_TILE = 256
EXPERT_TILE = 512
LOG2_E = 1.4426950408889634
UNDERFLOW_LOG2 = 160.0
NEG_BIG = -1e30
VMEM_LIMIT = 48 * 1024 * 1024

_BF16 = jnp.bfloat16
_F32 = jnp.float32


def _params(*sem):
    return pltpu.CompilerParams(dimension_semantics=sem, vmem_limit_bytes=VMEM_LIMIT)


def _rms(x, g):
    return x * lax.rsqrt(jnp.mean(x * x, axis=-1, keepdims=True) + RMS_EPS) * g


def _const_spec(shape):
    return pl.BlockSpec(shape, lambda *_: (0,) * len(shape))


def _inproj_body(x_ref, g_ref, w_ref, wt_ref, wf_ref, bf_ref, cos_ref, sin_ref, cost_ref, sint_ref,
                 qft_ref, kf_ref, vft_ref, qmt_ref, km_ref, vmt_ref, gates_ref, cb_ref, kmean_ref,
                 kabs_ref, bmax_ref,
                 carry_ref, *, tm, width, gate_cols):
    j = pl.program_id(1)
    h = _rms(x_ref[...], g_ref[...])
    hb = h.astype(_BF16)
    hbt = h.T.astype(_BF16)
    reps = width // LANES
    half = HEAD_DIM // 2

    def proj(c):
        return jnp.dot(hb, w_ref[:, c * width:(c + 1) * width], preferred_element_type=_F32)

    def proj_t(c):
        return jnp.dot(wt_ref[c * width:(c + 1) * width, :], hbt, preferred_element_type=_F32)

    def rope(t):
        cos = jnp.concatenate([cos_ref[...]] * reps, axis=1)
        sin = jnp.concatenate([sin_ref[...]] * reps, axis=1)
        first = (lax.broadcasted_iota(jnp.int32, (1, width), 1) % HEAD_DIM) < half
        partner = jnp.where(first, pltpu.roll(t, width - half, 1), pltpu.roll(t, half, 1))
        return t * cos + partner * sin

    def rope_t(t):
        cos = jnp.concatenate([cost_ref[...]] * reps, axis=0)
        sin = jnp.concatenate([sint_ref[...]] * reps, axis=0)
        first = (lax.broadcasted_iota(jnp.int32, (width, 1), 0) % HEAD_DIM) < half
        partner = jnp.where(first, pltpu.roll(t, width - half, 0), pltpu.roll(t, half, 0))
        return t * cos + partner * sin

    scale = HEAD_DIM ** -0.5 * LOG2_E
    qft_ref[...] = (proj_t(0) * scale).astype(_BF16)
    vft_ref[...] = proj_t(1).astype(_BF16)
    qmt_ref[...] = (rope_t(proj_t(2)) * scale).astype(_BF16)
    vmt_ref[...] = proj_t(3).astype(_BF16)
    kf = proj(0).astype(_BF16)
    kf_ref[...] = kf
    kf_abs = jnp.abs(kf.astype(_F32))
    km = rope(proj(1))
    km_ref[...] = km.astype(_BF16)

    @pl.when(j == 0)
    def _():
        kmean_ref[...] = jnp.zeros_like(kmean_ref)
        kabs_ref[...] = jnp.zeros_like(kabs_ref)
        bmax_ref[...] = jnp.zeros_like(bmax_ref)
        carry_ref[...] = jnp.zeros_like(carry_ref)

    blocks = tm // MOBA_BLOCK
    for b in range(blocks):
        rows = slice(b * MOBA_BLOCK, (b + 1) * MOBA_BLOCK)
        kmean_ref[pl.ds(j * blocks + b, 1), :] = jnp.mean(km[rows], axis=0, keepdims=True)
        kabs_ref[pl.ds(j * blocks + b, 1), :] = jnp.max(kf_abs[rows], axis=0, keepdims=True)

    for c in range(gate_cols // width):
        g = jnp.dot(hb, w_ref[:, (2 + c) * width:(3 + c) * width],
                    preferred_element_type=_F32)
        gates_ref[:, c * width:(c + 1) * width] = jax.nn.sigmoid(g).astype(_BF16)

    z = jnp.dot(hb, wf_ref[...], preferred_element_type=_F32) + bf_ref[...]
    log_f = jnp.minimum(z, 0.0) - jnp.log1p(jnp.exp(-jnp.abs(z)))
    r = lax.broadcasted_iota(jnp.int32, (LANES, LANES), 0)
    c = lax.broadcasted_iota(jnp.int32, (LANES, LANES), 1)
    tri = (c <= r).astype(_BF16)
    offset = carry_ref[0:1, :]
    blocks_cum = []
    for blk in range(tm // LANES):
        rest = log_f[blk * LANES:(blk + 1) * LANES]
        within = jnp.zeros((LANES, LANES), _F32)
        for _ in range(BIAS_PIECES):
            part = rest.astype(_BF16)
            rest = rest - part.astype(_F32)
            within = within + jnp.dot(tri, part, preferred_element_type=_F32)
        blocks_cum.append(within + offset)
        offset = offset + within[LANES - 1:LANES, :]
    cum = jnp.concatenate(blocks_cum, axis=0)
    carry_ref[...] = jnp.broadcast_to(offset, carry_ref.shape)

    head_lane = lax.broadcasted_iota(jnp.int32, (1, LANES), 1) < N_FOX_HEADS
    bias = jnp.where(head_lane, -cum * LOG2_E, 0.0)
    for b in range(blocks):
        bmax_ref[pl.ds(j * blocks + b, 1), :] = jnp.max(
            bias[b * MOBA_BLOCK:(b + 1) * MOBA_BLOCK], axis=0, keepdims=True)
    rest = bias
    placed = jnp.zeros((tm, LANES), _F32)
    for piece in range(BIAS_PIECES):
        part = rest.astype(_BF16).astype(_F32)
        rest = rest - part
        placed = placed + (pltpu.roll(part, N_FOX_HEADS * piece, 1) if piece else part)
    cb_ref[...] = placed.astype(_BF16)


def _inproj(x, g, w, wt, wf, bfp, cos, sin, cos_t, sin_t, *, tm):
    b, s, d = x.shape
    width = N_FOX_HEADS * HEAD_DIM
    gate_cols = w.shape[1] - 2 * width
    act = jax.ShapeDtypeStruct((b, s, width), _BF16)
    act_t = jax.ShapeDtypeStruct((b, width, s), _BF16)
    act_spec = pl.BlockSpec((None, tm, width), lambda bi, j: (bi, j, 0))
    act_t_spec = pl.BlockSpec((None, width, tm), lambda bi, j: (bi, 0, j))
    return pl.pallas_call(
        functools.partial(_inproj_body, tm=tm, width=width, gate_cols=gate_cols),
        grid=(b, s // tm),
        in_specs=[
            pl.BlockSpec((None, tm, d), lambda bi, j: (bi, j, 0)),
            _const_spec((1, d)),
            _const_spec(w.shape),
            _const_spec(wt.shape),
            _const_spec(wf.shape),
            _const_spec(bfp.shape),
            pl.BlockSpec((tm, LANES), lambda bi, j: (j, 0)),
            pl.BlockSpec((tm, LANES), lambda bi, j: (j, 0)),
            pl.BlockSpec((LANES, tm), lambda bi, j: (0, j)),
            pl.BlockSpec((LANES, tm), lambda bi, j: (0, j)),
        ],
        out_specs=[act_t_spec, act_spec, act_t_spec, act_t_spec, act_spec, act_t_spec,
                   pl.BlockSpec((None, tm, gate_cols), lambda bi, j: (bi, j, 0)),
                   pl.BlockSpec((None, tm, LANES), lambda bi, j: (bi, j, 0)),
                   pl.BlockSpec((None, LANES, width), lambda bi, j: (bi, 0, 0)),
                   pl.BlockSpec((None, LANES, width), lambda bi, j: (bi, 0, 0)),
                   pl.BlockSpec((None, LANES, LANES), lambda bi, j: (bi, 0, 0))],
        out_shape=[act_t, act, act_t, act_t, act, act_t,
                   jax.ShapeDtypeStruct((b, s, gate_cols), _BF16),
                   jax.ShapeDtypeStruct((b, s, LANES), _BF16),
                   jax.ShapeDtypeStruct((b, LANES, width), _F32),
                   jax.ShapeDtypeStruct((b, LANES, width), _F32),
                   jax.ShapeDtypeStruct((b, LANES, LANES), _F32)],
        scratch_shapes=[pltpu.VMEM((8, LANES), _F32)],
        compiler_params=_params("arbitrary", "arbitrary"),
        name="inproj",
    )(x, g, w, wt, wf, bfp, cos, sin, cos_t, sin_t)


def _attn_body(uq_ref, uk_ref, qt_ref, k_ref, kx_ref, vt_ref, aux_ref, aux2_ref, o_ref,
               w_ref, m_ref, l_ref, acc_ref, st_ref, p_ref, alpha_ref, pmax_ref, qn_ref, *,
               mode, tq, tk, nb, n_diag, n_units):
    hp = pl.program_id(1)
    wide = HEADS_PER_STEP * tq
    nq = w_ref.shape[0]
    feat = lax.broadcasted_iota(jnp.int32, (LANES, 1), 0)
    xr = lax.broadcasted_iota(jnp.int32, (LANES, wide), 0)
    xc = lax.broadcasted_iota(jnp.int32, (LANES, wide), 1)

    def setup(i, carry):
        qt = qt_ref[:, pl.ds(pl.multiple_of(i * tq, tq), tq)]
        zero = jnp.zeros_like(qt)
        wq = jnp.concatenate([jnp.where(feat < HEAD_DIM, qt, zero),
                              jnp.where(feat < HEAD_DIM, zero, qt)], axis=1)
        if mode == "fox":
            head = hp * HEADS_PER_STEP + xc // tq
            lower = ((xr % N_FOX_HEADS == head) & (xr < BIAS_PIECES * N_FOX_HEADS)).astype(_BF16)
            qn_ref[i] = jnp.broadcast_to(
                jnp.sum(jnp.abs(wq.astype(_F32)), axis=0, keepdims=True), qn_ref.shape[1:])
        else:
            own = (i * tq) // MOBA_BLOCK
            gate = jnp.dot(aux_ref[...].astype(_BF16), wq, preferred_element_type=_F32)[0:nb]
            blk = lax.broadcasted_iota(jnp.int32, (nb, wide), 0)
            gate = jnp.where(blk < own, gate, -jnp.inf)
            keep = blk == own
            for _ in range(MOBA_TOPK):
                mx = jnp.max(gate, axis=0, keepdims=True)
                cand = jnp.where((gate == mx) & (mx > -jnp.inf), blk, nb)
                pick = blk == jnp.min(cand, axis=0, keepdims=True)
                keep = keep | pick
                gate = jnp.where(pick, -jnp.inf, gate)
            lower = jnp.where(keep, 0.0, NEG_BIG)
            if nb < LANES:
                lower = jnp.concatenate([lower, jnp.zeros((LANES - nb, wide), _F32)], axis=0)
            lower = lower.astype(_BF16)
        w_ref[i] = jnp.concatenate([wq, lower], axis=0)
        m_ref[i] = jnp.full(m_ref.shape[1:], NEG_BIG, _F32)
        l_ref[i] = jnp.zeros(l_ref.shape[1:], _F32)
        acc_ref[i] = jnp.zeros(acc_ref.shape[1:], _F32)
        return carry

    lax.fori_loop(0, nq, setup, 0)

    def unit(t):
        return uq_ref[t], uk_ref[t]

    def scores(t, slot, causal):
        q, kb = unit(t)
        start = pl.multiple_of(kb * tk, tk)
        ka = jnp.concatenate([k_ref[pl.ds(start, tk), :], kx_ref[pl.ds(start, tk), :]], axis=1)
        st = jnp.dot(ka, w_ref[q], preferred_element_type=_F32)
        if causal:
            key = start + lax.broadcasted_iota(jnp.int32, (tk, wide), 0)
            qry = q * tq + lax.broadcasted_iota(jnp.int32, (tk, wide), 1) % tq
            st = jnp.where(key <= qry, st, NEG_BIG)
        st_ref[slot] = st
        pmax_ref[slot] = jnp.max(st.reshape(tk // 8, 8, wide), axis=0)

    def softmax(t, slot):
        q, _ = unit(t)
        rows = SOFTMAX_ROWS
        m_prev = m_ref[q][0:1, :]
        m_new = jnp.maximum(m_prev, jnp.max(pmax_ref[slot], axis=0, keepdims=True))
        for r in range(0, tk, rows):
            p_ref[slot, r:r + rows, :] = jnp.exp2(st_ref[slot, r:r + rows, :] - m_new).astype(_BF16)
        m_ref[q] = jnp.broadcast_to(m_new, m_ref.shape[1:])
        alpha_ref[slot] = jnp.broadcast_to(jnp.exp2(m_prev - m_new), alpha_ref.shape[1:])

    def values(t, slot):
        q, kb = unit(t)
        start = pl.multiple_of(kb * tk, tk)
        ones = jnp.ones((ONES_ROWS, tk), _BF16)
        for h in range(HEADS_PER_STEP):
            cols = slice(h * tq, (h + 1) * tq)
            feats = slice(h * HEAD_DIM, (h + 1) * HEAD_DIM)
            lhs = jnp.concatenate([vt_ref[feats, pl.ds(start, tk)], ones], axis=0)
            pv = jnp.dot(lhs, p_ref[slot, :, cols], preferred_element_type=_F32)
            alpha = alpha_ref[slot, 0:1, cols]
            acc_ref[q, feats, :] = acc_ref[q, feats, :] * alpha + pv[0:HEAD_DIM]
            l_ref[q, :, cols] = jnp.broadcast_to(
                alpha * l_ref[q, 0:1, cols] + pv[HEAD_DIM:HEAD_DIM + 1], (8, tq))

    def step(base, c, causal, do_scores=True, do_softmax=True, do_values=True):
        u = base + c
        if do_scores:
            scores(u, c % PIPE_UNITS, causal)
        if do_values:
            values(u - 2 * SOFTMAX_LAG, (c - 2 * SOFTMAX_LAG) % PIPE_UNITS)
        if do_softmax:
            softmax(u - SOFTMAX_LAG, (c - SOFTMAX_LAG) % PIPE_UNITS)

    def group(causal):
        def body(j, carry):
            for c in range(PIPE_UNITS):
                step(j * PIPE_UNITS, c, causal)
            return carry
        return body

    def pipeline(g_first, g_end, causal_groups):
        for c in range(PIPE_UNITS):
            step(g_first * PIPE_UNITS, c, g_first < causal_groups,
                 do_softmax=c >= SOFTMAX_LAG, do_values=c >= 2 * SOFTMAX_LAG)
        if g_first + 1 < causal_groups:
            lax.fori_loop(g_first + 1, causal_groups, group(True), 0)
        lax.fori_loop(max(g_first + 1, causal_groups), g_end, group(False), 0)
        for c in range(2 * SOFTMAX_LAG):
            step(g_end * PIPE_UNITS, c, False, do_scores=False, do_softmax=c < SOFTMAX_LAG)

    diag_groups = n_diag // PIPE_UNITS
    if mode == "moba":
        pipeline(0, n_units // PIPE_UNITS, diag_groups)
    else:
        pipeline(0, diag_groups, diag_groups)
        nq_blocks = n_diag
        col = lax.broadcasted_iota(jnp.int32, (LANES, LANES), 1)
        first_head = lax.broadcasted_iota(jnp.int32, (1, wide), 1) < tq
        head0 = hp * HEADS_PER_STEP

        def per_head(table, pick0, pick1, fill):
            v0 = jnp.max(jnp.where(pick0, table, fill), axis=1, keepdims=True)[0:nb]
            v1 = jnp.max(jnp.where(pick1, table, fill), axis=1, keepdims=True)[0:nb]
            return jnp.where(first_head, v0, v1)

        bias_max = per_head(aux2_ref[...], col == head0, col == head0 + 1, NEG_BIG)
        k_absmax = per_head(aux_ref[...], col < HEAD_DIM, col >= HEAD_DIM, 0.0)
        blk = lax.broadcasted_iota(jnp.int32, (nb, 1), 0)

        def scan(i, far):
            bound = bias_max + qn_ref[i][0:1, :] * k_absmax - m_ref[i][0:1, :]
            live = jnp.max(bound, axis=1, keepdims=True) >= -UNDERFLOW_LOG2
            dist = jnp.where(live & (blk < i), i - blk, 0)
            return jnp.maximum(far, jnp.max(dist))

        far = lax.fori_loop(1, nq_blocks, scan, jnp.int32(0))
        n_live = far * nq_blocks - (far * (far + 1)) // 2
        live_groups = (n_live + PIPE_UNITS - 1) // PIPE_UNITS

        @pl.when(live_groups > 0)
        def _():
            pipeline(diag_groups, diag_groups + live_groups, diag_groups)

    def finish(i, carry):
        out_t = jnp.concatenate(
            [acc_ref[i, h * HEAD_DIM:(h + 1) * HEAD_DIM, :] / l_ref[i, 0:1, h * tq:(h + 1) * tq]
             for h in range(HEADS_PER_STEP)], axis=0)
        o_ref[pl.ds(pl.multiple_of(i * tq, tq), tq), :] = out_t.T.astype(o_ref.dtype)
        return carry

    lax.fori_loop(0, nq, finish, 0, unroll=2)


def _attention(qt, k, kx, vt, aux, aux2, *, mode):
    b, width, s = qt.shape
    tq, tk = ATTN_Q, ATTN_K
    nq = s // tq
    nb = s // MOBA_BLOCK
    wide = HEADS_PER_STEP * tq
    assert tq == tk
    units = [(i, (i * tq) // tk) for i in range(nq)]
    units += [(i, i - dist) for dist in range(1, nq) for i in range(dist, nq)]
    assert nq % PIPE_UNITS == 0 and len(units) % PIPE_UNITS == 0 and PIPE_UNITS >= 2 * SOFTMAX_LAG
    unit_q = jnp.asarray([u[0] for u in units], jnp.int32)
    unit_k = jnp.asarray([u[1] for u in units], jnp.int32)
    aux_spec = pl.BlockSpec((None, LANES, LANES), lambda bi, hp, uq, uk: (bi, 0, hp))
    aux2_spec = pl.BlockSpec((None, LANES, LANES), lambda bi, hp, uq, uk: (bi, 0, 0))
    if mode == "fox":
        kx_spec = pl.BlockSpec((None, s, LANES), lambda bi, hp, uq, uk: (bi, 0, 0))
    else:
        kx_spec = pl.BlockSpec(kx.shape, lambda bi, hp, uq, uk: (0, 0))
    return pl.pallas_call(
        functools.partial(_attn_body, mode=mode, tq=tq, tk=tk, nb=nb, n_diag=nq, n_units=len(units)),
        grid_spec=pltpu.PrefetchScalarGridSpec(
            num_scalar_prefetch=2,
            grid=(b, width // LANES),
            in_specs=[
                pl.BlockSpec((None, LANES, s), lambda bi, hp, uq, uk: (bi, hp, 0)),
                pl.BlockSpec((None, s, LANES), lambda bi, hp, uq, uk: (bi, 0, hp)),
                kx_spec,
                pl.BlockSpec((None, LANES, s), lambda bi, hp, uq, uk: (bi, hp, 0)),
                aux_spec,
                aux2_spec,
            ],
            out_specs=pl.BlockSpec((None, s, LANES), lambda bi, hp, uq, uk: (bi, 0, hp)),
            scratch_shapes=[
                pltpu.VMEM((nq, 2 * LANES, wide), _BF16),
                pltpu.VMEM((nq, 8, wide), _F32),
                pltpu.VMEM((nq, 8, wide), _F32),
                pltpu.VMEM((nq, LANES, tq), _F32),
                pltpu.VMEM((PIPE_UNITS, tk, wide), _F32),
                pltpu.VMEM((PIPE_UNITS, tk, wide), _BF16),
                pltpu.VMEM((PIPE_UNITS, 8, wide), _F32),
                pltpu.VMEM((PIPE_UNITS, 8, wide), _F32),
                pltpu.VMEM((nq, 8, wide), _F32),
            ],
        ),
        out_shape=jax.ShapeDtypeStruct((b, s, width), _BF16),
        compiler_params=_params("arbitrary", "arbitrary"),
        name=mode,
    )(unit_q, unit_k, qt, k, kx, vt, aux, aux2)


def _pack_bf16_pair(a, b):
    lo = pltpu.bitcast(a.astype(_BF16).astype(_F32), jnp.uint32) >> 16
    hi = pltpu.bitcast(b.astype(_BF16).astype(_F32), jnp.uint32) & jnp.uint32(0xFFFF0000)
    return lo | hi


def _unpack_bf16_pair(u):
    lo = pltpu.bitcast(u << 16, _F32)
    hi = pltpu.bitcast(u & jnp.uint32(0xFFFF0000), _F32)
    return jnp.concatenate([lo, hi], axis=1)


def _postattn_body(yf_ref, ym_ref, gates_ref, x_ref, wfb_ref, wmb_ref, wout_ref, g_ref, wr_ref, br_ref,
                   x1_ref, h2_ref, route_ref, meta_ref, counts_ref, carry_ref, *, tm, d):
    step = pl.program_id(0)
    ya = jnp.dot(yf_ref[...], wfb_ref[...], preferred_element_type=_F32)
    yb = jnp.dot(ym_ref[...], wmb_ref[...], preferred_element_type=_F32)
    mixed = gates_ref[:, 0:d].astype(_F32) * ya + gates_ref[:, d:2 * d].astype(_F32) * yb
    x1 = x_ref[...] + jnp.dot(mixed.astype(_BF16), wout_ref[...], preferred_element_type=_F32)
    x1_ref[...] = x1
    h2 = _rms(x1, g_ref[...])
    h2_ref[...] = _pack_bf16_pair(h2[:, 0:d // 2], h2[:, d // 2:d])

    logits = jnp.dot(h2.astype(_BF16), wr_ref[...], preferred_element_type=_F32) + br_ref[...]
    lane = lax.broadcasted_iota(jnp.int32, (tm, LANES), 1)
    is_group = (lane >= N_EXPERTS) & (lane < N_EXPERTS + N_GROUPS)
    gl = jnp.where(is_group, logits, -jnp.inf)
    ge = jnp.exp(gl - jnp.max(gl, axis=-1, keepdims=True))
    gp = ge / jnp.sum(ge, axis=-1, keepdims=True)
    g_top = jnp.max(gp, axis=-1, keepdims=True)
    g_idx = jnp.min(jnp.where(is_group & (gp == g_top), lane, 2 * LANES), axis=-1,
                    keepdims=True) - N_EXPERTS
    in_group = (lane < N_EXPERTS) & ((lane // EXPERTS_PER_GROUP) == g_idx)
    fl = jnp.where(in_group, logits, -jnp.inf)
    f1 = jnp.max(fl, axis=-1, keepdims=True)
    e1 = jnp.min(jnp.where(fl == f1, lane, 2 * LANES), axis=-1, keepdims=True)
    fl2 = jnp.where(lane == e1, -jnp.inf, fl)
    f2 = jnp.max(fl2, axis=-1, keepdims=True)
    e2 = jnp.min(jnp.where(fl2 == f2, lane, 2 * LANES), axis=-1, keepdims=True)
    t2 = jnp.exp(f2 - f1)
    w1 = g_top * (1.0 / (1.0 + t2))
    w2 = g_top * (t2 / (1.0 + t2))

    @pl.when(step == 0)
    def _():
        carry_ref[...] = jnp.zeros_like(carry_ref)

    onehot = ((lane == e1) | (lane == e2)).astype(_F32)
    rr = lax.broadcasted_iota(jnp.int32, (tm, tm), 0)
    cc = lax.broadcasted_iota(jnp.int32, (tm, tm), 1)
    before = jnp.dot((cc < rr).astype(_BF16), onehot.astype(_BF16), preferred_element_type=_F32)
    before = before + carry_ref[0:1, :]
    r1 = jnp.sum(jnp.where(lane == e1, before, 0.0), axis=-1, keepdims=True)
    r2 = jnp.sum(jnp.where(lane == e2, before, 0.0), axis=-1, keepdims=True)
    total = carry_ref[0:1, :] + jnp.sum(onehot, axis=0, keepdims=True)
    carry_ref[...] = jnp.broadcast_to(total, carry_ref.shape)
    counts_ref[...] = jnp.broadcast_to(total, counts_ref.shape)

    slab = jnp.where(lane == 0, w1, 0.0)
    slab = jnp.where(lane == 1, w2, slab)
    slab = jnp.where(lane == 2, e1.astype(_F32), slab)
    slab = jnp.where(lane == 3, e2.astype(_F32), slab)
    slab = jnp.where(lane == 4, r1, slab)
    slab = jnp.where(lane == 5, r2, slab)
    route_ref[...] = slab
    meta_ref[...] = slab.T[0:8].astype(jnp.int32)


def _postattn(yf, ym, gates, x, wfb, wmb, wout, g, wr, br, *, tm):
    t, d = x.shape
    width = yf.shape[1]
    row = lambda cols: pl.BlockSpec((tm, cols), lambda i: (i, 0))
    return pl.pallas_call(
        functools.partial(_postattn_body, tm=tm, d=d),
        grid=(t // tm,),
        in_specs=[row(width), row(width), row(2 * d), row(d),
                  _const_spec(wfb.shape), _const_spec(wmb.shape), _const_spec(wout.shape),
                  _const_spec((1, d)), _const_spec(wr.shape), _const_spec(br.shape)],
        out_specs=[row(d), row(d // 2), row(LANES), pl.BlockSpec((8, tm), lambda i: (0, i)),
                   _const_spec((8, LANES))],
        out_shape=[jax.ShapeDtypeStruct((t, d), _F32),
                   jax.ShapeDtypeStruct((t, d // 2), jnp.uint32),
                   jax.ShapeDtypeStruct((t, LANES), _F32),
                   jax.ShapeDtypeStruct((8, t), jnp.int32),
                   jax.ShapeDtypeStruct((8, LANES), _F32)],
        scratch_shapes=[pltpu.VMEM((8, LANES), _F32)],
        compiler_params=_params("arbitrary"),
        name="postattn",
    )(yf, ym, gates, x, wfb, wmb, wout, g, wr, br)


def _sc_scatter(rows, idx, n_out):
    n, width = rows.shape
    workers = SC_CORES * SC_SUBCORES
    per_worker = n // workers
    chunks = per_worker // SC_SCATTER_CHUNK
    assert n % (workers * SC_SCATTER_CHUNK) == 0 and idx.shape == (2, n // SC_SCATTER_CHUNK, SC_SCATTER_CHUNK)
    mesh = plsc.VectorSubcoreMesh(core_axis_name="c", subcore_axis_name="s",
                                  num_cores=SC_CORES, num_subcores=SC_SUBCORES)

    @functools.partial(
        pl.kernel, out_type=jax.ShapeDtypeStruct((n_out, width), rows.dtype), mesh=mesh,
        scratch_types=[pltpu.VMEM((2, chunks, SC_SCATTER_CHUNK), jnp.int32),
                       pltpu.VMEM((SC_SCATTER_CHUNK, width), rows.dtype),
                       pltpu.SemaphoreType.DMA],
        name="sc_scatter")
    def scatter(rows_hbm, idx_hbm, out_hbm, idx_v, rows_v, sem):
        worker = lax.axis_index("s") * SC_CORES + lax.axis_index("c")
        for k in range(2):
            pltpu.sync_copy(idx_hbm.at[k, pl.ds(worker * chunks, chunks)], idx_v.at[k])

        def chunk(j, carry):
            start = worker * per_worker + j * SC_SCATTER_CHUNK
            pltpu.sync_copy(rows_hbm.at[pl.ds(start, SC_SCATTER_CHUNK)], rows_v)
            for k in range(2):
                pltpu.async_copy(rows_v, out_hbm.at[idx_v.at[k, j]], sem).wait()
            return carry

        lax.fori_loop(0, chunks, chunk, 0)

    return scatter(rows, idx)


def _experts_body(be_ref, valid_ref, nused_ref, xs_ref, wg_ref, wu_ref, wd_ref, out_ref,
                  wgb_ref, wub_ref, wdb_ref):
    j = pl.program_id(0)
    used = j < nused_ref[0]

    @pl.when(jnp.logical_not(used))
    def _():
        out_ref[...] = jnp.zeros_like(out_ref)

    @pl.when(used & ((j == 0) | (be_ref[j] != be_ref[jnp.maximum(j - 1, 0)])))
    def _():
        wgb_ref[...] = wg_ref[...].astype(_BF16)
        wub_ref[...] = wu_ref[...].astype(_BF16)
        wdb_ref[...] = wd_ref[...].astype(_BF16)

    @pl.when(used)
    def _():
        row = lax.broadcasted_iota(jnp.int32, xs_ref.shape, 0)
        packed = jnp.where(row < valid_ref[j], xs_ref[...], jnp.uint32(0))
        rows = _unpack_bf16_pair(packed).astype(_BF16)
        gate = jnp.dot(rows, wgb_ref[...], preferred_element_type=_F32)
        up = jnp.dot(rows, wub_ref[...], preferred_element_type=_F32)
        act = (gate * jax.nn.sigmoid(gate)) * up
        out = jnp.dot(act.astype(_BF16), wdb_ref[...], preferred_element_type=_F32)
        half = out.shape[1] // 2
        out_ref[...] = _pack_bf16_pair(out[:, 0:half], out[:, half:])


def _experts(block_expert, block_valid, n_used, xs, wg, wu, wd, *, te):
    n_rows, half = xs.shape
    _, d, de = wg.shape

    def blk(j, be, bv, nu):
        return jnp.minimum(j, nu[0] - 1)

    def weight(j, be, bv, nu):
        return (be[blk(j, be, bv, nu)], 0, 0)

    return pl.pallas_call(
        _experts_body,
        grid_spec=pltpu.PrefetchScalarGridSpec(
            num_scalar_prefetch=3,
            grid=(n_rows // te,),
            in_specs=[pl.BlockSpec((te, half), lambda j, be, bv, nu: (blk(j, be, bv, nu), 0)),
                      pl.BlockSpec((None, d, de), weight),
                      pl.BlockSpec((None, d, de), weight),
                      pl.BlockSpec((None, de, d), weight)],
            out_specs=pl.BlockSpec((te, d // 2), lambda j, be, bv, nu: (j, 0)),
            scratch_shapes=[pltpu.VMEM((d, de), _BF16), pltpu.VMEM((d, de), _BF16),
                            pltpu.VMEM((de, d), _BF16)],
        ),
        out_shape=jax.ShapeDtypeStruct((n_rows, d // 2), jnp.uint32),
        compiler_params=_params("arbitrary"),
        name="experts",
    )(block_expert, block_valid, n_used, xs, wg, wu, wd)


def _sc_gather(table, idx):
    n, = idx.shape
    width = table.shape[1]
    workers = SC_CORES * SC_SUBCORES
    per_worker = n // workers
    chunks = per_worker // SC_CHUNK
    assert n % (workers * SC_CHUNK) == 0
    mesh = plsc.VectorSubcoreMesh(core_axis_name="c", subcore_axis_name="s",
                                  num_cores=SC_CORES, num_subcores=SC_SUBCORES)

    @functools.partial(
        pl.kernel, out_type=jax.ShapeDtypeStruct((n, width), table.dtype), mesh=mesh,
        scratch_types=[pltpu.VMEM((per_worker,), jnp.int32),
                       pltpu.VMEM((SC_CHUNK, width), table.dtype),
                       pltpu.SemaphoreType.DMA],
        name="sc_gather")
    def gather(table_hbm, idx_hbm, out_hbm, idx_v, rows_v, sem):
        worker = lax.axis_index("s") * SC_CORES + lax.axis_index("c")
        base = worker * per_worker
        pltpu.sync_copy(idx_hbm.at[pl.ds(base, per_worker)], idx_v)

        def chunk(j, carry):
            off = pl.multiple_of(j * SC_CHUNK, SC_CHUNK)
            pltpu.async_copy(table_hbm.at[idx_v.at[pl.ds(off, SC_CHUNK)]], rows_v, sem).wait()
            pltpu.sync_copy(rows_v, out_hbm.at[pl.ds(base + off, SC_CHUNK)])
            return carry

        lax.fori_loop(0, chunks, chunk, 0)

    return gather(table, idx)


def _final_body(x1_ref, route_ref, p_ref, r1_ref, r2_ref, gp_ref, wpg_ref, wpp_ref, gf_ref, o_ref):
    emb = jnp.dot(p_ref[...].astype(_BF16), wpp_ref[...], preferred_element_type=_F32)
    route = route_ref[...]
    x2 = x1_ref[...] + (_unpack_bf16_pair(r1_ref[...]) * route[:, 0:1]
                        + _unpack_bf16_pair(r2_ref[...]) * route[:, 1:2])
    gate = jax.nn.sigmoid(jnp.dot(_rms(x2, gp_ref[...]).astype(_BF16), wpg_ref[...],
                                  preferred_element_type=_F32))
    o_ref[...] = _rms(x2 + gate * emb, gf_ref[...])


def _final(x1, route, p, slot_rows, gp, wpg, wpp, gf, *, tm):
    t, d = x1.shape
    ple = p.shape[1]
    steps = t // tm
    row = lambda cols: pl.BlockSpec((tm, cols), lambda i: (i, 0))
    return pl.pallas_call(
        _final_body,
        grid=(steps,),
        in_specs=[row(d), row(LANES), row(ple), row(d // 2),
                  pl.BlockSpec((tm, d // 2), lambda i: (steps + i, 0)),
                  _const_spec((1, d)), _const_spec(wpg.shape), _const_spec(wpp.shape),
                  _const_spec((1, d))],
        out_specs=row(d),
        out_shape=jax.ShapeDtypeStruct((t, d), _F32),
        compiler_params=_params("arbitrary"),
        name="final",
    )(x1, route, p, slot_rows, slot_rows, gp, wpg, wpp, gf)


def _rope_tables(s):
    half = HEAD_DIM // 2
    inv = 1.0 / (ROPE_THETA ** (jnp.arange(0, HEAD_DIM, 2, dtype=_F32) / HEAD_DIM))
    ang = jnp.arange(s, dtype=_F32)[:, None] * inv[None, :]
    cos, sin = jnp.cos(ang), jnp.sin(ang)
    reps = LANES // half
    cos_t = jnp.tile(cos, (1, reps))
    sin_t = jnp.tile(jnp.concatenate([-sin, sin], axis=1), (1, reps // 2))
    return cos_t, sin_t


def _layer(x, p, attn_norm, w_in, b_forget, w_fox_branch, w_moba_branch, w_out, moe_norm,
           w_group, b_group, w_fine, b_fine, w_gate, w_up, w_down, ple_norm, w_ple_gate, w_ple_proj,
           final_gain):
    b, s, d = x.shape
    t = b * s
    width = N_FOX_HEADS * HEAD_DIM
    assert N_MOBA_HEADS * HEAD_DIM == width and MOBA_BLOCK % ATTN_Q == 0 and ATTN_K % MOBA_BLOCK == 0
    assert BIAS_PIECES * N_FOX_HEADS <= LANES and s % ATTN_K == 0
    assert s % ROW_TILE == 0 and ROW_TILE % MOBA_BLOCK == 0 and s // MOBA_BLOCK <= LANES
    assert t % FINAL_TILE == 0 and N_EXPERTS + N_GROUPS <= LANES

    qkv_cols = 6 * width
    f_cols = N_FOX_HEADS
    chunk = lambda c: w_in[:, c * width:(c + 1) * width]
    w_rows = jnp.concatenate([chunk(1), chunk(4), w_in[:, qkv_cols + f_cols:]], axis=1).astype(_BF16)
    w_cols = jnp.concatenate([chunk(0), chunk(2), chunk(3), chunk(5)], axis=1).T.astype(_BF16)
    wf = jnp.zeros((d, LANES), _BF16).at[:, :f_cols].set(w_in[:, qkv_cols:qkv_cols + f_cols].astype(_BF16))
    bfp = jnp.zeros((1, LANES), _F32).at[0, :f_cols].set(b_forget.astype(_F32))
    cos, sin = _rope_tables(s)

    qft, kf, vft, qmt, km, vmt, gates, cb, kmean, kabs, bmax = _inproj(
        x, attn_norm.reshape(1, d), w_rows, w_cols, wf, bfp, cos, sin, cos.T, sin.T, tm=ROW_TILE)
    y_fox = _attention(qft, kf, cb, vft, kabs, bmax, mode="fox")
    block_of_key = jnp.arange(s, dtype=jnp.int32)[:, None] // MOBA_BLOCK
    block_onehot = (block_of_key == jnp.arange(LANES, dtype=jnp.int32)[None, :]).astype(_BF16)
    y_moba = _attention(qmt, km, block_onehot, vmt, kmean, bmax, mode="moba")

    wr = jnp.zeros((d, LANES), _BF16)
    wr = wr.at[:, :N_EXPERTS].set(w_fine.astype(_BF16))
    wr = wr.at[:, N_EXPERTS:N_EXPERTS + N_GROUPS].set(w_group.astype(_BF16))
    br = jnp.zeros((1, LANES), _F32)
    br = br.at[0, :N_EXPERTS].set(b_fine.astype(_F32))
    br = br.at[0, N_EXPERTS:N_EXPERTS + N_GROUPS].set(b_group.astype(_F32))
    x1, h2p, route, meta, counts = _postattn(
        y_fox.reshape(t, width), y_moba.reshape(t, width), gates.reshape(t, 2 * d), x.reshape(t, d),
        w_fox_branch.astype(_BF16), w_moba_branch.astype(_BF16), w_out.astype(_BF16),
        moe_norm.reshape(1, d), wr, br, tm=ROW_TILE)

    te = EXPERT_TILE
    n_blk = (2 * t) // te + N_EXPERTS
    counts = counts[0, :N_EXPERTS].astype(jnp.int32)
    blocks_per = (counts + te - 1) // te
    block_end = jnp.cumsum(blocks_per)
    row_start = (block_end - blocks_per) * te
    expert_ids = jnp.arange(N_EXPERTS, dtype=jnp.int32)[:, None, None]
    slot_major = meta[4:6] + jnp.sum(
        jnp.where(meta[2:4][None] == expert_ids, row_start[:, None, None], 0), axis=0)
    n_used = block_end[-1:].astype(jnp.int32)
    block_ids = jnp.arange(n_blk, dtype=jnp.int32)
    block_expert = jnp.minimum(
        jnp.sum((block_ids[:, None] >= block_end[None, :]).astype(jnp.int32), axis=1), N_EXPERTS - 1)

    first_block = block_end - blocks_per
    block_valid = jnp.clip(counts[block_expert] - (block_ids - first_block[block_expert]) * te, 0, te)
    xs = _sc_scatter(h2p, slot_major.reshape(2, t // SC_SCATTER_CHUNK, SC_SCATTER_CHUNK), n_blk * te)
    rows = _experts(block_expert, block_valid, n_used, xs, w_gate, w_up, w_down, te=te)
    slot_rows = _sc_gather(rows, slot_major.reshape(-1))
    out = _final(x1, route, p.reshape(t, -1), slot_rows, ple_norm.reshape(1, d),
                 w_ple_gate.astype(_BF16), w_ple_proj.astype(_BF16), final_gain.reshape(1, d),
                 tm=FINAL_TILE)
    return out.reshape(b, s, d)


def kernel(x, p, attn_norm, w_in, b_forget, w_fox_branch, w_moba_branch, w_out, moe_norm, w_group,
           b_group, w_fine, b_fine, w_gate, w_up, w_down, ple_norm, w_ple_gate, w_ple_proj, final_norm):
    depth = p.shape[0]
    assert depth == 1, "the final norm is fused into the last layer's kernel"
    i = 0
    return _layer(x, p[i], attn_norm[i], w_in[i], b_forget[i], w_fox_branch[i], w_moba_branch[i],
                  w_out[i], moe_norm[i], w_group[i], b_group[i], w_fine[i], b_fine[i], w_gate[i],
                  w_up[i], w_down[i], ple_norm[i], w_ple_gate[i], w_ple_proj[i], final_norm)
```

```python
import functools

import jax
import jax.numpy as jnp
from jax import lax
from jax.experimental import pallas as pl
from jax.experimental.pallas import tpu as pltpu
from jax.experimental.pallas import tpu_sc as plsc

HEAD_DIM = 64
N_FOX_HEADS = 8
N_MOBA_HEADS = 8
MOBA_BLOCK = 256
MOBA_TOPK = 3
ROPE_THETA = 10000.0
N_GROUPS = 4
EXPERTS_PER_GROUP = 8
N_EXPERTS = N_GROUPS * EXPERTS_PER_GROUP
RMS_EPS = 1e-6

LANES = 128
HEADS_PER_STEP = LANES // HEAD_DIM
ATTN_Q = 256
ATTN_K = 256
PIPE_UNITS = 8
SOFTMAX_LAG = 4
ONES_ROWS = 16
SOFTMAX_ROWS = 64
BIAS_PIECES = 3
ROW_TILE = 512
SC_CORES = 2
SC_SUBCORES = 16
SC_CHUNK = 64
SC_SCATTER_CHUNK = 128
FINAL_TILE = 256
EXPERT_PARTS = 2
EXPERT_TILE = 512
LOG2_E = 1.4426950408889634
UNDERFLOW_LOG2 = 160.0
NEG_BIG = -1e30
VMEM_LIMIT = 48 * 1024 * 1024

_BF16 = jnp.bfloat16
_F32 = jnp.float32


def _params(*sem):
    return pltpu.CompilerParams(dimension_semantics=sem, vmem_limit_bytes=VMEM_LIMIT)


def _rms(x, g):
    return x * lax.rsqrt(jnp.mean(x * x, axis=-1, keepdims=True) + RMS_EPS) * g


def _const_spec(shape):
    return pl.BlockSpec(shape, lambda *_: (0,) * len(shape))


def _inproj_body(x_ref, g_ref, w_ref, wt_ref, wf_ref, bf_ref, cos_ref, sin_ref, cost_ref, sint_ref,
                 qft_ref, kf_ref, vft_ref, qmt_ref, km_ref, vmt_ref, gates_ref, cb_ref, kmean_ref,
                 kabs_ref, bmax_ref,
                 carry_ref, *, tm, width, gate_cols):
    j = pl.program_id(1)
    h = _rms(x_ref[...], g_ref[...])
    hb = h.astype(_BF16)
    hbt = h.T.astype(_BF16)
    reps = width // LANES
    half = HEAD_DIM // 2

    def proj(c):
        return jnp.dot(hb, w_ref[:, c * width:(c + 1) * width], preferred_element_type=_F32)

    def proj_t(c):
        return jnp.dot(wt_ref[c * width:(c + 1) * width, :], hbt, preferred_element_type=_F32)

    def rope(t):
        cos = jnp.concatenate([cos_ref[...]] * reps, axis=1)
        sin = jnp.concatenate([sin_ref[...]] * reps, axis=1)
        first = (lax.broadcasted_iota(jnp.int32, (1, width), 1) % HEAD_DIM) < half
        partner = jnp.where(first, pltpu.roll(t, width - half, 1), pltpu.roll(t, half, 1))
        return t * cos + partner * sin

    def rope_t(t):
        cos = jnp.concatenate([cost_ref[...]] * reps, axis=0)
        sin = jnp.concatenate([sint_ref[...]] * reps, axis=0)
        first = (lax.broadcasted_iota(jnp.int32, (width, 1), 0) % HEAD_DIM) < half
        partner = jnp.where(first, pltpu.roll(t, width - half, 0), pltpu.roll(t, half, 0))
        return t * cos + partner * sin

    scale = HEAD_DIM ** -0.5 * LOG2_E
    qft_ref[...] = (proj_t(0) * scale).astype(_BF16)
    vft_ref[...] = proj_t(1).astype(_BF16)
    qmt_ref[...] = (rope_t(proj_t(2)) * scale).astype(_BF16)
    vmt_ref[...] = proj_t(3).astype(_BF16)
    kf = proj(0).astype(_BF16)
    kf_ref[...] = kf
    kf_abs = jnp.abs(kf.astype(_F32))
    km = rope(proj(1))
    km_ref[...] = km.astype(_BF16)

    @pl.when(j == 0)
    def _():
        kmean_ref[...] = jnp.zeros_like(kmean_ref)
        kabs_ref[...] = jnp.zeros_like(kabs_ref)
        bmax_ref[...] = jnp.zeros_like(bmax_ref)
        carry_ref[...] = jnp.zeros_like(carry_ref)

    blocks = tm // MOBA_BLOCK
    for b in range(blocks):
        rows = slice(b * MOBA_BLOCK, (b + 1) * MOBA_BLOCK)
        kmean_ref[pl.ds(j * blocks + b, 1), :] = jnp.mean(km[rows], axis=0, keepdims=True)
        kabs_ref[pl.ds(j * blocks + b, 1), :] = jnp.max(kf_abs[rows], axis=0, keepdims=True)

    for c in range(gate_cols // width):
        g = jnp.dot(hb, w_ref[:, (2 + c) * width:(3 + c) * width],
                    preferred_element_type=_F32)
        gates_ref[:, c * width:(c + 1) * width] = jax.nn.sigmoid(g).astype(_BF16)

    z = jnp.dot(hb, wf_ref[...], preferred_element_type=_F32) + bf_ref[...]
    log_f = jnp.minimum(z, 0.0) - jnp.log1p(jnp.exp(-jnp.abs(z)))
    r = lax.broadcasted_iota(jnp.int32, (LANES, LANES), 0)
    c = lax.broadcasted_iota(jnp.int32, (LANES, LANES), 1)
    tri = (c <= r).astype(_BF16)
    offset = carry_ref[0:1, :]
    blocks_cum = []
    for blk in range(tm // LANES):
        rest = log_f[blk * LANES:(blk + 1) * LANES]
        within = jnp.zeros((LANES, LANES), _F32)
        for _ in range(BIAS_PIECES):
            part = rest.astype(_BF16)
            rest = rest - part.astype(_F32)
            within = within + jnp.dot(tri, part, preferred_element_type=_F32)
        blocks_cum.append(within + offset)
        offset = offset + within[LANES - 1:LANES, :]
    cum = jnp.concatenate(blocks_cum, axis=0)
    carry_ref[...] = jnp.broadcast_to(offset, carry_ref.shape)

    head_lane = lax.broadcasted_iota(jnp.int32, (1, LANES), 1) < N_FOX_HEADS
    bias = jnp.where(head_lane, -cum * LOG2_E, 0.0)
    for b in range(blocks):
        bmax_ref[pl.ds(j * blocks + b, 1), :] = jnp.max(
            bias[b * MOBA_BLOCK:(b + 1) * MOBA_BLOCK], axis=0, keepdims=True)
    rest = bias
    placed = jnp.zeros((tm, LANES), _F32)
    for piece in range(BIAS_PIECES):
        part = rest.astype(_BF16).astype(_F32)
        rest = rest - part
        placed = placed + (pltpu.roll(part, N_FOX_HEADS * piece, 1) if piece else part)
    cb_ref[...] = placed.astype(_BF16)


def _inproj(x, g, w, wt, wf, bfp, cos, sin, cos_t, sin_t, *, tm):
    b, s, d = x.shape
    width = N_FOX_HEADS * HEAD_DIM
    gate_cols = w.shape[1] - 2 * width
    act = jax.ShapeDtypeStruct((b, s, width), _BF16)
    act_t = jax.ShapeDtypeStruct((b, width, s), _BF16)
    act_spec = pl.BlockSpec((None, tm, width), lambda bi, j: (bi, j, 0))
    act_t_spec = pl.BlockSpec((None, width, tm), lambda bi, j: (bi, 0, j))
    return pl.pallas_call(
        functools.partial(_inproj_body, tm=tm, width=width, gate_cols=gate_cols),
        grid=(b, s // tm),
        in_specs=[
            pl.BlockSpec((None, tm, d), lambda bi, j: (bi, j, 0)),
            _const_spec((1, d)),
            _const_spec(w.shape),
            _const_spec(wt.shape),
            _const_spec(wf.shape),
            _const_spec(bfp.shape),
            pl.BlockSpec((tm, LANES), lambda bi, j: (j, 0)),
            pl.BlockSpec((tm, LANES), lambda bi, j: (j, 0)),
            pl.BlockSpec((LANES, tm), lambda bi, j: (0, j)),
            pl.BlockSpec((LANES, tm), lambda bi, j: (0, j)),
        ],
        out_specs=[act_t_spec, act_spec, act_t_spec, act_t_spec, act_spec, act_t_spec,
                   pl.BlockSpec((None, tm, gate_cols), lambda bi, j: (bi, j, 0)),
                   pl.BlockSpec((None, tm, LANES), lambda bi, j: (bi, j, 0)),
                   pl.BlockSpec((None, LANES, width), lambda bi, j: (bi, 0, 0)),
                   pl.BlockSpec((None, LANES, width), lambda bi, j: (bi, 0, 0)),
                   pl.BlockSpec((None, LANES, LANES), lambda bi, j: (bi, 0, 0))],
        out_shape=[act_t, act, act_t, act_t, act, act_t,
                   jax.ShapeDtypeStruct((b, s, gate_cols), _BF16),
                   jax.ShapeDtypeStruct((b, s, LANES), _BF16),
                   jax.ShapeDtypeStruct((b, LANES, width), _F32),
                   jax.ShapeDtypeStruct((b, LANES, width), _F32),
                   jax.ShapeDtypeStruct((b, LANES, LANES), _F32)],
        scratch_shapes=[pltpu.VMEM((8, LANES), _F32)],
        compiler_params=_params("arbitrary", "arbitrary"),
        name="inproj",
    )(x, g, w, wt, wf, bfp, cos, sin, cos_t, sin_t)


def _attn_body(uq_ref, uk_ref, qt_ref, k_ref, kx_ref, vt_ref, aux_ref, aux2_ref, o_ref,
               w_ref, m_ref, l_ref, acc_ref, st_ref, p_ref, alpha_ref, pmax_ref, qn_ref, *,
               mode, tq, tk, nb, n_diag, n_units):
    hp = pl.program_id(1)
    wide = HEADS_PER_STEP * tq
    nq = w_ref.shape[0]
    feat = lax.broadcasted_iota(jnp.int32, (LANES, 1), 0)
    xr = lax.broadcasted_iota(jnp.int32, (LANES, wide), 0)
    xc = lax.broadcasted_iota(jnp.int32, (LANES, wide), 1)

    def setup(i, carry):
        qt = qt_ref[:, pl.ds(pl.multiple_of(i * tq, tq), tq)]
        zero = jnp.zeros_like(qt)
        wq = jnp.concatenate([jnp.where(feat < HEAD_DIM, qt, zero),
                              jnp.where(feat < HEAD_DIM, zero, qt)], axis=1)
        if mode == "fox":
            head = hp * HEADS_PER_STEP + xc // tq
            lower = ((xr % N_FOX_HEADS == head) & (xr < BIAS_PIECES * N_FOX_HEADS)).astype(_BF16)
            qn_ref[i] = jnp.broadcast_to(
                jnp.sum(jnp.abs(wq.astype(_F32)), axis=0, keepdims=True), qn_ref.shape[1:])
        else:
            own = (i * tq) // MOBA_BLOCK
            gate = jnp.dot(aux_ref[...].astype(_BF16), wq, preferred_element_type=_F32)[0:nb]
            blk = lax.broadcasted_iota(jnp.int32, (nb, wide), 0)
            gate = jnp.where(blk < own, gate, -jnp.inf)
            keep = blk == own
            for _ in range(MOBA_TOPK):
                mx = jnp.max(gate, axis=0, keepdims=True)
                cand = jnp.where((gate == mx) & (mx > -jnp.inf), blk, nb)
                pick = blk == jnp.min(cand, axis=0, keepdims=True)
                keep = keep | pick
                gate = jnp.where(pick, -jnp.inf, gate)
            lower = jnp.where(keep, 0.0, NEG_BIG)
            if nb < LANES:
                lower = jnp.concatenate([lower, jnp.zeros((LANES - nb, wide), _F32)], axis=0)
            lower = lower.astype(_BF16)
        w_ref[i] = jnp.concatenate([wq, lower], axis=0)
        m_ref[i] = jnp.full(m_ref.shape[1:], NEG_BIG, _F32)
        l_ref[i] = jnp.zeros(l_ref.shape[1:], _F32)
        acc_ref[i] = jnp.zeros(acc_ref.shape[1:], _F32)
        return carry

    lax.fori_loop(0, nq, setup, 0)

    def unit(t):
        return uq_ref[t], uk_ref[t]

    def scores(t, slot, causal):
        q, kb = unit(t)
        start = pl.multiple_of(kb * tk, tk)
        ka = jnp.concatenate([k_ref[pl.ds(start, tk), :], kx_ref[pl.ds(start, tk), :]], axis=1)
        st = jnp.dot(ka, w_ref[q], preferred_element_type=_F32)
        if causal:
            key = start + lax.broadcasted_iota(jnp.int32, (tk, wide), 0)
            qry = q * tq + lax.broadcasted_iota(jnp.int32, (tk, wide), 1) % tq
            st = jnp.where(key <= qry, st, NEG_BIG)
        st_ref[slot] = st
        pmax_ref[slot] = jnp.max(st.reshape(tk // 8, 8, wide), axis=0)

    def softmax(t, slot):
        q, _ = unit(t)
        rows = SOFTMAX_ROWS
        m_prev = m_ref[q][0:1, :]
        m_new = jnp.maximum(m_prev, jnp.max(pmax_ref[slot], axis=0, keepdims=True))
        for r in range(0, tk, rows):
            p_ref[slot, r:r + rows, :] = jnp.exp2(st_ref[slot, r:r + rows, :] - m_new).astype(_BF16)
        m_ref[q] = jnp.broadcast_to(m_new, m_ref.shape[1:])
        alpha_ref[slot] = jnp.broadcast_to(jnp.exp2(m_prev - m_new), alpha_ref.shape[1:])

    def values(t, slot):
        q, kb = unit(t)
        start = pl.multiple_of(kb * tk, tk)
        ones = jnp.ones((ONES_ROWS, tk), _BF16)
        for h in range(HEADS_PER_STEP):
            cols = slice(h * tq, (h + 1) * tq)
            feats = slice(h * HEAD_DIM, (h + 1) * HEAD_DIM)
            lhs = jnp.concatenate([vt_ref[feats, pl.ds(start, tk)], ones], axis=0)
            pv = jnp.dot(lhs, p_ref[slot, :, cols], preferred_element_type=_F32)
            alpha = alpha_ref[slot, 0:1, cols]
            acc_ref[q, feats, :] = acc_ref[q, feats, :] * alpha + pv[0:HEAD_DIM]
            l_ref[q, :, cols] = jnp.broadcast_to(
                alpha * l_ref[q, 0:1, cols] + pv[HEAD_DIM:HEAD_DIM + 1], (8, tq))

    def step(base, c, causal, do_scores=True, do_softmax=True, do_values=True):
        u = base + c
        if do_scores:
            scores(u, c % PIPE_UNITS, causal)
        if do_values:
            values(u - 2 * SOFTMAX_LAG, (c - 2 * SOFTMAX_LAG) % PIPE_UNITS)
        if do_softmax:
            softmax(u - SOFTMAX_LAG, (c - SOFTMAX_LAG) % PIPE_UNITS)

    def group(causal):
        def body(j, carry):
            for c in range(PIPE_UNITS):
                step(j * PIPE_UNITS, c, causal)
            return carry
        return body

    def pipeline(g_first, g_end, causal_groups):
        for c in range(PIPE_UNITS):
            step(g_first * PIPE_UNITS, c, g_first < causal_groups,
                 do_softmax=c >= SOFTMAX_LAG, do_values=c >= 2 * SOFTMAX_LAG)
        if g_first + 1 < causal_groups:
            lax.fori_loop(g_first + 1, causal_groups, group(True), 0)
        lax.fori_loop(max(g_first + 1, causal_groups), g_end, group(False), 0)
        for c in range(2 * SOFTMAX_LAG):
            step(g_end * PIPE_UNITS, c, False, do_scores=False, do_softmax=c < SOFTMAX_LAG)

    diag_groups = n_diag // PIPE_UNITS
    if mode == "moba":
        pipeline(0, n_units // PIPE_UNITS, diag_groups)
    else:
        pipeline(0, diag_groups, diag_groups)
        nq_blocks = n_diag
        col = lax.broadcasted_iota(jnp.int32, (LANES, LANES), 1)
        first_head = lax.broadcasted_iota(jnp.int32, (1, wide), 1) < tq
        head0 = hp * HEADS_PER_STEP

        def per_head(table, pick0, pick1, fill):
            v0 = jnp.max(jnp.where(pick0, table, fill), axis=1, keepdims=True)[0:nb]
            v1 = jnp.max(jnp.where(pick1, table, fill), axis=1, keepdims=True)[0:nb]
            return jnp.where(first_head, v0, v1)

        bias_max = per_head(aux2_ref[...], col == head0, col == head0 + 1, NEG_BIG)
        k_absmax = per_head(aux_ref[...], col < HEAD_DIM, col >= HEAD_DIM, 0.0)
        blk = lax.broadcasted_iota(jnp.int32, (nb, 1), 0)

        needed = jnp.zeros((nb, 1), jnp.int32)
        for i in range(1, nq_blocks):
            bound = bias_max + qn_ref[i, 0:1, :] * k_absmax - m_ref[i, 0:1, :]
            live = jnp.max(bound, axis=1, keepdims=True) >= -UNDERFLOW_LOG2
            needed = jnp.maximum(needed, jnp.where(live & (blk < i), i - blk, 0))
        far = jnp.max(needed)
        n_live = far * nq_blocks - (far * (far + 1)) // 2
        live_groups = (n_live + PIPE_UNITS - 1) // PIPE_UNITS

        @pl.when(live_groups > 0)
        def _():
            pipeline(diag_groups, diag_groups + live_groups, diag_groups)

    def finish(i, carry):
        out_t = jnp.concatenate(
            [acc_ref[i, h * HEAD_DIM:(h + 1) * HEAD_DIM, :] / l_ref[i, 0:1, h * tq:(h + 1) * tq]
             for h in range(HEADS_PER_STEP)], axis=0)
        o_ref[pl.ds(pl.multiple_of(i * tq, tq), tq), :] = out_t.T.astype(o_ref.dtype)
        return carry

    lax.fori_loop(0, nq, finish, 0, unroll=2)


def _attention(qt, k, kx, vt, aux, aux2, *, mode):
    b, width, s = qt.shape
    tq, tk = ATTN_Q, ATTN_K
    nq = s // tq
    nb = s // MOBA_BLOCK
    wide = HEADS_PER_STEP * tq
    assert tq == tk
    units = [(i, (i * tq) // tk) for i in range(nq)]
    units += [(i, i - dist) for dist in range(1, nq) for i in range(dist, nq)]
    assert nq % PIPE_UNITS == 0 and len(units) % PIPE_UNITS == 0 and PIPE_UNITS >= 2 * SOFTMAX_LAG
    unit_q = jnp.asarray([u[0] for u in units], jnp.int32)
    unit_k = jnp.asarray([u[1] for u in units], jnp.int32)
    aux_spec = pl.BlockSpec((None, LANES, LANES), lambda bi, hp, uq, uk: (bi, 0, hp))
    aux2_spec = pl.BlockSpec((None, LANES, LANES), lambda bi, hp, uq, uk: (bi, 0, 0))
    if mode == "fox":
        kx_spec = pl.BlockSpec((None, s, LANES), lambda bi, hp, uq, uk: (bi, 0, 0))
    else:
        kx_spec = pl.BlockSpec(kx.shape, lambda bi, hp, uq, uk: (0, 0))
    return pl.pallas_call(
        functools.partial(_attn_body, mode=mode, tq=tq, tk=tk, nb=nb, n_diag=nq, n_units=len(units)),
        grid_spec=pltpu.PrefetchScalarGridSpec(
            num_scalar_prefetch=2,
            grid=(b, width // LANES),
            in_specs=[
                pl.BlockSpec((None, LANES, s), lambda bi, hp, uq, uk: (bi, hp, 0)),
                pl.BlockSpec((None, s, LANES), lambda bi, hp, uq, uk: (bi, 0, hp)),
                kx_spec,
                pl.BlockSpec((None, LANES, s), lambda bi, hp, uq, uk: (bi, hp, 0)),
                aux_spec,
                aux2_spec,
            ],
            out_specs=pl.BlockSpec((None, s, LANES), lambda bi, hp, uq, uk: (bi, 0, hp)),
            scratch_shapes=[
                pltpu.VMEM((nq, 2 * LANES, wide), _BF16),
                pltpu.VMEM((nq, 8, wide), _F32),
                pltpu.VMEM((nq, 8, wide), _F32),
                pltpu.VMEM((nq, LANES, tq), _F32),
                pltpu.VMEM((PIPE_UNITS, tk, wide), _F32),
                pltpu.VMEM((PIPE_UNITS, tk, wide), _BF16),
                pltpu.VMEM((PIPE_UNITS, 8, wide), _F32),
                pltpu.VMEM((PIPE_UNITS, 8, wide), _F32),
                pltpu.VMEM((nq, 8, wide), _F32),
            ],
        ),
        out_shape=jax.ShapeDtypeStruct((b, s, width), _BF16),
        compiler_params=_params("arbitrary", "arbitrary"),
        name=mode,
    )(unit_q, unit_k, qt, k, kx, vt, aux, aux2)


def _pack_bf16_pair(a, b):
    lo = pltpu.bitcast(a.astype(_BF16).astype(_F32), jnp.uint32) >> 16
    hi = pltpu.bitcast(b.astype(_BF16).astype(_F32), jnp.uint32) & jnp.uint32(0xFFFF0000)
    return lo | hi


def _unpack_bf16_pair(u):
    lo = pltpu.bitcast(u << 16, _F32)
    hi = pltpu.bitcast(u & jnp.uint32(0xFFFF0000), _F32)
    return jnp.concatenate([lo, hi], axis=1)


def _postattn_body(yf_ref, ym_ref, gates_ref, x_ref, wfb_ref, wmb_ref, wout_ref, g_ref, wr_ref, br_ref,
                   x1_ref, h2_ref, route_ref, meta_ref, counts_ref, carry_ref, *, tm, d):
    step = pl.program_id(0)
    ya = jnp.dot(yf_ref[...], wfb_ref[...], preferred_element_type=_F32)
    yb = jnp.dot(ym_ref[...], wmb_ref[...], preferred_element_type=_F32)
    mixed = gates_ref[:, 0:d].astype(_F32) * ya + gates_ref[:, d:2 * d].astype(_F32) * yb
    x1 = x_ref[...] + jnp.dot(mixed.astype(_BF16), wout_ref[...], preferred_element_type=_F32)
    x1_ref[...] = x1
    h2 = _rms(x1, g_ref[...])
    h2_ref[...] = _pack_bf16_pair(h2[:, 0:d // 2], h2[:, d // 2:d])

    logits = jnp.dot(h2.astype(_BF16), wr_ref[...], preferred_element_type=_F32) + br_ref[...]
    lane = lax.broadcasted_iota(jnp.int32, (tm, LANES), 1)
    is_group = (lane >= N_EXPERTS) & (lane < N_EXPERTS + N_GROUPS)
    gl = jnp.where(is_group, logits, -jnp.inf)
    ge = jnp.exp(gl - jnp.max(gl, axis=-1, keepdims=True))
    gp = ge / jnp.sum(ge, axis=-1, keepdims=True)
    g_top = jnp.max(gp, axis=-1, keepdims=True)
    g_idx = jnp.min(jnp.where(is_group & (gp == g_top), lane, 2 * LANES), axis=-1,
                    keepdims=True) - N_EXPERTS
    in_group = (lane < N_EXPERTS) & ((lane // EXPERTS_PER_GROUP) == g_idx)
    fl = jnp.where(in_group, logits, -jnp.inf)
    f1 = jnp.max(fl, axis=-1, keepdims=True)
    e1 = jnp.min(jnp.where(fl == f1, lane, 2 * LANES), axis=-1, keepdims=True)
    fl2 = jnp.where(lane == e1, -jnp.inf, fl)
    f2 = jnp.max(fl2, axis=-1, keepdims=True)
    e2 = jnp.min(jnp.where(fl2 == f2, lane, 2 * LANES), axis=-1, keepdims=True)
    t2 = jnp.exp(f2 - f1)
    w1 = g_top * (1.0 / (1.0 + t2))
    w2 = g_top * (t2 / (1.0 + t2))

    @pl.when(step == 0)
    def _():
        carry_ref[...] = jnp.zeros_like(carry_ref)

    onehot = ((lane == e1) | (lane == e2)).astype(_F32)
    rr = lax.broadcasted_iota(jnp.int32, (tm, tm), 0)
    cc = lax.broadcasted_iota(jnp.int32, (tm, tm), 1)
    before = jnp.dot((cc < rr).astype(_BF16), onehot.astype(_BF16), preferred_element_type=_F32)
    before = before + carry_ref[0:1, :]
    r1 = jnp.sum(jnp.where(lane == e1, before, 0.0), axis=-1, keepdims=True)
    r2 = jnp.sum(jnp.where(lane == e2, before, 0.0), axis=-1, keepdims=True)
    total = carry_ref[0:1, :] + jnp.sum(onehot, axis=0, keepdims=True)
    carry_ref[...] = jnp.broadcast_to(total, carry_ref.shape)
    counts_ref[...] = jnp.broadcast_to(total, counts_ref.shape)

    slab = jnp.where(lane == 0, w1, 0.0)
    slab = jnp.where(lane == 1, w2, slab)
    slab = jnp.where(lane == 2, e1.astype(_F32), slab)
    slab = jnp.where(lane == 3, e2.astype(_F32), slab)
    slab = jnp.where(lane == 4, r1, slab)
    slab = jnp.where(lane == 5, r2, slab)
    route_ref[...] = slab
    meta_ref[...] = slab.T[0:8].astype(jnp.int32)


def _postattn(yf, ym, gates, x, wfb, wmb, wout, g, wr, br, *, tm):
    t, d = x.shape
    width = yf.shape[1]
    row = lambda cols: pl.BlockSpec((tm, cols), lambda i: (i, 0))
    return pl.pallas_call(
        functools.partial(_postattn_body, tm=tm, d=d),
        grid=(t // tm,),
        in_specs=[row(width), row(width), row(2 * d), row(d),
                  _const_spec(wfb.shape), _const_spec(wmb.shape), _const_spec(wout.shape),
                  _const_spec((1, d)), _const_spec(wr.shape), _const_spec(br.shape)],
        out_specs=[row(d), row(d // 2), row(LANES), pl.BlockSpec((8, tm), lambda i: (0, i)),
                   _const_spec((8, LANES))],
        out_shape=[jax.ShapeDtypeStruct((t, d), _F32),
                   jax.ShapeDtypeStruct((t, d // 2), jnp.uint32),
                   jax.ShapeDtypeStruct((t, LANES), _F32),
                   jax.ShapeDtypeStruct((8, t), jnp.int32),
                   jax.ShapeDtypeStruct((8, LANES), _F32)],
        scratch_shapes=[pltpu.VMEM((8, LANES), _F32)],
        compiler_params=_params("arbitrary"),
        name="postattn",
    )(yf, ym, gates, x, wfb, wmb, wout, g, wr, br)


def _sc_scatter(rows, idx, n_out):
    n, width = rows.shape
    workers = SC_CORES * SC_SUBCORES
    per_worker = n // workers
    chunks = per_worker // SC_SCATTER_CHUNK
    assert n % (workers * SC_SCATTER_CHUNK) == 0 and idx.shape == (2, n // SC_SCATTER_CHUNK, SC_SCATTER_CHUNK)
    mesh = plsc.VectorSubcoreMesh(core_axis_name="c", subcore_axis_name="s",
                                  num_cores=SC_CORES, num_subcores=SC_SUBCORES)

    @functools.partial(
        pl.kernel, out_type=jax.ShapeDtypeStruct((n_out, width), rows.dtype), mesh=mesh,
        scratch_types=[pltpu.VMEM((2, chunks, SC_SCATTER_CHUNK), jnp.int32),
                       pltpu.VMEM((SC_SCATTER_CHUNK, width), rows.dtype),
                       pltpu.SemaphoreType.DMA],
        name="sc_scatter")
    def scatter(rows_hbm, idx_hbm, out_hbm, idx_v, rows_v, sem):
        worker = lax.axis_index("s") * SC_CORES + lax.axis_index("c")
        for k in range(2):
            pltpu.sync_copy(idx_hbm.at[k, pl.ds(worker * chunks, chunks)], idx_v.at[k])

        def chunk(j, carry):
            start = worker * per_worker + j * SC_SCATTER_CHUNK
            pltpu.sync_copy(rows_hbm.at[pl.ds(start, SC_SCATTER_CHUNK)], rows_v)
            for k in range(2):
                pltpu.async_copy(rows_v, out_hbm.at[idx_v.at[k, j]], sem).wait()
            return carry

        lax.fori_loop(0, chunks, chunk, 0)

    return scatter(rows, idx)


def _experts_body(be_ref, valid_ref, nused_ref, xs_ref, wg_ref, wu_ref, wd_ref, out_ref,
                  wgb_ref, wub_ref, wdb_ref):
    j = pl.program_id(0)
    used = j < nused_ref[0]

    @pl.when(jnp.logical_not(used))
    def _():
        out_ref[...] = jnp.zeros_like(out_ref)

    @pl.when(used & ((j == 0) | (be_ref[j] != be_ref[jnp.maximum(j - 1, 0)])))
    def _():
        wgb_ref[...] = wg_ref[...].astype(_BF16)
        wub_ref[...] = wu_ref[...].astype(_BF16)
        wdb_ref[...] = wd_ref[...].astype(_BF16)

    @pl.when(used)
    def _():
        row = lax.broadcasted_iota(jnp.int32, xs_ref.shape, 0)
        packed = jnp.where(row < valid_ref[j], xs_ref[...], jnp.uint32(0))
        rows = _unpack_bf16_pair(packed).astype(_BF16)
        hidden = wgb_ref.shape[1]
        out = None
        for c in range(0, hidden, hidden // EXPERT_PARTS):
            cols = slice(c, c + hidden // EXPERT_PARTS)
            gate = jnp.dot(rows, wgb_ref[:, cols], preferred_element_type=_F32)
            up = jnp.dot(rows, wub_ref[:, cols], preferred_element_type=_F32)
            act = ((gate * jax.nn.sigmoid(gate)) * up).astype(_BF16)
            part = jnp.dot(act, wdb_ref[cols, :], preferred_element_type=_F32)
            out = part if out is None else out + part
        half = out.shape[1] // 2
        out_ref[...] = _pack_bf16_pair(out[:, 0:half], out[:, half:])


def _experts(block_expert, block_valid, n_used, xs, wg, wu, wd, *, te):
    n_rows, half = xs.shape
    _, d, de = wg.shape

    def blk(j, be, bv, nu):
        return jnp.minimum(j, nu[0] - 1)

    def weight(j, be, bv, nu):
        return (be[blk(j, be, bv, nu)], 0, 0)

    return pl.pallas_call(
        _experts_body,
        grid_spec=pltpu.PrefetchScalarGridSpec(
            num_scalar_prefetch=3,
            grid=(n_rows // te,),
            in_specs=[pl.BlockSpec((te, half), lambda j, be, bv, nu: (blk(j, be, bv, nu), 0)),
                      pl.BlockSpec((None, d, de), weight),
                      pl.BlockSpec((None, d, de), weight),
                      pl.BlockSpec((None, de, d), weight)],
            out_specs=pl.BlockSpec((te, d // 2), lambda j, be, bv, nu: (j, 0)),
            scratch_shapes=[pltpu.VMEM((d, de), _BF16), pltpu.VMEM((d, de), _BF16),
                            pltpu.VMEM((de, d), _BF16)],
        ),
        out_shape=jax.ShapeDtypeStruct((n_rows, d // 2), jnp.uint32),
        compiler_params=_params("arbitrary"),
        name="experts",
    )(block_expert, block_valid, n_used, xs, wg, wu, wd)


def _sc_gather(table, idx):
    n, = idx.shape
    width = table.shape[1]
    workers = SC_CORES * SC_SUBCORES
    per_worker = n // workers
    chunks = per_worker // SC_CHUNK
    assert n % (workers * SC_CHUNK) == 0
    mesh = plsc.VectorSubcoreMesh(core_axis_name="c", subcore_axis_name="s",
                                  num_cores=SC_CORES, num_subcores=SC_SUBCORES)

    @functools.partial(
        pl.kernel, out_type=jax.ShapeDtypeStruct((n, width), table.dtype), mesh=mesh,
        scratch_types=[pltpu.VMEM((per_worker,), jnp.int32),
                       pltpu.VMEM((SC_CHUNK, width), table.dtype),
                       pltpu.SemaphoreType.DMA],
        name="sc_gather")
    def gather(table_hbm, idx_hbm, out_hbm, idx_v, rows_v, sem):
        worker = lax.axis_index("s") * SC_CORES + lax.axis_index("c")
        base = worker * per_worker
        pltpu.sync_copy(idx_hbm.at[pl.ds(base, per_worker)], idx_v)

        def chunk(j, carry):
            off = pl.multiple_of(j * SC_CHUNK, SC_CHUNK)
            pltpu.async_copy(table_hbm.at[idx_v.at[pl.ds(off, SC_CHUNK)]], rows_v, sem).wait()
            pltpu.sync_copy(rows_v, out_hbm.at[pl.ds(base + off, SC_CHUNK)])
            return carry

        lax.fori_loop(0, chunks, chunk, 0)

    return gather(table, idx)


def _final_body(x1_ref, route_ref, p_ref, r1_ref, r2_ref, gp_ref, wpg_ref, wpp_ref, gf_ref, o_ref):
    emb = jnp.dot(p_ref[...].astype(_BF16), wpp_ref[...], preferred_element_type=_F32)
    route = route_ref[...]
    x2 = x1_ref[...] + (_unpack_bf16_pair(r1_ref[...]) * route[:, 0:1]
                        + _unpack_bf16_pair(r2_ref[...]) * route[:, 1:2])
    gate = jax.nn.sigmoid(jnp.dot(_rms(x2, gp_ref[...]).astype(_BF16), wpg_ref[...],
                                  preferred_element_type=_F32))
    o_ref[...] = _rms(x2 + gate * emb, gf_ref[...])


def _final(x1, route, p, slot_rows, gp, wpg, wpp, gf, *, tm):
    t, d = x1.shape
    ple = p.shape[1]
    steps = t // tm
    row = lambda cols: pl.BlockSpec((tm, cols), lambda i: (i, 0))
    return pl.pallas_call(
        _final_body,
        grid=(steps,),
        in_specs=[row(d), row(LANES), row(ple), row(d // 2),
                  pl.BlockSpec((tm, d // 2), lambda i: (steps + i, 0)),
                  _const_spec((1, d)), _const_spec(wpg.shape), _const_spec(wpp.shape),
                  _const_spec((1, d))],
        out_specs=row(d),
        out_shape=jax.ShapeDtypeStruct((t, d), _F32),
        compiler_params=_params("arbitrary"),
        name="final",
    )(x1, route, p, slot_rows, slot_rows, gp, wpg, wpp, gf)


def _rope_tables(s):
    half = HEAD_DIM // 2
    inv = 1.0 / (ROPE_THETA ** (jnp.arange(0, HEAD_DIM, 2, dtype=_F32) / HEAD_DIM))
    ang = jnp.arange(s, dtype=_F32)[:, None] * inv[None, :]
    cos, sin = jnp.cos(ang), jnp.sin(ang)
    reps = LANES // half
    cos_t = jnp.tile(cos, (1, reps))
    sin_t = jnp.tile(jnp.concatenate([-sin, sin], axis=1), (1, reps // 2))
    return cos_t, sin_t


def _layer(x, p, attn_norm, w_in, b_forget, w_fox_branch, w_moba_branch, w_out, moe_norm,
           w_group, b_group, w_fine, b_fine, w_gate, w_up, w_down, ple_norm, w_ple_gate, w_ple_proj,
           final_gain):
    b, s, d = x.shape
    t = b * s
    width = N_FOX_HEADS * HEAD_DIM
    assert N_MOBA_HEADS * HEAD_DIM == width and MOBA_BLOCK % ATTN_Q == 0 and ATTN_K % MOBA_BLOCK == 0
    assert BIAS_PIECES * N_FOX_HEADS <= LANES and s % ATTN_K == 0
    assert s % ROW_TILE == 0 and ROW_TILE % MOBA_BLOCK == 0 and s // MOBA_BLOCK <= LANES
    assert t % FINAL_TILE == 0 and N_EXPERTS + N_GROUPS <= LANES

    qkv_cols = 6 * width
    f_cols = N_FOX_HEADS
    chunk = lambda c: w_in[:, c * width:(c + 1) * width]
    w_rows = jnp.concatenate([chunk(1), chunk(4), w_in[:, qkv_cols + f_cols:]], axis=1).astype(_BF16)
    w_cols = jnp.concatenate([chunk(0), chunk(2), chunk(3), chunk(5)], axis=1).T.astype(_BF16)
    wf = jnp.zeros((d, LANES), _BF16).at[:, :f_cols].set(w_in[:, qkv_cols:qkv_cols + f_cols].astype(_BF16))
    bfp = jnp.zeros((1, LANES), _F32).at[0, :f_cols].set(b_forget.astype(_F32))
    cos, sin = _rope_tables(s)

    qft, kf, vft, qmt, km, vmt, gates, cb, kmean, kabs, bmax = _inproj(
        x, attn_norm.reshape(1, d), w_rows, w_cols, wf, bfp, cos, sin, cos.T, sin.T, tm=ROW_TILE)
    y_fox = _attention(qft, kf, cb, vft, kabs, bmax, mode="fox")
    block_of_key = jnp.arange(s, dtype=jnp.int32)[:, None] // MOBA_BLOCK
    block_onehot = (block_of_key == jnp.arange(LANES, dtype=jnp.int32)[None, :]).astype(_BF16)
    y_moba = _attention(qmt, km, block_onehot, vmt, kmean, bmax, mode="moba")

    wr = jnp.zeros((d, LANES), _BF16)
    wr = wr.at[:, :N_EXPERTS].set(w_fine.astype(_BF16))
    wr = wr.at[:, N_EXPERTS:N_EXPERTS + N_GROUPS].set(w_group.astype(_BF16))
    br = jnp.zeros((1, LANES), _F32)
    br = br.at[0, :N_EXPERTS].set(b_fine.astype(_F32))
    br = br.at[0, N_EXPERTS:N_EXPERTS + N_GROUPS].set(b_group.astype(_F32))
    x1, h2p, route, meta, counts = _postattn(
        y_fox.reshape(t, width), y_moba.reshape(t, width), gates.reshape(t, 2 * d), x.reshape(t, d),
        w_fox_branch.astype(_BF16), w_moba_branch.astype(_BF16), w_out.astype(_BF16),
        moe_norm.reshape(1, d), wr, br, tm=ROW_TILE)

    te = EXPERT_TILE
    n_blk = (2 * t) // te + N_EXPERTS
    counts = counts[0, :N_EXPERTS].astype(jnp.int32)
    blocks_per = (counts + te - 1) // te
    block_end = jnp.cumsum(blocks_per)
    row_start = (block_end - blocks_per) * te
    expert_ids = jnp.arange(N_EXPERTS, dtype=jnp.int32)[:, None, None]
    slot_major = meta[4:6] + jnp.sum(
        jnp.where(meta[2:4][None] == expert_ids, row_start[:, None, None], 0), axis=0)
    n_used = block_end[-1:].astype(jnp.int32)
    block_ids = jnp.arange(n_blk, dtype=jnp.int32)
    block_expert = jnp.minimum(
        jnp.sum((block_ids[:, None] >= block_end[None, :]).astype(jnp.int32), axis=1), N_EXPERTS - 1)

    first_block = block_end - blocks_per
    block_valid = jnp.clip(counts[block_expert] - (block_ids - first_block[block_expert]) * te, 0, te)
    xs = _sc_scatter(h2p, slot_major.reshape(2, t // SC_SCATTER_CHUNK, SC_SCATTER_CHUNK), n_blk * te)
    rows = _experts(block_expert, block_valid, n_used, xs, w_gate, w_up, w_down, te=te)
    slot_rows = _sc_gather(rows, slot_major.reshape(-1))
    out = _final(x1, route, p.reshape(t, -1), slot_rows, ple_norm.reshape(1, d),
                 w_ple_gate.astype(_BF16), w_ple_proj.astype(_BF16), final_gain.reshape(1, d),
                 tm=FINAL_TILE)
    return out.reshape(b, s, d)


def kernel(x, p, attn_norm, w_in, b_forget, w_fox_branch, w_moba_branch, w_out, moe_norm, w_group,
           b_group, w_fine, b_fine, w_gate, w_up, w_down, ple_norm, w_ple_gate, w_ple_proj, final_norm):
    depth = p.shape[0]
    assert depth == 1, "the final norm is fused into the last layer's kernel"
    i = 0
    return _layer(x, p[i], attn_norm[i], w_in[i], b_forget[i], w_fox_branch[i], w_moba_branch[i],
                  w_out[i], moe_norm[i], w_group[i], b_group[i], w_fine[i], b_fine[i], w_gate[i],
                  w_up[i], w_down[i], ple_norm[i], w_ple_gate[i], w_ple_proj[i], final_norm)
```

```python
import functools

import jax
import jax.numpy as jnp
from jax import lax
from jax.experimental import pallas as pl
from jax.experimental.pallas import tpu as pltpu
from jax.experimental.pallas import tpu_sc as plsc

HEAD_DIM = 64
N_FOX_HEADS = 8
N_MOBA_HEADS = 8
MOBA_BLOCK = 256
MOBA_TOPK = 3
ROPE_THETA = 10000.0
N_GROUPS = 4
EXPERTS_PER_GROUP = 8
N_EXPERTS = N_GROUPS * EXPERTS_PER_GROUP
RMS_EPS = 1e-6

LANES = 128
HEADS_PER_STEP = LANES // HEAD_DIM
ATTN_Q = 256
ATTN_K = 256
PIPE_UNITS = 8
SOFTMAX_LAG = 4
ONES_ROWS = 16
SOFTMAX_ROWS = 64
BIAS_PIECES = 3
ROW_TILE = 512
SC_CORES = 2
SC_SUBCORES = 16
SC_CHUNK = 64
SC_SCATTER_CHUNK = 128
FINAL_TILE = 256
EXPERT_PARTS = 2
EXPERT_TILE = 512
LOG2_E = 1.4426950408889634
UNDERFLOW_LOG2 = 160.0
NEG_BIG = -1e30
VMEM_LIMIT = 48 * 1024 * 1024

_BF16 = jnp.bfloat16
_F32 = jnp.float32


def _params(*sem):
    return pltpu.CompilerParams(dimension_semantics=sem, vmem_limit_bytes=VMEM_LIMIT)


def _rms(x, g):
    return x * lax.rsqrt(jnp.mean(x * x, axis=-1, keepdims=True) + RMS_EPS) * g


def _const_spec(shape):
    return pl.BlockSpec(shape, lambda *_: (0,) * len(shape))


def _inproj_body(x_ref, g_ref, w_ref, wt_ref, wf_ref, bf_ref, cos_ref, sin_ref, cost_ref, sint_ref,
                 qft_ref, kf_ref, vft_ref, qmt_ref, km_ref, vmt_ref, gates_ref, cb_ref, kmean_ref,
                 kabs_ref, bmax_ref,
                 carry_ref, *, tm, width, gate_cols):
    j = pl.program_id(1)
    h = _rms(x_ref[...], g_ref[...])
    hb = h.astype(_BF16)
    hbt = h.T.astype(_BF16)
    reps = width // LANES
    half = HEAD_DIM // 2

    def proj(c):
        return jnp.dot(hb, w_ref[:, c * width:(c + 1) * width], preferred_element_type=_F32)

    def proj_t(c):
        return jnp.dot(wt_ref[c * width:(c + 1) * width, :], hbt, preferred_element_type=_F32)

    def rope(t):
        cos = jnp.concatenate([cos_ref[...]] * reps, axis=1)
        sin = jnp.concatenate([sin_ref[...]] * reps, axis=1)
        first = (lax.broadcasted_iota(jnp.int32, (1, width), 1) % HEAD_DIM) < half
        partner = jnp.where(first, pltpu.roll(t, width - half, 1), pltpu.roll(t, half, 1))
        return t * cos + partner * sin

    def rope_t(t):
        cos = jnp.concatenate([cost_ref[...]] * reps, axis=0)
        sin = jnp.concatenate([sint_ref[...]] * reps, axis=0)
        first = (lax.broadcasted_iota(jnp.int32, (width, 1), 0) % HEAD_DIM) < half
        partner = jnp.where(first, pltpu.roll(t, width - half, 0), pltpu.roll(t, half, 0))
        return t * cos + partner * sin

    scale = HEAD_DIM ** -0.5 * LOG2_E
    qft_ref[...] = (proj_t(0) * scale).astype(_BF16)
    vft_ref[...] = proj_t(1).astype(_BF16)
    qmt_ref[...] = (rope_t(proj_t(2)) * scale).astype(_BF16)
    vmt_ref[...] = proj_t(3).astype(_BF16)
    kf = proj(0).astype(_BF16)
    kf_ref[...] = kf
    kf_abs = jnp.abs(kf.astype(_F32))
    km = rope(proj(1))
    km_ref[...] = km.astype(_BF16)

    @pl.when(j == 0)
    def _():
        kmean_ref[...] = jnp.zeros_like(kmean_ref)
        kabs_ref[...] = jnp.zeros_like(kabs_ref)
        bmax_ref[...] = jnp.zeros_like(bmax_ref)
        carry_ref[...] = jnp.zeros_like(carry_ref)

    blocks = tm // MOBA_BLOCK
    for b in range(blocks):
        rows = slice(b * MOBA_BLOCK, (b + 1) * MOBA_BLOCK)
        kmean_ref[pl.ds(j * blocks + b, 1), :] = jnp.mean(km[rows], axis=0, keepdims=True)
        kabs_ref[pl.ds(j * blocks + b, 1), :] = jnp.max(kf_abs[rows], axis=0, keepdims=True)

    for c in range(gate_cols // width):
        g = jnp.dot(hb, w_ref[:, (2 + c) * width:(3 + c) * width],
                    preferred_element_type=_F32)
        gates_ref[:, c * width:(c + 1) * width] = jax.nn.sigmoid(g).astype(_BF16)

    z = jnp.dot(hb, wf_ref[...], preferred_element_type=_F32) + bf_ref[...]
    log_f = jnp.minimum(z, 0.0) - jnp.log1p(jnp.exp(-jnp.abs(z)))
    r = lax.broadcasted_iota(jnp.int32, (LANES, LANES), 0)
    c = lax.broadcasted_iota(jnp.int32, (LANES, LANES), 1)
    tri = (c <= r).astype(_BF16)
    offset = carry_ref[0:1, :]
    blocks_cum = []
    for blk in range(tm // LANES):
        rest = log_f[blk * LANES:(blk + 1) * LANES]
        within = jnp.zeros((LANES, LANES), _F32)
        for _ in range(BIAS_PIECES):
            part = rest.astype(_BF16)
            rest = rest - part.astype(_F32)
            within = within + jnp.dot(tri, part, preferred_element_type=_F32)
        blocks_cum.append(within + offset)
        offset = offset + within[LANES - 1:LANES, :]
    cum = jnp.concatenate(blocks_cum, axis=0)
    carry_ref[...] = jnp.broadcast_to(offset, carry_ref.shape)

    head_lane = lax.broadcasted_iota(jnp.int32, (1, LANES), 1) < N_FOX_HEADS
    bias = jnp.where(head_lane, -cum * LOG2_E, 0.0)
    for b in range(blocks):
        bmax_ref[pl.ds(j * blocks + b, 1), :] = jnp.max(
            bias[b * MOBA_BLOCK:(b + 1) * MOBA_BLOCK], axis=0, keepdims=True)
    rest = bias
    placed = jnp.zeros((tm, LANES), _F32)
    for piece in range(BIAS_PIECES):
        part = rest.astype(_BF16).astype(_F32)
        rest = rest - part
        placed = placed + (pltpu.roll(part, N_FOX_HEADS * piece, 1) if piece else part)
    cb_ref[...] = placed.astype(_BF16)


def _inproj(x, g, w, wt, wf, bfp, cos, sin, cos_t, sin_t, *, tm):
    b, s, d = x.shape
    width = N_FOX_HEADS * HEAD_DIM
    gate_cols = w.shape[1] - 2 * width
    act = jax.ShapeDtypeStruct((b, s, width), _BF16)
    act_t = jax.ShapeDtypeStruct((b, width, s), _BF16)
    act_spec = pl.BlockSpec((None, tm, width), lambda bi, j: (bi, j, 0))
    act_t_spec = pl.BlockSpec((None, width, tm), lambda bi, j: (bi, 0, j))
    return pl.pallas_call(
        functools.partial(_inproj_body, tm=tm, width=width, gate_cols=gate_cols),
        grid=(b, s // tm),
        in_specs=[
            pl.BlockSpec((None, tm, d), lambda bi, j: (bi, j, 0)),
            _const_spec((1, d)),
            _const_spec(w.shape),
            _const_spec(wt.shape),
            _const_spec(wf.shape),
            _const_spec(bfp.shape),
            pl.BlockSpec((tm, LANES), lambda bi, j: (j, 0)),
            pl.BlockSpec((tm, LANES), lambda bi, j: (j, 0)),
            pl.BlockSpec((LANES, tm), lambda bi, j: (0, j)),
            pl.BlockSpec((LANES, tm), lambda bi, j: (0, j)),
        ],
        out_specs=[act_t_spec, act_spec, act_t_spec, act_t_spec, act_spec, act_t_spec,
                   pl.BlockSpec((None, tm, gate_cols), lambda bi, j: (bi, j, 0)),
                   pl.BlockSpec((None, tm, LANES), lambda bi, j: (bi, j, 0)),
                   pl.BlockSpec((None, LANES, width), lambda bi, j: (bi, 0, 0)),
                   pl.BlockSpec((None, LANES, width), lambda bi, j: (bi, 0, 0)),
                   pl.BlockSpec((None, LANES, LANES), lambda bi, j: (bi, 0, 0))],
        out_shape=[act_t, act, act_t, act_t, act, act_t,
                   jax.ShapeDtypeStruct((b, s, gate_cols), _BF16),
                   jax.ShapeDtypeStruct((b, s, LANES), _BF16),
                   jax.ShapeDtypeStruct((b, LANES, width), _F32),
                   jax.ShapeDtypeStruct((b, LANES, width), _F32),
                   jax.ShapeDtypeStruct((b, LANES, LANES), _F32)],
        scratch_shapes=[pltpu.VMEM((8, LANES), _F32)],
        compiler_params=_params("arbitrary", "arbitrary"),
        name="inproj",
    )(x, g, w, wt, wf, bfp, cos, sin, cos_t, sin_t)


def _attn_body(uq_ref, uk_ref, qt_ref, k_ref, kx_ref, vt_ref, aux_ref, aux2_ref, o_ref,
               w_ref, m_ref, l_ref, acc_ref, st_ref, p_ref, alpha_ref, pmax_ref, qn_ref, *,
               mode, tq, tk, nb, n_diag, n_units):
    hp = pl.program_id(1)
    wide = HEADS_PER_STEP * tq
    nq = w_ref.shape[0]
    feat = lax.broadcasted_iota(jnp.int32, (LANES, 1), 0)
    xr = lax.broadcasted_iota(jnp.int32, (LANES, wide), 0)
    xc = lax.broadcasted_iota(jnp.int32, (LANES, wide), 1)

    def setup(i, carry):
        qt = qt_ref[:, pl.ds(pl.multiple_of(i * tq, tq), tq)]
        zero = jnp.zeros_like(qt)
        wq = jnp.concatenate([jnp.where(feat < HEAD_DIM, qt, zero),
                              jnp.where(feat < HEAD_DIM, zero, qt)], axis=1)
        if mode == "fox":
            head = hp * HEADS_PER_STEP + xc // tq
            lower = ((xr % N_FOX_HEADS == head) & (xr < BIAS_PIECES * N_FOX_HEADS)).astype(_BF16)
            qn_ref[i] = jnp.broadcast_to(
                jnp.sum(jnp.abs(wq.astype(_F32)), axis=0, keepdims=True), qn_ref.shape[1:])
        else:
            own = (i * tq) // MOBA_BLOCK
            gate = jnp.dot(aux_ref[...].astype(_BF16), wq, preferred_element_type=_F32)[0:nb]
            blk = lax.broadcasted_iota(jnp.int32, (nb, wide), 0)
            gate = jnp.where(blk < own, gate, -jnp.inf)
            keep = blk == own
            for _ in range(MOBA_TOPK):
                mx = jnp.max(gate, axis=0, keepdims=True)
                cand = jnp.where((gate == mx) & (mx > -jnp.inf), blk, nb)
                pick = blk == jnp.min(cand, axis=0, keepdims=True)
                keep = keep | pick
                gate = jnp.where(pick, -jnp.inf, gate)
            lower = jnp.where(keep, 0.0, NEG_BIG)
            if nb < LANES:
                lower = jnp.concatenate([lower, jnp.zeros((LANES - nb, wide), _F32)], axis=0)
            lower = lower.astype(_BF16)
        w_ref[i] = jnp.concatenate([wq, lower], axis=0)
        m_ref[i] = jnp.full(m_ref.shape[1:], NEG_BIG, _F32)
        l_ref[i] = jnp.zeros(l_ref.shape[1:], _F32)
        acc_ref[i] = jnp.zeros(acc_ref.shape[1:], _F32)
        return carry

    lax.fori_loop(0, nq, setup, 0)

    def unit(t):
        return uq_ref[t], uk_ref[t]

    def scores(t, slot, causal):
        q, kb = unit(t)
        start = pl.multiple_of(kb * tk, tk)
        ka = jnp.concatenate([k_ref[pl.ds(start, tk), :], kx_ref[pl.ds(start, tk), :]], axis=1)
        st = jnp.dot(ka, w_ref[q], preferred_element_type=_F32)
        if causal:
            key = start + lax.broadcasted_iota(jnp.int32, (tk, wide), 0)
            qry = q * tq + lax.broadcasted_iota(jnp.int32, (tk, wide), 1) % tq
            st = jnp.where(key <= qry, st, NEG_BIG)
        st_ref[slot] = st

    def softmax(t, slot):
        q, _ = unit(t)
        rows = SOFTMAX_ROWS
        m_prev = m_ref[q][0:1, :]
        part = None
        for r in range(0, tk, rows):
            cur = jnp.max(st_ref[slot, r:r + rows, :].reshape(rows // 8, 8, wide), axis=0)
            part = cur if part is None else jnp.maximum(part, cur)
        m_new = jnp.maximum(m_prev, jnp.max(part, axis=0, keepdims=True))
        for r in range(0, tk, rows):
            p_ref[slot, r:r + rows, :] = jnp.exp2(st_ref[slot, r:r + rows, :] - m_new).astype(_BF16)
        m_ref[q] = jnp.broadcast_to(m_new, m_ref.shape[1:])
        alpha_ref[slot] = jnp.broadcast_to(jnp.exp2(m_prev - m_new), alpha_ref.shape[1:])

    def values(t, slot):
        q, kb = unit(t)
        start = pl.multiple_of(kb * tk, tk)
        ones = jnp.ones((ONES_ROWS, tk), _BF16)
        for h in range(HEADS_PER_STEP):
            cols = slice(h * tq, (h + 1) * tq)
            feats = slice(h * HEAD_DIM, (h + 1) * HEAD_DIM)
            lhs = jnp.concatenate([vt_ref[feats, pl.ds(start, tk)], ones], axis=0)
            pv = jnp.dot(lhs, p_ref[slot, :, cols], preferred_element_type=_F32)
            alpha = alpha_ref[slot, 0:1, cols]
            acc_ref[q, feats, :] = acc_ref[q, feats, :] * alpha + pv[0:HEAD_DIM]
            l_ref[q, :, cols] = jnp.broadcast_to(
                alpha * l_ref[q, 0:1, cols] + pv[HEAD_DIM:HEAD_DIM + 1], (8, tq))

    def step(base, c, causal, do_scores=True, do_softmax=True, do_values=True):
        u = base + c
        if do_scores:
            scores(u, c % PIPE_UNITS, causal)
        if do_values:
            values(u - 2 * SOFTMAX_LAG, (c - 2 * SOFTMAX_LAG) % PIPE_UNITS)
        if do_softmax:
            softmax(u - SOFTMAX_LAG, (c - SOFTMAX_LAG) % PIPE_UNITS)

    def group(causal):
        def body(j, carry):
            for c in range(PIPE_UNITS):
                step(j * PIPE_UNITS, c, causal)
            return carry
        return body

    def pipeline(g_first, g_end, causal_groups):
        for c in range(PIPE_UNITS):
            step(g_first * PIPE_UNITS, c, g_first < causal_groups,
                 do_softmax=c >= SOFTMAX_LAG, do_values=c >= 2 * SOFTMAX_LAG)
        if g_first + 1 < causal_groups:
            lax.fori_loop(g_first + 1, causal_groups, group(True), 0)
        lax.fori_loop(max(g_first + 1, causal_groups), g_end, group(False), 0)
        for c in range(2 * SOFTMAX_LAG):
            step(g_end * PIPE_UNITS, c, False, do_scores=False, do_softmax=c < SOFTMAX_LAG)

    diag_groups = n_diag // PIPE_UNITS
    if mode == "moba":
        pipeline(0, n_units // PIPE_UNITS, diag_groups)
    else:
        pipeline(0, diag_groups, diag_groups)
        nq_blocks = n_diag
        col = lax.broadcasted_iota(jnp.int32, (LANES, LANES), 1)
        first_head = lax.broadcasted_iota(jnp.int32, (1, wide), 1) < tq
        head0 = hp * HEADS_PER_STEP

        def per_head(table, pick0, pick1, fill):
            v0 = jnp.max(jnp.where(pick0, table, fill), axis=1, keepdims=True)[0:nb]
            v1 = jnp.max(jnp.where(pick1, table, fill), axis=1, keepdims=True)[0:nb]
            return jnp.where(first_head, v0, v1)

        bias_max = per_head(aux2_ref[...], col == head0, col == head0 + 1, NEG_BIG)
        k_absmax = per_head(aux_ref[...], col < HEAD_DIM, col >= HEAD_DIM, 0.0)
        blk = lax.broadcasted_iota(jnp.int32, (nb, 1), 0)

        needed = jnp.zeros((nb, 1), jnp.int32)
        for i in range(1, nq_blocks):
            bound = bias_max + qn_ref[i, 0:1, :] * k_absmax - m_ref[i, 0:1, :]
            live = jnp.max(bound, axis=1, keepdims=True) >= -UNDERFLOW_LOG2
            needed = jnp.maximum(needed, jnp.where(live & (blk < i), i - blk, 0))
        far = jnp.max(needed)
        n_live = far * nq_blocks - (far * (far + 1)) // 2
        live_groups = (n_live + PIPE_UNITS - 1) // PIPE_UNITS

        @pl.when(live_groups > 0)
        def _():
            pipeline(diag_groups, diag_groups + live_groups, diag_groups)

    def finish(i, carry):
        out_t = jnp.concatenate(
            [acc_ref[i, h * HEAD_DIM:(h + 1) * HEAD_DIM, :] / l_ref[i, 0:1, h * tq:(h + 1) * tq]
             for h in range(HEADS_PER_STEP)], axis=0)
        o_ref[pl.ds(pl.multiple_of(i * tq, tq), tq), :] = out_t.T.astype(o_ref.dtype)
        return carry

    lax.fori_loop(0, nq, finish, 0, unroll=2)


def _attention(qt, k, kx, vt, aux, aux2, *, mode):
    b, width, s = qt.shape
    tq, tk = ATTN_Q, ATTN_K
    nq = s // tq
    nb = s // MOBA_BLOCK
    wide = HEADS_PER_STEP * tq
    assert tq == tk
    units = [(i, (i * tq) // tk) for i in range(nq)]
    units += [(i, i - dist) for dist in range(1, nq) for i in range(dist, nq)]
    assert nq % PIPE_UNITS == 0 and len(units) % PIPE_UNITS == 0 and PIPE_UNITS >= 2 * SOFTMAX_LAG
    unit_q = jnp.asarray([u[0] for u in units], jnp.int32)
    unit_k = jnp.asarray([u[1] for u in units], jnp.int32)
    aux_spec = pl.BlockSpec((None, LANES, LANES), lambda bi, hp, uq, uk: (bi, 0, hp))
    aux2_spec = pl.BlockSpec((None, LANES, LANES), lambda bi, hp, uq, uk: (bi, 0, 0))
    if mode == "fox":
        kx_spec = pl.BlockSpec((None, s, LANES), lambda bi, hp, uq, uk: (bi, 0, 0))
    else:
        kx_spec = pl.BlockSpec(kx.shape, lambda bi, hp, uq, uk: (0, 0))
    return pl.pallas_call(
        functools.partial(_attn_body, mode=mode, tq=tq, tk=tk, nb=nb, n_diag=nq, n_units=len(units)),
        grid_spec=pltpu.PrefetchScalarGridSpec(
            num_scalar_prefetch=2,
            grid=(b, width // LANES),
            in_specs=[
                pl.BlockSpec((None, LANES, s), lambda bi, hp, uq, uk: (bi, hp, 0)),
                pl.BlockSpec((None, s, LANES), lambda bi, hp, uq, uk: (bi, 0, hp)),
                kx_spec,
                pl.BlockSpec((None, LANES, s), lambda bi, hp, uq, uk: (bi, hp, 0)),
                aux_spec,
                aux2_spec,
            ],
            out_specs=pl.BlockSpec((None, s, LANES), lambda bi, hp, uq, uk: (bi, 0, hp)),
            scratch_shapes=[
                pltpu.VMEM((nq, 2 * LANES, wide), _BF16),
                pltpu.VMEM((nq, 8, wide), _F32),
                pltpu.VMEM((nq, 8, wide), _F32),
                pltpu.VMEM((nq, LANES, tq), _F32),
                pltpu.VMEM((PIPE_UNITS, tk, wide), _F32),
                pltpu.VMEM((PIPE_UNITS, tk, wide), _BF16),
                pltpu.VMEM((PIPE_UNITS, 8, wide), _F32),
                pltpu.VMEM((PIPE_UNITS, 8, wide), _F32),
                pltpu.VMEM((nq, 8, wide), _F32),
            ],
        ),
        out_shape=jax.ShapeDtypeStruct((b, s, width), _BF16),
        compiler_params=_params("arbitrary", "arbitrary"),
        name=mode,
    )(unit_q, unit_k, qt, k, kx, vt, aux, aux2)


def _pack_bf16_pair(a, b):
    lo = pltpu.bitcast(a.astype(_BF16).astype(_F32), jnp.uint32) >> 16
    hi = pltpu.bitcast(b.astype(_BF16).astype(_F32), jnp.uint32) & jnp.uint32(0xFFFF0000)
    return lo | hi


def _unpack_bf16_pair(u):
    lo = pltpu.bitcast(u << 16, _F32)
    hi = pltpu.bitcast(u & jnp.uint32(0xFFFF0000), _F32)
    return jnp.concatenate([lo, hi], axis=1)


def _postattn_body(yf_ref, ym_ref, gates_ref, x_ref, wfb_ref, wmb_ref, wout_ref, g_ref, wr_ref, br_ref,
                   x1_ref, h2_ref, route_ref, meta_ref, counts_ref, carry_ref, *, tm, d):
    step = pl.program_id(0)
    ya = jnp.dot(yf_ref[...], wfb_ref[...], preferred_element_type=_F32)
    yb = jnp.dot(ym_ref[...], wmb_ref[...], preferred_element_type=_F32)
    mixed = gates_ref[:, 0:d].astype(_F32) * ya + gates_ref[:, d:2 * d].astype(_F32) * yb
    x1 = x_ref[...] + jnp.dot(mixed.astype(_BF16), wout_ref[...], preferred_element_type=_F32)
    x1_ref[...] = x1
    h2 = _rms(x1, g_ref[...])
    h2_ref[...] = _pack_bf16_pair(h2[:, 0:d // 2], h2[:, d // 2:d])

    logits = jnp.dot(h2.astype(_BF16), wr_ref[...], preferred_element_type=_F32) + br_ref[...]
    lane = lax.broadcasted_iota(jnp.int32, (tm, LANES), 1)
    is_group = (lane >= N_EXPERTS) & (lane < N_EXPERTS + N_GROUPS)
    gl = jnp.where(is_group, logits, -jnp.inf)
    ge = jnp.exp(gl - jnp.max(gl, axis=-1, keepdims=True))
    gp = ge / jnp.sum(ge, axis=-1, keepdims=True)
    g_top = jnp.max(gp, axis=-1, keepdims=True)
    g_idx = jnp.min(jnp.where(is_group & (gp == g_top), lane, 2 * LANES), axis=-1,
                    keepdims=True) - N_EXPERTS
    in_group = (lane < N_EXPERTS) & ((lane // EXPERTS_PER_GROUP) == g_idx)
    fl = jnp.where(in_group, logits, -jnp.inf)
    f1 = jnp.max(fl, axis=-1, keepdims=True)
    e1 = jnp.min(jnp.where(fl == f1, lane, 2 * LANES), axis=-1, keepdims=True)
    fl2 = jnp.where(lane == e1, -jnp.inf, fl)
    f2 = jnp.max(fl2, axis=-1, keepdims=True)
    e2 = jnp.min(jnp.where(fl2 == f2, lane, 2 * LANES), axis=-1, keepdims=True)
    t2 = jnp.exp(f2 - f1)
    w1 = g_top * (1.0 / (1.0 + t2))
    w2 = g_top * (t2 / (1.0 + t2))

    @pl.when(step == 0)
    def _():
        carry_ref[...] = jnp.zeros_like(carry_ref)

    onehot = ((lane == e1) | (lane == e2)).astype(_F32)
    rr = lax.broadcasted_iota(jnp.int32, (tm, tm), 0)
    cc = lax.broadcasted_iota(jnp.int32, (tm, tm), 1)
    before = jnp.dot((cc < rr).astype(_BF16), onehot.astype(_BF16), preferred_element_type=_F32)
    before = before + carry_ref[0:1, :]
    r1 = jnp.sum(jnp.where(lane == e1, before, 0.0), axis=-1, keepdims=True)
    r2 = jnp.sum(jnp.where(lane == e2, before, 0.0), axis=-1, keepdims=True)
    total = carry_ref[0:1, :] + jnp.sum(onehot, axis=0, keepdims=True)
    carry_ref[...] = jnp.broadcast_to(total, carry_ref.shape)
    counts_ref[...] = jnp.broadcast_to(total, counts_ref.shape)

    slab = jnp.where(lane == 0, w1, 0.0)
    slab = jnp.where(lane == 1, w2, slab)
    slab = jnp.where(lane == 2, e1.astype(_F32), slab)
    slab = jnp.where(lane == 3, e2.astype(_F32), slab)
    slab = jnp.where(lane == 4, r1, slab)
    slab = jnp.where(lane == 5, r2, slab)
    route_ref[...] = slab
    meta_ref[...] = slab.T[0:8].astype(jnp.int32)


def _postattn(yf, ym, gates, x, wfb, wmb, wout, g, wr, br, *, tm):
    t, d = x.shape
    width = yf.shape[1]
    row = lambda cols: pl.BlockSpec((tm, cols), lambda i: (i, 0))
    return pl.pallas_call(
        functools.partial(_postattn_body, tm=tm, d=d),
        grid=(t // tm,),
        in_specs=[row(width), row(width), row(2 * d), row(d),
                  _const_spec(wfb.shape), _const_spec(wmb.shape), _const_spec(wout.shape),
                  _const_spec((1, d)), _const_spec(wr.shape), _const_spec(br.shape)],
        out_specs=[row(d), row(d // 2), row(LANES), pl.BlockSpec((8, tm), lambda i: (0, i)),
                   _const_spec((8, LANES))],
        out_shape=[jax.ShapeDtypeStruct((t, d), _F32),
                   jax.ShapeDtypeStruct((t, d // 2), jnp.uint32),
                   jax.ShapeDtypeStruct((t, LANES), _F32),
                   jax.ShapeDtypeStruct((8, t), jnp.int32),
                   jax.ShapeDtypeStruct((8, LANES), _F32)],
        scratch_shapes=[pltpu.VMEM((8, LANES), _F32)],
        compiler_params=_params("arbitrary"),
        name="postattn",
    )(yf, ym, gates, x, wfb, wmb, wout, g, wr, br)


def _sc_scatter(rows, idx, n_out):
    n, width = rows.shape
    workers = SC_CORES * SC_SUBCORES
    per_worker = n // workers
    chunks = per_worker // SC_SCATTER_CHUNK
    assert n % (workers * SC_SCATTER_CHUNK) == 0 and idx.shape == (2, n // SC_SCATTER_CHUNK, SC_SCATTER_CHUNK)
    mesh = plsc.VectorSubcoreMesh(core_axis_name="c", subcore_axis_name="s",
                                  num_cores=SC_CORES, num_subcores=SC_SUBCORES)

    @functools.partial(
        pl.kernel, out_type=jax.ShapeDtypeStruct((n_out, width), rows.dtype), mesh=mesh,
        scratch_types=[pltpu.VMEM((2, chunks, SC_SCATTER_CHUNK), jnp.int32),
                       pltpu.VMEM((SC_SCATTER_CHUNK, width), rows.dtype),
                       pltpu.SemaphoreType.DMA],
        name="sc_scatter")
    def scatter(rows_hbm, idx_hbm, out_hbm, idx_v, rows_v, sem):
        worker = lax.axis_index("s") * SC_CORES + lax.axis_index("c")
        for k in range(2):
            pltpu.sync_copy(idx_hbm.at[k, pl.ds(worker * chunks, chunks)], idx_v.at[k])

        def chunk(j, carry):
            start = worker * per_worker + j * SC_SCATTER_CHUNK
            pltpu.sync_copy(rows_hbm.at[pl.ds(start, SC_SCATTER_CHUNK)], rows_v)
            for k in range(2):
                pltpu.async_copy(rows_v, out_hbm.at[idx_v.at[k, j]], sem).wait()
            return carry

        lax.fori_loop(0, chunks, chunk, 0)

    return scatter(rows, idx)


def _experts_body(be_ref, valid_ref, nused_ref, xs_ref, wg_ref, wu_ref, wd_ref, out_ref,
                  wgb_ref, wub_ref, wdb_ref):
    j = pl.program_id(0)
    used = j < nused_ref[0]

    @pl.when(jnp.logical_not(used))
    def _():
        out_ref[...] = jnp.zeros_like(out_ref)

    @pl.when(used & ((j == 0) | (be_ref[j] != be_ref[jnp.maximum(j - 1, 0)])))
    def _():
        wgb_ref[...] = wg_ref[...].astype(_BF16)
        wub_ref[...] = wu_ref[...].astype(_BF16)
        wdb_ref[...] = wd_ref[...].astype(_BF16)

    @pl.when(used)
    def _():
        row = lax.broadcasted_iota(jnp.int32, xs_ref.shape, 0)
        packed = jnp.where(row < valid_ref[j], xs_ref[...], jnp.uint32(0))
        rows = _unpack_bf16_pair(packed).astype(_BF16)
        hidden = wgb_ref.shape[1]
        out = None
        for c in range(0, hidden, hidden // EXPERT_PARTS):
            cols = slice(c, c + hidden // EXPERT_PARTS)
            gate = jnp.dot(rows, wgb_ref[:, cols], preferred_element_type=_F32)
            up = jnp.dot(rows, wub_ref[:, cols], preferred_element_type=_F32)
            act = ((gate * jax.nn.sigmoid(gate)) * up).astype(_BF16)
            part = jnp.dot(act, wdb_ref[cols, :], preferred_element_type=_F32)
            out = part if out is None else out + part
        half = out.shape[1] // 2
        out_ref[...] = _pack_bf16_pair(out[:, 0:half], out[:, half:])


def _experts(block_expert, block_valid, n_used, xs, wg, wu, wd, *, te):
    n_rows, half = xs.shape
    _, d, de = wg.shape

    def blk(j, be, bv, nu):
        return jnp.minimum(j, nu[0] - 1)

    def weight(j, be, bv, nu):
        return (be[blk(j, be, bv, nu)], 0, 0)

    return pl.pallas_call(
        _experts_body,
        grid_spec=pltpu.PrefetchScalarGridSpec(
            num_scalar_prefetch=3,
            grid=(n_rows // te,),
            in_specs=[pl.BlockSpec((te, half), lambda j, be, bv, nu: (blk(j, be, bv, nu), 0)),
                      pl.BlockSpec((None, d, de), weight),
                      pl.BlockSpec((None, d, de), weight),
                      pl.BlockSpec((None, de, d), weight)],
            out_specs=pl.BlockSpec((te, d // 2), lambda j, be, bv, nu: (j, 0)),
            scratch_shapes=[pltpu.VMEM((d, de), _BF16), pltpu.VMEM((d, de), _BF16),
                            pltpu.VMEM((de, d), _BF16)],
        ),
        out_shape=jax.ShapeDtypeStruct((n_rows, d // 2), jnp.uint32),
        compiler_params=_params("arbitrary"),
        name="experts",
    )(block_expert, block_valid, n_used, xs, wg, wu, wd)


def _sc_gather(table, idx):
    n, = idx.shape
    width = table.shape[1]
    workers = SC_CORES * SC_SUBCORES
    per_worker = n // workers
    chunks = per_worker // SC_CHUNK
    assert n % (workers * SC_CHUNK) == 0
    mesh = plsc.VectorSubcoreMesh(core_axis_name="c", subcore_axis_name="s",
                                  num_cores=SC_CORES, num_subcores=SC_SUBCORES)

    @functools.partial(
        pl.kernel, out_type=jax.ShapeDtypeStruct((n, width), table.dtype), mesh=mesh,
        scratch_types=[pltpu.VMEM((per_worker,), jnp.int32),
                       pltpu.VMEM((SC_CHUNK, width), table.dtype),
                       pltpu.SemaphoreType.DMA],
        name="sc_gather")
    def gather(table_hbm, idx_hbm, out_hbm, idx_v, rows_v, sem):
        worker = lax.axis_index("s") * SC_CORES + lax.axis_index("c")
        base = worker * per_worker
        pltpu.sync_copy(idx_hbm.at[pl.ds(base, per_worker)], idx_v)

        def chunk(j, carry):
            off = pl.multiple_of(j * SC_CHUNK, SC_CHUNK)
            pltpu.async_copy(table_hbm.at[idx_v.at[pl.ds(off, SC_CHUNK)]], rows_v, sem).wait()
            pltpu.sync_copy(rows_v, out_hbm.at[pl.ds(base + off, SC_CHUNK)])
            return carry

        lax.fori_loop(0, chunks, chunk, 0)

    return gather(table, idx)


def _final_body(x1_ref, route_ref, p_ref, r1_ref, r2_ref, gp_ref, wpg_ref, wpp_ref, gf_ref, o_ref):
    emb = jnp.dot(p_ref[...].astype(_BF16), wpp_ref[...], preferred_element_type=_F32)
    route = route_ref[...]
    x2 = x1_ref[...] + (_unpack_bf16_pair(r1_ref[...]) * route[:, 0:1]
                        + _unpack_bf16_pair(r2_ref[...]) * route[:, 1:2])
    gate = jax.nn.sigmoid(jnp.dot(_rms(x2, gp_ref[...]).astype(_BF16), wpg_ref[...],
                                  preferred_element_type=_F32))
    o_ref[...] = _rms(x2 + gate * emb, gf_ref[...])


def _final(x1, route, p, slot_rows, gp, wpg, wpp, gf, *, tm):
    t, d = x1.shape
    ple = p.shape[1]
    steps = t // tm
    row = lambda cols: pl.BlockSpec((tm, cols), lambda i: (i, 0))
    return pl.pallas_call(
        _final_body,
        grid=(steps,),
        in_specs=[row(d), row(LANES), row(ple), row(d // 2),
                  pl.BlockSpec((tm, d // 2), lambda i: (steps + i, 0)),
                  _const_spec((1, d)), _const_spec(wpg.shape), _const_spec(wpp.shape),
                  _const_spec((1, d))],
        out_specs=row(d),
        out_shape=jax.ShapeDtypeStruct((t, d), _F32),
        compiler_params=_params("arbitrary"),
        name="final",
    )(x1, route, p, slot_rows, slot_rows, gp, wpg, wpp, gf)


def _rope_tables(s):
    half = HEAD_DIM // 2
    inv = 1.0 / (ROPE_THETA ** (jnp.arange(0, HEAD_DIM, 2, dtype=_F32) / HEAD_DIM))
    ang = jnp.arange(s, dtype=_F32)[:, None] * inv[None, :]
    cos, sin = jnp.cos(ang), jnp.sin(ang)
    reps = LANES // half
    cos_t = jnp.tile(cos, (1, reps))
    sin_t = jnp.tile(jnp.concatenate([-sin, sin], axis=1), (1, reps // 2))
    return cos_t, sin_t


def _layer(x, p, attn_norm, w_in, b_forget, w_fox_branch, w_moba_branch, w_out, moe_norm,
           w_group, b_group, w_fine, b_fine, w_gate, w_up, w_down, ple_norm, w_ple_gate, w_ple_proj,
           final_gain):
    b, s, d = x.shape
    t = b * s
    width = N_FOX_HEADS * HEAD_DIM
    assert N_MOBA_HEADS * HEAD_DIM == width and MOBA_BLOCK % ATTN_Q == 0 and ATTN_K % MOBA_BLOCK == 0
    assert BIAS_PIECES * N_FOX_HEADS <= LANES and s % ATTN_K == 0
    assert s % ROW_TILE == 0 and ROW_TILE % MOBA_BLOCK == 0 and s // MOBA_BLOCK <= LANES
    assert t % FINAL_TILE == 0 and N_EXPERTS + N_GROUPS <= LANES

    qkv_cols = 6 * width
    f_cols = N_FOX_HEADS
    chunk = lambda c: w_in[:, c * width:(c + 1) * width]
    w_rows = jnp.concatenate([chunk(1), chunk(4), w_in[:, qkv_cols + f_cols:]], axis=1).astype(_BF16)
    w_cols = jnp.concatenate([chunk(0), chunk(2), chunk(3), chunk(5)], axis=1).T.astype(_BF16)
    wf = jnp.zeros((d, LANES), _BF16).at[:, :f_cols].set(w_in[:, qkv_cols:qkv_cols + f_cols].astype(_BF16))
    bfp = jnp.zeros((1, LANES), _F32).at[0, :f_cols].set(b_forget.astype(_F32))
    cos, sin = _rope_tables(s)

    qft, kf, vft, qmt, km, vmt, gates, cb, kmean, kabs, bmax = _inproj(
        x, attn_norm.reshape(1, d), w_rows, w_cols, wf, bfp, cos, sin, cos.T, sin.T, tm=ROW_TILE)
    y_fox = _attention(qft, kf, cb, vft, kabs, bmax, mode="fox")
    block_of_key = jnp.arange(s, dtype=jnp.int32)[:, None] // MOBA_BLOCK
    block_onehot = (block_of_key == jnp.arange(LANES, dtype=jnp.int32)[None, :]).astype(_BF16)
    y_moba = _attention(qmt, km, block_onehot, vmt, kmean, bmax, mode="moba")

    wr = jnp.zeros((d, LANES), _BF16)
    wr = wr.at[:, :N_EXPERTS].set(w_fine.astype(_BF16))
    wr = wr.at[:, N_EXPERTS:N_EXPERTS + N_GROUPS].set(w_group.astype(_BF16))
    br = jnp.zeros((1, LANES), _F32)
    br = br.at[0, :N_EXPERTS].set(b_fine.astype(_F32))
    br = br.at[0, N_EXPERTS:N_EXPERTS + N_GROUPS].set(b_group.astype(_F32))
    x1, h2p, route, meta, counts = _postattn(
        y_fox.reshape(t, width), y_moba.reshape(t, width), gates.reshape(t, 2 * d), x.reshape(t, d),
        w_fox_branch.astype(_BF16), w_moba_branch.astype(_BF16), w_out.astype(_BF16),
        moe_norm.reshape(1, d), wr, br, tm=ROW_TILE)

    te = EXPERT_TILE
    n_blk = (2 * t) // te + N_EXPERTS
    counts = counts[0, :N_EXPERTS].astype(jnp.int32)
    blocks_per = (counts + te - 1) // te
    block_end = jnp.cumsum(blocks_per)
    row_start = (block_end - blocks_per) * te
    expert_ids = jnp.arange(N_EXPERTS, dtype=jnp.int32)[:, None, None]
    slot_major = meta[4:6] + jnp.sum(
        jnp.where(meta[2:4][None] == expert_ids, row_start[:, None, None], 0), axis=0)
    n_used = block_end[-1:].astype(jnp.int32)
    block_ids = jnp.arange(n_blk, dtype=jnp.int32)
    block_expert = jnp.minimum(
        jnp.sum((block_ids[:, None] >= block_end[None, :]).astype(jnp.int32), axis=1), N_EXPERTS - 1)

    first_block = block_end - blocks_per
    block_valid = jnp.clip(counts[block_expert] - (block_ids - first_block[block_expert]) * te, 0, te)
    xs = _sc_scatter(h2p, slot_major.reshape(2, t // SC_SCATTER_CHUNK, SC_SCATTER_CHUNK), n_blk * te)
    rows = _experts(block_expert, block_valid, n_used, xs, w_gate, w_up, w_down, te=te)
    slot_rows = _sc_gather(rows, slot_major.reshape(-1))
    out = _final(x1, route, p.reshape(t, -1), slot_rows, ple_norm.reshape(1, d),
                 w_ple_gate.astype(_BF16), w_ple_proj.astype(_BF16), final_gain.reshape(1, d),
                 tm=FINAL_TILE)
    return out.reshape(b, s, d)


def kernel(x, p, attn_norm, w_in, b_forget, w_fox_branch, w_moba_branch, w_out, moe_norm, w_group,
           b_group, w_fine, b_fine, w_gate, w_up, w_down, ple_norm, w_ple_gate, w_ple_proj, final_norm):
    depth = p.shape[0]
    assert depth == 1, "the final norm is fused into the last layer's kernel"
    i = 0
    return _layer(x, p[i], attn_norm[i], w_in[i], b_forget[i], w_fox_branch[i], w_moba_branch[i],
                  w_out[i], moe_norm[i], w_group[i], b_group[i], w_fine[i], b_fine[i], w_gate[i],
                  w_up[i], w_down[i], ple_norm[i], w_ple_gate[i], w_ple_proj[i], final_norm)
```

```python
import functools

import jax
import jax.numpy as jnp
from jax import lax
from jax.experimental import pallas as pl
from jax.experimental.pallas import tpu as pltpu
from jax.experimental.pallas import tpu_sc as plsc

HEAD_DIM = 64
N_FOX_HEADS = 8
N_MOBA_HEADS = 8
MOBA_BLOCK = 256
MOBA_TOPK = 3
ROPE_THETA = 10000.0
N_GROUPS = 4
EXPERTS_PER_GROUP = 8
N_EXPERTS = N_GROUPS * EXPERTS_PER_GROUP
RMS_EPS = 1e-6

LANES = 128
HEADS_PER_STEP = LANES // HEAD_DIM
ATTN_Q = 256
ATTN_K = 256
PIPE_UNITS = 8
SOFTMAX_LAG = 4
ONES_ROWS = 16
SOFTMAX_ROWS = 64
BIAS_PIECES = 3
ROW_TILE = 512
SC_CORES = 2
SC_SUBCORES = 16
SC_CHUNK = 64
SC_SCATTER_CHUNK = 128
FINAL_TILE = 256
EXPERT_PARTS = 2
EXPERT_TILE = 512
LOG2_E = 1.4426950408889634
UNDERFLOW_LOG2 = 160.0
NEG_BIG = -1e30
VMEM_LIMIT = 48 * 1024 * 1024

_BF16 = jnp.bfloat16
_F32 = jnp.float32


def _params(*sem):
    return pltpu.CompilerParams(dimension_semantics=sem, vmem_limit_bytes=VMEM_LIMIT)


def _rms(x, g):
    return x * lax.rsqrt(jnp.mean(x * x, axis=-1, keepdims=True) + RMS_EPS) * g


def _const_spec(shape):
    return pl.BlockSpec(shape, lambda *_: (0,) * len(shape))


def _inproj_body(x_ref, g_ref, w_ref, wt_ref, wf_ref, bf_ref, cos_ref, sin_ref, cost_ref, sint_ref,
                 qft_ref, kf_ref, vft_ref, qmt_ref, km_ref, vmt_ref, gates_ref, cb_ref, kmean_ref,
                 kabs_ref, bmax_ref,
                 carry_ref, *, tm, width, gate_cols):
    j = pl.program_id(1)
    h = _rms(x_ref[...], g_ref[...])
    hb = h.astype(_BF16)
    hbt = h.T.astype(_BF16)
    reps = width // LANES
    half = HEAD_DIM // 2

    def proj(c):
        return jnp.dot(hb, w_ref[:, c * width:(c + 1) * width], preferred_element_type=_F32)

    def proj_t(c):
        return jnp.dot(wt_ref[c * width:(c + 1) * width, :], hbt, preferred_element_type=_F32)

    def rope(t):
        cos = jnp.concatenate([cos_ref[...]] * reps, axis=1)
        sin = jnp.concatenate([sin_ref[...]] * reps, axis=1)
        first = (lax.broadcasted_iota(jnp.int32, (1, width), 1) % HEAD_DIM) < half
        partner = jnp.where(first, pltpu.roll(t, width - half, 1), pltpu.roll(t, half, 1))
        return t * cos + partner * sin

    def rope_t(t):
        cos = jnp.concatenate([cost_ref[...]] * reps, axis=0)
        sin = jnp.concatenate([sint_ref[...]] * reps, axis=0)
        first = (lax.broadcasted_iota(jnp.int32, (width, 1), 0) % HEAD_DIM) < half
        partner = jnp.where(first, pltpu.roll(t, width - half, 0), pltpu.roll(t, half, 0))
        return t * cos + partner * sin

    scale = HEAD_DIM ** -0.5 * LOG2_E
    qft_ref[...] = (proj_t(0) * scale).astype(_BF16)
    vft_ref[...] = proj_t(1).astype(_BF16)
    qmt_ref[...] = (rope_t(proj_t(2)) * scale).astype(_BF16)
    vmt_ref[...] = proj_t(3).astype(_BF16)
    kf = proj(0).astype(_BF16)
    kf_ref[...] = kf
    kf_abs = jnp.abs(kf.astype(_F32))
    km = rope(proj(1))
    km_ref[...] = km.astype(_BF16)

    @pl.when(j == 0)
    def _():
        kmean_ref[...] = jnp.zeros_like(kmean_ref)
        kabs_ref[...] = jnp.zeros_like(kabs_ref)
        bmax_ref[...] = jnp.zeros_like(bmax_ref)
        carry_ref[...] = jnp.zeros_like(carry_ref)

    blocks = tm // MOBA_BLOCK
    for b in range(blocks):
        rows = slice(b * MOBA_BLOCK, (b + 1) * MOBA_BLOCK)
        kmean_ref[pl.ds(j * blocks + b, 1), :] = jnp.mean(km[rows], axis=0, keepdims=True)
        kabs_ref[pl.ds(j * blocks + b, 1), :] = jnp.max(kf_abs[rows], axis=0, keepdims=True)

    for c in range(gate_cols // width):
        g = jnp.dot(hb, w_ref[:, (2 + c) * width:(3 + c) * width],
                    preferred_element_type=_F32)
        gates_ref[:, c * width:(c + 1) * width] = jax.nn.sigmoid(g).astype(_BF16)

    z = jnp.dot(hb, wf_ref[...], preferred_element_type=_F32) + bf_ref[...]
    log_f = jnp.minimum(z, 0.0) - jnp.log1p(jnp.exp(-jnp.abs(z)))
    r = lax.broadcasted_iota(jnp.int32, (LANES, LANES), 0)
    c = lax.broadcasted_iota(jnp.int32, (LANES, LANES), 1)
    tri = (c <= r).astype(_BF16)
    offset = carry_ref[0:1, :]
    blocks_cum = []
    for blk in range(tm // LANES):
        rest = log_f[blk * LANES:(blk + 1) * LANES]
        within = jnp.zeros((LANES, LANES), _F32)
        for _ in range(BIAS_PIECES):
            part = rest.astype(_BF16)
            rest = rest - part.astype(_F32)
            within = within + jnp.dot(tri, part, preferred_element_type=_F32)
        blocks_cum.append(within + offset)
        offset = offset + within[LANES - 1:LANES, :]
    cum = jnp.concatenate(blocks_cum, axis=0)
    carry_ref[...] = jnp.broadcast_to(offset, carry_ref.shape)

    head_lane = lax.broadcasted_iota(jnp.int32, (1, LANES), 1) < N_FOX_HEADS
    bias = jnp.where(head_lane, -cum * LOG2_E, 0.0)
    for b in range(blocks):
        bmax_ref[pl.ds(j * blocks + b, 1), :] = jnp.max(
            bias[b * MOBA_BLOCK:(b + 1) * MOBA_BLOCK], axis=0, keepdims=True)
    rest = bias
    placed = jnp.zeros((tm, LANES), _F32)
    for piece in range(BIAS_PIECES):
        part = rest.astype(_BF16).astype(_F32)
        rest = rest - part
        placed = placed + (pltpu.roll(part, N_FOX_HEADS * piece, 1) if piece else part)
    cb_ref[...] = placed.astype(_BF16)


def _inproj(x, g, w, wt, wf, bfp, cos, sin, cos_t, sin_t, *, tm):
    b, s, d = x.shape
    width = N_FOX_HEADS * HEAD_DIM
    gate_cols = w.shape[1] - 2 * width
    act = jax.ShapeDtypeStruct((b, s, width), _BF16)
    act_t = jax.ShapeDtypeStruct((b, width, s), _BF16)
    act_spec = pl.BlockSpec((None, tm, width), lambda bi, j: (bi, j, 0))
    act_t_spec = pl.BlockSpec((None, width, tm), lambda bi, j: (bi, 0, j))
    return pl.pallas_call(
        functools.partial(_inproj_body, tm=tm, width=width, gate_cols=gate_cols),
        grid=(b, s // tm),
        in_specs=[
            pl.BlockSpec((None, tm, d), lambda bi, j: (bi, j, 0)),
            _const_spec((1, d)),
            _const_spec(w.shape),
            _const_spec(wt.shape),
            _const_spec(wf.shape),
            _const_spec(bfp.shape),
            pl.BlockSpec((tm, LANES), lambda bi, j: (j, 0)),
            pl.BlockSpec((tm, LANES), lambda bi, j: (j, 0)),
            pl.BlockSpec((LANES, tm), lambda bi, j: (0, j)),
            pl.BlockSpec((LANES, tm), lambda bi, j: (0, j)),
        ],
        out_specs=[act_t_spec, act_spec, act_t_spec, act_t_spec, act_spec, act_t_spec,
                   pl.BlockSpec((None, tm, gate_cols), lambda bi, j: (bi, j, 0)),
                   pl.BlockSpec((None, tm, LANES), lambda bi, j: (bi, j, 0)),
                   pl.BlockSpec((None, LANES, width), lambda bi, j: (bi, 0, 0)),
                   pl.BlockSpec((None, LANES, width), lambda bi, j: (bi, 0, 0)),
                   pl.BlockSpec((None, LANES, LANES), lambda bi, j: (bi, 0, 0))],
        out_shape=[act_t, act, act_t, act_t, act, act_t,
                   jax.ShapeDtypeStruct((b, s, gate_cols), _BF16),
                   jax.ShapeDtypeStruct((b, s, LANES), _BF16),
                   jax.ShapeDtypeStruct((b, LANES, width), _F32),
                   jax.ShapeDtypeStruct((b, LANES, width), _F32),
                   jax.ShapeDtypeStruct((b, LANES, LANES), _F32)],
        scratch_shapes=[pltpu.VMEM((8, LANES), _F32)],
        compiler_params=_params("arbitrary", "arbitrary"),
        name="inproj",
    )(x, g, w, wt, wf, bfp, cos, sin, cos_t, sin_t)


def _attn_body(uq_ref, uk_ref, qt_ref, k_ref, kx_ref, vt_ref, aux_ref, aux2_ref, o_ref,
               w_ref, m_ref, l_ref, acc_ref, st_ref, p_ref, alpha_ref, pmax_ref, qn_ref, *,
               mode, tq, tk, nb, n_diag, n_units):
    hp = pl.program_id(1)
    wide = HEADS_PER_STEP * tq
    nq = w_ref.shape[0]
    feat = lax.broadcasted_iota(jnp.int32, (LANES, 1), 0)
    xr = lax.broadcasted_iota(jnp.int32, (LANES, wide), 0)
    xc = lax.broadcasted_iota(jnp.int32, (LANES, wide), 1)

    def setup(i, carry):
        qt = qt_ref[:, pl.ds(pl.multiple_of(i * tq, tq), tq)]
        zero = jnp.zeros_like(qt)
        wq = jnp.concatenate([jnp.where(feat < HEAD_DIM, qt, zero),
                              jnp.where(feat < HEAD_DIM, zero, qt)], axis=1)
        if mode == "fox":
            head = hp * HEADS_PER_STEP + xc // tq
            lower = ((xr % N_FOX_HEADS == head) & (xr < BIAS_PIECES * N_FOX_HEADS)).astype(_BF16)
            qn_ref[i] = jnp.broadcast_to(
                jnp.sum(jnp.abs(wq.astype(_F32)), axis=0, keepdims=True), qn_ref.shape[1:])
        else:
            own = (i * tq) // MOBA_BLOCK
            gate = jnp.dot(aux_ref[...].astype(_BF16), wq, preferred_element_type=_F32)[0:nb]
            blk = lax.broadcasted_iota(jnp.int32, (nb, wide), 0)
            gate = jnp.where(blk < own, gate, -jnp.inf)
            keep = blk == own
            for _ in range(MOBA_TOPK):
                mx = jnp.max(gate, axis=0, keepdims=True)
                cand = jnp.where((gate == mx) & (mx > -jnp.inf), blk, nb)
                pick = blk == jnp.min(cand, axis=0, keepdims=True)
                keep = keep | pick
                gate = jnp.where(pick, -jnp.inf, gate)
            lower = jnp.where(keep, 0.0, NEG_BIG)
            if nb < LANES:
                lower = jnp.concatenate([lower, jnp.zeros((LANES - nb, wide), _F32)], axis=0)
            lower = lower.astype(_BF16)
        w_ref[i] = jnp.concatenate([wq, lower], axis=0)
        m_ref[i] = jnp.full(m_ref.shape[1:], NEG_BIG, _F32)
        l_ref[i] = jnp.zeros(l_ref.shape[1:], _F32)
        acc_ref[i] = jnp.zeros(acc_ref.shape[1:], _F32)
        return carry

    lax.fori_loop(0, nq, setup, 0)

    def unit(t):
        return uq_ref[t], uk_ref[t]

    def scores(t, slot, causal):
        q, kb = unit(t)
        start = pl.multiple_of(kb * tk, tk)
        ka = jnp.concatenate([k_ref[pl.ds(start, tk), :], kx_ref[pl.ds(start, tk), :]], axis=1)
        st = jnp.dot(ka, w_ref[q], preferred_element_type=_F32)
        if causal:
            key = start + lax.broadcasted_iota(jnp.int32, (tk, wide), 0)
            qry = q * tq + lax.broadcasted_iota(jnp.int32, (tk, wide), 1) % tq
            st = jnp.where(key <= qry, st, NEG_BIG)
        st_ref[slot] = st
        pmax_ref[slot] = jnp.max(st.reshape(tk // 8, 8, wide), axis=0)

    def softmax(t, slot):
        q, _ = unit(t)
        rows = SOFTMAX_ROWS
        m_prev = m_ref[q][0:1, :]
        m_new = jnp.maximum(m_prev, jnp.max(pmax_ref[slot], axis=0, keepdims=True))
        for r in range(0, tk, rows):
            p_ref[slot, r:r + rows, :] = jnp.exp2(st_ref[slot, r:r + rows, :] - m_new).astype(_BF16)
        m_ref[q] = jnp.broadcast_to(m_new, m_ref.shape[1:])
        alpha_ref[slot] = jnp.broadcast_to(jnp.exp2(m_prev - m_new), alpha_ref.shape[1:])

    def values(t, slot):
        q, kb = unit(t)
        start = pl.multiple_of(kb * tk, tk)
        ones = jnp.ones((ONES_ROWS, tk), _BF16)
        for h in range(HEADS_PER_STEP):
            cols = slice(h * tq, (h + 1) * tq)
            feats = slice(h * HEAD_DIM, (h + 1) * HEAD_DIM)
            lhs = jnp.concatenate([vt_ref[feats, pl.ds(start, tk)], ones], axis=0)
            pv = jnp.dot(lhs, p_ref[slot, :, cols], preferred_element_type=_F32)
            alpha = alpha_ref[slot, 0:1, cols]
            acc_ref[q, feats, :] = acc_ref[q, feats, :] * alpha + pv[0:HEAD_DIM]
            l_ref[q, :, cols] = jnp.broadcast_to(
                alpha * l_ref[q, 0:1, cols] + pv[HEAD_DIM:HEAD_DIM + 1], (8, tq))

    def step(base, c, causal, do_scores=True, do_softmax=True, do_values=True):
        u = base + c
        if do_scores:
            scores(u, c % PIPE_UNITS, causal)
        if do_values:
            values(u - 2 * SOFTMAX_LAG, (c - 2 * SOFTMAX_LAG) % PIPE_UNITS)
        if do_softmax:
            softmax(u - SOFTMAX_LAG, (c - SOFTMAX_LAG) % PIPE_UNITS)

    def group(causal):
        def body(j, carry):
            for c in range(PIPE_UNITS):
                step(j * PIPE_UNITS, c, causal)
            return carry
        return body

    def pipeline(g_first, g_end, causal_groups):
        for c in range(PIPE_UNITS):
            step(g_first * PIPE_UNITS, c, g_first < causal_groups,
                 do_softmax=c >= SOFTMAX_LAG, do_values=c >= 2 * SOFTMAX_LAG)
        if g_first + 1 < causal_groups:
            lax.fori_loop(g_first + 1, causal_groups, group(True), 0)
        lax.fori_loop(max(g_first + 1, causal_groups), g_end, group(False), 0)
        for c in range(2 * SOFTMAX_LAG):
            step(g_end * PIPE_UNITS, c, False, do_scores=False, do_softmax=c < SOFTMAX_LAG)

    diag_groups = n_diag // PIPE_UNITS
    if mode == "moba":
        pipeline(0, n_units // PIPE_UNITS, diag_groups)
    else:
        pipeline(0, diag_groups, diag_groups)
        nq_blocks = n_diag
        col = lax.broadcasted_iota(jnp.int32, (LANES, LANES), 1)
        first_head = lax.broadcasted_iota(jnp.int32, (1, wide), 1) < tq
        head0 = hp * HEADS_PER_STEP

        def per_head(table, pick0, pick1, fill):
            v0 = jnp.max(jnp.where(pick0, table, fill), axis=1, keepdims=True)[0:nb]
            v1 = jnp.max(jnp.where(pick1, table, fill), axis=1, keepdims=True)[0:nb]
            return jnp.where(first_head, v0, v1)

        bias_max = per_head(aux2_ref[...], col == head0, col == head0 + 1, NEG_BIG)
        k_absmax = per_head(aux_ref[...], col < HEAD_DIM, col >= HEAD_DIM, 0.0)
        blk = lax.broadcasted_iota(jnp.int32, (nb, 1), 0)

        needed = jnp.zeros((nb, 1), jnp.int32)
        for i in range(1, nq_blocks):
            bound = bias_max + qn_ref[i, 0:1, :] * k_absmax - m_ref[i, 0:1, :]
            live = jnp.max(bound, axis=1, keepdims=True) >= -UNDERFLOW_LOG2
            needed = jnp.maximum(needed, jnp.where(live & (blk < i), i - blk, 0))
        far = jnp.max(needed)
        n_live = far * nq_blocks - (far * (far + 1)) // 2
        live_groups = (n_live + PIPE_UNITS - 1) // PIPE_UNITS

        @pl.when(live_groups > 0)
        def _():
            pipeline(diag_groups, diag_groups + live_groups, diag_groups)

    def finish(i, carry):
        out_t = jnp.concatenate(
            [acc_ref[i, h * HEAD_DIM:(h + 1) * HEAD_DIM, :] / l_ref[i, 0:1, h * tq:(h + 1) * tq]
             for h in range(HEADS_PER_STEP)], axis=0)
        o_ref[pl.ds(pl.multiple_of(i * tq, tq), tq), :] = out_t.T.astype(o_ref.dtype)
        return carry

    lax.fori_loop(0, nq, finish, 0, unroll=2)


def _attention(qt, k, kx, vt, aux, aux2, *, mode):
    b, width, s = qt.shape
    tq, tk = ATTN_Q, ATTN_K
    nq = s // tq
    nb = s // MOBA_BLOCK
    wide = HEADS_PER_STEP * tq
    assert tq == tk
    units = [(i, (i * tq) // tk) for i in range(nq)]
    units += [(i, i - dist) for dist in range(1, nq) for i in range(dist, nq)]
    assert nq % PIPE_UNITS == 0 and len(units) % PIPE_UNITS == 0 and PIPE_UNITS >= 2 * SOFTMAX_LAG
    unit_q = jnp.asarray([u[0] for u in units], jnp.int32)
    unit_k = jnp.asarray([u[1] for u in units], jnp.int32)
    aux_spec = pl.BlockSpec((None, LANES, LANES), lambda bi, hp, uq, uk: (bi, 0, hp))
    aux2_spec = pl.BlockSpec((None, LANES, LANES), lambda bi, hp, uq, uk: (bi, 0, 0))
    if mode == "fox":
        kx_spec = pl.BlockSpec((None, s, LANES), lambda bi, hp, uq, uk: (bi, 0, 0))
    else:
        kx_spec = pl.BlockSpec(kx.shape, lambda bi, hp, uq, uk: (0, 0))
    return pl.pallas_call(
        functools.partial(_attn_body, mode=mode, tq=tq, tk=tk, nb=nb, n_diag=nq, n_units=len(units)),
        grid_spec=pltpu.PrefetchScalarGridSpec(
            num_scalar_prefetch=2,
            grid=(b, width // LANES),
            in_specs=[
                pl.BlockSpec((None, LANES, s), lambda bi, hp, uq, uk: (bi, hp, 0)),
                pl.BlockSpec((None, s, LANES), lambda bi, hp, uq, uk: (bi, 0, hp)),
                kx_spec,
                pl.BlockSpec((None, LANES, s), lambda bi, hp, uq, uk: (bi, hp, 0)),
                aux_spec,
                aux2_spec,
            ],
            out_specs=pl.BlockSpec((None, s, LANES), lambda bi, hp, uq, uk: (bi, 0, hp)),
            scratch_shapes=[
                pltpu.VMEM((nq, 2 * LANES, wide), _BF16),
                pltpu.VMEM((nq, 8, wide), _F32),
                pltpu.VMEM((nq, 8, wide), _F32),
                pltpu.VMEM((nq, LANES, tq), _F32),
                pltpu.VMEM((PIPE_UNITS, tk, wide), _F32),
                pltpu.VMEM((PIPE_UNITS, tk, wide), _BF16),
                pltpu.VMEM((PIPE_UNITS, 8, wide), _F32),
                pltpu.VMEM((PIPE_UNITS, 8, wide), _F32),
                pltpu.VMEM((nq, 8, wide), _F32),
            ],
        ),
        out_shape=jax.ShapeDtypeStruct((b, s, width), _BF16),
        compiler_params=_params("arbitrary", "arbitrary"),
        name=mode,
    )(unit_q, unit_k, qt, k, kx, vt, aux, aux2)


def _pack_bf16_pair(a, b):
    lo = pltpu.bitcast(a.astype(_BF16).astype(_F32), jnp.uint32) >> 16
    hi = pltpu.bitcast(b.astype(_BF16).astype(_F32), jnp.uint32) & jnp.uint32(0xFFFF0000)
    return lo | hi


def _unpack_bf16_pair(u):
    lo = pltpu.bitcast(u << 16, _F32)
    hi = pltpu.bitcast(u & jnp.uint32(0xFFFF0000), _F32)
    return jnp.concatenate([lo, hi], axis=1)


def _postattn_body(yf_ref, ym_ref, gates_ref, x_ref, wfb_ref, wmb_ref, wout_ref, g_ref, wr_ref, br_ref,
                   x1_ref, h2_ref, route_ref, meta_ref, counts_ref, carry_ref, *, tm, d):
    step = pl.program_id(0)
    ya = jnp.dot(yf_ref[...], wfb_ref[...], preferred_element_type=_F32)
    yb = jnp.dot(ym_ref[...], wmb_ref[...], preferred_element_type=_F32)
    mixed = gates_ref[:, 0:d].astype(_F32) * ya + gates_ref[:, d:2 * d].astype(_F32) * yb
    x1 = x_ref[...] + jnp.dot(mixed.astype(_BF16), wout_ref[...], preferred_element_type=_F32)
    x1_ref[...] = x1
    h2 = _rms(x1, g_ref[...])
    h2_ref[...] = _pack_bf16_pair(h2[:, 0:d // 2], h2[:, d // 2:d])

    logits = jnp.dot(h2.astype(_BF16), wr_ref[...], preferred_element_type=_F32) + br_ref[...]
    logits_t = logits.T
    gl = logits_t[N_EXPERTS:N_EXPERTS + N_GROUPS]
    ge = jnp.exp(gl - jnp.max(gl, axis=0, keepdims=True))
    gp = ge / jnp.sum(ge, axis=0, keepdims=True)
    g_top = jnp.max(gp, axis=0, keepdims=True)
    g_row = lax.broadcasted_iota(jnp.int32, (N_GROUPS, tm), 0)
    g_idx = jnp.min(jnp.where(gp == g_top, g_row, N_GROUPS), axis=0, keepdims=True)
    fl = logits_t[0:EXPERTS_PER_GROUP]
    for g in range(1, N_GROUPS):
        fl = jnp.where(g_idx == g, logits_t[g * EXPERTS_PER_GROUP:(g + 1) * EXPERTS_PER_GROUP], fl)
    f_row = lax.broadcasted_iota(jnp.int32, (EXPERTS_PER_GROUP, tm), 0)
    f1 = jnp.max(fl, axis=0, keepdims=True)
    i1 = jnp.min(jnp.where(fl == f1, f_row, EXPERTS_PER_GROUP), axis=0, keepdims=True)
    fl2 = jnp.where(f_row == i1, -jnp.inf, fl)
    f2 = jnp.max(fl2, axis=0, keepdims=True)
    i2 = jnp.min(jnp.where(fl2 == f2, f_row, EXPERTS_PER_GROUP), axis=0, keepdims=True)
    e1 = g_idx * EXPERTS_PER_GROUP + i1
    e2 = g_idx * EXPERTS_PER_GROUP + i2
    t2 = jnp.exp(f2 - f1)
    w1 = g_top * (1.0 / (1.0 + t2))
    w2 = g_top * (t2 / (1.0 + t2))

    @pl.when(step == 0)
    def _():
        carry_ref[...] = jnp.zeros_like(carry_ref)

    x_row = lax.broadcasted_iota(jnp.int32, (N_EXPERTS, tm), 0)
    onehot = ((x_row == e1) | (x_row == e2)).astype(_F32)
    rr = lax.broadcasted_iota(jnp.int32, (tm, tm), 0)
    cc = lax.broadcasted_iota(jnp.int32, (tm, tm), 1)
    before = jnp.dot(onehot.astype(_BF16), (rr < cc).astype(_BF16), preferred_element_type=_F32)
    before = before + carry_ref[0:N_EXPERTS, 0:1]
    r1 = jnp.sum(jnp.where(x_row == e1, before, 0.0), axis=0, keepdims=True)
    r2 = jnp.sum(jnp.where(x_row == e2, before, 0.0), axis=0, keepdims=True)
    total = carry_ref[0:N_EXPERTS, 0:1] + jnp.sum(onehot, axis=1, keepdims=True)
    carry_ref[0:N_EXPERTS, :] = jnp.broadcast_to(total, (N_EXPERTS, LANES))
    counts_ref[...] = carry_ref[...]

    m_row = lax.broadcasted_iota(jnp.int32, (8, tm), 0)
    meta = jnp.where(m_row == 2, e1, 0)
    meta = jnp.where(m_row == 3, e2, meta)
    meta = jnp.where(m_row == 4, r1.astype(jnp.int32), meta)
    meta = jnp.where(m_row == 5, r2.astype(jnp.int32), meta)
    meta_ref[...] = meta
    w_row = lax.broadcasted_iota(jnp.int32, (LANES, tm), 0)
    route_ref[...] = jnp.where(w_row == 0, w1, jnp.where(w_row == 1, w2, 0.0)).T


def _postattn(yf, ym, gates, x, wfb, wmb, wout, g, wr, br, *, tm):
    t, d = x.shape
    width = yf.shape[1]
    row = lambda cols: pl.BlockSpec((tm, cols), lambda i: (i, 0))
    return pl.pallas_call(
        functools.partial(_postattn_body, tm=tm, d=d),
        grid=(t // tm,),
        in_specs=[row(width), row(width), row(2 * d), row(d),
                  _const_spec(wfb.shape), _const_spec(wmb.shape), _const_spec(wout.shape),
                  _const_spec((1, d)), _const_spec(wr.shape), _const_spec(br.shape)],
        out_specs=[row(d), row(d // 2), row(LANES), pl.BlockSpec((8, tm), lambda i: (0, i)),
                   _const_spec((LANES, LANES))],
        out_shape=[jax.ShapeDtypeStruct((t, d), _F32),
                   jax.ShapeDtypeStruct((t, d // 2), jnp.uint32),
                   jax.ShapeDtypeStruct((t, LANES), _F32),
                   jax.ShapeDtypeStruct((8, t), jnp.int32),
                   jax.ShapeDtypeStruct((LANES, LANES), _F32)],
        scratch_shapes=[pltpu.VMEM((LANES, LANES), _F32)],
        compiler_params=_params("arbitrary"),
        name="postattn",
    )(yf, ym, gates, x, wfb, wmb, wout, g, wr, br)


def _sc_scatter(rows, idx, n_out):
    n, width = rows.shape
    workers = SC_CORES * SC_SUBCORES
    per_worker = n // workers
    chunks = per_worker // SC_SCATTER_CHUNK
    assert n % (workers * SC_SCATTER_CHUNK) == 0 and idx.shape == (2, n // SC_SCATTER_CHUNK, SC_SCATTER_CHUNK)
    mesh = plsc.VectorSubcoreMesh(core_axis_name="c", subcore_axis_name="s",
                                  num_cores=SC_CORES, num_subcores=SC_SUBCORES)

    @functools.partial(
        pl.kernel, out_type=jax.ShapeDtypeStruct((n_out, width), rows.dtype), mesh=mesh,
        scratch_types=[pltpu.VMEM((2, chunks, SC_SCATTER_CHUNK), jnp.int32),
                       pltpu.VMEM((SC_SCATTER_CHUNK, width), rows.dtype),
                       pltpu.SemaphoreType.DMA],
        name="sc_scatter")
    def scatter(rows_hbm, idx_hbm, out_hbm, idx_v, rows_v, sem):
        worker = lax.axis_index("s") * SC_CORES + lax.axis_index("c")
        for k in range(2):
            pltpu.sync_copy(idx_hbm.at[k, pl.ds(worker * chunks, chunks)], idx_v.at[k])

        def chunk(j, carry):
            start = worker * per_worker + j * SC_SCATTER_CHUNK
            pltpu.sync_copy(rows_hbm.at[pl.ds(start, SC_SCATTER_CHUNK)], rows_v)
            for k in range(2):
                pltpu.async_copy(rows_v, out_hbm.at[idx_v.at[k, j]], sem).wait()
            return carry

        lax.fori_loop(0, chunks, chunk, 0)

    return scatter(rows, idx)


def _experts_body(be_ref, valid_ref, nused_ref, xs_ref, wg_ref, wu_ref, wd_ref, out_ref,
                  wgb_ref, wub_ref, wdb_ref):
    j = pl.program_id(0)
    used = j < nused_ref[0]

    @pl.when(jnp.logical_not(used))
    def _():
        out_ref[...] = jnp.zeros_like(out_ref)

    @pl.when(used & ((j == 0) | (be_ref[j] != be_ref[jnp.maximum(j - 1, 0)])))
    def _():
        wgb_ref[...] = wg_ref[...].astype(_BF16)
        wub_ref[...] = wu_ref[...].astype(_BF16)
        wdb_ref[...] = wd_ref[...].astype(_BF16)

    @pl.when(used)
    def _():
        row = lax.broadcasted_iota(jnp.int32, xs_ref.shape, 0)
        packed = jnp.where(row < valid_ref[j], xs_ref[...], jnp.uint32(0))
        rows = _unpack_bf16_pair(packed).astype(_BF16)
        hidden = wgb_ref.shape[1]
        out = None
        for c in range(0, hidden, hidden // EXPERT_PARTS):
            cols = slice(c, c + hidden // EXPERT_PARTS)
            gate = jnp.dot(rows, wgb_ref[:, cols], preferred_element_type=_F32)
            up = jnp.dot(rows, wub_ref[:, cols], preferred_element_type=_F32)
            act = ((gate * jax.nn.sigmoid(gate)) * up).astype(_BF16)
            part = jnp.dot(act, wdb_ref[cols, :], preferred_element_type=_F32)
            out = part if out is None else out + part
        half = out.shape[1] // 2
        out_ref[...] = _pack_bf16_pair(out[:, 0:half], out[:, half:])


def _experts(block_expert, block_valid, n_used, xs, wg, wu, wd, *, te):
    n_rows, half = xs.shape
    _, d, de = wg.shape

    def blk(j, be, bv, nu):
        return jnp.minimum(j, nu[0] - 1)

    def weight(j, be, bv, nu):
        return (be[blk(j, be, bv, nu)], 0, 0)

    return pl.pallas_call(
        _experts_body,
        grid_spec=pltpu.PrefetchScalarGridSpec(
            num_scalar_prefetch=3,
            grid=(n_rows // te,),
            in_specs=[pl.BlockSpec((te, half), lambda j, be, bv, nu: (blk(j, be, bv, nu), 0)),
                      pl.BlockSpec((None, d, de), weight),
                      pl.BlockSpec((None, d, de), weight),
                      pl.BlockSpec((None, de, d), weight)],
            out_specs=pl.BlockSpec((te, d // 2), lambda j, be, bv, nu: (j, 0)),
            scratch_shapes=[pltpu.VMEM((d, de), _BF16), pltpu.VMEM((d, de), _BF16),
                            pltpu.VMEM((de, d), _BF16)],
        ),
        out_shape=jax.ShapeDtypeStruct((n_rows, d // 2), jnp.uint32),
        compiler_params=_params("arbitrary"),
        name="experts",
    )(block_expert, block_valid, n_used, xs, wg, wu, wd)


def _sc_gather(table, idx):
    n, = idx.shape
    width = table.shape[1]
    workers = SC_CORES * SC_SUBCORES
    per_worker = n // workers
    chunks = per_worker // SC_CHUNK
    assert n % (workers * SC_CHUNK) == 0
    mesh = plsc.VectorSubcoreMesh(core_axis_name="c", subcore_axis_name="s",
                                  num_cores=SC_CORES, num_subcores=SC_SUBCORES)

    @functools.partial(
        pl.kernel, out_type=jax.ShapeDtypeStruct((n, width), table.dtype), mesh=mesh,
        scratch_types=[pltpu.VMEM((per_worker,), jnp.int32),
                       pltpu.VMEM((SC_CHUNK, width), table.dtype),
                       pltpu.SemaphoreType.DMA],
        name="sc_gather")
    def gather(table_hbm, idx_hbm, out_hbm, idx_v, rows_v, sem):
        worker = lax.axis_index("s") * SC_CORES + lax.axis_index("c")
        base = worker * per_worker
        pltpu.sync_copy(idx_hbm.at[pl.ds(base, per_worker)], idx_v)

        def chunk(j, carry):
            off = pl.multiple_of(j * SC_CHUNK, SC_CHUNK)
            pltpu.async_copy(table_hbm.at[idx_v.at[pl.ds(off, SC_CHUNK)]], rows_v, sem).wait()
            pltpu.sync_copy(rows_v, out_hbm.at[pl.ds(base + off, SC_CHUNK)])
            return carry

        lax.fori_loop(0, chunks, chunk, 0)

    return gather(table, idx)


def _final_body(x1_ref, route_ref, p_ref, r1_ref, r2_ref, gp_ref, wpg_ref, wpp_ref, gf_ref, o_ref):
    emb = jnp.dot(p_ref[...].astype(_BF16), wpp_ref[...], preferred_element_type=_F32)
    route = route_ref[...]
    x2 = x1_ref[...] + (_unpack_bf16_pair(r1_ref[...]) * route[:, 0:1]
                        + _unpack_bf16_pair(r2_ref[...]) * route[:, 1:2])
    gate = jax.nn.sigmoid(jnp.dot(_rms(x2, gp_ref[...]).astype(_BF16), wpg_ref[...],
                                  preferred_element_type=_F32))
    o_ref[...] = _rms(x2 + gate * emb, gf_ref[...])


def _final(x1, route, p, slot_rows, gp, wpg, wpp, gf, *, tm):
    t, d = x1.shape
    ple = p.shape[1]
    steps = t // tm
    row = lambda cols: pl.BlockSpec((tm, cols), lambda i: (i, 0))
    return pl.pallas_call(
        _final_body,
        grid=(steps,),
        in_specs=[row(d), row(LANES), row(ple), row(d // 2),
                  pl.BlockSpec((tm, d // 2), lambda i: (steps + i, 0)),
                  _const_spec((1, d)), _const_spec(wpg.shape), _const_spec(wpp.shape),
                  _const_spec((1, d))],
        out_specs=row(d),
        out_shape=jax.ShapeDtypeStruct((t, d), _F32),
        compiler_params=_params("arbitrary"),
        name="final",
    )(x1, route, p, slot_rows, slot_rows, gp, wpg, wpp, gf)


def _rope_tables(s):
    half = HEAD_DIM // 2
    inv = 1.0 / (ROPE_THETA ** (jnp.arange(0, HEAD_DIM, 2, dtype=_F32) / HEAD_DIM))
    ang = jnp.arange(s, dtype=_F32)[:, None] * inv[None, :]
    cos, sin = jnp.cos(ang), jnp.sin(ang)
    reps = LANES // half
    cos_t = jnp.tile(cos, (1, reps))
    sin_t = jnp.tile(jnp.concatenate([-sin, sin], axis=1), (1, reps // 2))
    return cos_t, sin_t


def _layer(x, p, attn_norm, w_in, b_forget, w_fox_branch, w_moba_branch, w_out, moe_norm,
           w_group, b_group, w_fine, b_fine, w_gate, w_up, w_down, ple_norm, w_ple_gate, w_ple_proj,
           final_gain):
    b, s, d = x.shape
    t = b * s
    width = N_FOX_HEADS * HEAD_DIM
    assert N_MOBA_HEADS * HEAD_DIM == width and MOBA_BLOCK % ATTN_Q == 0 and ATTN_K % MOBA_BLOCK == 0
    assert BIAS_PIECES * N_FOX_HEADS <= LANES and s % ATTN_K == 0
    assert s % ROW_TILE == 0 and ROW_TILE % MOBA_BLOCK == 0 and s // MOBA_BLOCK <= LANES
    assert t % FINAL_TILE == 0 and N_EXPERTS + N_GROUPS <= LANES

    qkv_cols = 6 * width
    f_cols = N_FOX_HEADS
    chunk = lambda c: w_in[:, c * width:(c + 1) * width]
    w_rows = jnp.concatenate([chunk(1), chunk(4), w_in[:, qkv_cols + f_cols:]], axis=1).astype(_BF16)
    w_cols = jnp.concatenate([chunk(0), chunk(2), chunk(3), chunk(5)], axis=1).T.astype(_BF16)
    wf = jnp.zeros((d, LANES), _BF16).at[:, :f_cols].set(w_in[:, qkv_cols:qkv_cols + f_cols].astype(_BF16))
    bfp = jnp.zeros((1, LANES), _F32).at[0, :f_cols].set(b_forget.astype(_F32))
    cos, sin = _rope_tables(s)

    qft, kf, vft, qmt, km, vmt, gates, cb, kmean, kabs, bmax = _inproj(
        x, attn_norm.reshape(1, d), w_rows, w_cols, wf, bfp, cos, sin, cos.T, sin.T, tm=ROW_TILE)
    y_fox = _attention(qft, kf, cb, vft, kabs, bmax, mode="fox")
    block_of_key = jnp.arange(s, dtype=jnp.int32)[:, None] // MOBA_BLOCK
    block_onehot = (block_of_key == jnp.arange(LANES, dtype=jnp.int32)[None, :]).astype(_BF16)
    y_moba = _attention(qmt, km, block_onehot, vmt, kmean, bmax, mode="moba")

    wr = jnp.zeros((d, LANES), _BF16)
    wr = wr.at[:, :N_EXPERTS].set(w_fine.astype(_BF16))
    wr = wr.at[:, N_EXPERTS:N_EXPERTS + N_GROUPS].set(w_group.astype(_BF16))
    br = jnp.zeros((1, LANES), _F32)
    br = br.at[0, :N_EXPERTS].set(b_fine.astype(_F32))
    br = br.at[0, N_EXPERTS:N_EXPERTS + N_GROUPS].set(b_group.astype(_F32))
    x1, h2p, route, meta, counts = _postattn(
        y_fox.reshape(t, width), y_moba.reshape(t, width), gates.reshape(t, 2 * d), x.reshape(t, d),
        w_fox_branch.astype(_BF16), w_moba_branch.astype(_BF16), w_out.astype(_BF16),
        moe_norm.reshape(1, d), wr, br, tm=ROW_TILE)

    te = EXPERT_TILE
    n_blk = (2 * t) // te + N_EXPERTS
    counts = counts[:N_EXPERTS, 0].astype(jnp.int32)
    blocks_per = (counts + te - 1) // te
    block_end = jnp.cumsum(blocks_per)
    row_start = (block_end - blocks_per) * te
    expert_ids = jnp.arange(N_EXPERTS, dtype=jnp.int32)[:, None, None]
    slot_major = meta[4:6] + jnp.sum(
        jnp.where(meta[2:4][None] == expert_ids, row_start[:, None, None], 0), axis=0)
    n_used = block_end[-1:].astype(jnp.int32)
    block_ids = jnp.arange(n_blk, dtype=jnp.int32)
    block_expert = jnp.minimum(
        jnp.sum((block_ids[:, None] >= block_end[None, :]).astype(jnp.int32), axis=1), N_EXPERTS - 1)

    first_block = block_end - blocks_per
    block_valid = jnp.clip(counts[block_expert] - (block_ids - first_block[block_expert]) * te, 0, te)
    xs = _sc_scatter(h2p, slot_major.reshape(2, t // SC_SCATTER_CHUNK, SC_SCATTER_CHUNK), n_blk * te)
    rows = _experts(block_expert, block_valid, n_used, xs, w_gate, w_up, w_down, te=te)
    slot_rows = _sc_gather(rows, slot_major.reshape(-1))
    out = _final(x1, route, p.reshape(t, -1), slot_rows, ple_norm.reshape(1, d),
                 w_ple_gate.astype(_BF16), w_ple_proj.astype(_BF16), final_gain.reshape(1, d),
                 tm=FINAL_TILE)
    return out.reshape(b, s, d)


def kernel(x, p, attn_norm, w_in, b_forget, w_fox_branch, w_moba_branch, w_out, moe_norm, w_group,
           b_group, w_fine, b_fine, w_gate, w_up, w_down, ple_norm, w_ple_gate, w_ple_proj, final_norm):
    depth = p.shape[0]
    assert depth == 1, "the final norm is fused into the last layer's kernel"
    i = 0
    return _layer(x, p[i], attn_norm[i], w_in[i], b_forget[i], w_fox_branch[i], w_moba_branch[i],
                  w_out[i], moe_norm[i], w_group[i], b_group[i], w_fine[i], b_fine[i], w_gate[i],
                  w_up[i], w_down[i], ple_norm[i], w_ple_gate[i], w_ple_proj[i], final_norm)
```

```python
import functools

import jax
import jax.numpy as jnp
from jax import lax
from jax.experimental import pallas as pl
from jax.experimental.pallas import tpu as pltpu
from jax.experimental.pallas import tpu_sc as plsc

HEAD_DIM = 64
N_FOX_HEADS = 8
N_MOBA_HEADS = 8
MOBA_BLOCK = 256
MOBA_TOPK = 3
ROPE_THETA = 10000.0
N_GROUPS = 4
EXPERTS_PER_GROUP = 8
N_EXPERTS = N_GROUPS * EXPERTS_PER_GROUP
RMS_EPS = 1e-6

LANES = 128
HEADS_PER_STEP = LANES // HEAD_DIM
ATTN_Q = 256
FOX_PIPELINE = (256, 8, 4)
MOBA_PIPELINE = (512, 4, 2)
ONES_ROWS = 16
SOFTMAX_ROWS = 64
BIAS_PIECES = 3
ROW_TILE = 512
SC_CORES = 2
SC_SUBCORES = 16
SC_CHUNK = 64
SC_SCATTER_CHUNK = 128
FINAL_TILE = 256
EXPERT_PARTS = 2
EXPERT_TILE = 512
LOG2_E = 1.4426950408889634
UNDERFLOW_LOG2 = 160.0
NEG_BIG = -1e30
VMEM_LIMIT = 48 * 1024 * 1024

_BF16 = jnp.bfloat16
_F32 = jnp.float32


def _params(*sem):
    return pltpu.CompilerParams(dimension_semantics=sem, vmem_limit_bytes=VMEM_LIMIT)


def _rms(x, g):
    return x * lax.rsqrt(jnp.mean(x * x, axis=-1, keepdims=True) + RMS_EPS) * g


def _const_spec(shape):
    return pl.BlockSpec(shape, lambda *_: (0,) * len(shape))


def _inproj_body(x_ref, g_ref, w_ref, wt_ref, wf_ref, bf_ref, cos_ref, sin_ref, cost_ref, sint_ref,
                 qft_ref, kf_ref, vft_ref, qmt_ref, km_ref, vmt_ref, gates_ref, cb_ref, kmean_ref,
                 kabs_ref, bmax_ref,
                 carry_ref, *, tm, width, gate_cols):
    j = pl.program_id(1)
    h = _rms(x_ref[...], g_ref[...])
    hb = h.astype(_BF16)
    hbt = h.T.astype(_BF16)
    reps = width // LANES
    half = HEAD_DIM // 2

    def proj(c):
        return jnp.dot(hb, w_ref[:, c * width:(c + 1) * width], preferred_element_type=_F32)

    def proj_t(c):
        return jnp.dot(wt_ref[c * width:(c + 1) * width, :], hbt, preferred_element_type=_F32)

    def rope(t):
        cos = jnp.concatenate([cos_ref[...]] * reps, axis=1)
        sin = jnp.concatenate([sin_ref[...]] * reps, axis=1)
        first = (lax.broadcasted_iota(jnp.int32, (1, width), 1) % HEAD_DIM) < half
        partner = jnp.where(first, pltpu.roll(t, width - half, 1), pltpu.roll(t, half, 1))
        return t * cos + partner * sin

    def rope_t(t):
        cos = jnp.concatenate([cost_ref[...]] * reps, axis=0)
        sin = jnp.concatenate([sint_ref[...]] * reps, axis=0)
        first = (lax.broadcasted_iota(jnp.int32, (width, 1), 0) % HEAD_DIM) < half
        partner = jnp.where(first, pltpu.roll(t, width - half, 0), pltpu.roll(t, half, 0))
        return t * cos + partner * sin

    scale = HEAD_DIM ** -0.5 * LOG2_E
    qft_ref[...] = (proj_t(0) * scale).astype(_BF16)
    vft_ref[...] = proj_t(1).astype(_BF16)
    qmt_ref[...] = (rope_t(proj_t(2)) * scale).astype(_BF16)
    vmt_ref[...] = proj_t(3).astype(_BF16)
    kf = proj(0).astype(_BF16)
    kf_ref[...] = kf
    kf_abs = jnp.abs(kf.astype(_F32))
    km = rope(proj(1))
    km_ref[...] = km.astype(_BF16)

    @pl.when(j == 0)
    def _():
        kmean_ref[...] = jnp.zeros_like(kmean_ref)
        kabs_ref[...] = jnp.zeros_like(kabs_ref)
        bmax_ref[...] = jnp.zeros_like(bmax_ref)
        carry_ref[...] = jnp.zeros_like(carry_ref)

    blocks = tm // MOBA_BLOCK
    for b in range(blocks):
        rows = slice(b * MOBA_BLOCK, (b + 1) * MOBA_BLOCK)
        kmean_ref[pl.ds(j * blocks + b, 1), :] = jnp.mean(km[rows], axis=0, keepdims=True)
        kabs_ref[pl.ds(j * blocks + b, 1), :] = jnp.max(kf_abs[rows], axis=0, keepdims=True)

    for c in range(gate_cols // width):
        g = jnp.dot(hb, w_ref[:, (2 + c) * width:(3 + c) * width],
                    preferred_element_type=_F32)
        gates_ref[:, c * width:(c + 1) * width] = jax.nn.sigmoid(g).astype(_BF16)

    z = jnp.dot(hb, wf_ref[...], preferred_element_type=_F32) + bf_ref[...]
    log_f = jnp.minimum(z, 0.0) - jnp.log1p(jnp.exp(-jnp.abs(z)))
    r = lax.broadcasted_iota(jnp.int32, (LANES, LANES), 0)
    c = lax.broadcasted_iota(jnp.int32, (LANES, LANES), 1)
    tri = (c <= r).astype(_BF16)
    offset = carry_ref[0:1, :]
    blocks_cum = []
    for blk in range(tm // LANES):
        rest = log_f[blk * LANES:(blk + 1) * LANES]
        within = jnp.zeros((LANES, LANES), _F32)
        for _ in range(BIAS_PIECES):
            part = rest.astype(_BF16)
            rest = rest - part.astype(_F32)
            within = within + jnp.dot(tri, part, preferred_element_type=_F32)
        blocks_cum.append(within + offset)
        offset = offset + within[LANES - 1:LANES, :]
    cum = jnp.concatenate(blocks_cum, axis=0)
    carry_ref[...] = jnp.broadcast_to(offset, carry_ref.shape)

    head_lane = lax.broadcasted_iota(jnp.int32, (1, LANES), 1) < N_FOX_HEADS
    bias = jnp.where(head_lane, -cum * LOG2_E, 0.0)
    for b in range(blocks):
        bmax_ref[pl.ds(j * blocks + b, 1), :] = jnp.max(
            bias[b * MOBA_BLOCK:(b + 1) * MOBA_BLOCK], axis=0, keepdims=True)
    rest = bias
    placed = jnp.zeros((tm, LANES), _F32)
    for piece in range(BIAS_PIECES):
        part = rest.astype(_BF16).astype(_F32)
        rest = rest - part
        placed = placed + (pltpu.roll(part, N_FOX_HEADS * piece, 1) if piece else part)
    cb_ref[...] = placed.astype(_BF16)


def _inproj(x, g, w, wt, wf, bfp, cos, sin, cos_t, sin_t, *, tm):
    b, s, d = x.shape
    width = N_FOX_HEADS * HEAD_DIM
    gate_cols = w.shape[1] - 2 * width
    act = jax.ShapeDtypeStruct((b, s, width), _BF16)
    act_t = jax.ShapeDtypeStruct((b, width, s), _BF16)
    act_spec = pl.BlockSpec((None, tm, width), lambda bi, j: (bi, j, 0))
    act_t_spec = pl.BlockSpec((None, width, tm), lambda bi, j: (bi, 0, j))
    return pl.pallas_call(
        functools.partial(_inproj_body, tm=tm, width=width, gate_cols=gate_cols),
        grid=(b, s // tm),
        in_specs=[
            pl.BlockSpec((None, tm, d), lambda bi, j: (bi, j, 0)),
            _const_spec((1, d)),
            _const_spec(w.shape),
            _const_spec(wt.shape),
            _const_spec(wf.shape),
            _const_spec(bfp.shape),
            pl.BlockSpec((tm, LANES), lambda bi, j: (j, 0)),
            pl.BlockSpec((tm, LANES), lambda bi, j: (j, 0)),
            pl.BlockSpec((LANES, tm), lambda bi, j: (0, j)),
            pl.BlockSpec((LANES, tm), lambda bi, j: (0, j)),
        ],
        out_specs=[act_t_spec, act_spec, act_t_spec, act_t_spec, act_spec, act_t_spec,
                   pl.BlockSpec((None, tm, gate_cols), lambda bi, j: (bi, j, 0)),
                   pl.BlockSpec((None, tm, LANES), lambda bi, j: (bi, j, 0)),
                   pl.BlockSpec((None, LANES, width), lambda bi, j: (bi, 0, 0)),
                   pl.BlockSpec((None, LANES, width), lambda bi, j: (bi, 0, 0)),
                   pl.BlockSpec((None, LANES, LANES), lambda bi, j: (bi, 0, 0))],
        out_shape=[act_t, act, act_t, act_t, act, act_t,
                   jax.ShapeDtypeStruct((b, s, gate_cols), _BF16),
                   jax.ShapeDtypeStruct((b, s, LANES), _BF16),
                   jax.ShapeDtypeStruct((b, LANES, width), _F32),
                   jax.ShapeDtypeStruct((b, LANES, width), _F32),
                   jax.ShapeDtypeStruct((b, LANES, LANES), _F32)],
        scratch_shapes=[pltpu.VMEM((8, LANES), _F32)],
        compiler_params=_params("arbitrary", "arbitrary"),
        name="inproj",
    )(x, g, w, wt, wf, bfp, cos, sin, cos_t, sin_t)


def _attn_body(uq_ref, uk_ref, qt_ref, k_ref, kx_ref, vt_ref, aux_ref, aux2_ref, o_ref,
               w_ref, m_ref, l_ref, acc_ref, st_ref, p_ref, alpha_ref, pmax_ref, qn_ref, *,
               mode, tq, tk, nb, n_diag, n_units, pipe, lag):
    hp = pl.program_id(1)
    wide = HEADS_PER_STEP * tq
    nq = w_ref.shape[0]
    feat = lax.broadcasted_iota(jnp.int32, (LANES, 1), 0)
    xr = lax.broadcasted_iota(jnp.int32, (LANES, wide), 0)
    xc = lax.broadcasted_iota(jnp.int32, (LANES, wide), 1)

    def setup(i, carry):
        qt = qt_ref[:, pl.ds(pl.multiple_of(i * tq, tq), tq)]
        zero = jnp.zeros_like(qt)
        wq = jnp.concatenate([jnp.where(feat < HEAD_DIM, qt, zero),
                              jnp.where(feat < HEAD_DIM, zero, qt)], axis=1)
        if mode == "fox":
            head = hp * HEADS_PER_STEP + xc // tq
            lower = ((xr % N_FOX_HEADS == head) & (xr < BIAS_PIECES * N_FOX_HEADS)).astype(_BF16)
            qn_ref[i] = jnp.broadcast_to(
                jnp.sum(jnp.abs(wq.astype(_F32)), axis=0, keepdims=True), qn_ref.shape[1:])
        else:
            own = (i * tq) // MOBA_BLOCK
            gate = jnp.dot(aux_ref[...].astype(_BF16), wq, preferred_element_type=_F32)[0:nb]
            blk = lax.broadcasted_iota(jnp.int32, (nb, wide), 0)
            gate = jnp.where(blk < own, gate, -jnp.inf)
            keep = blk == own
            for _ in range(MOBA_TOPK):
                mx = jnp.max(gate, axis=0, keepdims=True)
                cand = jnp.where((gate == mx) & (mx > -jnp.inf), blk, nb)
                pick = blk == jnp.min(cand, axis=0, keepdims=True)
                keep = keep | pick
                gate = jnp.where(pick, -jnp.inf, gate)
            lower = jnp.where(keep, 0.0, NEG_BIG)
            if nb < LANES:
                lower = jnp.concatenate([lower, jnp.zeros((LANES - nb, wide), _F32)], axis=0)
            lower = lower.astype(_BF16)
        w_ref[i] = jnp.concatenate([wq, lower], axis=0)
        m_ref[i] = jnp.full(m_ref.shape[1:], NEG_BIG, _F32)
        l_ref[i] = jnp.zeros(l_ref.shape[1:], _F32)
        acc_ref[i] = jnp.zeros(acc_ref.shape[1:], _F32)
        return carry

    lax.fori_loop(0, nq, setup, 0)

    def unit(t):
        return uq_ref[t], uk_ref[t]

    def scores(t, slot, causal):
        q, kb = unit(t)
        start = pl.multiple_of(kb * tk, tk)
        ka = jnp.concatenate([k_ref[pl.ds(start, tk), :], kx_ref[pl.ds(start, tk), :]], axis=1)
        st = jnp.dot(ka, w_ref[q], preferred_element_type=_F32)
        if causal:
            key = start + lax.broadcasted_iota(jnp.int32, (tk, wide), 0)
            qry = q * tq + lax.broadcasted_iota(jnp.int32, (tk, wide), 1) % tq
            st = jnp.where(key <= qry, st, NEG_BIG)
        st_ref[slot] = st
        pmax_ref[slot] = jnp.max(st.reshape(tk // 8, 8, wide), axis=0)

    def softmax(t, slot):
        q, _ = unit(t)
        rows = SOFTMAX_ROWS
        m_prev = m_ref[q][0:1, :]
        m_new = jnp.maximum(m_prev, jnp.max(pmax_ref[slot], axis=0, keepdims=True))
        for r in range(0, tk, rows):
            p_ref[slot, r:r + rows, :] = jnp.exp2(st_ref[slot, r:r + rows, :] - m_new).astype(_BF16)
        m_ref[q] = jnp.broadcast_to(m_new, m_ref.shape[1:])
        alpha_ref[slot] = jnp.broadcast_to(jnp.exp2(m_prev - m_new), alpha_ref.shape[1:])

    def values(t, slot):
        q, kb = unit(t)
        start = pl.multiple_of(kb * tk, tk)
        ones = jnp.ones((ONES_ROWS, tk), _BF16)
        for h in range(HEADS_PER_STEP):
            cols = slice(h * tq, (h + 1) * tq)
            feats = slice(h * HEAD_DIM, (h + 1) * HEAD_DIM)
            lhs = jnp.concatenate([vt_ref[feats, pl.ds(start, tk)], ones], axis=0)
            pv = jnp.dot(lhs, p_ref[slot, :, cols], preferred_element_type=_F32)
            alpha = alpha_ref[slot, 0:1, cols]
            acc_ref[q, feats, :] = acc_ref[q, feats, :] * alpha + pv[0:HEAD_DIM]
            l_ref[q, :, cols] = jnp.broadcast_to(
                alpha * l_ref[q, 0:1, cols] + pv[HEAD_DIM:HEAD_DIM + 1], (8, tq))

    def step(base, c, causal, do_scores=True, do_softmax=True, do_values=True):
        u = base + c
        if do_scores:
            scores(u, c % pipe, causal)
        if do_values:
            values(u - 2 * lag, (c - 2 * lag) % pipe)
        if do_softmax:
            softmax(u - lag, (c - lag) % pipe)

    def group(causal):
        def body(j, carry):
            for c in range(pipe):
                step(j * pipe, c, causal)
            return carry
        return body

    def pipeline(g_first, g_end, causal_groups):
        for c in range(pipe):
            step(g_first * pipe, c, g_first < causal_groups,
                 do_softmax=c >= lag, do_values=c >= 2 * lag)
        if g_first + 1 < causal_groups:
            lax.fori_loop(g_first + 1, causal_groups, group(True), 0)
        lax.fori_loop(max(g_first + 1, causal_groups), g_end, group(False), 0)
        for c in range(2 * lag):
            step(g_end * pipe, c, False, do_scores=False, do_softmax=c < lag)

    diag_groups = n_diag // pipe
    if mode == "moba":
        pipeline(0, n_units // pipe, diag_groups)
    else:
        pipeline(0, diag_groups, diag_groups)
        nq_blocks = n_diag
        col = lax.broadcasted_iota(jnp.int32, (LANES, LANES), 1)
        first_head = lax.broadcasted_iota(jnp.int32, (1, wide), 1) < tq
        head0 = hp * HEADS_PER_STEP

        def per_head(table, pick0, pick1, fill):
            v0 = jnp.max(jnp.where(pick0, table, fill), axis=1, keepdims=True)[0:nb]
            v1 = jnp.max(jnp.where(pick1, table, fill), axis=1, keepdims=True)[0:nb]
            return jnp.where(first_head, v0, v1)

        bias_max = per_head(aux2_ref[...], col == head0, col == head0 + 1, NEG_BIG)
        k_absmax = per_head(aux_ref[...], col < HEAD_DIM, col >= HEAD_DIM, 0.0)
        blk = lax.broadcasted_iota(jnp.int32, (nb, 1), 0)

        needed = jnp.zeros((nb, 1), jnp.int32)
        for i in range(1, nq_blocks):
            bound = bias_max + qn_ref[i, 0:1, :] * k_absmax - m_ref[i, 0:1, :]
            live = jnp.max(bound, axis=1, keepdims=True) >= -UNDERFLOW_LOG2
            needed = jnp.maximum(needed, jnp.where(live & (blk < i), i - blk, 0))
        far = jnp.max(needed)
        n_live = far * nq_blocks - (far * (far + 1)) // 2
        live_groups = (n_live + pipe - 1) // pipe

        @pl.when(live_groups > 0)
        def _():
            pipeline(diag_groups, diag_groups + live_groups, diag_groups)

    def finish(i, carry):
        out_t = jnp.concatenate(
            [acc_ref[i, h * HEAD_DIM:(h + 1) * HEAD_DIM, :] / l_ref[i, 0:1, h * tq:(h + 1) * tq]
             for h in range(HEADS_PER_STEP)], axis=0)
        o_ref[pl.ds(pl.multiple_of(i * tq, tq), tq), :] = out_t.T.astype(o_ref.dtype)
        return carry

    lax.fori_loop(0, nq, finish, 0, unroll=2)


def _attention(qt, k, kx, vt, aux, aux2, *, mode):
    b, width, s = qt.shape
    tq = ATTN_Q
    tk, pipe, lag = FOX_PIPELINE if mode == "fox" else MOBA_PIPELINE
    nq = s // tq
    nb = s // MOBA_BLOCK
    wide = HEADS_PER_STEP * tq
    diag = [(i * tq) // tk for i in range(nq)]
    units = list(enumerate(diag))
    units += [(i, diag[i] - dist) for dist in range(1, nq) for i in range(nq) if diag[i] >= dist]
    assert mode != "fox" or tq == tk
    assert nq % pipe == 0 and len(units) % pipe == 0 and pipe >= 2 * lag
    unit_q = jnp.asarray([u[0] for u in units], jnp.int32)
    unit_k = jnp.asarray([u[1] for u in units], jnp.int32)
    aux_spec = pl.BlockSpec((None, LANES, LANES), lambda bi, hp, uq, uk: (bi, 0, hp))
    aux2_spec = pl.BlockSpec((None, LANES, LANES), lambda bi, hp, uq, uk: (bi, 0, 0))
    if mode == "fox":
        kx_spec = pl.BlockSpec((None, s, LANES), lambda bi, hp, uq, uk: (bi, 0, 0))
    else:
        kx_spec = pl.BlockSpec(kx.shape, lambda bi, hp, uq, uk: (0, 0))
    return pl.pallas_call(
        functools.partial(_attn_body, mode=mode, tq=tq, tk=tk, nb=nb, n_diag=nq, n_units=len(units),
                          pipe=pipe, lag=lag),
        grid_spec=pltpu.PrefetchScalarGridSpec(
            num_scalar_prefetch=2,
            grid=(b, width // LANES),
            in_specs=[
                pl.BlockSpec((None, LANES, s), lambda bi, hp, uq, uk: (bi, hp, 0)),
                pl.BlockSpec((None, s, LANES), lambda bi, hp, uq, uk: (bi, 0, hp)),
                kx_spec,
                pl.BlockSpec((None, LANES, s), lambda bi, hp, uq, uk: (bi, hp, 0)),
                aux_spec,
                aux2_spec,
            ],
            out_specs=pl.BlockSpec((None, s, LANES), lambda bi, hp, uq, uk: (bi, 0, hp)),
            scratch_shapes=[
                pltpu.VMEM((nq, 2 * LANES, wide), _BF16),
                pltpu.VMEM((nq, 8, wide), _F32),
                pltpu.VMEM((nq, 8, wide), _F32),
                pltpu.VMEM((nq, LANES, tq), _F32),
                pltpu.VMEM((pipe, tk, wide), _F32),
                pltpu.VMEM((pipe, tk, wide), _BF16),
                pltpu.VMEM((pipe, 8, wide), _F32),
                pltpu.VMEM((pipe, 8, wide), _F32),
                pltpu.VMEM((nq, 8, wide), _F32),
            ],
        ),
        out_shape=jax.ShapeDtypeStruct((b, s, width), _BF16),
        compiler_params=_params("arbitrary", "arbitrary"),
        name=mode,
    )(unit_q, unit_k, qt, k, kx, vt, aux, aux2)


def _pack_bf16_pair(a, b):
    lo = pltpu.bitcast(a.astype(_BF16).astype(_F32), jnp.uint32) >> 16
    hi = pltpu.bitcast(b.astype(_BF16).astype(_F32), jnp.uint32) & jnp.uint32(0xFFFF0000)
    return lo | hi


def _unpack_bf16_pair(u):
    lo = pltpu.bitcast(u << 16, _F32)
    hi = pltpu.bitcast(u & jnp.uint32(0xFFFF0000), _F32)
    return jnp.concatenate([lo, hi], axis=1)


def _postattn_body(yf_ref, ym_ref, gates_ref, x_ref, wfb_ref, wmb_ref, wout_ref, g_ref, wr_ref, br_ref,
                   x1_ref, h2_ref, route_ref, meta_ref, counts_ref, carry_ref, *, tm, d):
    step = pl.program_id(0)
    ya = jnp.dot(yf_ref[...], wfb_ref[...], preferred_element_type=_F32)
    yb = jnp.dot(ym_ref[...], wmb_ref[...], preferred_element_type=_F32)
    mixed = gates_ref[:, 0:d].astype(_F32) * ya + gates_ref[:, d:2 * d].astype(_F32) * yb
    x1 = x_ref[...] + jnp.dot(mixed.astype(_BF16), wout_ref[...], preferred_element_type=_F32)
    x1_ref[...] = x1
    h2 = _rms(x1, g_ref[...])
    h2_ref[...] = _pack_bf16_pair(h2[:, 0:d // 2], h2[:, d // 2:d])

    logits = jnp.dot(h2.astype(_BF16), wr_ref[...], preferred_element_type=_F32) + br_ref[...]
    logits_t = logits.T
    gl = logits_t[N_EXPERTS:N_EXPERTS + N_GROUPS]
    ge = jnp.exp(gl - jnp.max(gl, axis=0, keepdims=True))
    gp = ge / jnp.sum(ge, axis=0, keepdims=True)
    g_top = jnp.max(gp, axis=0, keepdims=True)
    g_row = lax.broadcasted_iota(jnp.int32, (N_GROUPS, tm), 0)
    g_idx = jnp.min(jnp.where(gp == g_top, g_row, N_GROUPS), axis=0, keepdims=True)
    fl = logits_t[0:EXPERTS_PER_GROUP]
    for g in range(1, N_GROUPS):
        fl = jnp.where(g_idx == g, logits_t[g * EXPERTS_PER_GROUP:(g + 1) * EXPERTS_PER_GROUP], fl)
    f_row = lax.broadcasted_iota(jnp.int32, (EXPERTS_PER_GROUP, tm), 0)
    f1 = jnp.max(fl, axis=0, keepdims=True)
    i1 = jnp.min(jnp.where(fl == f1, f_row, EXPERTS_PER_GROUP), axis=0, keepdims=True)
    fl2 = jnp.where(f_row == i1, -jnp.inf, fl)
    f2 = jnp.max(fl2, axis=0, keepdims=True)
    i2 = jnp.min(jnp.where(fl2 == f2, f_row, EXPERTS_PER_GROUP), axis=0, keepdims=True)
    e1 = g_idx * EXPERTS_PER_GROUP + i1
    e2 = g_idx * EXPERTS_PER_GROUP + i2
    t2 = jnp.exp(f2 - f1)
    w1 = g_top * (1.0 / (1.0 + t2))
    w2 = g_top * (t2 / (1.0 + t2))

    @pl.when(step == 0)
    def _():
        carry_ref[...] = jnp.zeros_like(carry_ref)

    x_row = lax.broadcasted_iota(jnp.int32, (N_EXPERTS, tm), 0)
    onehot = ((x_row == e1) | (x_row == e2)).astype(_F32)
    rr = lax.broadcasted_iota(jnp.int32, (tm, tm), 0)
    cc = lax.broadcasted_iota(jnp.int32, (tm, tm), 1)
    before = jnp.dot(onehot.astype(_BF16), (rr < cc).astype(_BF16), preferred_element_type=_F32)
    before = before + carry_ref[0:N_EXPERTS, 0:1]
    r1 = jnp.sum(jnp.where(x_row == e1, before, 0.0), axis=0, keepdims=True)
    r2 = jnp.sum(jnp.where(x_row == e2, before, 0.0), axis=0, keepdims=True)
    total = carry_ref[0:N_EXPERTS, 0:1] + jnp.sum(onehot, axis=1, keepdims=True)
    carry_ref[0:N_EXPERTS, :] = jnp.broadcast_to(total, (N_EXPERTS, LANES))
    counts_ref[...] = carry_ref[...]

    m_row = lax.broadcasted_iota(jnp.int32, (8, tm), 0)
    meta = jnp.where(m_row == 2, e1, 0)
    meta = jnp.where(m_row == 3, e2, meta)
    meta = jnp.where(m_row == 4, r1.astype(jnp.int32), meta)
    meta = jnp.where(m_row == 5, r2.astype(jnp.int32), meta)
    meta_ref[...] = meta
    w_row = lax.broadcasted_iota(jnp.int32, (LANES, tm), 0)
    route_ref[...] = jnp.where(w_row == 0, w1, jnp.where(w_row == 1, w2, 0.0)).T


def _postattn(yf, ym, gates, x, wfb, wmb, wout, g, wr, br, *, tm):
    t, d = x.shape
    width = yf.shape[1]
    row = lambda cols: pl.BlockSpec((tm, cols), lambda i: (i, 0))
    return pl.pallas_call(
        functools.partial(_postattn_body, tm=tm, d=d),
        grid=(t // tm,),
        in_specs=[row(width), row(width), row(2 * d), row(d),
                  _const_spec(wfb.shape), _const_spec(wmb.shape), _const_spec(wout.shape),
                  _const_spec((1, d)), _const_spec(wr.shape), _const_spec(br.shape)],
        out_specs=[row(d), row(d // 2), row(LANES), pl.BlockSpec((8, tm), lambda i: (0, i)),
                   _const_spec((LANES, LANES))],
        out_shape=[jax.ShapeDtypeStruct((t, d), _F32),
                   jax.ShapeDtypeStruct((t, d // 2), jnp.uint32),
                   jax.ShapeDtypeStruct((t, LANES), _F32),
                   jax.ShapeDtypeStruct((8, t), jnp.int32),
                   jax.ShapeDtypeStruct((LANES, LANES), _F32)],
        scratch_shapes=[pltpu.VMEM((LANES, LANES), _F32)],
        compiler_params=_params("arbitrary"),
        name="postattn",
    )(yf, ym, gates, x, wfb, wmb, wout, g, wr, br)


def _sc_scatter(rows, idx, n_out):
    n, width = rows.shape
    workers = SC_CORES * SC_SUBCORES
    per_worker = n // workers
    chunks = per_worker // SC_SCATTER_CHUNK
    assert n % (workers * SC_SCATTER_CHUNK) == 0 and idx.shape == (2, n // SC_SCATTER_CHUNK, SC_SCATTER_CHUNK)
    mesh = plsc.VectorSubcoreMesh(core_axis_name="c", subcore_axis_name="s",
                                  num_cores=SC_CORES, num_subcores=SC_SUBCORES)

    @functools.partial(
        pl.kernel, out_type=jax.ShapeDtypeStruct((n_out, width), rows.dtype), mesh=mesh,
        scratch_types=[pltpu.VMEM((2, chunks, SC_SCATTER_CHUNK), jnp.int32),
                       pltpu.VMEM((SC_SCATTER_CHUNK, width), rows.dtype),
                       pltpu.SemaphoreType.DMA],
        name="sc_scatter")
    def scatter(rows_hbm, idx_hbm, out_hbm, idx_v, rows_v, sem):
        worker = lax.axis_index("s") * SC_CORES + lax.axis_index("c")
        for k in range(2):
            pltpu.sync_copy(idx_hbm.at[k, pl.ds(worker * chunks, chunks)], idx_v.at[k])

        def chunk(j, carry):
            start = worker * per_worker + j * SC_SCATTER_CHUNK
            pltpu.sync_copy(rows_hbm.at[pl.ds(start, SC_SCATTER_CHUNK)], rows_v)
            for k in range(2):
                pltpu.async_copy(rows_v, out_hbm.at[idx_v.at[k, j]], sem).wait()
            return carry

        lax.fori_loop(0, chunks, chunk, 0)

    return scatter(rows, idx)


def _experts_body(be_ref, valid_ref, nused_ref, xs_ref, wg_ref, wu_ref, wd_ref, out_ref,
                  wgb_ref, wub_ref, wdb_ref):
    j = pl.program_id(0)
    used = j < nused_ref[0]

    @pl.when(jnp.logical_not(used))
    def _():
        out_ref[...] = jnp.zeros_like(out_ref)

    @pl.when(used & ((j == 0) | (be_ref[j] != be_ref[jnp.maximum(j - 1, 0)])))
    def _():
        wgb_ref[...] = wg_ref[...].astype(_BF16)
        wub_ref[...] = wu_ref[...].astype(_BF16)
        wdb_ref[...] = wd_ref[...].astype(_BF16)

    @pl.when(used)
    def _():
        row = lax.broadcasted_iota(jnp.int32, xs_ref.shape, 0)
        packed = jnp.where(row < valid_ref[j], xs_ref[...], jnp.uint32(0))
        rows = _unpack_bf16_pair(packed).astype(_BF16)
        hidden = wgb_ref.shape[1]
        out = None
        for c in range(0, hidden, hidden // EXPERT_PARTS):
            cols = slice(c, c + hidden // EXPERT_PARTS)
            gate = jnp.dot(rows, wgb_ref[:, cols], preferred_element_type=_F32)
            up = jnp.dot(rows, wub_ref[:, cols], preferred_element_type=_F32)
            act = ((gate * jax.nn.sigmoid(gate)) * up).astype(_BF16)
            part = jnp.dot(act, wdb_ref[cols, :], preferred_element_type=_F32)
            out = part if out is None else out + part
        half = out.shape[1] // 2
        out_ref[...] = _pack_bf16_pair(out[:, 0:half], out[:, half:])


def _experts(block_expert, block_valid, n_used, xs, wg, wu, wd, *, te):
    n_rows, half = xs.shape
    _, d, de = wg.shape

    def blk(j, be, bv, nu):
        return jnp.minimum(j, nu[0] - 1)

    def weight(j, be, bv, nu):
        return (be[blk(j, be, bv, nu)], 0, 0)

    return pl.pallas_call(
        _experts_body,
        grid_spec=pltpu.PrefetchScalarGridSpec(
            num_scalar_prefetch=3,
            grid=(n_rows // te,),
            in_specs=[pl.BlockSpec((te, half), lambda j, be, bv, nu: (blk(j, be, bv, nu), 0)),
                      pl.BlockSpec((None, d, de), weight),
                      pl.BlockSpec((None, d, de), weight),
                      pl.BlockSpec((None, de, d), weight)],
            out_specs=pl.BlockSpec((te, d // 2), lambda j, be, bv, nu: (j, 0)),
            scratch_shapes=[pltpu.VMEM((d, de), _BF16), pltpu.VMEM((d, de), _BF16),
                            pltpu.VMEM((de, d), _BF16)],
        ),
        out_shape=jax.ShapeDtypeStruct((n_rows, d // 2), jnp.uint32),
        compiler_params=_params("arbitrary"),
        name="experts",
    )(block_expert, block_valid, n_used, xs, wg, wu, wd)


def _sc_gather(table, idx):
    n, = idx.shape
    width = table.shape[1]
    workers = SC_CORES * SC_SUBCORES
    per_worker = n // workers
    chunks = per_worker // SC_CHUNK
    assert n % (workers * SC_CHUNK) == 0
    mesh = plsc.VectorSubcoreMesh(core_axis_name="c", subcore_axis_name="s",
                                  num_cores=SC_CORES, num_subcores=SC_SUBCORES)

    @functools.partial(
        pl.kernel, out_type=jax.ShapeDtypeStruct((n, width), table.dtype), mesh=mesh,
        scratch_types=[pltpu.VMEM((per_worker,), jnp.int32),
                       pltpu.VMEM((SC_CHUNK, width), table.dtype),
                       pltpu.SemaphoreType.DMA],
        name="sc_gather")
    def gather(table_hbm, idx_hbm, out_hbm, idx_v, rows_v, sem):
        worker = lax.axis_index("s") * SC_CORES + lax.axis_index("c")
        base = worker * per_worker
        pltpu.sync_copy(idx_hbm.at[pl.ds(base, per_worker)], idx_v)

        def chunk(j, carry):
            off = pl.multiple_of(j * SC_CHUNK, SC_CHUNK)
            pltpu.async_copy(table_hbm.at[idx_v.at[pl.ds(off, SC_CHUNK)]], rows_v, sem).wait()
            pltpu.sync_copy(rows_v, out_hbm.at[pl.ds(base + off, SC_CHUNK)])
            return carry

        lax.fori_loop(0, chunks, chunk, 0)

    return gather(table, idx)


def _final_body(x1_ref, route_ref, p_ref, r1_ref, r2_ref, gp_ref, wpg_ref, wpp_ref, gf_ref, o_ref):
    emb = jnp.dot(p_ref[...].astype(_BF16), wpp_ref[...], preferred_element_type=_F32)
    route = route_ref[...]
    x2 = x1_ref[...] + (_unpack_bf16_pair(r1_ref[...]) * route[:, 0:1]
                        + _unpack_bf16_pair(r2_ref[...]) * route[:, 1:2])
    gate = jax.nn.sigmoid(jnp.dot(_rms(x2, gp_ref[...]).astype(_BF16), wpg_ref[...],
                                  preferred_element_type=_F32))
    o_ref[...] = _rms(x2 + gate * emb, gf_ref[...])


def _final(x1, route, p, slot_rows, gp, wpg, wpp, gf, *, tm):
    t, d = x1.shape
    ple = p.shape[1]
    steps = t // tm
    row = lambda cols: pl.BlockSpec((tm, cols), lambda i: (i, 0))
    return pl.pallas_call(
        _final_body,
        grid=(steps,),
        in_specs=[row(d), row(LANES), row(ple), row(d // 2),
                  pl.BlockSpec((tm, d // 2), lambda i: (steps + i, 0)),
                  _const_spec((1, d)), _const_spec(wpg.shape), _const_spec(wpp.shape),
                  _const_spec((1, d))],
        out_specs=row(d),
        out_shape=jax.ShapeDtypeStruct((t, d), _F32),
        compiler_params=_params("arbitrary"),
        name="final",
    )(x1, route, p, slot_rows, slot_rows, gp, wpg, wpp, gf)


def _rope_tables(s):
    half = HEAD_DIM // 2
    inv = 1.0 / (ROPE_THETA ** (jnp.arange(0, HEAD_DIM, 2, dtype=_F32) / HEAD_DIM))
    ang = jnp.arange(s, dtype=_F32)[:, None] * inv[None, :]
    cos, sin = jnp.cos(ang), jnp.sin(ang)
    reps = LANES // half
    cos_t = jnp.tile(cos, (1, reps))
    sin_t = jnp.tile(jnp.concatenate([-sin, sin], axis=1), (1, reps // 2))
    return cos_t, sin_t


def _layer(x, p, attn_norm, w_in, b_forget, w_fox_branch, w_moba_branch, w_out, moe_norm,
           w_group, b_group, w_fine, b_fine, w_gate, w_up, w_down, ple_norm, w_ple_gate, w_ple_proj,
           final_gain):
    b, s, d = x.shape
    t = b * s
    width = N_FOX_HEADS * HEAD_DIM
    assert N_MOBA_HEADS * HEAD_DIM == width and MOBA_BLOCK % ATTN_Q == 0 and MOBA_PIPELINE[0] % MOBA_BLOCK == 0
    assert BIAS_PIECES * N_FOX_HEADS <= LANES and s % MOBA_PIPELINE[0] == 0
    assert s % ROW_TILE == 0 and ROW_TILE % MOBA_BLOCK == 0 and s // MOBA_BLOCK <= LANES
    assert t % FINAL_TILE == 0 and N_EXPERTS + N_GROUPS <= LANES

    qkv_cols = 6 * width
    f_cols = N_FOX_HEADS
    chunk = lambda c: w_in[:, c * width:(c + 1) * width]
    w_rows = jnp.concatenate([chunk(1), chunk(4), w_in[:, qkv_cols + f_cols:]], axis=1).astype(_BF16)
    w_cols = jnp.concatenate([chunk(0), chunk(2), chunk(3), chunk(5)], axis=1).T.astype(_BF16)
    wf = jnp.zeros((d, LANES), _BF16).at[:, :f_cols].set(w_in[:, qkv_cols:qkv_cols + f_cols].astype(_BF16))
    bfp = jnp.zeros((1, LANES), _F32).at[0, :f_cols].set(b_forget.astype(_F32))
    cos, sin = _rope_tables(s)

    qft, kf, vft, qmt, km, vmt, gates, cb, kmean, kabs, bmax = _inproj(
        x, attn_norm.reshape(1, d), w_rows, w_cols, wf, bfp, cos, sin, cos.T, sin.T, tm=ROW_TILE)
    y_fox = _attention(qft, kf, cb, vft, kabs, bmax, mode="fox")
    block_of_key = jnp.arange(s, dtype=jnp.int32)[:, None] // MOBA_BLOCK
    block_onehot = (block_of_key == jnp.arange(LANES, dtype=jnp.int32)[None, :]).astype(_BF16)
    y_moba = _attention(qmt, km, block_onehot, vmt, kmean, bmax, mode="moba")

    wr = jnp.zeros((d, LANES), _BF16)
    wr = wr.at[:, :N_EXPERTS].set(w_fine.astype(_BF16))
    wr = wr.at[:, N_EXPERTS:N_EXPERTS + N_GROUPS].set(w_group.astype(_BF16))
    br = jnp.zeros((1, LANES), _F32)
    br = br.at[0, :N_EXPERTS].set(b_fine.astype(_F32))
    br = br.at[0, N_EXPERTS:N_EXPERTS + N_GROUPS].set(b_group.astype(_F32))
    x1, h2p, route, meta, counts = _postattn(
        y_fox.reshape(t, width), y_moba.reshape(t, width), gates.reshape(t, 2 * d), x.reshape(t, d),
        w_fox_branch.astype(_BF16), w_moba_branch.astype(_BF16), w_out.astype(_BF16),
        moe_norm.reshape(1, d), wr, br, tm=ROW_TILE)

    te = EXPERT_TILE
    n_blk = (2 * t) // te + N_EXPERTS
    counts = counts[:N_EXPERTS, 0].astype(jnp.int32)
    blocks_per = (counts + te - 1) // te
    block_end = jnp.cumsum(blocks_per)
    row_start = (block_end - blocks_per) * te
    expert_ids = jnp.arange(N_EXPERTS, dtype=jnp.int32)[:, None, None]
    slot_major = meta[4:6] + jnp.sum(
        jnp.where(meta[2:4][None] == expert_ids, row_start[:, None, None], 0), axis=0)
    n_used = block_end[-1:].astype(jnp.int32)
    block_ids = jnp.arange(n_blk, dtype=jnp.int32)
    block_expert = jnp.minimum(
        jnp.sum((block_ids[:, None] >= block_end[None, :]).astype(jnp.int32), axis=1), N_EXPERTS - 1)

    first_block = block_end - blocks_per
    block_valid = jnp.clip(counts[block_expert] - (block_ids - first_block[block_expert]) * te, 0, te)
    xs = _sc_scatter(h2p, slot_major.reshape(2, t // SC_SCATTER_CHUNK, SC_SCATTER_CHUNK), n_blk * te)
    rows = _experts(block_expert, block_valid, n_used, xs, w_gate, w_up, w_down, te=te)
    slot_rows = _sc_gather(rows, slot_major.reshape(-1))
    out = _final(x1, route, p.reshape(t, -1), slot_rows, ple_norm.reshape(1, d),
                 w_ple_gate.astype(_BF16), w_ple_proj.astype(_BF16), final_gain.reshape(1, d),
                 tm=FINAL_TILE)
    return out.reshape(b, s, d)


def kernel(x, p, attn_norm, w_in, b_forget, w_fox_branch, w_moba_branch, w_out, moe_norm, w_group,
           b_group, w_fine, b_fine, w_gate, w_up, w_down, ple_norm, w_ple_gate, w_ple_proj, final_norm):
    depth = p.shape[0]
    assert depth == 1, "the final norm is fused into the last layer's kernel"
    i = 0
    return _layer(x, p[i], attn_norm[i], w_in[i], b_forget[i], w_fox_branch[i], w_moba_branch[i],
                  w_out[i], moe_norm[i], w_group[i], b_group[i], w_fine[i], b_fine[i], w_gate[i],
                  w_up[i], w_down[i], ple_norm[i], w_ple_gate[i], w_ple_proj[i], final_norm)
```

```python
import functools

import jax
import jax.numpy as jnp
from jax import lax
from jax.experimental import pallas as pl
from jax.experimental.pallas import tpu as pltpu
from jax.experimental.pallas import tpu_sc as plsc

HEAD_DIM = 64
N_FOX_HEADS = 8
N_MOBA_HEADS = 8
MOBA_BLOCK = 256
MOBA_TOPK = 3
ROPE_THETA = 10000.0
N_GROUPS = 4
EXPERTS_PER_GROUP = 8
N_EXPERTS = N_GROUPS * EXPERTS_PER_GROUP
RMS_EPS = 1e-6

LANES = 128
HEADS_PER_STEP = LANES // HEAD_DIM
ATTN_Q = 256
FOX_PIPELINE = (256, 8, 4)
MOBA_PIPELINE = (256, 4, 2)
ONES_ROWS = 16
SOFTMAX_ROWS = 64
BIAS_PIECES = 3
ROW_TILE = 512
SC_CORES = 2
SC_SUBCORES = 16
SC_CHUNK = 64
SC_SCATTER_CHUNK = 128
FINAL_TILE = 256
EXPERT_PARTS = 2
EXPERT_TILE = 512
LOG2_E = 1.4426950408889634
UNDERFLOW_LOG2 = 160.0
NEG_BIG = -1e30
VMEM_LIMIT = 48 * 1024 * 1024

_BF16 = jnp.bfloat16
_F32 = jnp.float32


def _params(*sem):
    return pltpu.CompilerParams(dimension_semantics=sem, vmem_limit_bytes=VMEM_LIMIT)


def _rms(x, g):
    return x * lax.rsqrt(jnp.mean(x * x, axis=-1, keepdims=True) + RMS_EPS) * g


def _const_spec(shape):
    return pl.BlockSpec(shape, lambda *_: (0,) * len(shape))


def _inproj_body(x_ref, g_ref, w_ref, wt_ref, wf_ref, bf_ref, cos_ref, sin_ref, cost_ref, sint_ref,
                 qft_ref, kf_ref, vft_ref, qmt_ref, km_ref, vmt_ref, gates_ref, cb_ref, kmean_ref,
                 kabs_ref, bmax_ref,
                 carry_ref, *, tm, width, gate_cols):
    j = pl.program_id(1)
    h = _rms(x_ref[...], g_ref[...])
    hb = h.astype(_BF16)
    hbt = h.T.astype(_BF16)
    reps = width // LANES
    half = HEAD_DIM // 2

    def proj(c):
        return jnp.dot(hb, w_ref[:, c * width:(c + 1) * width], preferred_element_type=_F32)

    def proj_t(c):
        return jnp.dot(wt_ref[c * width:(c + 1) * width, :], hbt, preferred_element_type=_F32)

    def rope(t):
        cos = jnp.concatenate([cos_ref[...]] * reps, axis=1)
        sin = jnp.concatenate([sin_ref[...]] * reps, axis=1)
        first = (lax.broadcasted_iota(jnp.int32, (1, width), 1) % HEAD_DIM) < half
        partner = jnp.where(first, pltpu.roll(t, width - half, 1), pltpu.roll(t, half, 1))
        return t * cos + partner * sin

    def rope_t(t):
        cos = jnp.concatenate([cost_ref[...]] * reps, axis=0)
        sin = jnp.concatenate([sint_ref[...]] * reps, axis=0)
        first = (lax.broadcasted_iota(jnp.int32, (width, 1), 0) % HEAD_DIM) < half
        partner = jnp.where(first, pltpu.roll(t, width - half, 0), pltpu.roll(t, half, 0))
        return t * cos + partner * sin

    scale = HEAD_DIM ** -0.5 * LOG2_E
    qft_ref[...] = (proj_t(0) * scale).astype(_BF16)
    vft_ref[...] = proj_t(1).astype(_BF16)
    qmt_ref[...] = (rope_t(proj_t(2)) * scale).astype(_BF16)
    vmt_ref[...] = proj_t(3).astype(_BF16)
    kf = proj(0).astype(_BF16)
    kf_ref[...] = kf
    kf_abs = jnp.abs(kf.astype(_F32))
    km = rope(proj(1))
    km_ref[...] = km.astype(_BF16)

    @pl.when(j == 0)
    def _():
        kmean_ref[...] = jnp.zeros_like(kmean_ref)
        kabs_ref[...] = jnp.zeros_like(kabs_ref)
        bmax_ref[...] = jnp.zeros_like(bmax_ref)
        carry_ref[...] = jnp.zeros_like(carry_ref)

    blocks = tm // MOBA_BLOCK
    for b in range(blocks):
        rows = slice(b * MOBA_BLOCK, (b + 1) * MOBA_BLOCK)
        kmean_ref[pl.ds(j * blocks + b, 1), :] = jnp.mean(km[rows], axis=0, keepdims=True)
        kabs_ref[pl.ds(j * blocks + b, 1), :] = jnp.max(kf_abs[rows], axis=0, keepdims=True)

    for c in range(gate_cols // width):
        g = jnp.dot(hb, w_ref[:, (2 + c) * width:(3 + c) * width],
                    preferred_element_type=_F32)
        gates_ref[:, c * width:(c + 1) * width] = jax.nn.sigmoid(g).astype(_BF16)

    z = jnp.dot(hb, wf_ref[...], preferred_element_type=_F32) + bf_ref[...]
    log_f = jnp.minimum(z, 0.0) - jnp.log1p(jnp.exp(-jnp.abs(z)))
    r = lax.broadcasted_iota(jnp.int32, (LANES, LANES), 0)
    c = lax.broadcasted_iota(jnp.int32, (LANES, LANES), 1)
    tri = (c <= r).astype(_BF16)
    offset = carry_ref[0:1, :]
    blocks_cum = []
    for blk in range(tm // LANES):
        rest = log_f[blk * LANES:(blk + 1) * LANES]
        within = jnp.zeros((LANES, LANES), _F32)
        for _ in range(BIAS_PIECES):
            part = rest.astype(_BF16)
            rest = rest - part.astype(_F32)
            within = within + jnp.dot(tri, part, preferred_element_type=_F32)
        blocks_cum.append(within + offset)
        offset = offset + within[LANES - 1:LANES, :]
    cum = jnp.concatenate(blocks_cum, axis=0)
    carry_ref[...] = jnp.broadcast_to(offset, carry_ref.shape)

    head_lane = lax.broadcasted_iota(jnp.int32, (1, LANES), 1) < N_FOX_HEADS
    bias = jnp.where(head_lane, -cum * LOG2_E, 0.0)
    for b in range(blocks):
        bmax_ref[pl.ds(j * blocks + b, 1), :] = jnp.max(
            bias[b * MOBA_BLOCK:(b + 1) * MOBA_BLOCK], axis=0, keepdims=True)
    rest = bias
    placed = jnp.zeros((tm, LANES), _F32)
    for piece in range(BIAS_PIECES):
        part = rest.astype(_BF16).astype(_F32)
        rest = rest - part
        placed = placed + (pltpu.roll(part, N_FOX_HEADS * piece, 1) if piece else part)
    cb_ref[...] = placed.astype(_BF16)


def _inproj(x, g, w, wt, wf, bfp, cos, sin, cos_t, sin_t, *, tm):
    b, s, d = x.shape
    width = N_FOX_HEADS * HEAD_DIM
    gate_cols = w.shape[1] - 2 * width
    act = jax.ShapeDtypeStruct((b, s, width), _BF16)
    act_t = jax.ShapeDtypeStruct((b, width, s), _BF16)
    act_spec = pl.BlockSpec((None, tm, width), lambda bi, j: (bi, j, 0))
    act_t_spec = pl.BlockSpec((None, width, tm), lambda bi, j: (bi, 0, j))
    return pl.pallas_call(
        functools.partial(_inproj_body, tm=tm, width=width, gate_cols=gate_cols),
        grid=(b, s // tm),
        in_specs=[
            pl.BlockSpec((None, tm, d), lambda bi, j: (bi, j, 0)),
            _const_spec((1, d)),
            _const_spec(w.shape),
            _const_spec(wt.shape),
            _const_spec(wf.shape),
            _const_spec(bfp.shape),
            pl.BlockSpec((tm, LANES), lambda bi, j: (j, 0)),
            pl.BlockSpec((tm, LANES), lambda bi, j: (j, 0)),
            pl.BlockSpec((LANES, tm), lambda bi, j: (0, j)),
            pl.BlockSpec((LANES, tm), lambda bi, j: (0, j)),
        ],
        out_specs=[act_t_spec, act_spec, act_t_spec, act_t_spec, act_spec, act_t_spec,
                   pl.BlockSpec((None, tm, gate_cols), lambda bi, j: (bi, j, 0)),
                   pl.BlockSpec((None, tm, LANES), lambda bi, j: (bi, j, 0)),
                   pl.BlockSpec((None, LANES, width), lambda bi, j: (bi, 0, 0)),
                   pl.BlockSpec((None, LANES, width), lambda bi, j: (bi, 0, 0)),
                   pl.BlockSpec((None, LANES, LANES), lambda bi, j: (bi, 0, 0))],
        out_shape=[act_t, act, act_t, act_t, act, act_t,
                   jax.ShapeDtypeStruct((b, s, gate_cols), _BF16),
                   jax.ShapeDtypeStruct((b, s, LANES), _BF16),
                   jax.ShapeDtypeStruct((b, LANES, width), _F32),
                   jax.ShapeDtypeStruct((b, LANES, width), _F32),
                   jax.ShapeDtypeStruct((b, LANES, LANES), _F32)],
        scratch_shapes=[pltpu.VMEM((8, LANES), _F32)],
        compiler_params=_params("arbitrary", "arbitrary"),
        name="inproj",
    )(x, g, w, wt, wf, bfp, cos, sin, cos_t, sin_t)


def _attn_body(uq_ref, uk_ref, qt_ref, k_ref, kx_ref, vt_ref, aux_ref, aux2_ref, o_ref,
               w_ref, m_ref, l_ref, acc_ref, st_ref, p_ref, alpha_ref, pmax_ref, qn_ref, *,
               mode, tq, tk, nb, n_diag, n_units, pipe, lag):
    hp = pl.program_id(1)
    wide = HEADS_PER_STEP * tq
    nq = w_ref.shape[0]
    feat = lax.broadcasted_iota(jnp.int32, (LANES, 1), 0)
    xr = lax.broadcasted_iota(jnp.int32, (LANES, wide), 0)
    xc = lax.broadcasted_iota(jnp.int32, (LANES, wide), 1)

    def setup(i, carry):
        qt = qt_ref[:, pl.ds(pl.multiple_of(i * tq, tq), tq)]
        zero = jnp.zeros_like(qt)
        wq = jnp.concatenate([jnp.where(feat < HEAD_DIM, qt, zero),
                              jnp.where(feat < HEAD_DIM, zero, qt)], axis=1)
        if mode == "fox":
            head = hp * HEADS_PER_STEP + xc // tq
            lower = ((xr % N_FOX_HEADS == head) & (xr < BIAS_PIECES * N_FOX_HEADS)).astype(_BF16)
            qn_ref[i] = jnp.broadcast_to(
                jnp.sum(jnp.abs(wq.astype(_F32)), axis=0, keepdims=True), qn_ref.shape[1:])
        else:
            own = (i * tq) // MOBA_BLOCK
            gate = jnp.dot(aux_ref[...].astype(_BF16), wq, preferred_element_type=_F32)[0:nb]
            blk = lax.broadcasted_iota(jnp.int32, (nb, wide), 0)
            gate = jnp.where(blk < own, gate, -jnp.inf)
            keep = blk == own
            for _ in range(MOBA_TOPK):
                mx = jnp.max(gate, axis=0, keepdims=True)
                cand = jnp.where((gate == mx) & (mx > -jnp.inf), blk, nb)
                pick = blk == jnp.min(cand, axis=0, keepdims=True)
                keep = keep | pick
                gate = jnp.where(pick, -jnp.inf, gate)
            lower = jnp.where(keep, 0.0, NEG_BIG)
            if nb < LANES:
                lower = jnp.concatenate([lower, jnp.zeros((LANES - nb, wide), _F32)], axis=0)
            lower = lower.astype(_BF16)
        w_ref[i] = jnp.concatenate([wq, lower], axis=0)
        m_ref[i] = jnp.full(m_ref.shape[1:], NEG_BIG, _F32)
        l_ref[i] = jnp.zeros(l_ref.shape[1:], _F32)
        acc_ref[i] = jnp.zeros(acc_ref.shape[1:], _F32)
        return carry

    lax.fori_loop(0, nq, setup, 0)

    def unit(t):
        return uq_ref[t], uk_ref[t]

    def scores(t, slot, causal):
        q, kb = unit(t)
        start = pl.multiple_of(kb * tk, tk)
        ka = jnp.concatenate([k_ref[pl.ds(start, tk), :], kx_ref[pl.ds(start, tk), :]], axis=1)
        st = jnp.dot(ka, w_ref[q], preferred_element_type=_F32)
        if causal:
            key = start + lax.broadcasted_iota(jnp.int32, (tk, wide), 0)
            qry = q * tq + lax.broadcasted_iota(jnp.int32, (tk, wide), 1) % tq
            st = jnp.where(key <= qry, st, NEG_BIG)
        st_ref[slot] = st
        pmax_ref[slot] = jnp.max(st.reshape(tk // 8, 8, wide), axis=0)

    def softmax(t, slot):
        q, _ = unit(t)
        rows = SOFTMAX_ROWS
        m_prev = m_ref[q][0:1, :]
        m_new = jnp.maximum(m_prev, jnp.max(pmax_ref[slot], axis=0, keepdims=True))
        for r in range(0, tk, rows):
            p_ref[slot, r:r + rows, :] = jnp.exp2(st_ref[slot, r:r + rows, :] - m_new).astype(_BF16)
        m_ref[q] = jnp.broadcast_to(m_new, m_ref.shape[1:])
        alpha_ref[slot] = jnp.broadcast_to(jnp.exp2(m_prev - m_new), alpha_ref.shape[1:])

    def values(t, slot):
        q, kb = unit(t)
        start = pl.multiple_of(kb * tk, tk)
        ones = jnp.ones((ONES_ROWS, tk), _BF16)
        for h in range(HEADS_PER_STEP):
            cols = slice(h * tq, (h + 1) * tq)
            feats = slice(h * HEAD_DIM, (h + 1) * HEAD_DIM)
            lhs = jnp.concatenate([vt_ref[feats, pl.ds(start, tk)], ones], axis=0)
            pv = jnp.dot(lhs, p_ref[slot, :, cols], preferred_element_type=_F32)
            alpha = alpha_ref[slot, 0:1, cols]
            acc_ref[q, feats, :] = acc_ref[q, feats, :] * alpha + pv[0:HEAD_DIM]
            l_ref[q, :, cols] = jnp.broadcast_to(
                alpha * l_ref[q, 0:1, cols] + pv[HEAD_DIM:HEAD_DIM + 1], (8, tq))

    def step(base, c, causal, do_scores=True, do_softmax=True, do_values=True):
        u = base + c
        if do_scores:
            scores(u, c % pipe, causal)
        if do_values:
            values(u - 2 * lag, (c - 2 * lag) % pipe)
        if do_softmax:
            softmax(u - lag, (c - lag) % pipe)

    def group(causal):
        def body(j, carry):
            for c in range(pipe):
                step(j * pipe, c, causal)
            return carry
        return body

    def pipeline(g_first, g_end, causal_groups):
        for c in range(pipe):
            step(g_first * pipe, c, g_first < causal_groups,
                 do_softmax=c >= lag, do_values=c >= 2 * lag)
        if g_first + 1 < causal_groups:
            lax.fori_loop(g_first + 1, causal_groups, group(True), 0)
        lax.fori_loop(max(g_first + 1, causal_groups), g_end, group(False), 0)
        for c in range(2 * lag):
            step(g_end * pipe, c, False, do_scores=False, do_softmax=c < lag)

    diag_groups = n_diag // pipe
    if mode == "moba":
        pipeline(0, n_units // pipe, diag_groups)
    else:
        pipeline(0, diag_groups, diag_groups)
        nq_blocks = n_diag
        col = lax.broadcasted_iota(jnp.int32, (LANES, LANES), 1)
        first_head = lax.broadcasted_iota(jnp.int32, (1, wide), 1) < tq
        head0 = hp * HEADS_PER_STEP

        def per_head(table, pick0, pick1, fill):
            v0 = jnp.max(jnp.where(pick0, table, fill), axis=1, keepdims=True)[0:nb]
            v1 = jnp.max(jnp.where(pick1, table, fill), axis=1, keepdims=True)[0:nb]
            return jnp.where(first_head, v0, v1)

        bias_max = per_head(aux2_ref[...], col == head0, col == head0 + 1, NEG_BIG)
        k_absmax = per_head(aux_ref[...], col < HEAD_DIM, col >= HEAD_DIM, 0.0)
        blk = lax.broadcasted_iota(jnp.int32, (nb, 1), 0)

        needed = jnp.zeros((nb, 1), jnp.int32)
        for i in range(1, nq_blocks):
            bound = bias_max + qn_ref[i, 0:1, :] * k_absmax - m_ref[i, 0:1, :]
            live = jnp.max(bound, axis=1, keepdims=True) >= -UNDERFLOW_LOG2
            needed = jnp.maximum(needed, jnp.where(live & (blk < i), i - blk, 0))
        far = jnp.max(needed)
        n_live = far * nq_blocks - (far * (far + 1)) // 2
        live_groups = (n_live + pipe - 1) // pipe

        @pl.when(live_groups > 0)
        def _():
            pipeline(diag_groups, diag_groups + live_groups, diag_groups)

    def finish(i, carry):
        out_t = jnp.concatenate(
            [acc_ref[i, h * HEAD_DIM:(h + 1) * HEAD_DIM, :] / l_ref[i, 0:1, h * tq:(h + 1) * tq]
             for h in range(HEADS_PER_STEP)], axis=0)
        o_ref[pl.ds(pl.multiple_of(i * tq, tq), tq), :] = out_t.T.astype(o_ref.dtype)
        return carry

    lax.fori_loop(0, nq, finish, 0, unroll=2)


def _attention(qt, k, kx, vt, aux, aux2, *, mode):
    b, width, s = qt.shape
    tq = ATTN_Q
    tk, pipe, lag = FOX_PIPELINE if mode == "fox" else MOBA_PIPELINE
    nq = s // tq
    nb = s // MOBA_BLOCK
    wide = HEADS_PER_STEP * tq
    diag = [(i * tq) // tk for i in range(nq)]
    units = list(enumerate(diag))
    units += [(i, diag[i] - dist) for dist in range(1, nq) for i in range(nq) if diag[i] >= dist]
    assert mode != "fox" or tq == tk
    assert nq % pipe == 0 and len(units) % pipe == 0 and pipe >= 2 * lag
    unit_q = jnp.asarray([u[0] for u in units], jnp.int32)
    unit_k = jnp.asarray([u[1] for u in units], jnp.int32)
    aux_spec = pl.BlockSpec((None, LANES, LANES), lambda bi, hp, uq, uk: (bi, 0, hp))
    aux2_spec = pl.BlockSpec((None, LANES, LANES), lambda bi, hp, uq, uk: (bi, 0, 0))
    if mode == "fox":
        kx_spec = pl.BlockSpec((None, s, LANES), lambda bi, hp, uq, uk: (bi, 0, 0))
    else:
        kx_spec = pl.BlockSpec(kx.shape, lambda bi, hp, uq, uk: (0, 0))
    return pl.pallas_call(
        functools.partial(_attn_body, mode=mode, tq=tq, tk=tk, nb=nb, n_diag=nq, n_units=len(units),
                          pipe=pipe, lag=lag),
        grid_spec=pltpu.PrefetchScalarGridSpec(
            num_scalar_prefetch=2,
            grid=(b, width // LANES),
            in_specs=[
                pl.BlockSpec((None, LANES, s), lambda bi, hp, uq, uk: (bi, hp, 0)),
                pl.BlockSpec((None, s, LANES), lambda bi, hp, uq, uk: (bi, 0, hp)),
                kx_spec,
                pl.BlockSpec((None, LANES, s), lambda bi, hp, uq, uk: (bi, hp, 0)),
                aux_spec,
                aux2_spec,
            ],
            out_specs=pl.BlockSpec((None, s, LANES), lambda bi, hp, uq, uk: (bi, 0, hp)),
            scratch_shapes=[
                pltpu.VMEM((nq, 2 * LANES, wide), _BF16),
                pltpu.VMEM((nq, 8, wide), _F32),
                pltpu.VMEM((nq, 8, wide), _F32),
                pltpu.VMEM((nq, LANES, tq), _F32),
                pltpu.VMEM((pipe, tk, wide), _F32),
                pltpu.VMEM((pipe, tk, wide), _BF16),
                pltpu.VMEM((pipe, 8, wide), _F32),
                pltpu.VMEM((pipe, 8, wide), _F32),
                pltpu.VMEM((nq, 8, wide), _F32),
            ],
        ),
        out_shape=jax.ShapeDtypeStruct((b, s, width), _BF16),
        compiler_params=_params("arbitrary", "arbitrary"),
        name=mode,
    )(unit_q, unit_k, qt, k, kx, vt, aux, aux2)


def _pack_bf16_pair(a, b):
    lo = pltpu.bitcast(a.astype(_BF16).astype(_F32), jnp.uint32) >> 16
    hi = pltpu.bitcast(b.astype(_BF16).astype(_F32), jnp.uint32) & jnp.uint32(0xFFFF0000)
    return lo | hi


def _unpack_bf16_pair(u):
    lo = pltpu.bitcast(u << 16, _F32)
    hi = pltpu.bitcast(u & jnp.uint32(0xFFFF0000), _F32)
    return jnp.concatenate([lo, hi], axis=1)


def _postattn_body(yf_ref, ym_ref, gates_ref, x_ref, wfb_ref, wmb_ref, wout_ref, g_ref, wr_ref, br_ref,
                   x1_ref, h2_ref, route_ref, meta_ref, counts_ref, carry_ref, *, tm, d):
    step = pl.program_id(0)
    ya = jnp.dot(yf_ref[...], wfb_ref[...], preferred_element_type=_F32)
    yb = jnp.dot(ym_ref[...], wmb_ref[...], preferred_element_type=_F32)
    mixed = gates_ref[:, 0:d].astype(_F32) * ya + gates_ref[:, d:2 * d].astype(_F32) * yb
    x1 = x_ref[...] + jnp.dot(mixed.astype(_BF16), wout_ref[...], preferred_element_type=_F32)
    x1_ref[...] = x1
    h2 = _rms(x1, g_ref[...])
    h2_ref[...] = _pack_bf16_pair(h2[:, 0:d // 2], h2[:, d // 2:d])

    logits = jnp.dot(h2.astype(_BF16), wr_ref[...], preferred_element_type=_F32) + br_ref[...]
    logits_t = logits.T
    gl = logits_t[N_EXPERTS:N_EXPERTS + N_GROUPS]
    ge = jnp.exp(gl - jnp.max(gl, axis=0, keepdims=True))
    gp = ge / jnp.sum(ge, axis=0, keepdims=True)
    g_top = jnp.max(gp, axis=0, keepdims=True)
    g_row = lax.broadcasted_iota(jnp.int32, (N_GROUPS, tm), 0)
    g_idx = jnp.min(jnp.where(gp == g_top, g_row, N_GROUPS), axis=0, keepdims=True)
    fl = logits_t[0:EXPERTS_PER_GROUP]
    for g in range(1, N_GROUPS):
        fl = jnp.where(g_idx == g, logits_t[g * EXPERTS_PER_GROUP:(g + 1) * EXPERTS_PER_GROUP], fl)
    f_row = lax.broadcasted_iota(jnp.int32, (EXPERTS_PER_GROUP, tm), 0)
    f1 = jnp.max(fl, axis=0, keepdims=True)
    i1 = jnp.min(jnp.where(fl == f1, f_row, EXPERTS_PER_GROUP), axis=0, keepdims=True)
    fl2 = jnp.where(f_row == i1, -jnp.inf, fl)
    f2 = jnp.max(fl2, axis=0, keepdims=True)
    i2 = jnp.min(jnp.where(fl2 == f2, f_row, EXPERTS_PER_GROUP), axis=0, keepdims=True)
    e1 = g_idx * EXPERTS_PER_GROUP + i1
    e2 = g_idx * EXPERTS_PER_GROUP + i2
    t2 = jnp.exp(f2 - f1)
    w1 = g_top * (1.0 / (1.0 + t2))
    w2 = g_top * (t2 / (1.0 + t2))

    @pl.when(step == 0)
    def _():
        carry_ref[...] = jnp.zeros_like(carry_ref)

    x_row = lax.broadcasted_iota(jnp.int32, (N_EXPERTS, tm), 0)
    onehot = ((x_row == e1) | (x_row == e2)).astype(_F32)
    rr = lax.broadcasted_iota(jnp.int32, (tm, tm), 0)
    cc = lax.broadcasted_iota(jnp.int32, (tm, tm), 1)
    before = jnp.dot(onehot.astype(_BF16), (rr < cc).astype(_BF16), preferred_element_type=_F32)
    before = before + carry_ref[0:N_EXPERTS, 0:1]
    r1 = jnp.sum(jnp.where(x_row == e1, before, 0.0), axis=0, keepdims=True)
    r2 = jnp.sum(jnp.where(x_row == e2, before, 0.0), axis=0, keepdims=True)
    total = carry_ref[0:N_EXPERTS, 0:1] + jnp.sum(onehot, axis=1, keepdims=True)
    carry_ref[0:N_EXPERTS, :] = jnp.broadcast_to(total, (N_EXPERTS, LANES))
    counts_ref[...] = carry_ref[...]

    m_row = lax.broadcasted_iota(jnp.int32, (8, tm), 0)
    meta = jnp.where(m_row == 2, e1, 0)
    meta = jnp.where(m_row == 3, e2, meta)
    meta = jnp.where(m_row == 4, r1.astype(jnp.int32), meta)
    meta = jnp.where(m_row == 5, r2.astype(jnp.int32), meta)
    meta_ref[...] = meta
    w_row = lax.broadcasted_iota(jnp.int32, (LANES, tm), 0)
    route_ref[...] = jnp.where(w_row == 0, w1, jnp.where(w_row == 1, w2, 0.0)).T


def _postattn(yf, ym, gates, x, wfb, wmb, wout, g, wr, br, *, tm):
    t, d = x.shape
    width = yf.shape[1]
    row = lambda cols: pl.BlockSpec((tm, cols), lambda i: (i, 0))
    return pl.pallas_call(
        functools.partial(_postattn_body, tm=tm, d=d),
        grid=(t // tm,),
        in_specs=[row(width), row(width), row(2 * d), row(d),
                  _const_spec(wfb.shape), _const_spec(wmb.shape), _const_spec(wout.shape),
                  _const_spec((1, d)), _const_spec(wr.shape), _const_spec(br.shape)],
        out_specs=[row(d), row(d // 2), row(LANES), pl.BlockSpec((8, tm), lambda i: (0, i)),
                   _const_spec((LANES, LANES))],
        out_shape=[jax.ShapeDtypeStruct((t, d), _F32),
                   jax.ShapeDtypeStruct((t, d // 2), jnp.uint32),
                   jax.ShapeDtypeStruct((t, LANES), _F32),
                   jax.ShapeDtypeStruct((8, t), jnp.int32),
                   jax.ShapeDtypeStruct((LANES, LANES), _F32)],
        scratch_shapes=[pltpu.VMEM((LANES, LANES), _F32)],
        compiler_params=_params("arbitrary"),
        name="postattn",
    )(yf, ym, gates, x, wfb, wmb, wout, g, wr, br)


def _sc_scatter(rows, idx, n_out):
    n, width = rows.shape
    workers = SC_CORES * SC_SUBCORES
    per_worker = n // workers
    chunks = per_worker // SC_SCATTER_CHUNK
    assert n % (workers * SC_SCATTER_CHUNK) == 0 and idx.shape == (2, n // SC_SCATTER_CHUNK, SC_SCATTER_CHUNK)
    mesh = plsc.VectorSubcoreMesh(core_axis_name="c", subcore_axis_name="s",
                                  num_cores=SC_CORES, num_subcores=SC_SUBCORES)

    @functools.partial(
        pl.kernel, out_type=jax.ShapeDtypeStruct((n_out, width), rows.dtype), mesh=mesh,
        scratch_types=[pltpu.VMEM((2, chunks, SC_SCATTER_CHUNK), jnp.int32),
                       pltpu.VMEM((SC_SCATTER_CHUNK, width), rows.dtype),
                       pltpu.SemaphoreType.DMA],
        name="sc_scatter")
    def scatter(rows_hbm, idx_hbm, out_hbm, idx_v, rows_v, sem):
        worker = lax.axis_index("s") * SC_CORES + lax.axis_index("c")
        for k in range(2):
            pltpu.sync_copy(idx_hbm.at[k, pl.ds(worker * chunks, chunks)], idx_v.at[k])

        def chunk(j, carry):
            start = worker * per_worker + j * SC_SCATTER_CHUNK
            pltpu.sync_copy(rows_hbm.at[pl.ds(start, SC_SCATTER_CHUNK)], rows_v)
            for k in range(2):
                pltpu.async_copy(rows_v, out_hbm.at[idx_v.at[k, j]], sem).wait()
            return carry

        lax.fori_loop(0, chunks, chunk, 0)

    return scatter(rows, idx)


def _experts_body(be_ref, valid_ref, nused_ref, xs_ref, wg_ref, wu_ref, wd_ref, out_ref,
                  wgb_ref, wub_ref, wdb_ref):
    j = pl.program_id(0)
    used = j < nused_ref[0]

    @pl.when(jnp.logical_not(used))
    def _():
        out_ref[...] = jnp.zeros_like(out_ref)

    @pl.when(used & ((j == 0) | (be_ref[j] != be_ref[jnp.maximum(j - 1, 0)])))
    def _():
        wgb_ref[...] = wg_ref[...].astype(_BF16)
        wub_ref[...] = wu_ref[...].astype(_BF16)
        wdb_ref[...] = wd_ref[...].astype(_BF16)

    @pl.when(used)
    def _():
        row = lax.broadcasted_iota(jnp.int32, xs_ref.shape, 0)
        packed = jnp.where(row < valid_ref[j], xs_ref[...], jnp.uint32(0))
        rows = _unpack_bf16_pair(packed).astype(_BF16)
        hidden = wgb_ref.shape[1]
        out = None
        for c in range(0, hidden, hidden // EXPERT_PARTS):
            cols = slice(c, c + hidden // EXPERT_PARTS)
            gate = jnp.dot(rows, wgb_ref[:, cols], preferred_element_type=_F32)
            up = jnp.dot(rows, wub_ref[:, cols], preferred_element_type=_F32)
            act = ((gate * jax.nn.sigmoid(gate)) * up).astype(_BF16)
            part = jnp.dot(act, wdb_ref[cols, :], preferred_element_type=_F32)
            out = part if out is None else out + part
        half = out.shape[1] // 2
        out_ref[...] = _pack_bf16_pair(out[:, 0:half], out[:, half:])


def _experts(block_expert, block_valid, n_used, xs, wg, wu, wd, *, te):
    n_rows, half = xs.shape
    _, d, de = wg.shape

    def blk(j, be, bv, nu):
        return jnp.minimum(j, nu[0] - 1)

    def weight(j, be, bv, nu):
        return (be[blk(j, be, bv, nu)], 0, 0)

    return pl.pallas_call(
        _experts_body,
        grid_spec=pltpu.PrefetchScalarGridSpec(
            num_scalar_prefetch=3,
            grid=(n_rows // te,),
            in_specs=[pl.BlockSpec((te, half), lambda j, be, bv, nu: (blk(j, be, bv, nu), 0)),
                      pl.BlockSpec((None, d, de), weight),
                      pl.BlockSpec((None, d, de), weight),
                      pl.BlockSpec((None, de, d), weight)],
            out_specs=pl.BlockSpec((te, d // 2), lambda j, be, bv, nu: (j, 0)),
            scratch_shapes=[pltpu.VMEM((d, de), _BF16), pltpu.VMEM((d, de), _BF16),
                            pltpu.VMEM((de, d), _BF16)],
        ),
        out_shape=jax.ShapeDtypeStruct((n_rows, d // 2), jnp.uint32),
        compiler_params=_params("arbitrary"),
        name="experts",
    )(block_expert, block_valid, n_used, xs, wg, wu, wd)


def _sc_gather(table, idx):
    n, = idx.shape
    width = table.shape[1]
    workers = SC_CORES * SC_SUBCORES
    per_worker = n // workers
    chunks = per_worker // SC_CHUNK
    assert n % (workers * SC_CHUNK) == 0
    mesh = plsc.VectorSubcoreMesh(core_axis_name="c", subcore_axis_name="s",
                                  num_cores=SC_CORES, num_subcores=SC_SUBCORES)

    @functools.partial(
        pl.kernel, out_type=jax.ShapeDtypeStruct((n, width), table.dtype), mesh=mesh,
        scratch_types=[pltpu.VMEM((per_worker,), jnp.int32),
                       pltpu.VMEM((SC_CHUNK, width), table.dtype),
                       pltpu.SemaphoreType.DMA],
        name="sc_gather")
    def gather(table_hbm, idx_hbm, out_hbm, idx_v, rows_v, sem):
        worker = lax.axis_index("s") * SC_CORES + lax.axis_index("c")
        base = worker * per_worker
        pltpu.sync_copy(idx_hbm.at[pl.ds(base, per_worker)], idx_v)

        def chunk(j, carry):
            off = pl.multiple_of(j * SC_CHUNK, SC_CHUNK)
            pltpu.async_copy(table_hbm.at[idx_v.at[pl.ds(off, SC_CHUNK)]], rows_v, sem).wait()
            pltpu.sync_copy(rows_v, out_hbm.at[pl.ds(base + off, SC_CHUNK)])
            return carry

        lax.fori_loop(0, chunks, chunk, 0)

    return gather(table, idx)


def _final_body(x1_ref, route_ref, p_ref, r1_ref, r2_ref, gp_ref, wpg_ref, wpp_ref, gf_ref, o_ref):
    emb = jnp.dot(p_ref[...].astype(_BF16), wpp_ref[...], preferred_element_type=_F32)
    route = route_ref[...]
    x2 = x1_ref[...] + (_unpack_bf16_pair(r1_ref[...]) * route[:, 0:1]
                        + _unpack_bf16_pair(r2_ref[...]) * route[:, 1:2])
    gate = jax.nn.sigmoid(jnp.dot(_rms(x2, gp_ref[...]).astype(_BF16), wpg_ref[...],
                                  preferred_element_type=_F32))
    o_ref[...] = _rms(x2 + gate * emb, gf_ref[...])


def _final(x1, route, p, slot_rows, gp, wpg, wpp, gf, *, tm):
    t, d = x1.shape
    ple = p.shape[1]
    steps = t // tm
    row = lambda cols: pl.BlockSpec((tm, cols), lambda i: (i, 0))
    return pl.pallas_call(
        _final_body,
        grid=(steps,),
        in_specs=[row(d), row(LANES), row(ple), row(d // 2),
                  pl.BlockSpec((tm, d // 2), lambda i: (steps + i, 0)),
                  _const_spec((1, d)), _const_spec(wpg.shape), _const_spec(wpp.shape),
                  _const_spec((1, d))],
        out_specs=row(d),
        out_shape=jax.ShapeDtypeStruct((t, d), _F32),
        compiler_params=_params("arbitrary"),
        name="final",
    )(x1, route, p, slot_rows, slot_rows, gp, wpg, wpp, gf)


def _rope_tables(s):
    half = HEAD_DIM // 2
    inv = 1.0 / (ROPE_THETA ** (jnp.arange(0, HEAD_DIM, 2, dtype=_F32) / HEAD_DIM))
    ang = jnp.arange(s, dtype=_F32)[:, None] * inv[None, :]
    cos, sin = jnp.cos(ang), jnp.sin(ang)
    reps = LANES // half
    cos_t = jnp.tile(cos, (1, reps))
    sin_t = jnp.tile(jnp.concatenate([-sin, sin], axis=1), (1, reps // 2))
    return cos_t, sin_t


def _layer(x, p, attn_norm, w_in, b_forget, w_fox_branch, w_moba_branch, w_out, moe_norm,
           w_group, b_group, w_fine, b_fine, w_gate, w_up, w_down, ple_norm, w_ple_gate, w_ple_proj,
           final_gain):
    b, s, d = x.shape
    t = b * s
    width = N_FOX_HEADS * HEAD_DIM
    assert N_MOBA_HEADS * HEAD_DIM == width and MOBA_BLOCK % ATTN_Q == 0 and MOBA_PIPELINE[0] % MOBA_BLOCK == 0
    assert BIAS_PIECES * N_FOX_HEADS <= LANES and s % MOBA_PIPELINE[0] == 0
    assert s % ROW_TILE == 0 and ROW_TILE % MOBA_BLOCK == 0 and s // MOBA_BLOCK <= LANES
    assert t % FINAL_TILE == 0 and N_EXPERTS + N_GROUPS <= LANES

    qkv_cols = 6 * width
    f_cols = N_FOX_HEADS
    chunk = lambda c: w_in[:, c * width:(c + 1) * width]
    w_rows = jnp.concatenate([chunk(1), chunk(4), w_in[:, qkv_cols + f_cols:]], axis=1).astype(_BF16)
    w_cols = jnp.concatenate([chunk(0), chunk(2), chunk(3), chunk(5)], axis=1).T.astype(_BF16)
    wf = jnp.zeros((d, LANES), _BF16).at[:, :f_cols].set(w_in[:, qkv_cols:qkv_cols + f_cols].astype(_BF16))
    bfp = jnp.zeros((1, LANES), _F32).at[0, :f_cols].set(b_forget.astype(_F32))
    cos, sin = _rope_tables(s)

    qft, kf, vft, qmt, km, vmt, gates, cb, kmean, kabs, bmax = _inproj(
        x, attn_norm.reshape(1, d), w_rows, w_cols, wf, bfp, cos, sin, cos.T, sin.T, tm=ROW_TILE)
    y_fox = _attention(qft, kf, cb, vft, kabs, bmax, mode="fox")
    block_of_key = jnp.arange(s, dtype=jnp.int32)[:, None] // MOBA_BLOCK
    block_onehot = (block_of_key == jnp.arange(LANES, dtype=jnp.int32)[None, :]).astype(_BF16)
    y_moba = _attention(qmt, km, block_onehot, vmt, kmean, bmax, mode="moba")

    wr = jnp.zeros((d, LANES), _BF16)
    wr = wr.at[:, :N_EXPERTS].set(w_fine.astype(_BF16))
    wr = wr.at[:, N_EXPERTS:N_EXPERTS + N_GROUPS].set(w_group.astype(_BF16))
    br = jnp.zeros((1, LANES), _F32)
    br = br.at[0, :N_EXPERTS].set(b_fine.astype(_F32))
    br = br.at[0, N_EXPERTS:N_EXPERTS + N_GROUPS].set(b_group.astype(_F32))
    x1, h2p, route, meta, counts = _postattn(
        y_fox.reshape(t, width), y_moba.reshape(t, width), gates.reshape(t, 2 * d), x.reshape(t, d),
        w_fox_branch.astype(_BF16), w_moba_branch.astype(_BF16), w_out.astype(_BF16),
        moe_norm.reshape(1, d), wr, br, tm=ROW_TILE)

    te = EXPERT_TILE
    n_blk = (2 * t) // te + N_EXPERTS
    counts = counts[:N_EXPERTS, 0].astype(jnp.int32)
    blocks_per = (counts + te - 1) // te
    block_end = jnp.cumsum(blocks_per)
    row_start = (block_end - blocks_per) * te
    expert_ids = jnp.arange(N_EXPERTS, dtype=jnp.int32)[:, None, None]
    slot_major = meta[4:6] + jnp.sum(
        jnp.where(meta[2:4][None] == expert_ids, row_start[:, None, None], 0), axis=0)
    n_used = block_end[-1:].astype(jnp.int32)
    block_ids = jnp.arange(n_blk, dtype=jnp.int32)
    block_expert = jnp.minimum(
        jnp.sum((block_ids[:, None] >= block_end[None, :]).astype(jnp.int32), axis=1), N_EXPERTS - 1)

    first_block = block_end - blocks_per
    block_valid = jnp.clip(counts[block_expert] - (block_ids - first_block[block_expert]) * te, 0, te)
    xs = _sc_scatter(h2p, slot_major.reshape(2, t // SC_SCATTER_CHUNK, SC_SCATTER_CHUNK), n_blk * te)
    rows = _experts(block_expert, block_valid, n_used, xs, w_gate, w_up, w_down, te=te)
    slot_rows = _sc_gather(rows, slot_major.reshape(-1))
    out = _final(x1, route, p.reshape(t, -1), slot_rows, ple_norm.reshape(1, d),
                 w_ple_gate.astype(_BF16), w_ple_proj.astype(_BF16), final_gain.reshape(1, d),
                 tm=FINAL_TILE)
    return out.reshape(b, s, d)


def kernel(x, p, attn_norm, w_in, b_forget, w_fox_branch, w_moba_branch, w_out, moe_norm, w_group,
           b_group, w_fine, b_fine, w_gate, w_up, w_down, ple_norm, w_ple_gate, w_ple_proj, final_norm):
    depth = p.shape[0]
    assert depth == 1, "the final norm is fused into the last layer's kernel"
    i = 0
    return _layer(x, p[i], attn_norm[i], w_in[i], b_forget[i], w_fox_branch[i], w_moba_branch[i],
                  w_out[i], moe_norm[i], w_group[i], b_group[i], w_fine[i], b_fine[i], w_gate[i],
                  w_up[i], w_down[i], ple_norm[i], w_ple_gate[i], w_ple_proj[i], final_norm)
```

```python
import functools

import jax
import jax.numpy as jnp
from jax import lax
from jax.experimental import pallas as pl
from jax.experimental.pallas import tpu as pltpu
from jax.experimental.pallas import tpu_sc as plsc

HEAD_DIM = 64
N_FOX_HEADS = 8
N_MOBA_HEADS = 8
MOBA_BLOCK = 256
MOBA_TOPK = 3
ROPE_THETA = 10000.0
N_GROUPS = 4
EXPERTS_PER_GROUP = 8
N_EXPERTS = N_GROUPS * EXPERTS_PER_GROUP
RMS_EPS = 1e-6

LANES = 128
HEADS_PER_STEP = LANES // HEAD_DIM
ATTN_Q = 256
FOX_PIPELINE = (256, 8, 4)
MOBA_PIPELINE = (256, 8, 4)
ONES_ROWS = 16
SOFTMAX_ROWS = 64
BIAS_PIECES = 3
ROW_TILE = 512
SC_CORES = 2
SC_SUBCORES = 16
SC_CHUNK = 64
SC_SCATTER_CHUNK = 128
FINAL_TILE = 512
EXPERT_PARTS = 2
EXPERT_TILE = 512
LOG2_E = 1.4426950408889634
UNDERFLOW_LOG2 = 160.0
NEG_BIG = -1e30
VMEM_LIMIT = 48 * 1024 * 1024

_BF16 = jnp.bfloat16
_F32 = jnp.float32


def _params(*sem):
    return pltpu.CompilerParams(dimension_semantics=sem, vmem_limit_bytes=VMEM_LIMIT)


def _rms(x, g):
    return x * lax.rsqrt(jnp.mean(x * x, axis=-1, keepdims=True) + RMS_EPS) * g


def _const_spec(shape):
    return pl.BlockSpec(shape, lambda *_: (0,) * len(shape))


def _inproj_body(x_ref, g_ref, w_ref, wt_ref, wf_ref, bf_ref, cos_ref, sin_ref, cost_ref, sint_ref,
                 qft_ref, kf_ref, vft_ref, qmt_ref, km_ref, vmt_ref, gates_ref, cb_ref, kmean_ref,
                 kabs_ref, bmax_ref,
                 carry_ref, *, tm, width, gate_cols):
    j = pl.program_id(1)
    h = _rms(x_ref[...], g_ref[...])
    hb = h.astype(_BF16)
    hbt = h.T.astype(_BF16)
    reps = width // LANES
    half = HEAD_DIM // 2

    def proj(c):
        return jnp.dot(hb, w_ref[:, c * width:(c + 1) * width], preferred_element_type=_F32)

    def proj_t(c):
        return jnp.dot(wt_ref[c * width:(c + 1) * width, :], hbt, preferred_element_type=_F32)

    def rope(t):
        cos = jnp.concatenate([cos_ref[...]] * reps, axis=1)
        sin = jnp.concatenate([sin_ref[...]] * reps, axis=1)
        first = (lax.broadcasted_iota(jnp.int32, (1, width), 1) % HEAD_DIM) < half
        partner = jnp.where(first, pltpu.roll(t, width - half, 1), pltpu.roll(t, half, 1))
        return t * cos + partner * sin

    def rope_t(t):
        cos = jnp.concatenate([cost_ref[...]] * reps, axis=0)
        sin = jnp.concatenate([sint_ref[...]] * reps, axis=0)
        first = (lax.broadcasted_iota(jnp.int32, (width, 1), 0) % HEAD_DIM) < half
        partner = jnp.where(first, pltpu.roll(t, width - half, 0), pltpu.roll(t, half, 0))
        return t * cos + partner * sin

    scale = HEAD_DIM ** -0.5 * LOG2_E
    qft_ref[...] = (proj_t(0) * scale).astype(_BF16)
    vft_ref[...] = proj_t(1).astype(_BF16)
    qmt_ref[...] = (rope_t(proj_t(2)) * scale).astype(_BF16)
    vmt_ref[...] = proj_t(3).astype(_BF16)
    kf = proj(0).astype(_BF16)
    kf_ref[...] = kf
    kf_abs = jnp.abs(kf.astype(_F32))
    km = rope(proj(1))
    km_ref[...] = km.astype(_BF16)

    @pl.when(j == 0)
    def _():
        kmean_ref[...] = jnp.zeros_like(kmean_ref)
        kabs_ref[...] = jnp.zeros_like(kabs_ref)
        bmax_ref[...] = jnp.zeros_like(bmax_ref)
        carry_ref[...] = jnp.zeros_like(carry_ref)

    blocks = tm // MOBA_BLOCK
    for b in range(blocks):
        rows = slice(b * MOBA_BLOCK, (b + 1) * MOBA_BLOCK)
        kmean_ref[pl.ds(j * blocks + b, 1), :] = jnp.mean(km[rows], axis=0, keepdims=True)
        kabs_ref[pl.ds(j * blocks + b, 1), :] = jnp.max(kf_abs[rows], axis=0, keepdims=True)

    for c in range(gate_cols // width):
        g = jnp.dot(hb, w_ref[:, (2 + c) * width:(3 + c) * width],
                    preferred_element_type=_F32)
        gates_ref[:, c * width:(c + 1) * width] = jax.nn.sigmoid(g).astype(_BF16)

    z = jnp.dot(hb, wf_ref[...], preferred_element_type=_F32) + bf_ref[...]
    log_f = jnp.minimum(z, 0.0) - jnp.log1p(jnp.exp(-jnp.abs(z)))
    r = lax.broadcasted_iota(jnp.int32, (LANES, LANES), 0)
    c = lax.broadcasted_iota(jnp.int32, (LANES, LANES), 1)
    tri = (c <= r).astype(_BF16)
    offset = carry_ref[0:1, :]
    blocks_cum = []
    for blk in range(tm // LANES):
        rest = log_f[blk * LANES:(blk + 1) * LANES]
        within = jnp.zeros((LANES, LANES), _F32)
        for _ in range(BIAS_PIECES):
            part = rest.astype(_BF16)
            rest = rest - part.astype(_F32)
            within = within + jnp.dot(tri, part, preferred_element_type=_F32)
        blocks_cum.append(within + offset)
        offset = offset + within[LANES - 1:LANES, :]
    cum = jnp.concatenate(blocks_cum, axis=0)
    carry_ref[...] = jnp.broadcast_to(offset, carry_ref.shape)

    head_lane = lax.broadcasted_iota(jnp.int32, (1, LANES), 1) < N_FOX_HEADS
    bias = jnp.where(head_lane, -cum * LOG2_E, 0.0)
    for b in range(blocks):
        bmax_ref[pl.ds(j * blocks + b, 1), :] = jnp.max(
            bias[b * MOBA_BLOCK:(b + 1) * MOBA_BLOCK], axis=0, keepdims=True)
    rest = bias
    placed = jnp.zeros((tm, LANES), _F32)
    for piece in range(BIAS_PIECES):
        part = rest.astype(_BF16).astype(_F32)
        rest = rest - part
        placed = placed + (pltpu.roll(part, N_FOX_HEADS * piece, 1) if piece else part)
    cb_ref[...] = placed.astype(_BF16)


def _inproj(x, g, w, wt, wf, bfp, cos, sin, cos_t, sin_t, *, tm):
    b, s, d = x.shape
    width = N_FOX_HEADS * HEAD_DIM
    gate_cols = w.shape[1] - 2 * width
    act = jax.ShapeDtypeStruct((b, s, width), _BF16)
    act_t = jax.ShapeDtypeStruct((b, width, s), _BF16)
    act_spec = pl.BlockSpec((None, tm, width), lambda bi, j: (bi, j, 0))
    act_t_spec = pl.BlockSpec((None, width, tm), lambda bi, j: (bi, 0, j))
    return pl.pallas_call(
        functools.partial(_inproj_body, tm=tm, width=width, gate_cols=gate_cols),
        grid=(b, s // tm),
        in_specs=[
            pl.BlockSpec((None, tm, d), lambda bi, j: (bi, j, 0)),
            _const_spec((1, d)),
            _const_spec(w.shape),
            _const_spec(wt.shape),
            _const_spec(wf.shape),
            _const_spec(bfp.shape),
            pl.BlockSpec((tm, LANES), lambda bi, j: (j, 0)),
            pl.BlockSpec((tm, LANES), lambda bi, j: (j, 0)),
            pl.BlockSpec((LANES, tm), lambda bi, j: (0, j)),
            pl.BlockSpec((LANES, tm), lambda bi, j: (0, j)),
        ],
        out_specs=[act_t_spec, act_spec, act_t_spec, act_t_spec, act_spec, act_t_spec,
                   pl.BlockSpec((None, tm, gate_cols), lambda bi, j: (bi, j, 0)),
                   pl.BlockSpec((None, tm, LANES), lambda bi, j: (bi, j, 0)),
                   pl.BlockSpec((None, LANES, width), lambda bi, j: (bi, 0, 0)),
                   pl.BlockSpec((None, LANES, width), lambda bi, j: (bi, 0, 0)),
                   pl.BlockSpec((None, LANES, LANES), lambda bi, j: (bi, 0, 0))],
        out_shape=[act_t, act, act_t, act_t, act, act_t,
                   jax.ShapeDtypeStruct((b, s, gate_cols), _BF16),
                   jax.ShapeDtypeStruct((b, s, LANES), _BF16),
                   jax.ShapeDtypeStruct((b, LANES, width), _F32),
                   jax.ShapeDtypeStruct((b, LANES, width), _F32),
                   jax.ShapeDtypeStruct((b, LANES, LANES), _F32)],
        scratch_shapes=[pltpu.VMEM((8, LANES), _F32)],
        compiler_params=_params("arbitrary", "arbitrary"),
        name="inproj",
    )(x, g, w, wt, wf, bfp, cos, sin, cos_t, sin_t)


def _attn_body(uq_ref, uk_ref, qt_ref, k_ref, kx_ref, vt_ref, aux_ref, aux2_ref, o_ref,
               w_ref, m_ref, l_ref, acc_ref, st_ref, p_ref, alpha_ref, pmax_ref, qn_ref, *,
               mode, tq, tk, nb, n_diag, n_units, pipe, lag):
    hp = pl.program_id(1)
    wide = HEADS_PER_STEP * tq
    nq = w_ref.shape[0]
    feat = lax.broadcasted_iota(jnp.int32, (LANES, 1), 0)
    xr = lax.broadcasted_iota(jnp.int32, (LANES, wide), 0)
    xc = lax.broadcasted_iota(jnp.int32, (LANES, wide), 1)

    def setup(i, carry):
        qt = qt_ref[:, pl.ds(pl.multiple_of(i * tq, tq), tq)]
        zero = jnp.zeros_like(qt)
        wq = jnp.concatenate([jnp.where(feat < HEAD_DIM, qt, zero),
                              jnp.where(feat < HEAD_DIM, zero, qt)], axis=1)
        if mode == "fox":
            head = hp * HEADS_PER_STEP + xc // tq
            lower = ((xr % N_FOX_HEADS == head) & (xr < BIAS_PIECES * N_FOX_HEADS)).astype(_BF16)
            qn_ref[i] = jnp.broadcast_to(
                jnp.sum(jnp.abs(wq.astype(_F32)), axis=0, keepdims=True), qn_ref.shape[1:])
        else:
            own = (i * tq) // MOBA_BLOCK
            gate = jnp.dot(aux_ref[...].astype(_BF16), wq, preferred_element_type=_F32)[0:nb]
            blk = lax.broadcasted_iota(jnp.int32, (nb, wide), 0)
            gate = jnp.where(blk < own, gate, -jnp.inf)
            keep = blk == own
            for _ in range(MOBA_TOPK):
                mx = jnp.max(gate, axis=0, keepdims=True)
                cand = jnp.where((gate == mx) & (mx > -jnp.inf), blk, nb)
                pick = blk == jnp.min(cand, axis=0, keepdims=True)
                keep = keep | pick
                gate = jnp.where(pick, -jnp.inf, gate)
            lower = jnp.where(keep, 0.0, NEG_BIG)
            if nb < LANES:
                lower = jnp.concatenate([lower, jnp.zeros((LANES - nb, wide), _F32)], axis=0)
            lower = lower.astype(_BF16)
        w_ref[i] = jnp.concatenate([wq, lower], axis=0)
        m_ref[i] = jnp.full(m_ref.shape[1:], NEG_BIG, _F32)
        l_ref[i] = jnp.zeros(l_ref.shape[1:], _F32)
        acc_ref[i] = jnp.zeros(acc_ref.shape[1:], _F32)
        return carry

    lax.fori_loop(0, nq, setup, 0)

    def unit(t):
        return uq_ref[t], uk_ref[t]

    def scores(t, slot, causal):
        q, kb = unit(t)
        start = pl.multiple_of(kb * tk, tk)
        ka = jnp.concatenate([k_ref[pl.ds(start, tk), :], kx_ref[pl.ds(start, tk), :]], axis=1)
        st = jnp.dot(ka, w_ref[q], preferred_element_type=_F32)
        if causal:
            key = start + lax.broadcasted_iota(jnp.int32, (tk, wide), 0)
            qry = q * tq + lax.broadcasted_iota(jnp.int32, (tk, wide), 1) % tq
            st = jnp.where(key <= qry, st, NEG_BIG)
        st_ref[slot] = st
        pmax_ref[slot] = jnp.max(st.reshape(tk // 8, 8, wide), axis=0)

    def softmax(t, slot):
        q, _ = unit(t)
        rows = SOFTMAX_ROWS
        m_prev = m_ref[q][0:1, :]
        m_new = jnp.maximum(m_prev, jnp.max(pmax_ref[slot], axis=0, keepdims=True))
        for r in range(0, tk, rows):
            p_ref[slot, r:r + rows, :] = jnp.exp2(st_ref[slot, r:r + rows, :] - m_new).astype(_BF16)
        m_ref[q] = jnp.broadcast_to(m_new, m_ref.shape[1:])
        alpha_ref[slot] = jnp.broadcast_to(jnp.exp2(m_prev - m_new), alpha_ref.shape[1:])

    def values(t, slot):
        q, kb = unit(t)
        start = pl.multiple_of(kb * tk, tk)
        ones = jnp.ones((ONES_ROWS, tk), _BF16)
        for h in range(HEADS_PER_STEP):
            cols = slice(h * tq, (h + 1) * tq)
            feats = slice(h * HEAD_DIM, (h + 1) * HEAD_DIM)
            lhs = jnp.concatenate([vt_ref[feats, pl.ds(start, tk)], ones], axis=0)
            pv = jnp.dot(lhs, p_ref[slot, :, cols], preferred_element_type=_F32)
            alpha = alpha_ref[slot, 0:1, cols]
            acc_ref[q, feats, :] = acc_ref[q, feats, :] * alpha + pv[0:HEAD_DIM]
            l_ref[q, :, cols] = jnp.broadcast_to(
                alpha * l_ref[q, 0:1, cols] + pv[HEAD_DIM:HEAD_DIM + 1], (8, tq))

    def step(base, c, causal, do_scores=True, do_softmax=True, do_values=True):
        u = base + c
        if do_scores:
            scores(u, c % pipe, causal)
        if do_values:
            values(u - 2 * lag, (c - 2 * lag) % pipe)
        if do_softmax:
            softmax(u - lag, (c - lag) % pipe)

    def group(causal):
        def body(j, carry):
            for c in range(pipe):
                step(j * pipe, c, causal)
            return carry
        return body

    def pipeline(g_first, g_end, causal_groups):
        for c in range(pipe):
            step(g_first * pipe, c, g_first < causal_groups,
                 do_softmax=c >= lag, do_values=c >= 2 * lag)
        if g_first + 1 < causal_groups:
            lax.fori_loop(g_first + 1, causal_groups, group(True), 0)
        lax.fori_loop(max(g_first + 1, causal_groups), g_end, group(False), 0)
        for c in range(2 * lag):
            step(g_end * pipe, c, False, do_scores=False, do_softmax=c < lag)

    diag_groups = n_diag // pipe
    if mode == "moba":
        pipeline(0, n_units // pipe, diag_groups)
    else:
        pipeline(0, diag_groups, diag_groups)
        nq_blocks = n_diag
        col = lax.broadcasted_iota(jnp.int32, (LANES, LANES), 1)
        first_head = lax.broadcasted_iota(jnp.int32, (1, wide), 1) < tq
        head0 = hp * HEADS_PER_STEP

        def per_head(table, pick0, pick1, fill):
            v0 = jnp.max(jnp.where(pick0, table, fill), axis=1, keepdims=True)[0:nb]
            v1 = jnp.max(jnp.where(pick1, table, fill), axis=1, keepdims=True)[0:nb]
            return jnp.where(first_head, v0, v1)

        bias_max = per_head(aux2_ref[...], col == head0, col == head0 + 1, NEG_BIG)
        k_absmax = per_head(aux_ref[...], col < HEAD_DIM, col >= HEAD_DIM, 0.0)
        blk = lax.broadcasted_iota(jnp.int32, (nb, 1), 0)

        needed = jnp.zeros((nb, 1), jnp.int32)
        for i in range(1, nq_blocks):
            bound = bias_max + qn_ref[i, 0:1, :] * k_absmax - m_ref[i, 0:1, :]
            live = jnp.max(bound, axis=1, keepdims=True) >= -UNDERFLOW_LOG2
            needed = jnp.maximum(needed, jnp.where(live & (blk < i), i - blk, 0))
        far = jnp.max(needed)
        n_live = far * nq_blocks - (far * (far + 1)) // 2
        live_groups = (n_live + pipe - 1) // pipe

        @pl.when(live_groups > 0)
        def _():
            pipeline(diag_groups, diag_groups + live_groups, diag_groups)

    def finish(i, carry):
        out_t = jnp.concatenate(
            [acc_ref[i, h * HEAD_DIM:(h + 1) * HEAD_DIM, :] / l_ref[i, 0:1, h * tq:(h + 1) * tq]
             for h in range(HEADS_PER_STEP)], axis=0)
        o_ref[pl.ds(pl.multiple_of(i * tq, tq), tq), :] = out_t.T.astype(o_ref.dtype)
        return carry

    lax.fori_loop(0, nq, finish, 0, unroll=2)


def _attention(qt, k, kx, vt, aux, aux2, *, mode):
    b, width, s = qt.shape
    tq = ATTN_Q
    tk, pipe, lag = FOX_PIPELINE if mode == "fox" else MOBA_PIPELINE
    nq = s // tq
    nb = s // MOBA_BLOCK
    wide = HEADS_PER_STEP * tq
    diag = [(i * tq) // tk for i in range(nq)]
    units = list(enumerate(diag))
    units += [(i, diag[i] - dist) for dist in range(1, nq) for i in range(nq) if diag[i] >= dist]
    assert mode != "fox" or tq == tk
    assert nq % pipe == 0 and len(units) % pipe == 0 and pipe >= 2 * lag
    unit_q = jnp.asarray([u[0] for u in units], jnp.int32)
    unit_k = jnp.asarray([u[1] for u in units], jnp.int32)
    aux_spec = pl.BlockSpec((None, LANES, LANES), lambda bi, hp, uq, uk: (bi, 0, hp))
    aux2_spec = pl.BlockSpec((None, LANES, LANES), lambda bi, hp, uq, uk: (bi, 0, 0))
    if mode == "fox":
        kx_spec = pl.BlockSpec((None, s, LANES), lambda bi, hp, uq, uk: (bi, 0, 0))
    else:
        kx_spec = pl.BlockSpec(kx.shape, lambda bi, hp, uq, uk: (0, 0))
    return pl.pallas_call(
        functools.partial(_attn_body, mode=mode, tq=tq, tk=tk, nb=nb, n_diag=nq, n_units=len(units),
                          pipe=pipe, lag=lag),
        grid_spec=pltpu.PrefetchScalarGridSpec(
            num_scalar_prefetch=2,
            grid=(b, width // LANES),
            in_specs=[
                pl.BlockSpec((None, LANES, s), lambda bi, hp, uq, uk: (bi, hp, 0)),
                pl.BlockSpec((None, s, LANES), lambda bi, hp, uq, uk: (bi, 0, hp)),
                kx_spec,
                pl.BlockSpec((None, LANES, s), lambda bi, hp, uq, uk: (bi, hp, 0)),
                aux_spec,
                aux2_spec,
            ],
            out_specs=pl.BlockSpec((None, s, LANES), lambda bi, hp, uq, uk: (bi, 0, hp)),
            scratch_shapes=[
                pltpu.VMEM((nq, 2 * LANES, wide), _BF16),
                pltpu.VMEM((nq, 8, wide), _F32),
                pltpu.VMEM((nq, 8, wide), _F32),
                pltpu.VMEM((nq, LANES, tq), _F32),
                pltpu.VMEM((pipe, tk, wide), _F32),
                pltpu.VMEM((pipe, tk, wide), _BF16),
                pltpu.VMEM((pipe, 8, wide), _F32),
                pltpu.VMEM((pipe, 8, wide), _F32),
                pltpu.VMEM((nq, 8, wide), _F32),
            ],
        ),
        out_shape=jax.ShapeDtypeStruct((b, s, width), _BF16),
        compiler_params=_params("arbitrary", "arbitrary"),
        name=mode,
    )(unit_q, unit_k, qt, k, kx, vt, aux, aux2)


def _pack_bf16_pair(a, b):
    lo = pltpu.bitcast(a.astype(_BF16).astype(_F32), jnp.uint32) >> 16
    hi = pltpu.bitcast(b.astype(_BF16).astype(_F32), jnp.uint32) & jnp.uint32(0xFFFF0000)
    return lo | hi


def _unpack_bf16_pair(u):
    lo = pltpu.bitcast(u << 16, _F32)
    hi = pltpu.bitcast(u & jnp.uint32(0xFFFF0000), _F32)
    return jnp.concatenate([lo, hi], axis=1)


def _postattn_body(yf_ref, ym_ref, gates_ref, x_ref, wfb_ref, wmb_ref, wout_ref, g_ref, wr_ref, br_ref,
                   x1_ref, h2_ref, route_ref, meta_ref, counts_ref, carry_ref, *, tm, d):
    step = pl.program_id(0)
    ya = jnp.dot(yf_ref[...], wfb_ref[...], preferred_element_type=_F32)
    yb = jnp.dot(ym_ref[...], wmb_ref[...], preferred_element_type=_F32)
    mixed = gates_ref[:, 0:d].astype(_F32) * ya + gates_ref[:, d:2 * d].astype(_F32) * yb
    x1 = x_ref[...] + jnp.dot(mixed.astype(_BF16), wout_ref[...], preferred_element_type=_F32)
    x1_ref[...] = x1
    h2 = _rms(x1, g_ref[...])
    h2_ref[...] = _pack_bf16_pair(h2[:, 0:d // 2], h2[:, d // 2:d])

    logits = jnp.dot(h2.astype(_BF16), wr_ref[...], preferred_element_type=_F32) + br_ref[...]
    logits_t = logits.T
    gl = logits_t[N_EXPERTS:N_EXPERTS + N_GROUPS]
    ge = jnp.exp(gl - jnp.max(gl, axis=0, keepdims=True))
    gp = ge / jnp.sum(ge, axis=0, keepdims=True)
    g_top = jnp.max(gp, axis=0, keepdims=True)
    g_row = lax.broadcasted_iota(jnp.int32, (N_GROUPS, tm), 0)
    g_idx = jnp.min(jnp.where(gp == g_top, g_row, N_GROUPS), axis=0, keepdims=True)
    fl = logits_t[0:EXPERTS_PER_GROUP]
    for g in range(1, N_GROUPS):
        fl = jnp.where(g_idx == g, logits_t[g * EXPERTS_PER_GROUP:(g + 1) * EXPERTS_PER_GROUP], fl)
    f_row = lax.broadcasted_iota(jnp.int32, (EXPERTS_PER_GROUP, tm), 0)
    f1 = jnp.max(fl, axis=0, keepdims=True)
    i1 = jnp.min(jnp.where(fl == f1, f_row, EXPERTS_PER_GROUP), axis=0, keepdims=True)
    fl2 = jnp.where(f_row == i1, -jnp.inf, fl)
    f2 = jnp.max(fl2, axis=0, keepdims=True)
    i2 = jnp.min(jnp.where(fl2 == f2, f_row, EXPERTS_PER_GROUP), axis=0, keepdims=True)
    e1 = g_idx * EXPERTS_PER_GROUP + i1
    e2 = g_idx * EXPERTS_PER_GROUP + i2
    t2 = jnp.exp(f2 - f1)
    w1 = g_top * (1.0 / (1.0 + t2))
    w2 = g_top * (t2 / (1.0 + t2))

    @pl.when(step == 0)
    def _():
        carry_ref[...] = jnp.zeros_like(carry_ref)

    x_row = lax.broadcasted_iota(jnp.int32, (N_EXPERTS, tm), 0)
    onehot = ((x_row == e1) | (x_row == e2)).astype(_F32)
    rr = lax.broadcasted_iota(jnp.int32, (tm, tm), 0)
    cc = lax.broadcasted_iota(jnp.int32, (tm, tm), 1)
    before = jnp.dot(onehot.astype(_BF16), (rr < cc).astype(_BF16), preferred_element_type=_F32)
    before = before + carry_ref[0:N_EXPERTS, 0:1]
    r1 = jnp.sum(jnp.where(x_row == e1, before, 0.0), axis=0, keepdims=True)
    r2 = jnp.sum(jnp.where(x_row == e2, before, 0.0), axis=0, keepdims=True)
    total = carry_ref[0:N_EXPERTS, 0:1] + jnp.sum(onehot, axis=1, keepdims=True)
    carry_ref[0:N_EXPERTS, :] = jnp.broadcast_to(total, (N_EXPERTS, LANES))
    counts_ref[...] = carry_ref[...]

    m_row = lax.broadcasted_iota(jnp.int32, (8, tm), 0)
    meta = jnp.where(m_row == 2, e1, 0)
    meta = jnp.where(m_row == 3, e2, meta)
    meta = jnp.where(m_row == 4, r1.astype(jnp.int32), meta)
    meta = jnp.where(m_row == 5, r2.astype(jnp.int32), meta)
    meta_ref[...] = meta
    w_row = lax.broadcasted_iota(jnp.int32, (LANES, tm), 0)
    route_ref[...] = jnp.where(w_row == 0, w1, jnp.where(w_row == 1, w2, 0.0)).T


def _postattn(yf, ym, gates, x, wfb, wmb, wout, g, wr, br, *, tm):
    t, d = x.shape
    width = yf.shape[1]
    row = lambda cols: pl.BlockSpec((tm, cols), lambda i: (i, 0))
    return pl.pallas_call(
        functools.partial(_postattn_body, tm=tm, d=d),
        grid=(t // tm,),
        in_specs=[row(width), row(width), row(2 * d), row(d),
                  _const_spec(wfb.shape), _const_spec(wmb.shape), _const_spec(wout.shape),
                  _const_spec((1, d)), _const_spec(wr.shape), _const_spec(br.shape)],
        out_specs=[row(d), row(d // 2), row(LANES), pl.BlockSpec((8, tm), lambda i: (0, i)),
                   _const_spec((LANES, LANES))],
        out_shape=[jax.ShapeDtypeStruct((t, d), _F32),
                   jax.ShapeDtypeStruct((t, d // 2), jnp.uint32),
                   jax.ShapeDtypeStruct((t, LANES), _F32),
                   jax.ShapeDtypeStruct((8, t), jnp.int32),
                   jax.ShapeDtypeStruct((LANES, LANES), _F32)],
        scratch_shapes=[pltpu.VMEM((LANES, LANES), _F32)],
        compiler_params=_params("arbitrary"),
        name="postattn",
    )(yf, ym, gates, x, wfb, wmb, wout, g, wr, br)


def _sc_scatter(rows, idx, n_out):
    n, width = rows.shape
    workers = SC_CORES * SC_SUBCORES
    per_worker = n // workers
    chunks = per_worker // SC_SCATTER_CHUNK
    assert n % (workers * SC_SCATTER_CHUNK) == 0 and idx.shape == (2, n // SC_SCATTER_CHUNK, SC_SCATTER_CHUNK)
    mesh = plsc.VectorSubcoreMesh(core_axis_name="c", subcore_axis_name="s",
                                  num_cores=SC_CORES, num_subcores=SC_SUBCORES)

    @functools.partial(
        pl.kernel, out_type=jax.ShapeDtypeStruct((n_out, width), rows.dtype), mesh=mesh,
        scratch_types=[pltpu.VMEM((2, chunks, SC_SCATTER_CHUNK), jnp.int32),
                       pltpu.VMEM((SC_SCATTER_CHUNK, width), rows.dtype),
                       pltpu.SemaphoreType.DMA],
        name="sc_scatter")
    def scatter(rows_hbm, idx_hbm, out_hbm, idx_v, rows_v, sem):
        worker = lax.axis_index("s") * SC_CORES + lax.axis_index("c")
        for k in range(2):
            pltpu.sync_copy(idx_hbm.at[k, pl.ds(worker * chunks, chunks)], idx_v.at[k])

        def chunk(j, carry):
            start = worker * per_worker + j * SC_SCATTER_CHUNK
            pltpu.sync_copy(rows_hbm.at[pl.ds(start, SC_SCATTER_CHUNK)], rows_v)
            for k in range(2):
                pltpu.async_copy(rows_v, out_hbm.at[idx_v.at[k, j]], sem).wait()
            return carry

        lax.fori_loop(0, chunks, chunk, 0)

    return scatter(rows, idx)


def _experts_body(be_ref, valid_ref, nused_ref, xs_ref, wg_ref, wu_ref, wd_ref, out_ref,
                  wgb_ref, wub_ref, wdb_ref):
    j = pl.program_id(0)
    used = j < nused_ref[0]

    @pl.when(jnp.logical_not(used))
    def _():
        out_ref[...] = jnp.zeros_like(out_ref)

    @pl.when(used & ((j == 0) | (be_ref[j] != be_ref[jnp.maximum(j - 1, 0)])))
    def _():
        wgb_ref[...] = wg_ref[...].astype(_BF16)
        wub_ref[...] = wu_ref[...].astype(_BF16)
        wdb_ref[...] = wd_ref[...].astype(_BF16)

    @pl.when(used)
    def _():
        row = lax.broadcasted_iota(jnp.int32, xs_ref.shape, 0)
        packed = jnp.where(row < valid_ref[j], xs_ref[...], jnp.uint32(0))
        rows = _unpack_bf16_pair(packed).astype(_BF16)
        hidden = wgb_ref.shape[1]
        out = None
        for c in range(0, hidden, hidden // EXPERT_PARTS):
            cols = slice(c, c + hidden // EXPERT_PARTS)
            gate = jnp.dot(rows, wgb_ref[:, cols], preferred_element_type=_F32)
            up = jnp.dot(rows, wub_ref[:, cols], preferred_element_type=_F32)
            act = ((gate * jax.nn.sigmoid(gate)) * up).astype(_BF16)
            part = jnp.dot(act, wdb_ref[cols, :], preferred_element_type=_F32)
            out = part if out is None else out + part
        half = out.shape[1] // 2
        out_ref[...] = _pack_bf16_pair(out[:, 0:half], out[:, half:])


def _experts(block_expert, block_valid, n_used, xs, wg, wu, wd, *, te):
    n_rows, half = xs.shape
    _, d, de = wg.shape

    def blk(j, be, bv, nu):
        return jnp.minimum(j, nu[0] - 1)

    def weight(j, be, bv, nu):
        return (be[blk(j, be, bv, nu)], 0, 0)

    return pl.pallas_call(
        _experts_body,
        grid_spec=pltpu.PrefetchScalarGridSpec(
            num_scalar_prefetch=3,
            grid=(n_rows // te,),
            in_specs=[pl.BlockSpec((te, half), lambda j, be, bv, nu: (blk(j, be, bv, nu), 0)),
                      pl.BlockSpec((None, d, de), weight),
                      pl.BlockSpec((None, d, de), weight),
                      pl.BlockSpec((None, de, d), weight)],
            out_specs=pl.BlockSpec((te, d // 2), lambda j, be, bv, nu: (j, 0)),
            scratch_shapes=[pltpu.VMEM((d, de), _BF16), pltpu.VMEM((d, de), _BF16),
                            pltpu.VMEM((de, d), _BF16)],
        ),
        out_shape=jax.ShapeDtypeStruct((n_rows, d // 2), jnp.uint32),
        compiler_params=_params("arbitrary"),
        name="experts",
    )(block_expert, block_valid, n_used, xs, wg, wu, wd)


def _sc_gather(table, idx):
    n, = idx.shape
    width = table.shape[1]
    workers = SC_CORES * SC_SUBCORES
    per_worker = n // workers
    chunks = per_worker // SC_CHUNK
    assert n % (workers * SC_CHUNK) == 0
    mesh = plsc.VectorSubcoreMesh(core_axis_name="c", subcore_axis_name="s",
                                  num_cores=SC_CORES, num_subcores=SC_SUBCORES)

    @functools.partial(
        pl.kernel, out_type=jax.ShapeDtypeStruct((n, width), table.dtype), mesh=mesh,
        scratch_types=[pltpu.VMEM((per_worker,), jnp.int32),
                       pltpu.VMEM((SC_CHUNK, width), table.dtype),
                       pltpu.SemaphoreType.DMA],
        name="sc_gather")
    def gather(table_hbm, idx_hbm, out_hbm, idx_v, rows_v, sem):
        worker = lax.axis_index("s") * SC_CORES + lax.axis_index("c")
        base = worker * per_worker
        pltpu.sync_copy(idx_hbm.at[pl.ds(base, per_worker)], idx_v)

        def chunk(j, carry):
            off = pl.multiple_of(j * SC_CHUNK, SC_CHUNK)
            pltpu.async_copy(table_hbm.at[idx_v.at[pl.ds(off, SC_CHUNK)]], rows_v, sem).wait()
            pltpu.sync_copy(rows_v, out_hbm.at[pl.ds(base + off, SC_CHUNK)])
            return carry

        lax.fori_loop(0, chunks, chunk, 0)

    return gather(table, idx)


def _final_body(x1_ref, route_ref, p_ref, r1_ref, r2_ref, gp_ref, wpg_ref, wpp_ref, gf_ref, o_ref):
    emb = jnp.dot(p_ref[...].astype(_BF16), wpp_ref[...], preferred_element_type=_F32)
    route = route_ref[...]
    x2 = x1_ref[...] + (_unpack_bf16_pair(r1_ref[...]) * route[:, 0:1]
                        + _unpack_bf16_pair(r2_ref[...]) * route[:, 1:2])
    gate = jax.nn.sigmoid(jnp.dot(_rms(x2, gp_ref[...]).astype(_BF16), wpg_ref[...],
                                  preferred_element_type=_F32))
    o_ref[...] = _rms(x2 + gate * emb, gf_ref[...])


def _final(x1, route, p, slot_rows, gp, wpg, wpp, gf, *, tm):
    t, d = x1.shape
    ple = p.shape[1]
    steps = t // tm
    row = lambda cols: pl.BlockSpec((tm, cols), lambda i: (i, 0))
    return pl.pallas_call(
        _final_body,
        grid=(steps,),
        in_specs=[row(d), row(LANES), row(ple), row(d // 2),
                  pl.BlockSpec((tm, d // 2), lambda i: (steps + i, 0)),
                  _const_spec((1, d)), _const_spec(wpg.shape), _const_spec(wpp.shape),
                  _const_spec((1, d))],
        out_specs=row(d),
        out_shape=jax.ShapeDtypeStruct((t, d), _F32),
        compiler_params=_params("arbitrary"),
        name="final",
    )(x1, route, p, slot_rows, slot_rows, gp, wpg, wpp, gf)


def _rope_tables(s):
    half = HEAD_DIM // 2
    inv = 1.0 / (ROPE_THETA ** (jnp.arange(0, HEAD_DIM, 2, dtype=_F32) / HEAD_DIM))
    ang = jnp.arange(s, dtype=_F32)[:, None] * inv[None, :]
    cos, sin = jnp.cos(ang), jnp.sin(ang)
    reps = LANES // half
    cos_t = jnp.tile(cos, (1, reps))
    sin_t = jnp.tile(jnp.concatenate([-sin, sin], axis=1), (1, reps // 2))
    return cos_t, sin_t


def _layer(x, p, attn_norm, w_in, b_forget, w_fox_branch, w_moba_branch, w_out, moe_norm,
           w_group, b_group, w_fine, b_fine, w_gate, w_up, w_down, ple_norm, w_ple_gate, w_ple_proj,
           final_gain):
    b, s, d = x.shape
    t = b * s
    width = N_FOX_HEADS * HEAD_DIM
    assert N_MOBA_HEADS * HEAD_DIM == width and MOBA_BLOCK % ATTN_Q == 0 and MOBA_PIPELINE[0] % MOBA_BLOCK == 0
    assert BIAS_PIECES * N_FOX_HEADS <= LANES and s % MOBA_PIPELINE[0] == 0
    assert s % ROW_TILE == 0 and ROW_TILE % MOBA_BLOCK == 0 and s // MOBA_BLOCK <= LANES
    assert t % FINAL_TILE == 0 and N_EXPERTS + N_GROUPS <= LANES

    qkv_cols = 6 * width
    f_cols = N_FOX_HEADS
    chunk = lambda c: w_in[:, c * width:(c + 1) * width]
    w_rows = jnp.concatenate([chunk(1), chunk(4), w_in[:, qkv_cols + f_cols:]], axis=1).astype(_BF16)
    w_cols = jnp.concatenate([chunk(0), chunk(2), chunk(3), chunk(5)], axis=1).T.astype(_BF16)
    wf = jnp.zeros((d, LANES), _BF16).at[:, :f_cols].set(w_in[:, qkv_cols:qkv_cols + f_cols].astype(_BF16))
    bfp = jnp.zeros((1, LANES), _F32).at[0, :f_cols].set(b_forget.astype(_F32))
    cos, sin = _rope_tables(s)

    qft, kf, vft, qmt, km, vmt, gates, cb, kmean, kabs, bmax = _inproj(
        x, attn_norm.reshape(1, d), w_rows, w_cols, wf, bfp, cos, sin, cos.T, sin.T, tm=ROW_TILE)
    y_fox = _attention(qft, kf, cb, vft, kabs, bmax, mode="fox")
    block_of_key = jnp.arange(s, dtype=jnp.int32)[:, None] // MOBA_BLOCK
    block_onehot = (block_of_key == jnp.arange(LANES, dtype=jnp.int32)[None, :]).astype(_BF16)
    y_moba = _attention(qmt, km, block_onehot, vmt, kmean, bmax, mode="moba")

    wr = jnp.zeros((d, LANES), _BF16)
    wr = wr.at[:, :N_EXPERTS].set(w_fine.astype(_BF16))
    wr = wr.at[:, N_EXPERTS:N_EXPERTS + N_GROUPS].set(w_group.astype(_BF16))
    br = jnp.zeros((1, LANES), _F32)
    br = br.at[0, :N_EXPERTS].set(b_fine.astype(_F32))
    br = br.at[0, N_EXPERTS:N_EXPERTS + N_GROUPS].set(b_group.astype(_F32))
    x1, h2p, route, meta, counts = _postattn(
        y_fox.reshape(t, width), y_moba.reshape(t, width), gates.reshape(t, 2 * d), x.reshape(t, d),
        w_fox_branch.astype(_BF16), w_moba_branch.astype(_BF16), w_out.astype(_BF16),
        moe_norm.reshape(1, d), wr, br, tm=ROW_TILE)

    te = EXPERT_TILE
    n_blk = (2 * t) // te + N_EXPERTS
    counts = counts[:N_EXPERTS, 0].astype(jnp.int32)
    blocks_per = (counts + te - 1) // te
    block_end = jnp.cumsum(blocks_per)
    row_start = (block_end - blocks_per) * te
    expert_ids = jnp.arange(N_EXPERTS, dtype=jnp.int32)[:, None, None]
    slot_major = meta[4:6] + jnp.sum(
        jnp.where(meta[2:4][None] == expert_ids, row_start[:, None, None], 0), axis=0)
    n_used = block_end[-1:].astype(jnp.int32)
    block_ids = jnp.arange(n_blk, dtype=jnp.int32)
    block_expert = jnp.minimum(
        jnp.sum((block_ids[:, None] >= block_end[None, :]).astype(jnp.int32), axis=1), N_EXPERTS - 1)

    first_block = block_end - blocks_per
    block_valid = jnp.clip(counts[block_expert] - (block_ids - first_block[block_expert]) * te, 0, te)
    xs = _sc_scatter(h2p, slot_major.reshape(2, t // SC_SCATTER_CHUNK, SC_SCATTER_CHUNK), n_blk * te)
    rows = _experts(block_expert, block_valid, n_used, xs, w_gate, w_up, w_down, te=te)
    slot_rows = _sc_gather(rows, slot_major.reshape(-1))
    out = _final(x1, route, p.reshape(t, -1), slot_rows, ple_norm.reshape(1, d),
                 w_ple_gate.astype(_BF16), w_ple_proj.astype(_BF16), final_gain.reshape(1, d),
                 tm=FINAL_TILE)
    return out.reshape(b, s, d)


def kernel(x, p, attn_norm, w_in, b_forget, w_fox_branch, w_moba_branch, w_out, moe_norm, w_group,
           b_group, w_fine, b_fine, w_gate, w_up, w_down, ple_norm, w_ple_gate, w_ple_proj, final_norm):
    depth = p.shape[0]
    assert depth == 1, "the final norm is fused into the last layer's kernel"
    i = 0
    return _layer(x, p[i], attn_norm[i], w_in[i], b_forget[i], w_fox_branch[i], w_moba_branch[i],
                  w_out[i], moe_norm[i], w_group[i], b_group[i], w_fine[i], b_fine[i], w_gate[i],
                  w_up[i], w_down[i], ple_norm[i], w_ple_gate[i], w_ple_proj[i], final_norm)
```

```python
import functools

import jax
import jax.numpy as jnp
from jax import lax
from jax.experimental import pallas as pl
from jax.experimental.pallas import tpu as pltpu
from jax.experimental.pallas import tpu_sc as plsc

HEAD_DIM = 64
N_FOX_HEADS = 8
N_MOBA_HEADS = 8
MOBA_BLOCK = 256
MOBA_TOPK = 3
ROPE_THETA = 10000.0
N_GROUPS = 4
EXPERTS_PER_GROUP = 8
N_EXPERTS = N_GROUPS * EXPERTS_PER_GROUP
RMS_EPS = 1e-6

LANES = 128
HEADS_PER_STEP = LANES // HEAD_DIM
ATTN_Q = 256
FOX_PIPELINE = (256, 8, 4)
MOBA_PIPELINE = (256, 8, 4)
ONES_ROWS = 16
SOFTMAX_ROWS = 64
BIAS_PIECES = 3
ROW_TILE = 512
SC_CORES = 2
SC_SUBCORES = 16
SC_CHUNK = 64
SC_SCATTER_CHUNK = 128
FINAL_TILE = 512
EXPERT_PARTS = 2
EXPERT_TILE = 512
LOG2_E = 1.4426950408889634
UNDERFLOW_LOG2 = 160.0
NEG_BIG = -1e30
VMEM_LIMIT = 48 * 1024 * 1024

_BF16 = jnp.bfloat16
_F32 = jnp.float32


def _params(*sem):
    return pltpu.CompilerParams(dimension_semantics=sem, vmem_limit_bytes=VMEM_LIMIT)


def _rms(x, g):
    return x * lax.rsqrt(jnp.mean(x * x, axis=-1, keepdims=True) + RMS_EPS) * g


def _const_spec(shape):
    return pl.BlockSpec(shape, lambda *_: (0,) * len(shape))


def _inproj_body(x_ref, g_ref, w_ref, wt_ref, wf_ref, bf_ref, cos_ref, sin_ref, cost_ref, sint_ref,
                 qft_ref, kf_ref, vft_ref, qmt_ref, km_ref, vmt_ref, gates_ref, cb_ref, kmean_ref,
                 kabs_ref, bmax_ref,
                 carry_ref, *, tm, width, gate_cols):
    j = pl.program_id(1)
    h = _rms(x_ref[...], g_ref[...])
    hb = h.astype(_BF16)
    hbt = h.T.astype(_BF16)
    reps = width // LANES
    half = HEAD_DIM // 2

    def proj(c):
        return jnp.dot(hb, w_ref[:, c * width:(c + 1) * width], preferred_element_type=_F32)

    def proj_t(c):
        return jnp.dot(wt_ref[c * width:(c + 1) * width, :], hbt, preferred_element_type=_F32)

    def rope(t):
        cos = jnp.concatenate([cos_ref[...]] * reps, axis=1)
        sin = jnp.concatenate([sin_ref[...]] * reps, axis=1)
        first = (lax.broadcasted_iota(jnp.int32, (1, width), 1) % HEAD_DIM) < half
        partner = jnp.where(first, pltpu.roll(t, width - half, 1), pltpu.roll(t, half, 1))
        return t * cos + partner * sin

    def rope_t(t):
        cos = jnp.concatenate([cost_ref[...]] * reps, axis=0)
        sin = jnp.concatenate([sint_ref[...]] * reps, axis=0)
        first = (lax.broadcasted_iota(jnp.int32, (width, 1), 0) % HEAD_DIM) < half
        partner = jnp.where(first, pltpu.roll(t, width - half, 0), pltpu.roll(t, half, 0))
        return t * cos + partner * sin

    scale = HEAD_DIM ** -0.5 * LOG2_E
    qft_ref[...] = (proj_t(0) * scale).astype(_BF16)
    vft_ref[...] = proj_t(1).astype(_BF16)
    qmt_ref[...] = (rope_t(proj_t(2)) * scale).astype(_BF16)
    vmt_ref[...] = proj_t(3).astype(_BF16)
    kf = proj(0).astype(_BF16)
    kf_ref[...] = kf
    kf_abs = jnp.abs(kf.astype(_F32))
    km = rope(proj(1))
    km_ref[...] = km.astype(_BF16)

    @pl.when(j == 0)
    def _():
        kmean_ref[...] = jnp.zeros_like(kmean_ref)
        kabs_ref[...] = jnp.zeros_like(kabs_ref)
        bmax_ref[...] = jnp.zeros_like(bmax_ref)
        carry_ref[...] = jnp.zeros_like(carry_ref)

    blocks = tm // MOBA_BLOCK
    for b in range(blocks):
        rows = slice(b * MOBA_BLOCK, (b + 1) * MOBA_BLOCK)
        kmean_ref[pl.ds(j * blocks + b, 1), :] = jnp.mean(km[rows], axis=0, keepdims=True)
        kabs_ref[pl.ds(j * blocks + b, 1), :] = jnp.max(kf_abs[rows], axis=0, keepdims=True)

    for c in range(gate_cols // width):
        g = jnp.dot(hb, w_ref[:, (2 + c) * width:(3 + c) * width],
                    preferred_element_type=_F32)
        gates_ref[:, c * width:(c + 1) * width] = jax.nn.sigmoid(g).astype(_BF16)

    z = jnp.dot(hb, wf_ref[...], preferred_element_type=_F32) + bf_ref[...]
    log_f = jnp.minimum(z, 0.0) - jnp.log1p(jnp.exp(-jnp.abs(z)))
    r = lax.broadcasted_iota(jnp.int32, (LANES, LANES), 0)
    c = lax.broadcasted_iota(jnp.int32, (LANES, LANES), 1)
    tri = (c <= r).astype(_BF16)
    offset = carry_ref[0:1, :]
    blocks_cum = []
    for blk in range(tm // LANES):
        rest = log_f[blk * LANES:(blk + 1) * LANES]
        within = jnp.zeros((LANES, LANES), _F32)
        for _ in range(BIAS_PIECES):
            part = rest.astype(_BF16)
            rest = rest - part.astype(_F32)
            within = within + jnp.dot(tri, part, preferred_element_type=_F32)
        blocks_cum.append(within + offset)
        offset = offset + within[LANES - 1:LANES, :]
    cum = jnp.concatenate(blocks_cum, axis=0)
    carry_ref[...] = jnp.broadcast_to(offset, carry_ref.shape)

    head_lane = lax.broadcasted_iota(jnp.int32, (1, LANES), 1) < N_FOX_HEADS
    bias = jnp.where(head_lane, -cum * LOG2_E, 0.0)
    for b in range(blocks):
        bmax_ref[pl.ds(j * blocks + b, 1), :] = jnp.max(
            bias[b * MOBA_BLOCK:(b + 1) * MOBA_BLOCK], axis=0, keepdims=True)
    rest = bias
    placed = jnp.zeros((tm, LANES), _F32)
    for piece in range(BIAS_PIECES):
        part = rest.astype(_BF16).astype(_F32)
        rest = rest - part
        placed = placed + (pltpu.roll(part, N_FOX_HEADS * piece, 1) if piece else part)
    cb_ref[...] = placed.astype(_BF16)


def _inproj(x, g, w, wt, wf, bfp, cos, sin, cos_t, sin_t, *, tm):
    b, s, d = x.shape
    width = N_FOX_HEADS * HEAD_DIM
    gate_cols = w.shape[1] - 2 * width
    act = jax.ShapeDtypeStruct((b, s, width), _BF16)
    act_t = jax.ShapeDtypeStruct((b, width, s), _BF16)
    act_spec = pl.BlockSpec((None, tm, width), lambda bi, j: (bi, j, 0))
    act_t_spec = pl.BlockSpec((None, width, tm), lambda bi, j: (bi, 0, j))
    return pl.pallas_call(
        functools.partial(_inproj_body, tm=tm, width=width, gate_cols=gate_cols),
        grid=(b, s // tm),
        in_specs=[
            pl.BlockSpec((None, tm, d), lambda bi, j: (bi, j, 0)),
            _const_spec((1, d)),
            _const_spec(w.shape),
            _const_spec(wt.shape),
            _const_spec(wf.shape),
            _const_spec(bfp.shape),
            pl.BlockSpec((tm, LANES), lambda bi, j: (j, 0)),
            pl.BlockSpec((tm, LANES), lambda bi, j: (j, 0)),
            pl.BlockSpec((LANES, tm), lambda bi, j: (0, j)),
            pl.BlockSpec((LANES, tm), lambda bi, j: (0, j)),
        ],
        out_specs=[act_t_spec, act_spec, act_t_spec, act_t_spec, act_spec, act_t_spec,
                   pl.BlockSpec((None, tm, gate_cols), lambda bi, j: (bi, j, 0)),
                   pl.BlockSpec((None, tm, LANES), lambda bi, j: (bi, j, 0)),
                   pl.BlockSpec((None, LANES, width), lambda bi, j: (bi, 0, 0)),
                   pl.BlockSpec((None, LANES, width), lambda bi, j: (bi, 0, 0)),
                   pl.BlockSpec((None, LANES, LANES), lambda bi, j: (bi, 0, 0))],
        out_shape=[act_t, act, act_t, act_t, act, act_t,
                   jax.ShapeDtypeStruct((b, s, gate_cols), _BF16),
                   jax.ShapeDtypeStruct((b, s, LANES), _BF16),
                   jax.ShapeDtypeStruct((b, LANES, width), _F32),
                   jax.ShapeDtypeStruct((b, LANES, width), _F32),
                   jax.ShapeDtypeStruct((b, LANES, LANES), _F32)],
        scratch_shapes=[pltpu.VMEM((8, LANES), _F32)],
        compiler_params=_params("arbitrary", "arbitrary"),
        name="inproj",
    )(x, g, w, wt, wf, bfp, cos, sin, cos_t, sin_t)


def _attn_body(uq_ref, uk_ref, qt_ref, k_ref, kx_ref, vt_ref, aux_ref, aux2_ref, o_ref,
               w_ref, m_ref, l_ref, acc_ref, st_ref, p_ref, alpha_ref, pmax_ref, qn_ref, *,
               mode, tq, tk, nb, n_diag, n_units, pipe, lag):
    hp = pl.program_id(1)
    wide = HEADS_PER_STEP * tq
    nq = w_ref.shape[0]
    feat = lax.broadcasted_iota(jnp.int32, (LANES, 1), 0)
    xr = lax.broadcasted_iota(jnp.int32, (LANES, wide), 0)
    xc = lax.broadcasted_iota(jnp.int32, (LANES, wide), 1)

    def setup(i, carry):
        qt = qt_ref[:, pl.ds(pl.multiple_of(i * tq, tq), tq)]
        zero = jnp.zeros_like(qt)
        wq = jnp.concatenate([jnp.where(feat < HEAD_DIM, qt, zero),
                              jnp.where(feat < HEAD_DIM, zero, qt)], axis=1)
        if mode == "fox":
            head = hp * HEADS_PER_STEP + xc // tq
            lower = ((xr % N_FOX_HEADS == head) & (xr < BIAS_PIECES * N_FOX_HEADS)).astype(_BF16)
            qn_ref[i] = jnp.broadcast_to(
                jnp.sum(jnp.abs(wq.astype(_F32)), axis=0, keepdims=True), qn_ref.shape[1:])
        else:
            own = (i * tq) // MOBA_BLOCK
            gate = jnp.dot(aux_ref[...].astype(_BF16), wq, preferred_element_type=_F32)[0:nb]
            blk = lax.broadcasted_iota(jnp.int32, (nb, wide), 0)
            gate = jnp.where(blk < own, gate, -jnp.inf)
            keep = blk == own
            for _ in range(MOBA_TOPK):
                mx = jnp.max(gate, axis=0, keepdims=True)
                cand = jnp.where((gate == mx) & (mx > -jnp.inf), blk, nb)
                pick = blk == jnp.min(cand, axis=0, keepdims=True)
                keep = keep | pick
                gate = jnp.where(pick, -jnp.inf, gate)
            lower = jnp.where(keep, 0.0, NEG_BIG)
            if nb < LANES:
                lower = jnp.concatenate([lower, jnp.zeros((LANES - nb, wide), _F32)], axis=0)
            lower = lower.astype(_BF16)
        w_ref[i] = jnp.concatenate([wq, lower], axis=0)
        m_ref[i] = jnp.full(m_ref.shape[1:], NEG_BIG, _F32)
        l_ref[i] = jnp.zeros(l_ref.shape[1:], _F32)
        acc_ref[i] = jnp.zeros(acc_ref.shape[1:], _F32)
        return carry

    lax.fori_loop(0, nq, setup, 0)

    def unit(t):
        return uq_ref[t], uk_ref[t]

    def scores(t, slot, causal):
        q, kb = unit(t)
        start = pl.multiple_of(kb * tk, tk)
        ka = jnp.concatenate([k_ref[pl.ds(start, tk), :], kx_ref[pl.ds(start, tk), :]], axis=1)
        st = jnp.dot(ka, w_ref[q], preferred_element_type=_F32)
        if causal:
            key = start + lax.broadcasted_iota(jnp.int32, (tk, wide), 0)
            qry = q * tq + lax.broadcasted_iota(jnp.int32, (tk, wide), 1) % tq
            st = jnp.where(key <= qry, st, NEG_BIG)
        st_ref[slot] = st
        pmax_ref[slot] = jnp.max(st.reshape(tk // 8, 8, wide), axis=0)

    def softmax(t, slot):
        q, _ = unit(t)
        rows = SOFTMAX_ROWS
        m_prev = m_ref[q][0:1, :]
        m_new = jnp.maximum(m_prev, jnp.max(pmax_ref[slot], axis=0, keepdims=True))
        for r in range(0, tk, rows):
            p_ref[slot, r:r + rows, :] = jnp.exp2(st_ref[slot, r:r + rows, :] - m_new).astype(_BF16)
        m_ref[q] = jnp.broadcast_to(m_new, m_ref.shape[1:])
        alpha_ref[slot] = jnp.broadcast_to(jnp.exp2(m_prev - m_new), alpha_ref.shape[1:])

    def values(t, slot):
        q, kb = unit(t)
        start = pl.multiple_of(kb * tk, tk)
        ones = jnp.ones((ONES_ROWS, tk), _BF16)
        for h in range(HEADS_PER_STEP):
            cols = slice(h * tq, (h + 1) * tq)
            feats = slice(h * HEAD_DIM, (h + 1) * HEAD_DIM)
            lhs = jnp.concatenate([vt_ref[feats, pl.ds(start, tk)], ones], axis=0)
            pv = jnp.dot(lhs, p_ref[slot, :, cols], preferred_element_type=_F32)
            alpha = alpha_ref[slot, 0:1, cols]
            acc_ref[q, feats, :] = acc_ref[q, feats, :] * alpha + pv[0:HEAD_DIM]
            l_ref[q, :, cols] = jnp.broadcast_to(
                alpha * l_ref[q, 0:1, cols] + pv[HEAD_DIM:HEAD_DIM + 1], (8, tq))

    def step(base, c, causal, do_scores=True, do_softmax=True, do_values=True):
        u = base + c
        if do_scores:
            scores(u, c % pipe, causal)
        if do_values:
            values(u - 2 * lag, (c - 2 * lag) % pipe)
        if do_softmax:
            softmax(u - lag, (c - lag) % pipe)

    def group(causal):
        def body(j, carry):
            for c in range(pipe):
                step(j * pipe, c, causal)
            return carry
        return body

    def pipeline(g_first, g_end, causal_groups):
        for c in range(pipe):
            step(g_first * pipe, c, g_first < causal_groups,
                 do_softmax=c >= lag, do_values=c >= 2 * lag)
        if g_first + 1 < causal_groups:
            lax.fori_loop(g_first + 1, causal_groups, group(True), 0)
        lax.fori_loop(max(g_first + 1, causal_groups), g_end, group(False), 0)
        for c in range(2 * lag):
            step(g_end * pipe, c, False, do_scores=False, do_softmax=c < lag)

    diag_groups = n_diag // pipe
    if mode == "moba":
        pipeline(0, n_units // pipe, diag_groups)
    else:
        pipeline(0, diag_groups, diag_groups)
        nq_blocks = n_diag
        col = lax.broadcasted_iota(jnp.int32, (LANES, LANES), 1)
        first_head = lax.broadcasted_iota(jnp.int32, (1, wide), 1) < tq
        head0 = hp * HEADS_PER_STEP

        def per_head(table, pick0, pick1, fill):
            v0 = jnp.max(jnp.where(pick0, table, fill), axis=1, keepdims=True)[0:nb]
            v1 = jnp.max(jnp.where(pick1, table, fill), axis=1, keepdims=True)[0:nb]
            return jnp.where(first_head, v0, v1)

        bias_max = per_head(aux2_ref[...], col == head0, col == head0 + 1, NEG_BIG)
        k_absmax = per_head(aux_ref[...], col < HEAD_DIM, col >= HEAD_DIM, 0.0)
        blk = lax.broadcasted_iota(jnp.int32, (nb, 1), 0)

        needed = jnp.zeros((nb, 1), jnp.int32)
        for i in range(1, nq_blocks):
            bound = bias_max + qn_ref[i, 0:1, :] * k_absmax - m_ref[i, 0:1, :]
            live = jnp.max(bound, axis=1, keepdims=True) >= -UNDERFLOW_LOG2
            needed = jnp.maximum(needed, jnp.where(live & (blk < i), i - blk, 0))
        far = jnp.max(needed)
        n_live = far * nq_blocks - (far * (far + 1)) // 2
        live_groups = (n_live + pipe - 1) // pipe

        @pl.when(live_groups > 0)
        def _():
            pipeline(diag_groups, diag_groups + live_groups, diag_groups)

    def finish(i, carry):
        out_t = jnp.concatenate(
            [acc_ref[i, h * HEAD_DIM:(h + 1) * HEAD_DIM, :] / l_ref[i, 0:1, h * tq:(h + 1) * tq]
             for h in range(HEADS_PER_STEP)], axis=0)
        o_ref[pl.ds(pl.multiple_of(i * tq, tq), tq), :] = out_t.T.astype(o_ref.dtype)
        return carry

    lax.fori_loop(0, nq, finish, 0, unroll=2)


def _attention(qt, k, kx, vt, aux, aux2, *, mode):
    b, width, s = qt.shape
    tq = ATTN_Q
    tk, pipe, lag = FOX_PIPELINE if mode == "fox" else MOBA_PIPELINE
    nq = s // tq
    nb = s // MOBA_BLOCK
    wide = HEADS_PER_STEP * tq
    diag = [(i * tq) // tk for i in range(nq)]
    units = list(enumerate(diag))
    units += [(i, diag[i] - dist) for dist in range(1, nq) for i in range(nq) if diag[i] >= dist]
    assert mode != "fox" or tq == tk
    assert nq % pipe == 0 and len(units) % pipe == 0 and pipe >= 2 * lag
    unit_q = jnp.asarray([u[0] for u in units], jnp.int32)
    unit_k = jnp.asarray([u[1] for u in units], jnp.int32)
    aux_spec = pl.BlockSpec((None, LANES, LANES), lambda bi, hp, uq, uk: (bi, 0, hp))
    aux2_spec = pl.BlockSpec((None, LANES, LANES), lambda bi, hp, uq, uk: (bi, 0, 0))
    if mode == "fox":
        kx_spec = pl.BlockSpec((None, s, LANES), lambda bi, hp, uq, uk: (bi, 0, 0))
    else:
        kx_spec = pl.BlockSpec(kx.shape, lambda bi, hp, uq, uk: (0, 0))
    return pl.pallas_call(
        functools.partial(_attn_body, mode=mode, tq=tq, tk=tk, nb=nb, n_diag=nq, n_units=len(units),
                          pipe=pipe, lag=lag),
        grid_spec=pltpu.PrefetchScalarGridSpec(
            num_scalar_prefetch=2,
            grid=(b, width // LANES),
            in_specs=[
                pl.BlockSpec((None, LANES, s), lambda bi, hp, uq, uk: (bi, hp, 0)),
                pl.BlockSpec((None, s, LANES), lambda bi, hp, uq, uk: (bi, 0, hp)),
                kx_spec,
                pl.BlockSpec((None, LANES, s), lambda bi, hp, uq, uk: (bi, hp, 0)),
                aux_spec,
                aux2_spec,
            ],
            out_specs=pl.BlockSpec((None, s, LANES), lambda bi, hp, uq, uk: (bi, 0, hp)),
            scratch_shapes=[
                pltpu.VMEM((nq, 2 * LANES, wide), _BF16),
                pltpu.VMEM((nq, 8, wide), _F32),
                pltpu.VMEM((nq, 8, wide), _F32),
                pltpu.VMEM((nq, LANES, tq), _F32),
                pltpu.VMEM((pipe, tk, wide), _F32),
                pltpu.VMEM((pipe, tk, wide), _BF16),
                pltpu.VMEM((pipe, 8, wide), _F32),
                pltpu.VMEM((pipe, 8, wide), _F32),
                pltpu.VMEM((nq, 8, wide), _F32),
            ],
        ),
        out_shape=jax.ShapeDtypeStruct((b, s, width), _BF16),
        compiler_params=_params("arbitrary", "arbitrary"),
        name=mode,
    )(unit_q, unit_k, qt, k, kx, vt, aux, aux2)


def _pack_bf16_pair(a, b):
    lo = pltpu.bitcast(a.astype(_BF16).astype(_F32), jnp.uint32) >> 16
    hi = pltpu.bitcast(b.astype(_BF16).astype(_F32), jnp.uint32) & jnp.uint32(0xFFFF0000)
    return lo | hi


def _unpack_bf16_pair(u):
    lo = pltpu.bitcast(u << 16, _F32)
    hi = pltpu.bitcast(u & jnp.uint32(0xFFFF0000), _F32)
    return jnp.concatenate([lo, hi], axis=1)


def _postattn_body(yf_ref, ym_ref, gates_ref, x_ref, wfb_ref, wmb_ref, wout_ref, g_ref, wr_ref, br_ref,
                   x1_ref, h2_ref, route_ref, meta_ref, counts_ref, carry_ref, *, tm, d):
    step = pl.program_id(0)
    ya = jnp.dot(yf_ref[...], wfb_ref[...], preferred_element_type=_F32)
    yb = jnp.dot(ym_ref[...], wmb_ref[...], preferred_element_type=_F32)
    mixed = gates_ref[:, 0:d].astype(_F32) * ya + gates_ref[:, d:2 * d].astype(_F32) * yb
    x1 = x_ref[...] + jnp.dot(mixed.astype(_BF16), wout_ref[...], preferred_element_type=_F32)
    x1_ref[...] = x1
    h2 = _rms(x1, g_ref[...])
    h2_ref[...] = _pack_bf16_pair(h2[:, 0:d // 2], h2[:, d // 2:d])

    logits = jnp.dot(h2.astype(_BF16), wr_ref[...], preferred_element_type=_F32) + br_ref[...]
    logits_t = logits.T
    gl = logits_t[N_EXPERTS:N_EXPERTS + N_GROUPS]
    ge = jnp.exp(gl - jnp.max(gl, axis=0, keepdims=True))
    gp = ge / jnp.sum(ge, axis=0, keepdims=True)
    g_top = jnp.max(gp, axis=0, keepdims=True)
    g_row = lax.broadcasted_iota(jnp.int32, (N_GROUPS, tm), 0)
    g_idx = jnp.min(jnp.where(gp == g_top, g_row, N_GROUPS), axis=0, keepdims=True)
    fl = logits_t[0:EXPERTS_PER_GROUP]
    for g in range(1, N_GROUPS):
        fl = jnp.where(g_idx == g, logits_t[g * EXPERTS_PER_GROUP:(g + 1) * EXPERTS_PER_GROUP], fl)
    f_row = lax.broadcasted_iota(jnp.int32, (EXPERTS_PER_GROUP, tm), 0)
    f1 = jnp.max(fl, axis=0, keepdims=True)
    i1 = jnp.min(jnp.where(fl == f1, f_row, EXPERTS_PER_GROUP), axis=0, keepdims=True)
    fl2 = jnp.where(f_row == i1, -jnp.inf, fl)
    f2 = jnp.max(fl2, axis=0, keepdims=True)
    i2 = jnp.min(jnp.where(fl2 == f2, f_row, EXPERTS_PER_GROUP), axis=0, keepdims=True)
    e1 = g_idx * EXPERTS_PER_GROUP + i1
    e2 = g_idx * EXPERTS_PER_GROUP + i2
    t2 = jnp.exp(f2 - f1)
    w1 = g_top * (1.0 / (1.0 + t2))
    w2 = g_top * (t2 / (1.0 + t2))

    @pl.when(step == 0)
    def _():
        carry_ref[...] = jnp.zeros_like(carry_ref)

    x_row = lax.broadcasted_iota(jnp.int32, (N_EXPERTS, tm), 0)
    onehot = ((x_row == e1) | (x_row == e2)).astype(_F32)
    rr = lax.broadcasted_iota(jnp.int32, (tm, tm), 0)
    cc = lax.broadcasted_iota(jnp.int32, (tm, tm), 1)
    before = jnp.dot(onehot.astype(_BF16), (rr < cc).astype(_BF16), preferred_element_type=_F32)
    before = before + carry_ref[0:N_EXPERTS, 0:1]
    r1 = jnp.sum(jnp.where(x_row == e1, before, 0.0), axis=0, keepdims=True)
    r2 = jnp.sum(jnp.where(x_row == e2, before, 0.0), axis=0, keepdims=True)
    total = carry_ref[0:N_EXPERTS, 0:1] + jnp.sum(onehot, axis=1, keepdims=True)
    carry_ref[0:N_EXPERTS, :] = jnp.broadcast_to(total, (N_EXPERTS, LANES))
    counts_ref[...] = carry_ref[...]

    m_row = lax.broadcasted_iota(jnp.int32, (8, tm), 0)
    meta = jnp.where(m_row == 2, e1, 0)
    meta = jnp.where(m_row == 3, e2, meta)
    meta = jnp.where(m_row == 4, r1.astype(jnp.int32), meta)
    meta = jnp.where(m_row == 5, r2.astype(jnp.int32), meta)
    meta_ref[...] = meta
    w_row = lax.broadcasted_iota(jnp.int32, (LANES, tm), 0)
    route_ref[...] = jnp.where(w_row == 0, w1, jnp.where(w_row == 1, w2, 0.0)).T


def _postattn(yf, ym, gates, x, wfb, wmb, wout, g, wr, br, *, tm):
    t, d = x.shape
    width = yf.shape[1]
    row = lambda cols: pl.BlockSpec((tm, cols), lambda i: (i, 0))
    return pl.pallas_call(
        functools.partial(_postattn_body, tm=tm, d=d),
        grid=(t // tm,),
        in_specs=[row(width), row(width), row(2 * d), row(d),
                  _const_spec(wfb.shape), _const_spec(wmb.shape), _const_spec(wout.shape),
                  _const_spec((1, d)), _const_spec(wr.shape), _const_spec(br.shape)],
        out_specs=[row(d), row(d // 2), row(LANES), pl.BlockSpec((8, tm), lambda i: (0, i)),
                   _const_spec((LANES, LANES))],
        out_shape=[jax.ShapeDtypeStruct((t, d), _F32),
                   jax.ShapeDtypeStruct((t, d // 2), jnp.uint32),
                   jax.ShapeDtypeStruct((t, LANES), _F32),
                   jax.ShapeDtypeStruct((8, t), jnp.int32),
                   jax.ShapeDtypeStruct((LANES, LANES), _F32)],
        scratch_shapes=[pltpu.VMEM((LANES, LANES), _F32)],
        compiler_params=_params("arbitrary"),
        name="postattn",
    )(yf, ym, gates, x, wfb, wmb, wout, g, wr, br)


def _sc_scatter(rows, idx, n_out):
    n, width = rows.shape
    workers = SC_CORES * SC_SUBCORES
    per_worker = n // workers
    chunks = per_worker // SC_SCATTER_CHUNK
    assert n % (workers * SC_SCATTER_CHUNK) == 0 and idx.shape == (2, n // SC_SCATTER_CHUNK, SC_SCATTER_CHUNK)
    mesh = plsc.VectorSubcoreMesh(core_axis_name="c", subcore_axis_name="s",
                                  num_cores=SC_CORES, num_subcores=SC_SUBCORES)

    @functools.partial(
        pl.kernel, out_type=jax.ShapeDtypeStruct((n_out, width), rows.dtype), mesh=mesh,
        scratch_types=[pltpu.VMEM((2, chunks, SC_SCATTER_CHUNK), jnp.int32),
                       pltpu.VMEM((SC_SCATTER_CHUNK, width), rows.dtype),
                       pltpu.SemaphoreType.DMA],
        name="sc_scatter")
    def scatter(rows_hbm, idx_hbm, out_hbm, idx_v, rows_v, sem):
        worker = lax.axis_index("s") * SC_CORES + lax.axis_index("c")
        for k in range(2):
            pltpu.sync_copy(idx_hbm.at[k, pl.ds(worker * chunks, chunks)], idx_v.at[k])

        def chunk(j, carry):
            start = worker * per_worker + j * SC_SCATTER_CHUNK
            pltpu.sync_copy(rows_hbm.at[pl.ds(start, SC_SCATTER_CHUNK)], rows_v)
            for k in range(2):
                pltpu.async_copy(rows_v, out_hbm.at[idx_v.at[k, j]], sem).wait()
            return carry

        lax.fori_loop(0, chunks, chunk, 0)

    return scatter(rows, idx)


def _experts_body(be_ref, valid_ref, nused_ref, xs_ref, wg_ref, wu_ref, wd_ref, out_ref,
                  wgb_ref, wub_ref, wdb_ref):
    j = pl.program_id(0)
    used = j < nused_ref[0]

    @pl.when(jnp.logical_not(used))
    def _():
        out_ref[...] = jnp.zeros_like(out_ref)

    @pl.when(used & ((j == 0) | (be_ref[j] != be_ref[jnp.maximum(j - 1, 0)])))
    def _():
        wgb_ref[...] = wg_ref[...].astype(_BF16)
        wub_ref[...] = wu_ref[...].astype(_BF16)
        wdb_ref[...] = wd_ref[...].astype(_BF16)

    @pl.when(used)
    def _():
        row = lax.broadcasted_iota(jnp.int32, xs_ref.shape, 0)
        packed = jnp.where(row < valid_ref[j], xs_ref[...], jnp.uint32(0))
        rows = _unpack_bf16_pair(packed).astype(_BF16)
        hidden = wgb_ref.shape[1]
        out = None
        for c in range(0, hidden, hidden // EXPERT_PARTS):
            cols = slice(c, c + hidden // EXPERT_PARTS)
            gate = jnp.dot(rows, wgb_ref[:, cols], preferred_element_type=_F32)
            up = jnp.dot(rows, wub_ref[:, cols], preferred_element_type=_F32)
            act = ((gate * jax.nn.sigmoid(gate)) * up).astype(_BF16)
            part = jnp.dot(act, wdb_ref[cols, :], preferred_element_type=_F32)
            out = part if out is None else out + part
        half = out.shape[1] // 2
        out_ref[...] = _pack_bf16_pair(out[:, 0:half], out[:, half:])


def _experts(block_expert, block_valid, n_used, xs, wg, wu, wd, *, te):
    n_rows, half = xs.shape
    _, d, de = wg.shape

    def blk(j, be, bv, nu):
        return jnp.minimum(j, nu[0] - 1)

    def weight(j, be, bv, nu):
        return (be[blk(j, be, bv, nu)], 0, 0)

    return pl.pallas_call(
        _experts_body,
        grid_spec=pltpu.PrefetchScalarGridSpec(
            num_scalar_prefetch=3,
            grid=(n_rows // te,),
            in_specs=[pl.BlockSpec((te, half), lambda j, be, bv, nu: (blk(j, be, bv, nu), 0)),
                      pl.BlockSpec((None, d, de), weight),
                      pl.BlockSpec((None, d, de), weight),
                      pl.BlockSpec((None, de, d), weight)],
            out_specs=pl.BlockSpec((te, d // 2), lambda j, be, bv, nu: (j, 0)),
            scratch_shapes=[pltpu.VMEM((d, de), _BF16), pltpu.VMEM((d, de), _BF16),
                            pltpu.VMEM((de, d), _BF16)],
        ),
        out_shape=jax.ShapeDtypeStruct((n_rows, d // 2), jnp.uint32),
        compiler_params=_params("arbitrary"),
        name="experts",
    )(block_expert, block_valid, n_used, xs, wg, wu, wd)


def _sc_gather(table, idx):
    n, = idx.shape
    width = table.shape[1]
    workers = SC_CORES * SC_SUBCORES
    per_worker = n // workers
    chunks = per_worker // SC_CHUNK
    assert n % (workers * SC_CHUNK * 2) == 0
    mesh = plsc.VectorSubcoreMesh(core_axis_name="c", subcore_axis_name="s",
                                  num_cores=SC_CORES, num_subcores=SC_SUBCORES)

    @functools.partial(
        pl.kernel, out_type=jax.ShapeDtypeStruct((n, width), table.dtype), mesh=mesh,
        scratch_types=[pltpu.VMEM((per_worker,), jnp.int32),
                       pltpu.VMEM((2, SC_CHUNK, width), table.dtype),
                       pltpu.SemaphoreType.DMA((2,))],
        name="sc_gather")
    def gather(table_hbm, idx_hbm, out_hbm, idx_v, rows_v, sems):
        worker = lax.axis_index("s") * SC_CORES + lax.axis_index("c")
        base = worker * per_worker
        pltpu.sync_copy(idx_hbm.at[pl.ds(base, per_worker)], idx_v)

        def fetch(j, slot):
            off = pl.multiple_of(j * SC_CHUNK, SC_CHUNK)
            return pltpu.async_copy(table_hbm.at[idx_v.at[pl.ds(off, SC_CHUNK)]], rows_v.at[slot],
                                    sems.at[slot])

        def drain(j, slot):
            off = pl.multiple_of(j * SC_CHUNK, SC_CHUNK)
            pltpu.make_async_copy(out_hbm.at[pl.ds(base + off, SC_CHUNK)], rows_v.at[slot],
                                  sems.at[slot]).wait()
            pltpu.sync_copy(rows_v.at[slot], out_hbm.at[pl.ds(base + off, SC_CHUNK)])

        fetch(0, 0)

        def pair(p, carry):
            fetch(2 * p + 1, 1)
            drain(2 * p, 0)

            @pl.when(p + 1 < chunks // 2)
            def _():
                fetch(2 * p + 2, 0)

            drain(2 * p + 1, 1)
            return carry

        lax.fori_loop(0, chunks // 2, pair, 0)

    return gather(table, idx)


def _final_body(x1_ref, route_ref, p_ref, r1_ref, r2_ref, gp_ref, wpg_ref, wpp_ref, gf_ref, o_ref):
    emb = jnp.dot(p_ref[...].astype(_BF16), wpp_ref[...], preferred_element_type=_F32)
    route = route_ref[...]
    x2 = x1_ref[...] + (_unpack_bf16_pair(r1_ref[...]) * route[:, 0:1]
                        + _unpack_bf16_pair(r2_ref[...]) * route[:, 1:2])
    gate = jax.nn.sigmoid(jnp.dot(_rms(x2, gp_ref[...]).astype(_BF16), wpg_ref[...],
                                  preferred_element_type=_F32))
    o_ref[...] = _rms(x2 + gate * emb, gf_ref[...])


def _final(x1, route, p, slot_rows, gp, wpg, wpp, gf, *, tm):
    t, d = x1.shape
    ple = p.shape[1]
    steps = t // tm
    row = lambda cols: pl.BlockSpec((tm, cols), lambda i: (i, 0))
    return pl.pallas_call(
        _final_body,
        grid=(steps,),
        in_specs=[row(d), row(LANES), row(ple), row(d // 2),
                  pl.BlockSpec((tm, d // 2), lambda i: (steps + i, 0)),
                  _const_spec((1, d)), _const_spec(wpg.shape), _const_spec(wpp.shape),
                  _const_spec((1, d))],
        out_specs=row(d),
        out_shape=jax.ShapeDtypeStruct((t, d), _F32),
        compiler_params=_params("arbitrary"),
        name="final",
    )(x1, route, p, slot_rows, slot_rows, gp, wpg, wpp, gf)


def _rope_tables(s):
    half = HEAD_DIM // 2
    inv = 1.0 / (ROPE_THETA ** (jnp.arange(0, HEAD_DIM, 2, dtype=_F32) / HEAD_DIM))
    ang = jnp.arange(s, dtype=_F32)[:, None] * inv[None, :]
    cos, sin = jnp.cos(ang), jnp.sin(ang)
    reps = LANES // half
    cos_t = jnp.tile(cos, (1, reps))
    sin_t = jnp.tile(jnp.concatenate([-sin, sin], axis=1), (1, reps // 2))
    return cos_t, sin_t


def _layer(x, p, attn_norm, w_in, b_forget, w_fox_branch, w_moba_branch, w_out, moe_norm,
           w_group, b_group, w_fine, b_fine, w_gate, w_up, w_down, ple_norm, w_ple_gate, w_ple_proj,
           final_gain):
    b, s, d = x.shape
    t = b * s
    width = N_FOX_HEADS * HEAD_DIM
    assert N_MOBA_HEADS * HEAD_DIM == width and MOBA_BLOCK % ATTN_Q == 0 and MOBA_PIPELINE[0] % MOBA_BLOCK == 0
    assert BIAS_PIECES * N_FOX_HEADS <= LANES and s % MOBA_PIPELINE[0] == 0
    assert s % ROW_TILE == 0 and ROW_TILE % MOBA_BLOCK == 0 and s // MOBA_BLOCK <= LANES
    assert t % FINAL_TILE == 0 and N_EXPERTS + N_GROUPS <= LANES

    qkv_cols = 6 * width
    f_cols = N_FOX_HEADS
    chunk = lambda c: w_in[:, c * width:(c + 1) * width]
    w_rows = jnp.concatenate([chunk(1), chunk(4), w_in[:, qkv_cols + f_cols:]], axis=1).astype(_BF16)
    w_cols = jnp.concatenate([chunk(0), chunk(2), chunk(3), chunk(5)], axis=1).T.astype(_BF16)
    wf = jnp.zeros((d, LANES), _BF16).at[:, :f_cols].set(w_in[:, qkv_cols:qkv_cols + f_cols].astype(_BF16))
    bfp = jnp.zeros((1, LANES), _F32).at[0, :f_cols].set(b_forget.astype(_F32))
    cos, sin = _rope_tables(s)

    qft, kf, vft, qmt, km, vmt, gates, cb, kmean, kabs, bmax = _inproj(
        x, attn_norm.reshape(1, d), w_rows, w_cols, wf, bfp, cos, sin, cos.T, sin.T, tm=ROW_TILE)
    y_fox = _attention(qft, kf, cb, vft, kabs, bmax, mode="fox")
    block_of_key = jnp.arange(s, dtype=jnp.int32)[:, None] // MOBA_BLOCK
    block_onehot = (block_of_key == jnp.arange(LANES, dtype=jnp.int32)[None, :]).astype(_BF16)
    y_moba = _attention(qmt, km, block_onehot, vmt, kmean, bmax, mode="moba")

    wr = jnp.zeros((d, LANES), _BF16)
    wr = wr.at[:, :N_EXPERTS].set(w_fine.astype(_BF16))
    wr = wr.at[:, N_EXPERTS:N_EXPERTS + N_GROUPS].set(w_group.astype(_BF16))
    br = jnp.zeros((1, LANES), _F32)
    br = br.at[0, :N_EXPERTS].set(b_fine.astype(_F32))
    br = br.at[0, N_EXPERTS:N_EXPERTS + N_GROUPS].set(b_group.astype(_F32))
    x1, h2p, route, meta, counts = _postattn(
        y_fox.reshape(t, width), y_moba.reshape(t, width), gates.reshape(t, 2 * d), x.reshape(t, d),
        w_fox_branch.astype(_BF16), w_moba_branch.astype(_BF16), w_out.astype(_BF16),
        moe_norm.reshape(1, d), wr, br, tm=ROW_TILE)

    te = EXPERT_TILE
    n_blk = (2 * t) // te + N_EXPERTS
    counts = counts[:N_EXPERTS, 0].astype(jnp.int32)
    blocks_per = (counts + te - 1) // te
    block_end = jnp.cumsum(blocks_per)
    row_start = (block_end - blocks_per) * te
    expert_ids = jnp.arange(N_EXPERTS, dtype=jnp.int32)[:, None, None]
    slot_major = meta[4:6] + jnp.sum(
        jnp.where(meta[2:4][None] == expert_ids, row_start[:, None, None], 0), axis=0)
    n_used = block_end[-1:].astype(jnp.int32)
    block_ids = jnp.arange(n_blk, dtype=jnp.int32)
    block_expert = jnp.minimum(
        jnp.sum((block_ids[:, None] >= block_end[None, :]).astype(jnp.int32), axis=1), N_EXPERTS - 1)

    first_block = block_end - blocks_per
    block_valid = jnp.clip(counts[block_expert] - (block_ids - first_block[block_expert]) * te, 0, te)
    xs = _sc_scatter(h2p, slot_major.reshape(2, t // SC_SCATTER_CHUNK, SC_SCATTER_CHUNK), n_blk * te)
    rows = _experts(block_expert, block_valid, n_used, xs, w_gate, w_up, w_down, te=te)
    slot_rows = _sc_gather(rows, slot_major.reshape(-1))
    out = _final(x1, route, p.reshape(t, -1), slot_rows, ple_norm.reshape(1, d),
                 w_ple_gate.astype(_BF16), w_ple_proj.astype(_BF16), final_gain.reshape(1, d),
                 tm=FINAL_TILE)
    return out.reshape(b, s, d)


def kernel(x, p, attn_norm, w_in, b_forget, w_fox_branch, w_moba_branch, w_out, moe_norm, w_group,
           b_group, w_fine, b_fine, w_gate, w_up, w_down, ple_norm, w_ple_gate, w_ple_proj, final_norm):
    depth = p.shape[0]
    assert depth == 1, "the final norm is fused into the last layer's kernel"
    i = 0
    return _layer(x, p[i], attn_norm[i], w_in[i], b_forget[i], w_fox_branch[i], w_moba_branch[i],
                  w_out[i], moe_norm[i], w_group[i], b_group[i], w_fine[i], b_fine[i], w_gate[i],
                  w_up[i], w_down[i], ple_norm[i], w_ple_gate[i], w_ple_proj[i], final_norm)
```

```python
import functools

import jax
import jax.numpy as jnp
from jax import lax
from jax.experimental import pallas as pl
from jax.experimental.pallas import tpu as pltpu
from jax.experimental.pallas import tpu_sc as plsc

HEAD_DIM = 64
N_FOX_HEADS = 8
N_MOBA_HEADS = 8
MOBA_BLOCK = 256
MOBA_TOPK = 3
ROPE_THETA = 10000.0
N_GROUPS = 4
EXPERTS_PER_GROUP = 8
N_EXPERTS = N_GROUPS * EXPERTS_PER_GROUP
RMS_EPS = 1e-6

LANES = 128
HEADS_PER_STEP = LANES // HEAD_DIM
ATTN_Q = 256
FOX_PIPELINE = (256, 8, 4)
MOBA_PIPELINE = (256, 8, 4)
ONES_ROWS = 16
SOFTMAX_ROWS = 64
BIAS_PIECES = 3
ROW_TILE = 512
SC_CORES = 2
SC_SUBCORES = 16
SC_CHUNK = 64
SC_SCATTER_CHUNK = 128
FINAL_TILE = 512
EXPERT_PARTS = 2
EXPERT_TILE = 512
LOG2_E = 1.4426950408889634
UNDERFLOW_LOG2 = 160.0
NEG_BIG = -1e30
VMEM_LIMIT = 48 * 1024 * 1024

_BF16 = jnp.bfloat16
_F32 = jnp.float32


def _params(*sem):
    return pltpu.CompilerParams(dimension_semantics=sem, vmem_limit_bytes=VMEM_LIMIT)


def _rms(x, g):
    return x * lax.rsqrt(jnp.mean(x * x, axis=-1, keepdims=True) + RMS_EPS) * g


def _const_spec(shape):
    return pl.BlockSpec(shape, lambda *_: (0,) * len(shape))


def _inproj_body(x_ref, g_ref, w_ref, wt_ref, wf_ref, bf_ref, cos_ref, sin_ref, cost_ref, sint_ref,
                 qft_ref, kf_ref, vft_ref, qmt_ref, km_ref, vmt_ref, gates_ref, cb_ref, kmean_ref,
                 kabs_ref, bmax_ref,
                 carry_ref, *, tm, width, gate_cols):
    j = pl.program_id(1)
    h = _rms(x_ref[...], g_ref[...])
    hb = h.astype(_BF16)
    hbt = h.T.astype(_BF16)
    reps = width // LANES
    half = HEAD_DIM // 2

    def proj(c):
        return jnp.dot(hb, w_ref[:, c * width:(c + 1) * width], preferred_element_type=_F32)

    def proj_t(c):
        return jnp.dot(wt_ref[c * width:(c + 1) * width, :], hbt, preferred_element_type=_F32)

    def rope(t):
        cos = jnp.concatenate([cos_ref[...]] * reps, axis=1)
        sin = jnp.concatenate([sin_ref[...]] * reps, axis=1)
        first = (lax.broadcasted_iota(jnp.int32, (1, width), 1) % HEAD_DIM) < half
        partner = jnp.where(first, pltpu.roll(t, width - half, 1), pltpu.roll(t, half, 1))
        return t * cos + partner * sin

    def rope_t(t):
        cos = jnp.concatenate([cost_ref[...]] * reps, axis=0)
        sin = jnp.concatenate([sint_ref[...]] * reps, axis=0)
        first = (lax.broadcasted_iota(jnp.int32, (width, 1), 0) % HEAD_DIM) < half
        partner = jnp.where(first, pltpu.roll(t, width - half, 0), pltpu.roll(t, half, 0))
        return t * cos + partner * sin

    scale = HEAD_DIM ** -0.5 * LOG2_E
    qft_ref[...] = (proj_t(0) * scale).astype(_BF16)
    vft_ref[...] = proj_t(1).astype(_BF16)
    qmt_ref[...] = (rope_t(proj_t(2)) * scale).astype(_BF16)
    vmt_ref[...] = proj_t(3).astype(_BF16)
    kf = proj(0).astype(_BF16)
    kf_ref[...] = kf
    kf_abs = jnp.abs(kf.astype(_F32))
    km = rope(proj(1))
    km_ref[...] = km.astype(_BF16)

    @pl.when(j == 0)
    def _():
        kmean_ref[...] = jnp.zeros_like(kmean_ref)
        kabs_ref[...] = jnp.zeros_like(kabs_ref)
        bmax_ref[...] = jnp.zeros_like(bmax_ref)
        carry_ref[...] = jnp.zeros_like(carry_ref)

    blocks = tm // MOBA_BLOCK
    for b in range(blocks):
        rows = slice(b * MOBA_BLOCK, (b + 1) * MOBA_BLOCK)
        kmean_ref[pl.ds(j * blocks + b, 1), :] = jnp.mean(km[rows], axis=0, keepdims=True)
        kabs_ref[pl.ds(j * blocks + b, 1), :] = jnp.max(kf_abs[rows], axis=0, keepdims=True)

    for c in range(gate_cols // width):
        g = jnp.dot(hb, w_ref[:, (2 + c) * width:(3 + c) * width],
                    preferred_element_type=_F32)
        gates_ref[:, c * width:(c + 1) * width] = jax.nn.sigmoid(g).astype(_BF16)

    z = jnp.dot(hb, wf_ref[...], preferred_element_type=_F32) + bf_ref[...]
    log_f = jnp.minimum(z, 0.0) - jnp.log1p(jnp.exp(-jnp.abs(z)))
    r = lax.broadcasted_iota(jnp.int32, (LANES, LANES), 0)
    c = lax.broadcasted_iota(jnp.int32, (LANES, LANES), 1)
    tri = (c <= r).astype(_BF16)
    offset = carry_ref[0:1, :]
    blocks_cum = []
    for blk in range(tm // LANES):
        rest = log_f[blk * LANES:(blk + 1) * LANES]
        within = jnp.zeros((LANES, LANES), _F32)
        for _ in range(BIAS_PIECES):
            part = rest.astype(_BF16)
            rest = rest - part.astype(_F32)
            within = within + jnp.dot(tri, part, preferred_element_type=_F32)
        blocks_cum.append(within + offset)
        offset = offset + within[LANES - 1:LANES, :]
    cum = jnp.concatenate(blocks_cum, axis=0)
    carry_ref[...] = jnp.broadcast_to(offset, carry_ref.shape)

    head_lane = lax.broadcasted_iota(jnp.int32, (1, LANES), 1) < N_FOX_HEADS
    bias = jnp.where(head_lane, -cum * LOG2_E, 0.0)
    for b in range(blocks):
        bmax_ref[pl.ds(j * blocks + b, 1), :] = jnp.max(
            bias[b * MOBA_BLOCK:(b + 1) * MOBA_BLOCK], axis=0, keepdims=True)
    rest = bias
    placed = jnp.zeros((tm, LANES), _F32)
    for piece in range(BIAS_PIECES):
        part = rest.astype(_BF16).astype(_F32)
        rest = rest - part
        placed = placed + (pltpu.roll(part, N_FOX_HEADS * piece, 1) if piece else part)
    cb_ref[...] = placed.astype(_BF16)


def _inproj(x, g, w, wt, wf, bfp, cos, sin, cos_t, sin_t, *, tm):
    b, s, d = x.shape
    width = N_FOX_HEADS * HEAD_DIM
    gate_cols = w.shape[1] - 2 * width
    act = jax.ShapeDtypeStruct((b, s, width), _BF16)
    act_t = jax.ShapeDtypeStruct((b, width, s), _BF16)
    act_spec = pl.BlockSpec((None, tm, width), lambda bi, j: (bi, j, 0))
    act_t_spec = pl.BlockSpec((None, width, tm), lambda bi, j: (bi, 0, j))
    return pl.pallas_call(
        functools.partial(_inproj_body, tm=tm, width=width, gate_cols=gate_cols),
        grid=(b, s // tm),
        in_specs=[
            pl.BlockSpec((None, tm, d), lambda bi, j: (bi, j, 0)),
            _const_spec((1, d)),
            _const_spec(w.shape),
            _const_spec(wt.shape),
            _const_spec(wf.shape),
            _const_spec(bfp.shape),
            pl.BlockSpec((tm, LANES), lambda bi, j: (j, 0)),
            pl.BlockSpec((tm, LANES), lambda bi, j: (j, 0)),
            pl.BlockSpec((LANES, tm), lambda bi, j: (0, j)),
            pl.BlockSpec((LANES, tm), lambda bi, j: (0, j)),
        ],
        out_specs=[act_t_spec, act_spec, act_t_spec, act_t_spec, act_spec, act_t_spec,
                   pl.BlockSpec((None, tm, gate_cols), lambda bi, j: (bi, j, 0)),
                   pl.BlockSpec((None, tm, LANES), lambda bi, j: (bi, j, 0)),
                   pl.BlockSpec((None, LANES, width), lambda bi, j: (bi, 0, 0)),
                   pl.BlockSpec((None, LANES, width), lambda bi, j: (bi, 0, 0)),
                   pl.BlockSpec((None, LANES, LANES), lambda bi, j: (bi, 0, 0))],
        out_shape=[act_t, act, act_t, act_t, act, act_t,
                   jax.ShapeDtypeStruct((b, s, gate_cols), _BF16),
                   jax.ShapeDtypeStruct((b, s, LANES), _BF16),
                   jax.ShapeDtypeStruct((b, LANES, width), _F32),
                   jax.ShapeDtypeStruct((b, LANES, width), _F32),
                   jax.ShapeDtypeStruct((b, LANES, LANES), _F32)],
        scratch_shapes=[pltpu.VMEM((8, LANES), _F32)],
        compiler_params=_params("arbitrary", "arbitrary"),
        name="inproj",
    )(x, g, w, wt, wf, bfp, cos, sin, cos_t, sin_t)


def _attn_body(uq_ref, uk_ref, qt_ref, k_ref, kx_ref, vt_ref, aux_ref, aux2_ref, o_ref,
               w_ref, m_ref, l_ref, acc_ref, st_ref, p_ref, alpha_ref, pmax_ref, qn_ref, *,
               mode, tq, tk, nb, n_diag, n_units, pipe, lag):
    hp = pl.program_id(1)
    wide = HEADS_PER_STEP * tq
    nq = w_ref.shape[0]
    feat = lax.broadcasted_iota(jnp.int32, (LANES, 1), 0)
    xr = lax.broadcasted_iota(jnp.int32, (LANES, wide), 0)
    xc = lax.broadcasted_iota(jnp.int32, (LANES, wide), 1)

    def setup(i, carry):
        qt = qt_ref[:, pl.ds(pl.multiple_of(i * tq, tq), tq)]
        zero = jnp.zeros_like(qt)
        wq = jnp.concatenate([jnp.where(feat < HEAD_DIM, qt, zero),
                              jnp.where(feat < HEAD_DIM, zero, qt)], axis=1)
        if mode == "fox":
            head = hp * HEADS_PER_STEP + xc // tq
            lower = ((xr % N_FOX_HEADS == head) & (xr < BIAS_PIECES * N_FOX_HEADS)).astype(_BF16)
            qn_ref[i] = jnp.broadcast_to(
                jnp.sum(jnp.abs(wq.astype(_F32)), axis=0, keepdims=True), qn_ref.shape[1:])
        else:
            own = (i * tq) // MOBA_BLOCK
            gate = jnp.dot(aux_ref[...].astype(_BF16), wq, preferred_element_type=_F32)[0:nb]
            blk = lax.broadcasted_iota(jnp.int32, (nb, wide), 0)
            gate = jnp.where(blk < own, gate, -jnp.inf)
            keep = blk == own
            for _ in range(MOBA_TOPK):
                mx = jnp.max(gate, axis=0, keepdims=True)
                cand = jnp.where((gate == mx) & (mx > -jnp.inf), blk, nb)
                pick = blk == jnp.min(cand, axis=0, keepdims=True)
                keep = keep | pick
                gate = jnp.where(pick, -jnp.inf, gate)
            lower = jnp.where(keep, 0.0, NEG_BIG)
            if nb < LANES:
                lower = jnp.concatenate([lower, jnp.zeros((LANES - nb, wide), _F32)], axis=0)
            lower = lower.astype(_BF16)
        w_ref[i] = jnp.concatenate([wq, lower], axis=0)
        m_ref[i] = jnp.full(m_ref.shape[1:], NEG_BIG, _F32)
        l_ref[i] = jnp.zeros(l_ref.shape[1:], _F32)
        acc_ref[i] = jnp.zeros(acc_ref.shape[1:], _F32)
        return carry

    lax.fori_loop(0, nq, setup, 0, unroll=8)

    def unit(t):
        return uq_ref[t], uk_ref[t]

    def scores(t, slot, causal):
        q, kb = unit(t)
        start = pl.multiple_of(kb * tk, tk)
        ka = jnp.concatenate([k_ref[pl.ds(start, tk), :], kx_ref[pl.ds(start, tk), :]], axis=1)
        st = jnp.dot(ka, w_ref[q], preferred_element_type=_F32)
        if causal:
            key = start + lax.broadcasted_iota(jnp.int32, (tk, wide), 0)
            qry = q * tq + lax.broadcasted_iota(jnp.int32, (tk, wide), 1) % tq
            st = jnp.where(key <= qry, st, NEG_BIG)
        st_ref[slot] = st
        pmax_ref[slot] = jnp.max(st.reshape(tk // 8, 8, wide), axis=0)

    def softmax(t, slot):
        q, _ = unit(t)
        rows = SOFTMAX_ROWS
        m_prev = m_ref[q][0:1, :]
        m_new = jnp.maximum(m_prev, jnp.max(pmax_ref[slot], axis=0, keepdims=True))
        for r in range(0, tk, rows):
            p_ref[slot, r:r + rows, :] = jnp.exp2(st_ref[slot, r:r + rows, :] - m_new).astype(_BF16)
        m_ref[q] = jnp.broadcast_to(m_new, m_ref.shape[1:])
        alpha_ref[slot] = jnp.broadcast_to(jnp.exp2(m_prev - m_new), alpha_ref.shape[1:])

    def values(t, slot):
        q, kb = unit(t)
        start = pl.multiple_of(kb * tk, tk)
        ones = jnp.ones((ONES_ROWS, tk), _BF16)
        for h in range(HEADS_PER_STEP):
            cols = slice(h * tq, (h + 1) * tq)
            feats = slice(h * HEAD_DIM, (h + 1) * HEAD_DIM)
            lhs = jnp.concatenate([vt_ref[feats, pl.ds(start, tk)], ones], axis=0)
            pv = jnp.dot(lhs, p_ref[slot, :, cols], preferred_element_type=_F32)
            alpha = alpha_ref[slot, 0:1, cols]
            acc_ref[q, feats, :] = acc_ref[q, feats, :] * alpha + pv[0:HEAD_DIM]
            l_ref[q, :, cols] = jnp.broadcast_to(
                alpha * l_ref[q, 0:1, cols] + pv[HEAD_DIM:HEAD_DIM + 1], (8, tq))

    def step(base, c, causal, do_scores=True, do_softmax=True, do_values=True):
        u = base + c
        if do_scores:
            scores(u, c % pipe, causal)
        if do_values:
            values(u - 2 * lag, (c - 2 * lag) % pipe)
        if do_softmax:
            softmax(u - lag, (c - lag) % pipe)

    def group(causal):
        def body(j, carry):
            for c in range(pipe):
                step(j * pipe, c, causal)
            return carry
        return body

    def pipeline(g_first, g_end, causal_groups):
        for c in range(pipe):
            step(g_first * pipe, c, g_first < causal_groups,
                 do_softmax=c >= lag, do_values=c >= 2 * lag)
        if g_first + 1 < causal_groups:
            lax.fori_loop(g_first + 1, causal_groups, group(True), 0)
        lax.fori_loop(max(g_first + 1, causal_groups), g_end, group(False), 0)
        for c in range(2 * lag):
            step(g_end * pipe, c, False, do_scores=False, do_softmax=c < lag)

    diag_groups = n_diag // pipe
    if mode == "moba":
        pipeline(0, n_units // pipe, diag_groups)
    else:
        pipeline(0, diag_groups, diag_groups)
        nq_blocks = n_diag
        col = lax.broadcasted_iota(jnp.int32, (LANES, LANES), 1)
        first_head = lax.broadcasted_iota(jnp.int32, (1, wide), 1) < tq
        head0 = hp * HEADS_PER_STEP

        def per_head(table, pick0, pick1, fill):
            v0 = jnp.max(jnp.where(pick0, table, fill), axis=1, keepdims=True)[0:nb]
            v1 = jnp.max(jnp.where(pick1, table, fill), axis=1, keepdims=True)[0:nb]
            return jnp.where(first_head, v0, v1)

        bias_max = per_head(aux2_ref[...], col == head0, col == head0 + 1, NEG_BIG)
        k_absmax = per_head(aux_ref[...], col < HEAD_DIM, col >= HEAD_DIM, 0.0)
        blk = lax.broadcasted_iota(jnp.int32, (nb, 1), 0)

        needed = jnp.zeros((nb, 1), jnp.int32)
        for i in range(1, nq_blocks):
            bound = bias_max + qn_ref[i, 0:1, :] * k_absmax - m_ref[i, 0:1, :]
            live = jnp.max(bound, axis=1, keepdims=True) >= -UNDERFLOW_LOG2
            needed = jnp.maximum(needed, jnp.where(live & (blk < i), i - blk, 0))
        far = jnp.max(needed)
        n_live = far * nq_blocks - (far * (far + 1)) // 2
        live_groups = (n_live + pipe - 1) // pipe

        @pl.when(live_groups > 0)
        def _():
            pipeline(diag_groups, diag_groups + live_groups, diag_groups)

    def finish(i, carry):
        out_t = jnp.concatenate(
            [acc_ref[i, h * HEAD_DIM:(h + 1) * HEAD_DIM, :] / l_ref[i, 0:1, h * tq:(h + 1) * tq]
             for h in range(HEADS_PER_STEP)], axis=0)
        o_ref[pl.ds(pl.multiple_of(i * tq, tq), tq), :] = out_t.T.astype(o_ref.dtype)
        return carry

    lax.fori_loop(0, nq, finish, 0, unroll=2)


def _attention(qt, k, kx, vt, aux, aux2, *, mode):
    b, width, s = qt.shape
    tq = ATTN_Q
    tk, pipe, lag = FOX_PIPELINE if mode == "fox" else MOBA_PIPELINE
    nq = s // tq
    nb = s // MOBA_BLOCK
    wide = HEADS_PER_STEP * tq
    diag = [(i * tq) // tk for i in range(nq)]
    units = list(enumerate(diag))
    units += [(i, diag[i] - dist) for dist in range(1, nq) for i in range(nq) if diag[i] >= dist]
    assert mode != "fox" or tq == tk
    assert nq % pipe == 0 and len(units) % pipe == 0 and pipe >= 2 * lag
    unit_q = jnp.asarray([u[0] for u in units], jnp.int32)
    unit_k = jnp.asarray([u[1] for u in units], jnp.int32)
    aux_spec = pl.BlockSpec((None, LANES, LANES), lambda bi, hp, uq, uk: (bi, 0, hp))
    aux2_spec = pl.BlockSpec((None, LANES, LANES), lambda bi, hp, uq, uk: (bi, 0, 0))
    if mode == "fox":
        kx_spec = pl.BlockSpec((None, s, LANES), lambda bi, hp, uq, uk: (bi, 0, 0))
    else:
        kx_spec = pl.BlockSpec(kx.shape, lambda bi, hp, uq, uk: (0, 0))
    return pl.pallas_call(
        functools.partial(_attn_body, mode=mode, tq=tq, tk=tk, nb=nb, n_diag=nq, n_units=len(units),
                          pipe=pipe, lag=lag),
        grid_spec=pltpu.PrefetchScalarGridSpec(
            num_scalar_prefetch=2,
            grid=(b, width // LANES),
            in_specs=[
                pl.BlockSpec((None, LANES, s), lambda bi, hp, uq, uk: (bi, hp, 0)),
                pl.BlockSpec((None, s, LANES), lambda bi, hp, uq, uk: (bi, 0, hp)),
                kx_spec,
                pl.BlockSpec((None, LANES, s), lambda bi, hp, uq, uk: (bi, hp, 0)),
                aux_spec,
                aux2_spec,
            ],
            out_specs=pl.BlockSpec((None, s, LANES), lambda bi, hp, uq, uk: (bi, 0, hp)),
            scratch_shapes=[
                pltpu.VMEM((nq, 2 * LANES, wide), _BF16),
                pltpu.VMEM((nq, 8, wide), _F32),
                pltpu.VMEM((nq, 8, wide), _F32),
                pltpu.VMEM((nq, LANES, tq), _F32),
                pltpu.VMEM((pipe, tk, wide), _F32),
                pltpu.VMEM((pipe, tk, wide), _BF16),
                pltpu.VMEM((pipe, 8, wide), _F32),
                pltpu.VMEM((pipe, 8, wide), _F32),
                pltpu.VMEM((nq, 8, wide), _F32),
            ],
        ),
        out_shape=jax.ShapeDtypeStruct((b, s, width), _BF16),
        compiler_params=_params("arbitrary", "arbitrary"),
        name=mode,
    )(unit_q, unit_k, qt, k, kx, vt, aux, aux2)


def _pack_bf16_pair(a, b):
    lo = pltpu.bitcast(a.astype(_BF16).astype(_F32), jnp.uint32) >> 16
    hi = pltpu.bitcast(b.astype(_BF16).astype(_F32), jnp.uint32) & jnp.uint32(0xFFFF0000)
    return lo | hi


def _unpack_bf16_pair(u):
    lo = pltpu.bitcast(u << 16, _F32)
    hi = pltpu.bitcast(u & jnp.uint32(0xFFFF0000), _F32)
    return jnp.concatenate([lo, hi], axis=1)


def _postattn_body(yf_ref, ym_ref, gates_ref, x_ref, wfb_ref, wmb_ref, wout_ref, g_ref, wr_ref, br_ref,
                   x1_ref, h2_ref, route_ref, meta_ref, counts_ref, carry_ref, *, tm, d):
    step = pl.program_id(0)
    ya = jnp.dot(yf_ref[...], wfb_ref[...], preferred_element_type=_F32)
    yb = jnp.dot(ym_ref[...], wmb_ref[...], preferred_element_type=_F32)
    mixed = gates_ref[:, 0:d].astype(_F32) * ya + gates_ref[:, d:2 * d].astype(_F32) * yb
    x1 = x_ref[...] + jnp.dot(mixed.astype(_BF16), wout_ref[...], preferred_element_type=_F32)
    x1_ref[...] = x1
    h2 = _rms(x1, g_ref[...])
    h2_ref[...] = _pack_bf16_pair(h2[:, 0:d // 2], h2[:, d // 2:d])

    logits = jnp.dot(h2.astype(_BF16), wr_ref[...], preferred_element_type=_F32) + br_ref[...]
    logits_t = logits.T
    gl = logits_t[N_EXPERTS:N_EXPERTS + N_GROUPS]
    ge = jnp.exp(gl - jnp.max(gl, axis=0, keepdims=True))
    gp = ge / jnp.sum(ge, axis=0, keepdims=True)
    g_top = jnp.max(gp, axis=0, keepdims=True)
    g_row = lax.broadcasted_iota(jnp.int32, (N_GROUPS, tm), 0)
    g_idx = jnp.min(jnp.where(gp == g_top, g_row, N_GROUPS), axis=0, keepdims=True)
    fl = logits_t[0:EXPERTS_PER_GROUP]
    for g in range(1, N_GROUPS):
        fl = jnp.where(g_idx == g, logits_t[g * EXPERTS_PER_GROUP:(g + 1) * EXPERTS_PER_GROUP], fl)
    f_row = lax.broadcasted_iota(jnp.int32, (EXPERTS_PER_GROUP, tm), 0)
    f1 = jnp.max(fl, axis=0, keepdims=True)
    i1 = jnp.min(jnp.where(fl == f1, f_row, EXPERTS_PER_GROUP), axis=0, keepdims=True)
    fl2 = jnp.where(f_row == i1, -jnp.inf, fl)
    f2 = jnp.max(fl2, axis=0, keepdims=True)
    i2 = jnp.min(jnp.where(fl2 == f2, f_row, EXPERTS_PER_GROUP), axis=0, keepdims=True)
    e1 = g_idx * EXPERTS_PER_GROUP + i1
    e2 = g_idx * EXPERTS_PER_GROUP + i2
    t2 = jnp.exp(f2 - f1)
    w1 = g_top * (1.0 / (1.0 + t2))
    w2 = g_top * (t2 / (1.0 + t2))

    @pl.when(step == 0)
    def _():
        carry_ref[...] = jnp.zeros_like(carry_ref)

    x_row = lax.broadcasted_iota(jnp.int32, (N_EXPERTS, tm), 0)
    onehot = ((x_row == e1) | (x_row == e2)).astype(_F32)
    rr = lax.broadcasted_iota(jnp.int32, (tm, tm), 0)
    cc = lax.broadcasted_iota(jnp.int32, (tm, tm), 1)
    before = jnp.dot(onehot.astype(_BF16), (rr < cc).astype(_BF16), preferred_element_type=_F32)
    before = before + carry_ref[0:N_EXPERTS, 0:1]
    r1 = jnp.sum(jnp.where(x_row == e1, before, 0.0), axis=0, keepdims=True)
    r2 = jnp.sum(jnp.where(x_row == e2, before, 0.0), axis=0, keepdims=True)
    total = carry_ref[0:N_EXPERTS, 0:1] + jnp.sum(onehot, axis=1, keepdims=True)
    carry_ref[0:N_EXPERTS, :] = jnp.broadcast_to(total, (N_EXPERTS, LANES))
    counts_ref[...] = carry_ref[...]

    m_row = lax.broadcasted_iota(jnp.int32, (8, tm), 0)
    meta = jnp.where(m_row == 2, e1, 0)
    meta = jnp.where(m_row == 3, e2, meta)
    meta = jnp.where(m_row == 4, r1.astype(jnp.int32), meta)
    meta = jnp.where(m_row == 5, r2.astype(jnp.int32), meta)
    meta_ref[...] = meta
    w_row = lax.broadcasted_iota(jnp.int32, (LANES, tm), 0)
    route_ref[...] = jnp.where(w_row == 0, w1, jnp.where(w_row == 1, w2, 0.0)).T


def _postattn(yf, ym, gates, x, wfb, wmb, wout, g, wr, br, *, tm):
    t, d = x.shape
    width = yf.shape[1]
    row = lambda cols: pl.BlockSpec((tm, cols), lambda i: (i, 0))
    return pl.pallas_call(
        functools.partial(_postattn_body, tm=tm, d=d),
        grid=(t // tm,),
        in_specs=[row(width), row(width), row(2 * d), row(d),
                  _const_spec(wfb.shape), _const_spec(wmb.shape), _const_spec(wout.shape),
                  _const_spec((1, d)), _const_spec(wr.shape), _const_spec(br.shape)],
        out_specs=[row(d), row(d // 2), row(LANES), pl.BlockSpec((8, tm), lambda i: (0, i)),
                   _const_spec((LANES, LANES))],
        out_shape=[jax.ShapeDtypeStruct((t, d), _F32),
                   jax.ShapeDtypeStruct((t, d // 2), jnp.uint32),
                   jax.ShapeDtypeStruct((t, LANES), _F32),
                   jax.ShapeDtypeStruct((8, t), jnp.int32),
                   jax.ShapeDtypeStruct((LANES, LANES), _F32)],
        scratch_shapes=[pltpu.VMEM((LANES, LANES), _F32)],
        compiler_params=_params("arbitrary"),
        name="postattn",
    )(yf, ym, gates, x, wfb, wmb, wout, g, wr, br)


def _sc_scatter(rows, idx, n_out):
    n, width = rows.shape
    workers = SC_CORES * SC_SUBCORES
    per_worker = n // workers
    chunks = per_worker // SC_SCATTER_CHUNK
    assert n % (workers * SC_SCATTER_CHUNK) == 0 and idx.shape == (2, n // SC_SCATTER_CHUNK, SC_SCATTER_CHUNK)
    mesh = plsc.VectorSubcoreMesh(core_axis_name="c", subcore_axis_name="s",
                                  num_cores=SC_CORES, num_subcores=SC_SUBCORES)

    @functools.partial(
        pl.kernel, out_type=jax.ShapeDtypeStruct((n_out, width), rows.dtype), mesh=mesh,
        scratch_types=[pltpu.VMEM((2, chunks, SC_SCATTER_CHUNK), jnp.int32),
                       pltpu.VMEM((SC_SCATTER_CHUNK, width), rows.dtype),
                       pltpu.SemaphoreType.DMA],
        name="sc_scatter")
    def scatter(rows_hbm, idx_hbm, out_hbm, idx_v, rows_v, sem):
        worker = lax.axis_index("s") * SC_CORES + lax.axis_index("c")
        for k in range(2):
            pltpu.sync_copy(idx_hbm.at[k, pl.ds(worker * chunks, chunks)], idx_v.at[k])

        def chunk(j, carry):
            start = worker * per_worker + j * SC_SCATTER_CHUNK
            pltpu.sync_copy(rows_hbm.at[pl.ds(start, SC_SCATTER_CHUNK)], rows_v)
            for k in range(2):
                pltpu.async_copy(rows_v, out_hbm.at[idx_v.at[k, j]], sem).wait()
            return carry

        lax.fori_loop(0, chunks, chunk, 0)

    return scatter(rows, idx)


def _experts_body(be_ref, valid_ref, nused_ref, xs_ref, wg_ref, wu_ref, wd_ref, out_ref,
                  wgb_ref, wub_ref, wdb_ref):
    j = pl.program_id(0)
    used = j < nused_ref[0]

    @pl.when(jnp.logical_not(used))
    def _():
        out_ref[...] = jnp.zeros_like(out_ref)

    @pl.when(used & ((j == 0) | (be_ref[j] != be_ref[jnp.maximum(j - 1, 0)])))
    def _():
        wgb_ref[...] = wg_ref[...].astype(_BF16)
        wub_ref[...] = wu_ref[...].astype(_BF16)
        wdb_ref[...] = wd_ref[...].astype(_BF16)

    @pl.when(used)
    def _():
        row = lax.broadcasted_iota(jnp.int32, xs_ref.shape, 0)
        packed = jnp.where(row < valid_ref[j], xs_ref[...], jnp.uint32(0))
        rows = _unpack_bf16_pair(packed).astype(_BF16)
        hidden = wgb_ref.shape[1]
        out = None
        for c in range(0, hidden, hidden // EXPERT_PARTS):
            cols = slice(c, c + hidden // EXPERT_PARTS)
            gate = jnp.dot(rows, wgb_ref[:, cols], preferred_element_type=_F32)
            up = jnp.dot(rows, wub_ref[:, cols], preferred_element_type=_F32)
            act = ((gate * jax.nn.sigmoid(gate)) * up).astype(_BF16)
            part = jnp.dot(act, wdb_ref[cols, :], preferred_element_type=_F32)
            out = part if out is None else out + part
        half = out.shape[1] // 2
        out_ref[...] = _pack_bf16_pair(out[:, 0:half], out[:, half:])


def _experts(block_expert, block_valid, n_used, xs, wg, wu, wd, *, te):
    n_rows, half = xs.shape
    _, d, de = wg.shape

    def blk(j, be, bv, nu):
        return jnp.minimum(j, nu[0] - 1)

    def weight(j, be, bv, nu):
        return (be[blk(j, be, bv, nu)], 0, 0)

    return pl.pallas_call(
        _experts_body,
        grid_spec=pltpu.PrefetchScalarGridSpec(
            num_scalar_prefetch=3,
            grid=(n_rows // te,),
            in_specs=[pl.BlockSpec((te, half), lambda j, be, bv, nu: (blk(j, be, bv, nu), 0)),
                      pl.BlockSpec((None, d, de), weight),
                      pl.BlockSpec((None, d, de), weight),
                      pl.BlockSpec((None, de, d), weight)],
            out_specs=pl.BlockSpec((te, d // 2), lambda j, be, bv, nu: (j, 0)),
            scratch_shapes=[pltpu.VMEM((d, de), _BF16), pltpu.VMEM((d, de), _BF16),
                            pltpu.VMEM((de, d), _BF16)],
        ),
        out_shape=jax.ShapeDtypeStruct((n_rows, d // 2), jnp.uint32),
        compiler_params=_params("arbitrary"),
        name="experts",
    )(block_expert, block_valid, n_used, xs, wg, wu, wd)


def _sc_gather(table, idx):
    n, = idx.shape
    width = table.shape[1]
    workers = SC_CORES * SC_SUBCORES
    per_worker = n // workers
    chunks = per_worker // SC_CHUNK
    assert n % (workers * SC_CHUNK * 2) == 0
    mesh = plsc.VectorSubcoreMesh(core_axis_name="c", subcore_axis_name="s",
                                  num_cores=SC_CORES, num_subcores=SC_SUBCORES)

    @functools.partial(
        pl.kernel, out_type=jax.ShapeDtypeStruct((n, width), table.dtype), mesh=mesh,
        scratch_types=[pltpu.VMEM((per_worker,), jnp.int32),
                       pltpu.VMEM((2, SC_CHUNK, width), table.dtype),
                       pltpu.SemaphoreType.DMA((2,))],
        name="sc_gather")
    def gather(table_hbm, idx_hbm, out_hbm, idx_v, rows_v, sems):
        worker = lax.axis_index("s") * SC_CORES + lax.axis_index("c")
        base = worker * per_worker
        pltpu.sync_copy(idx_hbm.at[pl.ds(base, per_worker)], idx_v)

        def fetch(j, slot):
            off = pl.multiple_of(j * SC_CHUNK, SC_CHUNK)
            return pltpu.async_copy(table_hbm.at[idx_v.at[pl.ds(off, SC_CHUNK)]], rows_v.at[slot],
                                    sems.at[slot])

        def drain(j, slot):
            off = pl.multiple_of(j * SC_CHUNK, SC_CHUNK)
            pltpu.make_async_copy(out_hbm.at[pl.ds(base + off, SC_CHUNK)], rows_v.at[slot],
                                  sems.at[slot]).wait()
            pltpu.sync_copy(rows_v.at[slot], out_hbm.at[pl.ds(base + off, SC_CHUNK)])

        fetch(0, 0)

        def pair(p, carry):
            fetch(2 * p + 1, 1)
            drain(2 * p, 0)

            @pl.when(p + 1 < chunks // 2)
            def _():
                fetch(2 * p + 2, 0)

            drain(2 * p + 1, 1)
            return carry

        lax.fori_loop(0, chunks // 2, pair, 0)

    return gather(table, idx)


def _final_body(x1_ref, route_ref, p_ref, r1_ref, r2_ref, gp_ref, wpg_ref, wpp_ref, gf_ref, o_ref):
    emb = jnp.dot(p_ref[...].astype(_BF16), wpp_ref[...], preferred_element_type=_F32)
    route = route_ref[...]
    x2 = x1_ref[...] + (_unpack_bf16_pair(r1_ref[...]) * route[:, 0:1]
                        + _unpack_bf16_pair(r2_ref[...]) * route[:, 1:2])
    gate = jax.nn.sigmoid(jnp.dot(_rms(x2, gp_ref[...]).astype(_BF16), wpg_ref[...],
                                  preferred_element_type=_F32))
    o_ref[...] = _rms(x2 + gate * emb, gf_ref[...])


def _final(x1, route, p, slot_rows, gp, wpg, wpp, gf, *, tm):
    t, d = x1.shape
    ple = p.shape[1]
    steps = t // tm
    row = lambda cols: pl.BlockSpec((tm, cols), lambda i: (i, 0))
    return pl.pallas_call(
        _final_body,
        grid=(steps,),
        in_specs=[row(d), row(LANES), row(ple), row(d // 2),
                  pl.BlockSpec((tm, d // 2), lambda i: (steps + i, 0)),
                  _const_spec((1, d)), _const_spec(wpg.shape), _const_spec(wpp.shape),
                  _const_spec((1, d))],
        out_specs=row(d),
        out_shape=jax.ShapeDtypeStruct((t, d), _F32),
        compiler_params=_params("arbitrary"),
        name="final",
    )(x1, route, p, slot_rows, slot_rows, gp, wpg, wpp, gf)


def _rope_tables(s):
    half = HEAD_DIM // 2
    inv = 1.0 / (ROPE_THETA ** (jnp.arange(0, HEAD_DIM, 2, dtype=_F32) / HEAD_DIM))
    ang = jnp.arange(s, dtype=_F32)[:, None] * inv[None, :]
    cos, sin = jnp.cos(ang), jnp.sin(ang)
    reps = LANES // half
    cos_t = jnp.tile(cos, (1, reps))
    sin_t = jnp.tile(jnp.concatenate([-sin, sin], axis=1), (1, reps // 2))
    return cos_t, sin_t


def _layer(x, p, attn_norm, w_in, b_forget, w_fox_branch, w_moba_branch, w_out, moe_norm,
           w_group, b_group, w_fine, b_fine, w_gate, w_up, w_down, ple_norm, w_ple_gate, w_ple_proj,
           final_gain):
    b, s, d = x.shape
    t = b * s
    width = N_FOX_HEADS * HEAD_DIM
    assert N_MOBA_HEADS * HEAD_DIM == width and MOBA_BLOCK % ATTN_Q == 0 and MOBA_PIPELINE[0] % MOBA_BLOCK == 0
    assert BIAS_PIECES * N_FOX_HEADS <= LANES and s % MOBA_PIPELINE[0] == 0
    assert s % ROW_TILE == 0 and ROW_TILE % MOBA_BLOCK == 0 and s // MOBA_BLOCK <= LANES
    assert t % FINAL_TILE == 0 and N_EXPERTS + N_GROUPS <= LANES

    qkv_cols = 6 * width
    f_cols = N_FOX_HEADS
    chunk = lambda c: w_in[:, c * width:(c + 1) * width]
    w_rows = jnp.concatenate([chunk(1), chunk(4), w_in[:, qkv_cols + f_cols:]], axis=1).astype(_BF16)
    w_cols = jnp.concatenate([chunk(0), chunk(2), chunk(3), chunk(5)], axis=1).T.astype(_BF16)
    wf = jnp.zeros((d, LANES), _BF16).at[:, :f_cols].set(w_in[:, qkv_cols:qkv_cols + f_cols].astype(_BF16))
    bfp = jnp.zeros((1, LANES), _F32).at[0, :f_cols].set(b_forget.astype(_F32))
    cos, sin = _rope_tables(s)

    qft, kf, vft, qmt, km, vmt, gates, cb, kmean, kabs, bmax = _inproj(
        x, attn_norm.reshape(1, d), w_rows, w_cols, wf, bfp, cos, sin, cos.T, sin.T, tm=ROW_TILE)
    y_fox = _attention(qft, kf, cb, vft, kabs, bmax, mode="fox")
    block_of_key = jnp.arange(s, dtype=jnp.int32)[:, None] // MOBA_BLOCK
    block_onehot = (block_of_key == jnp.arange(LANES, dtype=jnp.int32)[None, :]).astype(_BF16)
    y_moba = _attention(qmt, km, block_onehot, vmt, kmean, bmax, mode="moba")

    wr = jnp.zeros((d, LANES), _BF16)
    wr = wr.at[:, :N_EXPERTS].set(w_fine.astype(_BF16))
    wr = wr.at[:, N_EXPERTS:N_EXPERTS + N_GROUPS].set(w_group.astype(_BF16))
    br = jnp.zeros((1, LANES), _F32)
    br = br.at[0, :N_EXPERTS].set(b_fine.astype(_F32))
    br = br.at[0, N_EXPERTS:N_EXPERTS + N_GROUPS].set(b_group.astype(_F32))
    x1, h2p, route, meta, counts = _postattn(
        y_fox.reshape(t, width), y_moba.reshape(t, width), gates.reshape(t, 2 * d), x.reshape(t, d),
        w_fox_branch.astype(_BF16), w_moba_branch.astype(_BF16), w_out.astype(_BF16),
        moe_norm.reshape(1, d), wr, br, tm=ROW_TILE)

    te = EXPERT_TILE
    n_blk = (2 * t) // te + N_EXPERTS
    counts = counts[:N_EXPERTS, 0].astype(jnp.int32)
    blocks_per = (counts + te - 1) // te
    block_end = jnp.cumsum(blocks_per)
    row_start = (block_end - blocks_per) * te
    expert_ids = jnp.arange(N_EXPERTS, dtype=jnp.int32)[:, None, None]
    slot_major = meta[4:6] + jnp.sum(
        jnp.where(meta[2:4][None] == expert_ids, row_start[:, None, None], 0), axis=0)
    n_used = block_end[-1:].astype(jnp.int32)
    block_ids = jnp.arange(n_blk, dtype=jnp.int32)
    block_expert = jnp.minimum(
        jnp.sum((block_ids[:, None] >= block_end[None, :]).astype(jnp.int32), axis=1), N_EXPERTS - 1)

    first_block = block_end - blocks_per
    block_valid = jnp.clip(counts[block_expert] - (block_ids - first_block[block_expert]) * te, 0, te)
    xs = _sc_scatter(h2p, slot_major.reshape(2, t // SC_SCATTER_CHUNK, SC_SCATTER_CHUNK), n_blk * te)
    rows = _experts(block_expert, block_valid, n_used, xs, w_gate, w_up, w_down, te=te)
    slot_rows = _sc_gather(rows, slot_major.reshape(-1))
    out = _final(x1, route, p.reshape(t, -1), slot_rows, ple_norm.reshape(1, d),
                 w_ple_gate.astype(_BF16), w_ple_proj.astype(_BF16), final_gain.reshape(1, d),
                 tm=FINAL_TILE)
    return out.reshape(b, s, d)


def kernel(x, p, attn_norm, w_in, b_forget, w_fox_branch, w_moba_branch, w_out, moe_norm, w_group,
           b_group, w_fine, b_fine, w_gate, w_up, w_down, ple_norm, w_ple_gate, w_ple_proj, final_norm):
    depth = p.shape[0]
    assert depth == 1, "the final norm is fused into the last layer's kernel"
    i = 0
    return _layer(x, p[i], attn_norm[i], w_in[i], b_forget[i], w_fox_branch[i], w_moba_branch[i],
                  w_out[i], moe_norm[i], w_group[i], b_group[i], w_fine[i], b_fine[i], w_gate[i],
                  w_up[i], w_down[i], ple_norm[i], w_ple_gate[i], w_ple_proj[i], final_norm)
```

```python
import functools

import jax
import jax.numpy as jnp
from jax import lax
from jax.experimental import pallas as pl
from jax.experimental.pallas import tpu as pltpu
from jax.experimental.pallas import tpu_sc as plsc

HEAD_DIM = 64
N_FOX_HEADS = 8
N_MOBA_HEADS = 8
MOBA_BLOCK = 256
MOBA_TOPK = 3
ROPE_THETA = 10000.0
N_GROUPS = 4
EXPERTS_PER_GROUP = 8
N_EXPERTS = N_GROUPS * EXPERTS_PER_GROUP
RMS_EPS = 1e-6

LANES = 128
HEADS_PER_STEP = LANES // HEAD_DIM
ATTN_Q = 256
FOX_PIPELINE = (256, 8, 4)
MOBA_PIPELINE = (256, 8, 4)
ONES_ROWS = 16
SOFTMAX_ROWS = 64
BIAS_PIECES = 3
ROW_TILE = 512
SC_CORES = 2
SC_SUBCORES = 16
SC_CHUNK = 64
SC_SCATTER_CHUNK = 128
FINAL_TILE = 1024
EXPERT_PARTS = 2
EXPERT_TILE = 512
LOG2_E = 1.4426950408889634
UNDERFLOW_LOG2 = 160.0
NEG_BIG = -1e30
VMEM_LIMIT = 48 * 1024 * 1024

_BF16 = jnp.bfloat16
_F32 = jnp.float32


def _params(*sem):
    return pltpu.CompilerParams(dimension_semantics=sem, vmem_limit_bytes=VMEM_LIMIT)


def _rms(x, g):
    return x * lax.rsqrt(jnp.mean(x * x, axis=-1, keepdims=True) + RMS_EPS) * g


def _const_spec(shape):
    return pl.BlockSpec(shape, lambda *_: (0,) * len(shape))


def _inproj_body(x_ref, g_ref, w_ref, wt_ref, wf_ref, bf_ref, cos_ref, sin_ref, cost_ref, sint_ref,
                 qft_ref, kf_ref, vft_ref, qmt_ref, km_ref, vmt_ref, gates_ref, cb_ref, kmean_ref,
                 kabs_ref, bmax_ref,
                 carry_ref, *, tm, width, gate_cols):
    j = pl.program_id(1)
    h = _rms(x_ref[...], g_ref[...])
    hb = h.astype(_BF16)
    hbt = h.T.astype(_BF16)
    reps = width // LANES
    half = HEAD_DIM // 2

    def proj(c):
        return jnp.dot(hb, w_ref[:, c * width:(c + 1) * width], preferred_element_type=_F32)

    def proj_t(c):
        return jnp.dot(wt_ref[c * width:(c + 1) * width, :], hbt, preferred_element_type=_F32)

    def rope(t):
        cos = jnp.concatenate([cos_ref[...]] * reps, axis=1)
        sin = jnp.concatenate([sin_ref[...]] * reps, axis=1)
        first = (lax.broadcasted_iota(jnp.int32, (1, width), 1) % HEAD_DIM) < half
        partner = jnp.where(first, pltpu.roll(t, width - half, 1), pltpu.roll(t, half, 1))
        return t * cos + partner * sin

    def rope_t(t):
        cos = jnp.concatenate([cost_ref[...]] * reps, axis=0)
        sin = jnp.concatenate([sint_ref[...]] * reps, axis=0)
        first = (lax.broadcasted_iota(jnp.int32, (width, 1), 0) % HEAD_DIM) < half
        partner = jnp.where(first, pltpu.roll(t, width - half, 0), pltpu.roll(t, half, 0))
        return t * cos + partner * sin

    scale = HEAD_DIM ** -0.5 * LOG2_E
    qft_ref[...] = (proj_t(0) * scale).astype(_BF16)
    vft_ref[...] = proj_t(1).astype(_BF16)
    qmt_ref[...] = (rope_t(proj_t(2)) * scale).astype(_BF16)
    vmt_ref[...] = proj_t(3).astype(_BF16)
    kf = proj(0).astype(_BF16)
    kf_ref[...] = kf
    kf_abs = jnp.abs(kf.astype(_F32))
    km = rope(proj(1))
    km_ref[...] = km.astype(_BF16)

    @pl.when(j == 0)
    def _():
        kmean_ref[...] = jnp.zeros_like(kmean_ref)
        kabs_ref[...] = jnp.zeros_like(kabs_ref)
        bmax_ref[...] = jnp.zeros_like(bmax_ref)
        carry_ref[...] = jnp.zeros_like(carry_ref)

    blocks = tm // MOBA_BLOCK
    for b in range(blocks):
        rows = slice(b * MOBA_BLOCK, (b + 1) * MOBA_BLOCK)
        kmean_ref[pl.ds(j * blocks + b, 1), :] = jnp.mean(km[rows], axis=0, keepdims=True)
        kabs_ref[pl.ds(j * blocks + b, 1), :] = jnp.max(kf_abs[rows], axis=0, keepdims=True)

    for c in range(gate_cols // width):
        g = jnp.dot(hb, w_ref[:, (2 + c) * width:(3 + c) * width],
                    preferred_element_type=_F32)
        gates_ref[:, c * width:(c + 1) * width] = jax.nn.sigmoid(g).astype(_BF16)

    z = jnp.dot(hb, wf_ref[...], preferred_element_type=_F32) + bf_ref[...]
    log_f = jnp.minimum(z, 0.0) - jnp.log1p(jnp.exp(-jnp.abs(z)))
    r = lax.broadcasted_iota(jnp.int32, (LANES, LANES), 0)
    c = lax.broadcasted_iota(jnp.int32, (LANES, LANES), 1)
    tri = (c <= r).astype(_BF16)
    offset = carry_ref[0:1, :]
    blocks_cum = []
    for blk in range(tm // LANES):
        rest = log_f[blk * LANES:(blk + 1) * LANES]
        within = jnp.zeros((LANES, LANES), _F32)
        for _ in range(BIAS_PIECES):
            part = rest.astype(_BF16)
            rest = rest - part.astype(_F32)
            within = within + jnp.dot(tri, part, preferred_element_type=_F32)
        blocks_cum.append(within + offset)
        offset = offset + within[LANES - 1:LANES, :]
    cum = jnp.concatenate(blocks_cum, axis=0)
    carry_ref[...] = jnp.broadcast_to(offset, carry_ref.shape)

    head_lane = lax.broadcasted_iota(jnp.int32, (1, LANES), 1) < N_FOX_HEADS
    bias = jnp.where(head_lane, -cum * LOG2_E, 0.0)
    for b in range(blocks):
        bmax_ref[pl.ds(j * blocks + b, 1), :] = jnp.max(
            bias[b * MOBA_BLOCK:(b + 1) * MOBA_BLOCK], axis=0, keepdims=True)
    rest = bias
    placed = jnp.zeros((tm, LANES), _F32)
    for piece in range(BIAS_PIECES):
        part = rest.astype(_BF16).astype(_F32)
        rest = rest - part
        placed = placed + (pltpu.roll(part, N_FOX_HEADS * piece, 1) if piece else part)
    cb_ref[...] = placed.astype(_BF16)


def _inproj(x, g, w, wt, wf, bfp, cos, sin, cos_t, sin_t, *, tm):
    b, s, d = x.shape
    width = N_FOX_HEADS * HEAD_DIM
    gate_cols = w.shape[1] - 2 * width
    act = jax.ShapeDtypeStruct((b, s, width), _BF16)
    act_t = jax.ShapeDtypeStruct((b, width, s), _BF16)
    act_spec = pl.BlockSpec((None, tm, width), lambda bi, j: (bi, j, 0))
    act_t_spec = pl.BlockSpec((None, width, tm), lambda bi, j: (bi, 0, j))
    return pl.pallas_call(
        functools.partial(_inproj_body, tm=tm, width=width, gate_cols=gate_cols),
        grid=(b, s // tm),
        in_specs=[
            pl.BlockSpec((None, tm, d), lambda bi, j: (bi, j, 0)),
            _const_spec((1, d)),
            _const_spec(w.shape),
            _const_spec(wt.shape),
            _const_spec(wf.shape),
            _const_spec(bfp.shape),
            pl.BlockSpec((tm, LANES), lambda bi, j: (j, 0)),
            pl.BlockSpec((tm, LANES), lambda bi, j: (j, 0)),
            pl.BlockSpec((LANES, tm), lambda bi, j: (0, j)),
            pl.BlockSpec((LANES, tm), lambda bi, j: (0, j)),
        ],
        out_specs=[act_t_spec, act_spec, act_t_spec, act_t_spec, act_spec, act_t_spec,
                   pl.BlockSpec((None, tm, gate_cols), lambda bi, j: (bi, j, 0)),
                   pl.BlockSpec((None, tm, LANES), lambda bi, j: (bi, j, 0)),
                   pl.BlockSpec((None, LANES, width), lambda bi, j: (bi, 0, 0)),
                   pl.BlockSpec((None, LANES, width), lambda bi, j: (bi, 0, 0)),
                   pl.BlockSpec((None, LANES, LANES), lambda bi, j: (bi, 0, 0))],
        out_shape=[act_t, act, act_t, act_t, act, act_t,
                   jax.ShapeDtypeStruct((b, s, gate_cols), _BF16),
                   jax.ShapeDtypeStruct((b, s, LANES), _BF16),
                   jax.ShapeDtypeStruct((b, LANES, width), _F32),
                   jax.ShapeDtypeStruct((b, LANES, width), _F32),
                   jax.ShapeDtypeStruct((b, LANES, LANES), _F32)],
        scratch_shapes=[pltpu.VMEM((8, LANES), _F32)],
        compiler_params=_params("arbitrary", "arbitrary"),
        name="inproj",
    )(x, g, w, wt, wf, bfp, cos, sin, cos_t, sin_t)


def _attn_body(uq_ref, uk_ref, qt_ref, k_ref, kx_ref, vt_ref, aux_ref, aux2_ref, o_ref,
               w_ref, m_ref, l_ref, acc_ref, st_ref, p_ref, alpha_ref, pmax_ref, qn_ref, *,
               mode, tq, tk, nb, n_diag, n_units, pipe, lag):
    hp = pl.program_id(1)
    wide = HEADS_PER_STEP * tq
    nq = w_ref.shape[0]
    feat = lax.broadcasted_iota(jnp.int32, (LANES, 1), 0)
    xr = lax.broadcasted_iota(jnp.int32, (LANES, wide), 0)
    xc = lax.broadcasted_iota(jnp.int32, (LANES, wide), 1)

    def setup(i, carry):
        qt = qt_ref[:, pl.ds(pl.multiple_of(i * tq, tq), tq)]
        zero = jnp.zeros_like(qt)
        wq = jnp.concatenate([jnp.where(feat < HEAD_DIM, qt, zero),
                              jnp.where(feat < HEAD_DIM, zero, qt)], axis=1)
        if mode == "fox":
            head = hp * HEADS_PER_STEP + xc // tq
            lower = ((xr % N_FOX_HEADS == head) & (xr < BIAS_PIECES * N_FOX_HEADS)).astype(_BF16)
            qn_ref[i] = jnp.broadcast_to(
                jnp.sum(jnp.abs(wq.astype(_F32)), axis=0, keepdims=True), qn_ref.shape[1:])
        else:
            own = (i * tq) // MOBA_BLOCK
            gate = jnp.dot(aux_ref[...].astype(_BF16), wq, preferred_element_type=_F32)[0:nb]
            blk = lax.broadcasted_iota(jnp.int32, (nb, wide), 0)
            gate = jnp.where(blk < own, gate, -jnp.inf)
            keep = blk == own
            for _ in range(MOBA_TOPK):
                mx = jnp.max(gate, axis=0, keepdims=True)
                cand = jnp.where((gate == mx) & (mx > -jnp.inf), blk, nb)
                pick = blk == jnp.min(cand, axis=0, keepdims=True)
                keep = keep | pick
                gate = jnp.where(pick, -jnp.inf, gate)
            lower = jnp.where(keep, 0.0, NEG_BIG)
            if nb < LANES:
                lower = jnp.concatenate([lower, jnp.zeros((LANES - nb, wide), _F32)], axis=0)
            lower = lower.astype(_BF16)
        w_ref[i] = jnp.concatenate([wq, lower], axis=0)
        m_ref[i] = jnp.full(m_ref.shape[1:], NEG_BIG, _F32)
        l_ref[i] = jnp.zeros(l_ref.shape[1:], _F32)
        acc_ref[i] = jnp.zeros(acc_ref.shape[1:], _F32)
        return carry

    lax.fori_loop(0, nq, setup, 0, unroll=8)

    def unit(t):
        return uq_ref[t], uk_ref[t]

    def scores(t, slot, causal):
        q, kb = unit(t)
        start = pl.multiple_of(kb * tk, tk)
        ka = jnp.concatenate([k_ref[pl.ds(start, tk), :], kx_ref[pl.ds(start, tk), :]], axis=1)
        st = jnp.dot(ka, w_ref[q], preferred_element_type=_F32)
        if causal:
            key = start + lax.broadcasted_iota(jnp.int32, (tk, wide), 0)
            qry = q * tq + lax.broadcasted_iota(jnp.int32, (tk, wide), 1) % tq
            st = jnp.where(key <= qry, st, NEG_BIG)
        st_ref[slot] = st
        pmax_ref[slot] = jnp.max(st.reshape(tk // 8, 8, wide), axis=0)

    def softmax(t, slot):
        q, _ = unit(t)
        rows = SOFTMAX_ROWS
        m_prev = m_ref[q][0:1, :]
        m_new = jnp.maximum(m_prev, jnp.max(pmax_ref[slot], axis=0, keepdims=True))
        for r in range(0, tk, rows):
            p_ref[slot, r:r + rows, :] = jnp.exp2(st_ref[slot, r:r + rows, :] - m_new).astype(_BF16)
        m_ref[q] = jnp.broadcast_to(m_new, m_ref.shape[1:])
        alpha_ref[slot] = jnp.broadcast_to(jnp.exp2(m_prev - m_new), alpha_ref.shape[1:])

    def values(t, slot):
        q, kb = unit(t)
        start = pl.multiple_of(kb * tk, tk)
        ones = jnp.ones((ONES_ROWS, tk), _BF16)
        for h in range(HEADS_PER_STEP):
            cols = slice(h * tq, (h + 1) * tq)
            feats = slice(h * HEAD_DIM, (h + 1) * HEAD_DIM)
            lhs = jnp.concatenate([vt_ref[feats, pl.ds(start, tk)], ones], axis=0)
            pv = jnp.dot(lhs, p_ref[slot, :, cols], preferred_element_type=_F32)
            alpha = alpha_ref[slot, 0:1, cols]
            acc_ref[q, feats, :] = acc_ref[q, feats, :] * alpha + pv[0:HEAD_DIM]
            l_ref[q, :, cols] = jnp.broadcast_to(
                alpha * l_ref[q, 0:1, cols] + pv[HEAD_DIM:HEAD_DIM + 1], (8, tq))

    def step(base, c, causal, do_scores=True, do_softmax=True, do_values=True):
        u = base + c
        if do_scores:
            scores(u, c % pipe, causal)
        if do_values:
            values(u - 2 * lag, (c - 2 * lag) % pipe)
        if do_softmax:
            softmax(u - lag, (c - lag) % pipe)

    def group(causal):
        def body(j, carry):
            for c in range(pipe):
                step(j * pipe, c, causal)
            return carry
        return body

    def pipeline(g_first, g_end, causal_groups):
        for c in range(pipe):
            step(g_first * pipe, c, g_first < causal_groups,
                 do_softmax=c >= lag, do_values=c >= 2 * lag)
        if g_first + 1 < causal_groups:
            lax.fori_loop(g_first + 1, causal_groups, group(True), 0)
        lax.fori_loop(max(g_first + 1, causal_groups), g_end, group(False), 0)
        for c in range(2 * lag):
            step(g_end * pipe, c, False, do_scores=False, do_softmax=c < lag)

    diag_groups = n_diag // pipe
    if mode == "moba":
        pipeline(0, n_units // pipe, diag_groups)
    else:
        pipeline(0, diag_groups, diag_groups)
        nq_blocks = n_diag
        col = lax.broadcasted_iota(jnp.int32, (LANES, LANES), 1)
        first_head = lax.broadcasted_iota(jnp.int32, (1, wide), 1) < tq
        head0 = hp * HEADS_PER_STEP

        def per_head(table, pick0, pick1, fill):
            v0 = jnp.max(jnp.where(pick0, table, fill), axis=1, keepdims=True)[0:nb]
            v1 = jnp.max(jnp.where(pick1, table, fill), axis=1, keepdims=True)[0:nb]
            return jnp.where(first_head, v0, v1)

        bias_max = per_head(aux2_ref[...], col == head0, col == head0 + 1, NEG_BIG)
        k_absmax = per_head(aux_ref[...], col < HEAD_DIM, col >= HEAD_DIM, 0.0)
        blk = lax.broadcasted_iota(jnp.int32, (nb, 1), 0)

        needed = jnp.zeros((nb, 1), jnp.int32)
        for i in range(1, nq_blocks):
            bound = bias_max + qn_ref[i, 0:1, :] * k_absmax - m_ref[i, 0:1, :]
            live = jnp.max(bound, axis=1, keepdims=True) >= -UNDERFLOW_LOG2
            needed = jnp.maximum(needed, jnp.where(live & (blk < i), i - blk, 0))
        far = jnp.max(needed)
        n_live = far * nq_blocks - (far * (far + 1)) // 2
        live_groups = (n_live + pipe - 1) // pipe

        @pl.when(live_groups > 0)
        def _():
            pipeline(diag_groups, diag_groups + live_groups, diag_groups)

    def finish(i, carry):
        out_t = jnp.concatenate(
            [acc_ref[i, h * HEAD_DIM:(h + 1) * HEAD_DIM, :] / l_ref[i, 0:1, h * tq:(h + 1) * tq]
             for h in range(HEADS_PER_STEP)], axis=0)
        o_ref[pl.ds(pl.multiple_of(i * tq, tq), tq), :] = out_t.T.astype(o_ref.dtype)
        return carry

    lax.fori_loop(0, nq, finish, 0, unroll=8)


def _attention(qt, k, kx, vt, aux, aux2, *, mode):
    b, width, s = qt.shape
    tq = ATTN_Q
    tk, pipe, lag = FOX_PIPELINE if mode == "fox" else MOBA_PIPELINE
    nq = s // tq
    nb = s // MOBA_BLOCK
    wide = HEADS_PER_STEP * tq
    diag = [(i * tq) // tk for i in range(nq)]
    units = list(enumerate(diag))
    units += [(i, diag[i] - dist) for dist in range(1, nq) for i in range(nq) if diag[i] >= dist]
    assert mode != "fox" or tq == tk
    assert nq % pipe == 0 and len(units) % pipe == 0 and pipe >= 2 * lag
    unit_q = jnp.asarray([u[0] for u in units], jnp.int32)
    unit_k = jnp.asarray([u[1] for u in units], jnp.int32)
    aux_spec = pl.BlockSpec((None, LANES, LANES), lambda bi, hp, uq, uk: (bi, 0, hp))
    aux2_spec = pl.BlockSpec((None, LANES, LANES), lambda bi, hp, uq, uk: (bi, 0, 0))
    if mode == "fox":
        kx_spec = pl.BlockSpec((None, s, LANES), lambda bi, hp, uq, uk: (bi, 0, 0))
    else:
        kx_spec = pl.BlockSpec(kx.shape, lambda bi, hp, uq, uk: (0, 0))
    return pl.pallas_call(
        functools.partial(_attn_body, mode=mode, tq=tq, tk=tk, nb=nb, n_diag=nq, n_units=len(units),
                          pipe=pipe, lag=lag),
        grid_spec=pltpu.PrefetchScalarGridSpec(
            num_scalar_prefetch=2,
            grid=(b, width // LANES),
            in_specs=[
                pl.BlockSpec((None, LANES, s), lambda bi, hp, uq, uk: (bi, hp, 0)),
                pl.BlockSpec((None, s, LANES), lambda bi, hp, uq, uk: (bi, 0, hp)),
                kx_spec,
                pl.BlockSpec((None, LANES, s), lambda bi, hp, uq, uk: (bi, hp, 0)),
                aux_spec,
                aux2_spec,
            ],
            out_specs=pl.BlockSpec((None, s, LANES), lambda bi, hp, uq, uk: (bi, 0, hp)),
            scratch_shapes=[
                pltpu.VMEM((nq, 2 * LANES, wide), _BF16),
                pltpu.VMEM((nq, 8, wide), _F32),
                pltpu.VMEM((nq, 8, wide), _F32),
                pltpu.VMEM((nq, LANES, tq), _F32),
                pltpu.VMEM((pipe, tk, wide), _F32),
                pltpu.VMEM((pipe, tk, wide), _BF16),
                pltpu.VMEM((pipe, 8, wide), _F32),
                pltpu.VMEM((pipe, 8, wide), _F32),
                pltpu.VMEM((nq, 8, wide), _F32),
            ],
        ),
        out_shape=jax.ShapeDtypeStruct((b, s, width), _BF16),
        compiler_params=_params("arbitrary", "arbitrary"),
        name=mode,
    )(unit_q, unit_k, qt, k, kx, vt, aux, aux2)


def _pack_bf16_pair(a, b):
    lo = pltpu.bitcast(a.astype(_BF16).astype(_F32), jnp.uint32) >> 16
    hi = pltpu.bitcast(b.astype(_BF16).astype(_F32), jnp.uint32) & jnp.uint32(0xFFFF0000)
    return lo | hi


def _unpack_bf16_pair(u):
    lo = pltpu.bitcast(u << 16, _F32)
    hi = pltpu.bitcast(u & jnp.uint32(0xFFFF0000), _F32)
    return jnp.concatenate([lo, hi], axis=1)


def _postattn_body(yf_ref, ym_ref, gates_ref, x_ref, wfb_ref, wmb_ref, wout_ref, g_ref, wr_ref, br_ref,
                   x1_ref, h2_ref, route_ref, meta_ref, counts_ref, carry_ref, *, tm, d):
    step = pl.program_id(0)
    ya = jnp.dot(yf_ref[...], wfb_ref[...], preferred_element_type=_F32)
    yb = jnp.dot(ym_ref[...], wmb_ref[...], preferred_element_type=_F32)
    mixed = gates_ref[:, 0:d].astype(_F32) * ya + gates_ref[:, d:2 * d].astype(_F32) * yb
    x1 = x_ref[...] + jnp.dot(mixed.astype(_BF16), wout_ref[...], preferred_element_type=_F32)
    x1_ref[...] = x1
    h2 = _rms(x1, g_ref[...])
    h2_ref[...] = _pack_bf16_pair(h2[:, 0:d // 2], h2[:, d // 2:d])

    logits = jnp.dot(h2.astype(_BF16), wr_ref[...], preferred_element_type=_F32) + br_ref[...]
    logits_t = logits.T
    gl = logits_t[N_EXPERTS:N_EXPERTS + N_GROUPS]
    ge = jnp.exp(gl - jnp.max(gl, axis=0, keepdims=True))
    gp = ge / jnp.sum(ge, axis=0, keepdims=True)
    g_top = jnp.max(gp, axis=0, keepdims=True)
    g_row = lax.broadcasted_iota(jnp.int32, (N_GROUPS, tm), 0)
    g_idx = jnp.min(jnp.where(gp == g_top, g_row, N_GROUPS), axis=0, keepdims=True)
    fl = logits_t[0:EXPERTS_PER_GROUP]
    for g in range(1, N_GROUPS):
        fl = jnp.where(g_idx == g, logits_t[g * EXPERTS_PER_GROUP:(g + 1) * EXPERTS_PER_GROUP], fl)
    f_row = lax.broadcasted_iota(jnp.int32, (EXPERTS_PER_GROUP, tm), 0)
    f1 = jnp.max(fl, axis=0, keepdims=True)
    i1 = jnp.min(jnp.where(fl == f1, f_row, EXPERTS_PER_GROUP), axis=0, keepdims=True)
    fl2 = jnp.where(f_row == i1, -jnp.inf, fl)
    f2 = jnp.max(fl2, axis=0, keepdims=True)
    i2 = jnp.min(jnp.where(fl2 == f2, f_row, EXPERTS_PER_GROUP), axis=0, keepdims=True)
    e1 = g_idx * EXPERTS_PER_GROUP + i1
    e2 = g_idx * EXPERTS_PER_GROUP + i2
    t2 = jnp.exp(f2 - f1)
    w1 = g_top * (1.0 / (1.0 + t2))
    w2 = g_top * (t2 / (1.0 + t2))

    @pl.when(step == 0)
    def _():
        carry_ref[...] = jnp.zeros_like(carry_ref)

    x_row = lax.broadcasted_iota(jnp.int32, (N_EXPERTS, tm), 0)
    onehot = ((x_row == e1) | (x_row == e2)).astype(_F32)
    rr = lax.broadcasted_iota(jnp.int32, (tm, tm), 0)
    cc = lax.broadcasted_iota(jnp.int32, (tm, tm), 1)
    before = jnp.dot(onehot.astype(_BF16), (rr < cc).astype(_BF16), preferred_element_type=_F32)
    before = before + carry_ref[0:N_EXPERTS, 0:1]
    r1 = jnp.sum(jnp.where(x_row == e1, before, 0.0), axis=0, keepdims=True)
    r2 = jnp.sum(jnp.where(x_row == e2, before, 0.0), axis=0, keepdims=True)
    total = carry_ref[0:N_EXPERTS, 0:1] + jnp.sum(onehot, axis=1, keepdims=True)
    carry_ref[0:N_EXPERTS, :] = jnp.broadcast_to(total, (N_EXPERTS, LANES))
    counts_ref[...] = carry_ref[...]

    m_row = lax.broadcasted_iota(jnp.int32, (8, tm), 0)
    meta = jnp.where(m_row == 2, e1, 0)
    meta = jnp.where(m_row == 3, e2, meta)
    meta = jnp.where(m_row == 4, r1.astype(jnp.int32), meta)
    meta = jnp.where(m_row == 5, r2.astype(jnp.int32), meta)
    meta_ref[...] = meta
    w_row = lax.broadcasted_iota(jnp.int32, (LANES, tm), 0)
    route_ref[...] = jnp.where(w_row == 0, w1, jnp.where(w_row == 1, w2, 0.0)).T


def _postattn(yf, ym, gates, x, wfb, wmb, wout, g, wr, br, *, tm):
    t, d = x.shape
    width = yf.shape[1]
    row = lambda cols: pl.BlockSpec((tm, cols), lambda i: (i, 0))
    return pl.pallas_call(
        functools.partial(_postattn_body, tm=tm, d=d),
        grid=(t // tm,),
        in_specs=[row(width), row(width), row(2 * d), row(d),
                  _const_spec(wfb.shape), _const_spec(wmb.shape), _const_spec(wout.shape),
                  _const_spec((1, d)), _const_spec(wr.shape), _const_spec(br.shape)],
        out_specs=[row(d), row(d // 2), row(LANES), pl.BlockSpec((8, tm), lambda i: (0, i)),
                   _const_spec((LANES, LANES))],
        out_shape=[jax.ShapeDtypeStruct((t, d), _F32),
                   jax.ShapeDtypeStruct((t, d // 2), jnp.uint32),
                   jax.ShapeDtypeStruct((t, LANES), _F32),
                   jax.ShapeDtypeStruct((8, t), jnp.int32),
                   jax.ShapeDtypeStruct((LANES, LANES), _F32)],
        scratch_shapes=[pltpu.VMEM((LANES, LANES), _F32)],
        compiler_params=_params("arbitrary"),
        name="postattn",
    )(yf, ym, gates, x, wfb, wmb, wout, g, wr, br)


def _sc_scatter(rows, idx, n_out):
    n, width = rows.shape
    workers = SC_CORES * SC_SUBCORES
    per_worker = n // workers
    chunks = per_worker // SC_SCATTER_CHUNK
    assert n % (workers * SC_SCATTER_CHUNK) == 0 and idx.shape == (2, n // SC_SCATTER_CHUNK, SC_SCATTER_CHUNK)
    mesh = plsc.VectorSubcoreMesh(core_axis_name="c", subcore_axis_name="s",
                                  num_cores=SC_CORES, num_subcores=SC_SUBCORES)

    @functools.partial(
        pl.kernel, out_type=jax.ShapeDtypeStruct((n_out, width), rows.dtype), mesh=mesh,
        scratch_types=[pltpu.VMEM((2, chunks, SC_SCATTER_CHUNK), jnp.int32),
                       pltpu.VMEM((SC_SCATTER_CHUNK, width), rows.dtype),
                       pltpu.SemaphoreType.DMA],
        name="sc_scatter")
    def scatter(rows_hbm, idx_hbm, out_hbm, idx_v, rows_v, sem):
        worker = lax.axis_index("s") * SC_CORES + lax.axis_index("c")
        for k in range(2):
            pltpu.sync_copy(idx_hbm.at[k, pl.ds(worker * chunks, chunks)], idx_v.at[k])

        def chunk(j, carry):
            start = worker * per_worker + j * SC_SCATTER_CHUNK
            pltpu.sync_copy(rows_hbm.at[pl.ds(start, SC_SCATTER_CHUNK)], rows_v)
            for k in range(2):
                pltpu.async_copy(rows_v, out_hbm.at[idx_v.at[k, j]], sem).wait()
            return carry

        lax.fori_loop(0, chunks, chunk, 0)

    return scatter(rows, idx)


def _experts_body(be_ref, valid_ref, nused_ref, xs_ref, wg_ref, wu_ref, wd_ref, out_ref,
                  wgb_ref, wub_ref, wdb_ref):
    j = pl.program_id(0)
    used = j < nused_ref[0]

    @pl.when(jnp.logical_not(used))
    def _():
        out_ref[...] = jnp.zeros_like(out_ref)

    @pl.when(used & ((j == 0) | (be_ref[j] != be_ref[jnp.maximum(j - 1, 0)])))
    def _():
        wgb_ref[...] = wg_ref[...].astype(_BF16)
        wub_ref[...] = wu_ref[...].astype(_BF16)
        wdb_ref[...] = wd_ref[...].astype(_BF16)

    @pl.when(used)
    def _():
        row = lax.broadcasted_iota(jnp.int32, xs_ref.shape, 0)
        packed = jnp.where(row < valid_ref[j], xs_ref[...], jnp.uint32(0))
        rows = _unpack_bf16_pair(packed).astype(_BF16)
        hidden = wgb_ref.shape[1]
        out = None
        for c in range(0, hidden, hidden // EXPERT_PARTS):
            cols = slice(c, c + hidden // EXPERT_PARTS)
            gate = jnp.dot(rows, wgb_ref[:, cols], preferred_element_type=_F32)
            up = jnp.dot(rows, wub_ref[:, cols], preferred_element_type=_F32)
            act = ((gate * jax.nn.sigmoid(gate)) * up).astype(_BF16)
            part = jnp.dot(act, wdb_ref[cols, :], preferred_element_type=_F32)
            out = part if out is None else out + part
        half = out.shape[1] // 2
        out_ref[...] = _pack_bf16_pair(out[:, 0:half], out[:, half:])


def _experts(block_expert, block_valid, n_used, xs, wg, wu, wd, *, te):
    n_rows, half = xs.shape
    _, d, de = wg.shape

    def blk(j, be, bv, nu):
        return jnp.minimum(j, nu[0] - 1)

    def weight(j, be, bv, nu):
        return (be[blk(j, be, bv, nu)], 0, 0)

    return pl.pallas_call(
        _experts_body,
        grid_spec=pltpu.PrefetchScalarGridSpec(
            num_scalar_prefetch=3,
            grid=(n_rows // te,),
            in_specs=[pl.BlockSpec((te, half), lambda j, be, bv, nu: (blk(j, be, bv, nu), 0)),
                      pl.BlockSpec((None, d, de), weight),
                      pl.BlockSpec((None, d, de), weight),
                      pl.BlockSpec((None, de, d), weight)],
            out_specs=pl.BlockSpec((te, d // 2), lambda j, be, bv, nu: (j, 0)),
            scratch_shapes=[pltpu.VMEM((d, de), _BF16), pltpu.VMEM((d, de), _BF16),
                            pltpu.VMEM((de, d), _BF16)],
        ),
        out_shape=jax.ShapeDtypeStruct((n_rows, d // 2), jnp.uint32),
        compiler_params=_params("arbitrary"),
        name="experts",
    )(block_expert, block_valid, n_used, xs, wg, wu, wd)


def _sc_gather(table, idx):
    n, = idx.shape
    width = table.shape[1]
    workers = SC_CORES * SC_SUBCORES
    per_worker = n // workers
    chunks = per_worker // SC_CHUNK
    assert n % (workers * SC_CHUNK * 2) == 0
    mesh = plsc.VectorSubcoreMesh(core_axis_name="c", subcore_axis_name="s",
                                  num_cores=SC_CORES, num_subcores=SC_SUBCORES)

    @functools.partial(
        pl.kernel, out_type=jax.ShapeDtypeStruct((n, width), table.dtype), mesh=mesh,
        scratch_types=[pltpu.VMEM((per_worker,), jnp.int32),
                       pltpu.VMEM((2, SC_CHUNK, width), table.dtype),
                       pltpu.SemaphoreType.DMA((2,))],
        name="sc_gather")
    def gather(table_hbm, idx_hbm, out_hbm, idx_v, rows_v, sems):
        worker = lax.axis_index("s") * SC_CORES + lax.axis_index("c")
        base = worker * per_worker
        pltpu.sync_copy(idx_hbm.at[pl.ds(base, per_worker)], idx_v)

        def fetch(j, slot):
            off = pl.multiple_of(j * SC_CHUNK, SC_CHUNK)
            return pltpu.async_copy(table_hbm.at[idx_v.at[pl.ds(off, SC_CHUNK)]], rows_v.at[slot],
                                    sems.at[slot])

        def drain(j, slot):
            off = pl.multiple_of(j * SC_CHUNK, SC_CHUNK)
            pltpu.make_async_copy(out_hbm.at[pl.ds(base + off, SC_CHUNK)], rows_v.at[slot],
                                  sems.at[slot]).wait()
            pltpu.sync_copy(rows_v.at[slot], out_hbm.at[pl.ds(base + off, SC_CHUNK)])

        fetch(0, 0)

        def pair(p, carry):
            fetch(2 * p + 1, 1)
            drain(2 * p, 0)

            @pl.when(p + 1 < chunks // 2)
            def _():
                fetch(2 * p + 2, 0)

            drain(2 * p + 1, 1)
            return carry

        lax.fori_loop(0, chunks // 2, pair, 0)

    return gather(table, idx)


def _final_body(x1_ref, route_ref, p_ref, r1_ref, r2_ref, gp_ref, wpg_ref, wpp_ref, gf_ref, o_ref):
    emb = jnp.dot(p_ref[...].astype(_BF16), wpp_ref[...], preferred_element_type=_F32)
    route = route_ref[...]
    x2 = x1_ref[...] + (_unpack_bf16_pair(r1_ref[...]) * route[:, 0:1]
                        + _unpack_bf16_pair(r2_ref[...]) * route[:, 1:2])
    gate = jax.nn.sigmoid(jnp.dot(_rms(x2, gp_ref[...]).astype(_BF16), wpg_ref[...],
                                  preferred_element_type=_F32))
    o_ref[...] = _rms(x2 + gate * emb, gf_ref[...])


def _final(x1, route, p, slot_rows, gp, wpg, wpp, gf, *, tm):
    t, d = x1.shape
    ple = p.shape[1]
    steps = t // tm
    row = lambda cols: pl.BlockSpec((tm, cols), lambda i: (i, 0))
    return pl.pallas_call(
        _final_body,
        grid=(steps,),
        in_specs=[row(d), row(LANES), row(ple), row(d // 2),
                  pl.BlockSpec((tm, d // 2), lambda i: (steps + i, 0)),
                  _const_spec((1, d)), _const_spec(wpg.shape), _const_spec(wpp.shape),
                  _const_spec((1, d))],
        out_specs=row(d),
        out_shape=jax.ShapeDtypeStruct((t, d), _F32),
        compiler_params=_params("arbitrary"),
        name="final",
    )(x1, route, p, slot_rows, slot_rows, gp, wpg, wpp, gf)


def _rope_tables(s):
    half = HEAD_DIM // 2
    inv = 1.0 / (ROPE_THETA ** (jnp.arange(0, HEAD_DIM, 2, dtype=_F32) / HEAD_DIM))
    ang = jnp.arange(s, dtype=_F32)[:, None] * inv[None, :]
    cos, sin = jnp.cos(ang), jnp.sin(ang)
    reps = LANES // half
    cos_t = jnp.tile(cos, (1, reps))
    sin_t = jnp.tile(jnp.concatenate([-sin, sin], axis=1), (1, reps // 2))
    return cos_t, sin_t


def _layer(x, p, attn_norm, w_in, b_forget, w_fox_branch, w_moba_branch, w_out, moe_norm,
           w_group, b_group, w_fine, b_fine, w_gate, w_up, w_down, ple_norm, w_ple_gate, w_ple_proj,
           final_gain):
    b, s, d = x.shape
    t = b * s
    width = N_FOX_HEADS * HEAD_DIM
    assert N_MOBA_HEADS * HEAD_DIM == width and MOBA_BLOCK % ATTN_Q == 0 and MOBA_PIPELINE[0] % MOBA_BLOCK == 0
    assert BIAS_PIECES * N_FOX_HEADS <= LANES and s % MOBA_PIPELINE[0] == 0
    assert s % ROW_TILE == 0 and ROW_TILE % MOBA_BLOCK == 0 and s // MOBA_BLOCK <= LANES
    assert t % FINAL_TILE == 0 and N_EXPERTS + N_GROUPS <= LANES

    qkv_cols = 6 * width
    f_cols = N_FOX_HEADS
    chunk = lambda c: w_in[:, c * width:(c + 1) * width]
    w_rows = jnp.concatenate([chunk(1), chunk(4), w_in[:, qkv_cols + f_cols:]], axis=1).astype(_BF16)
    w_cols = jnp.concatenate([chunk(0), chunk(2), chunk(3), chunk(5)], axis=1).T.astype(_BF16)
    wf = jnp.zeros((d, LANES), _BF16).at[:, :f_cols].set(w_in[:, qkv_cols:qkv_cols + f_cols].astype(_BF16))
    bfp = jnp.zeros((1, LANES), _F32).at[0, :f_cols].set(b_forget.astype(_F32))
    cos, sin = _rope_tables(s)

    qft, kf, vft, qmt, km, vmt, gates, cb, kmean, kabs, bmax = _inproj(
        x, attn_norm.reshape(1, d), w_rows, w_cols, wf, bfp, cos, sin, cos.T, sin.T, tm=ROW_TILE)
    y_fox = _attention(qft, kf, cb, vft, kabs, bmax, mode="fox")
    block_of_key = jnp.arange(s, dtype=jnp.int32)[:, None] // MOBA_BLOCK
    block_onehot = (block_of_key == jnp.arange(LANES, dtype=jnp.int32)[None, :]).astype(_BF16)
    y_moba = _attention(qmt, km, block_onehot, vmt, kmean, bmax, mode="moba")

    wr = jnp.zeros((d, LANES), _BF16)
    wr = wr.at[:, :N_EXPERTS].set(w_fine.astype(_BF16))
    wr = wr.at[:, N_EXPERTS:N_EXPERTS + N_GROUPS].set(w_group.astype(_BF16))
    br = jnp.zeros((1, LANES), _F32)
    br = br.at[0, :N_EXPERTS].set(b_fine.astype(_F32))
    br = br.at[0, N_EXPERTS:N_EXPERTS + N_GROUPS].set(b_group.astype(_F32))
    x1, h2p, route, meta, counts = _postattn(
        y_fox.reshape(t, width), y_moba.reshape(t, width), gates.reshape(t, 2 * d), x.reshape(t, d),
        w_fox_branch.astype(_BF16), w_moba_branch.astype(_BF16), w_out.astype(_BF16),
        moe_norm.reshape(1, d), wr, br, tm=ROW_TILE)

    te = EXPERT_TILE
    n_blk = (2 * t) // te + N_EXPERTS
    counts = counts[:N_EXPERTS, 0].astype(jnp.int32)
    blocks_per = (counts + te - 1) // te
    block_end = jnp.cumsum(blocks_per)
    row_start = (block_end - blocks_per) * te
    expert_ids = jnp.arange(N_EXPERTS, dtype=jnp.int32)[:, None, None]
    slot_major = meta[4:6] + jnp.sum(
        jnp.where(meta[2:4][None] == expert_ids, row_start[:, None, None], 0), axis=0)
    n_used = block_end[-1:].astype(jnp.int32)
    block_ids = jnp.arange(n_blk, dtype=jnp.int32)
    block_expert = jnp.minimum(
        jnp.sum((block_ids[:, None] >= block_end[None, :]).astype(jnp.int32), axis=1), N_EXPERTS - 1)

    first_block = block_end - blocks_per
    block_valid = jnp.clip(counts[block_expert] - (block_ids - first_block[block_expert]) * te, 0, te)
    xs = _sc_scatter(h2p, slot_major.reshape(2, t // SC_SCATTER_CHUNK, SC_SCATTER_CHUNK), n_blk * te)
    rows = _experts(block_expert, block_valid, n_used, xs, w_gate, w_up, w_down, te=te)
    slot_rows = _sc_gather(rows, slot_major.reshape(-1))
    out = _final(x1, route, p.reshape(t, -1), slot_rows, ple_norm.reshape(1, d),
                 w_ple_gate.astype(_BF16), w_ple_proj.astype(_BF16), final_gain.reshape(1, d),
                 tm=FINAL_TILE)
    return out.reshape(b, s, d)


def kernel(x, p, attn_norm, w_in, b_forget, w_fox_branch, w_moba_branch, w_out, moe_norm, w_group,
           b_group, w_fine, b_fine, w_gate, w_up, w_down, ple_norm, w_ple_gate, w_ple_proj, final_norm):
    depth = p.shape[0]
    assert depth == 1, "the final norm is fused into the last layer's kernel"
    i = 0
    return _layer(x, p[i], attn_norm[i], w_in[i], b_forget[i], w_fox_branch[i], w_moba_branch[i],
                  w_out[i], moe_norm[i], w_group[i], b_group[i], w_fine[i], b_fine[i], w_gate[i],
                  w_up[i], w_down[i], ple_norm[i], w_ple_gate[i], w_ple_proj[i], final_norm)
```

```python
import functools

import jax
import jax.numpy as jnp
from jax import lax
from jax.experimental import pallas as pl
from jax.experimental.pallas import tpu as pltpu
from jax.experimental.pallas import tpu_sc as plsc

HEAD_DIM = 64
N_FOX_HEADS = 8
N_MOBA_HEADS = 8
MOBA_BLOCK = 256
MOBA_TOPK = 3
ROPE_THETA = 10000.0
N_GROUPS = 4
EXPERTS_PER_GROUP = 8
N_EXPERTS = N_GROUPS * EXPERTS_PER_GROUP
RMS_EPS = 1e-6

LANES = 128
HEADS_PER_STEP = LANES // HEAD_DIM
ATTN_Q = 256
FOX_PIPELINE = (256, 8, 4)
MOBA_PIPELINE = (256, 8, 4)
ONES_ROWS = 16
SOFTMAX_ROWS = 64
BIAS_PIECES = 3
ROW_TILE = 512
SC_CORES = 2
SC_SUBCORES = 16
SC_CHUNK = 64
SC_SCATTER_CHUNK = 128
FINAL_TILE = 1024
EXPERT_PARTS = 2
EXPERT_TILE = 512
LOG2_E = 1.4426950408889634
UNDERFLOW_LOG2 = 160.0
NEG_BIG = -1e30
VMEM_LIMIT = 48 * 1024 * 1024

_BF16 = jnp.bfloat16
_F32 = jnp.float32


def _params(*sem):
    return pltpu.CompilerParams(dimension_semantics=sem, vmem_limit_bytes=VMEM_LIMIT)


def _rms(x, g):
    return x * lax.rsqrt(jnp.mean(x * x, axis=-1, keepdims=True) + RMS_EPS) * g


def _const_spec(shape):
    return pl.BlockSpec(shape, lambda *_: (0,) * len(shape))


def _inproj_body(x_ref, g_ref, w_ref, wt_ref, wf_ref, bf_ref, cos_ref, sin_ref, cost_ref, sint_ref,
                 qft_ref, kf_ref, vft_ref, qmt_ref, km_ref, vmt_ref, gates_ref, cb_ref, kmean_ref,
                 kabs_ref, bmax_ref,
                 carry_ref, *, tm, width, gate_cols):
    j = pl.program_id(1)
    h = _rms(x_ref[...], g_ref[...])
    hb = h.astype(_BF16)
    hbt = h.T.astype(_BF16)
    reps = width // LANES
    half = HEAD_DIM // 2

    blocks = tm // MOBA_BLOCK

    @pl.when(j == 0)
    def _():
        kmean_ref[...] = jnp.zeros_like(kmean_ref)
        kabs_ref[...] = jnp.zeros_like(kabs_ref)
        bmax_ref[...] = jnp.zeros_like(bmax_ref)
        carry_ref[...] = jnp.zeros_like(carry_ref)

    z = jnp.dot(hb, wf_ref[...], preferred_element_type=_F32) + bf_ref[...]
    log_f = jnp.minimum(z, 0.0) - jnp.log1p(jnp.exp(-jnp.abs(z)))
    r = lax.broadcasted_iota(jnp.int32, (LANES, LANES), 0)
    c = lax.broadcasted_iota(jnp.int32, (LANES, LANES), 1)
    tri = (c <= r).astype(_BF16)
    offset = carry_ref[0:1, :]
    blocks_cum = []
    for blk in range(tm // LANES):
        rest = log_f[blk * LANES:(blk + 1) * LANES]
        within = jnp.zeros((LANES, LANES), _F32)
        for _ in range(BIAS_PIECES):
            part = rest.astype(_BF16)
            rest = rest - part.astype(_F32)
            within = within + jnp.dot(tri, part, preferred_element_type=_F32)
        blocks_cum.append(within + offset)
        offset = offset + within[LANES - 1:LANES, :]
    cum = jnp.concatenate(blocks_cum, axis=0)
    carry_ref[...] = jnp.broadcast_to(offset, carry_ref.shape)

    head_lane = lax.broadcasted_iota(jnp.int32, (1, LANES), 1) < N_FOX_HEADS
    bias = jnp.where(head_lane, -cum * LOG2_E, 0.0)
    for b in range(blocks):
        bmax_ref[pl.ds(j * blocks + b, 1), :] = jnp.max(
            bias[b * MOBA_BLOCK:(b + 1) * MOBA_BLOCK], axis=0, keepdims=True)
    rest = bias
    placed = jnp.zeros((tm, LANES), _F32)
    for piece in range(BIAS_PIECES):
        part = rest.astype(_BF16).astype(_F32)
        rest = rest - part
        placed = placed + (pltpu.roll(part, N_FOX_HEADS * piece, 1) if piece else part)
    cb_ref[...] = placed.astype(_BF16)

    def proj(c):
        return jnp.dot(hb, w_ref[:, c * width:(c + 1) * width], preferred_element_type=_F32)

    def proj_t(c):
        return jnp.dot(wt_ref[c * width:(c + 1) * width, :], hbt, preferred_element_type=_F32)

    def rope(t):
        cos = jnp.concatenate([cos_ref[...]] * reps, axis=1)
        sin = jnp.concatenate([sin_ref[...]] * reps, axis=1)
        first = (lax.broadcasted_iota(jnp.int32, (1, width), 1) % HEAD_DIM) < half
        partner = jnp.where(first, pltpu.roll(t, width - half, 1), pltpu.roll(t, half, 1))
        return t * cos + partner * sin

    def rope_t(t):
        cos = jnp.concatenate([cost_ref[...]] * reps, axis=0)
        sin = jnp.concatenate([sint_ref[...]] * reps, axis=0)
        first = (lax.broadcasted_iota(jnp.int32, (width, 1), 0) % HEAD_DIM) < half
        partner = jnp.where(first, pltpu.roll(t, width - half, 0), pltpu.roll(t, half, 0))
        return t * cos + partner * sin

    scale = HEAD_DIM ** -0.5 * LOG2_E
    kf = proj(0).astype(_BF16)
    kf_ref[...] = kf
    kf_abs = jnp.abs(kf.astype(_F32))
    km = rope(proj(1))
    km_ref[...] = km.astype(_BF16)

    for b in range(blocks):
        rows = slice(b * MOBA_BLOCK, (b + 1) * MOBA_BLOCK)
        kmean_ref[pl.ds(j * blocks + b, 1), :] = jnp.mean(km[rows], axis=0, keepdims=True)
        kabs_ref[pl.ds(j * blocks + b, 1), :] = jnp.max(kf_abs[rows], axis=0, keepdims=True)

    for c in range(gate_cols // width):
        g = jnp.dot(hb, w_ref[:, (2 + c) * width:(3 + c) * width],
                    preferred_element_type=_F32)
        gates_ref[:, c * width:(c + 1) * width] = jax.nn.sigmoid(g).astype(_BF16)

    qft_ref[...] = (proj_t(0) * scale).astype(_BF16)
    vft_ref[...] = proj_t(1).astype(_BF16)
    qmt_ref[...] = (rope_t(proj_t(2)) * scale).astype(_BF16)
    vmt_ref[...] = proj_t(3).astype(_BF16)


def _inproj(x, g, w, wt, wf, bfp, cos, sin, cos_t, sin_t, *, tm):
    b, s, d = x.shape
    width = N_FOX_HEADS * HEAD_DIM
    gate_cols = w.shape[1] - 2 * width
    act = jax.ShapeDtypeStruct((b, s, width), _BF16)
    act_t = jax.ShapeDtypeStruct((b, width, s), _BF16)
    act_spec = pl.BlockSpec((None, tm, width), lambda bi, j: (bi, j, 0))
    act_t_spec = pl.BlockSpec((None, width, tm), lambda bi, j: (bi, 0, j))
    return pl.pallas_call(
        functools.partial(_inproj_body, tm=tm, width=width, gate_cols=gate_cols),
        grid=(b, s // tm),
        in_specs=[
            pl.BlockSpec((None, tm, d), lambda bi, j: (bi, j, 0)),
            _const_spec((1, d)),
            _const_spec(w.shape),
            _const_spec(wt.shape),
            _const_spec(wf.shape),
            _const_spec(bfp.shape),
            pl.BlockSpec((tm, LANES), lambda bi, j: (j, 0)),
            pl.BlockSpec((tm, LANES), lambda bi, j: (j, 0)),
            pl.BlockSpec((LANES, tm), lambda bi, j: (0, j)),
            pl.BlockSpec((LANES, tm), lambda bi, j: (0, j)),
        ],
        out_specs=[act_t_spec, act_spec, act_t_spec, act_t_spec, act_spec, act_t_spec,
                   pl.BlockSpec((None, tm, gate_cols), lambda bi, j: (bi, j, 0)),
                   pl.BlockSpec((None, tm, LANES), lambda bi, j: (bi, j, 0)),
                   pl.BlockSpec((None, LANES, width), lambda bi, j: (bi, 0, 0)),
                   pl.BlockSpec((None, LANES, width), lambda bi, j: (bi, 0, 0)),
                   pl.BlockSpec((None, LANES, LANES), lambda bi, j: (bi, 0, 0))],
        out_shape=[act_t, act, act_t, act_t, act, act_t,
                   jax.ShapeDtypeStruct((b, s, gate_cols), _BF16),
                   jax.ShapeDtypeStruct((b, s, LANES), _BF16),
                   jax.ShapeDtypeStruct((b, LANES, width), _F32),
                   jax.ShapeDtypeStruct((b, LANES, width), _F32),
                   jax.ShapeDtypeStruct((b, LANES, LANES), _F32)],
        scratch_shapes=[pltpu.VMEM((8, LANES), _F32)],
        compiler_params=_params("arbitrary", "arbitrary"),
        name="inproj",
    )(x, g, w, wt, wf, bfp, cos, sin, cos_t, sin_t)


def _attn_body(uq_ref, uk_ref, qt_ref, k_ref, kx_ref, vt_ref, aux_ref, aux2_ref, o_ref,
               w_ref, m_ref, l_ref, acc_ref, st_ref, p_ref, alpha_ref, pmax_ref, qn_ref, *,
               mode, tq, tk, nb, n_diag, n_units, pipe, lag):
    hp = pl.program_id(1)
    wide = HEADS_PER_STEP * tq
    nq = w_ref.shape[0]
    feat = lax.broadcasted_iota(jnp.int32, (LANES, 1), 0)
    xr = lax.broadcasted_iota(jnp.int32, (LANES, wide), 0)
    xc = lax.broadcasted_iota(jnp.int32, (LANES, wide), 1)

    def setup(i, carry):
        qt = qt_ref[:, pl.ds(pl.multiple_of(i * tq, tq), tq)]
        zero = jnp.zeros_like(qt)
        wq = jnp.concatenate([jnp.where(feat < HEAD_DIM, qt, zero),
                              jnp.where(feat < HEAD_DIM, zero, qt)], axis=1)
        if mode == "fox":
            head = hp * HEADS_PER_STEP + xc // tq
            lower = ((xr % N_FOX_HEADS == head) & (xr < BIAS_PIECES * N_FOX_HEADS)).astype(_BF16)
            qn_ref[i] = jnp.broadcast_to(
                jnp.sum(jnp.abs(wq.astype(_F32)), axis=0, keepdims=True), qn_ref.shape[1:])
        else:
            own = (i * tq) // MOBA_BLOCK
            gate = jnp.dot(aux_ref[...].astype(_BF16), wq, preferred_element_type=_F32)[0:nb]
            blk = lax.broadcasted_iota(jnp.int32, (nb, wide), 0)
            gate = jnp.where(blk < own, gate, -jnp.inf)
            keep = blk == own
            for _ in range(MOBA_TOPK):
                mx = jnp.max(gate, axis=0, keepdims=True)
                cand = jnp.where((gate == mx) & (mx > -jnp.inf), blk, nb)
                pick = blk == jnp.min(cand, axis=0, keepdims=True)
                keep = keep | pick
                gate = jnp.where(pick, -jnp.inf, gate)
            lower = jnp.where(keep, 0.0, NEG_BIG)
            if nb < LANES:
                lower = jnp.concatenate([lower, jnp.zeros((LANES - nb, wide), _F32)], axis=0)
            lower = lower.astype(_BF16)
        w_ref[i] = jnp.concatenate([wq, lower], axis=0)
        m_ref[i] = jnp.full(m_ref.shape[1:], NEG_BIG, _F32)
        l_ref[i] = jnp.zeros(l_ref.shape[1:], _F32)
        acc_ref[i] = jnp.zeros(acc_ref.shape[1:], _F32)
        return carry

    lax.fori_loop(0, nq, setup, 0, unroll=8)

    def unit(t):
        return uq_ref[t], uk_ref[t]

    def scores(t, slot, causal):
        q, kb = unit(t)
        start = pl.multiple_of(kb * tk, tk)
        ka = jnp.concatenate([k_ref[pl.ds(start, tk), :], kx_ref[pl.ds(start, tk), :]], axis=1)
        st = jnp.dot(ka, w_ref[q], preferred_element_type=_F32)
        if causal:
            key = start + lax.broadcasted_iota(jnp.int32, (tk, wide), 0)
            qry = q * tq + lax.broadcasted_iota(jnp.int32, (tk, wide), 1) % tq
            st = jnp.where(key <= qry, st, NEG_BIG)
        st_ref[slot] = st
        pmax_ref[slot] = jnp.max(st.reshape(tk // 8, 8, wide), axis=0)

    def softmax(t, slot):
        q, _ = unit(t)
        rows = SOFTMAX_ROWS
        m_prev = m_ref[q][0:1, :]
        m_new = jnp.maximum(m_prev, jnp.max(pmax_ref[slot], axis=0, keepdims=True))
        for r in range(0, tk, rows):
            p_ref[slot, r:r + rows, :] = jnp.exp2(st_ref[slot, r:r + rows, :] - m_new).astype(_BF16)
        m_ref[q] = jnp.broadcast_to(m_new, m_ref.shape[1:])
        alpha_ref[slot] = jnp.broadcast_to(jnp.exp2(m_prev - m_new), alpha_ref.shape[1:])

    def values(t, slot):
        q, kb = unit(t)
        start = pl.multiple_of(kb * tk, tk)
        ones = jnp.ones((ONES_ROWS, tk), _BF16)
        for h in range(HEADS_PER_STEP):
            cols = slice(h * tq, (h + 1) * tq)
            feats = slice(h * HEAD_DIM, (h + 1) * HEAD_DIM)
            lhs = jnp.concatenate([vt_ref[feats, pl.ds(start, tk)], ones], axis=0)
            pv = jnp.dot(lhs, p_ref[slot, :, cols], preferred_element_type=_F32)
            alpha = alpha_ref[slot, 0:1, cols]
            acc_ref[q, feats, :] = acc_ref[q, feats, :] * alpha + pv[0:HEAD_DIM]
            l_ref[q, :, cols] = jnp.broadcast_to(
                alpha * l_ref[q, 0:1, cols] + pv[HEAD_DIM:HEAD_DIM + 1], (8, tq))

    def step(base, c, causal, do_scores=True, do_softmax=True, do_values=True):
        u = base + c
        if do_scores:
            scores(u, c % pipe, causal)
        if do_values:
            values(u - 2 * lag, (c - 2 * lag) % pipe)
        if do_softmax:
            softmax(u - lag, (c - lag) % pipe)

    def group(causal):
        def body(j, carry):
            for c in range(pipe):
                step(j * pipe, c, causal)
            return carry
        return body

    def pipeline(g_first, g_end, causal_groups):
        for c in range(pipe):
            step(g_first * pipe, c, g_first < causal_groups,
                 do_softmax=c >= lag, do_values=c >= 2 * lag)
        if g_first + 1 < causal_groups:
            lax.fori_loop(g_first + 1, causal_groups, group(True), 0)
        lax.fori_loop(max(g_first + 1, causal_groups), g_end, group(False), 0)
        for c in range(2 * lag):
            step(g_end * pipe, c, False, do_scores=False, do_softmax=c < lag)

    diag_groups = n_diag // pipe
    if mode == "moba":
        pipeline(0, n_units // pipe, diag_groups)
    else:
        pipeline(0, diag_groups, diag_groups)
        nq_blocks = n_diag
        col = lax.broadcasted_iota(jnp.int32, (LANES, LANES), 1)
        first_head = lax.broadcasted_iota(jnp.int32, (1, wide), 1) < tq
        head0 = hp * HEADS_PER_STEP

        def per_head(table, pick0, pick1, fill):
            v0 = jnp.max(jnp.where(pick0, table, fill), axis=1, keepdims=True)[0:nb]
            v1 = jnp.max(jnp.where(pick1, table, fill), axis=1, keepdims=True)[0:nb]
            return jnp.where(first_head, v0, v1)

        bias_max = per_head(aux2_ref[...], col == head0, col == head0 + 1, NEG_BIG)
        k_absmax = per_head(aux_ref[...], col < HEAD_DIM, col >= HEAD_DIM, 0.0)
        blk = lax.broadcasted_iota(jnp.int32, (nb, 1), 0)

        needed = jnp.zeros((nb, 1), jnp.int32)
        for i in range(1, nq_blocks):
            bound = bias_max + qn_ref[i, 0:1, :] * k_absmax - m_ref[i, 0:1, :]
            live = jnp.max(bound, axis=1, keepdims=True) >= -UNDERFLOW_LOG2
            needed = jnp.maximum(needed, jnp.where(live & (blk < i), i - blk, 0))
        far = jnp.max(needed)
        n_live = far * nq_blocks - (far * (far + 1)) // 2
        live_groups = (n_live + pipe - 1) // pipe

        @pl.when(live_groups > 0)
        def _():
            pipeline(diag_groups, diag_groups + live_groups, diag_groups)

    def finish(i, carry):
        out_t = jnp.concatenate(
            [acc_ref[i, h * HEAD_DIM:(h + 1) * HEAD_DIM, :] / l_ref[i, 0:1, h * tq:(h + 1) * tq]
             for h in range(HEADS_PER_STEP)], axis=0)
        o_ref[pl.ds(pl.multiple_of(i * tq, tq), tq), :] = out_t.T.astype(o_ref.dtype)
        return carry

    lax.fori_loop(0, nq, finish, 0, unroll=8)


def _attention(qt, k, kx, vt, aux, aux2, *, mode):
    b, width, s = qt.shape
    tq = ATTN_Q
    tk, pipe, lag = FOX_PIPELINE if mode == "fox" else MOBA_PIPELINE
    nq = s // tq
    nb = s // MOBA_BLOCK
    wide = HEADS_PER_STEP * tq
    diag = [(i * tq) // tk for i in range(nq)]
    units = list(enumerate(diag))
    units += [(i, diag[i] - dist) for dist in range(1, nq) for i in range(nq) if diag[i] >= dist]
    assert mode != "fox" or tq == tk
    assert nq % pipe == 0 and len(units) % pipe == 0 and pipe >= 2 * lag
    unit_q = jnp.asarray([u[0] for u in units], jnp.int32)
    unit_k = jnp.asarray([u[1] for u in units], jnp.int32)
    aux_spec = pl.BlockSpec((None, LANES, LANES), lambda bi, hp, uq, uk: (bi, 0, hp))
    aux2_spec = pl.BlockSpec((None, LANES, LANES), lambda bi, hp, uq, uk: (bi, 0, 0))
    if mode == "fox":
        kx_spec = pl.BlockSpec((None, s, LANES), lambda bi, hp, uq, uk: (bi, 0, 0))
    else:
        kx_spec = pl.BlockSpec(kx.shape, lambda bi, hp, uq, uk: (0, 0))
    return pl.pallas_call(
        functools.partial(_attn_body, mode=mode, tq=tq, tk=tk, nb=nb, n_diag=nq, n_units=len(units),
                          pipe=pipe, lag=lag),
        grid_spec=pltpu.PrefetchScalarGridSpec(
            num_scalar_prefetch=2,
            grid=(b, width // LANES),
            in_specs=[
                pl.BlockSpec((None, LANES, s), lambda bi, hp, uq, uk: (bi, hp, 0)),
                pl.BlockSpec((None, s, LANES), lambda bi, hp, uq, uk: (bi, 0, hp)),
                kx_spec,
                pl.BlockSpec((None, LANES, s), lambda bi, hp, uq, uk: (bi, hp, 0)),
                aux_spec,
                aux2_spec,
            ],
            out_specs=pl.BlockSpec((None, s, LANES), lambda bi, hp, uq, uk: (bi, 0, hp)),
            scratch_shapes=[
                pltpu.VMEM((nq, 2 * LANES, wide), _BF16),
                pltpu.VMEM((nq, 8, wide), _F32),
                pltpu.VMEM((nq, 8, wide), _F32),
                pltpu.VMEM((nq, LANES, tq), _F32),
                pltpu.VMEM((pipe, tk, wide), _F32),
                pltpu.VMEM((pipe, tk, wide), _BF16),
                pltpu.VMEM((pipe, 8, wide), _F32),
                pltpu.VMEM((pipe, 8, wide), _F32),
                pltpu.VMEM((nq, 8, wide), _F32),
            ],
        ),
        out_shape=jax.ShapeDtypeStruct((b, s, width), _BF16),
        compiler_params=_params("arbitrary", "arbitrary"),
        name=mode,
    )(unit_q, unit_k, qt, k, kx, vt, aux, aux2)


def _pack_bf16_pair(a, b):
    lo = pltpu.bitcast(a.astype(_BF16).astype(_F32), jnp.uint32) >> 16
    hi = pltpu.bitcast(b.astype(_BF16).astype(_F32), jnp.uint32) & jnp.uint32(0xFFFF0000)
    return lo | hi


def _unpack_bf16_pair(u):
    lo = pltpu.bitcast(u << 16, _F32)
    hi = pltpu.bitcast(u & jnp.uint32(0xFFFF0000), _F32)
    return jnp.concatenate([lo, hi], axis=1)


def _postattn_body(yf_ref, ym_ref, gates_ref, x_ref, wfb_ref, wmb_ref, wout_ref, g_ref, wr_ref, br_ref,
                   x1_ref, h2_ref, route_ref, meta_ref, counts_ref, carry_ref, *, tm, d):
    step = pl.program_id(0)
    ya = jnp.dot(yf_ref[...], wfb_ref[...], preferred_element_type=_F32)
    yb = jnp.dot(ym_ref[...], wmb_ref[...], preferred_element_type=_F32)
    mixed = gates_ref[:, 0:d].astype(_F32) * ya + gates_ref[:, d:2 * d].astype(_F32) * yb
    x1 = x_ref[...] + jnp.dot(mixed.astype(_BF16), wout_ref[...], preferred_element_type=_F32)
    x1_ref[...] = x1
    h2 = _rms(x1, g_ref[...])
    h2_ref[...] = _pack_bf16_pair(h2[:, 0:d // 2], h2[:, d // 2:d])

    logits = jnp.dot(h2.astype(_BF16), wr_ref[...], preferred_element_type=_F32) + br_ref[...]
    logits_t = logits.T
    gl = logits_t[N_EXPERTS:N_EXPERTS + N_GROUPS]
    ge = jnp.exp(gl - jnp.max(gl, axis=0, keepdims=True))
    gp = ge / jnp.sum(ge, axis=0, keepdims=True)
    g_top = jnp.max(gp, axis=0, keepdims=True)
    g_row = lax.broadcasted_iota(jnp.int32, (N_GROUPS, tm), 0)
    g_idx = jnp.min(jnp.where(gp == g_top, g_row, N_GROUPS), axis=0, keepdims=True)
    fl = logits_t[0:EXPERTS_PER_GROUP]
    for g in range(1, N_GROUPS):
        fl = jnp.where(g_idx == g, logits_t[g * EXPERTS_PER_GROUP:(g + 1) * EXPERTS_PER_GROUP], fl)
    f_row = lax.broadcasted_iota(jnp.int32, (EXPERTS_PER_GROUP, tm), 0)
    f1 = jnp.max(fl, axis=0, keepdims=True)
    i1 = jnp.min(jnp.where(fl == f1, f_row, EXPERTS_PER_GROUP), axis=0, keepdims=True)
    fl2 = jnp.where(f_row == i1, -jnp.inf, fl)
    f2 = jnp.max(fl2, axis=0, keepdims=True)
    i2 = jnp.min(jnp.where(fl2 == f2, f_row, EXPERTS_PER_GROUP), axis=0, keepdims=True)
    e1 = g_idx * EXPERTS_PER_GROUP + i1
    e2 = g_idx * EXPERTS_PER_GROUP + i2
    t2 = jnp.exp(f2 - f1)
    w1 = g_top * (1.0 / (1.0 + t2))
    w2 = g_top * (t2 / (1.0 + t2))

    @pl.when(step == 0)
    def _():
        carry_ref[...] = jnp.zeros_like(carry_ref)

    x_row = lax.broadcasted_iota(jnp.int32, (N_EXPERTS, tm), 0)
    onehot = ((x_row == e1) | (x_row == e2)).astype(_F32)
    rr = lax.broadcasted_iota(jnp.int32, (tm, tm), 0)
    cc = lax.broadcasted_iota(jnp.int32, (tm, tm), 1)
    before = jnp.dot(onehot.astype(_BF16), (rr < cc).astype(_BF16), preferred_element_type=_F32)
    before = before + carry_ref[0:N_EXPERTS, 0:1]
    r1 = jnp.sum(jnp.where(x_row == e1, before, 0.0), axis=0, keepdims=True)
    r2 = jnp.sum(jnp.where(x_row == e2, before, 0.0), axis=0, keepdims=True)
    total = carry_ref[0:N_EXPERTS, 0:1] + jnp.sum(onehot, axis=1, keepdims=True)
    carry_ref[0:N_EXPERTS, :] = jnp.broadcast_to(total, (N_EXPERTS, LANES))
    counts_ref[...] = carry_ref[...]

    m_row = lax.broadcasted_iota(jnp.int32, (8, tm), 0)
    meta = jnp.where(m_row == 2, e1, 0)
    meta = jnp.where(m_row == 3, e2, meta)
    meta = jnp.where(m_row == 4, r1.astype(jnp.int32), meta)
    meta = jnp.where(m_row == 5, r2.astype(jnp.int32), meta)
    meta_ref[...] = meta
    w_row = lax.broadcasted_iota(jnp.int32, (LANES, tm), 0)
    route_ref[...] = jnp.where(w_row == 0, w1, jnp.where(w_row == 1, w2, 0.0)).T


def _postattn(yf, ym, gates, x, wfb, wmb, wout, g, wr, br, *, tm):
    t, d = x.shape
    width = yf.shape[1]
    row = lambda cols: pl.BlockSpec((tm, cols), lambda i: (i, 0))
    return pl.pallas_call(
        functools.partial(_postattn_body, tm=tm, d=d),
        grid=(t // tm,),
        in_specs=[row(width), row(width), row(2 * d), row(d),
                  _const_spec(wfb.shape), _const_spec(wmb.shape), _const_spec(wout.shape),
                  _const_spec((1, d)), _const_spec(wr.shape), _const_spec(br.shape)],
        out_specs=[row(d), row(d // 2), row(LANES), pl.BlockSpec((8, tm), lambda i: (0, i)),
                   _const_spec((LANES, LANES))],
        out_shape=[jax.ShapeDtypeStruct((t, d), _F32),
                   jax.ShapeDtypeStruct((t, d // 2), jnp.uint32),
                   jax.ShapeDtypeStruct((t, LANES), _F32),
                   jax.ShapeDtypeStruct((8, t), jnp.int32),
                   jax.ShapeDtypeStruct((LANES, LANES), _F32)],
        scratch_shapes=[pltpu.VMEM((LANES, LANES), _F32)],
        compiler_params=_params("arbitrary"),
        name="postattn",
    )(yf, ym, gates, x, wfb, wmb, wout, g, wr, br)


def _sc_scatter(rows, idx, n_out):
    n, width = rows.shape
    workers = SC_CORES * SC_SUBCORES
    per_worker = n // workers
    chunks = per_worker // SC_SCATTER_CHUNK
    assert n % (workers * SC_SCATTER_CHUNK) == 0 and idx.shape == (2, n // SC_SCATTER_CHUNK, SC_SCATTER_CHUNK)
    mesh = plsc.VectorSubcoreMesh(core_axis_name="c", subcore_axis_name="s",
                                  num_cores=SC_CORES, num_subcores=SC_SUBCORES)

    @functools.partial(
        pl.kernel, out_type=jax.ShapeDtypeStruct((n_out, width), rows.dtype), mesh=mesh,
        scratch_types=[pltpu.VMEM((2, chunks, SC_SCATTER_CHUNK), jnp.int32),
                       pltpu.VMEM((SC_SCATTER_CHUNK, width), rows.dtype),
                       pltpu.SemaphoreType.DMA],
        name="sc_scatter")
    def scatter(rows_hbm, idx_hbm, out_hbm, idx_v, rows_v, sem):
        worker = lax.axis_index("s") * SC_CORES + lax.axis_index("c")
        for k in range(2):
            pltpu.sync_copy(idx_hbm.at[k, pl.ds(worker * chunks, chunks)], idx_v.at[k])

        def chunk(j, carry):
            start = worker * per_worker + j * SC_SCATTER_CHUNK
            pltpu.sync_copy(rows_hbm.at[pl.ds(start, SC_SCATTER_CHUNK)], rows_v)
            for k in range(2):
                pltpu.async_copy(rows_v, out_hbm.at[idx_v.at[k, j]], sem).wait()
            return carry

        lax.fori_loop(0, chunks, chunk, 0)

    return scatter(rows, idx)


def _experts_body(be_ref, valid_ref, nused_ref, xs_ref, wg_ref, wu_ref, wd_ref, out_ref,
                  wgb_ref, wub_ref, wdb_ref):
    j = pl.program_id(0)
    used = j < nused_ref[0]

    @pl.when(jnp.logical_not(used))
    def _():
        out_ref[...] = jnp.zeros_like(out_ref)

    @pl.when(used & ((j == 0) | (be_ref[j] != be_ref[jnp.maximum(j - 1, 0)])))
    def _():
        wgb_ref[...] = wg_ref[...].astype(_BF16)
        wub_ref[...] = wu_ref[...].astype(_BF16)
        wdb_ref[...] = wd_ref[...].astype(_BF16)

    @pl.when(used)
    def _():
        row = lax.broadcasted_iota(jnp.int32, xs_ref.shape, 0)
        packed = jnp.where(row < valid_ref[j], xs_ref[...], jnp.uint32(0))
        rows = _unpack_bf16_pair(packed).astype(_BF16)
        hidden = wgb_ref.shape[1]
        out = None
        for c in range(0, hidden, hidden // EXPERT_PARTS):
            cols = slice(c, c + hidden // EXPERT_PARTS)
            gate = jnp.dot(rows, wgb_ref[:, cols], preferred_element_type=_F32)
            up = jnp.dot(rows, wub_ref[:, cols], preferred_element_type=_F32)
            act = ((gate * jax.nn.sigmoid(gate)) * up).astype(_BF16)
            part = jnp.dot(act, wdb_ref[cols, :], preferred_element_type=_F32)
            out = part if out is None else out + part
        half = out.shape[1] // 2
        out_ref[...] = _pack_bf16_pair(out[:, 0:half], out[:, half:])


def _experts(block_expert, block_valid, n_used, xs, wg, wu, wd, *, te):
    n_rows, half = xs.shape
    _, d, de = wg.shape

    def blk(j, be, bv, nu):
        return jnp.minimum(j, nu[0] - 1)

    def weight(j, be, bv, nu):
        return (be[blk(j, be, bv, nu)], 0, 0)

    return pl.pallas_call(
        _experts_body,
        grid_spec=pltpu.PrefetchScalarGridSpec(
            num_scalar_prefetch=3,
            grid=(n_rows // te,),
            in_specs=[pl.BlockSpec((te, half), lambda j, be, bv, nu: (blk(j, be, bv, nu), 0)),
                      pl.BlockSpec((None, d, de), weight),
                      pl.BlockSpec((None, d, de), weight),
                      pl.BlockSpec((None, de, d), weight)],
            out_specs=pl.BlockSpec((te, d // 2), lambda j, be, bv, nu: (j, 0)),
            scratch_shapes=[pltpu.VMEM((d, de), _BF16), pltpu.VMEM((d, de), _BF16),
                            pltpu.VMEM((de, d), _BF16)],
        ),
        out_shape=jax.ShapeDtypeStruct((n_rows, d // 2), jnp.uint32),
        compiler_params=_params("arbitrary"),
        name="experts",
    )(block_expert, block_valid, n_used, xs, wg, wu, wd)


def _sc_gather(table, idx):
    n, = idx.shape
    width = table.shape[1]
    workers = SC_CORES * SC_SUBCORES
    per_worker = n // workers
    chunks = per_worker // SC_CHUNK
    assert n % (workers * SC_CHUNK * 2) == 0
    mesh = plsc.VectorSubcoreMesh(core_axis_name="c", subcore_axis_name="s",
                                  num_cores=SC_CORES, num_subcores=SC_SUBCORES)

    @functools.partial(
        pl.kernel, out_type=jax.ShapeDtypeStruct((n, width), table.dtype), mesh=mesh,
        scratch_types=[pltpu.VMEM((per_worker,), jnp.int32),
                       pltpu.VMEM((2, SC_CHUNK, width), table.dtype),
                       pltpu.SemaphoreType.DMA((2,))],
        name="sc_gather")
    def gather(table_hbm, idx_hbm, out_hbm, idx_v, rows_v, sems):
        worker = lax.axis_index("s") * SC_CORES + lax.axis_index("c")
        base = worker * per_worker
        pltpu.sync_copy(idx_hbm.at[pl.ds(base, per_worker)], idx_v)

        def fetch(j, slot):
            off = pl.multiple_of(j * SC_CHUNK, SC_CHUNK)
            return pltpu.async_copy(table_hbm.at[idx_v.at[pl.ds(off, SC_CHUNK)]], rows_v.at[slot],
                                    sems.at[slot])

        def drain(j, slot):
            off = pl.multiple_of(j * SC_CHUNK, SC_CHUNK)
            pltpu.make_async_copy(out_hbm.at[pl.ds(base + off, SC_CHUNK)], rows_v.at[slot],
                                  sems.at[slot]).wait()
            pltpu.sync_copy(rows_v.at[slot], out_hbm.at[pl.ds(base + off, SC_CHUNK)])

        fetch(0, 0)

        def pair(p, carry):
            fetch(2 * p + 1, 1)
            drain(2 * p, 0)

            @pl.when(p + 1 < chunks // 2)
            def _():
                fetch(2 * p + 2, 0)

            drain(2 * p + 1, 1)
            return carry

        lax.fori_loop(0, chunks // 2, pair, 0)

    return gather(table, idx)


def _final_body(x1_ref, route_ref, p_ref, r1_ref, r2_ref, gp_ref, wpg_ref, wpp_ref, gf_ref, o_ref):
    emb = jnp.dot(p_ref[...].astype(_BF16), wpp_ref[...], preferred_element_type=_F32)
    route = route_ref[...]
    x2 = x1_ref[...] + (_unpack_bf16_pair(r1_ref[...]) * route[:, 0:1]
                        + _unpack_bf16_pair(r2_ref[...]) * route[:, 1:2])
    gate = jax.nn.sigmoid(jnp.dot(_rms(x2, gp_ref[...]).astype(_BF16), wpg_ref[...],
                                  preferred_element_type=_F32))
    o_ref[...] = _rms(x2 + gate * emb, gf_ref[...])


def _final(x1, route, p, slot_rows, gp, wpg, wpp, gf, *, tm):
    t, d = x1.shape
    ple = p.shape[1]
    steps = t // tm
    row = lambda cols: pl.BlockSpec((tm, cols), lambda i: (i, 0))
    return pl.pallas_call(
        _final_body,
        grid=(steps,),
        in_specs=[row(d), row(LANES), row(ple), row(d // 2),
                  pl.BlockSpec((tm, d // 2), lambda i: (steps + i, 0)),
                  _const_spec((1, d)), _const_spec(wpg.shape), _const_spec(wpp.shape),
                  _const_spec((1, d))],
        out_specs=row(d),
        out_shape=jax.ShapeDtypeStruct((t, d), _F32),
        compiler_params=_params("arbitrary"),
        name="final",
    )(x1, route, p, slot_rows, slot_rows, gp, wpg, wpp, gf)


def _rope_tables(s):
    half = HEAD_DIM // 2
    inv = 1.0 / (ROPE_THETA ** (jnp.arange(0, HEAD_DIM, 2, dtype=_F32) / HEAD_DIM))
    ang = jnp.arange(s, dtype=_F32)[:, None] * inv[None, :]
    cos, sin = jnp.cos(ang), jnp.sin(ang)
    reps = LANES // half
    cos_t = jnp.tile(cos, (1, reps))
    sin_t = jnp.tile(jnp.concatenate([-sin, sin], axis=1), (1, reps // 2))
    return cos_t, sin_t


def _layer(x, p, attn_norm, w_in, b_forget, w_fox_branch, w_moba_branch, w_out, moe_norm,
           w_group, b_group, w_fine, b_fine, w_gate, w_up, w_down, ple_norm, w_ple_gate, w_ple_proj,
           final_gain):
    b, s, d = x.shape
    t = b * s
    width = N_FOX_HEADS * HEAD_DIM
    assert N_MOBA_HEADS * HEAD_DIM == width and MOBA_BLOCK % ATTN_Q == 0 and MOBA_PIPELINE[0] % MOBA_BLOCK == 0
    assert BIAS_PIECES * N_FOX_HEADS <= LANES and s % MOBA_PIPELINE[0] == 0
    assert s % ROW_TILE == 0 and ROW_TILE % MOBA_BLOCK == 0 and s // MOBA_BLOCK <= LANES
    assert t % FINAL_TILE == 0 and N_EXPERTS + N_GROUPS <= LANES

    qkv_cols = 6 * width
    f_cols = N_FOX_HEADS
    chunk = lambda c: w_in[:, c * width:(c + 1) * width]
    w_rows = jnp.concatenate([chunk(1), chunk(4), w_in[:, qkv_cols + f_cols:]], axis=1).astype(_BF16)
    w_cols = jnp.concatenate([chunk(0), chunk(2), chunk(3), chunk(5)], axis=1).T.astype(_BF16)
    wf = jnp.zeros((d, LANES), _BF16).at[:, :f_cols].set(w_in[:, qkv_cols:qkv_cols + f_cols].astype(_BF16))
    bfp = jnp.zeros((1, LANES), _F32).at[0, :f_cols].set(b_forget.astype(_F32))
    cos, sin = _rope_tables(s)

    qft, kf, vft, qmt, km, vmt, gates, cb, kmean, kabs, bmax = _inproj(
        x, attn_norm.reshape(1, d), w_rows, w_cols, wf, bfp, cos, sin, cos.T, sin.T, tm=ROW_TILE)
    y_fox = _attention(qft, kf, cb, vft, kabs, bmax, mode="fox")
    block_of_key = jnp.arange(s, dtype=jnp.int32)[:, None] // MOBA_BLOCK
    block_onehot = (block_of_key == jnp.arange(LANES, dtype=jnp.int32)[None, :]).astype(_BF16)
    y_moba = _attention(qmt, km, block_onehot, vmt, kmean, bmax, mode="moba")

    wr = jnp.zeros((d, LANES), _BF16)
    wr = wr.at[:, :N_EXPERTS].set(w_fine.astype(_BF16))
    wr = wr.at[:, N_EXPERTS:N_EXPERTS + N_GROUPS].set(w_group.astype(_BF16))
    br = jnp.zeros((1, LANES), _F32)
    br = br.at[0, :N_EXPERTS].set(b_fine.astype(_F32))
    br = br.at[0, N_EXPERTS:N_EXPERTS + N_GROUPS].set(b_group.astype(_F32))
    x1, h2p, route, meta, counts = _postattn(
        y_fox.reshape(t, width), y_moba.reshape(t, width), gates.reshape(t, 2 * d), x.reshape(t, d),
        w_fox_branch.astype(_BF16), w_moba_branch.astype(_BF16), w_out.astype(_BF16),
        moe_norm.reshape(1, d), wr, br, tm=ROW_TILE)

    te = EXPERT_TILE
    n_blk = (2 * t) // te + N_EXPERTS
    counts = counts[:N_EXPERTS, 0].astype(jnp.int32)
    blocks_per = (counts + te - 1) // te
    block_end = jnp.cumsum(blocks_per)
    row_start = (block_end - blocks_per) * te
    expert_ids = jnp.arange(N_EXPERTS, dtype=jnp.int32)[:, None, None]
    slot_major = meta[4:6] + jnp.sum(
        jnp.where(meta[2:4][None] == expert_ids, row_start[:, None, None], 0), axis=0)
    n_used = block_end[-1:].astype(jnp.int32)
    block_ids = jnp.arange(n_blk, dtype=jnp.int32)
    block_expert = jnp.minimum(
        jnp.sum((block_ids[:, None] >= block_end[None, :]).astype(jnp.int32), axis=1), N_EXPERTS - 1)

    first_block = block_end - blocks_per
    block_valid = jnp.clip(counts[block_expert] - (block_ids - first_block[block_expert]) * te, 0, te)
    xs = _sc_scatter(h2p, slot_major.reshape(2, t // SC_SCATTER_CHUNK, SC_SCATTER_CHUNK), n_blk * te)
    rows = _experts(block_expert, block_valid, n_used, xs, w_gate, w_up, w_down, te=te)
    slot_rows = _sc_gather(rows, slot_major.reshape(-1))
    out = _final(x1, route, p.reshape(t, -1), slot_rows, ple_norm.reshape(1, d),
                 w_ple_gate.astype(_BF16), w_ple_proj.astype(_BF16), final_gain.reshape(1, d),
                 tm=FINAL_TILE)
    return out.reshape(b, s, d)


def kernel(x, p, attn_norm, w_in, b_forget, w_fox_branch, w_moba_branch, w_out, moe_norm, w_group,
           b_group, w_fine, b_fine, w_gate, w_up, w_down, ple_norm, w_ple_gate, w_ple_proj, final_norm):
    depth = p.shape[0]
    assert depth == 1, "the final norm is fused into the last layer's kernel"
    i = 0
    return _layer(x, p[i], attn_norm[i], w_in[i], b_forget[i], w_fox_branch[i], w_moba_branch[i],
                  w_out[i], moe_norm[i], w_group[i], b_group[i], w_fine[i], b_fine[i], w_gate[i],
                  w_up[i], w_down[i], ple_norm[i], w_ple_gate[i], w_ple_proj[i], final_norm)
```

```python
import functools

import jax
import jax.numpy as jnp
from jax import lax
from jax.experimental import pallas as pl
from jax.experimental.pallas import tpu as pltpu
from jax.experimental.pallas import tpu_sc as plsc

HEAD_DIM = 64
N_FOX_HEADS = 8
N_MOBA_HEADS = 8
MOBA_BLOCK = 256
MOBA_TOPK = 3
ROPE_THETA = 10000.0
N_GROUPS = 4
EXPERTS_PER_GROUP = 8
N_EXPERTS = N_GROUPS * EXPERTS_PER_GROUP
RMS_EPS = 1e-6

LANES = 128
HEADS_PER_STEP = LANES // HEAD_DIM
ATTN_Q = 256
FOX_PIPELINE = (256, 8, 4)
MOBA_PIPELINE = (256, 8, 4)
ONES_ROWS = 16
SOFTMAX_ROWS = 64
BIAS_PIECES = 3
ROW_TILE = 512
SC_CORES = 2
SC_SUBCORES = 16
SC_CHUNK = 64
SC_SCATTER_CHUNK = 128
FINAL_TILE = 1024
EXPERT_PARTS = 2
EXPERT_TILE = 512
LOG2_E = 1.4426950408889634
UNDERFLOW_LOG2 = 160.0
NEG_BIG = -1e30
VMEM_LIMIT = 48 * 1024 * 1024

_BF16 = jnp.bfloat16
_F32 = jnp.float32


def _params(*sem):
    return pltpu.CompilerParams(dimension_semantics=sem, vmem_limit_bytes=VMEM_LIMIT)


def _rms(x, g):
    return x * lax.rsqrt(jnp.mean(x * x, axis=-1, keepdims=True) + RMS_EPS) * g


def _const_spec(shape):
    return pl.BlockSpec(shape, lambda *_: (0,) * len(shape))


def _inproj_body(x_ref, g_ref, w_ref, wt_ref, wf_ref, bf_ref, cos_ref, sin_ref, cost_ref, sint_ref,
                 qft_ref, kf_ref, vft_ref, qmt_ref, km_ref, vmt_ref, gates_ref, cb_ref, kmean_ref,
                 kabs_ref, bmax_ref,
                 carry_ref, *, tm, width, gate_cols):
    j = pl.program_id(1)
    h = _rms(x_ref[...], g_ref[...])
    hb = h.astype(_BF16)
    hbt = h.T.astype(_BF16)
    reps = width // LANES
    half = HEAD_DIM // 2

    blocks = tm // MOBA_BLOCK

    @pl.when(j == 0)
    def _():
        kmean_ref[...] = jnp.zeros_like(kmean_ref)
        kabs_ref[...] = jnp.zeros_like(kabs_ref)
        bmax_ref[...] = jnp.zeros_like(bmax_ref)
        carry_ref[...] = jnp.zeros_like(carry_ref)

    z = jnp.dot(hb, wf_ref[...], preferred_element_type=_F32) + bf_ref[...]
    log_f = jnp.minimum(z, 0.0) - jnp.log1p(jnp.exp(-jnp.abs(z)))
    r = lax.broadcasted_iota(jnp.int32, (LANES, LANES), 0)
    c = lax.broadcasted_iota(jnp.int32, (LANES, LANES), 1)
    tri = (c <= r).astype(_BF16)
    offset = carry_ref[0:1, :]
    blocks_cum = []
    for blk in range(tm // LANES):
        rest = log_f[blk * LANES:(blk + 1) * LANES]
        within = jnp.zeros((LANES, LANES), _F32)
        for _ in range(BIAS_PIECES):
            part = rest.astype(_BF16)
            rest = rest - part.astype(_F32)
            within = within + jnp.dot(tri, part, preferred_element_type=_F32)
        blocks_cum.append(within + offset)
        offset = offset + within[LANES - 1:LANES, :]
    cum = jnp.concatenate(blocks_cum, axis=0)
    carry_ref[...] = jnp.broadcast_to(offset, carry_ref.shape)

    head_lane = lax.broadcasted_iota(jnp.int32, (1, LANES), 1) < N_FOX_HEADS
    bias = jnp.where(head_lane, -cum * LOG2_E, 0.0)
    for b in range(blocks):
        bmax_ref[pl.ds(j * blocks + b, 1), :] = jnp.max(
            bias[b * MOBA_BLOCK:(b + 1) * MOBA_BLOCK], axis=0, keepdims=True)
    rest = bias
    placed = jnp.zeros((tm, LANES), _F32)
    for piece in range(BIAS_PIECES):
        part = rest.astype(_BF16).astype(_F32)
        rest = rest - part
        placed = placed + (pltpu.roll(part, N_FOX_HEADS * piece, 1) if piece else part)
    cb_ref[...] = placed.astype(_BF16)

    def proj(c):
        return jnp.dot(hb, w_ref[:, c * width:(c + 1) * width], preferred_element_type=_F32)

    def proj_t(c):
        return jnp.dot(wt_ref[c * width:(c + 1) * width, :], hbt, preferred_element_type=_F32)

    def rope(t):
        cos = jnp.concatenate([cos_ref[...]] * reps, axis=1)
        sin = jnp.concatenate([sin_ref[...]] * reps, axis=1)
        first = (lax.broadcasted_iota(jnp.int32, (1, width), 1) % HEAD_DIM) < half
        partner = jnp.where(first, pltpu.roll(t, width - half, 1), pltpu.roll(t, half, 1))
        return t * cos + partner * sin

    def rope_t(t):
        cos = jnp.concatenate([cost_ref[...]] * reps, axis=0)
        sin = jnp.concatenate([sint_ref[...]] * reps, axis=0)
        first = (lax.broadcasted_iota(jnp.int32, (width, 1), 0) % HEAD_DIM) < half
        partner = jnp.where(first, pltpu.roll(t, width - half, 0), pltpu.roll(t, half, 0))
        return t * cos + partner * sin

    scale = HEAD_DIM ** -0.5 * LOG2_E
    kf = proj(0).astype(_BF16)
    kf_ref[...] = kf
    kf_abs = jnp.abs(kf.astype(_F32))
    km = rope(proj(1))
    km_ref[...] = km.astype(_BF16)

    for b in range(blocks):
        rows = slice(b * MOBA_BLOCK, (b + 1) * MOBA_BLOCK)
        kmean_ref[pl.ds(j * blocks + b, 1), :] = jnp.mean(km[rows], axis=0, keepdims=True)
        kabs_ref[pl.ds(j * blocks + b, 1), :] = jnp.max(kf_abs[rows], axis=0, keepdims=True)

    for c in range(gate_cols // width):
        g = jnp.dot(hb, w_ref[:, (2 + c) * width:(3 + c) * width],
                    preferred_element_type=_F32)
        gates_ref[:, c * width:(c + 1) * width] = jax.nn.sigmoid(g).astype(_BF16)

    qft_ref[...] = (proj_t(0) * scale).astype(_BF16)
    vft_ref[...] = proj_t(1).astype(_BF16)
    qmt_ref[...] = (rope_t(proj_t(2)) * scale).astype(_BF16)
    vmt_ref[...] = proj_t(3).astype(_BF16)


def _inproj(x, g, w, wt, wf, bfp, cos, sin, cos_t, sin_t, *, tm):
    b, s, d = x.shape
    width = N_FOX_HEADS * HEAD_DIM
    gate_cols = w.shape[1] - 2 * width
    act = jax.ShapeDtypeStruct((b, s, width), _BF16)
    act_t = jax.ShapeDtypeStruct((b, width, s), _BF16)
    act_spec = pl.BlockSpec((None, tm, width), lambda bi, j: (bi, j, 0))
    act_t_spec = pl.BlockSpec((None, width, tm), lambda bi, j: (bi, 0, j))
    return pl.pallas_call(
        functools.partial(_inproj_body, tm=tm, width=width, gate_cols=gate_cols),
        grid=(b, s // tm),
        in_specs=[
            pl.BlockSpec((None, tm, d), lambda bi, j: (bi, j, 0)),
            _const_spec((1, d)),
            _const_spec(w.shape),
            _const_spec(wt.shape),
            _const_spec(wf.shape),
            _const_spec(bfp.shape),
            pl.BlockSpec((tm, LANES), lambda bi, j: (j, 0)),
            pl.BlockSpec((tm, LANES), lambda bi, j: (j, 0)),
            pl.BlockSpec((LANES, tm), lambda bi, j: (0, j)),
            pl.BlockSpec((LANES, tm), lambda bi, j: (0, j)),
        ],
        out_specs=[act_t_spec, act_spec, act_t_spec, act_t_spec, act_spec, act_t_spec,
                   pl.BlockSpec((None, tm, gate_cols), lambda bi, j: (bi, j, 0)),
                   pl.BlockSpec((None, tm, LANES), lambda bi, j: (bi, j, 0)),
                   pl.BlockSpec((None, LANES, width), lambda bi, j: (bi, 0, 0)),
                   pl.BlockSpec((None, LANES, width), lambda bi, j: (bi, 0, 0)),
                   pl.BlockSpec((None, LANES, LANES), lambda bi, j: (bi, 0, 0))],
        out_shape=[act_t, act, act_t, act_t, act, act_t,
                   jax.ShapeDtypeStruct((b, s, gate_cols), _BF16),
                   jax.ShapeDtypeStruct((b, s, LANES), _BF16),
                   jax.ShapeDtypeStruct((b, LANES, width), _F32),
                   jax.ShapeDtypeStruct((b, LANES, width), _F32),
                   jax.ShapeDtypeStruct((b, LANES, LANES), _F32)],
        scratch_shapes=[pltpu.VMEM((8, LANES), _F32)],
        compiler_params=_params("arbitrary", "arbitrary"),
        name="inproj",
    )(x, g, w, wt, wf, bfp, cos, sin, cos_t, sin_t)


def _attn_body(uq_ref, uk_ref, qt_ref, k_ref, kx_ref, vt_ref, aux_ref, aux2_ref, tri_ref, o_ref,
               w_ref, m_ref, l_ref, acc_ref, st_ref, p_ref, alpha_ref, pmax_ref, qn_ref, *,
               mode, tq, tk, nb, n_diag, n_units, pipe, lag):
    hp = pl.program_id(1)
    wide = HEADS_PER_STEP * tq
    nq = w_ref.shape[0]
    feat = lax.broadcasted_iota(jnp.int32, (LANES, 1), 0)
    xr = lax.broadcasted_iota(jnp.int32, (LANES, wide), 0)
    xc = lax.broadcasted_iota(jnp.int32, (LANES, wide), 1)

    def setup(i, carry):
        qt = qt_ref[:, pl.ds(pl.multiple_of(i * tq, tq), tq)]
        zero = jnp.zeros_like(qt)
        wq = jnp.concatenate([jnp.where(feat < HEAD_DIM, qt, zero),
                              jnp.where(feat < HEAD_DIM, zero, qt)], axis=1)
        if mode == "fox":
            head = hp * HEADS_PER_STEP + xc // tq
            lower = ((xr % N_FOX_HEADS == head) & (xr < BIAS_PIECES * N_FOX_HEADS)).astype(_BF16)
            qn_ref[i] = jnp.broadcast_to(
                jnp.sum(jnp.abs(wq.astype(_F32)), axis=0, keepdims=True), qn_ref.shape[1:])
        else:
            own = (i * tq) // MOBA_BLOCK
            gate = jnp.dot(aux_ref[...].astype(_BF16), wq, preferred_element_type=_F32)[0:nb]
            blk = lax.broadcasted_iota(jnp.int32, (nb, wide), 0)
            gate = jnp.where(blk < own, gate, -jnp.inf)
            keep = blk == own
            for _ in range(MOBA_TOPK):
                mx = jnp.max(gate, axis=0, keepdims=True)
                cand = jnp.where((gate == mx) & (mx > -jnp.inf), blk, nb)
                pick = blk == jnp.min(cand, axis=0, keepdims=True)
                keep = keep | pick
                gate = jnp.where(pick, -jnp.inf, gate)
            lower = jnp.where(keep, 0.0, NEG_BIG)
            if nb < LANES:
                lower = jnp.concatenate([lower, jnp.zeros((LANES - nb, wide), _F32)], axis=0)
            lower = lower.astype(_BF16)
        w_ref[i] = jnp.concatenate([wq, lower], axis=0)
        m_ref[i] = jnp.full(m_ref.shape[1:], NEG_BIG, _F32)
        l_ref[i] = jnp.zeros(l_ref.shape[1:], _F32)
        acc_ref[i] = jnp.zeros(acc_ref.shape[1:], _F32)
        return carry

    lax.fori_loop(0, nq, setup, 0, unroll=8)

    def unit(t):
        return uq_ref[t], uk_ref[t]

    def scores(t, slot, causal):
        q, kb = unit(t)
        start = pl.multiple_of(kb * tk, tk)
        ka = jnp.concatenate([k_ref[pl.ds(start, tk), :], kx_ref[pl.ds(start, tk), :]], axis=1)
        st = jnp.dot(ka, w_ref[q], preferred_element_type=_F32)
        if causal:
            st = st + tri_ref[...]
        st_ref[slot] = st
        pmax_ref[slot] = jnp.max(st.reshape(tk // 8, 8, wide), axis=0)

    def softmax(t, slot):
        q, _ = unit(t)
        rows = SOFTMAX_ROWS
        m_prev = m_ref[q][0:1, :]
        m_new = jnp.maximum(m_prev, jnp.max(pmax_ref[slot], axis=0, keepdims=True))
        for r in range(0, tk, rows):
            p_ref[slot, r:r + rows, :] = jnp.exp2(st_ref[slot, r:r + rows, :] - m_new).astype(_BF16)
        m_ref[q] = jnp.broadcast_to(m_new, m_ref.shape[1:])
        alpha_ref[slot] = jnp.broadcast_to(jnp.exp2(m_prev - m_new), alpha_ref.shape[1:])

    def values(t, slot):
        q, kb = unit(t)
        start = pl.multiple_of(kb * tk, tk)
        ones = jnp.ones((ONES_ROWS, tk), _BF16)
        for h in range(HEADS_PER_STEP):
            cols = slice(h * tq, (h + 1) * tq)
            feats = slice(h * HEAD_DIM, (h + 1) * HEAD_DIM)
            lhs = jnp.concatenate([vt_ref[feats, pl.ds(start, tk)], ones], axis=0)
            pv = jnp.dot(lhs, p_ref[slot, :, cols], preferred_element_type=_F32)
            alpha = alpha_ref[slot, 0:1, cols]
            acc_ref[q, feats, :] = acc_ref[q, feats, :] * alpha + pv[0:HEAD_DIM]
            l_ref[q, :, cols] = jnp.broadcast_to(
                alpha * l_ref[q, 0:1, cols] + pv[HEAD_DIM:HEAD_DIM + 1], (8, tq))

    def step(base, c, causal, do_scores=True, do_softmax=True, do_values=True):
        u = base + c
        if do_scores:
            scores(u, c % pipe, causal)
        if do_values:
            values(u - 2 * lag, (c - 2 * lag) % pipe)
        if do_softmax:
            softmax(u - lag, (c - lag) % pipe)

    def group(causal):
        def body(j, carry):
            for c in range(pipe):
                step(j * pipe, c, causal)
            return carry
        return body

    def pipeline(g_first, g_end, causal_groups):
        for c in range(pipe):
            step(g_first * pipe, c, g_first < causal_groups,
                 do_softmax=c >= lag, do_values=c >= 2 * lag)
        if g_first + 1 < causal_groups:
            lax.fori_loop(g_first + 1, causal_groups, group(True), 0)
        lax.fori_loop(max(g_first + 1, causal_groups), g_end, group(False), 0)
        for c in range(2 * lag):
            step(g_end * pipe, c, False, do_scores=False, do_softmax=c < lag)

    diag_groups = n_diag // pipe
    if mode == "moba":
        pipeline(0, n_units // pipe, diag_groups)
    else:
        pipeline(0, diag_groups, diag_groups)
        nq_blocks = n_diag
        col = lax.broadcasted_iota(jnp.int32, (LANES, LANES), 1)
        first_head = lax.broadcasted_iota(jnp.int32, (1, wide), 1) < tq
        head0 = hp * HEADS_PER_STEP

        def per_head(table, pick0, pick1, fill):
            v0 = jnp.max(jnp.where(pick0, table, fill), axis=1, keepdims=True)[0:nb]
            v1 = jnp.max(jnp.where(pick1, table, fill), axis=1, keepdims=True)[0:nb]
            return jnp.where(first_head, v0, v1)

        bias_max = per_head(aux2_ref[...], col == head0, col == head0 + 1, NEG_BIG)
        k_absmax = per_head(aux_ref[...], col < HEAD_DIM, col >= HEAD_DIM, 0.0)
        blk = lax.broadcasted_iota(jnp.int32, (nb, 1), 0)

        needed = jnp.zeros((nb, 1), jnp.int32)
        for i in range(1, nq_blocks):
            bound = bias_max + qn_ref[i, 0:1, :] * k_absmax - m_ref[i, 0:1, :]
            live = jnp.max(bound, axis=1, keepdims=True) >= -UNDERFLOW_LOG2
            needed = jnp.maximum(needed, jnp.where(live & (blk < i), i - blk, 0))
        far = jnp.max(needed)
        n_live = far * nq_blocks - (far * (far + 1)) // 2
        live_groups = (n_live + pipe - 1) // pipe

        @pl.when(live_groups > 0)
        def _():
            pipeline(diag_groups, diag_groups + live_groups, diag_groups)

    def finish(i, carry):
        out_t = jnp.concatenate(
            [acc_ref[i, h * HEAD_DIM:(h + 1) * HEAD_DIM, :] / l_ref[i, 0:1, h * tq:(h + 1) * tq]
             for h in range(HEADS_PER_STEP)], axis=0)
        o_ref[pl.ds(pl.multiple_of(i * tq, tq), tq), :] = out_t.T.astype(o_ref.dtype)
        return carry

    lax.fori_loop(0, nq, finish, 0, unroll=8)


def _attention(qt, k, kx, vt, aux, aux2, *, mode):
    b, width, s = qt.shape
    tq = ATTN_Q
    tk, pipe, lag = FOX_PIPELINE if mode == "fox" else MOBA_PIPELINE
    nq = s // tq
    nb = s // MOBA_BLOCK
    wide = HEADS_PER_STEP * tq
    diag = [(i * tq) // tk for i in range(nq)]
    units = list(enumerate(diag))
    units += [(i, diag[i] - dist) for dist in range(1, nq) for i in range(nq) if diag[i] >= dist]
    assert mode != "fox" or tq == tk
    assert nq % pipe == 0 and len(units) % pipe == 0 and pipe >= 2 * lag
    key_row = jnp.arange(tk, dtype=jnp.int32)[:, None]
    qry_col = jnp.arange(wide, dtype=jnp.int32)[None, :] % tq
    causal_bias = jnp.where(key_row <= qry_col, 0.0, NEG_BIG).astype(_F32)
    unit_q = jnp.asarray([u[0] for u in units], jnp.int32)
    unit_k = jnp.asarray([u[1] for u in units], jnp.int32)
    aux_spec = pl.BlockSpec((None, LANES, LANES), lambda bi, hp, uq, uk: (bi, 0, hp))
    aux2_spec = pl.BlockSpec((None, LANES, LANES), lambda bi, hp, uq, uk: (bi, 0, 0))
    if mode == "fox":
        kx_spec = pl.BlockSpec((None, s, LANES), lambda bi, hp, uq, uk: (bi, 0, 0))
    else:
        kx_spec = pl.BlockSpec(kx.shape, lambda bi, hp, uq, uk: (0, 0))
    return pl.pallas_call(
        functools.partial(_attn_body, mode=mode, tq=tq, tk=tk, nb=nb, n_diag=nq, n_units=len(units),
                          pipe=pipe, lag=lag),
        grid_spec=pltpu.PrefetchScalarGridSpec(
            num_scalar_prefetch=2,
            grid=(b, width // LANES),
            in_specs=[
                pl.BlockSpec((None, LANES, s), lambda bi, hp, uq, uk: (bi, hp, 0)),
                pl.BlockSpec((None, s, LANES), lambda bi, hp, uq, uk: (bi, 0, hp)),
                kx_spec,
                pl.BlockSpec((None, LANES, s), lambda bi, hp, uq, uk: (bi, hp, 0)),
                aux_spec,
                aux2_spec,
                pl.BlockSpec((tk, wide), lambda bi, hp, uq, uk: (0, 0)),
            ],
            out_specs=pl.BlockSpec((None, s, LANES), lambda bi, hp, uq, uk: (bi, 0, hp)),
            scratch_shapes=[
                pltpu.VMEM((nq, 2 * LANES, wide), _BF16),
                pltpu.VMEM((nq, 8, wide), _F32),
                pltpu.VMEM((nq, 8, wide), _F32),
                pltpu.VMEM((nq, LANES, tq), _F32),
                pltpu.VMEM((pipe, tk, wide), _F32),
                pltpu.VMEM((pipe, tk, wide), _BF16),
                pltpu.VMEM((pipe, 8, wide), _F32),
                pltpu.VMEM((pipe, 8, wide), _F32),
                pltpu.VMEM((nq, 8, wide), _F32),
            ],
        ),
        out_shape=jax.ShapeDtypeStruct((b, s, width), _BF16),
        compiler_params=_params("arbitrary", "arbitrary"),
        name=mode,
    )(unit_q, unit_k, qt, k, kx, vt, aux, aux2, causal_bias)


def _pack_bf16_pair(a, b):
    lo = pltpu.bitcast(a.astype(_BF16).astype(_F32), jnp.uint32) >> 16
    hi = pltpu.bitcast(b.astype(_BF16).astype(_F32), jnp.uint32) & jnp.uint32(0xFFFF0000)
    return lo | hi


def _unpack_bf16_pair(u):
    lo = pltpu.bitcast(u << 16, _F32)
    hi = pltpu.bitcast(u & jnp.uint32(0xFFFF0000), _F32)
    return jnp.concatenate([lo, hi], axis=1)


def _postattn_body(yf_ref, ym_ref, gates_ref, x_ref, wfb_ref, wmb_ref, wout_ref, g_ref, wr_ref, br_ref,
                   x1_ref, h2_ref, route_ref, meta_ref, counts_ref, carry_ref, *, tm, d):
    step = pl.program_id(0)
    ya = jnp.dot(yf_ref[...], wfb_ref[...], preferred_element_type=_F32)
    yb = jnp.dot(ym_ref[...], wmb_ref[...], preferred_element_type=_F32)
    mixed = gates_ref[:, 0:d].astype(_F32) * ya + gates_ref[:, d:2 * d].astype(_F32) * yb
    x1 = x_ref[...] + jnp.dot(mixed.astype(_BF16), wout_ref[...], preferred_element_type=_F32)
    x1_ref[...] = x1
    h2 = _rms(x1, g_ref[...])
    h2_ref[...] = _pack_bf16_pair(h2[:, 0:d // 2], h2[:, d // 2:d])

    logits = jnp.dot(h2.astype(_BF16), wr_ref[...], preferred_element_type=_F32) + br_ref[...]
    logits_t = logits.T
    gl = logits_t[N_EXPERTS:N_EXPERTS + N_GROUPS]
    ge = jnp.exp(gl - jnp.max(gl, axis=0, keepdims=True))
    gp = ge / jnp.sum(ge, axis=0, keepdims=True)
    g_top = jnp.max(gp, axis=0, keepdims=True)
    g_row = lax.broadcasted_iota(jnp.int32, (N_GROUPS, tm), 0)
    g_idx = jnp.min(jnp.where(gp == g_top, g_row, N_GROUPS), axis=0, keepdims=True)
    fl = logits_t[0:EXPERTS_PER_GROUP]
    for g in range(1, N_GROUPS):
        fl = jnp.where(g_idx == g, logits_t[g * EXPERTS_PER_GROUP:(g + 1) * EXPERTS_PER_GROUP], fl)
    f_row = lax.broadcasted_iota(jnp.int32, (EXPERTS_PER_GROUP, tm), 0)
    f1 = jnp.max(fl, axis=0, keepdims=True)
    i1 = jnp.min(jnp.where(fl == f1, f_row, EXPERTS_PER_GROUP), axis=0, keepdims=True)
    fl2 = jnp.where(f_row == i1, -jnp.inf, fl)
    f2 = jnp.max(fl2, axis=0, keepdims=True)
    i2 = jnp.min(jnp.where(fl2 == f2, f_row, EXPERTS_PER_GROUP), axis=0, keepdims=True)
    e1 = g_idx * EXPERTS_PER_GROUP + i1
    e2 = g_idx * EXPERTS_PER_GROUP + i2
    t2 = jnp.exp(f2 - f1)
    w1 = g_top * (1.0 / (1.0 + t2))
    w2 = g_top * (t2 / (1.0 + t2))

    @pl.when(step == 0)
    def _():
        carry_ref[...] = jnp.zeros_like(carry_ref)

    x_row = lax.broadcasted_iota(jnp.int32, (N_EXPERTS, tm), 0)
    onehot = ((x_row == e1) | (x_row == e2)).astype(_F32)
    rr = lax.broadcasted_iota(jnp.int32, (tm, tm), 0)
    cc = lax.broadcasted_iota(jnp.int32, (tm, tm), 1)
    before = jnp.dot(onehot.astype(_BF16), (rr < cc).astype(_BF16), preferred_element_type=_F32)
    before = before + carry_ref[0:N_EXPERTS, 0:1]
    r1 = jnp.sum(jnp.where(x_row == e1, before, 0.0), axis=0, keepdims=True)
    r2 = jnp.sum(jnp.where(x_row == e2, before, 0.0), axis=0, keepdims=True)
    total = carry_ref[0:N_EXPERTS, 0:1] + jnp.sum(onehot, axis=1, keepdims=True)
    carry_ref[0:N_EXPERTS, :] = jnp.broadcast_to(total, (N_EXPERTS, LANES))
    counts_ref[...] = carry_ref[...]

    m_row = lax.broadcasted_iota(jnp.int32, (8, tm), 0)
    meta = jnp.where(m_row == 2, e1, 0)
    meta = jnp.where(m_row == 3, e2, meta)
    meta = jnp.where(m_row == 4, r1.astype(jnp.int32), meta)
    meta = jnp.where(m_row == 5, r2.astype(jnp.int32), meta)
    meta_ref[...] = meta
    w_row = lax.broadcasted_iota(jnp.int32, (LANES, tm), 0)
    route_ref[...] = jnp.where(w_row == 0, w1, jnp.where(w_row == 1, w2, 0.0)).T


def _postattn(yf, ym, gates, x, wfb, wmb, wout, g, wr, br, *, tm):
    t, d = x.shape
    width = yf.shape[1]
    row = lambda cols: pl.BlockSpec((tm, cols), lambda i: (i, 0))
    return pl.pallas_call(
        functools.partial(_postattn_body, tm=tm, d=d),
        grid=(t // tm,),
        in_specs=[row(width), row(width), row(2 * d), row(d),
                  _const_spec(wfb.shape), _const_spec(wmb.shape), _const_spec(wout.shape),
                  _const_spec((1, d)), _const_spec(wr.shape), _const_spec(br.shape)],
        out_specs=[row(d), row(d // 2), row(LANES), pl.BlockSpec((8, tm), lambda i: (0, i)),
                   _const_spec((LANES, LANES))],
        out_shape=[jax.ShapeDtypeStruct((t, d), _F32),
                   jax.ShapeDtypeStruct((t, d // 2), jnp.uint32),
                   jax.ShapeDtypeStruct((t, LANES), _F32),
                   jax.ShapeDtypeStruct((8, t), jnp.int32),
                   jax.ShapeDtypeStruct((LANES, LANES), _F32)],
        scratch_shapes=[pltpu.VMEM((LANES, LANES), _F32)],
        compiler_params=_params("arbitrary"),
        name="postattn",
    )(yf, ym, gates, x, wfb, wmb, wout, g, wr, br)


def _sc_scatter(rows, idx, n_out):
    n, width = rows.shape
    workers = SC_CORES * SC_SUBCORES
    per_worker = n // workers
    chunks = per_worker // SC_SCATTER_CHUNK
    assert n % (workers * SC_SCATTER_CHUNK) == 0 and idx.shape == (2, n // SC_SCATTER_CHUNK, SC_SCATTER_CHUNK)
    mesh = plsc.VectorSubcoreMesh(core_axis_name="c", subcore_axis_name="s",
                                  num_cores=SC_CORES, num_subcores=SC_SUBCORES)

    @functools.partial(
        pl.kernel, out_type=jax.ShapeDtypeStruct((n_out, width), rows.dtype), mesh=mesh,
        scratch_types=[pltpu.VMEM((2, chunks, SC_SCATTER_CHUNK), jnp.int32),
                       pltpu.VMEM((SC_SCATTER_CHUNK, width), rows.dtype),
                       pltpu.SemaphoreType.DMA],
        name="sc_scatter")
    def scatter(rows_hbm, idx_hbm, out_hbm, idx_v, rows_v, sem):
        worker = lax.axis_index("s") * SC_CORES + lax.axis_index("c")
        for k in range(2):
            pltpu.sync_copy(idx_hbm.at[k, pl.ds(worker * chunks, chunks)], idx_v.at[k])

        def chunk(j, carry):
            start = worker * per_worker + j * SC_SCATTER_CHUNK
            pltpu.sync_copy(rows_hbm.at[pl.ds(start, SC_SCATTER_CHUNK)], rows_v)
            for k in range(2):
                pltpu.async_copy(rows_v, out_hbm.at[idx_v.at[k, j]], sem).wait()
            return carry

        lax.fori_loop(0, chunks, chunk, 0)

    return scatter(rows, idx)


def _experts_body(be_ref, valid_ref, nused_ref, xs_ref, wg_ref, wu_ref, wd_ref, out_ref,
                  wgb_ref, wub_ref, wdb_ref):
    j = pl.program_id(0)
    used = j < nused_ref[0]

    @pl.when(jnp.logical_not(used))
    def _():
        out_ref[...] = jnp.zeros_like(out_ref)

    @pl.when(used & ((j == 0) | (be_ref[j] != be_ref[jnp.maximum(j - 1, 0)])))
    def _():
        wgb_ref[...] = wg_ref[...].astype(_BF16)
        wub_ref[...] = wu_ref[...].astype(_BF16)
        wdb_ref[...] = wd_ref[...].astype(_BF16)

    @pl.when(used)
    def _():
        row = lax.broadcasted_iota(jnp.int32, xs_ref.shape, 0)
        packed = jnp.where(row < valid_ref[j], xs_ref[...], jnp.uint32(0))
        rows = _unpack_bf16_pair(packed).astype(_BF16)
        hidden = wgb_ref.shape[1]
        out = None
        for c in range(0, hidden, hidden // EXPERT_PARTS):
            cols = slice(c, c + hidden // EXPERT_PARTS)
            gate = jnp.dot(rows, wgb_ref[:, cols], preferred_element_type=_F32)
            up = jnp.dot(rows, wub_ref[:, cols], preferred_element_type=_F32)
            act = ((gate * jax.nn.sigmoid(gate)) * up).astype(_BF16)
            part = jnp.dot(act, wdb_ref[cols, :], preferred_element_type=_F32)
            out = part if out is None else out + part
        half = out.shape[1] // 2
        out_ref[...] = _pack_bf16_pair(out[:, 0:half], out[:, half:])


def _experts(block_expert, block_valid, n_used, xs, wg, wu, wd, *, te):
    n_rows, half = xs.shape
    _, d, de = wg.shape

    def blk(j, be, bv, nu):
        return jnp.minimum(j, nu[0] - 1)

    def weight(j, be, bv, nu):
        return (be[blk(j, be, bv, nu)], 0, 0)

    return pl.pallas_call(
        _experts_body,
        grid_spec=pltpu.PrefetchScalarGridSpec(
            num_scalar_prefetch=3,
            grid=(n_rows // te,),
            in_specs=[pl.BlockSpec((te, half), lambda j, be, bv, nu: (blk(j, be, bv, nu), 0)),
                      pl.BlockSpec((None, d, de), weight),
                      pl.BlockSpec((None, d, de), weight),
                      pl.BlockSpec((None, de, d), weight)],
            out_specs=pl.BlockSpec((te, d // 2), lambda j, be, bv, nu: (j, 0)),
            scratch_shapes=[pltpu.VMEM((d, de), _BF16), pltpu.VMEM((d, de), _BF16),
                            pltpu.VMEM((de, d), _BF16)],
        ),
        out_shape=jax.ShapeDtypeStruct((n_rows, d // 2), jnp.uint32),
        compiler_params=_params("arbitrary"),
        name="experts",
    )(block_expert, block_valid, n_used, xs, wg, wu, wd)


def _sc_gather(table, idx):
    n, = idx.shape
    width = table.shape[1]
    workers = SC_CORES * SC_SUBCORES
    per_worker = n // workers
    chunks = per_worker // SC_CHUNK
    assert n % (workers * SC_CHUNK * 2) == 0
    mesh = plsc.VectorSubcoreMesh(core_axis_name="c", subcore_axis_name="s",
                                  num_cores=SC_CORES, num_subcores=SC_SUBCORES)

    @functools.partial(
        pl.kernel, out_type=jax.ShapeDtypeStruct((n, width), table.dtype), mesh=mesh,
        scratch_types=[pltpu.VMEM((per_worker,), jnp.int32),
                       pltpu.VMEM((2, SC_CHUNK, width), table.dtype),
                       pltpu.SemaphoreType.DMA((2,))],
        name="sc_gather")
    def gather(table_hbm, idx_hbm, out_hbm, idx_v, rows_v, sems):
        worker = lax.axis_index("s") * SC_CORES + lax.axis_index("c")
        base = worker * per_worker
        pltpu.sync_copy(idx_hbm.at[pl.ds(base, per_worker)], idx_v)

        def fetch(j, slot):
            off = pl.multiple_of(j * SC_CHUNK, SC_CHUNK)
            return pltpu.async_copy(table_hbm.at[idx_v.at[pl.ds(off, SC_CHUNK)]], rows_v.at[slot],
                                    sems.at[slot])

        def drain(j, slot):
            off = pl.multiple_of(j * SC_CHUNK, SC_CHUNK)
            pltpu.make_async_copy(out_hbm.at[pl.ds(base + off, SC_CHUNK)], rows_v.at[slot],
                                  sems.at[slot]).wait()
            pltpu.sync_copy(rows_v.at[slot], out_hbm.at[pl.ds(base + off, SC_CHUNK)])

        fetch(0, 0)

        def pair(p, carry):
            fetch(2 * p + 1, 1)
            drain(2 * p, 0)

            @pl.when(p + 1 < chunks // 2)
            def _():
                fetch(2 * p + 2, 0)

            drain(2 * p + 1, 1)
            return carry

        lax.fori_loop(0, chunks // 2, pair, 0)

    return gather(table, idx)


def _final_body(x1_ref, route_ref, p_ref, r1_ref, r2_ref, gp_ref, wpg_ref, wpp_ref, gf_ref, o_ref):
    emb = jnp.dot(p_ref[...].astype(_BF16), wpp_ref[...], preferred_element_type=_F32)
    route = route_ref[...]
    x2 = x1_ref[...] + (_unpack_bf16_pair(r1_ref[...]) * route[:, 0:1]
                        + _unpack_bf16_pair(r2_ref[...]) * route[:, 1:2])
    gate = jax.nn.sigmoid(jnp.dot(_rms(x2, gp_ref[...]).astype(_BF16), wpg_ref[...],
                                  preferred_element_type=_F32))
    o_ref[...] = _rms(x2 + gate * emb, gf_ref[...])


def _final(x1, route, p, slot_rows, gp, wpg, wpp, gf, *, tm):
    t, d = x1.shape
    ple = p.shape[1]
    steps = t // tm
    row = lambda cols: pl.BlockSpec((tm, cols), lambda i: (i, 0))
    return pl.pallas_call(
        _final_body,
        grid=(steps,),
        in_specs=[row(d), row(LANES), row(ple), row(d // 2),
                  pl.BlockSpec((tm, d // 2), lambda i: (steps + i, 0)),
                  _const_spec((1, d)), _const_spec(wpg.shape), _const_spec(wpp.shape),
                  _const_spec((1, d))],
        out_specs=row(d),
        out_shape=jax.ShapeDtypeStruct((t, d), _F32),
        compiler_params=_params("arbitrary"),
        name="final",
    )(x1, route, p, slot_rows, slot_rows, gp, wpg, wpp, gf)


def _rope_tables(s):
    half = HEAD_DIM // 2
    inv = 1.0 / (ROPE_THETA ** (jnp.arange(0, HEAD_DIM, 2, dtype=_F32) / HEAD_DIM))
    ang = jnp.arange(s, dtype=_F32)[:, None] * inv[None, :]
    cos, sin = jnp.cos(ang), jnp.sin(ang)
    reps = LANES // half
    cos_t = jnp.tile(cos, (1, reps))
    sin_t = jnp.tile(jnp.concatenate([-sin, sin], axis=1), (1, reps // 2))
    return cos_t, sin_t


def _layer(x, p, attn_norm, w_in, b_forget, w_fox_branch, w_moba_branch, w_out, moe_norm,
           w_group, b_group, w_fine, b_fine, w_gate, w_up, w_down, ple_norm, w_ple_gate, w_ple_proj,
           final_gain):
    b, s, d = x.shape
    t = b * s
    width = N_FOX_HEADS * HEAD_DIM
    assert N_MOBA_HEADS * HEAD_DIM == width and MOBA_BLOCK % ATTN_Q == 0 and MOBA_PIPELINE[0] % MOBA_BLOCK == 0
    assert BIAS_PIECES * N_FOX_HEADS <= LANES and s % MOBA_PIPELINE[0] == 0
    assert s % ROW_TILE == 0 and ROW_TILE % MOBA_BLOCK == 0 and s // MOBA_BLOCK <= LANES
    assert t % FINAL_TILE == 0 and N_EXPERTS + N_GROUPS <= LANES

    qkv_cols = 6 * width
    f_cols = N_FOX_HEADS
    chunk = lambda c: w_in[:, c * width:(c + 1) * width]
    w_rows = jnp.concatenate([chunk(1), chunk(4), w_in[:, qkv_cols + f_cols:]], axis=1).astype(_BF16)
    w_cols = jnp.concatenate([chunk(0), chunk(2), chunk(3), chunk(5)], axis=1).T.astype(_BF16)
    wf = jnp.zeros((d, LANES), _BF16).at[:, :f_cols].set(w_in[:, qkv_cols:qkv_cols + f_cols].astype(_BF16))
    bfp = jnp.zeros((1, LANES), _F32).at[0, :f_cols].set(b_forget.astype(_F32))
    cos, sin = _rope_tables(s)

    qft, kf, vft, qmt, km, vmt, gates, cb, kmean, kabs, bmax = _inproj(
        x, attn_norm.reshape(1, d), w_rows, w_cols, wf, bfp, cos, sin, cos.T, sin.T, tm=ROW_TILE)
    y_fox = _attention(qft, kf, cb, vft, kabs, bmax, mode="fox")
    block_of_key = jnp.arange(s, dtype=jnp.int32)[:, None] // MOBA_BLOCK
    block_onehot = (block_of_key == jnp.arange(LANES, dtype=jnp.int32)[None, :]).astype(_BF16)
    y_moba = _attention(qmt, km, block_onehot, vmt, kmean, bmax, mode="moba")

    wr = jnp.zeros((d, LANES), _BF16)
    wr = wr.at[:, :N_EXPERTS].set(w_fine.astype(_BF16))
    wr = wr.at[:, N_EXPERTS:N_EXPERTS + N_GROUPS].set(w_group.astype(_BF16))
    br = jnp.zeros((1, LANES), _F32)
    br = br.at[0, :N_EXPERTS].set(b_fine.astype(_F32))
    br = br.at[0, N_EXPERTS:N_EXPERTS + N_GROUPS].set(b_group.astype(_F32))
    x1, h2p, route, meta, counts = _postattn(
        y_fox.reshape(t, width), y_moba.reshape(t, width), gates.reshape(t, 2 * d), x.reshape(t, d),
        w_fox_branch.astype(_BF16), w_moba_branch.astype(_BF16), w_out.astype(_BF16),
        moe_norm.reshape(1, d), wr, br, tm=ROW_TILE)

    te = EXPERT_TILE
    n_blk = (2 * t) // te + N_EXPERTS
    counts = counts[:N_EXPERTS, 0].astype(jnp.int32)
    blocks_per = (counts + te - 1) // te
    block_end = jnp.cumsum(blocks_per)
    row_start = (block_end - blocks_per) * te
    expert_ids = jnp.arange(N_EXPERTS, dtype=jnp.int32)[:, None, None]
    slot_major = meta[4:6] + jnp.sum(
        jnp.where(meta[2:4][None] == expert_ids, row_start[:, None, None], 0), axis=0)
    n_used = block_end[-1:].astype(jnp.int32)
    block_ids = jnp.arange(n_blk, dtype=jnp.int32)
    block_expert = jnp.minimum(
        jnp.sum((block_ids[:, None] >= block_end[None, :]).astype(jnp.int32), axis=1), N_EXPERTS - 1)

    first_block = block_end - blocks_per
    block_valid = jnp.clip(counts[block_expert] - (block_ids - first_block[block_expert]) * te, 0, te)
    xs = _sc_scatter(h2p, slot_major.reshape(2, t // SC_SCATTER_CHUNK, SC_SCATTER_CHUNK), n_blk * te)
    rows = _experts(block_expert, block_valid, n_used, xs, w_gate, w_up, w_down, te=te)
    slot_rows = _sc_gather(rows, slot_major.reshape(-1))
    out = _final(x1, route, p.reshape(t, -1), slot_rows, ple_norm.reshape(1, d),
                 w_ple_gate.astype(_BF16), w_ple_proj.astype(_BF16), final_gain.reshape(1, d),
                 tm=FINAL_TILE)
    return out.reshape(b, s, d)


def kernel(x, p, attn_norm, w_in, b_forget, w_fox_branch, w_moba_branch, w_out, moe_norm, w_group,
           b_group, w_fine, b_fine, w_gate, w_up, w_down, ple_norm, w_ple_gate, w_ple_proj, final_norm):
    depth = p.shape[0]
    assert depth == 1, "the final norm is fused into the last layer's kernel"
    i = 0
    return _layer(x, p[i], attn_norm[i], w_in[i], b_forget[i], w_fox_branch[i], w_moba_branch[i],
                  w_out[i], moe_norm[i], w_group[i], b_group[i], w_fine[i], b_fine[i], w_gate[i],
                  w_up[i], w_down[i], ple_norm[i], w_ple_gate[i], w_ple_proj[i], final_norm)
```

```python
import functools

import jax
import jax.numpy as jnp
from jax import lax
from jax.experimental import pallas as pl
from jax.experimental.pallas import tpu as pltpu
from jax.experimental.pallas import tpu_sc as plsc

HEAD_DIM = 64
N_FOX_HEADS = 8
N_MOBA_HEADS = 8
MOBA_BLOCK = 256
MOBA_TOPK = 3
ROPE_THETA = 10000.0
N_GROUPS = 4
EXPERTS_PER_GROUP = 8
N_EXPERTS = N_GROUPS * EXPERTS_PER_GROUP
RMS_EPS = 1e-6

LANES = 128
SUBLANES = 8
BLOCK_LOOP_UNROLL = 8
HEADS_PER_STEP = LANES // HEAD_DIM
ATTN_Q = 256
FOX_PIPELINE = (256, 8, 4)
MOBA_PIPELINE = (256, 8, 4)
ONES_ROWS = 16
SOFTMAX_ROWS = 64
BIAS_PIECES = 3
ROW_TILE = 512
SC_CORES = 2
SC_SUBCORES = 16
SC_CHUNK = 64
SC_SCATTER_CHUNK = 128
FINAL_TILE = 1024
EXPERT_PARTS = 2
EXPERT_TILE = 512
LOG2_E = 1.4426950408889634
UNDERFLOW_LOG2 = 160.0
NEG_BIG = -1e30
VMEM_LIMIT = 48 * 1024 * 1024

_BF16 = jnp.bfloat16
_F32 = jnp.float32


def _params(*sem):
    return pltpu.CompilerParams(dimension_semantics=sem, vmem_limit_bytes=VMEM_LIMIT)


def _rms(x, g):
    return x * lax.rsqrt(jnp.mean(x * x, axis=-1, keepdims=True) + RMS_EPS) * g


def _const_spec(shape):
    return pl.BlockSpec(shape, lambda *_: (0,) * len(shape))


def _inproj_body(x_ref, g_ref, w_ref, wt_ref, wf_ref, bf_ref, cos_ref, sin_ref, cost_ref, sint_ref,
                 qft_ref, kf_ref, vft_ref, qmt_ref, km_ref, vmt_ref, gates_ref, cb_ref, kmean_ref,
                 kabs_ref, bmax_ref,
                 carry_ref, *, tm, width, gate_cols):
    j = pl.program_id(1)
    h = _rms(x_ref[...], g_ref[...])
    hb = h.astype(_BF16)
    hbt = h.T.astype(_BF16)
    reps = width // LANES
    half = HEAD_DIM // 2

    blocks = tm // MOBA_BLOCK

    @pl.when(j == 0)
    def _():
        kmean_ref[...] = jnp.zeros_like(kmean_ref)
        kabs_ref[...] = jnp.zeros_like(kabs_ref)
        bmax_ref[...] = jnp.zeros_like(bmax_ref)
        carry_ref[...] = jnp.zeros_like(carry_ref)

    z = jnp.dot(hb, wf_ref[...], preferred_element_type=_F32) + bf_ref[...]
    log_f = jnp.minimum(z, 0.0) - jnp.log1p(jnp.exp(-jnp.abs(z)))
    r = lax.broadcasted_iota(jnp.int32, (LANES, LANES), 0)
    c = lax.broadcasted_iota(jnp.int32, (LANES, LANES), 1)
    tri = (c <= r).astype(_BF16)
    offset = carry_ref[0:1, :]
    blocks_cum = []
    for blk in range(tm // LANES):
        rest = log_f[blk * LANES:(blk + 1) * LANES]
        within = jnp.zeros((LANES, LANES), _F32)
        for _ in range(BIAS_PIECES):
            part = rest.astype(_BF16)
            rest = rest - part.astype(_F32)
            within = within + jnp.dot(tri, part, preferred_element_type=_F32)
        blocks_cum.append(within + offset)
        offset = offset + within[LANES - 1:LANES, :]
    cum = jnp.concatenate(blocks_cum, axis=0)
    carry_ref[...] = jnp.broadcast_to(offset, carry_ref.shape)

    head_lane = lax.broadcasted_iota(jnp.int32, (1, LANES), 1) < N_FOX_HEADS
    bias = jnp.where(head_lane, -cum * LOG2_E, 0.0)
    for b in range(blocks):
        bmax_ref[pl.ds(j * blocks + b, 1), :] = jnp.max(
            bias[b * MOBA_BLOCK:(b + 1) * MOBA_BLOCK], axis=0, keepdims=True)
    rest = bias
    placed = jnp.zeros((tm, LANES), _F32)
    for piece in range(BIAS_PIECES):
        part = rest.astype(_BF16).astype(_F32)
        rest = rest - part
        placed = placed + (pltpu.roll(part, N_FOX_HEADS * piece, 1) if piece else part)
    cb_ref[...] = placed.astype(_BF16)

    def proj(c):
        return jnp.dot(hb, w_ref[:, c * width:(c + 1) * width], preferred_element_type=_F32)

    def proj_t(c):
        return jnp.dot(wt_ref[c * width:(c + 1) * width, :], hbt, preferred_element_type=_F32)

    def rope(t):
        cos = jnp.concatenate([cos_ref[...]] * reps, axis=1)
        sin = jnp.concatenate([sin_ref[...]] * reps, axis=1)
        first = (lax.broadcasted_iota(jnp.int32, (1, width), 1) % HEAD_DIM) < half
        partner = jnp.where(first, pltpu.roll(t, width - half, 1), pltpu.roll(t, half, 1))
        return t * cos + partner * sin

    def rope_t(t):
        cos = jnp.concatenate([cost_ref[...]] * reps, axis=0)
        sin = jnp.concatenate([sint_ref[...]] * reps, axis=0)
        first = (lax.broadcasted_iota(jnp.int32, (width, 1), 0) % HEAD_DIM) < half
        partner = jnp.where(first, pltpu.roll(t, width - half, 0), pltpu.roll(t, half, 0))
        return t * cos + partner * sin

    scale = HEAD_DIM ** -0.5 * LOG2_E
    kf = proj(0).astype(_BF16)
    kf_ref[...] = kf
    kf_abs = jnp.abs(kf.astype(_F32))
    km = rope(proj(1))
    km_ref[...] = km.astype(_BF16)

    for b in range(blocks):
        rows = slice(b * MOBA_BLOCK, (b + 1) * MOBA_BLOCK)
        kmean_ref[pl.ds(j * blocks + b, 1), :] = jnp.mean(km[rows], axis=0, keepdims=True)
        kabs_ref[pl.ds(j * blocks + b, 1), :] = jnp.max(kf_abs[rows], axis=0, keepdims=True)

    for c in range(gate_cols // width):
        g = jnp.dot(hb, w_ref[:, (2 + c) * width:(3 + c) * width],
                    preferred_element_type=_F32)
        gates_ref[:, c * width:(c + 1) * width] = jax.nn.sigmoid(g).astype(_BF16)

    qft_ref[...] = (proj_t(0) * scale).astype(_BF16)
    vft_ref[...] = proj_t(1).astype(_BF16)
    qmt_ref[...] = (rope_t(proj_t(2)) * scale).astype(_BF16)
    vmt_ref[...] = proj_t(3).astype(_BF16)


def _inproj(x, g, w, wt, wf, bfp, cos, sin, cos_t, sin_t, *, tm):
    b, s, d = x.shape
    width = N_FOX_HEADS * HEAD_DIM
    gate_cols = w.shape[1] - 2 * width
    act = jax.ShapeDtypeStruct((b, s, width), _BF16)
    act_t = jax.ShapeDtypeStruct((b, width, s), _BF16)
    act_spec = pl.BlockSpec((None, tm, width), lambda bi, j: (bi, j, 0))
    act_t_spec = pl.BlockSpec((None, width, tm), lambda bi, j: (bi, 0, j))
    return pl.pallas_call(
        functools.partial(_inproj_body, tm=tm, width=width, gate_cols=gate_cols),
        grid=(b, s // tm),
        in_specs=[
            pl.BlockSpec((None, tm, d), lambda bi, j: (bi, j, 0)),
            _const_spec((1, d)),
            _const_spec(w.shape),
            _const_spec(wt.shape),
            _const_spec(wf.shape),
            _const_spec(bfp.shape),
            pl.BlockSpec((tm, LANES), lambda bi, j: (j, 0)),
            pl.BlockSpec((tm, LANES), lambda bi, j: (j, 0)),
            pl.BlockSpec((LANES, tm), lambda bi, j: (0, j)),
            pl.BlockSpec((LANES, tm), lambda bi, j: (0, j)),
        ],
        out_specs=[act_t_spec, act_spec, act_t_spec, act_t_spec, act_spec, act_t_spec,
                   pl.BlockSpec((None, tm, gate_cols), lambda bi, j: (bi, j, 0)),
                   pl.BlockSpec((None, tm, LANES), lambda bi, j: (bi, j, 0)),
                   pl.BlockSpec((None, LANES, width), lambda bi, j: (bi, 0, 0)),
                   pl.BlockSpec((None, LANES, width), lambda bi, j: (bi, 0, 0)),
                   pl.BlockSpec((None, LANES, LANES), lambda bi, j: (bi, 0, 0))],
        out_shape=[act_t, act, act_t, act_t, act, act_t,
                   jax.ShapeDtypeStruct((b, s, gate_cols), _BF16),
                   jax.ShapeDtypeStruct((b, s, LANES), _BF16),
                   jax.ShapeDtypeStruct((b, LANES, width), _F32),
                   jax.ShapeDtypeStruct((b, LANES, width), _F32),
                   jax.ShapeDtypeStruct((b, LANES, LANES), _F32)],
        scratch_shapes=[pltpu.VMEM((SUBLANES, LANES), _F32)],
        compiler_params=_params("arbitrary", "arbitrary"),
        name="inproj",
    )(x, g, w, wt, wf, bfp, cos, sin, cos_t, sin_t)


def _attn_body(uq_ref, uk_ref, qt_ref, k_ref, kx_ref, vt_ref, aux_ref, aux2_ref, tri_ref, o_ref,
               w_ref, m_ref, l_ref, acc_ref, st_ref, p_ref, alpha_ref, pmax_ref, qn_ref, *,
               mode, tq, tk, nb, n_diag, n_units, pipe, lag):
    hp = pl.program_id(1)
    wide = HEADS_PER_STEP * tq
    nq = w_ref.shape[0]
    feat = lax.broadcasted_iota(jnp.int32, (LANES, 1), 0)
    xr = lax.broadcasted_iota(jnp.int32, (LANES, wide), 0)
    xc = lax.broadcasted_iota(jnp.int32, (LANES, wide), 1)

    def setup(i, carry):
        qt = qt_ref[:, pl.ds(pl.multiple_of(i * tq, tq), tq)]
        zero = jnp.zeros_like(qt)
        wq = jnp.concatenate([jnp.where(feat < HEAD_DIM, qt, zero),
                              jnp.where(feat < HEAD_DIM, zero, qt)], axis=1)
        if mode == "fox":
            head = hp * HEADS_PER_STEP + xc // tq
            lower = ((xr % N_FOX_HEADS == head) & (xr < BIAS_PIECES * N_FOX_HEADS)).astype(_BF16)
            qn_ref[i] = jnp.broadcast_to(
                jnp.sum(jnp.abs(wq.astype(_F32)), axis=0, keepdims=True), qn_ref.shape[1:])
        else:
            own = (i * tq) // MOBA_BLOCK
            gate = jnp.dot(aux_ref[...].astype(_BF16), wq, preferred_element_type=_F32)[0:nb]
            blk = lax.broadcasted_iota(jnp.int32, (nb, wide), 0)
            gate = jnp.where(blk < own, gate, -jnp.inf)
            keep = blk == own
            for _ in range(MOBA_TOPK):
                mx = jnp.max(gate, axis=0, keepdims=True)
                cand = jnp.where((gate == mx) & (mx > -jnp.inf), blk, nb)
                pick = blk == jnp.min(cand, axis=0, keepdims=True)
                keep = keep | pick
                gate = jnp.where(pick, -jnp.inf, gate)
            lower = jnp.where(keep, 0.0, NEG_BIG)
            if nb < LANES:
                lower = jnp.concatenate([lower, jnp.zeros((LANES - nb, wide), _F32)], axis=0)
            lower = lower.astype(_BF16)
        w_ref[i] = jnp.concatenate([wq, lower], axis=0)
        m_ref[i] = jnp.full(m_ref.shape[1:], NEG_BIG, _F32)
        l_ref[i] = jnp.zeros(l_ref.shape[1:], _F32)
        acc_ref[i] = jnp.zeros(acc_ref.shape[1:], _F32)
        return carry

    lax.fori_loop(0, nq, setup, 0, unroll=BLOCK_LOOP_UNROLL)

    def unit(t):
        return uq_ref[t], uk_ref[t]

    def scores(t, slot, causal):
        q, kb = unit(t)
        start = pl.multiple_of(kb * tk, tk)
        ka = jnp.concatenate([k_ref[pl.ds(start, tk), :], kx_ref[pl.ds(start, tk), :]], axis=1)
        st = jnp.dot(ka, w_ref[q], preferred_element_type=_F32)
        if causal:
            st = st + tri_ref[...]
        st_ref[slot] = st
        pmax_ref[slot] = jnp.max(st.reshape(tk // SUBLANES, SUBLANES, wide), axis=0)

    def softmax(t, slot):
        q, _ = unit(t)
        rows = SOFTMAX_ROWS
        m_prev = m_ref[q][0:1, :]
        m_new = jnp.maximum(m_prev, jnp.max(pmax_ref[slot], axis=0, keepdims=True))
        for r in range(0, tk, rows):
            p_ref[slot, r:r + rows, :] = jnp.exp2(st_ref[slot, r:r + rows, :] - m_new).astype(_BF16)
        m_ref[q] = jnp.broadcast_to(m_new, m_ref.shape[1:])
        alpha_ref[slot] = jnp.broadcast_to(jnp.exp2(m_prev - m_new), alpha_ref.shape[1:])

    def values(t, slot):
        q, kb = unit(t)
        start = pl.multiple_of(kb * tk, tk)
        ones = jnp.ones((ONES_ROWS, tk), _BF16)
        for h in range(HEADS_PER_STEP):
            cols = slice(h * tq, (h + 1) * tq)
            feats = slice(h * HEAD_DIM, (h + 1) * HEAD_DIM)
            lhs = jnp.concatenate([vt_ref[feats, pl.ds(start, tk)], ones], axis=0)
            pv = jnp.dot(lhs, p_ref[slot, :, cols], preferred_element_type=_F32)
            alpha = alpha_ref[slot, 0:1, cols]
            acc_ref[q, feats, :] = acc_ref[q, feats, :] * alpha + pv[0:HEAD_DIM]
            l_ref[q, :, cols] = jnp.broadcast_to(
                alpha * l_ref[q, 0:1, cols] + pv[HEAD_DIM:HEAD_DIM + 1], (SUBLANES, tq))

    def step(base, c, causal, do_scores=True, do_softmax=True, do_values=True):
        u = base + c
        if do_scores:
            scores(u, c % pipe, causal)
        if do_values:
            values(u - 2 * lag, (c - 2 * lag) % pipe)
        if do_softmax:
            softmax(u - lag, (c - lag) % pipe)

    def group(causal):
        def body(j, carry):
            for c in range(pipe):
                step(j * pipe, c, causal)
            return carry
        return body

    def pipeline(g_first, g_end, causal_groups):
        for c in range(pipe):
            step(g_first * pipe, c, g_first < causal_groups,
                 do_softmax=c >= lag, do_values=c >= 2 * lag)
        if g_first + 1 < causal_groups:
            lax.fori_loop(g_first + 1, causal_groups, group(True), 0)
        lax.fori_loop(max(g_first + 1, causal_groups), g_end, group(False), 0)
        for c in range(2 * lag):
            step(g_end * pipe, c, False, do_scores=False, do_softmax=c < lag)

    diag_groups = n_diag // pipe
    if mode == "moba":
        pipeline(0, n_units // pipe, diag_groups)
    else:
        pipeline(0, diag_groups, diag_groups)
        nq_blocks = n_diag
        col = lax.broadcasted_iota(jnp.int32, (LANES, LANES), 1)
        first_head = lax.broadcasted_iota(jnp.int32, (1, wide), 1) < tq
        head0 = hp * HEADS_PER_STEP

        def per_head(table, pick0, pick1, fill):
            v0 = jnp.max(jnp.where(pick0, table, fill), axis=1, keepdims=True)[0:nb]
            v1 = jnp.max(jnp.where(pick1, table, fill), axis=1, keepdims=True)[0:nb]
            return jnp.where(first_head, v0, v1)

        bias_max = per_head(aux2_ref[...], col == head0, col == head0 + 1, NEG_BIG)
        k_absmax = per_head(aux_ref[...], col < HEAD_DIM, col >= HEAD_DIM, 0.0)
        blk = lax.broadcasted_iota(jnp.int32, (nb, 1), 0)

        needed = jnp.zeros((nb, 1), jnp.int32)
        for i in range(1, nq_blocks):
            bound = bias_max + qn_ref[i, 0:1, :] * k_absmax - m_ref[i, 0:1, :]
            live = jnp.max(bound, axis=1, keepdims=True) >= -UNDERFLOW_LOG2
            needed = jnp.maximum(needed, jnp.where(live & (blk < i), i - blk, 0))
        far = jnp.max(needed)
        n_live = far * nq_blocks - (far * (far + 1)) // 2
        live_groups = (n_live + pipe - 1) // pipe

        @pl.when(live_groups > 0)
        def _():
            pipeline(diag_groups, diag_groups + live_groups, diag_groups)

    def finish(i, carry):
        out_t = jnp.concatenate(
            [acc_ref[i, h * HEAD_DIM:(h + 1) * HEAD_DIM, :] / l_ref[i, 0:1, h * tq:(h + 1) * tq]
             for h in range(HEADS_PER_STEP)], axis=0)
        o_ref[pl.ds(pl.multiple_of(i * tq, tq), tq), :] = out_t.T.astype(o_ref.dtype)
        return carry

    lax.fori_loop(0, nq, finish, 0, unroll=BLOCK_LOOP_UNROLL)


def _attention(qt, k, kx, vt, aux, aux2, *, mode):
    b, width, s = qt.shape
    tq = ATTN_Q
    tk, pipe, lag = FOX_PIPELINE if mode == "fox" else MOBA_PIPELINE
    nq = s // tq
    nb = s // MOBA_BLOCK
    wide = HEADS_PER_STEP * tq
    diag = [(i * tq) // tk for i in range(nq)]
    units = list(enumerate(diag))
    units += [(i, diag[i] - dist) for dist in range(1, nq) for i in range(nq) if diag[i] >= dist]
    assert mode != "fox" or tq == tk
    assert nq % pipe == 0 and len(units) % pipe == 0 and pipe >= 2 * lag
    key_row = jnp.arange(tk, dtype=jnp.int32)[:, None]
    qry_col = jnp.arange(wide, dtype=jnp.int32)[None, :] % tq
    causal_bias = jnp.where(key_row <= qry_col, 0.0, NEG_BIG).astype(_F32)
    unit_q = jnp.asarray([u[0] for u in units], jnp.int32)
    unit_k = jnp.asarray([u[1] for u in units], jnp.int32)
    aux_spec = pl.BlockSpec((None, LANES, LANES), lambda bi, hp, uq, uk: (bi, 0, hp))
    aux2_spec = pl.BlockSpec((None, LANES, LANES), lambda bi, hp, uq, uk: (bi, 0, 0))
    if mode == "fox":
        kx_spec = pl.BlockSpec((None, s, LANES), lambda bi, hp, uq, uk: (bi, 0, 0))
    else:
        kx_spec = pl.BlockSpec(kx.shape, lambda bi, hp, uq, uk: (0, 0))
    return pl.pallas_call(
        functools.partial(_attn_body, mode=mode, tq=tq, tk=tk, nb=nb, n_diag=nq, n_units=len(units),
                          pipe=pipe, lag=lag),
        grid_spec=pltpu.PrefetchScalarGridSpec(
            num_scalar_prefetch=2,
            grid=(b, width // LANES),
            in_specs=[
                pl.BlockSpec((None, LANES, s), lambda bi, hp, uq, uk: (bi, hp, 0)),
                pl.BlockSpec((None, s, LANES), lambda bi, hp, uq, uk: (bi, 0, hp)),
                kx_spec,
                pl.BlockSpec((None, LANES, s), lambda bi, hp, uq, uk: (bi, hp, 0)),
                aux_spec,
                aux2_spec,
                pl.BlockSpec((tk, wide), lambda bi, hp, uq, uk: (0, 0)),
            ],
            out_specs=pl.BlockSpec((None, s, LANES), lambda bi, hp, uq, uk: (bi, 0, hp)),
            scratch_shapes=[
                pltpu.VMEM((nq, 2 * LANES, wide), _BF16),
                pltpu.VMEM((nq, SUBLANES, wide), _F32),
                pltpu.VMEM((nq, SUBLANES, wide), _F32),
                pltpu.VMEM((nq, LANES, tq), _F32),
                pltpu.VMEM((pipe, tk, wide), _F32),
                pltpu.VMEM((pipe, tk, wide), _BF16),
                pltpu.VMEM((pipe, SUBLANES, wide), _F32),
                pltpu.VMEM((pipe, SUBLANES, wide), _F32),
                pltpu.VMEM((nq, SUBLANES, wide), _F32),
            ],
        ),
        out_shape=jax.ShapeDtypeStruct((b, s, width), _BF16),
        compiler_params=_params("arbitrary", "arbitrary"),
        name=mode,
    )(unit_q, unit_k, qt, k, kx, vt, aux, aux2, causal_bias)


def _pack_bf16_pair(a, b):
    lo = pltpu.bitcast(a.astype(_BF16).astype(_F32), jnp.uint32) >> 16
    hi = pltpu.bitcast(b.astype(_BF16).astype(_F32), jnp.uint32) & jnp.uint32(0xFFFF0000)
    return lo | hi


def _unpack_bf16_pair(u):
    lo = pltpu.bitcast(u << 16, _F32)
    hi = pltpu.bitcast(u & jnp.uint32(0xFFFF0000), _F32)
    return jnp.concatenate([lo, hi], axis=1)


def _postattn_body(yf_ref, ym_ref, gates_ref, x_ref, wfb_ref, wmb_ref, wout_ref, g_ref, wr_ref, br_ref,
                   x1_ref, h2_ref, route_ref, meta_ref, counts_ref, carry_ref, *, tm, d):
    step = pl.program_id(0)
    ya = jnp.dot(yf_ref[...], wfb_ref[...], preferred_element_type=_F32)
    yb = jnp.dot(ym_ref[...], wmb_ref[...], preferred_element_type=_F32)
    mixed = gates_ref[:, 0:d].astype(_F32) * ya + gates_ref[:, d:2 * d].astype(_F32) * yb
    x1 = x_ref[...] + jnp.dot(mixed.astype(_BF16), wout_ref[...], preferred_element_type=_F32)
    x1_ref[...] = x1
    h2 = _rms(x1, g_ref[...])
    h2_ref[...] = _pack_bf16_pair(h2[:, 0:d // 2], h2[:, d // 2:d])

    logits = jnp.dot(h2.astype(_BF16), wr_ref[...], preferred_element_type=_F32) + br_ref[...]
    logits_t = logits.T
    gl = logits_t[N_EXPERTS:N_EXPERTS + N_GROUPS]
    ge = jnp.exp(gl - jnp.max(gl, axis=0, keepdims=True))
    gp = ge / jnp.sum(ge, axis=0, keepdims=True)
    g_top = jnp.max(gp, axis=0, keepdims=True)
    g_row = lax.broadcasted_iota(jnp.int32, (N_GROUPS, tm), 0)
    g_idx = jnp.min(jnp.where(gp == g_top, g_row, N_GROUPS), axis=0, keepdims=True)
    fl = logits_t[0:EXPERTS_PER_GROUP]
    for g in range(1, N_GROUPS):
        fl = jnp.where(g_idx == g, logits_t[g * EXPERTS_PER_GROUP:(g + 1) * EXPERTS_PER_GROUP], fl)
    f_row = lax.broadcasted_iota(jnp.int32, (EXPERTS_PER_GROUP, tm), 0)
    f1 = jnp.max(fl, axis=0, keepdims=True)
    i1 = jnp.min(jnp.where(fl == f1, f_row, EXPERTS_PER_GROUP), axis=0, keepdims=True)
    fl2 = jnp.where(f_row == i1, -jnp.inf, fl)
    f2 = jnp.max(fl2, axis=0, keepdims=True)
    i2 = jnp.min(jnp.where(fl2 == f2, f_row, EXPERTS_PER_GROUP), axis=0, keepdims=True)
    e1 = g_idx * EXPERTS_PER_GROUP + i1
    e2 = g_idx * EXPERTS_PER_GROUP + i2
    t2 = jnp.exp(f2 - f1)
    w1 = g_top * (1.0 / (1.0 + t2))
    w2 = g_top * (t2 / (1.0 + t2))

    @pl.when(step == 0)
    def _():
        carry_ref[...] = jnp.zeros_like(carry_ref)

    x_row = lax.broadcasted_iota(jnp.int32, (N_EXPERTS, tm), 0)
    onehot = ((x_row == e1) | (x_row == e2)).astype(_F32)
    rr = lax.broadcasted_iota(jnp.int32, (tm, tm), 0)
    cc = lax.broadcasted_iota(jnp.int32, (tm, tm), 1)
    before = jnp.dot(onehot.astype(_BF16), (rr < cc).astype(_BF16), preferred_element_type=_F32)
    before = before + carry_ref[0:N_EXPERTS, 0:1]
    r1 = jnp.sum(jnp.where(x_row == e1, before, 0.0), axis=0, keepdims=True)
    r2 = jnp.sum(jnp.where(x_row == e2, before, 0.0), axis=0, keepdims=True)
    total = carry_ref[0:N_EXPERTS, 0:1] + jnp.sum(onehot, axis=1, keepdims=True)
    carry_ref[0:N_EXPERTS, :] = jnp.broadcast_to(total, (N_EXPERTS, LANES))
    counts_ref[...] = carry_ref[...]

    m_row = lax.broadcasted_iota(jnp.int32, (SUBLANES, tm), 0)
    meta = jnp.where(m_row == 2, e1, 0)
    meta = jnp.where(m_row == 3, e2, meta)
    meta = jnp.where(m_row == 4, r1.astype(jnp.int32), meta)
    meta = jnp.where(m_row == 5, r2.astype(jnp.int32), meta)
    meta_ref[...] = meta
    w_row = lax.broadcasted_iota(jnp.int32, (LANES, tm), 0)
    route_ref[...] = jnp.where(w_row == 0, w1, jnp.where(w_row == 1, w2, 0.0)).T


def _postattn(yf, ym, gates, x, wfb, wmb, wout, g, wr, br, *, tm):
    t, d = x.shape
    width = yf.shape[1]
    row = lambda cols: pl.BlockSpec((tm, cols), lambda i: (i, 0))
    return pl.pallas_call(
        functools.partial(_postattn_body, tm=tm, d=d),
        grid=(t // tm,),
        in_specs=[row(width), row(width), row(2 * d), row(d),
                  _const_spec(wfb.shape), _const_spec(wmb.shape), _const_spec(wout.shape),
                  _const_spec((1, d)), _const_spec(wr.shape), _const_spec(br.shape)],
        out_specs=[row(d), row(d // 2), row(LANES), pl.BlockSpec((SUBLANES, tm), lambda i: (0, i)),
                   _const_spec((LANES, LANES))],
        out_shape=[jax.ShapeDtypeStruct((t, d), _F32),
                   jax.ShapeDtypeStruct((t, d // 2), jnp.uint32),
                   jax.ShapeDtypeStruct((t, LANES), _F32),
                   jax.ShapeDtypeStruct((SUBLANES, t), jnp.int32),
                   jax.ShapeDtypeStruct((LANES, LANES), _F32)],
        scratch_shapes=[pltpu.VMEM((LANES, LANES), _F32)],
        compiler_params=_params("arbitrary"),
        name="postattn",
    )(yf, ym, gates, x, wfb, wmb, wout, g, wr, br)


def _sc_scatter(rows, idx, n_out):
    n, width = rows.shape
    workers = SC_CORES * SC_SUBCORES
    per_worker = n // workers
    chunks = per_worker // SC_SCATTER_CHUNK
    assert n % (workers * SC_SCATTER_CHUNK) == 0 and idx.shape == (2, n // SC_SCATTER_CHUNK, SC_SCATTER_CHUNK)
    mesh = plsc.VectorSubcoreMesh(core_axis_name="c", subcore_axis_name="s",
                                  num_cores=SC_CORES, num_subcores=SC_SUBCORES)

    @functools.partial(
        pl.kernel, out_type=jax.ShapeDtypeStruct((n_out, width), rows.dtype), mesh=mesh,
        scratch_types=[pltpu.VMEM((2, chunks, SC_SCATTER_CHUNK), jnp.int32),
                       pltpu.VMEM((SC_SCATTER_CHUNK, width), rows.dtype),
                       pltpu.SemaphoreType.DMA],
        name="sc_scatter")
    def scatter(rows_hbm, idx_hbm, out_hbm, idx_v, rows_v, sem):
        worker = lax.axis_index("s") * SC_CORES + lax.axis_index("c")
        for k in range(2):
            pltpu.sync_copy(idx_hbm.at[k, pl.ds(worker * chunks, chunks)], idx_v.at[k])

        def chunk(j, carry):
            start = worker * per_worker + j * SC_SCATTER_CHUNK
            pltpu.sync_copy(rows_hbm.at[pl.ds(start, SC_SCATTER_CHUNK)], rows_v)
            for k in range(2):
                pltpu.async_copy(rows_v, out_hbm.at[idx_v.at[k, j]], sem).wait()
            return carry

        lax.fori_loop(0, chunks, chunk, 0)

    return scatter(rows, idx)


def _experts_body(be_ref, valid_ref, nused_ref, xs_ref, wg_ref, wu_ref, wd_ref, out_ref,
                  wgb_ref, wub_ref, wdb_ref):
    j = pl.program_id(0)
    used = j < nused_ref[0]

    @pl.when(jnp.logical_not(used))
    def _():
        out_ref[...] = jnp.zeros_like(out_ref)

    @pl.when(used & ((j == 0) | (be_ref[j] != be_ref[jnp.maximum(j - 1, 0)])))
    def _():
        wgb_ref[...] = wg_ref[...].astype(_BF16)
        wub_ref[...] = wu_ref[...].astype(_BF16)
        wdb_ref[...] = wd_ref[...].astype(_BF16)

    @pl.when(used)
    def _():
        row = lax.broadcasted_iota(jnp.int32, xs_ref.shape, 0)
        packed = jnp.where(row < valid_ref[j], xs_ref[...], jnp.uint32(0))
        rows = _unpack_bf16_pair(packed).astype(_BF16)
        hidden = wgb_ref.shape[1]
        out = None
        for c in range(0, hidden, hidden // EXPERT_PARTS):
            cols = slice(c, c + hidden // EXPERT_PARTS)
            gate = jnp.dot(rows, wgb_ref[:, cols], preferred_element_type=_F32)
            up = jnp.dot(rows, wub_ref[:, cols], preferred_element_type=_F32)
            act = ((gate * jax.nn.sigmoid(gate)) * up).astype(_BF16)
            part = jnp.dot(act, wdb_ref[cols, :], preferred_element_type=_F32)
            out = part if out is None else out + part
        half = out.shape[1] // 2
        out_ref[...] = _pack_bf16_pair(out[:, 0:half], out[:, half:])


def _experts(block_expert, block_valid, n_used, xs, wg, wu, wd, *, te):
    n_rows, half = xs.shape
    _, d, de = wg.shape

    def blk(j, be, bv, nu):
        return jnp.minimum(j, nu[0] - 1)

    def weight(j, be, bv, nu):
        return (be[blk(j, be, bv, nu)], 0, 0)

    return pl.pallas_call(
        _experts_body,
        grid_spec=pltpu.PrefetchScalarGridSpec(
            num_scalar_prefetch=3,
            grid=(n_rows // te,),
            in_specs=[pl.BlockSpec((te, half), lambda j, be, bv, nu: (blk(j, be, bv, nu), 0)),
                      pl.BlockSpec((None, d, de), weight),
                      pl.BlockSpec((None, d, de), weight),
                      pl.BlockSpec((None, de, d), weight)],
            out_specs=pl.BlockSpec((te, d // 2), lambda j, be, bv, nu: (j, 0)),
            scratch_shapes=[pltpu.VMEM((d, de), _BF16), pltpu.VMEM((d, de), _BF16),
                            pltpu.VMEM((de, d), _BF16)],
        ),
        out_shape=jax.ShapeDtypeStruct((n_rows, d // 2), jnp.uint32),
        compiler_params=_params("arbitrary"),
        name="experts",
    )(block_expert, block_valid, n_used, xs, wg, wu, wd)


def _sc_gather(table, idx):
    n, = idx.shape
    width = table.shape[1]
    workers = SC_CORES * SC_SUBCORES
    per_worker = n // workers
    chunks = per_worker // SC_CHUNK
    assert n % (workers * SC_CHUNK * 2) == 0
    mesh = plsc.VectorSubcoreMesh(core_axis_name="c", subcore_axis_name="s",
                                  num_cores=SC_CORES, num_subcores=SC_SUBCORES)

    @functools.partial(
        pl.kernel, out_type=jax.ShapeDtypeStruct((n, width), table.dtype), mesh=mesh,
        scratch_types=[pltpu.VMEM((per_worker,), jnp.int32),
                       pltpu.VMEM((2, SC_CHUNK, width), table.dtype),
                       pltpu.SemaphoreType.DMA((2,))],
        name="sc_gather")
    def gather(table_hbm, idx_hbm, out_hbm, idx_v, rows_v, sems):
        worker = lax.axis_index("s") * SC_CORES + lax.axis_index("c")
        base = worker * per_worker
        pltpu.sync_copy(idx_hbm.at[pl.ds(base, per_worker)], idx_v)

        def fetch(j, slot):
            off = pl.multiple_of(j * SC_CHUNK, SC_CHUNK)
            return pltpu.async_copy(table_hbm.at[idx_v.at[pl.ds(off, SC_CHUNK)]], rows_v.at[slot],
                                    sems.at[slot])

        def drain(j, slot):
            off = pl.multiple_of(j * SC_CHUNK, SC_CHUNK)
            pltpu.make_async_copy(out_hbm.at[pl.ds(base + off, SC_CHUNK)], rows_v.at[slot],
                                  sems.at[slot]).wait()
            pltpu.sync_copy(rows_v.at[slot], out_hbm.at[pl.ds(base + off, SC_CHUNK)])

        fetch(0, 0)

        def pair(p, carry):
            fetch(2 * p + 1, 1)
            drain(2 * p, 0)

            @pl.when(p + 1 < chunks // 2)
            def _():
                fetch(2 * p + 2, 0)

            drain(2 * p + 1, 1)
            return carry

        lax.fori_loop(0, chunks // 2, pair, 0)

    return gather(table, idx)


def _final_body(x1_ref, route_ref, p_ref, r1_ref, r2_ref, gp_ref, wpg_ref, wpp_ref, gf_ref, o_ref):
    emb = jnp.dot(p_ref[...].astype(_BF16), wpp_ref[...], preferred_element_type=_F32)
    route = route_ref[...]
    x2 = x1_ref[...] + (_unpack_bf16_pair(r1_ref[...]) * route[:, 0:1]
                        + _unpack_bf16_pair(r2_ref[...]) * route[:, 1:2])
    gate = jax.nn.sigmoid(jnp.dot(_rms(x2, gp_ref[...]).astype(_BF16), wpg_ref[...],
                                  preferred_element_type=_F32))
    o_ref[...] = _rms(x2 + gate * emb, gf_ref[...])


def _final(x1, route, p, slot_rows, gp, wpg, wpp, gf, *, tm):
    t, d = x1.shape
    ple = p.shape[1]
    steps = t // tm
    row = lambda cols: pl.BlockSpec((tm, cols), lambda i: (i, 0))
    return pl.pallas_call(
        _final_body,
        grid=(steps,),
        in_specs=[row(d), row(LANES), row(ple), row(d // 2),
                  pl.BlockSpec((tm, d // 2), lambda i: (steps + i, 0)),
                  _const_spec((1, d)), _const_spec(wpg.shape), _const_spec(wpp.shape),
                  _const_spec((1, d))],
        out_specs=row(d),
        out_shape=jax.ShapeDtypeStruct((t, d), _F32),
        compiler_params=_params("arbitrary"),
        name="final",
    )(x1, route, p, slot_rows, slot_rows, gp, wpg, wpp, gf)


def _rope_tables(s):
    half = HEAD_DIM // 2
    inv = 1.0 / (ROPE_THETA ** (jnp.arange(0, HEAD_DIM, 2, dtype=_F32) / HEAD_DIM))
    ang = jnp.arange(s, dtype=_F32)[:, None] * inv[None, :]
    cos, sin = jnp.cos(ang), jnp.sin(ang)
    reps = LANES // half
    cos_t = jnp.tile(cos, (1, reps))
    sin_t = jnp.tile(jnp.concatenate([-sin, sin], axis=1), (1, reps // 2))
    return cos_t, sin_t


def _layer(x, p, attn_norm, w_in, b_forget, w_fox_branch, w_moba_branch, w_out, moe_norm,
           w_group, b_group, w_fine, b_fine, w_gate, w_up, w_down, ple_norm, w_ple_gate, w_ple_proj,
           final_gain):
    b, s, d = x.shape
    t = b * s
    width = N_FOX_HEADS * HEAD_DIM
    assert N_MOBA_HEADS * HEAD_DIM == width and MOBA_BLOCK % ATTN_Q == 0 and MOBA_PIPELINE[0] % MOBA_BLOCK == 0
    assert BIAS_PIECES * N_FOX_HEADS <= LANES and s % MOBA_PIPELINE[0] == 0
    assert s % ROW_TILE == 0 and ROW_TILE % MOBA_BLOCK == 0 and s // MOBA_BLOCK <= LANES
    assert t % FINAL_TILE == 0 and N_EXPERTS + N_GROUPS <= LANES

    qkv_cols = 6 * width
    f_cols = N_FOX_HEADS
    chunk = lambda c: w_in[:, c * width:(c + 1) * width]
    w_rows = jnp.concatenate([chunk(1), chunk(4), w_in[:, qkv_cols + f_cols:]], axis=1).astype(_BF16)
    w_cols = jnp.concatenate([chunk(0), chunk(2), chunk(3), chunk(5)], axis=1).T.astype(_BF16)
    wf = jnp.zeros((d, LANES), _BF16).at[:, :f_cols].set(w_in[:, qkv_cols:qkv_cols + f_cols].astype(_BF16))
    bfp = jnp.zeros((1, LANES), _F32).at[0, :f_cols].set(b_forget.astype(_F32))
    cos, sin = _rope_tables(s)

    qft, kf, vft, qmt, km, vmt, gates, cb, kmean, kabs, bmax = _inproj(
        x, attn_norm.reshape(1, d), w_rows, w_cols, wf, bfp, cos, sin, cos.T, sin.T, tm=ROW_TILE)
    y_fox = _attention(qft, kf, cb, vft, kabs, bmax, mode="fox")
    block_of_key = jnp.arange(s, dtype=jnp.int32)[:, None] // MOBA_BLOCK
    block_onehot = (block_of_key == jnp.arange(LANES, dtype=jnp.int32)[None, :]).astype(_BF16)
    y_moba = _attention(qmt, km, block_onehot, vmt, kmean, bmax, mode="moba")

    wr = jnp.zeros((d, LANES), _BF16)
    wr = wr.at[:, :N_EXPERTS].set(w_fine.astype(_BF16))
    wr = wr.at[:, N_EXPERTS:N_EXPERTS + N_GROUPS].set(w_group.astype(_BF16))
    br = jnp.zeros((1, LANES), _F32)
    br = br.at[0, :N_EXPERTS].set(b_fine.astype(_F32))
    br = br.at[0, N_EXPERTS:N_EXPERTS + N_GROUPS].set(b_group.astype(_F32))
    x1, h2p, route, meta, counts = _postattn(
        y_fox.reshape(t, width), y_moba.reshape(t, width), gates.reshape(t, 2 * d), x.reshape(t, d),
        w_fox_branch.astype(_BF16), w_moba_branch.astype(_BF16), w_out.astype(_BF16),
        moe_norm.reshape(1, d), wr, br, tm=ROW_TILE)

    te = EXPERT_TILE
    n_blk = (2 * t) // te + N_EXPERTS
    counts = counts[:N_EXPERTS, 0].astype(jnp.int32)
    blocks_per = (counts + te - 1) // te
    block_end = jnp.cumsum(blocks_per)
    row_start = (block_end - blocks_per) * te
    expert_ids = jnp.arange(N_EXPERTS, dtype=jnp.int32)[:, None, None]
    slot_major = meta[4:6] + jnp.sum(
        jnp.where(meta[2:4][None] == expert_ids, row_start[:, None, None], 0), axis=0)
    n_used = block_end[-1:].astype(jnp.int32)
    block_ids = jnp.arange(n_blk, dtype=jnp.int32)
    block_expert = jnp.minimum(
        jnp.sum((block_ids[:, None] >= block_end[None, :]).astype(jnp.int32), axis=1), N_EXPERTS - 1)

    first_block = block_end - blocks_per
    block_valid = jnp.clip(counts[block_expert] - (block_ids - first_block[block_expert]) * te, 0, te)
    xs = _sc_scatter(h2p, slot_major.reshape(2, t // SC_SCATTER_CHUNK, SC_SCATTER_CHUNK), n_blk * te)
    rows = _experts(block_expert, block_valid, n_used, xs, w_gate, w_up, w_down, te=te)
    slot_rows = _sc_gather(rows, slot_major.reshape(-1))
    out = _final(x1, route, p.reshape(t, -1), slot_rows, ple_norm.reshape(1, d),
                 w_ple_gate.astype(_BF16), w_ple_proj.astype(_BF16), final_gain.reshape(1, d),
                 tm=FINAL_TILE)
    return out.reshape(b, s, d)


def kernel(x, p, attn_norm, w_in, b_forget, w_fox_branch, w_moba_branch, w_out, moe_norm, w_group,
           b_group, w_fine, b_fine, w_gate, w_up, w_down, ple_norm, w_ple_gate, w_ple_proj, final_norm):
    depth = p.shape[0]
    assert depth == 1, "the final norm is fused into the last layer's kernel"
    i = 0
    return _layer(x, p[i], attn_norm[i], w_in[i], b_forget[i], w_fox_branch[i], w_moba_branch[i],
                  w_out[i], moe_norm[i], w_group[i], b_group[i], w_fine[i], b_fine[i], w_gate[i],
                  w_up[i], w_down[i], ple_norm[i], w_ple_gate[i], w_ple_proj[i], final_norm)
```

```python
import functools

import jax
import jax.numpy as jnp
from jax import lax
from jax.experimental import pallas as pl
from jax.experimental.pallas import tpu as pltpu
from jax.experimental.pallas import tpu_sc as plsc

HEAD_DIM = 64
N_FOX_HEADS = 8
N_MOBA_HEADS = 8
MOBA_BLOCK = 256
MOBA_TOPK = 3
ROPE_THETA = 10000.0
N_GROUPS = 4
EXPERTS_PER_GROUP = 8
N_EXPERTS = N_GROUPS * EXPERTS_PER_GROUP
RMS_EPS = 1e-6

LANES = 128
SUBLANES = 8
BLOCK_LOOP_UNROLL = 8
HEADS_PER_STEP = LANES // HEAD_DIM
ATTN_Q = 256
FOX_PIPELINE = (256, 8, 4)
MOBA_PIPELINE = (256, 8, 4)
ONES_ROWS = 16
SOFTMAX_ROWS = 64
BIAS_PIECES = 3
ROW_TILE = 512
POSTATTN_TILE = 1024
SC_CORES = 2
SC_SUBCORES = 16
SC_CHUNK = 64
SC_SCATTER_CHUNK = 128
FINAL_TILE = 1024
EXPERT_PARTS = 2
EXPERT_TILE = 512
LOG2_E = 1.4426950408889634
UNDERFLOW_LOG2 = 160.0
NEG_BIG = -1e30
VMEM_LIMIT = 48 * 1024 * 1024

_BF16 = jnp.bfloat16
_F32 = jnp.float32


def _params(*sem):
    return pltpu.CompilerParams(dimension_semantics=sem, vmem_limit_bytes=VMEM_LIMIT)


def _rms(x, g):
    return x * lax.rsqrt(jnp.mean(x * x, axis=-1, keepdims=True) + RMS_EPS) * g


def _const_spec(shape):
    return pl.BlockSpec(shape, lambda *_: (0,) * len(shape))


def _inproj_body(x_ref, g_ref, w_ref, wt_ref, wf_ref, bf_ref, cos_ref, sin_ref, cost_ref, sint_ref,
                 qft_ref, kf_ref, vft_ref, qmt_ref, km_ref, vmt_ref, gates_ref, cb_ref, kmean_ref,
                 kabs_ref, bmax_ref,
                 carry_ref, *, tm, width, gate_cols):
    j = pl.program_id(1)
    h = _rms(x_ref[...], g_ref[...])
    hb = h.astype(_BF16)
    hbt = h.T.astype(_BF16)
    reps = width // LANES
    half = HEAD_DIM // 2

    blocks = tm // MOBA_BLOCK

    @pl.when(j == 0)
    def _():
        kmean_ref[...] = jnp.zeros_like(kmean_ref)
        kabs_ref[...] = jnp.zeros_like(kabs_ref)
        bmax_ref[...] = jnp.zeros_like(bmax_ref)
        carry_ref[...] = jnp.zeros_like(carry_ref)

    z = jnp.dot(hb, wf_ref[...], preferred_element_type=_F32) + bf_ref[...]
    log_f = jnp.minimum(z, 0.0) - jnp.log1p(jnp.exp(-jnp.abs(z)))
    r = lax.broadcasted_iota(jnp.int32, (LANES, LANES), 0)
    c = lax.broadcasted_iota(jnp.int32, (LANES, LANES), 1)
    tri = (c <= r).astype(_BF16)
    offset = carry_ref[0:1, :]
    blocks_cum = []
    for blk in range(tm // LANES):
        rest = log_f[blk * LANES:(blk + 1) * LANES]
        within = jnp.zeros((LANES, LANES), _F32)
        for _ in range(BIAS_PIECES):
            part = rest.astype(_BF16)
            rest = rest - part.astype(_F32)
            within = within + jnp.dot(tri, part, preferred_element_type=_F32)
        blocks_cum.append(within + offset)
        offset = offset + within[LANES - 1:LANES, :]
    cum = jnp.concatenate(blocks_cum, axis=0)
    carry_ref[...] = jnp.broadcast_to(offset, carry_ref.shape)

    head_lane = lax.broadcasted_iota(jnp.int32, (1, LANES), 1) < N_FOX_HEADS
    bias = jnp.where(head_lane, -cum * LOG2_E, 0.0)
    for b in range(blocks):
        bmax_ref[pl.ds(j * blocks + b, 1), :] = jnp.max(
            bias[b * MOBA_BLOCK:(b + 1) * MOBA_BLOCK], axis=0, keepdims=True)
    rest = bias
    placed = jnp.zeros((tm, LANES), _F32)
    for piece in range(BIAS_PIECES):
        part = rest.astype(_BF16).astype(_F32)
        rest = rest - part
        placed = placed + (pltpu.roll(part, N_FOX_HEADS * piece, 1) if piece else part)
    cb_ref[...] = placed.astype(_BF16)

    def proj(c):
        return jnp.dot(hb, w_ref[:, c * width:(c + 1) * width], preferred_element_type=_F32)

    def proj_t(c):
        return jnp.dot(wt_ref[c * width:(c + 1) * width, :], hbt, preferred_element_type=_F32)

    def rope(t):
        cos = jnp.concatenate([cos_ref[...]] * reps, axis=1)
        sin = jnp.concatenate([sin_ref[...]] * reps, axis=1)
        first = (lax.broadcasted_iota(jnp.int32, (1, width), 1) % HEAD_DIM) < half
        partner = jnp.where(first, pltpu.roll(t, width - half, 1), pltpu.roll(t, half, 1))
        return t * cos + partner * sin

    def rope_t(t):
        cos = jnp.concatenate([cost_ref[...]] * reps, axis=0)
        sin = jnp.concatenate([sint_ref[...]] * reps, axis=0)
        first = (lax.broadcasted_iota(jnp.int32, (width, 1), 0) % HEAD_DIM) < half
        partner = jnp.where(first, pltpu.roll(t, width - half, 0), pltpu.roll(t, half, 0))
        return t * cos + partner * sin

    scale = HEAD_DIM ** -0.5 * LOG2_E
    kf = proj(0).astype(_BF16)
    kf_ref[...] = kf
    kf_abs = jnp.abs(kf.astype(_F32))
    km = rope(proj(1))
    km_ref[...] = km.astype(_BF16)

    for b in range(blocks):
        rows = slice(b * MOBA_BLOCK, (b + 1) * MOBA_BLOCK)
        kmean_ref[pl.ds(j * blocks + b, 1), :] = jnp.mean(km[rows], axis=0, keepdims=True)
        kabs_ref[pl.ds(j * blocks + b, 1), :] = jnp.max(kf_abs[rows], axis=0, keepdims=True)

    for c in range(gate_cols // width):
        g = jnp.dot(hb, w_ref[:, (2 + c) * width:(3 + c) * width],
                    preferred_element_type=_F32)
        gates_ref[:, c * width:(c + 1) * width] = jax.nn.sigmoid(g).astype(_BF16)

    qft_ref[...] = (proj_t(0) * scale).astype(_BF16)
    vft_ref[...] = proj_t(1).astype(_BF16)
    qmt_ref[...] = (rope_t(proj_t(2)) * scale).astype(_BF16)
    vmt_ref[...] = proj_t(3).astype(_BF16)


def _inproj(x, g, w, wt, wf, bfp, cos, sin, cos_t, sin_t, *, tm):
    b, s, d = x.shape
    width = N_FOX_HEADS * HEAD_DIM
    gate_cols = w.shape[1] - 2 * width
    act = jax.ShapeDtypeStruct((b, s, width), _BF16)
    act_t = jax.ShapeDtypeStruct((b, width, s), _BF16)
    act_spec = pl.BlockSpec((None, tm, width), lambda bi, j: (bi, j, 0))
    act_t_spec = pl.BlockSpec((None, width, tm), lambda bi, j: (bi, 0, j))
    return pl.pallas_call(
        functools.partial(_inproj_body, tm=tm, width=width, gate_cols=gate_cols),
        grid=(b, s // tm),
        in_specs=[
            pl.BlockSpec((None, tm, d), lambda bi, j: (bi, j, 0)),
            _const_spec((1, d)),
            _const_spec(w.shape),
            _const_spec(wt.shape),
            _const_spec(wf.shape),
            _const_spec(bfp.shape),
            pl.BlockSpec((tm, LANES), lambda bi, j: (j, 0)),
            pl.BlockSpec((tm, LANES), lambda bi, j: (j, 0)),
            pl.BlockSpec((LANES, tm), lambda bi, j: (0, j)),
            pl.BlockSpec((LANES, tm), lambda bi, j: (0, j)),
        ],
        out_specs=[act_t_spec, act_spec, act_t_spec, act_t_spec, act_spec, act_t_spec,
                   pl.BlockSpec((None, tm, gate_cols), lambda bi, j: (bi, j, 0)),
                   pl.BlockSpec((None, tm, LANES), lambda bi, j: (bi, j, 0)),
                   pl.BlockSpec((None, LANES, width), lambda bi, j: (bi, 0, 0)),
                   pl.BlockSpec((None, LANES, width), lambda bi, j: (bi, 0, 0)),
                   pl.BlockSpec((None, LANES, LANES), lambda bi, j: (bi, 0, 0))],
        out_shape=[act_t, act, act_t, act_t, act, act_t,
                   jax.ShapeDtypeStruct((b, s, gate_cols), _BF16),
                   jax.ShapeDtypeStruct((b, s, LANES), _BF16),
                   jax.ShapeDtypeStruct((b, LANES, width), _F32),
                   jax.ShapeDtypeStruct((b, LANES, width), _F32),
                   jax.ShapeDtypeStruct((b, LANES, LANES), _F32)],
        scratch_shapes=[pltpu.VMEM((SUBLANES, LANES), _F32)],
        compiler_params=_params("arbitrary", "arbitrary"),
        name="inproj",
    )(x, g, w, wt, wf, bfp, cos, sin, cos_t, sin_t)


def _attn_body(uq_ref, uk_ref, qt_ref, k_ref, kx_ref, vt_ref, aux_ref, aux2_ref, tri_ref, o_ref,
               w_ref, m_ref, l_ref, acc_ref, st_ref, p_ref, alpha_ref, pmax_ref, qn_ref, *,
               mode, tq, tk, nb, n_diag, n_units, pipe, lag):
    hp = pl.program_id(1)
    wide = HEADS_PER_STEP * tq
    nq = w_ref.shape[0]
    feat = lax.broadcasted_iota(jnp.int32, (LANES, 1), 0)
    xr = lax.broadcasted_iota(jnp.int32, (LANES, wide), 0)
    xc = lax.broadcasted_iota(jnp.int32, (LANES, wide), 1)

    def setup(i, carry):
        qt = qt_ref[:, pl.ds(pl.multiple_of(i * tq, tq), tq)]
        zero = jnp.zeros_like(qt)
        wq = jnp.concatenate([jnp.where(feat < HEAD_DIM, qt, zero),
                              jnp.where(feat < HEAD_DIM, zero, qt)], axis=1)
        if mode == "fox":
            head = hp * HEADS_PER_STEP + xc // tq
            lower = ((xr % N_FOX_HEADS == head) & (xr < BIAS_PIECES * N_FOX_HEADS)).astype(_BF16)
            qn_ref[i] = jnp.broadcast_to(
                jnp.sum(jnp.abs(wq.astype(_F32)), axis=0, keepdims=True), qn_ref.shape[1:])
        else:
            own = (i * tq) // MOBA_BLOCK
            gate = jnp.dot(aux_ref[...].astype(_BF16), wq, preferred_element_type=_F32)[0:nb]
            blk = lax.broadcasted_iota(jnp.int32, (nb, wide), 0)
            gate = jnp.where(blk < own, gate, -jnp.inf)
            keep = blk == own
            for _ in range(MOBA_TOPK):
                mx = jnp.max(gate, axis=0, keepdims=True)
                cand = jnp.where((gate == mx) & (mx > -jnp.inf), blk, nb)
                pick = blk == jnp.min(cand, axis=0, keepdims=True)
                keep = keep | pick
                gate = jnp.where(pick, -jnp.inf, gate)
            lower = jnp.where(keep, 0.0, NEG_BIG)
            if nb < LANES:
                lower = jnp.concatenate([lower, jnp.zeros((LANES - nb, wide), _F32)], axis=0)
            lower = lower.astype(_BF16)
        w_ref[i] = jnp.concatenate([wq, lower], axis=0)
        m_ref[i] = jnp.full(m_ref.shape[1:], NEG_BIG, _F32)
        l_ref[i] = jnp.zeros(l_ref.shape[1:], _F32)
        acc_ref[i] = jnp.zeros(acc_ref.shape[1:], _F32)
        return carry

    lax.fori_loop(0, nq, setup, 0, unroll=BLOCK_LOOP_UNROLL)

    def unit(t):
        return uq_ref[t], uk_ref[t]

    def scores(t, slot, causal):
        q, kb = unit(t)
        start = pl.multiple_of(kb * tk, tk)
        ka = jnp.concatenate([k_ref[pl.ds(start, tk), :], kx_ref[pl.ds(start, tk), :]], axis=1)
        st = jnp.dot(ka, w_ref[q], preferred_element_type=_F32)
        if causal:
            st = st + tri_ref[...]
        st_ref[slot] = st
        pmax_ref[slot] = jnp.max(st.reshape(tk // SUBLANES, SUBLANES, wide), axis=0)

    def softmax(t, slot):
        q, _ = unit(t)
        rows = SOFTMAX_ROWS
        m_prev = m_ref[q][0:1, :]
        m_new = jnp.maximum(m_prev, jnp.max(pmax_ref[slot], axis=0, keepdims=True))
        for r in range(0, tk, rows):
            p_ref[slot, r:r + rows, :] = jnp.exp2(st_ref[slot, r:r + rows, :] - m_new).astype(_BF16)
        m_ref[q] = jnp.broadcast_to(m_new, m_ref.shape[1:])
        alpha_ref[slot] = jnp.broadcast_to(jnp.exp2(m_prev - m_new), alpha_ref.shape[1:])

    def values(t, slot):
        q, kb = unit(t)
        start = pl.multiple_of(kb * tk, tk)
        ones = jnp.ones((ONES_ROWS, tk), _BF16)
        for h in range(HEADS_PER_STEP):
            cols = slice(h * tq, (h + 1) * tq)
            feats = slice(h * HEAD_DIM, (h + 1) * HEAD_DIM)
            lhs = jnp.concatenate([vt_ref[feats, pl.ds(start, tk)], ones], axis=0)
            pv = jnp.dot(lhs, p_ref[slot, :, cols], preferred_element_type=_F32)
            alpha = alpha_ref[slot, 0:1, cols]
            acc_ref[q, feats, :] = acc_ref[q, feats, :] * alpha + pv[0:HEAD_DIM]
            l_ref[q, :, cols] = jnp.broadcast_to(
                alpha * l_ref[q, 0:1, cols] + pv[HEAD_DIM:HEAD_DIM + 1], (SUBLANES, tq))

    def step(base, c, causal, do_scores=True, do_softmax=True, do_values=True):
        u = base + c
        if do_scores:
            scores(u, c % pipe, causal)
        if do_values:
            values(u - 2 * lag, (c - 2 * lag) % pipe)
        if do_softmax:
            softmax(u - lag, (c - lag) % pipe)

    def group(causal):
        def body(j, carry):
            for c in range(pipe):
                step(j * pipe, c, causal)
            return carry
        return body

    def pipeline(g_first, g_end, causal_groups):
        for c in range(pipe):
            step(g_first * pipe, c, g_first < causal_groups,
                 do_softmax=c >= lag, do_values=c >= 2 * lag)
        if g_first + 1 < causal_groups:
            lax.fori_loop(g_first + 1, causal_groups, group(True), 0)
        lax.fori_loop(max(g_first + 1, causal_groups), g_end, group(False), 0)
        for c in range(2 * lag):
            step(g_end * pipe, c, False, do_scores=False, do_softmax=c < lag)

    diag_groups = n_diag // pipe
    if mode == "moba":
        pipeline(0, n_units // pipe, diag_groups)
    else:
        pipeline(0, diag_groups, diag_groups)
        nq_blocks = n_diag
        col = lax.broadcasted_iota(jnp.int32, (LANES, LANES), 1)
        first_head = lax.broadcasted_iota(jnp.int32, (1, wide), 1) < tq
        head0 = hp * HEADS_PER_STEP

        def per_head(table, pick0, pick1, fill):
            v0 = jnp.max(jnp.where(pick0, table, fill), axis=1, keepdims=True)[0:nb]
            v1 = jnp.max(jnp.where(pick1, table, fill), axis=1, keepdims=True)[0:nb]
            return jnp.where(first_head, v0, v1)

        bias_max = per_head(aux2_ref[...], col == head0, col == head0 + 1, NEG_BIG)
        k_absmax = per_head(aux_ref[...], col < HEAD_DIM, col >= HEAD_DIM, 0.0)
        blk = lax.broadcasted_iota(jnp.int32, (nb, 1), 0)

        needed = jnp.zeros((nb, 1), jnp.int32)
        for i in range(1, nq_blocks):
            bound = bias_max + qn_ref[i, 0:1, :] * k_absmax - m_ref[i, 0:1, :]
            live = jnp.max(bound, axis=1, keepdims=True) >= -UNDERFLOW_LOG2
            needed = jnp.maximum(needed, jnp.where(live & (blk < i), i - blk, 0))
        far = jnp.max(needed)
        n_live = far * nq_blocks - (far * (far + 1)) // 2
        live_groups = (n_live + pipe - 1) // pipe

        @pl.when(live_groups > 0)
        def _():
            pipeline(diag_groups, diag_groups + live_groups, diag_groups)

    def finish(i, carry):
        out_t = jnp.concatenate(
            [acc_ref[i, h * HEAD_DIM:(h + 1) * HEAD_DIM, :] / l_ref[i, 0:1, h * tq:(h + 1) * tq]
             for h in range(HEADS_PER_STEP)], axis=0)
        o_ref[pl.ds(pl.multiple_of(i * tq, tq), tq), :] = out_t.T.astype(o_ref.dtype)
        return carry

    lax.fori_loop(0, nq, finish, 0, unroll=BLOCK_LOOP_UNROLL)


def _attention(qt, k, kx, vt, aux, aux2, *, mode):
    b, width, s = qt.shape
    tq = ATTN_Q
    tk, pipe, lag = FOX_PIPELINE if mode == "fox" else MOBA_PIPELINE
    nq = s // tq
    nb = s // MOBA_BLOCK
    wide = HEADS_PER_STEP * tq
    diag = [(i * tq) // tk for i in range(nq)]
    units = list(enumerate(diag))
    units += [(i, diag[i] - dist) for dist in range(1, nq) for i in range(nq) if diag[i] >= dist]
    assert mode != "fox" or tq == tk
    assert nq % pipe == 0 and len(units) % pipe == 0 and pipe >= 2 * lag
    key_row = jnp.arange(tk, dtype=jnp.int32)[:, None]
    qry_col = jnp.arange(wide, dtype=jnp.int32)[None, :] % tq
    causal_bias = jnp.where(key_row <= qry_col, 0.0, NEG_BIG).astype(_F32)
    unit_q = jnp.asarray([u[0] for u in units], jnp.int32)
    unit_k = jnp.asarray([u[1] for u in units], jnp.int32)
    aux_spec = pl.BlockSpec((None, LANES, LANES), lambda bi, hp, uq, uk: (bi, 0, hp))
    aux2_spec = pl.BlockSpec((None, LANES, LANES), lambda bi, hp, uq, uk: (bi, 0, 0))
    if mode == "fox":
        kx_spec = pl.BlockSpec((None, s, LANES), lambda bi, hp, uq, uk: (bi, 0, 0))
    else:
        kx_spec = pl.BlockSpec(kx.shape, lambda bi, hp, uq, uk: (0, 0))
    return pl.pallas_call(
        functools.partial(_attn_body, mode=mode, tq=tq, tk=tk, nb=nb, n_diag=nq, n_units=len(units),
                          pipe=pipe, lag=lag),
        grid_spec=pltpu.PrefetchScalarGridSpec(
            num_scalar_prefetch=2,
            grid=(b, width // LANES),
            in_specs=[
                pl.BlockSpec((None, LANES, s), lambda bi, hp, uq, uk: (bi, hp, 0)),
                pl.BlockSpec((None, s, LANES), lambda bi, hp, uq, uk: (bi, 0, hp)),
                kx_spec,
                pl.BlockSpec((None, LANES, s), lambda bi, hp, uq, uk: (bi, hp, 0)),
                aux_spec,
                aux2_spec,
                pl.BlockSpec((tk, wide), lambda bi, hp, uq, uk: (0, 0)),
            ],
            out_specs=pl.BlockSpec((None, s, LANES), lambda bi, hp, uq, uk: (bi, 0, hp)),
            scratch_shapes=[
                pltpu.VMEM((nq, 2 * LANES, wide), _BF16),
                pltpu.VMEM((nq, SUBLANES, wide), _F32),
                pltpu.VMEM((nq, SUBLANES, wide), _F32),
                pltpu.VMEM((nq, LANES, tq), _F32),
                pltpu.VMEM((pipe, tk, wide), _F32),
                pltpu.VMEM((pipe, tk, wide), _BF16),
                pltpu.VMEM((pipe, SUBLANES, wide), _F32),
                pltpu.VMEM((pipe, SUBLANES, wide), _F32),
                pltpu.VMEM((nq, SUBLANES, wide), _F32),
            ],
        ),
        out_shape=jax.ShapeDtypeStruct((b, s, width), _BF16),
        compiler_params=_params("arbitrary", "arbitrary"),
        name=mode,
    )(unit_q, unit_k, qt, k, kx, vt, aux, aux2, causal_bias)


def _pack_bf16_pair(a, b):
    lo = pltpu.bitcast(a.astype(_BF16).astype(_F32), jnp.uint32) >> 16
    hi = pltpu.bitcast(b.astype(_BF16).astype(_F32), jnp.uint32) & jnp.uint32(0xFFFF0000)
    return lo | hi


def _unpack_bf16_pair(u):
    lo = pltpu.bitcast(u << 16, _F32)
    hi = pltpu.bitcast(u & jnp.uint32(0xFFFF0000), _F32)
    return jnp.concatenate([lo, hi], axis=1)


def _postattn_body(yf_ref, ym_ref, gates_ref, x_ref, wfb_ref, wmb_ref, wout_ref, g_ref, wr_ref, br_ref,
                   x1_ref, h2_ref, route_ref, meta_ref, counts_ref, carry_ref, *, tm, d):
    step = pl.program_id(0)
    ya = jnp.dot(yf_ref[...], wfb_ref[...], preferred_element_type=_F32)
    yb = jnp.dot(ym_ref[...], wmb_ref[...], preferred_element_type=_F32)
    mixed = gates_ref[:, 0:d].astype(_F32) * ya + gates_ref[:, d:2 * d].astype(_F32) * yb
    x1 = x_ref[...] + jnp.dot(mixed.astype(_BF16), wout_ref[...], preferred_element_type=_F32)
    x1_ref[...] = x1
    h2 = _rms(x1, g_ref[...])
    h2_ref[...] = _pack_bf16_pair(h2[:, 0:d // 2], h2[:, d // 2:d])

    logits = jnp.dot(h2.astype(_BF16), wr_ref[...], preferred_element_type=_F32) + br_ref[...]
    logits_t = logits.T
    gl = logits_t[N_EXPERTS:N_EXPERTS + N_GROUPS]
    ge = jnp.exp(gl - jnp.max(gl, axis=0, keepdims=True))
    gp = ge / jnp.sum(ge, axis=0, keepdims=True)
    g_top = jnp.max(gp, axis=0, keepdims=True)
    g_row = lax.broadcasted_iota(jnp.int32, (N_GROUPS, tm), 0)
    g_idx = jnp.min(jnp.where(gp == g_top, g_row, N_GROUPS), axis=0, keepdims=True)
    fl = logits_t[0:EXPERTS_PER_GROUP]
    for g in range(1, N_GROUPS):
        fl = jnp.where(g_idx == g, logits_t[g * EXPERTS_PER_GROUP:(g + 1) * EXPERTS_PER_GROUP], fl)
    f_row = lax.broadcasted_iota(jnp.int32, (EXPERTS_PER_GROUP, tm), 0)
    f1 = jnp.max(fl, axis=0, keepdims=True)
    i1 = jnp.min(jnp.where(fl == f1, f_row, EXPERTS_PER_GROUP), axis=0, keepdims=True)
    fl2 = jnp.where(f_row == i1, -jnp.inf, fl)
    f2 = jnp.max(fl2, axis=0, keepdims=True)
    i2 = jnp.min(jnp.where(fl2 == f2, f_row, EXPERTS_PER_GROUP), axis=0, keepdims=True)
    e1 = g_idx * EXPERTS_PER_GROUP + i1
    e2 = g_idx * EXPERTS_PER_GROUP + i2
    t2 = jnp.exp(f2 - f1)
    w1 = g_top * (1.0 / (1.0 + t2))
    w2 = g_top * (t2 / (1.0 + t2))

    @pl.when(step == 0)
    def _():
        carry_ref[...] = jnp.zeros_like(carry_ref)

    x_row = lax.broadcasted_iota(jnp.int32, (N_EXPERTS, tm), 0)
    onehot = ((x_row == e1) | (x_row == e2)).astype(_F32)
    rr = lax.broadcasted_iota(jnp.int32, (tm, tm), 0)
    cc = lax.broadcasted_iota(jnp.int32, (tm, tm), 1)
    before = jnp.dot(onehot.astype(_BF16), (rr < cc).astype(_BF16), preferred_element_type=_F32)
    before = before + carry_ref[0:N_EXPERTS, 0:1]
    r1 = jnp.sum(jnp.where(x_row == e1, before, 0.0), axis=0, keepdims=True)
    r2 = jnp.sum(jnp.where(x_row == e2, before, 0.0), axis=0, keepdims=True)
    total = carry_ref[0:N_EXPERTS, 0:1] + jnp.sum(onehot, axis=1, keepdims=True)
    carry_ref[0:N_EXPERTS, :] = jnp.broadcast_to(total, (N_EXPERTS, LANES))
    counts_ref[...] = carry_ref[...]

    m_row = lax.broadcasted_iota(jnp.int32, (SUBLANES, tm), 0)
    meta = jnp.where(m_row == 2, e1, 0)
    meta = jnp.where(m_row == 3, e2, meta)
    meta = jnp.where(m_row == 4, r1.astype(jnp.int32), meta)
    meta = jnp.where(m_row == 5, r2.astype(jnp.int32), meta)
    meta_ref[...] = meta
    w_row = lax.broadcasted_iota(jnp.int32, (LANES, tm), 0)
    route_ref[...] = jnp.where(w_row == 0, w1, jnp.where(w_row == 1, w2, 0.0)).T


def _postattn(yf, ym, gates, x, wfb, wmb, wout, g, wr, br, *, tm):
    t, d = x.shape
    width = yf.shape[1]
    row = lambda cols: pl.BlockSpec((tm, cols), lambda i: (i, 0))
    return pl.pallas_call(
        functools.partial(_postattn_body, tm=tm, d=d),
        grid=(t // tm,),
        in_specs=[row(width), row(width), row(2 * d), row(d),
                  _const_spec(wfb.shape), _const_spec(wmb.shape), _const_spec(wout.shape),
                  _const_spec((1, d)), _const_spec(wr.shape), _const_spec(br.shape)],
        out_specs=[row(d), row(d // 2), row(LANES), pl.BlockSpec((SUBLANES, tm), lambda i: (0, i)),
                   _const_spec((LANES, LANES))],
        out_shape=[jax.ShapeDtypeStruct((t, d), _F32),
                   jax.ShapeDtypeStruct((t, d // 2), jnp.uint32),
                   jax.ShapeDtypeStruct((t, LANES), _F32),
                   jax.ShapeDtypeStruct((SUBLANES, t), jnp.int32),
                   jax.ShapeDtypeStruct((LANES, LANES), _F32)],
        scratch_shapes=[pltpu.VMEM((LANES, LANES), _F32)],
        compiler_params=_params("arbitrary"),
        name="postattn",
    )(yf, ym, gates, x, wfb, wmb, wout, g, wr, br)


def _sc_scatter(rows, idx, n_out):
    n, width = rows.shape
    workers = SC_CORES * SC_SUBCORES
    per_worker = n // workers
    chunks = per_worker // SC_SCATTER_CHUNK
    assert n % (workers * SC_SCATTER_CHUNK) == 0 and idx.shape == (2, n // SC_SCATTER_CHUNK, SC_SCATTER_CHUNK)
    mesh = plsc.VectorSubcoreMesh(core_axis_name="c", subcore_axis_name="s",
                                  num_cores=SC_CORES, num_subcores=SC_SUBCORES)

    @functools.partial(
        pl.kernel, out_type=jax.ShapeDtypeStruct((n_out, width), rows.dtype), mesh=mesh,
        scratch_types=[pltpu.VMEM((2, chunks, SC_SCATTER_CHUNK), jnp.int32),
                       pltpu.VMEM((SC_SCATTER_CHUNK, width), rows.dtype),
                       pltpu.SemaphoreType.DMA],
        name="sc_scatter")
    def scatter(rows_hbm, idx_hbm, out_hbm, idx_v, rows_v, sem):
        worker = lax.axis_index("s") * SC_CORES + lax.axis_index("c")
        for k in range(2):
            pltpu.sync_copy(idx_hbm.at[k, pl.ds(worker * chunks, chunks)], idx_v.at[k])

        def chunk(j, carry):
            start = worker * per_worker + j * SC_SCATTER_CHUNK
            pltpu.sync_copy(rows_hbm.at[pl.ds(start, SC_SCATTER_CHUNK)], rows_v)
            for k in range(2):
                pltpu.async_copy(rows_v, out_hbm.at[idx_v.at[k, j]], sem).wait()
            return carry

        lax.fori_loop(0, chunks, chunk, 0)

    return scatter(rows, idx)


def _experts_body(be_ref, valid_ref, nused_ref, xs_ref, wg_ref, wu_ref, wd_ref, out_ref,
                  wgb_ref, wub_ref, wdb_ref):
    j = pl.program_id(0)
    used = j < nused_ref[0]

    @pl.when(jnp.logical_not(used))
    def _():
        out_ref[...] = jnp.zeros_like(out_ref)

    @pl.when(used & ((j == 0) | (be_ref[j] != be_ref[jnp.maximum(j - 1, 0)])))
    def _():
        wgb_ref[...] = wg_ref[...].astype(_BF16)
        wub_ref[...] = wu_ref[...].astype(_BF16)
        wdb_ref[...] = wd_ref[...].astype(_BF16)

    @pl.when(used)
    def _():
        row = lax.broadcasted_iota(jnp.int32, xs_ref.shape, 0)
        packed = jnp.where(row < valid_ref[j], xs_ref[...], jnp.uint32(0))
        rows = _unpack_bf16_pair(packed).astype(_BF16)
        hidden = wgb_ref.shape[1]
        out = None
        for c in range(0, hidden, hidden // EXPERT_PARTS):
            cols = slice(c, c + hidden // EXPERT_PARTS)
            gate = jnp.dot(rows, wgb_ref[:, cols], preferred_element_type=_F32)
            up = jnp.dot(rows, wub_ref[:, cols], preferred_element_type=_F32)
            act = ((gate * jax.nn.sigmoid(gate)) * up).astype(_BF16)
            part = jnp.dot(act, wdb_ref[cols, :], preferred_element_type=_F32)
            out = part if out is None else out + part
        half = out.shape[1] // 2
        out_ref[...] = _pack_bf16_pair(out[:, 0:half], out[:, half:])


def _experts(block_expert, block_valid, n_used, xs, wg, wu, wd, *, te):
    n_rows, half = xs.shape
    _, d, de = wg.shape

    def blk(j, be, bv, nu):
        return jnp.minimum(j, nu[0] - 1)

    def weight(j, be, bv, nu):
        return (be[blk(j, be, bv, nu)], 0, 0)

    return pl.pallas_call(
        _experts_body,
        grid_spec=pltpu.PrefetchScalarGridSpec(
            num_scalar_prefetch=3,
            grid=(n_rows // te,),
            in_specs=[pl.BlockSpec((te, half), lambda j, be, bv, nu: (blk(j, be, bv, nu), 0)),
                      pl.BlockSpec((None, d, de), weight),
                      pl.BlockSpec((None, d, de), weight),
                      pl.BlockSpec((None, de, d), weight)],
            out_specs=pl.BlockSpec((te, d // 2), lambda j, be, bv, nu: (j, 0)),
            scratch_shapes=[pltpu.VMEM((d, de), _BF16), pltpu.VMEM((d, de), _BF16),
                            pltpu.VMEM((de, d), _BF16)],
        ),
        out_shape=jax.ShapeDtypeStruct((n_rows, d // 2), jnp.uint32),
        compiler_params=_params("arbitrary"),
        name="experts",
    )(block_expert, block_valid, n_used, xs, wg, wu, wd)


def _sc_gather(table, idx):
    n, = idx.shape
    width = table.shape[1]
    workers = SC_CORES * SC_SUBCORES
    per_worker = n // workers
    chunks = per_worker // SC_CHUNK
    assert n % (workers * SC_CHUNK * 2) == 0
    mesh = plsc.VectorSubcoreMesh(core_axis_name="c", subcore_axis_name="s",
                                  num_cores=SC_CORES, num_subcores=SC_SUBCORES)

    @functools.partial(
        pl.kernel, out_type=jax.ShapeDtypeStruct((n, width), table.dtype), mesh=mesh,
        scratch_types=[pltpu.VMEM((per_worker,), jnp.int32),
                       pltpu.VMEM((2, SC_CHUNK, width), table.dtype),
                       pltpu.SemaphoreType.DMA((2,))],
        name="sc_gather")
    def gather(table_hbm, idx_hbm, out_hbm, idx_v, rows_v, sems):
        worker = lax.axis_index("s") * SC_CORES + lax.axis_index("c")
        base = worker * per_worker
        pltpu.sync_copy(idx_hbm.at[pl.ds(base, per_worker)], idx_v)

        def fetch(j, slot):
            off = pl.multiple_of(j * SC_CHUNK, SC_CHUNK)
            return pltpu.async_copy(table_hbm.at[idx_v.at[pl.ds(off, SC_CHUNK)]], rows_v.at[slot],
                                    sems.at[slot])

        def drain(j, slot):
            off = pl.multiple_of(j * SC_CHUNK, SC_CHUNK)
            pltpu.make_async_copy(out_hbm.at[pl.ds(base + off, SC_CHUNK)], rows_v.at[slot],
                                  sems.at[slot]).wait()
            pltpu.sync_copy(rows_v.at[slot], out_hbm.at[pl.ds(base + off, SC_CHUNK)])

        fetch(0, 0)

        def pair(p, carry):
            fetch(2 * p + 1, 1)
            drain(2 * p, 0)

            @pl.when(p + 1 < chunks // 2)
            def _():
                fetch(2 * p + 2, 0)

            drain(2 * p + 1, 1)
            return carry

        lax.fori_loop(0, chunks // 2, pair, 0)

    return gather(table, idx)


def _final_body(x1_ref, route_ref, p_ref, r1_ref, r2_ref, gp_ref, wpg_ref, wpp_ref, gf_ref, o_ref):
    emb = jnp.dot(p_ref[...].astype(_BF16), wpp_ref[...], preferred_element_type=_F32)
    route = route_ref[...]
    x2 = x1_ref[...] + (_unpack_bf16_pair(r1_ref[...]) * route[:, 0:1]
                        + _unpack_bf16_pair(r2_ref[...]) * route[:, 1:2])
    gate = jax.nn.sigmoid(jnp.dot(_rms(x2, gp_ref[...]).astype(_BF16), wpg_ref[...],
                                  preferred_element_type=_F32))
    o_ref[...] = _rms(x2 + gate * emb, gf_ref[...])


def _final(x1, route, p, slot_rows, gp, wpg, wpp, gf, *, tm):
    t, d = x1.shape
    ple = p.shape[1]
    steps = t // tm
    row = lambda cols: pl.BlockSpec((tm, cols), lambda i: (i, 0))
    return pl.pallas_call(
        _final_body,
        grid=(steps,),
        in_specs=[row(d), row(LANES), row(ple), row(d // 2),
                  pl.BlockSpec((tm, d // 2), lambda i: (steps + i, 0)),
                  _const_spec((1, d)), _const_spec(wpg.shape), _const_spec(wpp.shape),
                  _const_spec((1, d))],
        out_specs=row(d),
        out_shape=jax.ShapeDtypeStruct((t, d), _F32),
        compiler_params=_params("arbitrary"),
        name="final",
    )(x1, route, p, slot_rows, slot_rows, gp, wpg, wpp, gf)


def _rope_tables(s):
    half = HEAD_DIM // 2
    inv = 1.0 / (ROPE_THETA ** (jnp.arange(0, HEAD_DIM, 2, dtype=_F32) / HEAD_DIM))
    ang = jnp.arange(s, dtype=_F32)[:, None] * inv[None, :]
    cos, sin = jnp.cos(ang), jnp.sin(ang)
    reps = LANES // half
    cos_t = jnp.tile(cos, (1, reps))
    sin_t = jnp.tile(jnp.concatenate([-sin, sin], axis=1), (1, reps // 2))
    return cos_t, sin_t


def _layer(x, p, attn_norm, w_in, b_forget, w_fox_branch, w_moba_branch, w_out, moe_norm,
           w_group, b_group, w_fine, b_fine, w_gate, w_up, w_down, ple_norm, w_ple_gate, w_ple_proj,
           final_gain):
    b, s, d = x.shape
    t = b * s
    width = N_FOX_HEADS * HEAD_DIM
    assert N_MOBA_HEADS * HEAD_DIM == width and MOBA_BLOCK % ATTN_Q == 0 and MOBA_PIPELINE[0] % MOBA_BLOCK == 0
    assert BIAS_PIECES * N_FOX_HEADS <= LANES and s % MOBA_PIPELINE[0] == 0
    assert s % ROW_TILE == 0 and ROW_TILE % MOBA_BLOCK == 0 and s // MOBA_BLOCK <= LANES
    assert t % FINAL_TILE == 0 and t % POSTATTN_TILE == 0 and N_EXPERTS + N_GROUPS <= LANES

    qkv_cols = 6 * width
    f_cols = N_FOX_HEADS
    chunk = lambda c: w_in[:, c * width:(c + 1) * width]
    w_rows = jnp.concatenate([chunk(1), chunk(4), w_in[:, qkv_cols + f_cols:]], axis=1).astype(_BF16)
    w_cols = jnp.concatenate([chunk(0), chunk(2), chunk(3), chunk(5)], axis=1).T.astype(_BF16)
    wf = jnp.zeros((d, LANES), _BF16).at[:, :f_cols].set(w_in[:, qkv_cols:qkv_cols + f_cols].astype(_BF16))
    bfp = jnp.zeros((1, LANES), _F32).at[0, :f_cols].set(b_forget.astype(_F32))
    cos, sin = _rope_tables(s)

    qft, kf, vft, qmt, km, vmt, gates, cb, kmean, kabs, bmax = _inproj(
        x, attn_norm.reshape(1, d), w_rows, w_cols, wf, bfp, cos, sin, cos.T, sin.T, tm=ROW_TILE)
    y_fox = _attention(qft, kf, cb, vft, kabs, bmax, mode="fox")
    block_of_key = jnp.arange(s, dtype=jnp.int32)[:, None] // MOBA_BLOCK
    block_onehot = (block_of_key == jnp.arange(LANES, dtype=jnp.int32)[None, :]).astype(_BF16)
    y_moba = _attention(qmt, km, block_onehot, vmt, kmean, bmax, mode="moba")

    wr = jnp.zeros((d, LANES), _BF16)
    wr = wr.at[:, :N_EXPERTS].set(w_fine.astype(_BF16))
    wr = wr.at[:, N_EXPERTS:N_EXPERTS + N_GROUPS].set(w_group.astype(_BF16))
    br = jnp.zeros((1, LANES), _F32)
    br = br.at[0, :N_EXPERTS].set(b_fine.astype(_F32))
    br = br.at[0, N_EXPERTS:N_EXPERTS + N_GROUPS].set(b_group.astype(_F32))
    x1, h2p, route, meta, counts = _postattn(
        y_fox.reshape(t, width), y_moba.reshape(t, width), gates.reshape(t, 2 * d), x.reshape(t, d),
        w_fox_branch.astype(_BF16), w_moba_branch.astype(_BF16), w_out.astype(_BF16),
        moe_norm.reshape(1, d), wr, br, tm=POSTATTN_TILE)

    te = EXPERT_TILE
    n_blk = (2 * t) // te + N_EXPERTS
    counts = counts[:N_EXPERTS, 0].astype(jnp.int32)
    blocks_per = (counts + te - 1) // te
    block_end = jnp.cumsum(blocks_per)
    row_start = (block_end - blocks_per) * te
    expert_ids = jnp.arange(N_EXPERTS, dtype=jnp.int32)[:, None, None]
    slot_major = meta[4:6] + jnp.sum(
        jnp.where(meta[2:4][None] == expert_ids, row_start[:, None, None], 0), axis=0)
    n_used = block_end[-1:].astype(jnp.int32)
    block_ids = jnp.arange(n_blk, dtype=jnp.int32)
    block_expert = jnp.minimum(
        jnp.sum((block_ids[:, None] >= block_end[None, :]).astype(jnp.int32), axis=1), N_EXPERTS - 1)

    first_block = block_end - blocks_per
    block_valid = jnp.clip(counts[block_expert] - (block_ids - first_block[block_expert]) * te, 0, te)
    xs = _sc_scatter(h2p, slot_major.reshape(2, t // SC_SCATTER_CHUNK, SC_SCATTER_CHUNK), n_blk * te)
    rows = _experts(block_expert, block_valid, n_used, xs, w_gate, w_up, w_down, te=te)
    slot_rows = _sc_gather(rows, slot_major.reshape(-1))
    out = _final(x1, route, p.reshape(t, -1), slot_rows, ple_norm.reshape(1, d),
                 w_ple_gate.astype(_BF16), w_ple_proj.astype(_BF16), final_gain.reshape(1, d),
                 tm=FINAL_TILE)
    return out.reshape(b, s, d)


def kernel(x, p, attn_norm, w_in, b_forget, w_fox_branch, w_moba_branch, w_out, moe_norm, w_group,
           b_group, w_fine, b_fine, w_gate, w_up, w_down, ple_norm, w_ple_gate, w_ple_proj, final_norm):
    depth = p.shape[0]
    assert depth == 1, "the final norm is fused into the last layer's kernel"
    i = 0
    return _layer(x, p[i], attn_norm[i], w_in[i], b_forget[i], w_fox_branch[i], w_moba_branch[i],
                  w_out[i], moe_norm[i], w_group[i], b_group[i], w_fine[i], b_fine[i], w_gate[i],
                  w_up[i], w_down[i], ple_norm[i], w_ple_gate[i], w_ple_proj[i], final_norm)
```

```python
import functools

import jax
import jax.numpy as jnp
from jax import lax
from jax.experimental import pallas as pl
from jax.experimental.pallas import tpu as pltpu
from jax.experimental.pallas import tpu_sc as plsc

HEAD_DIM = 64
N_FOX_HEADS = 8
N_MOBA_HEADS = 8
MOBA_BLOCK = 256
MOBA_TOPK = 3
ROPE_THETA = 10000.0
N_GROUPS = 4
EXPERTS_PER_GROUP = 8
N_EXPERTS = N_GROUPS * EXPERTS_PER_GROUP
RMS_EPS = 1e-6

LANES = 128
SUBLANES = 8
BLOCK_LOOP_UNROLL = 16
HEADS_PER_STEP = LANES // HEAD_DIM
ATTN_Q = 256
FOX_PIPELINE = (256, 8, 4)
MOBA_PIPELINE = (256, 8, 4)
ONES_ROWS = 16
SOFTMAX_ROWS = 64
BIAS_PIECES = 3
ROW_TILE = 512
POSTATTN_TILE = 1024
SC_CORES = 2
SC_SUBCORES = 16
SC_CHUNK = 64
SC_SCATTER_CHUNK = 128
FINAL_TILE = 1024
EXPERT_PARTS = 2
EXPERT_TILE = 512
LOG2_E = 1.4426950408889634
UNDERFLOW_LOG2 = 160.0
NEG_BIG = -1e30
VMEM_LIMIT = 48 * 1024 * 1024

_BF16 = jnp.bfloat16
_F32 = jnp.float32


def _params(*sem):
    return pltpu.CompilerParams(dimension_semantics=sem, vmem_limit_bytes=VMEM_LIMIT)


def _rms(x, g):
    return x * lax.rsqrt(jnp.mean(x * x, axis=-1, keepdims=True) + RMS_EPS) * g


def _const_spec(shape):
    return pl.BlockSpec(shape, lambda *_: (0,) * len(shape))


def _inproj_body(x_ref, g_ref, w_ref, wt_ref, wf_ref, bf_ref, cos_ref, sin_ref, cost_ref, sint_ref,
                 qft_ref, kf_ref, vft_ref, qmt_ref, km_ref, vmt_ref, gates_ref, cb_ref, kmean_ref,
                 kabs_ref, bmax_ref,
                 carry_ref, *, tm, width, gate_cols):
    j = pl.program_id(1)
    h = _rms(x_ref[...], g_ref[...])
    hb = h.astype(_BF16)
    hbt = h.T.astype(_BF16)
    reps = width // LANES
    half = HEAD_DIM // 2

    blocks = tm // MOBA_BLOCK

    @pl.when(j == 0)
    def _():
        kmean_ref[...] = jnp.zeros_like(kmean_ref)
        kabs_ref[...] = jnp.zeros_like(kabs_ref)
        bmax_ref[...] = jnp.zeros_like(bmax_ref)
        carry_ref[...] = jnp.zeros_like(carry_ref)

    z = jnp.dot(hb, wf_ref[...], preferred_element_type=_F32) + bf_ref[...]
    log_f = jnp.minimum(z, 0.0) - jnp.log1p(jnp.exp(-jnp.abs(z)))
    r = lax.broadcasted_iota(jnp.int32, (LANES, LANES), 0)
    c = lax.broadcasted_iota(jnp.int32, (LANES, LANES), 1)
    tri = (c <= r).astype(_BF16)
    offset = carry_ref[0:1, :]
    blocks_cum = []
    for blk in range(tm // LANES):
        rest = log_f[blk * LANES:(blk + 1) * LANES]
        within = jnp.zeros((LANES, LANES), _F32)
        for _ in range(BIAS_PIECES):
            part = rest.astype(_BF16)
            rest = rest - part.astype(_F32)
            within = within + jnp.dot(tri, part, preferred_element_type=_F32)
        blocks_cum.append(within + offset)
        offset = offset + within[LANES - 1:LANES, :]
    cum = jnp.concatenate(blocks_cum, axis=0)
    carry_ref[...] = jnp.broadcast_to(offset, carry_ref.shape)

    head_lane = lax.broadcasted_iota(jnp.int32, (1, LANES), 1) < N_FOX_HEADS
    bias = jnp.where(head_lane, -cum * LOG2_E, 0.0)
    for b in range(blocks):
        bmax_ref[pl.ds(j * blocks + b, 1), :] = jnp.max(
            bias[b * MOBA_BLOCK:(b + 1) * MOBA_BLOCK], axis=0, keepdims=True)
    rest = bias
    placed = jnp.zeros((tm, LANES), _F32)
    for piece in range(BIAS_PIECES):
        part = rest.astype(_BF16).astype(_F32)
        rest = rest - part
        placed = placed + (pltpu.roll(part, N_FOX_HEADS * piece, 1) if piece else part)
    cb_ref[...] = placed.astype(_BF16)

    def proj(c):
        return jnp.dot(hb, w_ref[:, c * width:(c + 1) * width], preferred_element_type=_F32)

    def proj_t(c):
        return jnp.dot(wt_ref[c * width:(c + 1) * width, :], hbt, preferred_element_type=_F32)

    def rope(t):
        cos = jnp.concatenate([cos_ref[...]] * reps, axis=1)
        sin = jnp.concatenate([sin_ref[...]] * reps, axis=1)
        first = (lax.broadcasted_iota(jnp.int32, (1, width), 1) % HEAD_DIM) < half
        partner = jnp.where(first, pltpu.roll(t, width - half, 1), pltpu.roll(t, half, 1))
        return t * cos + partner * sin

    def rope_t(t):
        cos = jnp.concatenate([cost_ref[...]] * reps, axis=0)
        sin = jnp.concatenate([sint_ref[...]] * reps, axis=0)
        first = (lax.broadcasted_iota(jnp.int32, (width, 1), 0) % HEAD_DIM) < half
        partner = jnp.where(first, pltpu.roll(t, width - half, 0), pltpu.roll(t, half, 0))
        return t * cos + partner * sin

    scale = HEAD_DIM ** -0.5 * LOG2_E
    kf = proj(0).astype(_BF16)
    kf_ref[...] = kf
    kf_abs = jnp.abs(kf.astype(_F32))
    km = rope(proj(1))
    km_ref[...] = km.astype(_BF16)

    for b in range(blocks):
        rows = slice(b * MOBA_BLOCK, (b + 1) * MOBA_BLOCK)
        kmean_ref[pl.ds(j * blocks + b, 1), :] = jnp.mean(km[rows], axis=0, keepdims=True)
        kabs_ref[pl.ds(j * blocks + b, 1), :] = jnp.max(kf_abs[rows], axis=0, keepdims=True)

    for c in range(gate_cols // width):
        g = jnp.dot(hb, w_ref[:, (2 + c) * width:(3 + c) * width],
                    preferred_element_type=_F32)
        gates_ref[:, c * width:(c + 1) * width] = jax.nn.sigmoid(g).astype(_BF16)

    qft_ref[...] = (proj_t(0) * scale).astype(_BF16)
    vft_ref[...] = proj_t(1).astype(_BF16)
    qmt_ref[...] = (rope_t(proj_t(2)) * scale).astype(_BF16)
    vmt_ref[...] = proj_t(3).astype(_BF16)


def _inproj(x, g, w, wt, wf, bfp, cos, sin, cos_t, sin_t, *, tm):
    b, s, d = x.shape
    width = N_FOX_HEADS * HEAD_DIM
    gate_cols = w.shape[1] - 2 * width
    act = jax.ShapeDtypeStruct((b, s, width), _BF16)
    act_t = jax.ShapeDtypeStruct((b, width, s), _BF16)
    act_spec = pl.BlockSpec((None, tm, width), lambda bi, j: (bi, j, 0))
    act_t_spec = pl.BlockSpec((None, width, tm), lambda bi, j: (bi, 0, j))
    return pl.pallas_call(
        functools.partial(_inproj_body, tm=tm, width=width, gate_cols=gate_cols),
        grid=(b, s // tm),
        in_specs=[
            pl.BlockSpec((None, tm, d), lambda bi, j: (bi, j, 0)),
            _const_spec((1, d)),
            _const_spec(w.shape),
            _const_spec(wt.shape),
            _const_spec(wf.shape),
            _const_spec(bfp.shape),
            pl.BlockSpec((tm, LANES), lambda bi, j: (j, 0)),
            pl.BlockSpec((tm, LANES), lambda bi, j: (j, 0)),
            pl.BlockSpec((LANES, tm), lambda bi, j: (0, j)),
            pl.BlockSpec((LANES, tm), lambda bi, j: (0, j)),
        ],
        out_specs=[act_t_spec, act_spec, act_t_spec, act_t_spec, act_spec, act_t_spec,
                   pl.BlockSpec((None, tm, gate_cols), lambda bi, j: (bi, j, 0)),
                   pl.BlockSpec((None, tm, LANES), lambda bi, j: (bi, j, 0)),
                   pl.BlockSpec((None, LANES, width), lambda bi, j: (bi, 0, 0)),
                   pl.BlockSpec((None, LANES, width), lambda bi, j: (bi, 0, 0)),
                   pl.BlockSpec((None, LANES, LANES), lambda bi, j: (bi, 0, 0))],
        out_shape=[act_t, act, act_t, act_t, act, act_t,
                   jax.ShapeDtypeStruct((b, s, gate_cols), _BF16),
                   jax.ShapeDtypeStruct((b, s, LANES), _BF16),
                   jax.ShapeDtypeStruct((b, LANES, width), _F32),
                   jax.ShapeDtypeStruct((b, LANES, width), _F32),
                   jax.ShapeDtypeStruct((b, LANES, LANES), _F32)],
        scratch_shapes=[pltpu.VMEM((SUBLANES, LANES), _F32)],
        compiler_params=_params("arbitrary", "arbitrary"),
        name="inproj",
    )(x, g, w, wt, wf, bfp, cos, sin, cos_t, sin_t)


def _attn_body(uq_ref, uk_ref, qt_ref, k_ref, kx_ref, vt_ref, aux_ref, aux2_ref, tri_ref, o_ref,
               w_ref, m_ref, l_ref, acc_ref, st_ref, p_ref, alpha_ref, pmax_ref, qn_ref, *,
               mode, tq, tk, nb, n_diag, n_units, pipe, lag):
    hp = pl.program_id(1)
    wide = HEADS_PER_STEP * tq
    nq = w_ref.shape[0]
    feat = lax.broadcasted_iota(jnp.int32, (LANES, 1), 0)
    xr = lax.broadcasted_iota(jnp.int32, (LANES, wide), 0)
    xc = lax.broadcasted_iota(jnp.int32, (LANES, wide), 1)

    def setup(i, carry):
        qt = qt_ref[:, pl.ds(pl.multiple_of(i * tq, tq), tq)]
        zero = jnp.zeros_like(qt)
        wq = jnp.concatenate([jnp.where(feat < HEAD_DIM, qt, zero),
                              jnp.where(feat < HEAD_DIM, zero, qt)], axis=1)
        if mode == "fox":
            head = hp * HEADS_PER_STEP + xc // tq
            lower = ((xr % N_FOX_HEADS == head) & (xr < BIAS_PIECES * N_FOX_HEADS)).astype(_BF16)
            qn_ref[i] = jnp.broadcast_to(
                jnp.sum(jnp.abs(wq.astype(_F32)), axis=0, keepdims=True), qn_ref.shape[1:])
        else:
            own = (i * tq) // MOBA_BLOCK
            gate = jnp.dot(aux_ref[...].astype(_BF16), wq, preferred_element_type=_F32)[0:nb]
            blk = lax.broadcasted_iota(jnp.int32, (nb, wide), 0)
            gate = jnp.where(blk < own, gate, -jnp.inf)
            keep = blk == own
            for _ in range(MOBA_TOPK):
                mx = jnp.max(gate, axis=0, keepdims=True)
                cand = jnp.where((gate == mx) & (mx > -jnp.inf), blk, nb)
                pick = blk == jnp.min(cand, axis=0, keepdims=True)
                keep = keep | pick
                gate = jnp.where(pick, -jnp.inf, gate)
            lower = jnp.where(keep, 0.0, NEG_BIG)
            if nb < LANES:
                lower = jnp.concatenate([lower, jnp.zeros((LANES - nb, wide), _F32)], axis=0)
            lower = lower.astype(_BF16)
        w_ref[i] = jnp.concatenate([wq, lower], axis=0)
        m_ref[i] = jnp.full(m_ref.shape[1:], NEG_BIG, _F32)
        l_ref[i] = jnp.zeros(l_ref.shape[1:], _F32)
        acc_ref[i] = jnp.zeros(acc_ref.shape[1:], _F32)
        return carry

    lax.fori_loop(0, nq, setup, 0, unroll=BLOCK_LOOP_UNROLL)

    def unit(t):
        return uq_ref[t], uk_ref[t]

    def scores(t, slot, causal):
        q, kb = unit(t)
        start = pl.multiple_of(kb * tk, tk)
        ka = jnp.concatenate([k_ref[pl.ds(start, tk), :], kx_ref[pl.ds(start, tk), :]], axis=1)
        st = jnp.dot(ka, w_ref[q], preferred_element_type=_F32)
        if causal:
            st = st + tri_ref[...]
        st_ref[slot] = st
        pmax_ref[slot] = jnp.max(st.reshape(tk // SUBLANES, SUBLANES, wide), axis=0)

    def softmax(t, slot):
        q, _ = unit(t)
        rows = SOFTMAX_ROWS
        m_prev = m_ref[q][0:1, :]
        m_new = jnp.maximum(m_prev, jnp.max(pmax_ref[slot], axis=0, keepdims=True))
        for r in range(0, tk, rows):
            p_ref[slot, r:r + rows, :] = jnp.exp2(st_ref[slot, r:r + rows, :] - m_new).astype(_BF16)
        m_ref[q] = jnp.broadcast_to(m_new, m_ref.shape[1:])
        alpha_ref[slot] = jnp.broadcast_to(jnp.exp2(m_prev - m_new), alpha_ref.shape[1:])

    def values(t, slot):
        q, kb = unit(t)
        start = pl.multiple_of(kb * tk, tk)
        ones = jnp.ones((ONES_ROWS, tk), _BF16)
        for h in range(HEADS_PER_STEP):
            cols = slice(h * tq, (h + 1) * tq)
            feats = slice(h * HEAD_DIM, (h + 1) * HEAD_DIM)
            lhs = jnp.concatenate([vt_ref[feats, pl.ds(start, tk)], ones], axis=0)
            pv = jnp.dot(lhs, p_ref[slot, :, cols], preferred_element_type=_F32)
            alpha = alpha_ref[slot, 0:1, cols]
            acc_ref[q, feats, :] = acc_ref[q, feats, :] * alpha + pv[0:HEAD_DIM]
            l_ref[q, :, cols] = jnp.broadcast_to(
                alpha * l_ref[q, 0:1, cols] + pv[HEAD_DIM:HEAD_DIM + 1], (SUBLANES, tq))

    def step(base, c, causal, do_scores=True, do_softmax=True, do_values=True):
        u = base + c
        if do_scores:
            scores(u, c % pipe, causal)
        if do_values:
            values(u - 2 * lag, (c - 2 * lag) % pipe)
        if do_softmax:
            softmax(u - lag, (c - lag) % pipe)

    def group(causal):
        def body(j, carry):
            for c in range(pipe):
                step(j * pipe, c, causal)
            return carry
        return body

    def pipeline(g_first, g_end, causal_groups):
        for c in range(pipe):
            step(g_first * pipe, c, g_first < causal_groups,
                 do_softmax=c >= lag, do_values=c >= 2 * lag)
        if g_first + 1 < causal_groups:
            lax.fori_loop(g_first + 1, causal_groups, group(True), 0)
        lax.fori_loop(max(g_first + 1, causal_groups), g_end, group(False), 0)
        for c in range(2 * lag):
            step(g_end * pipe, c, False, do_scores=False, do_softmax=c < lag)

    diag_groups = n_diag // pipe
    if mode == "moba":
        pipeline(0, n_units // pipe, diag_groups)
    else:
        pipeline(0, diag_groups, diag_groups)
        nq_blocks = n_diag
        col = lax.broadcasted_iota(jnp.int32, (LANES, LANES), 1)
        first_head = lax.broadcasted_iota(jnp.int32, (1, wide), 1) < tq
        head0 = hp * HEADS_PER_STEP

        def per_head(table, pick0, pick1, fill):
            v0 = jnp.max(jnp.where(pick0, table, fill), axis=1, keepdims=True)[0:nb]
            v1 = jnp.max(jnp.where(pick1, table, fill), axis=1, keepdims=True)[0:nb]
            return jnp.where(first_head, v0, v1)

        bias_max = per_head(aux2_ref[...], col == head0, col == head0 + 1, NEG_BIG)
        k_absmax = per_head(aux_ref[...], col < HEAD_DIM, col >= HEAD_DIM, 0.0)
        blk = lax.broadcasted_iota(jnp.int32, (nb, 1), 0)

        needed = jnp.zeros((nb, 1), jnp.int32)
        for i in range(1, nq_blocks):
            bound = bias_max + qn_ref[i, 0:1, :] * k_absmax - m_ref[i, 0:1, :]
            live = jnp.max(bound, axis=1, keepdims=True) >= -UNDERFLOW_LOG2
            needed = jnp.maximum(needed, jnp.where(live & (blk < i), i - blk, 0))
        far = jnp.max(needed)
        n_live = far * nq_blocks - (far * (far + 1)) // 2
        live_groups = (n_live + pipe - 1) // pipe

        @pl.when(live_groups > 0)
        def _():
            pipeline(diag_groups, diag_groups + live_groups, diag_groups)

    def finish(i, carry):
        out_t = jnp.concatenate(
            [acc_ref[i, h * HEAD_DIM:(h + 1) * HEAD_DIM, :] / l_ref[i, 0:1, h * tq:(h + 1) * tq]
             for h in range(HEADS_PER_STEP)], axis=0)
        o_ref[pl.ds(pl.multiple_of(i * tq, tq), tq), :] = out_t.T.astype(o_ref.dtype)
        return carry

    lax.fori_loop(0, nq, finish, 0, unroll=BLOCK_LOOP_UNROLL)


def _attention(qt, k, kx, vt, aux, aux2, *, mode):
    b, width, s = qt.shape
    tq = ATTN_Q
    tk, pipe, lag = FOX_PIPELINE if mode == "fox" else MOBA_PIPELINE
    nq = s // tq
    nb = s // MOBA_BLOCK
    wide = HEADS_PER_STEP * tq
    diag = [(i * tq) // tk for i in range(nq)]
    units = list(enumerate(diag))
    units += [(i, diag[i] - dist) for dist in range(1, nq) for i in range(nq) if diag[i] >= dist]
    assert mode != "fox" or tq == tk
    assert nq % pipe == 0 and len(units) % pipe == 0 and pipe >= 2 * lag
    key_row = jnp.arange(tk, dtype=jnp.int32)[:, None]
    qry_col = jnp.arange(wide, dtype=jnp.int32)[None, :] % tq
    causal_bias = jnp.where(key_row <= qry_col, 0.0, NEG_BIG).astype(_F32)
    unit_q = jnp.asarray([u[0] for u in units], jnp.int32)
    unit_k = jnp.asarray([u[1] for u in units], jnp.int32)
    aux_spec = pl.BlockSpec((None, LANES, LANES), lambda bi, hp, uq, uk: (bi, 0, hp))
    aux2_spec = pl.BlockSpec((None, LANES, LANES), lambda bi, hp, uq, uk: (bi, 0, 0))
    if mode == "fox":
        kx_spec = pl.BlockSpec((None, s, LANES), lambda bi, hp, uq, uk: (bi, 0, 0))
    else:
        kx_spec = pl.BlockSpec(kx.shape, lambda bi, hp, uq, uk: (0, 0))
    return pl.pallas_call(
        functools.partial(_attn_body, mode=mode, tq=tq, tk=tk, nb=nb, n_diag=nq, n_units=len(units),
                          pipe=pipe, lag=lag),
        grid_spec=pltpu.PrefetchScalarGridSpec(
            num_scalar_prefetch=2,
            grid=(b, width // LANES),
            in_specs=[
                pl.BlockSpec((None, LANES, s), lambda bi, hp, uq, uk: (bi, hp, 0)),
                pl.BlockSpec((None, s, LANES), lambda bi, hp, uq, uk: (bi, 0, hp)),
                kx_spec,
                pl.BlockSpec((None, LANES, s), lambda bi, hp, uq, uk: (bi, hp, 0)),
                aux_spec,
                aux2_spec,
                pl.BlockSpec((tk, wide), lambda bi, hp, uq, uk: (0, 0)),
            ],
            out_specs=pl.BlockSpec((None, s, LANES), lambda bi, hp, uq, uk: (bi, 0, hp)),
            scratch_shapes=[
                pltpu.VMEM((nq, 2 * LANES, wide), _BF16),
                pltpu.VMEM((nq, SUBLANES, wide), _F32),
                pltpu.VMEM((nq, SUBLANES, wide), _F32),
                pltpu.VMEM((nq, LANES, tq), _F32),
                pltpu.VMEM((pipe, tk, wide), _F32),
                pltpu.VMEM((pipe, tk, wide), _BF16),
                pltpu.VMEM((pipe, SUBLANES, wide), _F32),
                pltpu.VMEM((pipe, SUBLANES, wide), _F32),
                pltpu.VMEM((nq, SUBLANES, wide), _F32),
            ],
        ),
        out_shape=jax.ShapeDtypeStruct((b, s, width), _BF16),
        compiler_params=_params("arbitrary", "arbitrary"),
        name=mode,
    )(unit_q, unit_k, qt, k, kx, vt, aux, aux2, causal_bias)


def _pack_bf16_pair(a, b):
    lo = pltpu.bitcast(a.astype(_BF16).astype(_F32), jnp.uint32) >> 16
    hi = pltpu.bitcast(b.astype(_BF16).astype(_F32), jnp.uint32) & jnp.uint32(0xFFFF0000)
    return lo | hi


def _unpack_bf16_pair(u):
    lo = pltpu.bitcast(u << 16, _F32)
    hi = pltpu.bitcast(u & jnp.uint32(0xFFFF0000), _F32)
    return jnp.concatenate([lo, hi], axis=1)


def _postattn_body(yf_ref, ym_ref, gates_ref, x_ref, wfb_ref, wmb_ref, wout_ref, g_ref, wr_ref, br_ref,
                   x1_ref, h2_ref, route_ref, meta_ref, counts_ref, carry_ref, *, tm, d):
    step = pl.program_id(0)
    ya = jnp.dot(yf_ref[...], wfb_ref[...], preferred_element_type=_F32)
    yb = jnp.dot(ym_ref[...], wmb_ref[...], preferred_element_type=_F32)
    mixed = gates_ref[:, 0:d].astype(_F32) * ya + gates_ref[:, d:2 * d].astype(_F32) * yb
    x1 = x_ref[...] + jnp.dot(mixed.astype(_BF16), wout_ref[...], preferred_element_type=_F32)
    x1_ref[...] = x1
    h2 = _rms(x1, g_ref[...])
    h2_ref[...] = _pack_bf16_pair(h2[:, 0:d // 2], h2[:, d // 2:d])

    logits = jnp.dot(h2.astype(_BF16), wr_ref[...], preferred_element_type=_F32) + br_ref[...]
    logits_t = logits.T
    gl = logits_t[N_EXPERTS:N_EXPERTS + N_GROUPS]
    ge = jnp.exp(gl - jnp.max(gl, axis=0, keepdims=True))
    gp = ge / jnp.sum(ge, axis=0, keepdims=True)
    g_top = jnp.max(gp, axis=0, keepdims=True)
    g_row = lax.broadcasted_iota(jnp.int32, (N_GROUPS, tm), 0)
    g_idx = jnp.min(jnp.where(gp == g_top, g_row, N_GROUPS), axis=0, keepdims=True)
    fl = logits_t[0:EXPERTS_PER_GROUP]
    for g in range(1, N_GROUPS):
        fl = jnp.where(g_idx == g, logits_t[g * EXPERTS_PER_GROUP:(g + 1) * EXPERTS_PER_GROUP], fl)
    f_row = lax.broadcasted_iota(jnp.int32, (EXPERTS_PER_GROUP, tm), 0)
    f1 = jnp.max(fl, axis=0, keepdims=True)
    i1 = jnp.min(jnp.where(fl == f1, f_row, EXPERTS_PER_GROUP), axis=0, keepdims=True)
    fl2 = jnp.where(f_row == i1, -jnp.inf, fl)
    f2 = jnp.max(fl2, axis=0, keepdims=True)
    i2 = jnp.min(jnp.where(fl2 == f2, f_row, EXPERTS_PER_GROUP), axis=0, keepdims=True)
    e1 = g_idx * EXPERTS_PER_GROUP + i1
    e2 = g_idx * EXPERTS_PER_GROUP + i2
    t2 = jnp.exp(f2 - f1)
    w1 = g_top * (1.0 / (1.0 + t2))
    w2 = g_top * (t2 / (1.0 + t2))

    @pl.when(step == 0)
    def _():
        carry_ref[...] = jnp.zeros_like(carry_ref)

    x_row = lax.broadcasted_iota(jnp.int32, (N_EXPERTS, tm), 0)
    onehot = ((x_row == e1) | (x_row == e2)).astype(_F32)
    rr = lax.broadcasted_iota(jnp.int32, (tm, tm), 0)
    cc = lax.broadcasted_iota(jnp.int32, (tm, tm), 1)
    before = jnp.dot(onehot.astype(_BF16), (rr < cc).astype(_BF16), preferred_element_type=_F32)
    before = before + carry_ref[0:N_EXPERTS, 0:1]
    r1 = jnp.sum(jnp.where(x_row == e1, before, 0.0), axis=0, keepdims=True)
    r2 = jnp.sum(jnp.where(x_row == e2, before, 0.0), axis=0, keepdims=True)
    total = carry_ref[0:N_EXPERTS, 0:1] + jnp.sum(onehot, axis=1, keepdims=True)
    carry_ref[0:N_EXPERTS, :] = jnp.broadcast_to(total, (N_EXPERTS, LANES))
    counts_ref[...] = carry_ref[...]

    m_row = lax.broadcasted_iota(jnp.int32, (SUBLANES, tm), 0)
    meta = jnp.where(m_row == 2, e1, 0)
    meta = jnp.where(m_row == 3, e2, meta)
    meta = jnp.where(m_row == 4, r1.astype(jnp.int32), meta)
    meta = jnp.where(m_row == 5, r2.astype(jnp.int32), meta)
    meta_ref[...] = meta
    w_row = lax.broadcasted_iota(jnp.int32, (LANES, tm), 0)
    route_ref[...] = jnp.where(w_row == 0, w1, jnp.where(w_row == 1, w2, 0.0)).T


def _postattn(yf, ym, gates, x, wfb, wmb, wout, g, wr, br, *, tm):
    t, d = x.shape
    width = yf.shape[1]
    row = lambda cols: pl.BlockSpec((tm, cols), lambda i: (i, 0))
    return pl.pallas_call(
        functools.partial(_postattn_body, tm=tm, d=d),
        grid=(t // tm,),
        in_specs=[row(width), row(width), row(2 * d), row(d),
                  _const_spec(wfb.shape), _const_spec(wmb.shape), _const_spec(wout.shape),
                  _const_spec((1, d)), _const_spec(wr.shape), _const_spec(br.shape)],
        out_specs=[row(d), row(d // 2), row(LANES), pl.BlockSpec((SUBLANES, tm), lambda i: (0, i)),
                   _const_spec((LANES, LANES))],
        out_shape=[jax.ShapeDtypeStruct((t, d), _F32),
                   jax.ShapeDtypeStruct((t, d // 2), jnp.uint32),
                   jax.ShapeDtypeStruct((t, LANES), _F32),
                   jax.ShapeDtypeStruct((SUBLANES, t), jnp.int32),
                   jax.ShapeDtypeStruct((LANES, LANES), _F32)],
        scratch_shapes=[pltpu.VMEM((LANES, LANES), _F32)],
        compiler_params=_params("arbitrary"),
        name="postattn",
    )(yf, ym, gates, x, wfb, wmb, wout, g, wr, br)


def _sc_scatter(rows, idx, n_out):
    n, width = rows.shape
    workers = SC_CORES * SC_SUBCORES
    per_worker = n // workers
    chunks = per_worker // SC_SCATTER_CHUNK
    assert n % (workers * SC_SCATTER_CHUNK) == 0 and idx.shape == (2, n // SC_SCATTER_CHUNK, SC_SCATTER_CHUNK)
    mesh = plsc.VectorSubcoreMesh(core_axis_name="c", subcore_axis_name="s",
                                  num_cores=SC_CORES, num_subcores=SC_SUBCORES)

    @functools.partial(
        pl.kernel, out_type=jax.ShapeDtypeStruct((n_out, width), rows.dtype), mesh=mesh,
        scratch_types=[pltpu.VMEM((2, chunks, SC_SCATTER_CHUNK), jnp.int32),
                       pltpu.VMEM((SC_SCATTER_CHUNK, width), rows.dtype),
                       pltpu.SemaphoreType.DMA],
        name="sc_scatter")
    def scatter(rows_hbm, idx_hbm, out_hbm, idx_v, rows_v, sem):
        worker = lax.axis_index("s") * SC_CORES + lax.axis_index("c")
        for k in range(2):
            pltpu.sync_copy(idx_hbm.at[k, pl.ds(worker * chunks, chunks)], idx_v.at[k])

        def chunk(j, carry):
            start = worker * per_worker + j * SC_SCATTER_CHUNK
            pltpu.sync_copy(rows_hbm.at[pl.ds(start, SC_SCATTER_CHUNK)], rows_v)
            for k in range(2):
                pltpu.async_copy(rows_v, out_hbm.at[idx_v.at[k, j]], sem).wait()
            return carry

        lax.fori_loop(0, chunks, chunk, 0)

    return scatter(rows, idx)


def _experts_body(be_ref, valid_ref, nused_ref, xs_ref, wg_ref, wu_ref, wd_ref, out_ref,
                  wgb_ref, wub_ref, wdb_ref):
    j = pl.program_id(0)
    used = j < nused_ref[0]

    @pl.when(jnp.logical_not(used))
    def _():
        out_ref[...] = jnp.zeros_like(out_ref)

    @pl.when(used & ((j == 0) | (be_ref[j] != be_ref[jnp.maximum(j - 1, 0)])))
    def _():
        wgb_ref[...] = wg_ref[...].astype(_BF16)
        wub_ref[...] = wu_ref[...].astype(_BF16)
        wdb_ref[...] = wd_ref[...].astype(_BF16)

    @pl.when(used)
    def _():
        row = lax.broadcasted_iota(jnp.int32, xs_ref.shape, 0)
        packed = jnp.where(row < valid_ref[j], xs_ref[...], jnp.uint32(0))
        rows = _unpack_bf16_pair(packed).astype(_BF16)
        hidden = wgb_ref.shape[1]
        out = None
        for c in range(0, hidden, hidden // EXPERT_PARTS):
            cols = slice(c, c + hidden // EXPERT_PARTS)
            gate = jnp.dot(rows, wgb_ref[:, cols], preferred_element_type=_F32)
            up = jnp.dot(rows, wub_ref[:, cols], preferred_element_type=_F32)
            act = ((gate * jax.nn.sigmoid(gate)) * up).astype(_BF16)
            part = jnp.dot(act, wdb_ref[cols, :], preferred_element_type=_F32)
            out = part if out is None else out + part
        half = out.shape[1] // 2
        out_ref[...] = _pack_bf16_pair(out[:, 0:half], out[:, half:])


def _experts(block_expert, block_valid, n_used, xs, wg, wu, wd, *, te):
    n_rows, half = xs.shape
    _, d, de = wg.shape

    def blk(j, be, bv, nu):
        return jnp.minimum(j, nu[0] - 1)

    def weight(j, be, bv, nu):
        return (be[blk(j, be, bv, nu)], 0, 0)

    return pl.pallas_call(
        _experts_body,
        grid_spec=pltpu.PrefetchScalarGridSpec(
            num_scalar_prefetch=3,
            grid=(n_rows // te,),
            in_specs=[pl.BlockSpec((te, half), lambda j, be, bv, nu: (blk(j, be, bv, nu), 0)),
                      pl.BlockSpec((None, d, de), weight),
                      pl.BlockSpec((None, d, de), weight),
                      pl.BlockSpec((None, de, d), weight)],
            out_specs=pl.BlockSpec((te, d // 2), lambda j, be, bv, nu: (j, 0)),
            scratch_shapes=[pltpu.VMEM((d, de), _BF16), pltpu.VMEM((d, de), _BF16),
                            pltpu.VMEM((de, d), _BF16)],
        ),
        out_shape=jax.ShapeDtypeStruct((n_rows, d // 2), jnp.uint32),
        compiler_params=_params("arbitrary"),
        name="experts",
    )(block_expert, block_valid, n_used, xs, wg, wu, wd)


def _sc_gather(table, idx):
    n, = idx.shape
    width = table.shape[1]
    workers = SC_CORES * SC_SUBCORES
    per_worker = n // workers
    chunks = per_worker // SC_CHUNK
    assert n % (workers * SC_CHUNK * 2) == 0
    mesh = plsc.VectorSubcoreMesh(core_axis_name="c", subcore_axis_name="s",
                                  num_cores=SC_CORES, num_subcores=SC_SUBCORES)

    @functools.partial(
        pl.kernel, out_type=jax.ShapeDtypeStruct((n, width), table.dtype), mesh=mesh,
        scratch_types=[pltpu.VMEM((per_worker,), jnp.int32),
                       pltpu.VMEM((2, SC_CHUNK, width), table.dtype),
                       pltpu.SemaphoreType.DMA((2,))],
        name="sc_gather")
    def gather(table_hbm, idx_hbm, out_hbm, idx_v, rows_v, sems):
        worker = lax.axis_index("s") * SC_CORES + lax.axis_index("c")
        base = worker * per_worker
        pltpu.sync_copy(idx_hbm.at[pl.ds(base, per_worker)], idx_v)

        def fetch(j, slot):
            off = pl.multiple_of(j * SC_CHUNK, SC_CHUNK)
            return pltpu.async_copy(table_hbm.at[idx_v.at[pl.ds(off, SC_CHUNK)]], rows_v.at[slot],
                                    sems.at[slot])

        def drain(j, slot):
            off = pl.multiple_of(j * SC_CHUNK, SC_CHUNK)
            pltpu.make_async_copy(out_hbm.at[pl.ds(base + off, SC_CHUNK)], rows_v.at[slot],
                                  sems.at[slot]).wait()
            pltpu.sync_copy(rows_v.at[slot], out_hbm.at[pl.ds(base + off, SC_CHUNK)])

        fetch(0, 0)

        def pair(p, carry):
            fetch(2 * p + 1, 1)
            drain(2 * p, 0)

            @pl.when(p + 1 < chunks // 2)
            def _():
                fetch(2 * p + 2, 0)

            drain(2 * p + 1, 1)
            return carry

        lax.fori_loop(0, chunks // 2, pair, 0)

    return gather(table, idx)


def _final_body(x1_ref, route_ref, p_ref, r1_ref, r2_ref, gp_ref, wpg_ref, wpp_ref, gf_ref, o_ref):
    emb = jnp.dot(p_ref[...].astype(_BF16), wpp_ref[...], preferred_element_type=_F32)
    route = route_ref[...]
    x2 = x1_ref[...] + (_unpack_bf16_pair(r1_ref[...]) * route[:, 0:1]
                        + _unpack_bf16_pair(r2_ref[...]) * route[:, 1:2])
    gate = jax.nn.sigmoid(jnp.dot(_rms(x2, gp_ref[...]).astype(_BF16), wpg_ref[...],
                                  preferred_element_type=_F32))
    o_ref[...] = _rms(x2 + gate * emb, gf_ref[...])


def _final(x1, route, p, slot_rows, gp, wpg, wpp, gf, *, tm):
    t, d = x1.shape
    ple = p.shape[1]
    steps = t // tm
    row = lambda cols: pl.BlockSpec((tm, cols), lambda i: (i, 0))
    return pl.pallas_call(
        _final_body,
        grid=(steps,),
        in_specs=[row(d), row(LANES), row(ple), row(d // 2),
                  pl.BlockSpec((tm, d // 2), lambda i: (steps + i, 0)),
                  _const_spec((1, d)), _const_spec(wpg.shape), _const_spec(wpp.shape),
                  _const_spec((1, d))],
        out_specs=row(d),
        out_shape=jax.ShapeDtypeStruct((t, d), _F32),
        compiler_params=_params("arbitrary"),
        name="final",
    )(x1, route, p, slot_rows, slot_rows, gp, wpg, wpp, gf)


def _rope_tables(s):
    half = HEAD_DIM // 2
    inv = 1.0 / (ROPE_THETA ** (jnp.arange(0, HEAD_DIM, 2, dtype=_F32) / HEAD_DIM))
    ang = jnp.arange(s, dtype=_F32)[:, None] * inv[None, :]
    cos, sin = jnp.cos(ang), jnp.sin(ang)
    reps = LANES // half
    cos_t = jnp.tile(cos, (1, reps))
    sin_t = jnp.tile(jnp.concatenate([-sin, sin], axis=1), (1, reps // 2))
    return cos_t, sin_t


def _layer(x, p, attn_norm, w_in, b_forget, w_fox_branch, w_moba_branch, w_out, moe_norm,
           w_group, b_group, w_fine, b_fine, w_gate, w_up, w_down, ple_norm, w_ple_gate, w_ple_proj,
           final_gain):
    b, s, d = x.shape
    t = b * s
    width = N_FOX_HEADS * HEAD_DIM
    assert N_MOBA_HEADS * HEAD_DIM == width and MOBA_BLOCK % ATTN_Q == 0 and MOBA_PIPELINE[0] % MOBA_BLOCK == 0
    assert BIAS_PIECES * N_FOX_HEADS <= LANES and s % MOBA_PIPELINE[0] == 0
    assert s % ROW_TILE == 0 and ROW_TILE % MOBA_BLOCK == 0 and s // MOBA_BLOCK <= LANES
    assert t % FINAL_TILE == 0 and t % POSTATTN_TILE == 0 and N_EXPERTS + N_GROUPS <= LANES

    qkv_cols = 6 * width
    f_cols = N_FOX_HEADS
    chunk = lambda c: w_in[:, c * width:(c + 1) * width]
    w_rows = jnp.concatenate([chunk(1), chunk(4), w_in[:, qkv_cols + f_cols:]], axis=1).astype(_BF16)
    w_cols = jnp.concatenate([chunk(0), chunk(2), chunk(3), chunk(5)], axis=1).T.astype(_BF16)
    wf = jnp.zeros((d, LANES), _BF16).at[:, :f_cols].set(w_in[:, qkv_cols:qkv_cols + f_cols].astype(_BF16))
    bfp = jnp.zeros((1, LANES), _F32).at[0, :f_cols].set(b_forget.astype(_F32))
    cos, sin = _rope_tables(s)

    qft, kf, vft, qmt, km, vmt, gates, cb, kmean, kabs, bmax = _inproj(
        x, attn_norm.reshape(1, d), w_rows, w_cols, wf, bfp, cos, sin, cos.T, sin.T, tm=ROW_TILE)
    y_fox = _attention(qft, kf, cb, vft, kabs, bmax, mode="fox")
    block_of_key = jnp.arange(s, dtype=jnp.int32)[:, None] // MOBA_BLOCK
    block_onehot = (block_of_key == jnp.arange(LANES, dtype=jnp.int32)[None, :]).astype(_BF16)
    y_moba = _attention(qmt, km, block_onehot, vmt, kmean, bmax, mode="moba")

    wr = jnp.zeros((d, LANES), _BF16)
    wr = wr.at[:, :N_EXPERTS].set(w_fine.astype(_BF16))
    wr = wr.at[:, N_EXPERTS:N_EXPERTS + N_GROUPS].set(w_group.astype(_BF16))
    br = jnp.zeros((1, LANES), _F32)
    br = br.at[0, :N_EXPERTS].set(b_fine.astype(_F32))
    br = br.at[0, N_EXPERTS:N_EXPERTS + N_GROUPS].set(b_group.astype(_F32))
    x1, h2p, route, meta, counts = _postattn(
        y_fox.reshape(t, width), y_moba.reshape(t, width), gates.reshape(t, 2 * d), x.reshape(t, d),
        w_fox_branch.astype(_BF16), w_moba_branch.astype(_BF16), w_out.astype(_BF16),
        moe_norm.reshape(1, d), wr, br, tm=POSTATTN_TILE)

    te = EXPERT_TILE
    n_blk = (2 * t) // te + N_EXPERTS
    counts = counts[:N_EXPERTS, 0].astype(jnp.int32)
    blocks_per = (counts + te - 1) // te
    block_end = jnp.cumsum(blocks_per)
    row_start = (block_end - blocks_per) * te
    expert_ids = jnp.arange(N_EXPERTS, dtype=jnp.int32)[:, None, None]
    slot_major = meta[4:6] + jnp.sum(
        jnp.where(meta[2:4][None] == expert_ids, row_start[:, None, None], 0), axis=0)
    n_used = block_end[-1:].astype(jnp.int32)
    block_ids = jnp.arange(n_blk, dtype=jnp.int32)
    block_expert = jnp.minimum(
        jnp.sum((block_ids[:, None] >= block_end[None, :]).astype(jnp.int32), axis=1), N_EXPERTS - 1)

    first_block = block_end - blocks_per
    block_valid = jnp.clip(counts[block_expert] - (block_ids - first_block[block_expert]) * te, 0, te)
    xs = _sc_scatter(h2p, slot_major.reshape(2, t // SC_SCATTER_CHUNK, SC_SCATTER_CHUNK), n_blk * te)
    rows = _experts(block_expert, block_valid, n_used, xs, w_gate, w_up, w_down, te=te)
    slot_rows = _sc_gather(rows, slot_major.reshape(-1))
    out = _final(x1, route, p.reshape(t, -1), slot_rows, ple_norm.reshape(1, d),
                 w_ple_gate.astype(_BF16), w_ple_proj.astype(_BF16), final_gain.reshape(1, d),
                 tm=FINAL_TILE)
    return out.reshape(b, s, d)


def kernel(x, p, attn_norm, w_in, b_forget, w_fox_branch, w_moba_branch, w_out, moe_norm, w_group,
           b_group, w_fine, b_fine, w_gate, w_up, w_down, ple_norm, w_ple_gate, w_ple_proj, final_norm):
    depth = p.shape[0]
    assert depth == 1, "the final norm is fused into the last layer's kernel"
    i = 0
    return _layer(x, p[i], attn_norm[i], w_in[i], b_forget[i], w_fox_branch[i], w_moba_branch[i],
                  w_out[i], moe_norm[i], w_group[i], b_group[i], w_fine[i], b_fine[i], w_gate[i],
                  w_up[i], w_down[i], ple_norm[i], w_ple_gate[i], w_ple_proj[i], final_norm)
```

```python
import functools

import jax
import jax.numpy as jnp
from jax import lax
from jax.experimental import pallas as pl
from jax.experimental.pallas import tpu as pltpu
from jax.experimental.pallas import tpu_sc as plsc

HEAD_DIM = 64
N_FOX_HEADS = 8
N_MOBA_HEADS = 8
MOBA_BLOCK = 256
MOBA_TOPK = 3
ROPE_THETA = 10000.0
N_GROUPS = 4
EXPERTS_PER_GROUP = 8
N_EXPERTS = N_GROUPS * EXPERTS_PER_GROUP
RMS_EPS = 1e-6

LANES = 128
SUBLANES = 8
BLOCK_LOOP_UNROLL = 16
HEADS_PER_STEP = LANES // HEAD_DIM
ATTN_Q = 256
FOX_PIPELINE = (256, 8, 4)
MOBA_PIPELINE = (256, 8, 4)
ONES_ROWS = 16
SOFTMAX_ROWS = 64
BIAS_PIECES = 3
ROW_TILE = 512
POSTATTN_TILE = 1024
SC_CORES = 2
SC_SUBCORES = 16
SC_CHUNK = 64
SC_SCATTER_CHUNK = 128
COMBINE_PARTS = 2
FINAL_TILE = 1024
EXPERT_PARTS = 2
EXPERT_TILE = 512
LOG2_E = 1.4426950408889634
UNDERFLOW_LOG2 = 160.0
NEG_BIG = -1e30
VMEM_LIMIT = 48 * 1024 * 1024

_BF16 = jnp.bfloat16
_F32 = jnp.float32


def _params(*sem):
    return pltpu.CompilerParams(dimension_semantics=sem, vmem_limit_bytes=VMEM_LIMIT)


def _rms(x, g):
    return x * lax.rsqrt(jnp.mean(x * x, axis=-1, keepdims=True) + RMS_EPS) * g


def _const_spec(shape):
    return pl.BlockSpec(shape, lambda *_: (0,) * len(shape))


def _inproj_body(x_ref, g_ref, w_ref, wt_ref, wf_ref, bf_ref, cos_ref, sin_ref, cost_ref, sint_ref,
                 qft_ref, kf_ref, vft_ref, qmt_ref, km_ref, vmt_ref, gates_ref, cb_ref, kmean_ref,
                 kabs_ref, bmax_ref,
                 carry_ref, *, tm, width, gate_cols):
    j = pl.program_id(1)
    h = _rms(x_ref[...], g_ref[...])
    hb = h.astype(_BF16)
    hbt = h.T.astype(_BF16)
    reps = width // LANES
    half = HEAD_DIM // 2

    blocks = tm // MOBA_BLOCK

    @pl.when(j == 0)
    def _():
        kmean_ref[...] = jnp.zeros_like(kmean_ref)
        kabs_ref[...] = jnp.zeros_like(kabs_ref)
        bmax_ref[...] = jnp.zeros_like(bmax_ref)
        carry_ref[...] = jnp.zeros_like(carry_ref)

    z = jnp.dot(hb, wf_ref[...], preferred_element_type=_F32) + bf_ref[...]
    log_f = jnp.minimum(z, 0.0) - jnp.log1p(jnp.exp(-jnp.abs(z)))
    r = lax.broadcasted_iota(jnp.int32, (LANES, LANES), 0)
    c = lax.broadcasted_iota(jnp.int32, (LANES, LANES), 1)
    tri = (c <= r).astype(_BF16)
    offset = carry_ref[0:1, :]
    blocks_cum = []
    for blk in range(tm // LANES):
        rest = log_f[blk * LANES:(blk + 1) * LANES]
        within = jnp.zeros((LANES, LANES), _F32)
        for _ in range(BIAS_PIECES):
            part = rest.astype(_BF16)
            rest = rest - part.astype(_F32)
            within = within + jnp.dot(tri, part, preferred_element_type=_F32)
        blocks_cum.append(within + offset)
        offset = offset + within[LANES - 1:LANES, :]
    cum = jnp.concatenate(blocks_cum, axis=0)
    carry_ref[...] = jnp.broadcast_to(offset, carry_ref.shape)

    head_lane = lax.broadcasted_iota(jnp.int32, (1, LANES), 1) < N_FOX_HEADS
    bias = jnp.where(head_lane, -cum * LOG2_E, 0.0)
    for b in range(blocks):
        bmax_ref[pl.ds(j * blocks + b, 1), :] = jnp.max(
            bias[b * MOBA_BLOCK:(b + 1) * MOBA_BLOCK], axis=0, keepdims=True)
    rest = bias
    placed = jnp.zeros((tm, LANES), _F32)
    for piece in range(BIAS_PIECES):
        part = rest.astype(_BF16).astype(_F32)
        rest = rest - part
        placed = placed + (pltpu.roll(part, N_FOX_HEADS * piece, 1) if piece else part)
    cb_ref[...] = placed.astype(_BF16)

    def proj(c):
        return jnp.dot(hb, w_ref[:, c * width:(c + 1) * width], preferred_element_type=_F32)

    def proj_t(c):
        return jnp.dot(wt_ref[c * width:(c + 1) * width, :], hbt, preferred_element_type=_F32)

    def rope(t):
        cos = jnp.concatenate([cos_ref[...]] * reps, axis=1)
        sin = jnp.concatenate([sin_ref[...]] * reps, axis=1)
        first = (lax.broadcasted_iota(jnp.int32, (1, width), 1) % HEAD_DIM) < half
        partner = jnp.where(first, pltpu.roll(t, width - half, 1), pltpu.roll(t, half, 1))
        return t * cos + partner * sin

    def rope_t(t):
        cos = jnp.concatenate([cost_ref[...]] * reps, axis=0)
        sin = jnp.concatenate([sint_ref[...]] * reps, axis=0)
        first = (lax.broadcasted_iota(jnp.int32, (width, 1), 0) % HEAD_DIM) < half
        partner = jnp.where(first, pltpu.roll(t, width - half, 0), pltpu.roll(t, half, 0))
        return t * cos + partner * sin

    scale = HEAD_DIM ** -0.5 * LOG2_E
    kf = proj(0).astype(_BF16)
    kf_ref[...] = kf
    kf_abs = jnp.abs(kf.astype(_F32))
    km = rope(proj(1))
    km_ref[...] = km.astype(_BF16)

    for b in range(blocks):
        rows = slice(b * MOBA_BLOCK, (b + 1) * MOBA_BLOCK)
        kmean_ref[pl.ds(j * blocks + b, 1), :] = jnp.mean(km[rows], axis=0, keepdims=True)
        kabs_ref[pl.ds(j * blocks + b, 1), :] = jnp.max(kf_abs[rows], axis=0, keepdims=True)

    for c in range(gate_cols // width):
        g = jnp.dot(hb, w_ref[:, (2 + c) * width:(3 + c) * width],
                    preferred_element_type=_F32)
        gates_ref[:, c * width:(c + 1) * width] = jax.nn.sigmoid(g).astype(_BF16)

    qft_ref[...] = (proj_t(0) * scale).astype(_BF16)
    vft_ref[...] = proj_t(1).astype(_BF16)
    qmt_ref[...] = (rope_t(proj_t(2)) * scale).astype(_BF16)
    vmt_ref[...] = proj_t(3).astype(_BF16)


def _inproj(x, g, w, wt, wf, bfp, cos, sin, cos_t, sin_t, *, tm):
    b, s, d = x.shape
    width = N_FOX_HEADS * HEAD_DIM
    gate_cols = w.shape[1] - 2 * width
    act = jax.ShapeDtypeStruct((b, s, width), _BF16)
    act_t = jax.ShapeDtypeStruct((b, width, s), _BF16)
    act_spec = pl.BlockSpec((None, tm, width), lambda bi, j: (bi, j, 0))
    act_t_spec = pl.BlockSpec((None, width, tm), lambda bi, j: (bi, 0, j))
    return pl.pallas_call(
        functools.partial(_inproj_body, tm=tm, width=width, gate_cols=gate_cols),
        grid=(b, s // tm),
        in_specs=[
            pl.BlockSpec((None, tm, d), lambda bi, j: (bi, j, 0)),
            _const_spec((1, d)),
            _const_spec(w.shape),
            _const_spec(wt.shape),
            _const_spec(wf.shape),
            _const_spec(bfp.shape),
            pl.BlockSpec((tm, LANES), lambda bi, j: (j, 0)),
            pl.BlockSpec((tm, LANES), lambda bi, j: (j, 0)),
            pl.BlockSpec((LANES, tm), lambda bi, j: (0, j)),
            pl.BlockSpec((LANES, tm), lambda bi, j: (0, j)),
        ],
        out_specs=[act_t_spec, act_spec, act_t_spec, act_t_spec, act_spec, act_t_spec,
                   pl.BlockSpec((None, tm, gate_cols), lambda bi, j: (bi, j, 0)),
                   pl.BlockSpec((None, tm, LANES), lambda bi, j: (bi, j, 0)),
                   pl.BlockSpec((None, LANES, width), lambda bi, j: (bi, 0, 0)),
                   pl.BlockSpec((None, LANES, width), lambda bi, j: (bi, 0, 0)),
                   pl.BlockSpec((None, LANES, LANES), lambda bi, j: (bi, 0, 0))],
        out_shape=[act_t, act, act_t, act_t, act, act_t,
                   jax.ShapeDtypeStruct((b, s, gate_cols), _BF16),
                   jax.ShapeDtypeStruct((b, s, LANES), _BF16),
                   jax.ShapeDtypeStruct((b, LANES, width), _F32),
                   jax.ShapeDtypeStruct((b, LANES, width), _F32),
                   jax.ShapeDtypeStruct((b, LANES, LANES), _F32)],
        scratch_shapes=[pltpu.VMEM((SUBLANES, LANES), _F32)],
        compiler_params=_params("arbitrary", "arbitrary"),
        name="inproj",
    )(x, g, w, wt, wf, bfp, cos, sin, cos_t, sin_t)


def _attn_body(uq_ref, uk_ref, qt_ref, k_ref, kx_ref, vt_ref, aux_ref, aux2_ref, tri_ref, o_ref,
               w_ref, m_ref, l_ref, acc_ref, st_ref, p_ref, alpha_ref, pmax_ref, qn_ref, *,
               mode, tq, tk, nb, n_diag, n_units, pipe, lag):
    hp = pl.program_id(1)
    wide = HEADS_PER_STEP * tq
    nq = w_ref.shape[0]
    feat = lax.broadcasted_iota(jnp.int32, (LANES, 1), 0)
    xr = lax.broadcasted_iota(jnp.int32, (LANES, wide), 0)
    xc = lax.broadcasted_iota(jnp.int32, (LANES, wide), 1)

    def setup(i, carry):
        qt = qt_ref[:, pl.ds(pl.multiple_of(i * tq, tq), tq)]
        zero = jnp.zeros_like(qt)
        wq = jnp.concatenate([jnp.where(feat < HEAD_DIM, qt, zero),
                              jnp.where(feat < HEAD_DIM, zero, qt)], axis=1)
        if mode == "fox":
            head = hp * HEADS_PER_STEP + xc // tq
            lower = ((xr % N_FOX_HEADS == head) & (xr < BIAS_PIECES * N_FOX_HEADS)).astype(_BF16)
            qn_ref[i] = jnp.broadcast_to(
                jnp.sum(jnp.abs(wq.astype(_F32)), axis=0, keepdims=True), qn_ref.shape[1:])
        else:
            own = (i * tq) // MOBA_BLOCK
            gate = jnp.dot(aux_ref[...].astype(_BF16), wq, preferred_element_type=_F32)[0:nb]
            blk = lax.broadcasted_iota(jnp.int32, (nb, wide), 0)
            gate = jnp.where(blk < own, gate, -jnp.inf)
            keep = blk == own
            for _ in range(MOBA_TOPK):
                mx = jnp.max(gate, axis=0, keepdims=True)
                cand = jnp.where((gate == mx) & (mx > -jnp.inf), blk, nb)
                pick = blk == jnp.min(cand, axis=0, keepdims=True)
                keep = keep | pick
                gate = jnp.where(pick, -jnp.inf, gate)
            lower = jnp.where(keep, 0.0, NEG_BIG)
            if nb < LANES:
                lower = jnp.concatenate([lower, jnp.zeros((LANES - nb, wide), _F32)], axis=0)
            lower = lower.astype(_BF16)
        w_ref[i] = jnp.concatenate([wq, lower], axis=0)
        m_ref[i] = jnp.full(m_ref.shape[1:], NEG_BIG, _F32)
        l_ref[i] = jnp.zeros(l_ref.shape[1:], _F32)
        acc_ref[i] = jnp.zeros(acc_ref.shape[1:], _F32)
        return carry

    lax.fori_loop(0, nq, setup, 0, unroll=BLOCK_LOOP_UNROLL)

    def unit(t):
        return uq_ref[t], uk_ref[t]

    def scores(t, slot, causal):
        q, kb = unit(t)
        start = pl.multiple_of(kb * tk, tk)
        ka = jnp.concatenate([k_ref[pl.ds(start, tk), :], kx_ref[pl.ds(start, tk), :]], axis=1)
        st = jnp.dot(ka, w_ref[q], preferred_element_type=_F32)
        if causal:
            st = st + tri_ref[...]
        st_ref[slot] = st
        pmax_ref[slot] = jnp.max(st.reshape(tk // SUBLANES, SUBLANES, wide), axis=0)

    def softmax(t, slot):
        q, _ = unit(t)
        rows = SOFTMAX_ROWS
        m_prev = m_ref[q][0:1, :]
        m_new = jnp.maximum(m_prev, jnp.max(pmax_ref[slot], axis=0, keepdims=True))
        for r in range(0, tk, rows):
            p_ref[slot, r:r + rows, :] = jnp.exp2(st_ref[slot, r:r + rows, :] - m_new).astype(_BF16)
        m_ref[q] = jnp.broadcast_to(m_new, m_ref.shape[1:])
        alpha_ref[slot] = jnp.broadcast_to(jnp.exp2(m_prev - m_new), alpha_ref.shape[1:])

    def values(t, slot):
        q, kb = unit(t)
        start = pl.multiple_of(kb * tk, tk)
        ones = jnp.ones((ONES_ROWS, tk), _BF16)
        for h in range(HEADS_PER_STEP):
            cols = slice(h * tq, (h + 1) * tq)
            feats = slice(h * HEAD_DIM, (h + 1) * HEAD_DIM)
            lhs = jnp.concatenate([vt_ref[feats, pl.ds(start, tk)], ones], axis=0)
            pv = jnp.dot(lhs, p_ref[slot, :, cols], preferred_element_type=_F32)
            alpha = alpha_ref[slot, 0:1, cols]
            acc_ref[q, feats, :] = acc_ref[q, feats, :] * alpha + pv[0:HEAD_DIM]
            l_ref[q, :, cols] = jnp.broadcast_to(
                alpha * l_ref[q, 0:1, cols] + pv[HEAD_DIM:HEAD_DIM + 1], (SUBLANES, tq))

    def step(base, c, causal, do_scores=True, do_softmax=True, do_values=True):
        u = base + c
        if do_scores:
            scores(u, c % pipe, causal)
        if do_values:
            values(u - 2 * lag, (c - 2 * lag) % pipe)
        if do_softmax:
            softmax(u - lag, (c - lag) % pipe)

    def group(causal):
        def body(j, carry):
            for c in range(pipe):
                step(j * pipe, c, causal)
            return carry
        return body

    def pipeline(g_first, g_end, causal_groups):
        for c in range(pipe):
            step(g_first * pipe, c, g_first < causal_groups,
                 do_softmax=c >= lag, do_values=c >= 2 * lag)
        if g_first + 1 < causal_groups:
            lax.fori_loop(g_first + 1, causal_groups, group(True), 0)
        lax.fori_loop(max(g_first + 1, causal_groups), g_end, group(False), 0)
        for c in range(2 * lag):
            step(g_end * pipe, c, False, do_scores=False, do_softmax=c < lag)

    diag_groups = n_diag // pipe
    if mode == "moba":
        pipeline(0, n_units // pipe, diag_groups)
    else:
        pipeline(0, diag_groups, diag_groups)
        nq_blocks = n_diag
        col = lax.broadcasted_iota(jnp.int32, (LANES, LANES), 1)
        first_head = lax.broadcasted_iota(jnp.int32, (1, wide), 1) < tq
        head0 = hp * HEADS_PER_STEP

        def per_head(table, pick0, pick1, fill):
            v0 = jnp.max(jnp.where(pick0, table, fill), axis=1, keepdims=True)[0:nb]
            v1 = jnp.max(jnp.where(pick1, table, fill), axis=1, keepdims=True)[0:nb]
            return jnp.where(first_head, v0, v1)

        bias_max = per_head(aux2_ref[...], col == head0, col == head0 + 1, NEG_BIG)
        k_absmax = per_head(aux_ref[...], col < HEAD_DIM, col >= HEAD_DIM, 0.0)
        blk = lax.broadcasted_iota(jnp.int32, (nb, 1), 0)

        needed = jnp.zeros((nb, 1), jnp.int32)
        for i in range(1, nq_blocks):
            bound = bias_max + qn_ref[i, 0:1, :] * k_absmax - m_ref[i, 0:1, :]
            live = jnp.max(bound, axis=1, keepdims=True) >= -UNDERFLOW_LOG2
            needed = jnp.maximum(needed, jnp.where(live & (blk < i), i - blk, 0))
        far = jnp.max(needed)
        n_live = far * nq_blocks - (far * (far + 1)) // 2
        live_groups = (n_live + pipe - 1) // pipe

        @pl.when(live_groups > 0)
        def _():
            pipeline(diag_groups, diag_groups + live_groups, diag_groups)

    def finish(i, carry):
        out_t = jnp.concatenate(
            [acc_ref[i, h * HEAD_DIM:(h + 1) * HEAD_DIM, :] / l_ref[i, 0:1, h * tq:(h + 1) * tq]
             for h in range(HEADS_PER_STEP)], axis=0)
        o_ref[pl.ds(pl.multiple_of(i * tq, tq), tq), :] = out_t.T.astype(o_ref.dtype)
        return carry

    lax.fori_loop(0, nq, finish, 0, unroll=BLOCK_LOOP_UNROLL)


def _attention(qt, k, kx, vt, aux, aux2, *, mode):
    b, width, s = qt.shape
    tq = ATTN_Q
    tk, pipe, lag = FOX_PIPELINE if mode == "fox" else MOBA_PIPELINE
    nq = s // tq
    nb = s // MOBA_BLOCK
    wide = HEADS_PER_STEP * tq
    diag = [(i * tq) // tk for i in range(nq)]
    units = list(enumerate(diag))
    units += [(i, diag[i] - dist) for dist in range(1, nq) for i in range(nq) if diag[i] >= dist]
    assert mode != "fox" or tq == tk
    assert nq % pipe == 0 and len(units) % pipe == 0 and pipe >= 2 * lag
    key_row = jnp.arange(tk, dtype=jnp.int32)[:, None]
    qry_col = jnp.arange(wide, dtype=jnp.int32)[None, :] % tq
    causal_bias = jnp.where(key_row <= qry_col, 0.0, NEG_BIG).astype(_F32)
    unit_q = jnp.asarray([u[0] for u in units], jnp.int32)
    unit_k = jnp.asarray([u[1] for u in units], jnp.int32)
    aux_spec = pl.BlockSpec((None, LANES, LANES), lambda bi, hp, uq, uk: (bi, 0, hp))
    aux2_spec = pl.BlockSpec((None, LANES, LANES), lambda bi, hp, uq, uk: (bi, 0, 0))
    if mode == "fox":
        kx_spec = pl.BlockSpec((None, s, LANES), lambda bi, hp, uq, uk: (bi, 0, 0))
    else:
        kx_spec = pl.BlockSpec(kx.shape, lambda bi, hp, uq, uk: (0, 0))
    return pl.pallas_call(
        functools.partial(_attn_body, mode=mode, tq=tq, tk=tk, nb=nb, n_diag=nq, n_units=len(units),
                          pipe=pipe, lag=lag),
        grid_spec=pltpu.PrefetchScalarGridSpec(
            num_scalar_prefetch=2,
            grid=(b, width // LANES),
            in_specs=[
                pl.BlockSpec((None, LANES, s), lambda bi, hp, uq, uk: (bi, hp, 0)),
                pl.BlockSpec((None, s, LANES), lambda bi, hp, uq, uk: (bi, 0, hp)),
                kx_spec,
                pl.BlockSpec((None, LANES, s), lambda bi, hp, uq, uk: (bi, hp, 0)),
                aux_spec,
                aux2_spec,
                pl.BlockSpec((tk, wide), lambda bi, hp, uq, uk: (0, 0)),
            ],
            out_specs=pl.BlockSpec((None, s, LANES), lambda bi, hp, uq, uk: (bi, 0, hp)),
            scratch_shapes=[
                pltpu.VMEM((nq, 2 * LANES, wide), _BF16),
                pltpu.VMEM((nq, SUBLANES, wide), _F32),
                pltpu.VMEM((nq, SUBLANES, wide), _F32),
                pltpu.VMEM((nq, LANES, tq), _F32),
                pltpu.VMEM((pipe, tk, wide), _F32),
                pltpu.VMEM((pipe, tk, wide), _BF16),
                pltpu.VMEM((pipe, SUBLANES, wide), _F32),
                pltpu.VMEM((pipe, SUBLANES, wide), _F32),
                pltpu.VMEM((nq, SUBLANES, wide), _F32),
            ],
        ),
        out_shape=jax.ShapeDtypeStruct((b, s, width), _BF16),
        compiler_params=_params("arbitrary", "arbitrary"),
        name=mode,
    )(unit_q, unit_k, qt, k, kx, vt, aux, aux2, causal_bias)


def _pack_bf16_pair(a, b):
    lo = pltpu.bitcast(a.astype(_BF16).astype(_F32), jnp.uint32) >> 16
    hi = pltpu.bitcast(b.astype(_BF16).astype(_F32), jnp.uint32) & jnp.uint32(0xFFFF0000)
    return lo | hi


def _unpack_bf16_pair(u):
    lo = pltpu.bitcast(u << 16, _F32)
    hi = pltpu.bitcast(u & jnp.uint32(0xFFFF0000), _F32)
    return jnp.concatenate([lo, hi], axis=1)


def _postattn_body(yf_ref, ym_ref, gates_ref, x_ref, wfb_ref, wmb_ref, wout_ref, g_ref, wr_ref, br_ref,
                   x1_ref, h2_ref, route_ref, meta_ref, counts_ref, carry_ref, *, tm, d):
    step = pl.program_id(0)
    ya = jnp.dot(yf_ref[...], wfb_ref[...], preferred_element_type=_F32)
    yb = jnp.dot(ym_ref[...], wmb_ref[...], preferred_element_type=_F32)
    mixed = gates_ref[:, 0:d].astype(_F32) * ya + gates_ref[:, d:2 * d].astype(_F32) * yb
    x1 = x_ref[...] + jnp.dot(mixed.astype(_BF16), wout_ref[...], preferred_element_type=_F32)
    x1_ref[...] = x1
    h2 = _rms(x1, g_ref[...])
    h2_ref[...] = _pack_bf16_pair(h2[:, 0:d // 2], h2[:, d // 2:d])

    logits = jnp.dot(h2.astype(_BF16), wr_ref[...], preferred_element_type=_F32) + br_ref[...]
    logits_t = logits.T
    gl = logits_t[N_EXPERTS:N_EXPERTS + N_GROUPS]
    ge = jnp.exp(gl - jnp.max(gl, axis=0, keepdims=True))
    gp = ge / jnp.sum(ge, axis=0, keepdims=True)
    g_top = jnp.max(gp, axis=0, keepdims=True)
    g_row = lax.broadcasted_iota(jnp.int32, (N_GROUPS, tm), 0)
    g_idx = jnp.min(jnp.where(gp == g_top, g_row, N_GROUPS), axis=0, keepdims=True)
    fl = logits_t[0:EXPERTS_PER_GROUP]
    for g in range(1, N_GROUPS):
        fl = jnp.where(g_idx == g, logits_t[g * EXPERTS_PER_GROUP:(g + 1) * EXPERTS_PER_GROUP], fl)
    f_row = lax.broadcasted_iota(jnp.int32, (EXPERTS_PER_GROUP, tm), 0)
    f1 = jnp.max(fl, axis=0, keepdims=True)
    i1 = jnp.min(jnp.where(fl == f1, f_row, EXPERTS_PER_GROUP), axis=0, keepdims=True)
    fl2 = jnp.where(f_row == i1, -jnp.inf, fl)
    f2 = jnp.max(fl2, axis=0, keepdims=True)
    i2 = jnp.min(jnp.where(fl2 == f2, f_row, EXPERTS_PER_GROUP), axis=0, keepdims=True)
    e1 = g_idx * EXPERTS_PER_GROUP + i1
    e2 = g_idx * EXPERTS_PER_GROUP + i2
    t2 = jnp.exp(f2 - f1)
    w1 = g_top * (1.0 / (1.0 + t2))
    w2 = g_top * (t2 / (1.0 + t2))

    @pl.when(step == 0)
    def _():
        carry_ref[...] = jnp.zeros_like(carry_ref)

    x_row = lax.broadcasted_iota(jnp.int32, (N_EXPERTS, tm), 0)
    onehot = ((x_row == e1) | (x_row == e2)).astype(_F32)
    rr = lax.broadcasted_iota(jnp.int32, (tm, tm), 0)
    cc = lax.broadcasted_iota(jnp.int32, (tm, tm), 1)
    before = jnp.dot(onehot.astype(_BF16), (rr < cc).astype(_BF16), preferred_element_type=_F32)
    before = before + carry_ref[0:N_EXPERTS, 0:1]
    r1 = jnp.sum(jnp.where(x_row == e1, before, 0.0), axis=0, keepdims=True)
    r2 = jnp.sum(jnp.where(x_row == e2, before, 0.0), axis=0, keepdims=True)
    total = carry_ref[0:N_EXPERTS, 0:1] + jnp.sum(onehot, axis=1, keepdims=True)
    carry_ref[0:N_EXPERTS, :] = jnp.broadcast_to(total, (N_EXPERTS, LANES))
    counts_ref[...] = carry_ref[...]

    m_row = lax.broadcasted_iota(jnp.int32, (SUBLANES, tm), 0)
    meta = jnp.where(m_row == 2, e1, 0)
    meta = jnp.where(m_row == 3, e2, meta)
    meta = jnp.where(m_row == 4, r1.astype(jnp.int32), meta)
    meta = jnp.where(m_row == 5, r2.astype(jnp.int32), meta)
    meta_ref[...] = meta
    w_row = lax.broadcasted_iota(jnp.int32, (LANES, tm), 0)
    route_ref[...] = jnp.where(w_row == 0, w1, jnp.where(w_row == 1, w2, 0.0)).T


def _postattn(yf, ym, gates, x, wfb, wmb, wout, g, wr, br, *, tm):
    t, d = x.shape
    width = yf.shape[1]
    row = lambda cols: pl.BlockSpec((tm, cols), lambda i: (i, 0))
    return pl.pallas_call(
        functools.partial(_postattn_body, tm=tm, d=d),
        grid=(t // tm,),
        in_specs=[row(width), row(width), row(2 * d), row(d),
                  _const_spec(wfb.shape), _const_spec(wmb.shape), _const_spec(wout.shape),
                  _const_spec((1, d)), _const_spec(wr.shape), _const_spec(br.shape)],
        out_specs=[row(d), row(d // 2), row(LANES), pl.BlockSpec((SUBLANES, tm), lambda i: (0, i)),
                   _const_spec((LANES, LANES))],
        out_shape=[jax.ShapeDtypeStruct((t, d), _F32),
                   jax.ShapeDtypeStruct((t, d // 2), jnp.uint32),
                   jax.ShapeDtypeStruct((t, LANES), _F32),
                   jax.ShapeDtypeStruct((SUBLANES, t), jnp.int32),
                   jax.ShapeDtypeStruct((LANES, LANES), _F32)],
        scratch_shapes=[pltpu.VMEM((LANES, LANES), _F32)],
        compiler_params=_params("arbitrary"),
        name="postattn",
    )(yf, ym, gates, x, wfb, wmb, wout, g, wr, br)


def _sc_scatter(rows, idx, n_out):
    n, width = rows.shape
    workers = SC_CORES * SC_SUBCORES
    per_worker = n // workers
    chunks = per_worker // SC_SCATTER_CHUNK
    assert n % (workers * SC_SCATTER_CHUNK) == 0 and idx.shape == (2, n // SC_SCATTER_CHUNK, SC_SCATTER_CHUNK)
    mesh = plsc.VectorSubcoreMesh(core_axis_name="c", subcore_axis_name="s",
                                  num_cores=SC_CORES, num_subcores=SC_SUBCORES)

    @functools.partial(
        pl.kernel, out_type=jax.ShapeDtypeStruct((n_out, width), rows.dtype), mesh=mesh,
        scratch_types=[pltpu.VMEM((2, chunks, SC_SCATTER_CHUNK), jnp.int32),
                       pltpu.VMEM((SC_SCATTER_CHUNK, width), rows.dtype),
                       pltpu.SemaphoreType.DMA],
        name="sc_scatter")
    def scatter(rows_hbm, idx_hbm, out_hbm, idx_v, rows_v, sem):
        worker = lax.axis_index("s") * SC_CORES + lax.axis_index("c")
        for k in range(2):
            pltpu.sync_copy(idx_hbm.at[k, pl.ds(worker * chunks, chunks)], idx_v.at[k])

        def chunk(j, carry):
            start = worker * per_worker + j * SC_SCATTER_CHUNK
            pltpu.sync_copy(rows_hbm.at[pl.ds(start, SC_SCATTER_CHUNK)], rows_v)
            for k in range(2):
                pltpu.async_copy(rows_v, out_hbm.at[idx_v.at[k, j]], sem).wait()
            return carry

        lax.fori_loop(0, chunks, chunk, 0)

    return scatter(rows, idx)


def _experts_body(be_ref, valid_ref, nused_ref, xs_ref, wg_ref, wu_ref, wd_ref, out_ref,
                  wgb_ref, wub_ref, wdb_ref):
    j = pl.program_id(0)
    used = j < nused_ref[0]

    @pl.when(jnp.logical_not(used))
    def _():
        out_ref[...] = jnp.zeros_like(out_ref)

    @pl.when(used & ((j == 0) | (be_ref[j] != be_ref[jnp.maximum(j - 1, 0)])))
    def _():
        wgb_ref[...] = wg_ref[...].astype(_BF16)
        wub_ref[...] = wu_ref[...].astype(_BF16)
        wdb_ref[...] = wd_ref[...].astype(_BF16)

    @pl.when(used)
    def _():
        row = lax.broadcasted_iota(jnp.int32, xs_ref.shape, 0)
        packed = jnp.where(row < valid_ref[j], xs_ref[...], jnp.uint32(0))
        rows = _unpack_bf16_pair(packed).astype(_BF16)
        hidden = wgb_ref.shape[1]
        out = None
        for c in range(0, hidden, hidden // EXPERT_PARTS):
            cols = slice(c, c + hidden // EXPERT_PARTS)
            gate = jnp.dot(rows, wgb_ref[:, cols], preferred_element_type=_F32)
            up = jnp.dot(rows, wub_ref[:, cols], preferred_element_type=_F32)
            act = ((gate * jax.nn.sigmoid(gate)) * up).astype(_BF16)
            part = jnp.dot(act, wdb_ref[cols, :], preferred_element_type=_F32)
            out = part if out is None else out + part
        half = out.shape[1] // 2
        out_ref[...] = _pack_bf16_pair(out[:, 0:half], out[:, half:])


def _experts(block_expert, block_valid, n_used, xs, wg, wu, wd, *, te):
    n_rows, half = xs.shape
    _, d, de = wg.shape

    def blk(j, be, bv, nu):
        return jnp.minimum(j, nu[0] - 1)

    def weight(j, be, bv, nu):
        return (be[blk(j, be, bv, nu)], 0, 0)

    return pl.pallas_call(
        _experts_body,
        grid_spec=pltpu.PrefetchScalarGridSpec(
            num_scalar_prefetch=3,
            grid=(n_rows // te,),
            in_specs=[pl.BlockSpec((te, half), lambda j, be, bv, nu: (blk(j, be, bv, nu), 0)),
                      pl.BlockSpec((None, d, de), weight),
                      pl.BlockSpec((None, d, de), weight),
                      pl.BlockSpec((None, de, d), weight)],
            out_specs=pl.BlockSpec((te, d // 2), lambda j, be, bv, nu: (j, 0)),
            scratch_shapes=[pltpu.VMEM((d, de), _BF16), pltpu.VMEM((d, de), _BF16),
                            pltpu.VMEM((de, d), _BF16)],
        ),
        out_shape=jax.ShapeDtypeStruct((n_rows, d // 2), jnp.uint32),
        compiler_params=_params("arbitrary"),
        name="experts",
    )(block_expert, block_valid, n_used, xs, wg, wu, wd)


def _sc_gather(table, idx):
    n, = idx.shape
    width = table.shape[1]
    workers = SC_CORES * SC_SUBCORES
    per_worker = n // workers
    chunks = per_worker // SC_CHUNK
    assert n % (workers * SC_CHUNK * 2) == 0
    mesh = plsc.VectorSubcoreMesh(core_axis_name="c", subcore_axis_name="s",
                                  num_cores=SC_CORES, num_subcores=SC_SUBCORES)

    @functools.partial(
        pl.kernel, out_type=jax.ShapeDtypeStruct((n, width), table.dtype), mesh=mesh,
        scratch_types=[pltpu.VMEM((per_worker,), jnp.int32),
                       pltpu.VMEM((2, SC_CHUNK, width), table.dtype),
                       pltpu.SemaphoreType.DMA((2,))],
        name="sc_gather")
    def gather(table_hbm, idx_hbm, out_hbm, idx_v, rows_v, sems):
        worker = lax.axis_index("s") * SC_CORES + lax.axis_index("c")
        base = worker * per_worker
        pltpu.sync_copy(idx_hbm.at[pl.ds(base, per_worker)], idx_v)

        def fetch(j, slot):
            off = pl.multiple_of(j * SC_CHUNK, SC_CHUNK)
            return pltpu.async_copy(table_hbm.at[idx_v.at[pl.ds(off, SC_CHUNK)]], rows_v.at[slot],
                                    sems.at[slot])

        def drain(j, slot):
            off = pl.multiple_of(j * SC_CHUNK, SC_CHUNK)
            pltpu.make_async_copy(out_hbm.at[pl.ds(base + off, SC_CHUNK)], rows_v.at[slot],
                                  sems.at[slot]).wait()
            pltpu.sync_copy(rows_v.at[slot], out_hbm.at[pl.ds(base + off, SC_CHUNK)])

        fetch(0, 0)

        def pair(p, carry):
            fetch(2 * p + 1, 1)
            drain(2 * p, 0)

            @pl.when(p + 1 < chunks // 2)
            def _():
                fetch(2 * p + 2, 0)

            drain(2 * p + 1, 1)
            return carry

        lax.fori_loop(0, chunks // 2, pair, 0)

    return gather(table, idx)


def _final_body(x1_ref, route_ref, p_ref, r1_ref, r2_ref, gp_ref, wpg_ref, wpp_ref, gf_ref, *rest):
    o_ref = rest[-1]
    emb = jnp.dot(p_ref[...].astype(_BF16), wpp_ref[...], preferred_element_type=_F32)
    route = route_ref[...]
    x2 = x1_ref[...] + (_unpack_bf16_pair(r1_ref[...]) * route[:, 0:1]
                        + _unpack_bf16_pair(r2_ref[...]) * route[:, 1:2])
    gate = jax.nn.sigmoid(jnp.dot(_rms(x2, gp_ref[...]).astype(_BF16), wpg_ref[...],
                                  preferred_element_type=_F32))
    o_ref[...] = _rms(x2 + gate * emb, gf_ref[...])


def _final(x1, route, p, slot_rows, gp, wpg, wpp, gf, out_prev, *, tm, first):
    t, d = x1.shape
    ple = p.shape[1]
    steps = slot_rows.shape[0] // (2 * tm)
    start = first // tm
    row = lambda cols: pl.BlockSpec((tm, cols), lambda i: (start + i, 0))
    prev = [] if out_prev is None else [out_prev]
    return pl.pallas_call(
        _final_body,
        grid=(steps,),
        in_specs=[row(d), row(LANES), row(ple),
                  pl.BlockSpec((tm, d // 2), lambda i: (i, 0)),
                  pl.BlockSpec((tm, d // 2), lambda i: (steps + i, 0)),
                  _const_spec((1, d)), _const_spec(wpg.shape), _const_spec(wpp.shape),
                  _const_spec((1, d))] + [pl.BlockSpec(memory_space=pl.ANY)] * len(prev),
        out_specs=row(d),
        out_shape=jax.ShapeDtypeStruct((t, d), _F32),
        input_output_aliases={9: 0} if prev else {},
        compiler_params=_params("arbitrary"),
        name="final",
    )(x1, route, p, slot_rows, slot_rows, gp, wpg, wpp, gf, *prev)


def _rope_tables(s):
    half = HEAD_DIM // 2
    inv = 1.0 / (ROPE_THETA ** (jnp.arange(0, HEAD_DIM, 2, dtype=_F32) / HEAD_DIM))
    ang = jnp.arange(s, dtype=_F32)[:, None] * inv[None, :]
    cos, sin = jnp.cos(ang), jnp.sin(ang)
    reps = LANES // half
    cos_t = jnp.tile(cos, (1, reps))
    sin_t = jnp.tile(jnp.concatenate([-sin, sin], axis=1), (1, reps // 2))
    return cos_t, sin_t


def _layer(x, p, attn_norm, w_in, b_forget, w_fox_branch, w_moba_branch, w_out, moe_norm,
           w_group, b_group, w_fine, b_fine, w_gate, w_up, w_down, ple_norm, w_ple_gate, w_ple_proj,
           final_gain):
    b, s, d = x.shape
    t = b * s
    width = N_FOX_HEADS * HEAD_DIM
    assert N_MOBA_HEADS * HEAD_DIM == width and MOBA_BLOCK % ATTN_Q == 0 and MOBA_PIPELINE[0] % MOBA_BLOCK == 0
    assert BIAS_PIECES * N_FOX_HEADS <= LANES and s % MOBA_PIPELINE[0] == 0
    assert s % ROW_TILE == 0 and ROW_TILE % MOBA_BLOCK == 0 and s // MOBA_BLOCK <= LANES
    assert t % (COMBINE_PARTS * FINAL_TILE) == 0 and t % POSTATTN_TILE == 0 and N_EXPERTS + N_GROUPS <= LANES

    qkv_cols = 6 * width
    f_cols = N_FOX_HEADS
    chunk = lambda c: w_in[:, c * width:(c + 1) * width]
    w_rows = jnp.concatenate([chunk(1), chunk(4), w_in[:, qkv_cols + f_cols:]], axis=1).astype(_BF16)
    w_cols = jnp.concatenate([chunk(0), chunk(2), chunk(3), chunk(5)], axis=1).T.astype(_BF16)
    wf = jnp.zeros((d, LANES), _BF16).at[:, :f_cols].set(w_in[:, qkv_cols:qkv_cols + f_cols].astype(_BF16))
    bfp = jnp.zeros((1, LANES), _F32).at[0, :f_cols].set(b_forget.astype(_F32))
    cos, sin = _rope_tables(s)

    qft, kf, vft, qmt, km, vmt, gates, cb, kmean, kabs, bmax = _inproj(
        x, attn_norm.reshape(1, d), w_rows, w_cols, wf, bfp, cos, sin, cos.T, sin.T, tm=ROW_TILE)
    y_fox = _attention(qft, kf, cb, vft, kabs, bmax, mode="fox")
    block_of_key = jnp.arange(s, dtype=jnp.int32)[:, None] // MOBA_BLOCK
    block_onehot = (block_of_key == jnp.arange(LANES, dtype=jnp.int32)[None, :]).astype(_BF16)
    y_moba = _attention(qmt, km, block_onehot, vmt, kmean, bmax, mode="moba")

    wr = jnp.zeros((d, LANES), _BF16)
    wr = wr.at[:, :N_EXPERTS].set(w_fine.astype(_BF16))
    wr = wr.at[:, N_EXPERTS:N_EXPERTS + N_GROUPS].set(w_group.astype(_BF16))
    br = jnp.zeros((1, LANES), _F32)
    br = br.at[0, :N_EXPERTS].set(b_fine.astype(_F32))
    br = br.at[0, N_EXPERTS:N_EXPERTS + N_GROUPS].set(b_group.astype(_F32))
    x1, h2p, route, meta, counts = _postattn(
        y_fox.reshape(t, width), y_moba.reshape(t, width), gates.reshape(t, 2 * d), x.reshape(t, d),
        w_fox_branch.astype(_BF16), w_moba_branch.astype(_BF16), w_out.astype(_BF16),
        moe_norm.reshape(1, d), wr, br, tm=POSTATTN_TILE)

    te = EXPERT_TILE
    n_blk = (2 * t) // te + N_EXPERTS
    counts = counts[:N_EXPERTS, 0].astype(jnp.int32)
    blocks_per = (counts + te - 1) // te
    block_end = jnp.cumsum(blocks_per)
    row_start = (block_end - blocks_per) * te
    expert_ids = jnp.arange(N_EXPERTS, dtype=jnp.int32)[:, None, None]
    slot_major = meta[4:6] + jnp.sum(
        jnp.where(meta[2:4][None] == expert_ids, row_start[:, None, None], 0), axis=0)
    n_used = block_end[-1:].astype(jnp.int32)
    block_ids = jnp.arange(n_blk, dtype=jnp.int32)
    block_expert = jnp.minimum(
        jnp.sum((block_ids[:, None] >= block_end[None, :]).astype(jnp.int32), axis=1), N_EXPERTS - 1)

    first_block = block_end - blocks_per
    block_valid = jnp.clip(counts[block_expert] - (block_ids - first_block[block_expert]) * te, 0, te)
    xs = _sc_scatter(h2p, slot_major.reshape(2, t // SC_SCATTER_CHUNK, SC_SCATTER_CHUNK), n_blk * te)
    rows = _experts(block_expert, block_valid, n_used, xs, w_gate, w_up, w_down, te=te)
    out = None
    part = t // COMBINE_PARTS
    for first in range(0, t, part):
        slot_rows = _sc_gather(rows, slot_major[:, first:first + part].reshape(-1))
        out = _final(x1, route, p.reshape(t, -1), slot_rows, ple_norm.reshape(1, d),
                     w_ple_gate.astype(_BF16), w_ple_proj.astype(_BF16), final_gain.reshape(1, d),
                     out, tm=FINAL_TILE, first=first)
    return out.reshape(b, s, d)


def kernel(x, p, attn_norm, w_in, b_forget, w_fox_branch, w_moba_branch, w_out, moe_norm, w_group,
           b_group, w_fine, b_fine, w_gate, w_up, w_down, ple_norm, w_ple_gate, w_ple_proj, final_norm):
    depth = p.shape[0]
    assert depth == 1, "the final norm is fused into the last layer's kernel"
    i = 0
    return _layer(x, p[i], attn_norm[i], w_in[i], b_forget[i], w_fox_branch[i], w_moba_branch[i],
                  w_out[i], moe_norm[i], w_group[i], b_group[i], w_fine[i], b_fine[i], w_gate[i],
                  w_up[i], w_down[i], ple_norm[i], w_ple_gate[i], w_ple_proj[i], final_norm)
```

```python
import functools

import jax
import jax.numpy as jnp
from jax import lax
from jax.experimental import pallas as pl
from jax.experimental.pallas import tpu as pltpu
from jax.experimental.pallas import tpu_sc as plsc

HEAD_DIM = 64
N_FOX_HEADS = 8
N_MOBA_HEADS = 8
MOBA_BLOCK = 256
MOBA_TOPK = 3
ROPE_THETA = 10000.0
N_GROUPS = 4
EXPERTS_PER_GROUP = 8
N_EXPERTS = N_GROUPS * EXPERTS_PER_GROUP
RMS_EPS = 1e-6

LANES = 128
SUBLANES = 8
BLOCK_LOOP_UNROLL = 16
HEADS_PER_STEP = LANES // HEAD_DIM
ATTN_Q = 256
FOX_PIPELINE = (256, 8, 4)
MOBA_PIPELINE = (256, 8, 4)
ONES_ROWS = 16
SOFTMAX_ROWS = 64
BIAS_PIECES = 3
ROW_TILE = 512
POSTATTN_TILE = 1024
SC_CORES = 2
SC_SUBCORES = 16
SC_CHUNK = 64
SC_SCATTER_CHUNK = 128
COMBINE_PARTS = 4
FINAL_TILE = 1024
EXPERT_PARTS = 2
EXPERT_TILE = 512
LOG2_E = 1.4426950408889634
UNDERFLOW_LOG2 = 160.0
NEG_BIG = -1e30
VMEM_LIMIT = 48 * 1024 * 1024

_BF16 = jnp.bfloat16
_F32 = jnp.float32


def _params(*sem):
    return pltpu.CompilerParams(dimension_semantics=sem, vmem_limit_bytes=VMEM_LIMIT)


def _rms(x, g):
    return x * lax.rsqrt(jnp.mean(x * x, axis=-1, keepdims=True) + RMS_EPS) * g


def _const_spec(shape):
    return pl.BlockSpec(shape, lambda *_: (0,) * len(shape))


def _inproj_body(x_ref, g_ref, w_ref, wt_ref, wf_ref, bf_ref, cos_ref, sin_ref, cost_ref, sint_ref,
                 qft_ref, kf_ref, vft_ref, qmt_ref, km_ref, vmt_ref, gates_ref, cb_ref, kmean_ref,
                 kabs_ref, bmax_ref,
                 carry_ref, *, tm, width, gate_cols):
    j = pl.program_id(1)
    h = _rms(x_ref[...], g_ref[...])
    hb = h.astype(_BF16)
    hbt = h.T.astype(_BF16)
    reps = width // LANES
    half = HEAD_DIM // 2

    blocks = tm // MOBA_BLOCK

    @pl.when(j == 0)
    def _():
        kmean_ref[...] = jnp.zeros_like(kmean_ref)
        kabs_ref[...] = jnp.zeros_like(kabs_ref)
        bmax_ref[...] = jnp.zeros_like(bmax_ref)
        carry_ref[...] = jnp.zeros_like(carry_ref)

    z = jnp.dot(hb, wf_ref[...], preferred_element_type=_F32) + bf_ref[...]
    log_f = jnp.minimum(z, 0.0) - jnp.log1p(jnp.exp(-jnp.abs(z)))
    r = lax.broadcasted_iota(jnp.int32, (LANES, LANES), 0)
    c = lax.broadcasted_iota(jnp.int32, (LANES, LANES), 1)
    tri = (c <= r).astype(_BF16)
    offset = carry_ref[0:1, :]
    blocks_cum = []
    for blk in range(tm // LANES):
        rest = log_f[blk * LANES:(blk + 1) * LANES]
        within = jnp.zeros((LANES, LANES), _F32)
        for _ in range(BIAS_PIECES):
            part = rest.astype(_BF16)
            rest = rest - part.astype(_F32)
            within = within + jnp.dot(tri, part, preferred_element_type=_F32)
        blocks_cum.append(within + offset)
        offset = offset + within[LANES - 1:LANES, :]
    cum = jnp.concatenate(blocks_cum, axis=0)
    carry_ref[...] = jnp.broadcast_to(offset, carry_ref.shape)

    head_lane = lax.broadcasted_iota(jnp.int32, (1, LANES), 1) < N_FOX_HEADS
    bias = jnp.where(head_lane, -cum * LOG2_E, 0.0)
    for b in range(blocks):
        bmax_ref[pl.ds(j * blocks + b, 1), :] = jnp.max(
            bias[b * MOBA_BLOCK:(b + 1) * MOBA_BLOCK], axis=0, keepdims=True)
    rest = bias
    placed = jnp.zeros((tm, LANES), _F32)
    for piece in range(BIAS_PIECES):
        part = rest.astype(_BF16).astype(_F32)
        rest = rest - part
        placed = placed + (pltpu.roll(part, N_FOX_HEADS * piece, 1) if piece else part)
    cb_ref[...] = placed.astype(_BF16)

    def proj(c):
        return jnp.dot(hb, w_ref[:, c * width:(c + 1) * width], preferred_element_type=_F32)

    def proj_t(c):
        return jnp.dot(wt_ref[c * width:(c + 1) * width, :], hbt, preferred_element_type=_F32)

    def rope(t):
        cos = jnp.concatenate([cos_ref[...]] * reps, axis=1)
        sin = jnp.concatenate([sin_ref[...]] * reps, axis=1)
        first = (lax.broadcasted_iota(jnp.int32, (1, width), 1) % HEAD_DIM) < half
        partner = jnp.where(first, pltpu.roll(t, width - half, 1), pltpu.roll(t, half, 1))
        return t * cos + partner * sin

    def rope_t(t):
        cos = jnp.concatenate([cost_ref[...]] * reps, axis=0)
        sin = jnp.concatenate([sint_ref[...]] * reps, axis=0)
        first = (lax.broadcasted_iota(jnp.int32, (width, 1), 0) % HEAD_DIM) < half
        partner = jnp.where(first, pltpu.roll(t, width - half, 0), pltpu.roll(t, half, 0))
        return t * cos + partner * sin

    scale = HEAD_DIM ** -0.5 * LOG2_E
    kf = proj(0).astype(_BF16)
    kf_ref[...] = kf
    kf_abs = jnp.abs(kf.astype(_F32))
    km = rope(proj(1))
    km_ref[...] = km.astype(_BF16)

    for b in range(blocks):
        rows = slice(b * MOBA_BLOCK, (b + 1) * MOBA_BLOCK)
        kmean_ref[pl.ds(j * blocks + b, 1), :] = jnp.mean(km[rows], axis=0, keepdims=True)
        kabs_ref[pl.ds(j * blocks + b, 1), :] = jnp.max(kf_abs[rows], axis=0, keepdims=True)

    for c in range(gate_cols // width):
        g = jnp.dot(hb, w_ref[:, (2 + c) * width:(3 + c) * width],
                    preferred_element_type=_F32)
        gates_ref[:, c * width:(c + 1) * width] = jax.nn.sigmoid(g).astype(_BF16)

    qft_ref[...] = (proj_t(0) * scale).astype(_BF16)
    vft_ref[...] = proj_t(1).astype(_BF16)
    qmt_ref[...] = (rope_t(proj_t(2)) * scale).astype(_BF16)
    vmt_ref[...] = proj_t(3).astype(_BF16)


def _inproj(x, g, w, wt, wf, bfp, cos, sin, cos_t, sin_t, *, tm):
    b, s, d = x.shape
    width = N_FOX_HEADS * HEAD_DIM
    gate_cols = w.shape[1] - 2 * width
    act = jax.ShapeDtypeStruct((b, s, width), _BF16)
    act_t = jax.ShapeDtypeStruct((b, width, s), _BF16)
    act_spec = pl.BlockSpec((None, tm, width), lambda bi, j: (bi, j, 0))
    act_t_spec = pl.BlockSpec((None, width, tm), lambda bi, j: (bi, 0, j))
    return pl.pallas_call(
        functools.partial(_inproj_body, tm=tm, width=width, gate_cols=gate_cols),
        grid=(b, s // tm),
        in_specs=[
            pl.BlockSpec((None, tm, d), lambda bi, j: (bi, j, 0)),
            _const_spec((1, d)),
            _const_spec(w.shape),
            _const_spec(wt.shape),
            _const_spec(wf.shape),
            _const_spec(bfp.shape),
            pl.BlockSpec((tm, LANES), lambda bi, j: (j, 0)),
            pl.BlockSpec((tm, LANES), lambda bi, j: (j, 0)),
            pl.BlockSpec((LANES, tm), lambda bi, j: (0, j)),
            pl.BlockSpec((LANES, tm), lambda bi, j: (0, j)),
        ],
        out_specs=[act_t_spec, act_spec, act_t_spec, act_t_spec, act_spec, act_t_spec,
                   pl.BlockSpec((None, tm, gate_cols), lambda bi, j: (bi, j, 0)),
                   pl.BlockSpec((None, tm, LANES), lambda bi, j: (bi, j, 0)),
                   pl.BlockSpec((None, LANES, width), lambda bi, j: (bi, 0, 0)),
                   pl.BlockSpec((None, LANES, width), lambda bi, j: (bi, 0, 0)),
                   pl.BlockSpec((None, LANES, LANES), lambda bi, j: (bi, 0, 0))],
        out_shape=[act_t, act, act_t, act_t, act, act_t,
                   jax.ShapeDtypeStruct((b, s, gate_cols), _BF16),
                   jax.ShapeDtypeStruct((b, s, LANES), _BF16),
                   jax.ShapeDtypeStruct((b, LANES, width), _F32),
                   jax.ShapeDtypeStruct((b, LANES, width), _F32),
                   jax.ShapeDtypeStruct((b, LANES, LANES), _F32)],
        scratch_shapes=[pltpu.VMEM((SUBLANES, LANES), _F32)],
        compiler_params=_params("arbitrary", "arbitrary"),
        name="inproj",
    )(x, g, w, wt, wf, bfp, cos, sin, cos_t, sin_t)


def _attn_body(uq_ref, uk_ref, qt_ref, k_ref, kx_ref, vt_ref, aux_ref, aux2_ref, tri_ref, o_ref,
               w_ref, m_ref, l_ref, acc_ref, st_ref, p_ref, alpha_ref, pmax_ref, qn_ref, *,
               mode, tq, tk, nb, n_diag, n_units, pipe, lag):
    hp = pl.program_id(1)
    wide = HEADS_PER_STEP * tq
    nq = w_ref.shape[0]
    feat = lax.broadcasted_iota(jnp.int32, (LANES, 1), 0)
    xr = lax.broadcasted_iota(jnp.int32, (LANES, wide), 0)
    xc = lax.broadcasted_iota(jnp.int32, (LANES, wide), 1)

    def setup(i, carry):
        qt = qt_ref[:, pl.ds(pl.multiple_of(i * tq, tq), tq)]
        zero = jnp.zeros_like(qt)
        wq = jnp.concatenate([jnp.where(feat < HEAD_DIM, qt, zero),
                              jnp.where(feat < HEAD_DIM, zero, qt)], axis=1)
        if mode == "fox":
            head = hp * HEADS_PER_STEP + xc // tq
            lower = ((xr % N_FOX_HEADS == head) & (xr < BIAS_PIECES * N_FOX_HEADS)).astype(_BF16)
            qn_ref[i] = jnp.broadcast_to(
                jnp.sum(jnp.abs(wq.astype(_F32)), axis=0, keepdims=True), qn_ref.shape[1:])
        else:
            own = (i * tq) // MOBA_BLOCK
            gate = jnp.dot(aux_ref[...].astype(_BF16), wq, preferred_element_type=_F32)[0:nb]
            blk = lax.broadcasted_iota(jnp.int32, (nb, wide), 0)
            gate = jnp.where(blk < own, gate, -jnp.inf)
            keep = blk == own
            for _ in range(MOBA_TOPK):
                mx = jnp.max(gate, axis=0, keepdims=True)
                cand = jnp.where((gate == mx) & (mx > -jnp.inf), blk, nb)
                pick = blk == jnp.min(cand, axis=0, keepdims=True)
                keep = keep | pick
                gate = jnp.where(pick, -jnp.inf, gate)
            lower = jnp.where(keep, 0.0, NEG_BIG)
            if nb < LANES:
                lower = jnp.concatenate([lower, jnp.zeros((LANES - nb, wide), _F32)], axis=0)
            lower = lower.astype(_BF16)
        w_ref[i] = jnp.concatenate([wq, lower], axis=0)
        m_ref[i] = jnp.full(m_ref.shape[1:], NEG_BIG, _F32)
        l_ref[i] = jnp.zeros(l_ref.shape[1:], _F32)
        acc_ref[i] = jnp.zeros(acc_ref.shape[1:], _F32)
        return carry

    lax.fori_loop(0, nq, setup, 0, unroll=BLOCK_LOOP_UNROLL)

    def unit(t):
        return uq_ref[t], uk_ref[t]

    def scores(t, slot, causal):
        q, kb = unit(t)
        start = pl.multiple_of(kb * tk, tk)
        ka = jnp.concatenate([k_ref[pl.ds(start, tk), :], kx_ref[pl.ds(start, tk), :]], axis=1)
        st = jnp.dot(ka, w_ref[q], preferred_element_type=_F32)
        if causal:
            st = st + tri_ref[...]
        st_ref[slot] = st
        pmax_ref[slot] = jnp.max(st.reshape(tk // SUBLANES, SUBLANES, wide), axis=0)

    def softmax(t, slot):
        q, _ = unit(t)
        rows = SOFTMAX_ROWS
        m_prev = m_ref[q][0:1, :]
        m_new = jnp.maximum(m_prev, jnp.max(pmax_ref[slot], axis=0, keepdims=True))
        for r in range(0, tk, rows):
            p_ref[slot, r:r + rows, :] = jnp.exp2(st_ref[slot, r:r + rows, :] - m_new).astype(_BF16)
        m_ref[q] = jnp.broadcast_to(m_new, m_ref.shape[1:])
        alpha_ref[slot] = jnp.broadcast_to(jnp.exp2(m_prev - m_new), alpha_ref.shape[1:])

    def values(t, slot):
        q, kb = unit(t)
        start = pl.multiple_of(kb * tk, tk)
        ones = jnp.ones((ONES_ROWS, tk), _BF16)
        for h in range(HEADS_PER_STEP):
            cols = slice(h * tq, (h + 1) * tq)
            feats = slice(h * HEAD_DIM, (h + 1) * HEAD_DIM)
            lhs = jnp.concatenate([vt_ref[feats, pl.ds(start, tk)], ones], axis=0)
            pv = jnp.dot(lhs, p_ref[slot, :, cols], preferred_element_type=_F32)
            alpha = alpha_ref[slot, 0:1, cols]
            acc_ref[q, feats, :] = acc_ref[q, feats, :] * alpha + pv[0:HEAD_DIM]
            l_ref[q, :, cols] = jnp.broadcast_to(
                alpha * l_ref[q, 0:1, cols] + pv[HEAD_DIM:HEAD_DIM + 1], (SUBLANES, tq))

    def step(base, c, causal, do_scores=True, do_softmax=True, do_values=True):
        u = base + c
        if do_scores:
            scores(u, c % pipe, causal)
        if do_values:
            values(u - 2 * lag, (c - 2 * lag) % pipe)
        if do_softmax:
            softmax(u - lag, (c - lag) % pipe)

    def group(causal):
        def body(j, carry):
            for c in range(pipe):
                step(j * pipe, c, causal)
            return carry
        return body

    def pipeline(g_first, g_end, causal_groups):
        for c in range(pipe):
            step(g_first * pipe, c, g_first < causal_groups,
                 do_softmax=c >= lag, do_values=c >= 2 * lag)
        if g_first + 1 < causal_groups:
            lax.fori_loop(g_first + 1, causal_groups, group(True), 0)
        lax.fori_loop(max(g_first + 1, causal_groups), g_end, group(False), 0)
        for c in range(2 * lag):
            step(g_end * pipe, c, False, do_scores=False, do_softmax=c < lag)

    diag_groups = n_diag // pipe
    if mode == "moba":
        pipeline(0, n_units // pipe, diag_groups)
    else:
        pipeline(0, diag_groups, diag_groups)
        nq_blocks = n_diag
        col = lax.broadcasted_iota(jnp.int32, (LANES, LANES), 1)
        first_head = lax.broadcasted_iota(jnp.int32, (1, wide), 1) < tq
        head0 = hp * HEADS_PER_STEP

        def per_head(table, pick0, pick1, fill):
            v0 = jnp.max(jnp.where(pick0, table, fill), axis=1, keepdims=True)[0:nb]
            v1 = jnp.max(jnp.where(pick1, table, fill), axis=1, keepdims=True)[0:nb]
            return jnp.where(first_head, v0, v1)

        bias_max = per_head(aux2_ref[...], col == head0, col == head0 + 1, NEG_BIG)
        k_absmax = per_head(aux_ref[...], col < HEAD_DIM, col >= HEAD_DIM, 0.0)
        blk = lax.broadcasted_iota(jnp.int32, (nb, 1), 0)

        needed = jnp.zeros((nb, 1), jnp.int32)
        for i in range(1, nq_blocks):
            bound = bias_max + qn_ref[i, 0:1, :] * k_absmax - m_ref[i, 0:1, :]
            live = jnp.max(bound, axis=1, keepdims=True) >= -UNDERFLOW_LOG2
            needed = jnp.maximum(needed, jnp.where(live & (blk < i), i - blk, 0))
        far = jnp.max(needed)
        n_live = far * nq_blocks - (far * (far + 1)) // 2
        live_groups = (n_live + pipe - 1) // pipe

        @pl.when(live_groups > 0)
        def _():
            pipeline(diag_groups, diag_groups + live_groups, diag_groups)

    def finish(i, carry):
        out_t = jnp.concatenate(
            [acc_ref[i, h * HEAD_DIM:(h + 1) * HEAD_DIM, :] / l_ref[i, 0:1, h * tq:(h + 1) * tq]
             for h in range(HEADS_PER_STEP)], axis=0)
        o_ref[pl.ds(pl.multiple_of(i * tq, tq), tq), :] = out_t.T.astype(o_ref.dtype)
        return carry

    lax.fori_loop(0, nq, finish, 0, unroll=BLOCK_LOOP_UNROLL)


def _attention(qt, k, kx, vt, aux, aux2, *, mode):
    b, width, s = qt.shape
    tq = ATTN_Q
    tk, pipe, lag = FOX_PIPELINE if mode == "fox" else MOBA_PIPELINE
    nq = s // tq
    nb = s // MOBA_BLOCK
    wide = HEADS_PER_STEP * tq
    diag = [(i * tq) // tk for i in range(nq)]
    units = list(enumerate(diag))
    units += [(i, diag[i] - dist) for dist in range(1, nq) for i in range(nq) if diag[i] >= dist]
    assert mode != "fox" or tq == tk
    assert nq % pipe == 0 and len(units) % pipe == 0 and pipe >= 2 * lag
    key_row = jnp.arange(tk, dtype=jnp.int32)[:, None]
    qry_col = jnp.arange(wide, dtype=jnp.int32)[None, :] % tq
    causal_bias = jnp.where(key_row <= qry_col, 0.0, NEG_BIG).astype(_F32)
    unit_q = jnp.asarray([u[0] for u in units], jnp.int32)
    unit_k = jnp.asarray([u[1] for u in units], jnp.int32)
    aux_spec = pl.BlockSpec((None, LANES, LANES), lambda bi, hp, uq, uk: (bi, 0, hp))
    aux2_spec = pl.BlockSpec((None, LANES, LANES), lambda bi, hp, uq, uk: (bi, 0, 0))
    if mode == "fox":
        kx_spec = pl.BlockSpec((None, s, LANES), lambda bi, hp, uq, uk: (bi, 0, 0))
    else:
        kx_spec = pl.BlockSpec(kx.shape, lambda bi, hp, uq, uk: (0, 0))
    return pl.pallas_call(
        functools.partial(_attn_body, mode=mode, tq=tq, tk=tk, nb=nb, n_diag=nq, n_units=len(units),
                          pipe=pipe, lag=lag),
        grid_spec=pltpu.PrefetchScalarGridSpec(
            num_scalar_prefetch=2,
            grid=(b, width // LANES),
            in_specs=[
                pl.BlockSpec((None, LANES, s), lambda bi, hp, uq, uk: (bi, hp, 0)),
                pl.BlockSpec((None, s, LANES), lambda bi, hp, uq, uk: (bi, 0, hp)),
                kx_spec,
                pl.BlockSpec((None, LANES, s), lambda bi, hp, uq, uk: (bi, hp, 0)),
                aux_spec,
                aux2_spec,
                pl.BlockSpec((tk, wide), lambda bi, hp, uq, uk: (0, 0)),
            ],
            out_specs=pl.BlockSpec((None, s, LANES), lambda bi, hp, uq, uk: (bi, 0, hp)),
            scratch_shapes=[
                pltpu.VMEM((nq, 2 * LANES, wide), _BF16),
                pltpu.VMEM((nq, SUBLANES, wide), _F32),
                pltpu.VMEM((nq, SUBLANES, wide), _F32),
                pltpu.VMEM((nq, LANES, tq), _F32),
                pltpu.VMEM((pipe, tk, wide), _F32),
                pltpu.VMEM((pipe, tk, wide), _BF16),
                pltpu.VMEM((pipe, SUBLANES, wide), _F32),
                pltpu.VMEM((pipe, SUBLANES, wide), _F32),
                pltpu.VMEM((nq, SUBLANES, wide), _F32),
            ],
        ),
        out_shape=jax.ShapeDtypeStruct((b, s, width), _BF16),
        compiler_params=_params("arbitrary", "arbitrary"),
        name=mode,
    )(unit_q, unit_k, qt, k, kx, vt, aux, aux2, causal_bias)


def _pack_bf16_pair(a, b):
    lo = pltpu.bitcast(a.astype(_BF16).astype(_F32), jnp.uint32) >> 16
    hi = pltpu.bitcast(b.astype(_BF16).astype(_F32), jnp.uint32) & jnp.uint32(0xFFFF0000)
    return lo | hi


def _unpack_bf16_pair(u):
    lo = pltpu.bitcast(u << 16, _F32)
    hi = pltpu.bitcast(u & jnp.uint32(0xFFFF0000), _F32)
    return jnp.concatenate([lo, hi], axis=1)


def _postattn_body(yf_ref, ym_ref, gates_ref, x_ref, wfb_ref, wmb_ref, wout_ref, g_ref, wr_ref, br_ref,
                   x1_ref, h2_ref, route_ref, meta_ref, counts_ref, carry_ref, *, tm, d):
    step = pl.program_id(0)
    ya = jnp.dot(yf_ref[...], wfb_ref[...], preferred_element_type=_F32)
    yb = jnp.dot(ym_ref[...], wmb_ref[...], preferred_element_type=_F32)
    mixed = gates_ref[:, 0:d].astype(_F32) * ya + gates_ref[:, d:2 * d].astype(_F32) * yb
    x1 = x_ref[...] + jnp.dot(mixed.astype(_BF16), wout_ref[...], preferred_element_type=_F32)
    x1_ref[...] = x1
    h2 = _rms(x1, g_ref[...])
    h2_ref[...] = _pack_bf16_pair(h2[:, 0:d // 2], h2[:, d // 2:d])

    logits = jnp.dot(h2.astype(_BF16), wr_ref[...], preferred_element_type=_F32) + br_ref[...]
    logits_t = logits.T
    gl = logits_t[N_EXPERTS:N_EXPERTS + N_GROUPS]
    ge = jnp.exp(gl - jnp.max(gl, axis=0, keepdims=True))
    gp = ge / jnp.sum(ge, axis=0, keepdims=True)
    g_top = jnp.max(gp, axis=0, keepdims=True)
    g_row = lax.broadcasted_iota(jnp.int32, (N_GROUPS, tm), 0)
    g_idx = jnp.min(jnp.where(gp == g_top, g_row, N_GROUPS), axis=0, keepdims=True)
    fl = logits_t[0:EXPERTS_PER_GROUP]
    for g in range(1, N_GROUPS):
        fl = jnp.where(g_idx == g, logits_t[g * EXPERTS_PER_GROUP:(g + 1) * EXPERTS_PER_GROUP], fl)
    f_row = lax.broadcasted_iota(jnp.int32, (EXPERTS_PER_GROUP, tm), 0)
    f1 = jnp.max(fl, axis=0, keepdims=True)
    i1 = jnp.min(jnp.where(fl == f1, f_row, EXPERTS_PER_GROUP), axis=0, keepdims=True)
    fl2 = jnp.where(f_row == i1, -jnp.inf, fl)
    f2 = jnp.max(fl2, axis=0, keepdims=True)
    i2 = jnp.min(jnp.where(fl2 == f2, f_row, EXPERTS_PER_GROUP), axis=0, keepdims=True)
    e1 = g_idx * EXPERTS_PER_GROUP + i1
    e2 = g_idx * EXPERTS_PER_GROUP + i2
    t2 = jnp.exp(f2 - f1)
    w1 = g_top * (1.0 / (1.0 + t2))
    w2 = g_top * (t2 / (1.0 + t2))

    @pl.when(step == 0)
    def _():
        carry_ref[...] = jnp.zeros_like(carry_ref)

    x_row = lax.broadcasted_iota(jnp.int32, (N_EXPERTS, tm), 0)
    onehot = ((x_row == e1) | (x_row == e2)).astype(_F32)
    rr = lax.broadcasted_iota(jnp.int32, (tm, tm), 0)
    cc = lax.broadcasted_iota(jnp.int32, (tm, tm), 1)
    before = jnp.dot(onehot.astype(_BF16), (rr < cc).astype(_BF16), preferred_element_type=_F32)
    before = before + carry_ref[0:N_EXPERTS, 0:1]
    r1 = jnp.sum(jnp.where(x_row == e1, before, 0.0), axis=0, keepdims=True)
    r2 = jnp.sum(jnp.where(x_row == e2, before, 0.0), axis=0, keepdims=True)
    total = carry_ref[0:N_EXPERTS, 0:1] + jnp.sum(onehot, axis=1, keepdims=True)
    carry_ref[0:N_EXPERTS, :] = jnp.broadcast_to(total, (N_EXPERTS, LANES))
    counts_ref[...] = carry_ref[...]

    m_row = lax.broadcasted_iota(jnp.int32, (SUBLANES, tm), 0)
    meta = jnp.where(m_row == 2, e1, 0)
    meta = jnp.where(m_row == 3, e2, meta)
    meta = jnp.where(m_row == 4, r1.astype(jnp.int32), meta)
    meta = jnp.where(m_row == 5, r2.astype(jnp.int32), meta)
    meta_ref[...] = meta
    w_row = lax.broadcasted_iota(jnp.int32, (LANES, tm), 0)
    route_ref[...] = jnp.where(w_row == 0, w1, jnp.where(w_row == 1, w2, 0.0)).T


def _postattn(yf, ym, gates, x, wfb, wmb, wout, g, wr, br, *, tm):
    t, d = x.shape
    width = yf.shape[1]
    row = lambda cols: pl.BlockSpec((tm, cols), lambda i: (i, 0))
    return pl.pallas_call(
        functools.partial(_postattn_body, tm=tm, d=d),
        grid=(t // tm,),
        in_specs=[row(width), row(width), row(2 * d), row(d),
                  _const_spec(wfb.shape), _const_spec(wmb.shape), _const_spec(wout.shape),
                  _const_spec((1, d)), _const_spec(wr.shape), _const_spec(br.shape)],
        out_specs=[row(d), row(d // 2), row(LANES), pl.BlockSpec((SUBLANES, tm), lambda i: (0, i)),
                   _const_spec((LANES, LANES))],
        out_shape=[jax.ShapeDtypeStruct((t, d), _F32),
                   jax.ShapeDtypeStruct((t, d // 2), jnp.uint32),
                   jax.ShapeDtypeStruct((t, LANES), _F32),
                   jax.ShapeDtypeStruct((SUBLANES, t), jnp.int32),
                   jax.ShapeDtypeStruct((LANES, LANES), _F32)],
        scratch_shapes=[pltpu.VMEM((LANES, LANES), _F32)],
        compiler_params=_params("arbitrary"),
        name="postattn",
    )(yf, ym, gates, x, wfb, wmb, wout, g, wr, br)


def _sc_scatter(rows, idx, n_out):
    n, width = rows.shape
    workers = SC_CORES * SC_SUBCORES
    per_worker = n // workers
    chunks = per_worker // SC_SCATTER_CHUNK
    assert n % (workers * SC_SCATTER_CHUNK) == 0 and idx.shape == (2, n // SC_SCATTER_CHUNK, SC_SCATTER_CHUNK)
    mesh = plsc.VectorSubcoreMesh(core_axis_name="c", subcore_axis_name="s",
                                  num_cores=SC_CORES, num_subcores=SC_SUBCORES)

    @functools.partial(
        pl.kernel, out_type=jax.ShapeDtypeStruct((n_out, width), rows.dtype), mesh=mesh,
        scratch_types=[pltpu.VMEM((2, chunks, SC_SCATTER_CHUNK), jnp.int32),
                       pltpu.VMEM((SC_SCATTER_CHUNK, width), rows.dtype),
                       pltpu.SemaphoreType.DMA],
        name="sc_scatter")
    def scatter(rows_hbm, idx_hbm, out_hbm, idx_v, rows_v, sem):
        worker = lax.axis_index("s") * SC_CORES + lax.axis_index("c")
        for k in range(2):
            pltpu.sync_copy(idx_hbm.at[k, pl.ds(worker * chunks, chunks)], idx_v.at[k])

        def chunk(j, carry):
            start = worker * per_worker + j * SC_SCATTER_CHUNK
            pltpu.sync_copy(rows_hbm.at[pl.ds(start, SC_SCATTER_CHUNK)], rows_v)
            for k in range(2):
                pltpu.async_copy(rows_v, out_hbm.at[idx_v.at[k, j]], sem).wait()
            return carry

        lax.fori_loop(0, chunks, chunk, 0)

    return scatter(rows, idx)


def _experts_body(be_ref, valid_ref, nused_ref, xs_ref, wg_ref, wu_ref, wd_ref, out_ref,
                  wgb_ref, wub_ref, wdb_ref):
    j = pl.program_id(0)
    used = j < nused_ref[0]

    @pl.when(jnp.logical_not(used))
    def _():
        out_ref[...] = jnp.zeros_like(out_ref)

    @pl.when(used & ((j == 0) | (be_ref[j] != be_ref[jnp.maximum(j - 1, 0)])))
    def _():
        wgb_ref[...] = wg_ref[...].astype(_BF16)
        wub_ref[...] = wu_ref[...].astype(_BF16)
        wdb_ref[...] = wd_ref[...].astype(_BF16)

    @pl.when(used)
    def _():
        row = lax.broadcasted_iota(jnp.int32, xs_ref.shape, 0)
        packed = jnp.where(row < valid_ref[j], xs_ref[...], jnp.uint32(0))
        rows = _unpack_bf16_pair(packed).astype(_BF16)
        hidden = wgb_ref.shape[1]
        out = None
        for c in range(0, hidden, hidden // EXPERT_PARTS):
            cols = slice(c, c + hidden // EXPERT_PARTS)
            gate = jnp.dot(rows, wgb_ref[:, cols], preferred_element_type=_F32)
            up = jnp.dot(rows, wub_ref[:, cols], preferred_element_type=_F32)
            act = ((gate * jax.nn.sigmoid(gate)) * up).astype(_BF16)
            part = jnp.dot(act, wdb_ref[cols, :], preferred_element_type=_F32)
            out = part if out is None else out + part
        half = out.shape[1] // 2
        out_ref[...] = _pack_bf16_pair(out[:, 0:half], out[:, half:])


def _experts(block_expert, block_valid, n_used, xs, wg, wu, wd, *, te):
    n_rows, half = xs.shape
    _, d, de = wg.shape

    def blk(j, be, bv, nu):
        return jnp.minimum(j, nu[0] - 1)

    def weight(j, be, bv, nu):
        return (be[blk(j, be, bv, nu)], 0, 0)

    return pl.pallas_call(
        _experts_body,
        grid_spec=pltpu.PrefetchScalarGridSpec(
            num_scalar_prefetch=3,
            grid=(n_rows // te,),
            in_specs=[pl.BlockSpec((te, half), lambda j, be, bv, nu: (blk(j, be, bv, nu), 0)),
                      pl.BlockSpec((None, d, de), weight),
                      pl.BlockSpec((None, d, de), weight),
                      pl.BlockSpec((None, de, d), weight)],
            out_specs=pl.BlockSpec((te, d // 2), lambda j, be, bv, nu: (j, 0)),
            scratch_shapes=[pltpu.VMEM((d, de), _BF16), pltpu.VMEM((d, de), _BF16),
                            pltpu.VMEM((de, d), _BF16)],
        ),
        out_shape=jax.ShapeDtypeStruct((n_rows, d // 2), jnp.uint32),
        compiler_params=_params("arbitrary"),
        name="experts",
    )(block_expert, block_valid, n_used, xs, wg, wu, wd)


def _sc_gather(table, idx):
    n, = idx.shape
    width = table.shape[1]
    workers = SC_CORES * SC_SUBCORES
    per_worker = n // workers
    chunks = per_worker // SC_CHUNK
    assert n % (workers * SC_CHUNK * 2) == 0
    mesh = plsc.VectorSubcoreMesh(core_axis_name="c", subcore_axis_name="s",
                                  num_cores=SC_CORES, num_subcores=SC_SUBCORES)

    @functools.partial(
        pl.kernel, out_type=jax.ShapeDtypeStruct((n, width), table.dtype), mesh=mesh,
        scratch_types=[pltpu.VMEM((per_worker,), jnp.int32),
                       pltpu.VMEM((2, SC_CHUNK, width), table.dtype),
                       pltpu.SemaphoreType.DMA((2,))],
        name="sc_gather")
    def gather(table_hbm, idx_hbm, out_hbm, idx_v, rows_v, sems):
        worker = lax.axis_index("s") * SC_CORES + lax.axis_index("c")
        base = worker * per_worker
        pltpu.sync_copy(idx_hbm.at[pl.ds(base, per_worker)], idx_v)

        def fetch(j, slot):
            off = pl.multiple_of(j * SC_CHUNK, SC_CHUNK)
            return pltpu.async_copy(table_hbm.at[idx_v.at[pl.ds(off, SC_CHUNK)]], rows_v.at[slot],
                                    sems.at[slot])

        def drain(j, slot):
            off = pl.multiple_of(j * SC_CHUNK, SC_CHUNK)
            pltpu.make_async_copy(out_hbm.at[pl.ds(base + off, SC_CHUNK)], rows_v.at[slot],
                                  sems.at[slot]).wait()
            pltpu.sync_copy(rows_v.at[slot], out_hbm.at[pl.ds(base + off, SC_CHUNK)])

        fetch(0, 0)

        def pair(p, carry):
            fetch(2 * p + 1, 1)
            drain(2 * p, 0)

            @pl.when(p + 1 < chunks // 2)
            def _():
                fetch(2 * p + 2, 0)

            drain(2 * p + 1, 1)
            return carry

        lax.fori_loop(0, chunks // 2, pair, 0)

    return gather(table, idx)


def _final_body(x1_ref, route_ref, p_ref, r1_ref, r2_ref, gp_ref, wpg_ref, wpp_ref, gf_ref, *rest):
    o_ref = rest[-1]
    emb = jnp.dot(p_ref[...].astype(_BF16), wpp_ref[...], preferred_element_type=_F32)
    route = route_ref[...]
    x2 = x1_ref[...] + (_unpack_bf16_pair(r1_ref[...]) * route[:, 0:1]
                        + _unpack_bf16_pair(r2_ref[...]) * route[:, 1:2])
    gate = jax.nn.sigmoid(jnp.dot(_rms(x2, gp_ref[...]).astype(_BF16), wpg_ref[...],
                                  preferred_element_type=_F32))
    o_ref[...] = _rms(x2 + gate * emb, gf_ref[...])


def _final(x1, route, p, slot_rows, gp, wpg, wpp, gf, out_prev, *, tm, first):
    t, d = x1.shape
    ple = p.shape[1]
    steps = slot_rows.shape[0] // (2 * tm)
    start = first // tm
    row = lambda cols: pl.BlockSpec((tm, cols), lambda i: (start + i, 0))
    prev = [] if out_prev is None else [out_prev]
    return pl.pallas_call(
        _final_body,
        grid=(steps,),
        in_specs=[row(d), row(LANES), row(ple),
                  pl.BlockSpec((tm, d // 2), lambda i: (i, 0)),
                  pl.BlockSpec((tm, d // 2), lambda i: (steps + i, 0)),
                  _const_spec((1, d)), _const_spec(wpg.shape), _const_spec(wpp.shape),
                  _const_spec((1, d))] + [pl.BlockSpec(memory_space=pl.ANY)] * len(prev),
        out_specs=row(d),
        out_shape=jax.ShapeDtypeStruct((t, d), _F32),
        input_output_aliases={9: 0} if prev else {},
        compiler_params=_params("arbitrary"),
        name="final",
    )(x1, route, p, slot_rows, slot_rows, gp, wpg, wpp, gf, *prev)


def _rope_tables(s):
    half = HEAD_DIM // 2
    inv = 1.0 / (ROPE_THETA ** (jnp.arange(0, HEAD_DIM, 2, dtype=_F32) / HEAD_DIM))
    ang = jnp.arange(s, dtype=_F32)[:, None] * inv[None, :]
    cos, sin = jnp.cos(ang), jnp.sin(ang)
    reps = LANES // half
    cos_t = jnp.tile(cos, (1, reps))
    sin_t = jnp.tile(jnp.concatenate([-sin, sin], axis=1), (1, reps // 2))
    return cos_t, sin_t


def _layer(x, p, attn_norm, w_in, b_forget, w_fox_branch, w_moba_branch, w_out, moe_norm,
           w_group, b_group, w_fine, b_fine, w_gate, w_up, w_down, ple_norm, w_ple_gate, w_ple_proj,
           final_gain):
    b, s, d = x.shape
    t = b * s
    width = N_FOX_HEADS * HEAD_DIM
    assert N_MOBA_HEADS * HEAD_DIM == width and MOBA_BLOCK % ATTN_Q == 0 and MOBA_PIPELINE[0] % MOBA_BLOCK == 0
    assert BIAS_PIECES * N_FOX_HEADS <= LANES and s % MOBA_PIPELINE[0] == 0
    assert s % ROW_TILE == 0 and ROW_TILE % MOBA_BLOCK == 0 and s // MOBA_BLOCK <= LANES
    assert t % (COMBINE_PARTS * FINAL_TILE) == 0 and t % POSTATTN_TILE == 0 and N_EXPERTS + N_GROUPS <= LANES

    qkv_cols = 6 * width
    f_cols = N_FOX_HEADS
    chunk = lambda c: w_in[:, c * width:(c + 1) * width]
    w_rows = jnp.concatenate([chunk(1), chunk(4), w_in[:, qkv_cols + f_cols:]], axis=1).astype(_BF16)
    w_cols = jnp.concatenate([chunk(0), chunk(2), chunk(3), chunk(5)], axis=1).T.astype(_BF16)
    wf = jnp.zeros((d, LANES), _BF16).at[:, :f_cols].set(w_in[:, qkv_cols:qkv_cols + f_cols].astype(_BF16))
    bfp = jnp.zeros((1, LANES), _F32).at[0, :f_cols].set(b_forget.astype(_F32))
    cos, sin = _rope_tables(s)

    qft, kf, vft, qmt, km, vmt, gates, cb, kmean, kabs, bmax = _inproj(
        x, attn_norm.reshape(1, d), w_rows, w_cols, wf, bfp, cos, sin, cos.T, sin.T, tm=ROW_TILE)
    y_fox = _attention(qft, kf, cb, vft, kabs, bmax, mode="fox")
    block_of_key = jnp.arange(s, dtype=jnp.int32)[:, None] // MOBA_BLOCK
    block_onehot = (block_of_key == jnp.arange(LANES, dtype=jnp.int32)[None, :]).astype(_BF16)
    y_moba = _attention(qmt, km, block_onehot, vmt, kmean, bmax, mode="moba")

    wr = jnp.zeros((d, LANES), _BF16)
    wr = wr.at[:, :N_EXPERTS].set(w_fine.astype(_BF16))
    wr = wr.at[:, N_EXPERTS:N_EXPERTS + N_GROUPS].set(w_group.astype(_BF16))
    br = jnp.zeros((1, LANES), _F32)
    br = br.at[0, :N_EXPERTS].set(b_fine.astype(_F32))
    br = br.at[0, N_EXPERTS:N_EXPERTS + N_GROUPS].set(b_group.astype(_F32))
    x1, h2p, route, meta, counts = _postattn(
        y_fox.reshape(t, width), y_moba.reshape(t, width), gates.reshape(t, 2 * d), x.reshape(t, d),
        w_fox_branch.astype(_BF16), w_moba_branch.astype(_BF16), w_out.astype(_BF16),
        moe_norm.reshape(1, d), wr, br, tm=POSTATTN_TILE)

    te = EXPERT_TILE
    n_blk = (2 * t) // te + N_EXPERTS
    counts = counts[:N_EXPERTS, 0].astype(jnp.int32)
    blocks_per = (counts + te - 1) // te
    block_end = jnp.cumsum(blocks_per)
    row_start = (block_end - blocks_per) * te
    expert_ids = jnp.arange(N_EXPERTS, dtype=jnp.int32)[:, None, None]
    slot_major = meta[4:6] + jnp.sum(
        jnp.where(meta[2:4][None] == expert_ids, row_start[:, None, None], 0), axis=0)
    n_used = block_end[-1:].astype(jnp.int32)
    block_ids = jnp.arange(n_blk, dtype=jnp.int32)
    block_expert = jnp.minimum(
        jnp.sum((block_ids[:, None] >= block_end[None, :]).astype(jnp.int32), axis=1), N_EXPERTS - 1)

    first_block = block_end - blocks_per
    block_valid = jnp.clip(counts[block_expert] - (block_ids - first_block[block_expert]) * te, 0, te)
    xs = _sc_scatter(h2p, slot_major.reshape(2, t // SC_SCATTER_CHUNK, SC_SCATTER_CHUNK), n_blk * te)
    rows = _experts(block_expert, block_valid, n_used, xs, w_gate, w_up, w_down, te=te)
    out = None
    part = t // COMBINE_PARTS
    for first in range(0, t, part):
        slot_rows = _sc_gather(rows, slot_major[:, first:first + part].reshape(-1))
        out = _final(x1, route, p.reshape(t, -1), slot_rows, ple_norm.reshape(1, d),
                     w_ple_gate.astype(_BF16), w_ple_proj.astype(_BF16), final_gain.reshape(1, d),
                     out, tm=FINAL_TILE, first=first)
    return out.reshape(b, s, d)


def kernel(x, p, attn_norm, w_in, b_forget, w_fox_branch, w_moba_branch, w_out, moe_norm, w_group,
           b_group, w_fine, b_fine, w_gate, w_up, w_down, ple_norm, w_ple_gate, w_ple_proj, final_norm):
    depth = p.shape[0]
    assert depth == 1, "the final norm is fused into the last layer's kernel"
    i = 0
    return _layer(x, p[i], attn_norm[i], w_in[i], b_forget[i], w_fox_branch[i], w_moba_branch[i],
                  w_out[i], moe_norm[i], w_group[i], b_group[i], w_fine[i], b_fine[i], w_gate[i],
                  w_up[i], w_down[i], ple_norm[i], w_ple_gate[i], w_ple_proj[i], final_norm)
```
